```python
import jax, jax.numpy as jnp
from jax import lax
import numpy as np

D_MODEL = 2048
BATCH = 16
SEQ = 2048
DEPTH = 4

N_MIXERS = 2
N_A_LAYERS = (DEPTH + 1) // 2
N_B_LAYERS = DEPTH // 2

HG_EXPAND = 128
HG_HEADS = D_MODEL // HG_EXPAND
HG_FDIM = HG_HEADS * HG_EXPAND
HG_VDIM = D_MODEL
HG_HEAD_V = HG_VDIM // HG_HEADS
HG_CHUNK = 64
HG_SUB = 16
LB_FLOOR = 1e-30

GM_WIDTH = D_MODEL
GM_GROUPS = 16
GM_GROUP_DIM = GM_WIDTH // GM_GROUPS
GM_CHUNK = 128

FFN_HIDDEN = -(-8 * D_MODEL // (3 * 256)) * 256

PLE_DIM = 256

EPS = 1e-6

kernel_name = "hgrn2_gmlp_interleaved_sandwich_ple"


def _rmsnorm(x, g):
    xf = x.astype(jnp.float32)
    y = xf * lax.rsqrt(jnp.mean(xf * xf, axis=-1, keepdims=True) + EPS)
    return (y * g.astype(jnp.float32)).astype(x.dtype)


def _layernorm(x, g, b):
    xf = x.astype(jnp.float32)
    mu = jnp.mean(xf, axis=-1, keepdims=True)
    xc = xf - mu
    y = xc * lax.rsqrt(jnp.mean(xc * xc, axis=-1, keepdims=True) + EPS)
    return (y * g.astype(jnp.float32) + b.astype(jnp.float32)).astype(x.dtype)


def _hgrn_lower_bounds(lb_logits):
    sm = jax.nn.softmax(lb_logits.astype(jnp.float32), axis=0)
    return jnp.cumsum(sm, axis=0) - sm[:1]


def _hgrn2_chunkwise(q, k, v, log_f):
    B, S, H, K = q.shape
    V = v.shape[-1]
    C, L = HG_CHUNK, HG_SUB
    n = C // L
    nc = S // C

    def to_chunks(t):
        return t.reshape(B, nc, C, H, t.shape[-1]).transpose(1, 0, 3, 2, 4)

    causal = jnp.tril(jnp.ones((L, L), dtype=bool))[:, :, None]
    sub_mask = jnp.tril(jnp.ones((n, n), dtype=bool), k=-1)[:, :, None, None]
    eye_n = jnp.eye(n, dtype=jnp.float32)[:, None, :, None]

    def step(state, inp):
        qc, kc, vc, gc = inp
        b = jnp.cumsum(gc, axis=2)
        o_inter = jnp.einsum('bhtk,bhkv->bhtv', qc * jnp.exp(b), state)
        qs = qc.reshape(B, H, n, L, K)
        ks = kc.reshape(B, H, n, L, K)
        bs = b.reshape(B, H, n, L, K)
        gs = gc.reshape(B, H, n, L, K)
        bstart = bs[:, :, :, :1, :] - gs[:, :, :, :1, :]
        qf = qs * jnp.exp(bs - bstart)
        expo = bstart[:, :, :, None] - bs[:, :, None]
        kf = jnp.where(sub_mask, ks[:, :, None] * jnp.exp(jnp.where(sub_mask, expo, 0.0)), 0.0)
        a_off = jnp.einsum('bhjtk,bhjisk->bhjtis', qf, kf)
        diff = bs[:, :, :, :, None, :] - bs[:, :, :, None, :, :]
        wdec = jnp.where(causal, jnp.exp(jnp.where(causal, diff, 0.0)), 0.0)
        a_diag = jnp.einsum('bhjtk,bhjsk,bhjtsk->bhjts', qs, ks, wdec)
        a = (a_off + a_diag[:, :, :, :, None, :] * eye_n).reshape(B, H, C, C)
        o_intra = jnp.einsum('bhts,bhsv->bhtv', a, vc)
        b_last = b[:, :, -1:, :]
        new_state = (jnp.exp(b_last[:, :, 0, :])[..., None] * state
                     + jnp.einsum('bhsk,bhsv->bhkv', kc * jnp.exp(b_last - b), vc))
        return new_state, o_inter + o_intra

    state0 = jnp.zeros((B, H, K, V), dtype=jnp.float32)
    _, o = lax.scan(step, state0, (to_chunks(q), to_chunks(k), to_chunks(v), to_chunks(log_f)))
    return o.transpose(1, 0, 3, 2, 4).reshape(B, S, H, V)


def _hgrn2_mixer(h, w_in, lb, out_norm, w_out):
    B, S, _ = h.shape
    proj = h @ w_in
    zq, zf, zi, zg = jnp.split(proj, [HG_FDIM, 2 * HG_FDIM, 2 * HG_FDIM + HG_VDIM], axis=-1)
    q = jax.nn.silu(zq.astype(jnp.float32))
    zf = zf.astype(jnp.float32)
    log_lb = jnp.log(jnp.maximum(lb, LB_FLOOR))
    log_f = jnp.logaddexp(log_lb, jnp.log1p(-lb) + jax.nn.log_sigmoid(zf))
    k = (1.0 - lb) * jax.nn.sigmoid(-zf)
    v = zi.astype(jnp.float32)
    heads = lambda t, d: t.reshape(B, S, HG_HEADS, d)
    o = _hgrn2_chunkwise(heads(q, HG_EXPAND), heads(k, HG_EXPAND),
                         heads(v, HG_HEAD_V), heads(log_f, HG_EXPAND))
    o = _rmsnorm(o, out_norm) * jax.nn.silu(heads(zg.astype(jnp.float32), HG_HEAD_V))
    return o.reshape(B, S, HG_VDIM).astype(h.dtype) @ w_out


def _chunk_gmlp_mixer(h, w_in, ln_g, ln_b, w_s, b_s, w_out):
    B, S, _ = h.shape
    z = jax.nn.gelu(h @ w_in, approximate=False)
    u, v = jnp.split(z, 2, axis=-1)
    v = _layernorm(v, ln_g, ln_b)
    v = v.reshape(B, S // GM_CHUNK, GM_CHUNK, GM_GROUPS, GM_GROUP_DIM)
    ws = w_s * jnp.tril(jnp.ones((GM_CHUNK, GM_CHUNK), dtype=w_s.dtype))
    sv = jnp.einsum('gts,bnsgc->bntgc', ws, v) + b_s.T[None, None, :, :, None]
    y = u * sv.reshape(B, S, GM_WIDTH)
    return y @ w_out


def _swiglu(h, w_gate, w_up, w_down):
    return (jax.nn.silu(h @ w_gate) * (h @ w_up)) @ w_down


def _fwd_setup_inputs(seed: int = 0) -> dict:
    key = jax.random.key(seed)
    ks = jax.random.split(key, 24)
    f32 = jnp.float32
    nrm = lambda k, shape, s: (jax.random.normal(k, shape, f32) * s).astype(f32)
    gain = lambda k, shape: 1.0 + nrm(k, shape, 0.02)
    D = D_MODEL
    return {
        "x": nrm(ks[0], (BATCH, SEQ, D), 1.0),
        "p": nrm(ks[1], (DEPTH, BATCH, SEQ, PLE_DIM), 1.0),
        "hg_w_in": nrm(ks[2], (N_A_LAYERS, D, 2 * HG_FDIM + 2 * HG_VDIM), D ** -0.5),
        "hg_lb_logits": nrm(ks[3], (N_A_LAYERS, HG_FDIM), 0.5),
        "hg_out_norm": gain(ks[4], (N_A_LAYERS, HG_HEAD_V)),
        "hg_w_out": nrm(ks[5], (N_A_LAYERS, HG_VDIM, D), HG_VDIM ** -0.5),
        "gm_w_in": nrm(ks[6], (N_B_LAYERS, D, 2 * GM_WIDTH), D ** -0.5),
        "gm_ln_g": gain(ks[7], (N_B_LAYERS, GM_WIDTH)),
        "gm_ln_b": nrm(ks[8], (N_B_LAYERS, GM_WIDTH), 0.02),
        "gm_w_s": nrm(ks[9], (N_B_LAYERS, GM_GROUPS, GM_CHUNK, GM_CHUNK), GM_CHUNK ** -0.5),
        "gm_b_s": 1.0 + nrm(ks[10], (N_B_LAYERS, GM_GROUPS, GM_CHUNK), 0.1),
        "gm_w_out": nrm(ks[11], (N_B_LAYERS, GM_WIDTH, D), GM_WIDTH ** -0.5),
        "norm_mix_pre": gain(ks[12], (DEPTH, D)),
        "norm_mix_post": gain(ks[13], (DEPTH, D)),
        "norm_ffn_pre": gain(ks[14], (DEPTH, D)),
        "norm_ffn_post": gain(ks[15], (DEPTH, D)),
        "ffn_w_gate": nrm(ks[16], (DEPTH, D, FFN_HIDDEN), D ** -0.5),
        "ffn_w_up": nrm(ks[17], (DEPTH, D, FFN_HIDDEN), D ** -0.5),
        "ffn_w_down": nrm(ks[18], (DEPTH, FFN_HIDDEN, D), FFN_HIDDEN ** -0.5),
        "ple_w_proj": nrm(ks[19], (DEPTH, PLE_DIM, D), PLE_DIM ** -0.5),
        "ple_w_gate": nrm(ks[20], (DEPTH, D, D), D ** -0.5),
        "ple_norm": gain(ks[21], (DEPTH, D)),
    }


def _fwd_reference(x, p, hg_w_in, hg_lb_logits, hg_out_norm, hg_w_out, gm_w_in, gm_ln_g, gm_ln_b,
              gm_w_s, gm_b_s, gm_w_out, norm_mix_pre, norm_mix_post, norm_ffn_pre, norm_ffn_post,
              ffn_w_gate, ffn_w_up, ffn_w_down, ple_w_proj, ple_w_gate, ple_norm):
    lbs = _hgrn_lower_bounds(hg_lb_logits)
    h = x
    for i in range(DEPTH):
        j = i // N_MIXERS
        a = _rmsnorm(h, norm_mix_pre[i])
        if i % N_MIXERS == 0:
            m = _hgrn2_mixer(a, hg_w_in[j], lbs[j], hg_out_norm[j], hg_w_out[j])
        else:
            m = _chunk_gmlp_mixer(a, gm_w_in[j], gm_ln_g[j], gm_ln_b[j], gm_w_s[j], gm_b_s[j],
                                  gm_w_out[j])
        h = h + _rmsnorm(m, norm_mix_post[i])
        f = _swiglu(_rmsnorm(h, norm_ffn_pre[i]), ffn_w_gate[i], ffn_w_up[i], ffn_w_down[i])
        h = h + _rmsnorm(f, norm_ffn_post[i])
        e = p[i].astype(h.dtype) @ ple_w_proj[i]
        gate = jax.nn.sigmoid(h @ ple_w_gate[i])
        h = h + _rmsnorm(e * gate, ple_norm[i])
    return h


import jax as _jax
import jax.numpy as _jnp

TWIN_FORMAT = 'train_step'
FWD_PARAMS = ['x', 'p', 'hg_w_in', 'hg_lb_logits', 'hg_out_norm', 'hg_w_out', 'gm_w_in', 'gm_ln_g', 'gm_ln_b', 'gm_w_s', 'gm_b_s', 'gm_w_out', 'norm_mix_pre', 'norm_mix_post', 'norm_ffn_pre', 'norm_ffn_post', 'ffn_w_gate', 'ffn_w_up', 'ffn_w_down', 'ple_w_proj', 'ple_w_gate', 'ple_norm']
TWIN_WEIGHTS = ['hg_w_in', 'hg_lb_logits', 'hg_out_norm', 'hg_w_out', 'gm_w_in', 'gm_ln_g', 'gm_ln_b', 'gm_w_s', 'gm_b_s', 'gm_w_out', 'norm_mix_pre', 'norm_mix_post', 'norm_ffn_pre', 'norm_ffn_post', 'ffn_w_gate', 'ffn_w_up', 'ffn_w_down', 'ple_w_proj', 'ple_w_gate', 'ple_norm']
TWIN_DIFF_INPUT = 'x'
TWIN_INPUTS = ['x', 'p', 'hg_w_in', 'hg_lb_logits', 'hg_out_norm', 'hg_w_out', 'gm_w_in', 'gm_ln_g', 'gm_ln_b', 'gm_w_s', 'gm_b_s', 'gm_w_out', 'norm_mix_pre', 'norm_mix_post', 'norm_ffn_pre', 'norm_ffn_post', 'ffn_w_gate', 'ffn_w_up', 'ffn_w_down', 'ple_w_proj', 'ple_w_gate', 'ple_norm', 'loss_target', 'm_hg_w_in', 'm_hg_lb_logits', 'm_hg_out_norm', 'm_hg_w_out', 'm_gm_w_in', 'm_gm_ln_g', 'm_gm_ln_b', 'm_gm_w_s', 'm_gm_b_s', 'm_gm_w_out', 'm_norm_mix_pre', 'm_norm_mix_post', 'm_norm_ffn_pre', 'm_norm_ffn_post', 'm_ffn_w_gate', 'm_ffn_w_up', 'm_ffn_w_down', 'm_ple_w_proj', 'm_ple_w_gate', 'm_ple_norm', 'v_hg_w_in', 'v_hg_lb_logits', 'v_hg_out_norm', 'v_hg_w_out', 'v_gm_w_in', 'v_gm_ln_g', 'v_gm_ln_b', 'v_gm_w_s', 'v_gm_b_s', 'v_gm_w_out', 'v_norm_mix_pre', 'v_norm_mix_post', 'v_norm_ffn_pre', 'v_norm_ffn_post', 'v_ffn_w_gate', 'v_ffn_w_up', 'v_ffn_w_down', 'v_ple_w_proj', 'v_ple_w_gate', 'v_ple_norm']
TWIN_OUTPUTS = ['loss', 'grad_x', 'grad_hg_w_in', 'grad_hg_lb_logits', 'grad_hg_out_norm', 'grad_hg_w_out', 'grad_gm_w_in', 'grad_gm_ln_g', 'grad_gm_ln_b', 'grad_gm_w_s', 'grad_gm_b_s', 'grad_gm_w_out', 'grad_norm_mix_pre', 'grad_norm_mix_post', 'grad_norm_ffn_pre', 'grad_norm_ffn_post', 'grad_ffn_w_gate', 'grad_ffn_w_up', 'grad_ffn_w_down', 'grad_ple_w_proj', 'grad_ple_w_gate', 'grad_ple_norm', 'delta_hg_w_in', 'delta_hg_lb_logits', 'delta_hg_out_norm', 'delta_hg_w_out', 'delta_gm_w_in', 'delta_gm_ln_g', 'delta_gm_ln_b', 'delta_gm_w_s', 'delta_gm_b_s', 'delta_gm_w_out', 'delta_norm_mix_pre', 'delta_norm_mix_post', 'delta_norm_ffn_pre', 'delta_norm_ffn_post', 'delta_ffn_w_gate', 'delta_ffn_w_up', 'delta_ffn_w_down', 'delta_ple_w_proj', 'delta_ple_w_gate', 'delta_ple_norm', 'new_m_hg_w_in', 'new_m_hg_lb_logits', 'new_m_hg_out_norm', 'new_m_hg_w_out', 'new_m_gm_w_in', 'new_m_gm_ln_g', 'new_m_gm_ln_b', 'new_m_gm_w_s', 'new_m_gm_b_s', 'new_m_gm_w_out', 'new_m_norm_mix_pre', 'new_m_norm_mix_post', 'new_m_norm_ffn_pre', 'new_m_norm_ffn_post', 'new_m_ffn_w_gate', 'new_m_ffn_w_up', 'new_m_ffn_w_down', 'new_m_ple_w_proj', 'new_m_ple_w_gate', 'new_m_ple_norm', 'new_v_hg_w_in', 'new_v_hg_lb_logits', 'new_v_hg_out_norm', 'new_v_hg_w_out', 'new_v_gm_w_in', 'new_v_gm_ln_g', 'new_v_gm_ln_b', 'new_v_gm_w_s', 'new_v_gm_b_s', 'new_v_gm_w_out', 'new_v_norm_mix_pre', 'new_v_norm_mix_post', 'new_v_norm_ffn_pre', 'new_v_norm_ffn_post', 'new_v_ffn_w_gate', 'new_v_ffn_w_up', 'new_v_ffn_w_down', 'new_v_ple_w_proj', 'new_v_ple_w_gate', 'new_v_ple_norm']
TWIN_LEAF_KINDS = {'loss': 'loss', 'grad_x': 'grad_x', 'grad_hg_w_in': 'grad_w', 'grad_hg_lb_logits': 'grad_w', 'grad_hg_out_norm': 'grad_w', 'grad_hg_w_out': 'grad_w', 'grad_gm_w_in': 'grad_w', 'grad_gm_ln_g': 'grad_w', 'grad_gm_ln_b': 'grad_w', 'grad_gm_w_s': 'grad_w', 'grad_gm_b_s': 'grad_w', 'grad_gm_w_out': 'grad_w', 'grad_norm_mix_pre': 'grad_w', 'grad_norm_mix_post': 'grad_w', 'grad_norm_ffn_pre': 'grad_w', 'grad_norm_ffn_post': 'grad_w', 'grad_ffn_w_gate': 'grad_w', 'grad_ffn_w_up': 'grad_w', 'grad_ffn_w_down': 'grad_w', 'grad_ple_w_proj': 'grad_w', 'grad_ple_w_gate': 'grad_w', 'grad_ple_norm': 'grad_w', 'delta_hg_w_in': 'delta_w', 'delta_hg_lb_logits': 'delta_w', 'delta_hg_out_norm': 'delta_w', 'delta_hg_w_out': 'delta_w', 'delta_gm_w_in': 'delta_w', 'delta_gm_ln_g': 'delta_w', 'delta_gm_ln_b': 'delta_w', 'delta_gm_w_s': 'delta_w', 'delta_gm_b_s': 'delta_w', 'delta_gm_w_out': 'delta_w', 'delta_norm_mix_pre': 'delta_w', 'delta_norm_mix_post': 'delta_w', 'delta_norm_ffn_pre': 'delta_w', 'delta_norm_ffn_post': 'delta_w', 'delta_ffn_w_gate': 'delta_w', 'delta_ffn_w_up': 'delta_w', 'delta_ffn_w_down': 'delta_w', 'delta_ple_w_proj': 'delta_w', 'delta_ple_w_gate': 'delta_w', 'delta_ple_norm': 'delta_w', 'new_m_hg_w_in': 'new_m', 'new_m_hg_lb_logits': 'new_m', 'new_m_hg_out_norm': 'new_m', 'new_m_hg_w_out': 'new_m', 'new_m_gm_w_in': 'new_m', 'new_m_gm_ln_g': 'new_m', 'new_m_gm_ln_b': 'new_m', 'new_m_gm_w_s': 'new_m', 'new_m_gm_b_s': 'new_m', 'new_m_gm_w_out': 'new_m', 'new_m_norm_mix_pre': 'new_m', 'new_m_norm_mix_post': 'new_m', 'new_m_norm_ffn_pre': 'new_m', 'new_m_norm_ffn_post': 'new_m', 'new_m_ffn_w_gate': 'new_m', 'new_m_ffn_w_up': 'new_m', 'new_m_ffn_w_down': 'new_m', 'new_m_ple_w_proj': 'new_m', 'new_m_ple_w_gate': 'new_m', 'new_m_ple_norm': 'new_m', 'new_v_hg_w_in': 'new_v', 'new_v_hg_lb_logits': 'new_v', 'new_v_hg_out_norm': 'new_v', 'new_v_hg_w_out': 'new_v', 'new_v_gm_w_in': 'new_v', 'new_v_gm_ln_g': 'new_v', 'new_v_gm_ln_b': 'new_v', 'new_v_gm_w_s': 'new_v', 'new_v_gm_b_s': 'new_v', 'new_v_gm_w_out': 'new_v', 'new_v_norm_mix_pre': 'new_v', 'new_v_norm_mix_post': 'new_v', 'new_v_norm_ffn_pre': 'new_v', 'new_v_norm_ffn_post': 'new_v', 'new_v_ffn_w_gate': 'new_v', 'new_v_ffn_w_up': 'new_v', 'new_v_ffn_w_down': 'new_v', 'new_v_ple_w_proj': 'new_v', 'new_v_ple_w_gate': 'new_v', 'new_v_ple_norm': 'new_v'}


def _forward(args):
    return _fwd_reference(*[args[k] for k in FWD_PARAMS])


def _output_shape():
    out = _jax.eval_shape(lambda: _forward(_fwd_setup_inputs(0)))
    return out.shape, out.dtype

N_MICROBATCH = 1
ADAM_LR = 0.001
ADAM_B1 = 0.9
ADAM_B2 = 0.999
ADAM_EPS = 1e-08
ADAM_WD = 0.01
ADAM_STEP = 10
PER_EXAMPLE_BATCH_AXIS = {'x': 0, 'p': 1, 'loss_target': 0}
SHARED_INPUTS = []
_WEIGHT_DTYPES = {'hg_w_in': _jnp.float32, 'hg_lb_logits': _jnp.float32, 'hg_out_norm': _jnp.float32, 'hg_w_out': _jnp.float32, 'gm_w_in': _jnp.float32, 'gm_ln_g': _jnp.float32, 'gm_ln_b': _jnp.float32, 'gm_w_s': _jnp.float32, 'gm_b_s': _jnp.float32, 'gm_w_out': _jnp.float32, 'norm_mix_pre': _jnp.float32, 'norm_mix_post': _jnp.float32, 'norm_ffn_pre': _jnp.float32, 'norm_ffn_post': _jnp.float32, 'ffn_w_gate': _jnp.float32, 'ffn_w_up': _jnp.float32, 'ffn_w_down': _jnp.float32, 'ple_w_proj': _jnp.float32, 'ple_w_gate': _jnp.float32, 'ple_norm': _jnp.float32}
MOMENT_SCALE = {'hg_w_in': 6.438304e-01, 'hg_lb_logits': 4.006562e-02, 'hg_out_norm': 3.461548e+00, 'hg_w_out': 9.441439e-01, 'gm_w_in': 5.299206e-01, 'gm_ln_g': 2.750747e-01, 'gm_ln_b': 2.919568e-01, 'gm_w_s': 2.767575e-01, 'gm_b_s': 4.220998e-01, 'gm_w_out': 2.071502e+00, 'norm_mix_pre': 1.061497e+00, 'norm_mix_post': 1.588072e+01, 'norm_ffn_pre': 8.497096e-01, 'norm_ffn_post': 1.579300e+01, 'ffn_w_gate': 3.402489e-01, 'ffn_w_up': 3.782093e-01, 'ffn_w_down': 6.293670e-01, 'ple_w_proj': 5.346884e-01, 'ple_w_gate': 3.074035e-01, 'ple_norm': 1.594504e+01}


def _to_microbatches(a, axis):
    t = _jnp.moveaxis(a, axis, 0)
    t = t.reshape((N_MICROBATCH, t.shape[0] // N_MICROBATCH) + t.shape[1:])
    return _jnp.moveaxis(t, 1, axis + 1)


def setup_inputs(seed: int = 0) -> dict:
    inp = _fwd_setup_inputs(seed)
    key = _jax.random.fold_in(_jax.random.key(seed), 7919)
    shape, _ = _output_shape()
    out = dict(inp)
    out["loss_target"] = _jax.random.normal(_jax.random.fold_in(key, 0), shape, _jnp.float32)
    for i, name in enumerate(TWIN_WEIGHTS):
        w = inp[name].astype(_jnp.float32)
        if MOMENT_SCALE is None:
            s = _jnp.sqrt(_jnp.mean(_jnp.square(w)) + 1e-30)
        else:
            s = MOMENT_SCALE[name]
        km, kv = _jax.random.split(_jax.random.fold_in(key, i + 1))
        out[name] = w
        out["m_" + name] = s * _jax.random.normal(km, w.shape, _jnp.float32)
        out["v_" + name] = (s * s) * _jax.random.uniform(kv, w.shape, _jnp.float32, 0.5, 1.5)
    if N_MICROBATCH > 1:
        for name, axis in PER_EXAMPLE_BATCH_AXIS.items():
            out[name] = _to_microbatches(out[name], axis)
    return {'x': out['x'], 'p': out['p'], 'hg_w_in': out['hg_w_in'], 'hg_lb_logits': out['hg_lb_logits'], 'hg_out_norm': out['hg_out_norm'], 'hg_w_out': out['hg_w_out'], 'gm_w_in': out['gm_w_in'], 'gm_ln_g': out['gm_ln_g'], 'gm_ln_b': out['gm_ln_b'], 'gm_w_s': out['gm_w_s'], 'gm_b_s': out['gm_b_s'], 'gm_w_out': out['gm_w_out'], 'norm_mix_pre': out['norm_mix_pre'], 'norm_mix_post': out['norm_mix_post'], 'norm_ffn_pre': out['norm_ffn_pre'], 'norm_ffn_post': out['norm_ffn_post'], 'ffn_w_gate': out['ffn_w_gate'], 'ffn_w_up': out['ffn_w_up'], 'ffn_w_down': out['ffn_w_down'], 'ple_w_proj': out['ple_w_proj'], 'ple_w_gate': out['ple_w_gate'], 'ple_norm': out['ple_norm'], 'loss_target': out['loss_target'], 'm_hg_w_in': out['m_hg_w_in'], 'm_hg_lb_logits': out['m_hg_lb_logits'], 'm_hg_out_norm': out['m_hg_out_norm'], 'm_hg_w_out': out['m_hg_w_out'], 'm_gm_w_in': out['m_gm_w_in'], 'm_gm_ln_g': out['m_gm_ln_g'], 'm_gm_ln_b': out['m_gm_ln_b'], 'm_gm_w_s': out['m_gm_w_s'], 'm_gm_b_s': out['m_gm_b_s'], 'm_gm_w_out': out['m_gm_w_out'], 'm_norm_mix_pre': out['m_norm_mix_pre'], 'm_norm_mix_post': out['m_norm_mix_post'], 'm_norm_ffn_pre': out['m_norm_ffn_pre'], 'm_norm_ffn_post': out['m_norm_ffn_post'], 'm_ffn_w_gate': out['m_ffn_w_gate'], 'm_ffn_w_up': out['m_ffn_w_up'], 'm_ffn_w_down': out['m_ffn_w_down'], 'm_ple_w_proj': out['m_ple_w_proj'], 'm_ple_w_gate': out['m_ple_w_gate'], 'm_ple_norm': out['m_ple_norm'], 'v_hg_w_in': out['v_hg_w_in'], 'v_hg_lb_logits': out['v_hg_lb_logits'], 'v_hg_out_norm': out['v_hg_out_norm'], 'v_hg_w_out': out['v_hg_w_out'], 'v_gm_w_in': out['v_gm_w_in'], 'v_gm_ln_g': out['v_gm_ln_g'], 'v_gm_ln_b': out['v_gm_ln_b'], 'v_gm_w_s': out['v_gm_w_s'], 'v_gm_b_s': out['v_gm_b_s'], 'v_gm_w_out': out['v_gm_w_out'], 'v_norm_mix_pre': out['v_norm_mix_pre'], 'v_norm_mix_post': out['v_norm_mix_post'], 'v_norm_ffn_pre': out['v_norm_ffn_pre'], 'v_norm_ffn_post': out['v_norm_ffn_post'], 'v_ffn_w_gate': out['v_ffn_w_gate'], 'v_ffn_w_up': out['v_ffn_w_up'], 'v_ffn_w_down': out['v_ffn_w_down'], 'v_ple_w_proj': out['v_ple_w_proj'], 'v_ple_w_gate': out['v_ple_w_gate'], 'v_ple_norm': out['v_ple_norm']}


def _loss(weights, diff, rest, loss_target):
    with _jax.named_scope("forward"):
        args = {**rest, TWIN_DIFF_INPUT: diff, **{k: w.astype(_WEIGHT_DTYPES[k]) for k, w in weights.items()}}
        y = _forward(args)
    with _jax.named_scope("loss_head"):
        err = _jnp.square(y.astype(_jnp.float32) - loss_target)
        return 0.5 * _jnp.sum(_jnp.mean(err, axis=-1)) if err.ndim else 0.5 * err


def _adamw(w, g, m, v):
    m = ADAM_B1 * m + (1.0 - ADAM_B1) * g
    v = ADAM_B2 * v + (1.0 - ADAM_B2) * _jnp.square(g)
    m_hat = m / (1.0 - ADAM_B1 ** ADAM_STEP)
    v_hat = v / (1.0 - ADAM_B2 ** ADAM_STEP)
    delta = -ADAM_LR * (m_hat / (_jnp.sqrt(v_hat) + ADAM_EPS) + ADAM_WD * w)
    return delta, m, v


def reference(x, p, hg_w_in, hg_lb_logits, hg_out_norm, hg_w_out, gm_w_in, gm_ln_g, gm_ln_b, gm_w_s, gm_b_s, gm_w_out, norm_mix_pre, norm_mix_post, norm_ffn_pre, norm_ffn_post, ffn_w_gate, ffn_w_up, ffn_w_down, ple_w_proj, ple_w_gate, ple_norm, loss_target, m_hg_w_in, m_hg_lb_logits, m_hg_out_norm, m_hg_w_out, m_gm_w_in, m_gm_ln_g, m_gm_ln_b, m_gm_w_s, m_gm_b_s, m_gm_w_out, m_norm_mix_pre, m_norm_mix_post, m_norm_ffn_pre, m_norm_ffn_post, m_ffn_w_gate, m_ffn_w_up, m_ffn_w_down, m_ple_w_proj, m_ple_w_gate, m_ple_norm, v_hg_w_in, v_hg_lb_logits, v_hg_out_norm, v_hg_w_out, v_gm_w_in, v_gm_ln_g, v_gm_ln_b, v_gm_w_s, v_gm_b_s, v_gm_w_out, v_norm_mix_pre, v_norm_mix_post, v_norm_ffn_pre, v_norm_ffn_post, v_ffn_w_gate, v_ffn_w_up, v_ffn_w_down, v_ple_w_proj, v_ple_w_gate, v_ple_norm):
    given = dict(x=x, p=p, hg_w_in=hg_w_in, hg_lb_logits=hg_lb_logits, hg_out_norm=hg_out_norm, hg_w_out=hg_w_out, gm_w_in=gm_w_in, gm_ln_g=gm_ln_g, gm_ln_b=gm_ln_b, gm_w_s=gm_w_s, gm_b_s=gm_b_s, gm_w_out=gm_w_out, norm_mix_pre=norm_mix_pre, norm_mix_post=norm_mix_post, norm_ffn_pre=norm_ffn_pre, norm_ffn_post=norm_ffn_post, ffn_w_gate=ffn_w_gate, ffn_w_up=ffn_w_up, ffn_w_down=ffn_w_down, ple_w_proj=ple_w_proj, ple_w_gate=ple_w_gate, ple_norm=ple_norm, loss_target=loss_target, m_hg_w_in=m_hg_w_in, m_hg_lb_logits=m_hg_lb_logits, m_hg_out_norm=m_hg_out_norm, m_hg_w_out=m_hg_w_out, m_gm_w_in=m_gm_w_in, m_gm_ln_g=m_gm_ln_g, m_gm_ln_b=m_gm_ln_b, m_gm_w_s=m_gm_w_s, m_gm_b_s=m_gm_b_s, m_gm_w_out=m_gm_w_out, m_norm_mix_pre=m_norm_mix_pre, m_norm_mix_post=m_norm_mix_post, m_norm_ffn_pre=m_norm_ffn_pre, m_norm_ffn_post=m_norm_ffn_post, m_ffn_w_gate=m_ffn_w_gate, m_ffn_w_up=m_ffn_w_up, m_ffn_w_down=m_ffn_w_down, m_ple_w_proj=m_ple_w_proj, m_ple_w_gate=m_ple_w_gate, m_ple_norm=m_ple_norm, v_hg_w_in=v_hg_w_in, v_hg_lb_logits=v_hg_lb_logits, v_hg_out_norm=v_hg_out_norm, v_hg_w_out=v_hg_w_out, v_gm_w_in=v_gm_w_in, v_gm_ln_g=v_gm_ln_g, v_gm_ln_b=v_gm_ln_b, v_gm_w_s=v_gm_w_s, v_gm_b_s=v_gm_b_s, v_gm_w_out=v_gm_w_out, v_norm_mix_pre=v_norm_mix_pre, v_norm_mix_post=v_norm_mix_post, v_norm_ffn_pre=v_norm_ffn_pre, v_norm_ffn_post=v_norm_ffn_post, v_ffn_w_gate=v_ffn_w_gate, v_ffn_w_up=v_ffn_w_up, v_ffn_w_down=v_ffn_w_down, v_ple_w_proj=v_ple_w_proj, v_ple_w_gate=v_ple_w_gate, v_ple_norm=v_ple_norm)
    weights = {n: given[n] for n in TWIN_WEIGHTS}
    shared = {n: given[n] for n in SHARED_INPUTS}
    per_example = {n: given[n] for n in ['x', 'p']}
    grad_fn = _jax.value_and_grad(_loss, argnums=(0, 1))

    def one_microbatch(ex, loss_target):
        ex = dict(ex)
        diff = ex.pop(TWIN_DIFF_INPUT)
        return grad_fn(weights, diff, {**shared, **ex}, loss_target)

    if N_MICROBATCH == 1:
        loss, (grad_w, grad_x) = one_microbatch(per_example, given["loss_target"])
    else:
        def body(carry, xs):
            loss_sum, grad_sum = carry
            l_k, (gw_k, gx_k) = one_microbatch(xs[0], xs[1])
            with _jax.named_scope("update"):
                return (loss_sum + l_k, _jax.tree.map(_jnp.add, grad_sum, gw_k)), gx_k

        init = (_jnp.zeros((), _jnp.float32), _jax.tree.map(_jnp.zeros_like, weights))
        (loss, grad_w), grad_x = _jax.lax.scan(body, init, (per_example, given["loss_target"]))
    with _jax.named_scope("update"):
        delta_w, new_m, new_v = {}, {}, {}
        for n in TWIN_WEIGHTS:
            delta_w[n], new_m[n], new_v[n] = _adamw(weights[n], grad_w[n], given["m_" + n], given["v_" + n])
    return (loss, grad_x, *[grad_w[n] for n in TWIN_WEIGHTS], *[delta_w[n] for n in TWIN_WEIGHTS],
            *[new_m[n] for n in TWIN_WEIGHTS], *[new_v[n] for n in TWIN_WEIGHTS])
```

```python
import functools

import jax
import jax.numpy as jnp
from jax import lax
from jax.experimental import pallas as pl
from jax.experimental.pallas import tpu as pltpu

F32 = jnp.float32
BF16 = jnp.bfloat16
MESH_ID = pl.DeviceIdType.MESH

LANES = 128
N_CHIPS = 4
N_DEV = 8
VMEM_LIMIT = 56 * 1024 * 1024
HG_SUB = 16
HG_BLOCK = 256
GM_CHUNK = 128
GM_BLOCK = 512
LB_FLOOR = 1e-30
EPS = 1e-6
ADAM_LR, ADAM_B1, ADAM_B2, ADAM_EPS, ADAM_WD, ADAM_STEP = 0.001, 0.9, 0.999, 1e-08, 0.01, 10


def _tile(n, pref, mult=LANES):
    if n <= pref:
        return n
    t = (pref // mult) * mult
    while t >= mult:
        if n % t == 0:
            return t
        t -= mult
    return n


def _cp(n_axes):
    return pltpu.CompilerParams(dimension_semantics=("arbitrary",) * n_axes, vmem_limit_bytes=VMEM_LIMIT)


def _dense(block):
    return tuple(b for b in block if b is not None)


def _bmap(name, grid, ins, outs, compute, scalars=()):
    n_s, n_in = len(scalars), len(ins)

    def body(*refs):
        in_refs = refs[n_s:n_s + n_in]
        out_refs = refs[n_s + n_in:]
        vals = [r[...] for r in in_refs]
        res = compute(*vals)
        for r, o, spec in zip(out_refs, res, outs):
            keep = spec[4]
            if keep is None:
                r[...] = o.astype(r.dtype)
            else:
                first = functools.reduce(jnp.logical_and, [pl.program_id(a) == 0 for a in range(keep, len(grid))])

                @pl.when(first)
                def _():
                    r[...] = jnp.zeros(r.shape, r.dtype)

                r[...] += o.astype(r.dtype)

    grid_spec = pltpu.PrefetchScalarGridSpec(
        num_scalar_prefetch=n_s, grid=grid,
        in_specs=[pl.BlockSpec(b, m) for _, b, m in ins],
        out_specs=[pl.BlockSpec(o[2], o[3]) for o in outs])
    return pl.pallas_call(
        body, name=name, grid_spec=grid_spec,
        out_shape=[jax.ShapeDtypeStruct(o[0], o[1]) for o in outs],
        compiler_params=_cp(len(grid)),
    )(*scalars, *[a for a, _, _ in ins])


def bmap_fwd(name, fn, grid, ins, outs, scalars=()):
    return _bmap(name, grid, ins, outs, lambda *v: fn(*[x.astype(F32) for x in v]), scalars)


def bmap_bwd(name, fn, grid, ins, cots, grads, scalars=()):
    n_in = len(ins)
    diff = [g[0] for g in grads]
    cot_ins = [c for c in cots if c is not None]

    def compute(*vals):
        xs = [v.astype(F32) for v in vals[:n_in]]
        cvals = list(vals[n_in:])

        def f(*d):
            full = list(xs)
            for i, dv in zip(diff, d):
                full[i] = dv
            return tuple(fn(*full))

        res, pull = jax.vjp(f, *[xs[i] for i in diff])
        cts = []
        for r, c in zip(res, cots):
            cts.append(jnp.zeros_like(r) if c is None else cvals.pop(0).astype(F32))
        return pull(tuple(cts))

    outs = [(ins[i][0].shape, dt, ins[i][1], ins[i][2], keep) for i, dt, keep in grads]
    return _bmap(name, grid, list(ins) + cot_ins, outs, compute, scalars)


def _mm(name, a, b, out_shape, out_dtype, grid, a_spec, b_spec, o_spec, dims, addend=None, alias_out=None):
    nk = grid[2]
    o_dense = _dense(o_spec[0])
    has_add = addend is not None
    has_alias = alias_out is not None

    def body(*refs):
        a_ref, b_ref = refs[0], refs[1]
        pos = 2
        c_ref = None
        if has_add:
            c_ref = refs[pos]
            pos += 1
        if has_alias:
            pos += 1
        o_ref = refs[pos]
        acc_ref = refs[pos + 1] if nk > 1 else None
        p = lax.dot_general(a_ref[...].astype(BF16), b_ref[...].astype(BF16), (dims, ((), ())),
                            preferred_element_type=F32)

        def finish(total):
            if has_add:
                total = total + c_ref[...].astype(F32)
            o_ref[...] = total.astype(o_ref.dtype)

        if nk == 1:
            finish(p)
        else:
            k = pl.program_id(2)

            @pl.when(k == 0)
            def _():
                acc_ref[...] = p

            @pl.when(jnp.logical_and(k > 0, k < nk - 1))
            def _():
                acc_ref[...] += p

            @pl.when(k == nk - 1)
            def _():
                finish(acc_ref[...] + p)

    in_specs = [pl.BlockSpec(*a_spec), pl.BlockSpec(*b_spec)]
    operands = [a, b]
    if has_add:
        in_specs.append(pl.BlockSpec(o_spec[0], o_spec[1]))
        operands.append(addend)
    aliases = {}
    if has_alias:
        in_specs.append(pl.BlockSpec(memory_space=pl.ANY))
        aliases = {len(operands): 0}
        operands.append(alias_out)
    return pl.pallas_call(
        body, name=name, grid=grid, in_specs=in_specs, out_specs=pl.BlockSpec(*o_spec),
        out_shape=jax.ShapeDtypeStruct(out_shape, out_dtype),
        scratch_shapes=[pltpu.VMEM(o_dense, F32)] if nk > 1 else [],
        input_output_aliases=aliases,
        compiler_params=pltpu.CompilerParams(dimension_semantics=("parallel", "parallel", "arbitrary"),
                                             vmem_limit_bytes=VMEM_LIMIT),
    )(*operands)


NN, NT, TN = ((1,), (0,)), ((1,), (1,)), ((0,), (0,))
TM = 512
TT = 1024
TN_PREF = 1408


def mm_fwd(name, x, wg, l, kind, out_dtype=F32, parts=False, xl=None):
    _, _, R, C = wg.shape
    T = x.shape[-2]
    tm = _tile(T, TM, 8)
    x_blk = (tm, None) if xl is None else (None, tm, None)
    if kind == "col":
        tn = _tile(C, TN_PREF)
        npc = C // tn
        grid = (T // tm, N_CHIPS * npc, 1)
        a_blk = (tm, R) if xl is None else (None, tm, R)
        a_map = (lambda i, j, k: (i, 0)) if xl is None else (lambda i, j, k: (xl, i, 0))
        b_spec = ((None, None, R, tn), lambda i, j, k: (j // npc, l, 0, j % npc))
        if parts:
            out_shape = (N_CHIPS, T, C)
            o_spec = ((None, tm, tn), lambda i, j, k: (j // npc, i, j % npc))
        else:
            out_shape = (T, N_CHIPS * C)
            o_spec = ((tm, tn), lambda i, j, k: (i, j))
    else:
        tn = _tile(C, 1024)
        grid = (T // tm, C // tn, N_CHIPS)
        a_blk = (tm, R)
        a_map = lambda i, j, k: (i, k)
        b_spec = ((None, None, R, tn), lambda i, j, k: (k, l, 0, j))
        out_shape = (T, C)
        o_spec = ((tm, tn), lambda i, j, k: (i, j))
    del x_blk
    return _mm(name, x, wg, out_shape, out_dtype, grid, (a_blk, a_map), b_spec, o_spec, NN)


def mm_bwd_x(name, dy, wg, l, kind, out_dtype=F32, parts=False, addend=None):
    _, _, R, C = wg.shape
    T = dy.shape[-2]
    tm = _tile(T, TM, 8)
    if kind == "col":
        tk = _tile(C, TN_PREF)
        npc = C // tk
        tno = _tile(R, 2048)
        grid = (T // tm, R // tno, N_CHIPS * npc)
        if parts:
            a_spec = ((None, tm, tk), lambda i, j, k: (k // npc, i, k % npc))
        else:
            a_spec = ((tm, tk), lambda i, j, k: (i, k))
        b_spec = ((None, None, tno, tk), lambda i, j, k: (k // npc, l, j, k % npc))
        out_shape = (T, R)
        o_spec = ((tm, tno), lambda i, j, k: (i, j))
    else:
        grid = (T // tm, N_CHIPS, 1)
        a_spec = ((tm, C), lambda i, j, k: (i, 0))
        b_spec = ((None, None, R, C), lambda i, j, k: (j, l, 0, 0))
        out_shape = (T, N_CHIPS * R)
        o_spec = ((tm, R), lambda i, j, k: (i, j))
    return _mm(name, dy, wg, out_shape, out_dtype, grid, a_spec, b_spec, o_spec, NT, addend=addend)


def mm_bwd_w(name, x, dy, dwg, l, kind, parts=False, xl=None):
    _, _, R, C = dwg.shape
    T = dy.shape[-2]
    tt = _tile(T, TT, 16)
    nt = T // tt
    if kind == "col":
        tn = _tile(C, TN_PREF)
        npc = C // tn
        tr = _tile(R, 1024)
        grid = (R // tr, N_CHIPS * npc, nt)
        if xl is None:
            a_spec = ((tt, tr), lambda i, j, t: (t, i))
        else:
            a_spec = ((None, tt, tr), lambda i, j, t: (xl, t, i))
        if parts:
            b_spec = ((None, tt, tn), lambda i, j, t: (j // npc, t, j % npc))
        else:
            b_spec = ((tt, tn), lambda i, j, t: (t, j))
        o_spec = ((None, None, tr, tn), lambda i, j, t: (j // npc, l, i, j % npc))
    else:
        tn = _tile(C, 1024)
        grid = (N_CHIPS, C // tn, nt)
        a_spec = ((tt, R), lambda i, j, t: (t, i))
        b_spec = ((tt, tn), lambda i, j, t: (t, j))
        o_spec = ((None, None, R, tn), lambda i, j, t: (i, l, 0, j))
    return _mm(name, x, dy, dwg.shape, dwg.dtype, grid, a_spec, b_spec, o_spec, TN, alias_out=dwg)


def _sigmoid(x):
    return 1.0 / (1.0 + jnp.exp(-x))


def _rms(x, g):
    return x * lax.rsqrt(jnp.mean(x * x, axis=-1, keepdims=True) + EPS) * g


def f_prenorm(h, g):
    return (_rms(h, g),)


def f_prenorm_thru(h, g):
    return _rms(h, g), h


def f_post_pre(h, m, g_post, g_pre):
    h1 = h + _rms(m, g_post)
    return h1, _rms(h1, g_pre)


def f_swiglu(gate, up):
    return (gate * _sigmoid(gate) * up,)


def f_post(h1, f, g):
    return (h1 + _rms(f, g),)


def f_ple(h2, e, zg, g):
    return (h2 + _rms(e * _sigmoid(zg), g),)


def f_ple_pre(h2, e, zg, g, g_next):
    h3 = h2 + _rms(e * _sigmoid(zg), g)
    return h3, _rms(h3, g_next)


def _gelu(x):
    return 0.5 * x * (1.0 + lax.erf(x * 0.7071067811865476))


def f_gm_in(z, ln_g, ln_b):
    w = z.shape[-1] // 2
    u = _gelu(z[:, :w])
    v = _gelu(z[:, w:])
    mu = jnp.mean(v, axis=-1, keepdims=True)
    vc = v - mu
    vn = vc * lax.rsqrt(jnp.mean(vc * vc, axis=-1, keepdims=True) + EPS) * ln_g + ln_b
    return u, vn


def f_gm_spatial(u, vn, ws, bs):
    t = lax.broadcasted_iota(jnp.int32, ws.shape, 0)
    s = lax.broadcasted_iota(jnp.int32, ws.shape, 1)
    wm = jnp.where(t >= s, ws, 0.0).astype(BF16)
    ys = []
    for n in range(u.shape[0] // GM_CHUNK):
        rows = slice(n * GM_CHUNK, (n + 1) * GM_CHUNK)
        sv = jnp.dot(wm, vn[rows].astype(BF16), preferred_element_type=F32) + bs
        ys.append(u[rows] * sv)
    return (jnp.concatenate(ys, axis=0) if len(ys) > 1 else ys[0],)


def f_adam(w, g, m, v):
    m = ADAM_B1 * m + (1.0 - ADAM_B1) * g
    v = ADAM_B2 * v + (1.0 - ADAM_B2) * jnp.square(g)
    m_hat = m / (1.0 - ADAM_B1 ** ADAM_STEP)
    v_hat = v / (1.0 - ADAM_B2 ** ADAM_STEP)
    delta = -ADAM_LR * (m_hat / (jnp.sqrt(v_hat) + ADAM_EPS) + ADAM_WD * w)
    return delta, m, v


def _make_f_lb(n_layers):
    def f_lb(*logits):
        mx = functools.reduce(jnp.maximum, logits)
        ex = [jnp.exp(r - mx) for r in logits]
        tot = functools.reduce(lambda a, b: a + b, ex)
        sm = [e / tot for e in ex]
        outs = []
        run = jnp.zeros_like(sm[0])
        for j in range(n_layers):
            if j > 0:
                run = run + sm[j]
            lb = run
            outs += [jnp.log(jnp.maximum(lb, LB_FLOOR)), jnp.log(1.0 - lb), 1.0 - lb]
        return tuple(outs)
    return f_lb


def rows_fwd(name, fn, rows, params, out_dtypes, tm=256):
    T = rows[0].shape[0]
    tm = _tile(T, tm, 16)
    ins = [(r, (tm, r.shape[1]), lambda i: (i, 0)) for r in rows]
    ins += [(p, p.shape, lambda i: (0, 0)) for p in params]
    shapes = jax.eval_shape(lambda *a: fn(*a), *[jax.ShapeDtypeStruct((tm, r.shape[1]), F32) for r in rows],
                            *[jax.ShapeDtypeStruct(p.shape, F32) for p in params])
    outs = [((T, s.shape[1]), dt, (tm, s.shape[1]), lambda i: (i, 0), None) for s, dt in zip(shapes, out_dtypes)]
    return bmap_fwd(name, fn, (T // tm,), ins, outs)


def rows_bwd(name, fn, rows, params, cots, row_grad_dtypes, tm=256):
    T = rows[0].shape[0]
    tm = _tile(T, tm, 16)
    ins = [(r, (tm, r.shape[1]), lambda i: (i, 0)) for r in rows]
    ins += [(p, p.shape, lambda i: (0, 0)) for p in params]
    cts = [None if c is None else (c, (tm, c.shape[1]), lambda i: (i, 0)) for c in cots]
    grads = [(i, dt, None) for i, dt in enumerate(row_grad_dtypes) if dt is not None]
    grads += [(len(rows) + j, F32, 0) for j in range(len(params))]
    return bmap_bwd(name, fn, (T // tm,), ins, cts, grads)


def _log_sigmoid(z):
    return jnp.minimum(z, 0.0) - jnp.log(1.0 + jnp.exp(-jnp.abs(z)))


def _hg_gates(zf, ll0, ll1, oml):
    x2 = ll1 + _log_sigmoid(zf)
    mx = jnp.maximum(ll0, x2)
    g = mx + jnp.log(jnp.exp(ll0 - mx) + jnp.exp(x2 - mx))
    k = oml * _sigmoid(-zf)
    n = zf.shape[0]
    t = lax.broadcasted_iota(jnp.int32, (n, n), 0)
    s = lax.broadcasted_iota(jnp.int32, (n, n), 1)
    tri = jnp.where(t >= s, 1.0, 0.0).astype(F32)
    b = jnp.dot(tri, g, precision=lax.Precision.HIGHEST, preferred_element_type=F32)
    tot = jnp.sum(g, axis=0, keepdims=True)
    return k, b, tot


def _hg_state(st, zf, zi, ll0, ll1, oml):
    k, b, tot = _hg_gates(zf, ll0, ll1, oml)
    kd = k * jnp.exp(tot - b)
    return st * jnp.exp(tot) + jnp.dot(zi.T.astype(BF16), kd.astype(BF16), preferred_element_type=F32)


def _hg_step(st, zq, zf, zi, zg, ll0, ll1, oml, onorm):
    n = zq.shape[0]
    q = zq * _sigmoid(zq)
    k, b, tot = _hg_gates(zf, ll0, ll1, oml)
    o = lax.dot_general((q * jnp.exp(b)).astype(BF16), st.astype(BF16), (NT, ((), ())), preferred_element_type=F32)
    rows = lax.broadcasted_iota(jnp.int32, (n, 1), 0)
    for s in range(n):
        sel = rows == s
        b_s = jnp.sum(jnp.where(sel, b, 0.0), axis=0, keepdims=True)
        k_s = jnp.sum(jnp.where(sel, k, 0.0), axis=0, keepdims=True)
        v_s = jnp.sum(jnp.where(sel, zi, 0.0), axis=0, keepdims=True)
        w = jnp.where(rows >= s, jnp.exp(jnp.minimum(b - b_s, 0.0)), 0.0)
        a = jnp.sum(q * k_s * w, axis=1, keepdims=True)
        o = o + a * v_s
    kd = k * jnp.exp(tot - b)
    st_new = st * jnp.exp(tot) + jnp.dot(zi.T.astype(BF16), kd.astype(BF16), preferred_element_type=F32)
    og = _rms(o, onorm) * (zg * _sigmoid(zg))
    return og, st_new


def hgrn_fwd(name, proj4, ll0, ll1, oml, onorm, n_seq):
    _, T, D = proj4.shape
    H = D // LANES
    S = T // n_seq
    tb = min(HG_BLOCK, S)
    nblk = S // tb
    nsub = tb // HG_SUB

    def body(p_ref, ll0_ref, ll1_ref, oml_ref, on_ref, og_ref, st_ref, st):
        @pl.when(pl.program_id(2) == 0)
        def _():
            st[...] = jnp.zeros(st.shape, F32)

        st_ref[...] = st[...]
        pr = (ll0_ref[...], ll1_ref[...], oml_ref[...])
        on = on_ref[...]

        def step(j, carry):
            r = pl.ds(pl.multiple_of(j * HG_SUB, HG_SUB), HG_SUB)
            og, s_new = _hg_step(st[...], p_ref[0, r, :], p_ref[1, r, :], p_ref[2, r, :], p_ref[3, r, :], *pr, on)
            og_ref[r, :] = og.astype(og_ref.dtype)
            st[...] = s_new
            return carry

        lax.fori_loop(0, nsub, step, 0)

    vec = pl.BlockSpec((1, LANES), lambda h, b, n: (0, h))
    return pl.pallas_call(
        body, name=name, grid=(H, n_seq, nblk),
        in_specs=[pl.BlockSpec((4, tb, LANES), lambda h, b, n: (0, b * nblk + n, h)), vec, vec, vec,
                  pl.BlockSpec((1, LANES), lambda h, b, n: (0, 0))],
        out_specs=[pl.BlockSpec((tb, LANES), lambda h, b, n: (b * nblk + n, h)),
                   pl.BlockSpec((None, None, None, LANES, LANES), lambda h, b, n: (h, b, n, 0, 0))],
        out_shape=[jax.ShapeDtypeStruct((T, D), BF16),
                   jax.ShapeDtypeStruct((H, n_seq, nblk, LANES, LANES), F32)],
        scratch_shapes=[pltpu.VMEM((LANES, LANES), F32)],
        compiler_params=_cp(3),
    )(proj4, ll0, ll1, oml, onorm)


def hgrn_bwd(name, proj4, states, dog, ll0, ll1, oml, onorm, n_seq):
    _, T, D = proj4.shape
    H = D // LANES
    S = T // n_seq
    tb = min(HG_BLOCK, S)
    nblk = S // tb
    nsub = tb // HG_SUB

    def body(p_ref, st_ref, dog_ref, ll0_ref, ll1_ref, oml_ref, on_ref,
             dp_ref, dll0_ref, dll1_ref, doml_ref, don_ref, sbuf, dst):
        b_id, n_id = pl.program_id(1), pl.program_id(2)

        @pl.when(n_id == 0)
        def _():
            dst[...] = jnp.zeros(dst.shape, F32)

        @pl.when(jnp.logical_and(b_id == 0, n_id == 0))
        def _():
            for r in (dll0_ref, dll1_ref, doml_ref):
                r[...] = jnp.zeros(r.shape, F32)

        @pl.when(jnp.logical_and(jnp.logical_and(b_id == 0, n_id == 0), pl.program_id(0) == 0))
        def _():
            don_ref[...] = jnp.zeros(don_ref.shape, F32)

        pr = (ll0_ref[...], ll1_ref[...], oml_ref[...])
        on = on_ref[...]

        def fwd(j, s_cur):
            r = pl.ds(pl.multiple_of(j * HG_SUB, HG_SUB), HG_SUB)
            sbuf[j] = s_cur
            return _hg_state(s_cur, p_ref[1, r, :], p_ref[2, r, :], *pr)

        lax.fori_loop(0, nsub, fwd, st_ref[...])

        def bwd(jj, acc):
            j = nsub - 1 - jj
            r = pl.ds(pl.multiple_of(j * HG_SUB, HG_SUB), HG_SUB)
            args = (sbuf[j], p_ref[0, r, :], p_ref[1, r, :], p_ref[2, r, :], p_ref[3, r, :], *pr, on)
            _, pull = jax.vjp(_hg_step, *args)
            d = pull((dog_ref[r, :].astype(F32), dst[...]))
            dst[...] = d[0]
            for part in range(4):
                dp_ref[part, r, :] = d[1 + part].astype(dp_ref.dtype)
            return tuple(a + g for a, g in zip(acc, d[5:]))

        zero = jnp.zeros((1, LANES), F32)
        acc = lax.fori_loop(0, nsub, bwd, (zero, zero, zero, zero))
        dll0_ref[...] += acc[0]
        dll1_ref[...] += acc[1]
        doml_ref[...] += acc[2]
        don_ref[...] += acc[3]

    rev = lambda b, n: b * nblk + (nblk - 1 - n)
    vec = pl.BlockSpec((1, LANES), lambda h, b, n: (0, h))
    one = pl.BlockSpec((1, LANES), lambda h, b, n: (0, 0))
    return pl.pallas_call(
        body, name=name, grid=(H, n_seq, nblk),
        in_specs=[pl.BlockSpec((4, tb, LANES), lambda h, b, n: (0, rev(b, n), h)),
                  pl.BlockSpec((None, None, None, LANES, LANES), lambda h, b, n: (h, b, nblk - 1 - n, 0, 0)),
                  pl.BlockSpec((tb, LANES), lambda h, b, n: (rev(b, n), h)), vec, vec, vec, one],
        out_specs=[pl.BlockSpec((4, tb, LANES), lambda h, b, n: (0, rev(b, n), h)), vec, vec, vec, one],
        out_shape=[jax.ShapeDtypeStruct((4, T, D), BF16)] + [jax.ShapeDtypeStruct((1, D), F32)] * 3
        + [jax.ShapeDtypeStruct((1, LANES), F32)],
        scratch_shapes=[pltpu.VMEM((nsub, LANES, LANES), F32), pltpu.VMEM((LANES, LANES), F32)],
        compiler_params=_cp(3),
    )(proj4, states, dog, ll0, ll1, oml, onorm)


def _place():
    x, y, c = lax.axis_index("x"), lax.axis_index("y"), lax.axis_index("c")
    chips = [(1 - x, y), (x, 1 - y), (1 - x, 1 - y)]
    return x, y, c, chips


ANY = pl.BlockSpec(memory_space=pl.ANY)


def _comm_call(name, body, ins, out_shapes, sems):
    return pl.pallas_call(
        body, name=name, in_specs=[ANY] * len(ins), out_specs=[ANY] * len(out_shapes),
        out_shape=out_shapes, scratch_shapes=sems,
        compiler_params=pltpu.CompilerParams(has_side_effects=True),
    )(*ins)


def gather_weights(shards):
    n = len(shards)

    def body(*refs):
        src, dst = refs[:n], refs[n:2 * n]
        send1, recv1, send2, recv2, local = refs[2 * n:]
        x, y, c, chips = _place()
        q = 2 * x + y
        sib = (x, y, 1 - c)

        def half(a, who):
            lh = src[a].shape[0] // 2
            return pl.ds(who * lh, lh)

        own = [pltpu.make_async_copy(src[a], dst[a].at[q], local.at[a]) for a in range(n)]
        for cp in own:
            cp.start()
        first, passed = [], []
        for a in range(n):
            for j, (px, py) in enumerate(chips):
                first.append(pltpu.make_async_remote_copy(
                    src_ref=src[a].at[half(a, c)], dst_ref=dst[a].at[q, half(a, c)],
                    send_sem=send1.at[a, j], recv_sem=recv1.at[a, j], device_id=(px, py, c), device_id_type=MESH_ID))
        for cp in first:
            cp.start()
        for j, (px, py) in enumerate(chips):
            qj = 2 * px + py
            for a in range(n):
                got = dst[a].at[qj, half(a, c)]
                pltpu.make_async_remote_copy(src_ref=got, dst_ref=got, send_sem=send1.at[a, j], recv_sem=recv1.at[a, j],
                                             device_id=(px, py, c), device_id_type=MESH_ID).wait_recv()
                fw = pltpu.make_async_remote_copy(src_ref=got, dst_ref=got, send_sem=send2.at[a, j],
                                                  recv_sem=recv2.at[a, j], device_id=sib, device_id_type=MESH_ID)
                fw.start()
                passed.append(fw)
        for j, (px, py) in enumerate(chips):
            qj = 2 * px + py
            for a in range(n):
                theirs = dst[a].at[qj, half(a, 1 - c)]
                pltpu.make_async_remote_copy(src_ref=theirs, dst_ref=theirs, send_sem=send2.at[a, j],
                                             recv_sem=recv2.at[a, j], device_id=sib, device_id_type=MESH_ID).wait_recv()
        for cp in first + passed:
            cp.wait_send()
        for cp in own:
            cp.wait()

    outs = [jax.ShapeDtypeStruct((N_CHIPS,) + s.shape, s.dtype) for s in shards]
    sems = [pltpu.SemaphoreType.DMA((n, 3))] * 4 + [pltpu.SemaphoreType.DMA((n,))]
    return _comm_call("gather_weights", body, shards, outs, sems)


def swap_layer_halves(grads):
    n = len(grads)

    def body(*refs):
        src, dst = refs[:n], refs[n:2 * n]
        send, recv = refs[2 * n:]
        x, y, c, _ = _place()
        cps = []
        for a in range(n):
            lh = src[a].shape[1] // 2
            cps.append(pltpu.make_async_remote_copy(
                src_ref=src[a].at[:, pl.ds((1 - c) * lh, lh)], dst_ref=dst[a], send_sem=send.at[a], recv_sem=recv.at[a],
                device_id=(x, y, 1 - c), device_id_type=MESH_ID))
        for cp in cps:
            cp.start()
        for cp in cps:
            cp.wait()

    outs = [jax.ShapeDtypeStruct((g.shape[0], g.shape[1] // 2) + g.shape[2:], g.dtype) for g in grads]
    return _comm_call("swap_layer_halves", body, grads, outs, [pltpu.SemaphoreType.DMA((n,))] * 2)


def scatter_to_chips(parts):
    n = len(parts)

    def body(*refs):
        src, dst = refs[:n], refs[n:2 * n]
        send, recv, local = refs[2 * n:]
        x, y, c, chips = _place()
        q = 2 * x + y
        own = [pltpu.make_async_copy(src[a].at[q], dst[a].at[q], local.at[a]) for a in range(n)]
        for cp in own:
            cp.start()
        cps = []
        for a in range(n):
            for j, (px, py) in enumerate(chips):
                cps.append(pltpu.make_async_remote_copy(
                    src_ref=src[a].at[2 * px + py], dst_ref=dst[a].at[q], send_sem=send.at[a, j], recv_sem=recv.at[a, j],
                    device_id=(px, py, c), device_id_type=MESH_ID))
        for cp in cps:
            cp.start()
        for a in range(n):
            for j, (px, py) in enumerate(chips):
                got = dst[a].at[2 * px + py]
                pltpu.make_async_remote_copy(src_ref=got, dst_ref=got, send_sem=send.at[a, j], recv_sem=recv.at[a, j],
                                             device_id=(px, py, c), device_id_type=MESH_ID).wait_recv()
        for cp in cps:
            cp.wait_send()
        for cp in own:
            cp.wait()

    outs = [jax.ShapeDtypeStruct(p.shape, p.dtype) for p in parts]
    sems = [pltpu.SemaphoreType.DMA((n, 3))] * 2 + [pltpu.SemaphoreType.DMA((n,))]
    return _comm_call("scatter_to_chips", body, parts, outs, sems)


def join_layer_halves(halves):
    n = len(halves)

    def body(*refs):
        src, dst = refs[:n], refs[n:2 * n]
        send, recv, local = refs[2 * n:]
        x, y, c, _ = _place()
        own, cps = [], []
        for a in range(n):
            lh = src[a].shape[0]
            mine = dst[a].at[pl.ds(c * lh, lh)]
            own.append(pltpu.make_async_copy(src[a], mine, local.at[a]))
            cps.append(pltpu.make_async_remote_copy(src_ref=src[a], dst_ref=mine, send_sem=send.at[a], recv_sem=recv.at[a],
                                                    device_id=(x, y, 1 - c), device_id_type=MESH_ID))
        for cp in own + cps:
            cp.start()
        for a in range(n):
            lh = src[a].shape[0]
            theirs = dst[a].at[pl.ds((1 - c) * lh, lh)]
            pltpu.make_async_remote_copy(src_ref=theirs, dst_ref=theirs, send_sem=send.at[a], recv_sem=recv.at[a],
                                         device_id=(x, y, 1 - c), device_id_type=MESH_ID).wait_recv()
        for cp in cps:
            cp.wait_send()
        for cp in own:
            cp.wait()

    outs = [jax.ShapeDtypeStruct((2 * h.shape[0],) + h.shape[1:], h.dtype) for h in halves]
    sems = [pltpu.SemaphoreType.DMA((n,))] * 3
    return _comm_call("join_layer_halves", body, halves, outs, sems)


def share_with_all(packed):
    def body(src, dst, send, recv, local):
        x, y, c, _ = _place()
        me = 4 * x + 2 * y + c
        own = pltpu.make_async_copy(src, dst.at[me], local)
        own.start()
        cps = []
        for k in range(1, N_DEV):
            px, py, pc = x ^ (k >> 2), y ^ ((k >> 1) & 1), c ^ (k & 1)
            cps.append(pltpu.make_async_remote_copy(src_ref=src, dst_ref=dst.at[me], send_sem=send.at[k - 1],
                                                    recv_sem=recv.at[k - 1], device_id=(px, py, pc), device_id_type=MESH_ID))
        for cp in cps:
            cp.start()
        for k in range(1, N_DEV):
            px, py, pc = x ^ (k >> 2), y ^ ((k >> 1) & 1), c ^ (k & 1)
            got = dst.at[4 * px + 2 * py + pc]
            pltpu.make_async_remote_copy(src_ref=got, dst_ref=got, send_sem=send.at[k - 1], recv_sem=recv.at[k - 1],
                                         device_id=(px, py, pc), device_id_type=MESH_ID).wait_recv()
        for cp in cps:
            cp.wait_send()
        own.wait()

    outs = [jax.ShapeDtypeStruct((N_DEV,) + packed.shape, packed.dtype)]
    sems = [pltpu.SemaphoreType.DMA((N_DEV - 1,))] * 2 + [pltpu.SemaphoreType.DMA(())]
    return _comm_call("share_with_all", body, [packed], outs, sems)[0]


def _w_tiles(R, C):
    tr = _tile(R, max(8, (1 << 20) // (4 * C) // 8 * 8), 8)
    return tr


def cast_bf16(w):
    L, R, C = w.shape
    tr = _w_tiles(R, C)
    spec = ((None, tr, C), lambda l, r: (l, r, 0))
    return bmap_fwd("cast_bf16", lambda a: (a,), (L, R // tr), [(w,) + spec], [(w.shape, BF16) + spec + (None,)])[0]


def add_sibling_half(g, b1, c_arr):
    Q, L, R, C = g.shape
    lh = L // 2
    tr = _w_tiles(R, C)
    blk = (None, None, tr, C)
    ins = [(g, blk, lambda q, l, r, c: (q, c[0] * lh + l, r, 0)), (b1, blk, lambda q, l, r, c: (q, l, r, 0))]
    outs = [(b1.shape, BF16, blk, lambda q, l, r, c: (q, l, r, 0), None)]
    return bmap_fwd("add_sibling_half", lambda a, b: (a + b,), (Q, lh, R // tr), ins, outs, scalars=(c_arr,))[0]


def sum_chips(b2):
    Q, lh, R, C = b2.shape
    tr = _w_tiles(R, C)
    blk = (None, None, tr, C)
    ins = [(b2, blk, (lambda l, r, qq=qq: (qq, l, r, 0))) for qq in range(Q)]
    outs = [((lh, R, C), F32, (None, tr, C), lambda l, r: (l, r, 0), None)]
    return bmap_fwd("sum_chips", lambda a, b, c, d: (((a + b) + c) + d,), (lh, R // tr), ins, outs)[0]


def sum_devices(slots):
    nd, NR, C = slots.shape
    tr = _tile(NR, 512, 8)
    ins = [(slots, (None, tr, C), (lambda r, dd=dd: (dd, r, 0))) for dd in range(nd)]
    outs = [((NR, C), F32, (tr, C), lambda r: (r, 0), None)]
    return bmap_fwd("sum_devices", lambda *a: (functools.reduce(lambda u, v: u + v, a),), (NR // tr,), ins, outs)[0]


def adamw(name, w, g, m, v):
    if w.ndim == 2:
        R, C = w.shape
        tr = _w_tiles(R, C)
        spec = ((tr, C), lambda r: (r, 0))
        grid = (R // tr,)
    else:
        L, R, C = w.shape
        tr = _w_tiles(R, C)
        spec = ((None, tr, C), lambda l, r: (l, r, 0))
        grid = (L, R // tr)
    ins = [(a,) + spec for a in (w, g, m, v)]
    outs = [(w.shape, F32) + spec + (None,)] * 3
    return bmap_fwd(name, f_adam, grid, ins, outs)


def loss_and_grad(h, target):
    T, D = h.shape
    tm = _tile(T, 256, 8)

    def fn(hv, tv):
        d = hv - tv
        return jnp.sum(d * d, keepdims=True).reshape(1, 1) * (0.5 / D), d * (1.0 / D)

    ins = [(h, (tm, D), lambda i: (i, 0)), (target, (tm, D), lambda i: (i, 0))]
    outs = [((1, 1), F32, (1, 1), lambda i: (0, 0), 0), ((T, D), F32, (tm, D), lambda i: (i, 0), None)]
    return bmap_fwd("loss_and_grad", fn, (T // tm,), ins, outs)


BIG = ("hg_w_in", "hg_w_out", "gm_w_in", "gm_w_out", "ffn_w_gate", "ffn_w_up", "ffn_w_down", "ple_w_proj", "ple_w_gate")
KIND = {"hg_w_in": "col", "hg_w_out": "row", "gm_w_in": "col", "gm_w_out": "row", "ffn_w_gate": "col",
        "ffn_w_up": "col", "ffn_w_down": "row", "ple_w_proj": "col", "ple_w_gate": "row"}
SMALL = ("hg_lb_logits", "hg_out_norm", "gm_ln_g", "gm_ln_b", "gm_w_s", "gm_b_s", "norm_mix_pre", "norm_mix_post",
         "norm_ffn_pre", "norm_ffn_post", "ple_norm")
WEIGHTS = ("hg_w_in", "hg_lb_logits", "hg_out_norm", "hg_w_out", "gm_w_in", "gm_ln_g", "gm_ln_b", "gm_w_s", "gm_b_s",
           "gm_w_out", "norm_mix_pre", "norm_mix_post", "norm_ffn_pre", "norm_ffn_post", "ffn_w_gate", "ffn_w_up",
           "ffn_w_down", "ple_w_proj", "ple_w_gate", "ple_norm")


def _pack(arrs):
    rows = []
    for a in arrs:
        flat = a.reshape(-1)
        pad = (-flat.shape[0]) % (8 * LANES)
        rows.append(jnp.pad(flat, (0, pad)).reshape(-1, LANES))
    return jnp.concatenate(rows, axis=0)


def _unpack(packed, shapes):
    out, r = [], 0
    for s in shapes:
        size = 1
        for d in s:
            size *= d
        nr = -(-size // (8 * LANES)) * 8
        out.append(packed[r:r + nr].reshape(-1)[:size].reshape(s))
        r += nr
    return out


def _step(x, p, W, M, V, loss_target):
    n_seq, S, D = x.shape
    T = n_seq * S
    depth = p.shape[0]
    n_hg = W["hg_w_in"].shape[0]
    x2 = x.reshape(T, D)
    p3 = p.reshape(depth, T, p.shape[-1])
    tgt = loss_target.reshape(T, D)
    xi, yi, ci = lax.axis_index("x"), lax.axis_index("y"), lax.axis_index("c")
    q_me = 2 * xi + yi
    c_arr = jnp.reshape(ci, (1,)).astype(jnp.int32)

    G = dict(zip(BIG, gather_weights([cast_bf16(W[k]) for k in BIG])))
    DW = {k: lax.empty(G[k].shape, BF16) for k in BIG}
    ln_full = share_with_all(_pack([W["gm_ln_g"], W["gm_ln_b"]]))
    n_gm, dq = W["gm_ln_g"].shape
    ln_parts = [_unpack(ln_full[4 * qx + 2 * qy + 0], [(n_gm, dq), (n_gm, dq)]) for qx in range(2) for qy in range(2)]
    ln_g = jnp.concatenate([lp[0] for lp in ln_parts], axis=1)
    ln_b = jnp.concatenate([lp[1] for lp in ln_parts], axis=1)

    row = lambda a, i: a[i][None, :]
    f_lb = _make_f_lb(n_hg)
    lb_rows = [row(W["hg_lb_logits"], j) for j in range(n_hg)]
    one = (1, D)
    lb_ins = [(r, one, lambda i: (0, 0)) for r in lb_rows]
    lb_out = bmap_fwd("hg_lower_bounds", f_lb, (1,), lb_ins, [(one, F32, one, lambda i: (0, 0), None)] * (3 * n_hg))

    saved = []
    h = x2
    a = rows_fwd("prenorm", f_prenorm, [h], [row(W["norm_mix_pre"], 0)], [BF16])[0]
    for i in range(depth):
        j = i // 2
        sv = {"h": h, "a": a}
        if i % 2 == 0:
            proj4 = mm_fwd("hg_in", a, G["hg_w_in"], j, "col", parts=True)
            lbp = lb_out[3 * j:3 * j + 3]
            onorm = row(W["hg_out_norm"], j)
            og, states = hgrn_fwd("hgrn_fwd", proj4, *lbp, onorm, n_seq)
            m = mm_fwd("hg_out", og, G["hg_w_out"], j, "row")
            sv.update(proj4=proj4, states=states, og=og, lbp=lbp, onorm=onorm)
        else:
            z = mm_fwd("gm_in", a, G["gm_w_in"], j, "col")
            lg, lb_ = row(ln_g, j), row(ln_b, j)
            u, vn = rows_fwd("gm_gelu_ln", f_gm_in, [z], [lg, lb_], [F32, BF16], tm=128)
            ws = W["gm_w_s"][j]
            bs = W["gm_b_s"][j][:, :, None]
            gb = min(GM_BLOCK, S)
            sp_grid = (D // LANES, T // gb)
            sp_ins = [(u, (gb, LANES), lambda g, n: (n, g)), (vn, (gb, LANES), lambda g, n: (n, g)),
                      (ws, (None, GM_CHUNK, GM_CHUNK), lambda g, n: (g, 0, 0)),
                      (bs, (None, GM_CHUNK, 1), lambda g, n: (g, 0, 0))]
            y = bmap_fwd("gm_spatial", f_gm_spatial, sp_grid, sp_ins,
                         [((T, D), BF16, (gb, LANES), lambda g, n: (n, g), None)])[0]
            m = mm_fwd("gm_out", y, G["gm_w_out"], j, "row")
            sv.update(z=z, lg=lg, lb_=lb_, sp_ins=sp_ins, sp_grid=sp_grid, y=y)
        g_post, g_fpre = row(W["norm_mix_post"], i), row(W["norm_ffn_pre"], i)
        h1, fin = rows_fwd("mix_post_ffn_pre", f_post_pre, [h, m], [g_post, g_fpre], [F32, BF16])
        gate = mm_fwd("ffn_gate", fin, G["ffn_w_gate"], i, "col")
        up = mm_fwd("ffn_up", fin, G["ffn_w_up"], i, "col")
        act = rows_fwd("swiglu", f_swiglu, [gate, up], [], [BF16], tm=128)[0]
        f = mm_fwd("ffn_down", act, G["ffn_w_down"], i, "row")
        g_fpost = row(W["norm_ffn_post"], i)
        h2 = rows_fwd("ffn_post", f_post, [h1, f], [g_fpost], [F32])[0]
        e = mm_fwd("ple_proj", p3, G["ple_w_proj"], i, "col", xl=i)
        zg = mm_fwd("ple_gate", h2, G["ple_w_gate"], i, "row")
        g_ple = row(W["ple_norm"], i)
        sv.update(m=m, h1=h1, fin=fin, gate=gate, up=up, act=act, f=f, h2=h2, e=e, zg=zg,
                  g_post=g_post, g_fpre=g_fpre, g_fpost=g_fpost, g_ple=g_ple)
        if i + 1 < depth:
            g_next = row(W["norm_mix_pre"], i + 1)
            h, a = rows_fwd("ple_next_pre", f_ple_pre, [h2, e, zg], [g_ple, g_next], [F32, BF16])
            sv["g_next"] = g_next
        else:
            h = rows_fwd("ple_last", f_ple, [h2, e, zg], [g_ple], [F32])[0]
        saved.append(sv)

    loss_part, dh = loss_and_grad(h, tgt)
    loss = lax.psum(loss_part[0, 0], ("x", "y", "c"))

    sg = {k: [None] * W[k].shape[0] for k in ("norm_mix_pre", "norm_mix_post", "norm_ffn_pre", "norm_ffn_post", "ple_norm",
                                              "hg_out_norm", "gm_ln_g", "gm_ln_b", "gm_w_s", "gm_b_s")}
    d_lbp = [None] * (3 * n_hg)
    da_next = None
    for i in reversed(range(depth)):
        j = i // 2
        sv = saved[i]
        if i + 1 < depth:
            dh2, de, dzg, d_gple, d_gnext = rows_bwd("ple_next_pre_bwd", f_ple_pre, [sv["h2"], sv["e"], sv["zg"]],
                                                     [sv["g_ple"], sv["g_next"]], [dh, da_next], [F32, BF16, BF16])
            sg["norm_mix_pre"][i + 1] = d_gnext
        else:
            dh2, de, dzg, d_gple = rows_bwd("ple_last_bwd", f_ple, [sv["h2"], sv["e"], sv["zg"]], [sv["g_ple"]], [dh],
                                            [F32, BF16, BF16])
        sg["ple_norm"][i] = d_gple
        DW["ple_w_proj"] = mm_bwd_w("ple_proj_dw", p3, de, DW["ple_w_proj"], i, "col", xl=i)
        DW["ple_w_gate"] = mm_bwd_w("ple_gate_dw", sv["h2"], dzg, DW["ple_w_gate"], i, "row")
        dh2 = mm_bwd_x("ple_gate_dx", dzg, G["ple_w_gate"], i, "row", addend=dh2)
        dh1, df, d_gfpost = rows_bwd("ffn_post_bwd", f_post, [sv["h1"], sv["f"]], [sv["g_fpost"]], [dh2], [F32, BF16])
        sg["norm_ffn_post"][i] = d_gfpost
        dact = mm_bwd_x("ffn_down_dx", df, G["ffn_w_down"], i, "row")
        DW["ffn_w_down"] = mm_bwd_w("ffn_down_dw", sv["act"], df, DW["ffn_w_down"], i, "row")
        dgate, dup = rows_bwd("swiglu_bwd", f_swiglu, [sv["gate"], sv["up"]], [], [dact], [BF16, BF16], tm=128)
        dfin = mm_bwd_x("ffn_gate_dx", dgate, G["ffn_w_gate"], i, "col")
        dfin = mm_bwd_x("ffn_up_dx", dup, G["ffn_w_up"], i, "col", addend=dfin)
        DW["ffn_w_gate"] = mm_bwd_w("ffn_gate_dw", sv["fin"], dgate, DW["ffn_w_gate"], i, "col")
        DW["ffn_w_up"] = mm_bwd_w("ffn_up_dw", sv["fin"], dup, DW["ffn_w_up"], i, "col")
        dh, dm, d_gpost, d_gfpre = rows_bwd("mix_post_ffn_pre_bwd", f_post_pre, [sv["h"], sv["m"]],
                                            [sv["g_post"], sv["g_fpre"]], [dh1, dfin], [F32, BF16])
        sg["norm_mix_post"][i], sg["norm_ffn_pre"][i] = d_gpost, d_gfpre
        if i % 2 == 0:
            dog = mm_bwd_x("hg_out_dx", dm, G["hg_w_out"], j, "row")
            DW["hg_w_out"] = mm_bwd_w("hg_out_dw", sv["og"], dm, DW["hg_w_out"], j, "row")
            dproj4, d0, d1, d2, d_on = hgrn_bwd("hgrn_bwd", sv["proj4"], sv["states"], dog, *sv["lbp"], sv["onorm"], n_seq)
            d_lbp[3 * j:3 * j + 3] = [d0, d1, d2]
            sg["hg_out_norm"][j] = d_on
            da_next = mm_bwd_x("hg_in_dx", dproj4, G["hg_w_in"], j, "col", parts=True)
            DW["hg_w_in"] = mm_bwd_w("hg_in_dw", sv["a"], dproj4, DW["hg_w_in"], j, "col", parts=True)
        else:
            dy = mm_bwd_x("gm_out_dx", dm, G["gm_w_out"], j, "row")
            DW["gm_w_out"] = mm_bwd_w("gm_out_dw", sv["y"], dm, DW["gm_w_out"], j, "row")
            gb = sv["sp_ins"][0][1][0]
            du, dvn, dws, dbs = bmap_bwd("gm_spatial_bwd", f_gm_spatial, sv["sp_grid"], sv["sp_ins"],
                                         [(dy, (gb, LANES), lambda g, n: (n, g))],
                                         [(0, F32, None), (1, F32, None), (2, F32, 1), (3, F32, 1)])
            sg["gm_w_s"][j], sg["gm_b_s"][j] = dws, dbs[:, :, 0]
            dz, d_lg, d_lb = rows_bwd("gm_gelu_ln_bwd", f_gm_in, [sv["z"]], [sv["lg"], sv["lb_"]], [du, dvn], [BF16], tm=128)
            sg["gm_ln_g"][j], sg["gm_ln_b"][j] = d_lg, d_lb
            da_next = mm_bwd_x("gm_in_dx", dz, G["gm_w_in"], j, "col")
            DW["gm_w_in"] = mm_bwd_w("gm_in_dw", sv["a"], dz, DW["gm_w_in"], j, "col")
    g0 = row(W["norm_mix_pre"], 0)
    grad_x, d_g0 = rows_bwd("prenorm_bwd", f_prenorm_thru, [saved[0]["h"]], [g0], [da_next, dh], [F32])
    sg["norm_mix_pre"][0] = d_g0
    d_logits = bmap_bwd("hg_lower_bounds_bwd", f_lb, (1,), lb_ins, [(d, one, lambda i: (0, 0)) for d in d_lbp],
                        [(jj, F32, None) for jj in range(n_hg)])

    small_g = {k: jnp.stack([v.reshape(W[k].shape[1:] if k not in ("gm_ln_g", "gm_ln_b") else (D,)) for v in sg[k]])
               for k in sg}
    small_g["hg_lb_logits"] = jnp.concatenate(d_logits, axis=0)
    small_shapes = [small_g[k].shape for k in SMALL]
    red = sum_devices(share_with_all(_pack([small_g[k] for k in SMALL])))
    small_red = dict(zip(SMALL, _unpack(red, small_shapes)))
    for k in ("gm_ln_g", "gm_ln_b"):
        small_red[k] = lax.dynamic_slice_in_dim(small_red[k], q_me * dq, dq, axis=1)
    pk = lambda d: _pack([d[k] for k in SMALL])
    s_delta, s_m, s_v = adamw("adamw_small", pk(W), pk(small_red), pk(M), pk(V))
    shard_shapes = [W[k].shape for k in SMALL]
    out_g, out_d, out_m, out_v = dict(small_red), {}, {}, {}
    for dct, packed in ((out_d, s_delta), (out_m, s_m), (out_v, s_v)):
        dct.update(zip(SMALL, _unpack(packed, shard_shapes)))

    b1 = swap_layer_halves([DW[k] for k in BIG])
    pair = [add_sibling_half(DW[k], b, c_arr) for k, b in zip(BIG, b1)]
    b2 = scatter_to_chips(pair)
    full = join_layer_halves([sum_chips(b) for b in b2])
    for k, g in zip(BIG, full):
        out_g[k] = g
        out_d[k], out_m[k], out_v[k] = adamw("adamw_" + k, W[k], g, M[k], V[k])

    outs = [loss, grad_x.reshape(x.shape)]
    for dct in (out_g, out_d, out_m, out_v):
        outs += [dct[k] for k in WEIGHTS]
    return tuple(outs)


def kernel(x, p, hg_w_in, hg_lb_logits, hg_out_norm, hg_w_out, gm_w_in, gm_ln_g, gm_ln_b, gm_w_s, gm_b_s, gm_w_out, norm_mix_pre, norm_mix_post, norm_ffn_pre, norm_ffn_post, ffn_w_gate, ffn_w_up, ffn_w_down, ple_w_proj, ple_w_gate, ple_norm, loss_target, m_hg_w_in, m_hg_lb_logits, m_hg_out_norm, m_hg_w_out, m_gm_w_in, m_gm_ln_g, m_gm_ln_b, m_gm_w_s, m_gm_b_s, m_gm_w_out, m_norm_mix_pre, m_norm_mix_post, m_norm_ffn_pre, m_norm_ffn_post, m_ffn_w_gate, m_ffn_w_up, m_ffn_w_down, m_ple_w_proj, m_ple_w_gate, m_ple_norm, v_hg_w_in, v_hg_lb_logits, v_hg_out_norm, v_hg_w_out, v_gm_w_in, v_gm_ln_g, v_gm_ln_b, v_gm_w_s, v_gm_b_s, v_gm_w_out, v_norm_mix_pre, v_norm_mix_post, v_norm_ffn_pre, v_norm_ffn_post, v_ffn_w_gate, v_ffn_w_up, v_ffn_w_down, v_ple_w_proj, v_ple_w_gate, v_ple_norm):
    W = dict(zip(WEIGHTS, (hg_w_in, hg_lb_logits, hg_out_norm, hg_w_out, gm_w_in, gm_ln_g, gm_ln_b, gm_w_s, gm_b_s, gm_w_out,
                           norm_mix_pre, norm_mix_post, norm_ffn_pre, norm_ffn_post, ffn_w_gate, ffn_w_up, ffn_w_down,
                           ple_w_proj, ple_w_gate, ple_norm)))
    M = dict(zip(WEIGHTS, (m_hg_w_in, m_hg_lb_logits, m_hg_out_norm, m_hg_w_out, m_gm_w_in, m_gm_ln_g, m_gm_ln_b, m_gm_w_s,
                           m_gm_b_s, m_gm_w_out, m_norm_mix_pre, m_norm_mix_post, m_norm_ffn_pre, m_norm_ffn_post,
                           m_ffn_w_gate, m_ffn_w_up, m_ffn_w_down, m_ple_w_proj, m_ple_w_gate, m_ple_norm)))
    V = dict(zip(WEIGHTS, (v_hg_w_in, v_hg_lb_logits, v_hg_out_norm, v_hg_w_out, v_gm_w_in, v_gm_ln_g, v_gm_ln_b, v_gm_w_s,
                           v_gm_b_s, v_gm_w_out, v_norm_mix_pre, v_norm_mix_post, v_norm_ffn_pre, v_norm_ffn_post,
                           v_ffn_w_gate, v_ffn_w_up, v_ffn_w_down, v_ple_w_proj, v_ple_w_gate, v_ple_norm)))
    return _step(x, p, W, M, V, loss_target)
```

```python
import functools

import jax
import jax.numpy as jnp
from jax import lax
from jax.experimental import pallas as pl
from jax.experimental.pallas import tpu as pltpu

F32 = jnp.float32
BF16 = jnp.bfloat16
MESH_ID = pl.DeviceIdType.MESH

LANES = 128
N_CHIPS = 4
N_DEV = 8
VMEM_LIMIT = 56 * 1024 * 1024
HG_SUB = 64
HG_BLOCK = 256
HG_HEADS_PER = 2
GM_CHUNK = 128
GM_BLOCK = 512
LB_FLOOR = 1e-30
EPS = 1e-6
ADAM_LR, ADAM_B1, ADAM_B2, ADAM_EPS, ADAM_WD, ADAM_STEP = 0.001, 0.9, 0.999, 1e-08, 0.01, 10


def _tile(n, pref, mult=LANES):
    if n <= pref:
        return n
    t = (pref // mult) * mult
    while t >= mult:
        if n % t == 0:
            return t
        t -= mult
    return n


def _cp(n_axes):
    return pltpu.CompilerParams(dimension_semantics=("arbitrary",) * n_axes, vmem_limit_bytes=VMEM_LIMIT)


def _dense(block):
    return tuple(b for b in block if b is not None)


def _bmap(name, grid, ins, outs, compute, scalars=()):
    n_s, n_in = len(scalars), len(ins)

    def body(*refs):
        in_refs = refs[n_s:n_s + n_in]
        out_refs = refs[n_s + n_in:]
        vals = [r[...] for r in in_refs]
        res = compute(*vals)
        for r, o, spec in zip(out_refs, res, outs):
            keep = spec[4]
            if keep is None:
                r[...] = o.astype(r.dtype)
            else:
                first = functools.reduce(jnp.logical_and, [pl.program_id(a) == 0 for a in range(keep, len(grid))])

                @pl.when(first)
                def _():
                    r[...] = jnp.zeros(r.shape, r.dtype)

                r[...] += o.astype(r.dtype)

    grid_spec = pltpu.PrefetchScalarGridSpec(
        num_scalar_prefetch=n_s, grid=grid,
        in_specs=[pl.BlockSpec(b, m) for _, b, m in ins],
        out_specs=[pl.BlockSpec(o[2], o[3]) for o in outs])
    return pl.pallas_call(
        body, name=name, grid_spec=grid_spec,
        out_shape=[jax.ShapeDtypeStruct(o[0], o[1]) for o in outs],
        compiler_params=_cp(len(grid)),
    )(*scalars, *[a for a, _, _ in ins])


def bmap_fwd(name, fn, grid, ins, outs, scalars=()):
    return _bmap(name, grid, ins, outs, lambda *v: fn(*[x.astype(F32) for x in v]), scalars)


def bmap_bwd(name, fn, grid, ins, cots, grads, scalars=()):
    n_in = len(ins)
    diff = [g[0] for g in grads]
    cot_ins = [c for c in cots if c is not None]

    def compute(*vals):
        xs = [v.astype(F32) for v in vals[:n_in]]
        cvals = list(vals[n_in:])

        def f(*d):
            full = list(xs)
            for i, dv in zip(diff, d):
                full[i] = dv
            return tuple(fn(*full))

        res, pull = jax.vjp(f, *[xs[i] for i in diff])
        cts = []
        for r, c in zip(res, cots):
            cts.append(jnp.zeros_like(r) if c is None else cvals.pop(0).astype(F32))
        return pull(tuple(cts))

    outs = [(ins[i][0].shape, dt, ins[i][1], ins[i][2], keep) for i, dt, keep in grads]
    return _bmap(name, grid, list(ins) + cot_ins, outs, compute, scalars)


def _mm(name, a, b, out_shape, out_dtype, grid, a_spec, b_spec, o_spec, dims, addend=None, alias_out=None):
    nk = grid[2]
    o_dense = _dense(o_spec[0])
    has_add = addend is not None
    has_alias = alias_out is not None

    def body(*refs):
        a_ref, b_ref = refs[0], refs[1]
        pos = 2
        c_ref = None
        if has_add:
            c_ref = refs[pos]
            pos += 1
        if has_alias:
            pos += 1
        o_ref = refs[pos]
        acc_ref = refs[pos + 1] if nk > 1 else None
        p = lax.dot_general(a_ref[...].astype(BF16), b_ref[...].astype(BF16), (dims, ((), ())),
                            preferred_element_type=F32)

        def finish(total):
            if has_add:
                total = total + c_ref[...].astype(F32)
            o_ref[...] = total.astype(o_ref.dtype)

        if nk == 1:
            finish(p)
        else:
            k = pl.program_id(2)

            @pl.when(k == 0)
            def _():
                acc_ref[...] = p

            @pl.when(jnp.logical_and(k > 0, k < nk - 1))
            def _():
                acc_ref[...] += p

            @pl.when(k == nk - 1)
            def _():
                finish(acc_ref[...] + p)

    in_specs = [pl.BlockSpec(*a_spec), pl.BlockSpec(*b_spec)]
    operands = [a, b]
    if has_add:
        in_specs.append(pl.BlockSpec(o_spec[0], o_spec[1]))
        operands.append(addend)
    aliases = {}
    if has_alias:
        in_specs.append(pl.BlockSpec(memory_space=pl.ANY))
        aliases = {len(operands): 0}
        operands.append(alias_out)
    return pl.pallas_call(
        body, name=name, grid=grid, in_specs=in_specs, out_specs=pl.BlockSpec(*o_spec),
        out_shape=jax.ShapeDtypeStruct(out_shape, out_dtype),
        scratch_shapes=[pltpu.VMEM(o_dense, F32)] if nk > 1 else [],
        input_output_aliases=aliases,
        compiler_params=pltpu.CompilerParams(dimension_semantics=("parallel", "parallel", "arbitrary"),
                                             vmem_limit_bytes=VMEM_LIMIT),
    )(*operands)


NN, NT, TN = ((1,), (0,)), ((1,), (1,)), ((0,), (0,))
TM = 512
TT = 1024
TN_PREF = 1408


def mm_fwd(name, x, wg, l, kind, out_dtype=F32, parts=False, xl=None):
    _, _, R, C = wg.shape
    T = x.shape[-2]
    tm = _tile(T, TM, 8)
    if kind == "col":
        tn = _tile(C, TN_PREF)
        npc = C // tn
        grid = (T // tm, N_CHIPS * npc, 1)
        a_blk = (tm, R) if xl is None else (None, tm, R)
        a_map = (lambda i, j, k: (i, 0)) if xl is None else (lambda i, j, k: (xl, i, 0))
        b_spec = ((None, None, R, tn), lambda i, j, k: (j // npc, l, 0, j % npc))
        if parts:
            out_shape = (N_CHIPS, T, C)
            o_spec = ((None, tm, tn), lambda i, j, k: (j // npc, i, j % npc))
        else:
            out_shape = (T, N_CHIPS * C)
            o_spec = ((tm, tn), lambda i, j, k: (i, j))
    else:
        tn = _tile(C, 1024)
        grid = (T // tm, C // tn, N_CHIPS)
        a_blk = (tm, R)
        a_map = lambda i, j, k: (i, k)
        b_spec = ((None, None, R, tn), lambda i, j, k: (k, l, 0, j))
        out_shape = (T, C)
        o_spec = ((tm, tn), lambda i, j, k: (i, j))
    return _mm(name, x, wg, out_shape, out_dtype, grid, (a_blk, a_map), b_spec, o_spec, NN)


def mm_bwd_x(name, dy, wg, l, kind, out_dtype=F32, parts=False, addend=None):
    _, _, R, C = wg.shape
    T = dy.shape[-2]
    tm = _tile(T, TM, 8)
    if kind == "col":
        tk = _tile(C, TN_PREF)
        npc = C // tk
        tno = _tile(R, 2048)
        grid = (T // tm, R // tno, N_CHIPS * npc)
        if parts:
            a_spec = ((None, tm, tk), lambda i, j, k: (k // npc, i, k % npc))
        else:
            a_spec = ((tm, tk), lambda i, j, k: (i, k))
        b_spec = ((None, None, tno, tk), lambda i, j, k: (k // npc, l, j, k % npc))
        out_shape = (T, R)
        o_spec = ((tm, tno), lambda i, j, k: (i, j))
    else:
        grid = (T // tm, N_CHIPS, 1)
        a_spec = ((tm, C), lambda i, j, k: (i, 0))
        b_spec = ((None, None, R, C), lambda i, j, k: (j, l, 0, 0))
        out_shape = (T, N_CHIPS * R)
        o_spec = ((tm, R), lambda i, j, k: (i, j))
    return _mm(name, dy, wg, out_shape, out_dtype, grid, a_spec, b_spec, o_spec, NT, addend=addend)


def mm_bwd_w(name, x, dy, dwg, l, kind, parts=False, xl=None):
    _, _, R, C = dwg.shape
    T = dy.shape[-2]
    tt = _tile(T, TT, 16)
    nt = T // tt
    if kind == "col":
        tn = _tile(C, TN_PREF)
        npc = C // tn
        tr = _tile(R, 1024)
        grid = (R // tr, N_CHIPS * npc, nt)
        if xl is None:
            a_spec = ((tt, tr), lambda i, j, t: (t, i))
        else:
            a_spec = ((None, tt, tr), lambda i, j, t: (xl, t, i))
        if parts:
            b_spec = ((None, tt, tn), lambda i, j, t: (j // npc, t, j % npc))
        else:
            b_spec = ((tt, tn), lambda i, j, t: (t, j))
        o_spec = ((None, None, tr, tn), lambda i, j, t: (j // npc, l, i, j % npc))
    else:
        tn = _tile(C, 1024)
        grid = (N_CHIPS, C // tn, nt)
        a_spec = ((tt, R), lambda i, j, t: (t, i))
        b_spec = ((tt, tn), lambda i, j, t: (t, j))
        o_spec = ((None, None, R, tn), lambda i, j, t: (i, l, 0, j))
    return _mm(name, x, dy, dwg.shape, dwg.dtype, grid, a_spec, b_spec, o_spec, TN, alias_out=dwg)


def _sigmoid(x):
    return 0.5 * jnp.tanh(0.5 * x) + 0.5


def _rms(x, g):
    return x * lax.rsqrt(jnp.mean(x * x, axis=-1, keepdims=True) + EPS) * g


def f_prenorm(h, g):
    return (_rms(h, g),)


def f_prenorm_thru(h, g):
    return _rms(h, g), h


def f_post_pre(h, m, g_post, g_pre):
    h1 = h + _rms(m, g_post)
    return h1, _rms(h1, g_pre)


def f_swiglu(gate, up):
    return (gate * _sigmoid(gate) * up,)


def f_post(h1, f, g):
    return (h1 + _rms(f, g),)


def f_ple(h2, e, zg, g):
    return (h2 + _rms(e * _sigmoid(zg), g),)


def f_ple_pre(h2, e, zg, g, g_next):
    h3 = h2 + _rms(e * _sigmoid(zg), g)
    return h3, _rms(h3, g_next)


def _gelu(x):
    return 0.5 * x * (1.0 + lax.erf(x * 0.7071067811865476))


def f_gm_in(z, ln_g, ln_b):
    w = z.shape[-1] // 2
    u = _gelu(z[:, :w])
    v = _gelu(z[:, w:])
    mu = jnp.mean(v, axis=-1, keepdims=True)
    vc = v - mu
    vn = vc * lax.rsqrt(jnp.mean(vc * vc, axis=-1, keepdims=True) + EPS) * ln_g + ln_b
    return u, vn


def f_gm_spatial(u, vn, ws, bs):
    t = lax.broadcasted_iota(jnp.int32, ws.shape, 0)
    s = lax.broadcasted_iota(jnp.int32, ws.shape, 1)
    wm = jnp.where(t >= s, ws, 0.0).astype(BF16)
    ys = []
    for n in range(u.shape[0] // GM_CHUNK):
        rows = slice(n * GM_CHUNK, (n + 1) * GM_CHUNK)
        sv = jnp.dot(wm, vn[rows].astype(BF16), preferred_element_type=F32) + bs
        ys.append(u[rows] * sv)
    return (jnp.concatenate(ys, axis=0) if len(ys) > 1 else ys[0],)


def f_adam(w, g, m, v):
    m = ADAM_B1 * m + (1.0 - ADAM_B1) * g
    v = ADAM_B2 * v + (1.0 - ADAM_B2) * jnp.square(g)
    m_hat = m / (1.0 - ADAM_B1 ** ADAM_STEP)
    v_hat = v / (1.0 - ADAM_B2 ** ADAM_STEP)
    delta = -ADAM_LR * (m_hat / (jnp.sqrt(v_hat) + ADAM_EPS) + ADAM_WD * w)
    return delta, m, v


def _make_f_lb(n_layers):
    def f_lb(*logits):
        mx = functools.reduce(jnp.maximum, logits)
        ex = [jnp.exp(r - mx) for r in logits]
        tot = functools.reduce(lambda a, b: a + b, ex)
        sm = [e / tot for e in ex]
        outs = []
        run = jnp.zeros_like(sm[0])
        for j in range(n_layers):
            if j > 0:
                run = run + sm[j]
            lb = run
            outs += [jnp.log(jnp.maximum(lb, LB_FLOOR)), jnp.log(1.0 - lb), 1.0 - lb]
        return tuple(outs)
    return f_lb


def rows_fwd(name, fn, rows, params, out_dtypes, tm=256):
    T = rows[0].shape[0]
    tm = _tile(T, tm, 16)
    ins = [(r, (tm, r.shape[1]), lambda i: (i, 0)) for r in rows]
    ins += [(p, p.shape, lambda i: (0, 0)) for p in params]
    shapes = jax.eval_shape(lambda *a: fn(*a), *[jax.ShapeDtypeStruct((tm, r.shape[1]), F32) for r in rows],
                            *[jax.ShapeDtypeStruct(p.shape, F32) for p in params])
    outs = [((T, s.shape[1]), dt, (tm, s.shape[1]), lambda i: (i, 0), None) for s, dt in zip(shapes, out_dtypes)]
    return bmap_fwd(name, fn, (T // tm,), ins, outs)


def rows_bwd(name, fn, rows, params, cots, row_grad_dtypes, tm=256):
    T = rows[0].shape[0]
    tm = _tile(T, tm, 16)
    ins = [(r, (tm, r.shape[1]), lambda i: (i, 0)) for r in rows]
    ins += [(p, p.shape, lambda i: (0, 0)) for p in params]
    cts = [None if c is None else (c, (tm, c.shape[1]), lambda i: (i, 0)) for c in cots]
    grads = [(i, dt, None) for i, dt in enumerate(row_grad_dtypes) if dt is not None]
    grads += [(len(rows) + j, F32, 0) for j in range(len(params))]
    return bmap_bwd(name, fn, (T // tm,), ins, cts, grads)


def _log_sigmoid(z):
    return jnp.minimum(z, 0.0) - jnp.log(1.0 + jnp.exp(-jnp.abs(z)))


def _hg_gates(zf, ll0, ll1, oml):
    x2 = ll1 + _log_sigmoid(zf)
    mx = jnp.maximum(ll0, x2)
    g = mx + jnp.log(jnp.exp(ll0 - mx) + jnp.exp(x2 - mx))
    return g, oml * _sigmoid(-zf)


def hg_constants(n):
    levels = n.bit_length() - 1
    r = jnp.arange(n, dtype=jnp.int32)
    bounds = [r] + [((r >> (s + 1)) << (s + 1)) + ((1 << s) - 1) for s in range(levels)]
    sel = jnp.concatenate([(r[None, :] <= bd[:, None]) for bd in bounds], axis=0).astype(BF16)
    later = jnp.stack([((r >> s) & 1) for s in range(levels)]).astype(F32)
    later = jnp.broadcast_to(later[:, :, None], (levels, n, LANES))
    pair = jnp.stack([(r[:, None] >> (s + 1)) == (r[None, :] >> (s + 1)) for s in range(levels)]).astype(F32)
    return sel, sel.T, later, pair


def _dot2(m, x):
    hi = x.astype(BF16)
    lo = (x - hi.astype(F32)).astype(BF16)
    p = jnp.dot(m, jnp.concatenate([hi, lo], axis=1), preferred_element_type=F32)
    w = x.shape[1]
    return p[:, :w] + p[:, w:]


@jax.custom_vjp
def _sel_dot(sel, selt, g):
    return _dot2(sel, g)


def _sel_dot_fwd(sel, selt, g):
    return _dot2(sel, g), (sel, selt)


def _sel_dot_bwd(res, d):
    sel, selt = res
    return jnp.zeros_like(sel), jnp.zeros_like(selt), _dot2(selt, d)


_sel_dot.defvjp(_sel_dot_fwd, _sel_dot_bwd)


def _hg_state(st, zf, zi, ll0, ll1, oml, tri):
    g, k = _hg_gates(zf, ll0, ll1, oml)
    b = _dot2(tri, g)
    tot = jnp.sum(g, axis=0, keepdims=True)
    kd = k * jnp.exp(tot - b)
    return st * jnp.exp(tot) + jnp.dot(zi.T.astype(BF16), kd.astype(BF16), preferred_element_type=F32)


def _hg_step(st, zq, zf, zi, zg, ll0, ll1, oml, onorm, sel, selt, later, pair):
    n = zq.shape[0]
    levels = n.bit_length() - 1
    q = zq * _sigmoid(zq)
    g, k = _hg_gates(zf, ll0, ll1, oml)
    sums = _sel_dot(sel, selt, g)
    b = sums[:n]
    tot = jnp.sum(g, axis=0, keepdims=True)
    o = lax.dot_general((q * jnp.exp(b)).astype(BF16), st.astype(BF16), (NT, ((), ())), preferred_element_type=F32)
    a = jnp.zeros((n, n), F32)
    for s in range(levels):
        lt = later[s]
        fs = 1.0 - lt
        rel = b - sums[(s + 1) * n:(s + 2) * n]
        ql = lt * q * jnp.exp(lt * rel)
        kl = fs * k * jnp.exp(-fs * rel)
        al = lax.dot_general(ql.astype(BF16), kl.astype(BF16), (NT, ((), ())), preferred_element_type=F32)
        a = a + pair[s] * al
    o = o + jnp.dot(a.astype(BF16), zi.astype(BF16), preferred_element_type=F32)
    o = o + jnp.sum(q * k, axis=1, keepdims=True) * zi
    kd = k * jnp.exp(tot - b)
    st_new = st * jnp.exp(tot) + jnp.dot(zi.T.astype(BF16), kd.astype(BF16), preferred_element_type=F32)
    og = _rms(o, onorm) * (zg * _sigmoid(zg))
    return og, st_new


def _whole(arr, n_grid):
    zeros = (0,) * arr.ndim
    return pl.BlockSpec(arr.shape, (lambda h, n: zeros) if n_grid == 2 else (lambda i: zeros))


def _hg_dims(proj4, n_seq):
    _, T, D = proj4.shape
    S = T // n_seq
    hp = HG_HEADS_PER if (D // LANES) % HG_HEADS_PER == 0 else 1
    tb = min(HG_BLOCK, S)
    streams = [(b, hl) for b in range(n_seq) for hl in range(hp)]
    return T, D, S, hp, D // (LANES * hp), LANES * hp, tb, S // tb, tb // HG_SUB, streams


def hgrn_fwd(name, proj4, ll0, ll1, oml, onorm, n_seq):
    T, D, S, hp, n_hg, W, tb, nblk, nsub, streams = _hg_dims(proj4, n_seq)
    ns = len(streams)

    def body(p_ref, ll0_ref, ll1_ref, oml_ref, on_ref, sel_ref, selt_ref, later_ref, pair_ref, og_ref, st_ref, st):
        @pl.when(pl.program_id(1) == 0)
        def _():
            st[...] = jnp.zeros(st.shape, F32)

        st_ref[...] = st[...]
        on = on_ref[...]

        def step(j, carry):
            r = pl.ds(pl.multiple_of(j * HG_SUB, HG_SUB), HG_SUB)
            consts = (sel_ref[...], selt_ref[...], later_ref[...], pair_ref[...])
            args = []
            for si, (b, hl) in enumerate(streams):
                ln = slice(hl * LANES, (hl + 1) * LANES)
                args.append((st[si], p_ref[0, b, r, ln], p_ref[1, b, r, ln], p_ref[2, b, r, ln], p_ref[3, b, r, ln],
                             ll0_ref[:, ln], ll1_ref[:, ln], oml_ref[:, ln], on) + consts)
            res = [_hg_step(*a) for a in args]
            for si, (b, hl) in enumerate(streams):
                og_ref[b, r, hl * LANES:(hl + 1) * LANES] = res[si][0].astype(og_ref.dtype)
                st[si] = res[si][1]
            return carry

        lax.fori_loop(0, nsub, step, 0)

    vec = pl.BlockSpec((1, W), lambda h, n: (0, h))
    consts = hg_constants(HG_SUB)
    og, states = pl.pallas_call(
        body, name=name, grid=(n_hg, nblk),
        in_specs=[pl.BlockSpec((4, n_seq, tb, W), lambda h, n: (0, 0, n, h)), vec, vec, vec,
                  pl.BlockSpec((1, LANES), lambda h, n: (0, 0))] + [_whole(c, 2) for c in consts],
        out_specs=[pl.BlockSpec((n_seq, tb, W), lambda h, n: (0, n, h)),
                   pl.BlockSpec((None, None, ns, LANES, LANES), lambda h, n: (h, n, 0, 0, 0))],
        out_shape=[jax.ShapeDtypeStruct((n_seq, S, D), BF16),
                   jax.ShapeDtypeStruct((n_hg, nblk, ns, LANES, LANES), F32)],
        scratch_shapes=[pltpu.VMEM((ns, LANES, LANES), F32)],
        compiler_params=_cp(2),
    )(proj4.reshape(4, n_seq, S, D), ll0, ll1, oml, onorm, *consts)
    return og.reshape(T, D), states


def hgrn_bwd(name, proj4, states, dog, ll0, ll1, oml, onorm, n_seq):
    T, D, S, hp, n_hg, W, tb, nblk, nsub, streams = _hg_dims(proj4, n_seq)
    ns = len(streams)

    def body(p_ref, st_ref, dog_ref, ll0_ref, ll1_ref, oml_ref, on_ref, sel_ref, selt_ref, later_ref, pair_ref,
             dp_ref, dll0_ref, dll1_ref, doml_ref, don_ref, sbuf, dst):
        n_id = pl.program_id(1)

        @pl.when(n_id == 0)
        def _():
            dst[...] = jnp.zeros(dst.shape, F32)
            for ref in (dll0_ref, dll1_ref, doml_ref):
                ref[...] = jnp.zeros(ref.shape, F32)

        @pl.when(jnp.logical_and(n_id == 0, pl.program_id(0) == 0))
        def _():
            don_ref[...] = jnp.zeros(don_ref.shape, F32)

        on = on_ref[...]

        def fwd(j, carry):
            r = pl.ds(pl.multiple_of(j * HG_SUB, HG_SUB), HG_SUB)
            tri = sel_ref[0:HG_SUB, :]
            args = []
            for si, (b, hl) in enumerate(streams):
                ln = slice(hl * LANES, (hl + 1) * LANES)
                args.append((carry[si], p_ref[1, b, r, ln], p_ref[2, b, r, ln],
                             ll0_ref[:, ln], ll1_ref[:, ln], oml_ref[:, ln], tri))
            for si in range(ns):
                sbuf[si, j] = carry[si]
            return tuple(_hg_state(*a) for a in args)

        lax.fori_loop(0, nsub, fwd, tuple(st_ref[si] for si in range(ns)))

        def bwd(jj, carry):
            j = nsub - 1 - jj
            r = pl.ds(pl.multiple_of(j * HG_SUB, HG_SUB), HG_SUB)
            args, cts = [], []
            for si, (b, hl) in enumerate(streams):
                ln = slice(hl * LANES, (hl + 1) * LANES)
                args.append((sbuf[si, j], p_ref[0, b, r, ln], p_ref[1, b, r, ln], p_ref[2, b, r, ln],
                             p_ref[3, b, r, ln], ll0_ref[:, ln], ll1_ref[:, ln], oml_ref[:, ln], on))
                cts.append((dog_ref[b, r, ln].astype(F32), dst[si]))
            consts = (sel_ref[...], selt_ref[...], later_ref[...], pair_ref[...])
            step_fn = lambda *a: _hg_step(*a, *consts)
            ds = [jax.vjp(step_fn, *a)[1](ct) for a, ct in zip(args, cts)]
            d_on = carry
            for si, (b, hl) in enumerate(streams):
                ln = slice(hl * LANES, (hl + 1) * LANES)
                d = ds[si]
                dst[si] = d[0]
                for part in range(4):
                    dp_ref[part, b, r, ln] = d[1 + part].astype(dp_ref.dtype)
                dll0_ref[:, ln] += d[5]
                dll1_ref[:, ln] += d[6]
                doml_ref[:, ln] += d[7]
                d_on = d_on + d[8]
            return d_on

        don_ref[...] += lax.fori_loop(0, nsub, bwd, jnp.zeros((1, LANES), F32))

    last = nblk - 1
    vec = pl.BlockSpec((1, W), lambda h, n: (0, h))
    one = pl.BlockSpec((1, LANES), lambda h, n: (0, 0))
    consts = hg_constants(HG_SUB)
    dproj, d0, d1, d2, d_on = pl.pallas_call(
        body, name=name, grid=(n_hg, nblk),
        in_specs=[pl.BlockSpec((4, n_seq, tb, W), lambda h, n: (0, 0, last - n, h)),
                  pl.BlockSpec((None, None, ns, LANES, LANES), lambda h, n: (h, last - n, 0, 0, 0)),
                  pl.BlockSpec((n_seq, tb, W), lambda h, n: (0, last - n, h)), vec, vec, vec, one]
        + [_whole(c, 2) for c in consts],
        out_specs=[pl.BlockSpec((4, n_seq, tb, W), lambda h, n: (0, 0, last - n, h)), vec, vec, vec, one],
        out_shape=[jax.ShapeDtypeStruct((4, n_seq, S, D), BF16)] + [jax.ShapeDtypeStruct((1, D), F32)] * 3
        + [jax.ShapeDtypeStruct((1, LANES), F32)],
        scratch_shapes=[pltpu.VMEM((ns, nsub, LANES, LANES), F32), pltpu.VMEM((ns, LANES, LANES), F32)],
        compiler_params=_cp(2),
    )(proj4.reshape(4, n_seq, S, D), states, dog.reshape(n_seq, S, D), ll0, ll1, oml, onorm, *consts)
    return dproj.reshape(4, T, D), d0, d1, d2, d_on


def _place():
    x, y, c = lax.axis_index("x"), lax.axis_index("y"), lax.axis_index("c")
    chips = [(1 - x, y), (x, 1 - y), (1 - x, 1 - y)]
    return x, y, c, chips


ANY = pl.BlockSpec(memory_space=pl.ANY)


def _comm_call(name, body, ins, out_shapes, sems, aliases=None):
    return pl.pallas_call(
        body, name=name, in_specs=[ANY] * len(ins), out_specs=[ANY] * len(out_shapes),
        out_shape=out_shapes, scratch_shapes=sems, input_output_aliases=aliases or {},
        compiler_params=pltpu.CompilerParams(has_side_effects=True),
    )(*ins)


def gather_weights(bufs):
    n = len(bufs)

    def body(*refs):
        dst = refs[n:2 * n]
        send1, recv1, send2, recv2 = refs[2 * n:]
        x, y, c, chips = _place()
        q = 2 * x + y
        sib = (x, y, 1 - c)

        def half(a, who):
            lh = dst[a].shape[1] // 2
            return pl.ds(who * lh, lh)

        first, passed = [], []
        for a in range(n):
            for j, (px, py) in enumerate(chips):
                mine = dst[a].at[q, half(a, c)]
                first.append(pltpu.make_async_remote_copy(
                    src_ref=mine, dst_ref=mine,
                    send_sem=send1.at[a, j], recv_sem=recv1.at[a, j], device_id=(px, py, c), device_id_type=MESH_ID))
        for cp in first:
            cp.start()
        for j, (px, py) in enumerate(chips):
            qj = 2 * px + py
            for a in range(n):
                got = dst[a].at[qj, half(a, c)]
                pltpu.make_async_remote_copy(src_ref=got, dst_ref=got, send_sem=send1.at[a, j], recv_sem=recv1.at[a, j],
                                             device_id=(px, py, c), device_id_type=MESH_ID).wait_recv()
                fw = pltpu.make_async_remote_copy(src_ref=got, dst_ref=got, send_sem=send2.at[a, j],
                                                  recv_sem=recv2.at[a, j], device_id=sib, device_id_type=MESH_ID)
                fw.start()
                passed.append(fw)
        for j, (px, py) in enumerate(chips):
            qj = 2 * px + py
            for a in range(n):
                theirs = dst[a].at[qj, half(a, 1 - c)]
                pltpu.make_async_remote_copy(src_ref=theirs, dst_ref=theirs, send_sem=send2.at[a, j],
                                             recv_sem=recv2.at[a, j], device_id=sib, device_id_type=MESH_ID).wait_recv()
        for cp in first + passed:
            cp.wait_send()

    outs = [jax.ShapeDtypeStruct(s.shape, s.dtype) for s in bufs]
    sems = [pltpu.SemaphoreType.DMA((n, 3))] * 4
    return _comm_call("gather_weights", body, bufs, outs, sems, aliases={a: a for a in range(n)})


def swap_layer_halves(grads):
    n = len(grads)

    def body(*refs):
        src, dst = refs[:n], refs[n:2 * n]
        send, recv = refs[2 * n:]
        x, y, c, _ = _place()
        cps = []
        for a in range(n):
            lh = src[a].shape[1] // 2
            cps.append(pltpu.make_async_remote_copy(
                src_ref=src[a].at[:, pl.ds((1 - c) * lh, lh)], dst_ref=dst[a], send_sem=send.at[a], recv_sem=recv.at[a],
                device_id=(x, y, 1 - c), device_id_type=MESH_ID))
        for cp in cps:
            cp.start()
        for cp in cps:
            cp.wait()

    outs = [jax.ShapeDtypeStruct((g.shape[0], g.shape[1] // 2) + g.shape[2:], g.dtype) for g in grads]
    return _comm_call("swap_layer_halves", body, grads, outs, [pltpu.SemaphoreType.DMA((n,))] * 2)


def scatter_to_chips(parts):
    n = len(parts)

    def body(*refs):
        src, dst = refs[:n], refs[n:2 * n]
        send, recv = refs[2 * n:]
        x, y, c, chips = _place()
        cps = []
        for a in range(n):
            for j, (px, py) in enumerate(chips):
                cps.append(pltpu.make_async_remote_copy(
                    src_ref=src[a].at[2 * px + py], dst_ref=dst[a].at[j], send_sem=send.at[a, j], recv_sem=recv.at[a, j],
                    device_id=(px, py, c), device_id_type=MESH_ID))
        for cp in cps:
            cp.start()
        for cp in cps:
            cp.wait()

    outs = [jax.ShapeDtypeStruct((3,) + p.shape[1:], p.dtype) for p in parts]
    sems = [pltpu.SemaphoreType.DMA((n, 3))] * 2
    return _comm_call("scatter_to_chips", body, parts, outs, sems)


def join_layer_halves(bufs):
    n = len(bufs)

    def body(*refs):
        dst = refs[n:2 * n]
        send, recv = refs[2 * n:]
        x, y, c, _ = _place()
        cps = []
        for a in range(n):
            lh = dst[a].shape[0] // 2
            mine = dst[a].at[pl.ds(c * lh, lh)]
            cps.append(pltpu.make_async_remote_copy(src_ref=mine, dst_ref=mine, send_sem=send.at[a], recv_sem=recv.at[a],
                                                    device_id=(x, y, 1 - c), device_id_type=MESH_ID))
        for cp in cps:
            cp.start()
        for a in range(n):
            lh = dst[a].shape[0] // 2
            theirs = dst[a].at[pl.ds((1 - c) * lh, lh)]
            pltpu.make_async_remote_copy(src_ref=theirs, dst_ref=theirs, send_sem=send.at[a], recv_sem=recv.at[a],
                                         device_id=(x, y, 1 - c), device_id_type=MESH_ID).wait_recv()
        for cp in cps:
            cp.wait_send()

    outs = [jax.ShapeDtypeStruct(b.shape, b.dtype) for b in bufs]
    sems = [pltpu.SemaphoreType.DMA((n,))] * 2
    return _comm_call("join_layer_halves", body, bufs, outs, sems, aliases={a: a for a in range(n)})


def share_with_all(packed):
    def body(src, dst, send, recv, local):
        x, y, c, _ = _place()
        me = 4 * x + 2 * y + c
        own = pltpu.make_async_copy(src, dst.at[me], local)
        own.start()
        cps = []
        for k in range(1, N_DEV):
            px, py, pc = x ^ (k >> 2), y ^ ((k >> 1) & 1), c ^ (k & 1)
            cps.append(pltpu.make_async_remote_copy(src_ref=src, dst_ref=dst.at[me], send_sem=send.at[k - 1],
                                                    recv_sem=recv.at[k - 1], device_id=(px, py, pc), device_id_type=MESH_ID))
        for cp in cps:
            cp.start()
        for k in range(1, N_DEV):
            px, py, pc = x ^ (k >> 2), y ^ ((k >> 1) & 1), c ^ (k & 1)
            got = dst.at[4 * px + 2 * py + pc]
            pltpu.make_async_remote_copy(src_ref=got, dst_ref=got, send_sem=send.at[k - 1], recv_sem=recv.at[k - 1],
                                         device_id=(px, py, pc), device_id_type=MESH_ID).wait_recv()
        for cp in cps:
            cp.wait_send()
        own.wait()

    outs = [jax.ShapeDtypeStruct((N_DEV,) + packed.shape, packed.dtype)]
    sems = [pltpu.SemaphoreType.DMA((N_DEV - 1,))] * 2 + [pltpu.SemaphoreType.DMA(())]
    return _comm_call("share_with_all", body, [packed], outs, sems)[0]


def _w_tiles(R, C):
    tr = _tile(R, max(8, (1 << 20) // (4 * C) // 8 * 8), 8)
    return tr


def cast_bf16(w, q_arr):
    L, R, C = w.shape
    tr = _w_tiles(R, C)
    ins = [(w, (None, tr, C), lambda l, r, q: (l, r, 0))]
    outs = [((N_CHIPS,) + w.shape, BF16, (None, None, tr, C), lambda l, r, q: (q[0], l, r, 0), None)]
    return bmap_fwd("cast_bf16", lambda a: (a,), (L, R // tr), ins, outs, scalars=(q_arr,))[0]


def add_sibling_half(g, b1, c_arr):
    Q, L, R, C = g.shape
    lh = L // 2
    tr = _w_tiles(R, C)
    blk = (None, None, tr, C)
    ins = [(g, blk, lambda q, l, r, c: (q, c[0] * lh + l, r, 0)), (b1, blk, lambda q, l, r, c: (q, l, r, 0))]
    outs = [(b1.shape, BF16, blk, lambda q, l, r, c: (q, l, r, 0), None)]
    return bmap_fwd("add_sibling_half", lambda a, b: (a + b,), (Q, lh, R // tr), ins, outs, scalars=(c_arr,))[0]


def sum_chips(pair, b2, q_arr, c_arr):
    _, lh, R, C = b2.shape
    tr = _w_tiles(R, C)
    blk = (None, None, tr, C)
    ins = [(pair, blk, lambda l, r, q, c: (q[0], l, r, 0))]
    ins += [(b2, blk, (lambda l, r, q, c, jj=jj: (jj, l, r, 0))) for jj in range(3)]
    outs = [((2 * lh, R, C), F32, (None, tr, C), lambda l, r, q, c: (c[0] * lh + l, r, 0), None)]
    return bmap_fwd("sum_chips", lambda a, b, c, d: (((a + b) + c) + d,), (lh, R // tr), ins, outs,
                    scalars=(q_arr, c_arr))[0]


def sum_devices(slots):
    nd, NR, C = slots.shape
    tr = _tile(NR, 512, 8)
    ins = [(slots, (None, tr, C), (lambda r, dd=dd: (dd, r, 0))) for dd in range(nd)]
    outs = [((NR, C), F32, (tr, C), lambda r: (r, 0), None)]
    return bmap_fwd("sum_devices", lambda *a: (functools.reduce(lambda u, v: u + v, a),), (NR // tr,), ins, outs)[0]


def adamw(name, w, g, m, v):
    if w.ndim == 2:
        R, C = w.shape
        tr = _w_tiles(R, C)
        spec = ((tr, C), lambda r: (r, 0))
        grid = (R // tr,)
    else:
        L, R, C = w.shape
        tr = _w_tiles(R, C)
        spec = ((None, tr, C), lambda l, r: (l, r, 0))
        grid = (L, R // tr)
    ins = [(a,) + spec for a in (w, g, m, v)]
    outs = [(w.shape, F32) + spec + (None,)] * 3
    return bmap_fwd(name, f_adam, grid, ins, outs)


def loss_and_grad(h, target):
    T, D = h.shape
    tm = _tile(T, 256, 8)

    def fn(hv, tv):
        d = hv - tv
        return jnp.sum(d * d, keepdims=True).reshape(1, 1) * (0.5 / D), d * (1.0 / D)

    ins = [(h, (tm, D), lambda i: (i, 0)), (target, (tm, D), lambda i: (i, 0))]
    outs = [((1, 1), F32, (1, 1), lambda i: (0, 0), 0), ((T, D), F32, (tm, D), lambda i: (i, 0), None)]
    return bmap_fwd("loss_and_grad", fn, (T // tm,), ins, outs)


BIG = ("hg_w_in", "hg_w_out", "gm_w_in", "gm_w_out", "ffn_w_gate", "ffn_w_up", "ffn_w_down", "ple_w_proj", "ple_w_gate")
KIND = {"hg_w_in": "col", "hg_w_out": "row", "gm_w_in": "col", "gm_w_out": "row", "ffn_w_gate": "col",
        "ffn_w_up": "col", "ffn_w_down": "row", "ple_w_proj": "col", "ple_w_gate": "row"}
SMALL = ("hg_lb_logits", "hg_out_norm", "gm_ln_g", "gm_ln_b", "gm_w_s", "gm_b_s", "norm_mix_pre", "norm_mix_post",
         "norm_ffn_pre", "norm_ffn_post", "ple_norm")
WEIGHTS = ("hg_w_in", "hg_lb_logits", "hg_out_norm", "hg_w_out", "gm_w_in", "gm_ln_g", "gm_ln_b", "gm_w_s", "gm_b_s",
           "gm_w_out", "norm_mix_pre", "norm_mix_post", "norm_ffn_pre", "norm_ffn_post", "ffn_w_gate", "ffn_w_up",
           "ffn_w_down", "ple_w_proj", "ple_w_gate", "ple_norm")


def _pack(arrs):
    rows = []
    for a in arrs:
        flat = a.reshape(-1)
        pad = (-flat.shape[0]) % (8 * LANES)
        rows.append(jnp.pad(flat, (0, pad)).reshape(-1, LANES))
    return jnp.concatenate(rows, axis=0)


def _unpack(packed, shapes):
    out, r = [], 0
    for s in shapes:
        size = 1
        for d in s:
            size *= d
        nr = -(-size // (8 * LANES)) * 8
        out.append(packed[r:r + nr].reshape(-1)[:size].reshape(s))
        r += nr
    return out


def _step(x, p, W, M, V, loss_target):
    n_seq, S, D = x.shape
    T = n_seq * S
    depth = p.shape[0]
    n_hg = W["hg_w_in"].shape[0]
    x2 = x.reshape(T, D)
    p3 = p.reshape(depth, T, p.shape[-1])
    tgt = loss_target.reshape(T, D)
    xi, yi, ci = lax.axis_index("x"), lax.axis_index("y"), lax.axis_index("c")
    q_me = 2 * xi + yi
    c_arr = jnp.reshape(ci, (1,)).astype(jnp.int32)
    q_arr = jnp.reshape(q_me, (1,)).astype(jnp.int32)

    G = dict(zip(BIG, gather_weights([cast_bf16(W[k], q_arr) for k in BIG])))
    DW = {k: lax.empty(G[k].shape, BF16) for k in BIG}
    ln_full = share_with_all(_pack([W["gm_ln_g"], W["gm_ln_b"]]))
    n_gm, dq = W["gm_ln_g"].shape
    ln_parts = [_unpack(ln_full[4 * qx + 2 * qy + 0], [(n_gm, dq), (n_gm, dq)]) for qx in range(2) for qy in range(2)]
    ln_g = jnp.concatenate([lp[0] for lp in ln_parts], axis=1)
    ln_b = jnp.concatenate([lp[1] for lp in ln_parts], axis=1)

    row = lambda a, i: a[i][None, :]
    f_lb = _make_f_lb(n_hg)
    lb_rows = [row(W["hg_lb_logits"], j) for j in range(n_hg)]
    one = (1, D)
    lb_ins = [(r, one, lambda i: (0, 0)) for r in lb_rows]
    lb_out = bmap_fwd("hg_lower_bounds", f_lb, (1,), lb_ins, [(one, F32, one, lambda i: (0, 0), None)] * (3 * n_hg))

    saved = []
    h = x2
    a = rows_fwd("prenorm", f_prenorm, [h], [row(W["norm_mix_pre"], 0)], [BF16])[0]
    for i in range(depth):
        j = i // 2
        sv = {"h": h, "a": a}
        if i % 2 == 0:
            proj4 = mm_fwd("hg_in", a, G["hg_w_in"], j, "col", parts=True)
            lbp = lb_out[3 * j:3 * j + 3]
            onorm = row(W["hg_out_norm"], j)
            og, states = hgrn_fwd("hgrn_fwd", proj4, *lbp, onorm, n_seq)
            m = mm_fwd("hg_out", og, G["hg_w_out"], j, "row")
            sv.update(proj4=proj4, states=states, og=og, lbp=lbp, onorm=onorm)
        else:
            z = mm_fwd("gm_in", a, G["gm_w_in"], j, "col")
            lg, lb_ = row(ln_g, j), row(ln_b, j)
            u, vn = rows_fwd("gm_gelu_ln", f_gm_in, [z], [lg, lb_], [F32, BF16], tm=128)
            ws = W["gm_w_s"][j]
            bs = W["gm_b_s"][j][:, :, None]
            gb = min(GM_BLOCK, S)
            sp_grid = (D // LANES, T // gb)
            sp_ins = [(u, (gb, LANES), lambda g, n: (n, g)), (vn, (gb, LANES), lambda g, n: (n, g)),
                      (ws, (None, GM_CHUNK, GM_CHUNK), lambda g, n: (g, 0, 0)),
                      (bs, (None, GM_CHUNK, 1), lambda g, n: (g, 0, 0))]
            y = bmap_fwd("gm_spatial", f_gm_spatial, sp_grid, sp_ins,
                         [((T, D), BF16, (gb, LANES), lambda g, n: (n, g), None)])[0]
            m = mm_fwd("gm_out", y, G["gm_w_out"], j, "row")
            sv.update(z=z, lg=lg, lb_=lb_, sp_ins=sp_ins, sp_grid=sp_grid, y=y)
        g_post, g_fpre = row(W["norm_mix_post"], i), row(W["norm_ffn_pre"], i)
        h1, fin = rows_fwd("mix_post_ffn_pre", f_post_pre, [h, m], [g_post, g_fpre], [F32, BF16])
        gate = mm_fwd("ffn_gate", fin, G["ffn_w_gate"], i, "col")
        up = mm_fwd("ffn_up", fin, G["ffn_w_up"], i, "col")
        act = rows_fwd("swiglu", f_swiglu, [gate, up], [], [BF16], tm=128)[0]
        f = mm_fwd("ffn_down", act, G["ffn_w_down"], i, "row")
        g_fpost = row(W["norm_ffn_post"], i)
        h2 = rows_fwd("ffn_post", f_post, [h1, f], [g_fpost], [F32])[0]
        e = mm_fwd("ple_proj", p3, G["ple_w_proj"], i, "col", xl=i)
        zg = mm_fwd("ple_gate", h2, G["ple_w_gate"], i, "row")
        g_ple = row(W["ple_norm"], i)
        sv.update(m=m, h1=h1, fin=fin, gate=gate, up=up, act=act, f=f, h2=h2, e=e, zg=zg,
                  g_post=g_post, g_fpre=g_fpre, g_fpost=g_fpost, g_ple=g_ple)
        if i + 1 < depth:
            g_next = row(W["norm_mix_pre"], i + 1)
            h, a = rows_fwd("ple_next_pre", f_ple_pre, [h2, e, zg], [g_ple, g_next], [F32, BF16])
            sv["g_next"] = g_next
        else:
            h = rows_fwd("ple_last", f_ple, [h2, e, zg], [g_ple], [F32])[0]
        saved.append(sv)

    loss_part, dh = loss_and_grad(h, tgt)
    loss = lax.psum(loss_part[0, 0], ("x", "y", "c"))

    sg = {k: [None] * W[k].shape[0] for k in ("norm_mix_pre", "norm_mix_post", "norm_ffn_pre", "norm_ffn_post", "ple_norm",
                                              "hg_out_norm", "gm_ln_g", "gm_ln_b", "gm_w_s", "gm_b_s")}
    d_lbp = [None] * (3 * n_hg)
    da_next = None
    for i in reversed(range(depth)):
        j = i // 2
        sv = saved[i]
        if i + 1 < depth:
            dh2, de, dzg, d_gple, d_gnext = rows_bwd("ple_next_pre_bwd", f_ple_pre, [sv["h2"], sv["e"], sv["zg"]],
                                                     [sv["g_ple"], sv["g_next"]], [dh, da_next], [F32, BF16, BF16])
            sg["norm_mix_pre"][i + 1] = d_gnext
        else:
            dh2, de, dzg, d_gple = rows_bwd("ple_last_bwd", f_ple, [sv["h2"], sv["e"], sv["zg"]], [sv["g_ple"]], [dh],
                                            [F32, BF16, BF16])
        sg["ple_norm"][i] = d_gple
        DW["ple_w_proj"] = mm_bwd_w("ple_proj_dw", p3, de, DW["ple_w_proj"], i, "col", xl=i)
        DW["ple_w_gate"] = mm_bwd_w("ple_gate_dw", sv["h2"], dzg, DW["ple_w_gate"], i, "row")
        dh2 = mm_bwd_x("ple_gate_dx", dzg, G["ple_w_gate"], i, "row", addend=dh2)
        dh1, df, d_gfpost = rows_bwd("ffn_post_bwd", f_post, [sv["h1"], sv["f"]], [sv["g_fpost"]], [dh2], [F32, BF16])
        sg["norm_ffn_post"][i] = d_gfpost
        dact = mm_bwd_x("ffn_down_dx", df, G["ffn_w_down"], i, "row")
        DW["ffn_w_down"] = mm_bwd_w("ffn_down_dw", sv["act"], df, DW["ffn_w_down"], i, "row")
        dgate, dup = rows_bwd("swiglu_bwd", f_swiglu, [sv["gate"], sv["up"]], [], [dact], [BF16, BF16], tm=128)
        dfin = mm_bwd_x("ffn_gate_dx", dgate, G["ffn_w_gate"], i, "col")
        dfin = mm_bwd_x("ffn_up_dx", dup, G["ffn_w_up"], i, "col", addend=dfin)
        DW["ffn_w_gate"] = mm_bwd_w("ffn_gate_dw", sv["fin"], dgate, DW["ffn_w_gate"], i, "col")
        DW["ffn_w_up"] = mm_bwd_w("ffn_up_dw", sv["fin"], dup, DW["ffn_w_up"], i, "col")
        dh, dm, d_gpost, d_gfpre = rows_bwd("mix_post_ffn_pre_bwd", f_post_pre, [sv["h"], sv["m"]],
                                            [sv["g_post"], sv["g_fpre"]], [dh1, dfin], [F32, BF16])
        sg["norm_mix_post"][i], sg["norm_ffn_pre"][i] = d_gpost, d_gfpre
        if i % 2 == 0:
            dog = mm_bwd_x("hg_out_dx", dm, G["hg_w_out"], j, "row")
            DW["hg_w_out"] = mm_bwd_w("hg_out_dw", sv["og"], dm, DW["hg_w_out"], j, "row")
            dproj4, d0, d1, d2, d_on = hgrn_bwd("hgrn_bwd", sv["proj4"], sv["states"], dog, *sv["lbp"], sv["onorm"], n_seq)
            d_lbp[3 * j:3 * j + 3] = [d0, d1, d2]
            sg["hg_out_norm"][j] = d_on
            da_next = mm_bwd_x("hg_in_dx", dproj4, G["hg_w_in"], j, "col", parts=True)
            DW["hg_w_in"] = mm_bwd_w("hg_in_dw", sv["a"], dproj4, DW["hg_w_in"], j, "col", parts=True)
        else:
            dy = mm_bwd_x("gm_out_dx", dm, G["gm_w_out"], j, "row")
            DW["gm_w_out"] = mm_bwd_w("gm_out_dw", sv["y"], dm, DW["gm_w_out"], j, "row")
            gb = sv["sp_ins"][0][1][0]
            du, dvn, dws, dbs = bmap_bwd("gm_spatial_bwd", f_gm_spatial, sv["sp_grid"], sv["sp_ins"],
                                         [(dy, (gb, LANES), lambda g, n: (n, g))],
                                         [(0, F32, None), (1, F32, None), (2, F32, 1), (3, F32, 1)])
            sg["gm_w_s"][j], sg["gm_b_s"][j] = dws, dbs[:, :, 0]
            dz, d_lg, d_lb = rows_bwd("gm_gelu_ln_bwd", f_gm_in, [sv["z"]], [sv["lg"], sv["lb_"]], [du, dvn], [BF16], tm=128)
            sg["gm_ln_g"][j], sg["gm_ln_b"][j] = d_lg, d_lb
            da_next = mm_bwd_x("gm_in_dx", dz, G["gm_w_in"], j, "col")
            DW["gm_w_in"] = mm_bwd_w("gm_in_dw", sv["a"], dz, DW["gm_w_in"], j, "col")
    g0 = row(W["norm_mix_pre"], 0)
    grad_x, d_g0 = rows_bwd("prenorm_bwd", f_prenorm_thru, [saved[0]["h"]], [g0], [da_next, dh], [F32])
    sg["norm_mix_pre"][0] = d_g0
    d_logits = bmap_bwd("hg_lower_bounds_bwd", f_lb, (1,), lb_ins, [(d, one, lambda i: (0, 0)) for d in d_lbp],
                        [(jj, F32, None) for jj in range(n_hg)])

    small_g = {k: jnp.stack([v.reshape(W[k].shape[1:] if k not in ("gm_ln_g", "gm_ln_b") else (D,)) for v in sg[k]])
               for k in sg}
    small_g["hg_lb_logits"] = jnp.concatenate(d_logits, axis=0)
    small_shapes = [small_g[k].shape for k in SMALL]
    red = sum_devices(share_with_all(_pack([small_g[k] for k in SMALL])))
    small_red = dict(zip(SMALL, _unpack(red, small_shapes)))
    for k in ("gm_ln_g", "gm_ln_b"):
        small_red[k] = lax.dynamic_slice_in_dim(small_red[k], q_me * dq, dq, axis=1)
    pk = lambda d: _pack([d[k] for k in SMALL])
    s_delta, s_m, s_v = adamw("adamw_small", pk(W), pk(small_red), pk(M), pk(V))
    shard_shapes = [W[k].shape for k in SMALL]
    out_g, out_d, out_m, out_v = dict(small_red), {}, {}, {}
    for dct, packed in ((out_d, s_delta), (out_m, s_m), (out_v, s_v)):
        dct.update(zip(SMALL, _unpack(packed, shard_shapes)))

    b1 = swap_layer_halves([DW[k] for k in BIG])
    pair = [add_sibling_half(DW[k], b, c_arr) for k, b in zip(BIG, b1)]
    b2 = scatter_to_chips(pair)
    full = join_layer_halves([sum_chips(pr, b, q_arr, c_arr) for pr, b in zip(pair, b2)])
    for k, g in zip(BIG, full):
        out_g[k] = g
        out_d[k], out_m[k], out_v[k] = adamw("adamw_" + k, W[k], g, M[k], V[k])

    outs = [loss, grad_x.reshape(x.shape)]
    for dct in (out_g, out_d, out_m, out_v):
        outs += [dct[k] for k in WEIGHTS]
    return tuple(outs)


def kernel(x, p, hg_w_in, hg_lb_logits, hg_out_norm, hg_w_out, gm_w_in, gm_ln_g, gm_ln_b, gm_w_s, gm_b_s, gm_w_out, norm_mix_pre, norm_mix_post, norm_ffn_pre, norm_ffn_post, ffn_w_gate, ffn_w_up, ffn_w_down, ple_w_proj, ple_w_gate, ple_norm, loss_target, m_hg_w_in, m_hg_lb_logits, m_hg_out_norm, m_hg_w_out, m_gm_w_in, m_gm_ln_g, m_gm_ln_b, m_gm_w_s, m_gm_b_s, m_gm_w_out, m_norm_mix_pre, m_norm_mix_post, m_norm_ffn_pre, m_norm_ffn_post, m_ffn_w_gate, m_ffn_w_up, m_ffn_w_down, m_ple_w_proj, m_ple_w_gate, m_ple_norm, v_hg_w_in, v_hg_lb_logits, v_hg_out_norm, v_hg_w_out, v_gm_w_in, v_gm_ln_g, v_gm_ln_b, v_gm_w_s, v_gm_b_s, v_gm_w_out, v_norm_mix_pre, v_norm_mix_post, v_norm_ffn_pre, v_norm_ffn_post, v_ffn_w_gate, v_ffn_w_up, v_ffn_w_down, v_ple_w_proj, v_ple_w_gate, v_ple_norm):
    W = dict(zip(WEIGHTS, (hg_w_in, hg_lb_logits, hg_out_norm, hg_w_out, gm_w_in, gm_ln_g, gm_ln_b, gm_w_s, gm_b_s, gm_w_out,
                           norm_mix_pre, norm_mix_post, norm_ffn_pre, norm_ffn_post, ffn_w_gate, ffn_w_up, ffn_w_down,
                           ple_w_proj, ple_w_gate, ple_norm)))
    M = dict(zip(WEIGHTS, (m_hg_w_in, m_hg_lb_logits, m_hg_out_norm, m_hg_w_out, m_gm_w_in, m_gm_ln_g, m_gm_ln_b, m_gm_w_s,
                           m_gm_b_s, m_gm_w_out, m_norm_mix_pre, m_norm_mix_post, m_norm_ffn_pre, m_norm_ffn_post,
                           m_ffn_w_gate, m_ffn_w_up, m_ffn_w_down, m_ple_w_proj, m_ple_w_gate, m_ple_norm)))
    V = dict(zip(WEIGHTS, (v_hg_w_in, v_hg_lb_logits, v_hg_out_norm, v_hg_w_out, v_gm_w_in, v_gm_ln_g, v_gm_ln_b, v_gm_w_s,
                           v_gm_b_s, v_gm_w_out, v_norm_mix_pre, v_norm_mix_post, v_norm_ffn_pre, v_norm_ffn_post,
                           v_ffn_w_gate, v_ffn_w_up, v_ffn_w_down, v_ple_w_proj, v_ple_w_gate, v_ple_norm)))
    return _step(x, p, W, M, V, loss_target)
```

```python
import functools

import jax
import jax.numpy as jnp
from jax import lax
from jax.experimental import pallas as pl
from jax.experimental.pallas import tpu as pltpu

F32 = jnp.float32
BF16 = jnp.bfloat16
MESH_ID = pl.DeviceIdType.MESH

LANES = 128
N_CHIPS = 4
N_DEV = 8
VMEM_LIMIT = 56 * 1024 * 1024
HG_SUB = 64
HG_BLOCK = 256
HG_HEADS_PER = 2
GM_CHUNK = 128
GM_BLOCK = 512
PACK_ROWS = 512
LB_FLOOR = 1e-30
EPS = 1e-6
ADAM_LR, ADAM_B1, ADAM_B2, ADAM_EPS, ADAM_WD, ADAM_STEP = 0.001, 0.9, 0.999, 1e-08, 0.01, 10


def _tile(n, pref, mult=LANES):
    if n <= pref:
        return n
    t = (pref // mult) * mult
    while t >= mult:
        if n % t == 0:
            return t
        t -= mult
    return n


def _cp(n_axes):
    return pltpu.CompilerParams(dimension_semantics=("arbitrary",) * n_axes, vmem_limit_bytes=VMEM_LIMIT)


def _dense(block):
    return tuple(b for b in block if b is not None)


def _bmap(name, grid, ins, outs, compute, scalars=(), into=None):
    n_s, n_in = len(scalars), len(ins)
    n_extra = 0 if into is None else 1

    def body(*refs):
        in_refs = refs[n_s:n_s + n_in]
        out_refs = refs[n_s + n_in + n_extra:]
        vals = [r[...] for r in in_refs]
        res = compute(*vals)
        for r, o, spec in zip(out_refs, res, outs):
            keep = spec[4]
            if keep is None:
                r[...] = o.astype(r.dtype)
            else:
                first = functools.reduce(jnp.logical_and, [pl.program_id(a) == 0 for a in range(keep, len(grid))])

                @pl.when(first)
                def _():
                    r[...] = jnp.zeros(r.shape, r.dtype)

                r[...] += o.astype(r.dtype)

    grid_spec = pltpu.PrefetchScalarGridSpec(
        num_scalar_prefetch=n_s, grid=grid,
        in_specs=[pl.BlockSpec(b, m) for _, b, m in ins] + [pl.BlockSpec(memory_space=pl.ANY)] * n_extra,
        out_specs=[pl.BlockSpec(o[2], o[3]) for o in outs])
    return pl.pallas_call(
        body, name=name, grid_spec=grid_spec,
        out_shape=[jax.ShapeDtypeStruct(o[0], o[1]) for o in outs],
        input_output_aliases={n_s + n_in: 0} if n_extra else {},
        compiler_params=_cp(len(grid)),
    )(*scalars, *[a for a, _, _ in ins], *([into] if n_extra else []))


def bmap_fwd(name, fn, grid, ins, outs, scalars=(), into=None):
    return _bmap(name, grid, ins, outs, lambda *v: fn(*[x.astype(F32) for x in v]), scalars, into)


def bmap_bwd(name, fn, grid, ins, cots, grads, scalars=()):
    n_in = len(ins)
    diff = [g[0] for g in grads]
    cot_ins = [c for c in cots if c is not None]

    def compute(*vals):
        xs = [v.astype(F32) for v in vals[:n_in]]
        cvals = list(vals[n_in:])

        def f(*d):
            full = list(xs)
            for i, dv in zip(diff, d):
                full[i] = dv
            return tuple(fn(*full))

        res, pull = jax.vjp(f, *[xs[i] for i in diff])
        cts = []
        for r, c in zip(res, cots):
            cts.append(jnp.zeros_like(r) if c is None else cvals.pop(0).astype(F32))
        return pull(tuple(cts))

    outs = [(ins[i][0].shape, dt, ins[i][1], ins[i][2], keep) for i, dt, keep in grads]
    return _bmap(name, grid, list(ins) + cot_ins, outs, compute, scalars)


def _mm(name, a, b, out_shape, out_dtype, grid, a_spec, b_spec, o_spec, dims, addend=None, alias_out=None):
    nk = grid[2]
    o_dense = _dense(o_spec[0])
    has_add = addend is not None
    has_alias = alias_out is not None

    def body(*refs):
        a_ref, b_ref = refs[0], refs[1]
        pos = 2
        c_ref = None
        if has_add:
            c_ref = refs[pos]
            pos += 1
        if has_alias:
            pos += 1
        o_ref = refs[pos]
        acc_ref = refs[pos + 1] if nk > 1 else None
        p = lax.dot_general(a_ref[...].astype(BF16), b_ref[...].astype(BF16), (dims, ((), ())),
                            preferred_element_type=F32)

        def finish(total):
            if has_add:
                total = total + c_ref[...].astype(F32)
            o_ref[...] = total.astype(o_ref.dtype)

        if nk == 1:
            finish(p)
        else:
            k = pl.program_id(2)

            @pl.when(k == 0)
            def _():
                acc_ref[...] = p

            @pl.when(jnp.logical_and(k > 0, k < nk - 1))
            def _():
                acc_ref[...] += p

            @pl.when(k == nk - 1)
            def _():
                finish(acc_ref[...] + p)

    in_specs = [pl.BlockSpec(*a_spec), pl.BlockSpec(*b_spec)]
    operands = [a, b]
    if has_add:
        in_specs.append(pl.BlockSpec(o_spec[0], o_spec[1]))
        operands.append(addend)
    aliases = {}
    if has_alias:
        in_specs.append(pl.BlockSpec(memory_space=pl.ANY))
        aliases = {len(operands): 0}
        operands.append(alias_out)
    return pl.pallas_call(
        body, name=name, grid=grid, in_specs=in_specs, out_specs=pl.BlockSpec(*o_spec),
        out_shape=jax.ShapeDtypeStruct(out_shape, out_dtype),
        scratch_shapes=[pltpu.VMEM(o_dense, F32)] if nk > 1 else [],
        input_output_aliases=aliases,
        compiler_params=pltpu.CompilerParams(dimension_semantics=("parallel", "parallel", "arbitrary"),
                                             vmem_limit_bytes=VMEM_LIMIT),
    )(*operands)


NN, NT, TN = ((1,), (0,)), ((1,), (1,)), ((0,), (0,))
TM = 512
TT = 1024
TN_PREF = 1408


def mm_fwd(name, x, wg, l, kind, out_dtype=F32, parts=False, xl=None):
    if isinstance(wg, dict):
        wg, l = wg[l], 0
    _, _, R, C = wg.shape
    T = x.shape[-2]
    tm = _tile(T, TM, 8)
    if kind == "col":
        tn = _tile(C, TN_PREF)
        npc = C // tn
        grid = (T // tm, N_CHIPS * npc, 1)
        a_blk = (tm, R) if xl is None else (None, tm, R)
        a_map = (lambda i, j, k: (i, 0)) if xl is None else (lambda i, j, k: (xl, i, 0))
        b_spec = ((None, None, R, tn), lambda i, j, k: (j // npc, l, 0, j % npc))
        if parts:
            out_shape = (N_CHIPS, T, C)
            o_spec = ((None, tm, tn), lambda i, j, k: (j // npc, i, j % npc))
        else:
            out_shape = (T, N_CHIPS * C)
            o_spec = ((tm, tn), lambda i, j, k: (i, j))
    else:
        tn = _tile(C, 1024)
        grid = (T // tm, C // tn, N_CHIPS)
        a_blk = (tm, R)
        a_map = lambda i, j, k: (i, k)
        b_spec = ((None, None, R, tn), lambda i, j, k: (k, l, 0, j))
        out_shape = (T, C)
        o_spec = ((tm, tn), lambda i, j, k: (i, j))
    return _mm(name, x, wg, out_shape, out_dtype, grid, (a_blk, a_map), b_spec, o_spec, NN)


def mm_bwd_x(name, dy, wg, l, kind, out_dtype=F32, parts=False, addend=None):
    if isinstance(wg, dict):
        wg, l = wg[l], 0
    _, _, R, C = wg.shape
    T = dy.shape[-2]
    tm = _tile(T, TM, 8)
    if kind == "col":
        tk = _tile(C, TN_PREF)
        npc = C // tk
        tno = _tile(R, 2048)
        grid = (T // tm, R // tno, N_CHIPS * npc)
        if parts:
            a_spec = ((None, tm, tk), lambda i, j, k: (k // npc, i, k % npc))
        else:
            a_spec = ((tm, tk), lambda i, j, k: (i, k))
        b_spec = ((None, None, tno, tk), lambda i, j, k: (k // npc, l, j, k % npc))
        out_shape = (T, R)
        o_spec = ((tm, tno), lambda i, j, k: (i, j))
    else:
        grid = (T // tm, N_CHIPS, 1)
        a_spec = ((tm, C), lambda i, j, k: (i, 0))
        b_spec = ((None, None, R, C), lambda i, j, k: (j, l, 0, 0))
        out_shape = (T, N_CHIPS * R)
        o_spec = ((tm, R), lambda i, j, k: (i, j))
    return _mm(name, dy, wg, out_shape, out_dtype, grid, a_spec, b_spec, o_spec, NT, addend=addend)


def mm_bwd_w(name, x, dy, dwg, l, kind, parts=False, xl=None):
    if isinstance(dwg, dict):
        return {**dwg, l: mm_bwd_w(name, x, dy, dwg[l], 0, kind, parts=parts, xl=xl)}
    _, _, R, C = dwg.shape
    T = dy.shape[-2]
    tt = _tile(T, TT, 16)
    nt = T // tt
    if kind == "col":
        tn = _tile(C, TN_PREF)
        npc = C // tn
        tr = _tile(R, 1024)
        grid = (R // tr, N_CHIPS * npc, nt)
        if xl is None:
            a_spec = ((tt, tr), lambda i, j, t: (t, i))
        else:
            a_spec = ((None, tt, tr), lambda i, j, t: (xl, t, i))
        if parts:
            b_spec = ((None, tt, tn), lambda i, j, t: (j // npc, t, j % npc))
        else:
            b_spec = ((tt, tn), lambda i, j, t: (t, j))
        o_spec = ((None, None, tr, tn), lambda i, j, t: (j // npc, l, i, j % npc))
    else:
        tn = _tile(C, 1024)
        grid = (N_CHIPS, C // tn, nt)
        a_spec = ((tt, R), lambda i, j, t: (t, i))
        b_spec = ((tt, tn), lambda i, j, t: (t, j))
        o_spec = ((None, None, R, tn), lambda i, j, t: (i, l, 0, j))
    return _mm(name, x, dy, dwg.shape, dwg.dtype, grid, a_spec, b_spec, o_spec, TN, alias_out=dwg)


def _sigmoid(x):
    return 0.5 * jnp.tanh(0.5 * x) + 0.5


def _rms(x, g):
    return x * lax.rsqrt(jnp.mean(x * x, axis=-1, keepdims=True) + EPS) * g


def f_prenorm(h, g):
    return (_rms(h, g),)


def f_prenorm_thru(h, g):
    return _rms(h, g), h


def f_post_pre(h, m, g_post, g_pre):
    h1 = h + _rms(m, g_post)
    return h1, _rms(h1, g_pre)


def f_swiglu(gate, up):
    return (gate * _sigmoid(gate) * up,)


def f_post(h1, f, g):
    return (h1 + _rms(f, g),)


def f_ple(h2, e, zg, g):
    return (h2 + _rms(e * _sigmoid(zg), g),)


def f_ple_pre(h2, e, zg, g, g_next):
    h3 = h2 + _rms(e * _sigmoid(zg), g)
    return h3, _rms(h3, g_next)


def _gelu(x):
    return 0.5 * x * (1.0 + lax.erf(x * 0.7071067811865476))


def f_gm_in(z, ln_g, ln_b):
    w = z.shape[-1] // 2
    u = _gelu(z[:, :w])
    v = _gelu(z[:, w:])
    mu = jnp.mean(v, axis=-1, keepdims=True)
    vc = v - mu
    vn = vc * lax.rsqrt(jnp.mean(vc * vc, axis=-1, keepdims=True) + EPS) * ln_g + ln_b
    return u, vn


def f_gm_spatial(u, vn, ws, bs):
    t = lax.broadcasted_iota(jnp.int32, ws.shape, 0)
    s = lax.broadcasted_iota(jnp.int32, ws.shape, 1)
    wm = jnp.where(t >= s, ws, 0.0).astype(BF16)
    ys = []
    for n in range(u.shape[0] // GM_CHUNK):
        rows = slice(n * GM_CHUNK, (n + 1) * GM_CHUNK)
        sv = jnp.dot(wm, vn[rows].astype(BF16), preferred_element_type=F32) + bs
        ys.append(u[rows] * sv)
    return (jnp.concatenate(ys, axis=0) if len(ys) > 1 else ys[0],)


def f_adam(w, g, m, v):
    m = ADAM_B1 * m + (1.0 - ADAM_B1) * g
    v = ADAM_B2 * v + (1.0 - ADAM_B2) * jnp.square(g)
    m_hat = m / (1.0 - ADAM_B1 ** ADAM_STEP)
    v_hat = v / (1.0 - ADAM_B2 ** ADAM_STEP)
    delta = -ADAM_LR * (m_hat / (jnp.sqrt(v_hat) + ADAM_EPS) + ADAM_WD * w)
    return delta, m, v


def _make_f_lb(n_layers):
    def f_lb(*logits):
        mx = functools.reduce(jnp.maximum, logits)
        ex = [jnp.exp(r - mx) for r in logits]
        tot = functools.reduce(lambda a, b: a + b, ex)
        sm = [e / tot for e in ex]
        outs = []
        run = jnp.zeros_like(sm[0])
        for j in range(n_layers):
            if j > 0:
                run = run + sm[j]
            lb = run
            outs += [jnp.log(jnp.maximum(lb, LB_FLOOR)), jnp.log(1.0 - lb), 1.0 - lb]
        return tuple(outs)
    return f_lb


def rows_fwd(name, fn, rows, params, out_dtypes, tm=256):
    T = rows[0].shape[0]
    tm = _tile(T, tm, 16)
    ins = [(r, (tm, r.shape[1]), lambda i: (i, 0)) for r in rows]
    ins += [(p, p.shape, lambda i: (0, 0)) for p in params]
    shapes = jax.eval_shape(lambda *a: fn(*a), *[jax.ShapeDtypeStruct((tm, r.shape[1]), F32) for r in rows],
                            *[jax.ShapeDtypeStruct(p.shape, F32) for p in params])
    outs = [((T, s.shape[1]), dt, (tm, s.shape[1]), lambda i: (i, 0), None) for s, dt in zip(shapes, out_dtypes)]
    return bmap_fwd(name, fn, (T // tm,), ins, outs)


def rows_bwd(name, fn, rows, params, cots, row_grad_dtypes, tm=256):
    T = rows[0].shape[0]
    tm = _tile(T, tm, 16)
    ins = [(r, (tm, r.shape[1]), lambda i: (i, 0)) for r in rows]
    ins += [(p, p.shape, lambda i: (0, 0)) for p in params]
    cts = [None if c is None else (c, (tm, c.shape[1]), lambda i: (i, 0)) for c in cots]
    grads = [(i, dt, None) for i, dt in enumerate(row_grad_dtypes) if dt is not None]
    grads += [(len(rows) + j, F32, 0) for j in range(len(params))]
    return bmap_bwd(name, fn, (T // tm,), ins, cts, grads)


def _log_sigmoid(z):
    return jnp.minimum(z, 0.0) - jnp.log(1.0 + jnp.exp(-jnp.abs(z)))


def _hg_gates(zf, ll0, ll1, oml):
    x2 = ll1 + _log_sigmoid(zf)
    mx = jnp.maximum(ll0, x2)
    g = mx + jnp.log(jnp.exp(ll0 - mx) + jnp.exp(x2 - mx))
    return g, oml * _sigmoid(-zf)


def hg_constants(n):
    levels = n.bit_length() - 1
    r = jnp.arange(n, dtype=jnp.int32)
    bounds = [r] + [((r >> (s + 1)) << (s + 1)) + ((1 << s) - 1) for s in range(levels)]
    sel = jnp.concatenate([(r[None, :] <= bd[:, None]) for bd in bounds], axis=0).astype(BF16)
    later = jnp.stack([((r >> s) & 1) for s in range(levels)]).astype(F32)
    later = jnp.broadcast_to(later[:, :, None], (levels, n, LANES))
    pair = jnp.stack([(r[:, None] >> (s + 1)) == (r[None, :] >> (s + 1)) for s in range(levels)]).astype(F32)
    return sel, sel.T, later, pair


def _dot2(m, x):
    hi = x.astype(BF16)
    lo = (x - hi.astype(F32)).astype(BF16)
    p = jnp.dot(m, jnp.concatenate([hi, lo], axis=1), preferred_element_type=F32)
    w = x.shape[1]
    return p[:, :w] + p[:, w:]


@jax.custom_vjp
def _sel_dot(sel, selt, g):
    return _dot2(sel, g)


def _sel_dot_fwd(sel, selt, g):
    return _dot2(sel, g), (sel, selt)


def _sel_dot_bwd(res, d):
    sel, selt = res
    return jnp.zeros_like(sel), jnp.zeros_like(selt), _dot2(selt, d)


_sel_dot.defvjp(_sel_dot_fwd, _sel_dot_bwd)


def _hg_state(st, zf, zi, ll0, ll1, oml, tri):
    g, k = _hg_gates(zf, ll0, ll1, oml)
    b = _dot2(tri, g)
    tot = jnp.sum(g, axis=0, keepdims=True)
    kd = k * jnp.exp(tot - b)
    return st * jnp.exp(tot) + jnp.dot(zi.T.astype(BF16), kd.astype(BF16), preferred_element_type=F32)


def _hg_step(st, zq, zf, zi, zg, ll0, ll1, oml, onorm, sel, selt, later, pair):
    n = zq.shape[0]
    levels = n.bit_length() - 1
    q = zq * _sigmoid(zq)
    g, k = _hg_gates(zf, ll0, ll1, oml)
    sums = _sel_dot(sel, selt, g)
    b = sums[:n]
    tot = jnp.sum(g, axis=0, keepdims=True)
    o = lax.dot_general((q * jnp.exp(b)).astype(BF16), st.astype(BF16), (NT, ((), ())), preferred_element_type=F32)
    a = jnp.zeros((n, n), F32)
    for s in range(levels):
        lt = later[s]
        fs = 1.0 - lt
        rel = b - sums[(s + 1) * n:(s + 2) * n]
        ql = lt * q * jnp.exp(lt * rel)
        kl = fs * k * jnp.exp(-fs * rel)
        al = lax.dot_general(ql.astype(BF16), kl.astype(BF16), (NT, ((), ())), preferred_element_type=F32)
        a = a + pair[s] * al
    o = o + jnp.dot(a.astype(BF16), zi.astype(BF16), preferred_element_type=F32)
    o = o + jnp.sum(q * k, axis=1, keepdims=True) * zi
    kd = k * jnp.exp(tot - b)
    st_new = st * jnp.exp(tot) + jnp.dot(zi.T.astype(BF16), kd.astype(BF16), preferred_element_type=F32)
    og = _rms(o, onorm) * (zg * _sigmoid(zg))
    return og, st_new


def _whole(arr, n_grid):
    zeros = (0,) * arr.ndim
    return pl.BlockSpec(arr.shape, (lambda h, n: zeros) if n_grid == 2 else (lambda i: zeros))


def _hg_dims(proj4, n_seq):
    _, T, D = proj4.shape
    S = T // n_seq
    hp = HG_HEADS_PER if (D // LANES) % HG_HEADS_PER == 0 else 1
    tb = min(HG_BLOCK, S)
    streams = [(b, hl) for b in range(n_seq) for hl in range(hp)]
    return T, D, S, hp, D // (LANES * hp), LANES * hp, tb, S // tb, tb // HG_SUB, streams


def hgrn_fwd(name, proj4, ll0, ll1, oml, onorm, n_seq):
    T, D, S, hp, n_hg, W, tb, nblk, nsub, streams = _hg_dims(proj4, n_seq)
    ns = len(streams)

    def body(p_ref, ll0_ref, ll1_ref, oml_ref, on_ref, sel_ref, selt_ref, later_ref, pair_ref, og_ref, st_ref, st):
        @pl.when(pl.program_id(1) == 0)
        def _():
            st[...] = jnp.zeros(st.shape, F32)

        st_ref[...] = st[...]
        on = on_ref[...]

        def step(j, carry):
            r = pl.ds(pl.multiple_of(j * HG_SUB, HG_SUB), HG_SUB)
            consts = (sel_ref[...], selt_ref[...], later_ref[...], pair_ref[...])
            args = []
            for si, (b, hl) in enumerate(streams):
                ln = slice(hl * LANES, (hl + 1) * LANES)
                args.append((st[si], p_ref[0, b, r, ln], p_ref[1, b, r, ln], p_ref[2, b, r, ln], p_ref[3, b, r, ln],
                             ll0_ref[:, ln], ll1_ref[:, ln], oml_ref[:, ln], on) + consts)
            res = [_hg_step(*a) for a in args]
            for si, (b, hl) in enumerate(streams):
                og_ref[b, r, hl * LANES:(hl + 1) * LANES] = res[si][0].astype(og_ref.dtype)
                st[si] = res[si][1]
            return carry

        lax.fori_loop(0, nsub, step, 0)

    vec = pl.BlockSpec((1, W), lambda h, n: (0, h))
    consts = hg_constants(HG_SUB)
    og, states = pl.pallas_call(
        body, name=name, grid=(n_hg, nblk),
        in_specs=[pl.BlockSpec((4, n_seq, tb, W), lambda h, n: (0, 0, n, h)), vec, vec, vec,
                  pl.BlockSpec((1, LANES), lambda h, n: (0, 0))] + [_whole(c, 2) for c in consts],
        out_specs=[pl.BlockSpec((n_seq, tb, W), lambda h, n: (0, n, h)),
                   pl.BlockSpec((None, None, ns, LANES, LANES), lambda h, n: (h, n, 0, 0, 0))],
        out_shape=[jax.ShapeDtypeStruct((n_seq, S, D), BF16),
                   jax.ShapeDtypeStruct((n_hg, nblk, ns, LANES, LANES), F32)],
        scratch_shapes=[pltpu.VMEM((ns, LANES, LANES), F32)],
        compiler_params=_cp(2),
    )(proj4.reshape(4, n_seq, S, D), ll0, ll1, oml, onorm, *consts)
    return og.reshape(T, D), states


def hgrn_bwd(name, proj4, states, dog, ll0, ll1, oml, onorm, n_seq):
    T, D, S, hp, n_hg, W, tb, nblk, nsub, streams = _hg_dims(proj4, n_seq)
    ns = len(streams)

    def body(p_ref, st_ref, dog_ref, ll0_ref, ll1_ref, oml_ref, on_ref, sel_ref, selt_ref, later_ref, pair_ref,
             dp_ref, dll0_ref, dll1_ref, doml_ref, don_ref, sbuf, dst):
        n_id = pl.program_id(1)

        @pl.when(n_id == 0)
        def _():
            dst[...] = jnp.zeros(dst.shape, F32)
            for ref in (dll0_ref, dll1_ref, doml_ref):
                ref[...] = jnp.zeros(ref.shape, F32)

        @pl.when(jnp.logical_and(n_id == 0, pl.program_id(0) == 0))
        def _():
            don_ref[...] = jnp.zeros(don_ref.shape, F32)

        on = on_ref[...]

        def fwd(j, carry):
            r = pl.ds(pl.multiple_of(j * HG_SUB, HG_SUB), HG_SUB)
            tri = sel_ref[0:HG_SUB, :]
            args = []
            for si, (b, hl) in enumerate(streams):
                ln = slice(hl * LANES, (hl + 1) * LANES)
                args.append((carry[si], p_ref[1, b, r, ln], p_ref[2, b, r, ln],
                             ll0_ref[:, ln], ll1_ref[:, ln], oml_ref[:, ln], tri))
            for si in range(ns):
                sbuf[si, j] = carry[si]
            return tuple(_hg_state(*a) for a in args)

        lax.fori_loop(0, nsub, fwd, tuple(st_ref[si] for si in range(ns)))

        def bwd(jj, carry):
            j = nsub - 1 - jj
            r = pl.ds(pl.multiple_of(j * HG_SUB, HG_SUB), HG_SUB)
            args, cts = [], []
            for si, (b, hl) in enumerate(streams):
                ln = slice(hl * LANES, (hl + 1) * LANES)
                args.append((sbuf[si, j], p_ref[0, b, r, ln], p_ref[1, b, r, ln], p_ref[2, b, r, ln],
                             p_ref[3, b, r, ln], ll0_ref[:, ln], ll1_ref[:, ln], oml_ref[:, ln], on))
                cts.append((dog_ref[b, r, ln].astype(F32), dst[si]))
            consts = (sel_ref[...], selt_ref[...], later_ref[...], pair_ref[...])
            step_fn = lambda *a: _hg_step(*a, *consts)
            ds = [jax.vjp(step_fn, *a)[1](ct) for a, ct in zip(args, cts)]
            d_on = carry
            for si, (b, hl) in enumerate(streams):
                ln = slice(hl * LANES, (hl + 1) * LANES)
                d = ds[si]
                dst[si] = d[0]
                for part in range(4):
                    dp_ref[part, b, r, ln] = d[1 + part].astype(dp_ref.dtype)
                dll0_ref[:, ln] += d[5]
                dll1_ref[:, ln] += d[6]
                doml_ref[:, ln] += d[7]
                d_on = d_on + d[8]
            return d_on

        don_ref[...] += lax.fori_loop(0, nsub, bwd, jnp.zeros((1, LANES), F32))

    last = nblk - 1
    vec = pl.BlockSpec((1, W), lambda h, n: (0, h))
    one = pl.BlockSpec((1, LANES), lambda h, n: (0, 0))
    consts = hg_constants(HG_SUB)
    dproj, d0, d1, d2, d_on = pl.pallas_call(
        body, name=name, grid=(n_hg, nblk),
        in_specs=[pl.BlockSpec((4, n_seq, tb, W), lambda h, n: (0, 0, last - n, h)),
                  pl.BlockSpec((None, None, ns, LANES, LANES), lambda h, n: (h, last - n, 0, 0, 0)),
                  pl.BlockSpec((n_seq, tb, W), lambda h, n: (0, last - n, h)), vec, vec, vec, one]
        + [_whole(c, 2) for c in consts],
        out_specs=[pl.BlockSpec((4, n_seq, tb, W), lambda h, n: (0, 0, last - n, h)), vec, vec, vec, one],
        out_shape=[jax.ShapeDtypeStruct((4, n_seq, S, D), BF16)] + [jax.ShapeDtypeStruct((1, D), F32)] * 3
        + [jax.ShapeDtypeStruct((1, LANES), F32)],
        scratch_shapes=[pltpu.VMEM((ns, nsub, LANES, LANES), F32), pltpu.VMEM((ns, LANES, LANES), F32)],
        compiler_params=_cp(2),
    )(proj4.reshape(4, n_seq, S, D), states, dog.reshape(n_seq, S, D), ll0, ll1, oml, onorm, *consts)
    return dproj.reshape(4, T, D), d0, d1, d2, d_on


def _place():
    x, y, c = lax.axis_index("x"), lax.axis_index("y"), lax.axis_index("c")
    chips = [(1 - x, y), (x, 1 - y), (1 - x, 1 - y)]
    return x, y, c, chips


ANY = pl.BlockSpec(memory_space=pl.ANY)


def _comm_call(name, body, ins, out_shapes, sems, aliases=None):
    return pl.pallas_call(
        body, name=name, in_specs=[ANY] * len(ins), out_specs=[ANY] * len(out_shapes),
        out_shape=out_shapes, scratch_shapes=sems, input_output_aliases=aliases or {},
        compiler_params=pltpu.CompilerParams(has_side_effects=True),
    )(*ins)


HBM_SPEC = pl.BlockSpec(memory_space=pltpu.HBM)
SEM_SPEC = pl.BlockSpec(memory_space=pltpu.SEMAPHORE)
SPLIT_EFFECT = pltpu.SideEffectType.DATAFLOW_SIDE_EFFECTING
N_PEER_CHIPS = 3


def split_start(name, build, arrays):
    n = len(arrays)

    def body(*refs):
        send, recv = refs[n], refs[n + 1]
        token = refs[2 * n + 2]
        starts, _ = build(refs[:n], send, recv)
        for cp in starts:
            cp.start()
        token[...] = jnp.zeros_like(token)

    res = pl.pallas_call(
        body, name=name,
        out_shape=(pltpu.SemaphoreType.DMA((N_PEER_CHIPS,)), pltpu.SemaphoreType.DMA((N_PEER_CHIPS,)),
                   *[pltpu.HBM(a.shape, a.dtype) for a in arrays], jax.ShapeDtypeStruct((8, LANES), F32)),
        in_specs=[HBM_SPEC] * n,
        out_specs=(SEM_SPEC, SEM_SPEC, *[HBM_SPEC] * n, pl.BlockSpec(memory_space=pltpu.VMEM)),
        input_output_aliases={i: 2 + i for i in range(n)},
        compiler_params=pltpu.CompilerParams(has_side_effects=SPLIT_EFFECT),
    )(*[pltpu.with_memory_space_constraint(a, pltpu.HBM) for a in arrays])
    return res[0], res[1], list(res[2:2 + n]), res[2 + n]


def split_wait(name, build, send, recv, arrays, after):
    n = len(arrays)

    def body(*refs):
        starts, arrivals = build(refs[:n], refs[n], refs[n + 1])
        for cp in starts:
            cp.wait_send()
        for cp in arrivals:
            cp.wait_recv()

    return list(pl.pallas_call(
        body, name=name, out_shape=tuple(pltpu.HBM(a.shape, a.dtype) for a in arrays),
        in_specs=[HBM_SPEC] * n + [SEM_SPEC, SEM_SPEC, ANY], out_specs=tuple([HBM_SPEC] * n),
        input_output_aliases={i: i for i in range(n)},
        compiler_params=pltpu.CompilerParams(has_side_effects=SPLIT_EFFECT),
    )(*arrays, send, recv, after))


def _row_half(ref, dim, who):
    rh = ref.shape[dim] // 2
    return pl.ds(who * rh, rh)


def gather_build(refs, send, recv):
    x, y, c, chips = _place()
    q = 2 * x + y
    starts, arrivals = [], []
    for buf in refs:
        rows = _row_half(buf, 2, c)
        for j, (px, py) in enumerate(chips):
            mine, got = buf.at[q, :, rows], buf.at[2 * px + py, :, rows]
            starts.append(pltpu.make_async_remote_copy(src_ref=mine, dst_ref=mine, send_sem=send.at[j], recv_sem=recv.at[j],
                                                       device_id=(px, py, c), device_id_type=MESH_ID))
            arrivals.append(pltpu.make_async_remote_copy(src_ref=got, dst_ref=got, send_sem=send.at[j], recv_sem=recv.at[j],
                                                         device_id=(px, py, c), device_id_type=MESH_ID))
    return starts, arrivals


def gather_forward(name, bufs):
    n = len(bufs)

    def body(*refs):
        dst = refs[n:2 * n]
        send, recv = refs[2 * n:]
        x, y, c, chips = _place()
        sib = (x, y, 1 - c)
        cps = []
        for a in range(n):
            for j, (px, py) in enumerate(chips):
                got = dst[a].at[2 * px + py, :, _row_half(dst[a], 2, c)]
                cps.append(pltpu.make_async_remote_copy(src_ref=got, dst_ref=got, send_sem=send.at[a, j], recv_sem=recv.at[a, j],
                                                        device_id=sib, device_id_type=MESH_ID))
        for cp in cps:
            cp.start()
        for a in range(n):
            for j, (px, py) in enumerate(chips):
                theirs = dst[a].at[2 * px + py, :, _row_half(dst[a], 2, 1 - c)]
                pltpu.make_async_remote_copy(src_ref=theirs, dst_ref=theirs, send_sem=send.at[a, j], recv_sem=recv.at[a, j],
                                             device_id=sib, device_id_type=MESH_ID).wait_recv()
        for cp in cps:
            cp.wait_send()

    outs = [jax.ShapeDtypeStruct(s.shape, s.dtype) for s in bufs]
    sems = [pltpu.SemaphoreType.DMA((n, N_PEER_CHIPS))] * 2
    return _comm_call(name, body, bufs, outs, sems, aliases={a: a for a in range(n)})


def swap_row_halves(name, grads):
    n = len(grads)

    def body(*refs):
        src, dst = refs[:n], refs[n:2 * n]
        send, recv = refs[2 * n:]
        x, y, c, _ = _place()
        cps = []
        for a in range(n):
            cps.append(pltpu.make_async_remote_copy(
                src_ref=src[a].at[:, :, _row_half(src[a], 2, 1 - c)], dst_ref=dst[a], send_sem=send.at[a], recv_sem=recv.at[a],
                device_id=(x, y, 1 - c), device_id_type=MESH_ID))
        for cp in cps:
            cp.start()
        for cp in cps:
            cp.wait()

    outs = [jax.ShapeDtypeStruct(g.shape[:2] + (g.shape[2] // 2, g.shape[3]), g.dtype) for g in grads]
    return _comm_call(name, body, grads, outs, [pltpu.SemaphoreType.DMA((n,))] * 2)


def scatter_build(refs, send, recv):
    n = len(refs) // 2
    x, y, c, chips = _place()
    starts, arrivals = [], []
    for a in range(n):
        src, dst = refs[a], refs[n + a]
        for j, (px, py) in enumerate(chips):
            starts.append(pltpu.make_async_remote_copy(src_ref=src.at[2 * px + py], dst_ref=dst.at[j], send_sem=send.at[j],
                                                       recv_sem=recv.at[j], device_id=(px, py, c), device_id_type=MESH_ID))
            arrivals.append(pltpu.make_async_remote_copy(src_ref=dst.at[j], dst_ref=dst.at[j], send_sem=send.at[j],
                                                         recv_sem=recv.at[j], device_id=(px, py, c), device_id_type=MESH_ID))
    return starts, arrivals


def join_row_halves(bufs):
    n = len(bufs)

    def body(*refs):
        dst = refs[n:2 * n]
        send, recv = refs[2 * n:]
        x, y, c, _ = _place()
        cps = []
        for a in range(n):
            mine = dst[a].at[:, _row_half(dst[a], 1, c)]
            cps.append(pltpu.make_async_remote_copy(src_ref=mine, dst_ref=mine, send_sem=send.at[a], recv_sem=recv.at[a],
                                                    device_id=(x, y, 1 - c), device_id_type=MESH_ID))
        for cp in cps:
            cp.start()
        for a in range(n):
            theirs = dst[a].at[:, _row_half(dst[a], 1, 1 - c)]
            pltpu.make_async_remote_copy(src_ref=theirs, dst_ref=theirs, send_sem=send.at[a], recv_sem=recv.at[a],
                                         device_id=(x, y, 1 - c), device_id_type=MESH_ID).wait_recv()
        for cp in cps:
            cp.wait_send()

    outs = [jax.ShapeDtypeStruct(b.shape, b.dtype) for b in bufs]
    sems = [pltpu.SemaphoreType.DMA((n,))] * 2
    return _comm_call("join_row_halves", body, bufs, outs, sems, aliases={a: a for a in range(n)})


def share_with_all(name, packed, me):
    slots = lax.dynamic_update_slice(jnp.zeros((N_DEV,) + packed.shape, packed.dtype), packed[None], (me, 0, 0))

    def body(_, dst, send, recv):
        x, y, c, _ = _place()
        me = 4 * x + 2 * y + c
        cps = []
        for k in range(1, N_DEV):
            px, py, pc = x ^ (k >> 2), y ^ ((k >> 1) & 1), c ^ (k & 1)
            cps.append(pltpu.make_async_remote_copy(src_ref=dst.at[me], dst_ref=dst.at[me], send_sem=send.at[k - 1],
                                                    recv_sem=recv.at[k - 1], device_id=(px, py, pc), device_id_type=MESH_ID))
        for cp in cps:
            cp.start()
        for k in range(1, N_DEV):
            px, py, pc = x ^ (k >> 2), y ^ ((k >> 1) & 1), c ^ (k & 1)
            got = dst.at[4 * px + 2 * py + pc]
            pltpu.make_async_remote_copy(src_ref=got, dst_ref=got, send_sem=send.at[k - 1], recv_sem=recv.at[k - 1],
                                         device_id=(px, py, pc), device_id_type=MESH_ID).wait_recv()
        for cp in cps:
            cp.wait_send()

    outs = [jax.ShapeDtypeStruct(slots.shape, slots.dtype)]
    sems = [pltpu.SemaphoreType.DMA((N_DEV - 1,))] * 2
    return _comm_call(name, body, [slots], outs, sems, aliases={0: 0})[0]


def _w_tiles(R, C):
    return _tile(R, max(16, (1 << 20) // (4 * C) // 16 * 16), 16)


def cast_bf16(w, l, q_arr):
    _, R, C = w.shape
    tr = _w_tiles(R, C)
    ins = [(w, (None, tr, C), lambda r, q: (l, r, 0))]
    outs = [((N_CHIPS, 1, R, C), BF16, (None, None, tr, C), lambda r, q: (q[0], 0, r, 0), None)]
    return bmap_fwd("cast_bf16", lambda a: (a,), (R // tr,), ins, outs, scalars=(q_arr,))[0]


def add_sibling_half(g, b1, c_arr):
    Q, _, rh, C = b1.shape
    tr = _w_tiles(rh, C)
    nb = rh // tr
    blk = (None, None, tr, C)
    ins = [(g, blk, lambda q, r, c: (q, 0, c[0] * nb + r, 0)), (b1, blk, lambda q, r, c: (q, 0, r, 0))]
    outs = [(b1.shape, BF16, blk, lambda q, r, c: (q, 0, r, 0), None)]
    return bmap_fwd("add_sibling_half", lambda a, b: (a + b,), (Q, nb), ins, outs, scalars=(c_arr,))[0]


def sum_chips(pair, b2, into, l, q_arr, c_arr):
    _, _, rh, C = b2.shape
    tr = _w_tiles(rh, C)
    nb = rh // tr
    blk = (None, None, tr, C)
    ins = [(pair, blk, lambda r, q, c: (q[0], 0, r, 0))]
    ins += [(b2, blk, (lambda r, q, c, jj=jj: (jj, 0, r, 0))) for jj in range(3)]
    outs = [(into.shape, F32, (None, tr, C), lambda r, q, c: (l, c[0] * nb + r, 0), None)]
    return bmap_fwd("sum_chips", lambda a, b, c, d: (((a + b) + c) + d,), (nb,), ins, outs,
                    scalars=(q_arr, c_arr), into=into)[0]


def sum_devices(slots):
    nd, NR, C = slots.shape
    tr = _tile(NR, 512, 8)
    ins = [(slots, (None, tr, C), (lambda r, dd=dd: (dd, r, 0))) for dd in range(nd)]
    outs = [((NR, C), F32, (tr, C), lambda r: (r, 0), None)]
    return bmap_fwd("sum_devices", lambda *a: (functools.reduce(lambda u, v: u + v, a),), (NR // tr,), ins, outs)[0]


def adamw(name, w, g, m, v):
    if w.ndim == 2:
        R, C = w.shape
        tr = _w_tiles(R, C)
        spec = ((tr, C), lambda r: (r, 0))
        grid = (R // tr,)
    else:
        L, R, C = w.shape
        tr = _w_tiles(R, C)
        spec = ((None, tr, C), lambda l, r: (l, r, 0))
        grid = (L, R // tr)
    ins = [(a,) + spec for a in (w, g, m, v)]
    outs = [(w.shape, F32) + spec + (None,)] * 3
    return bmap_fwd(name, f_adam, grid, ins, outs)


def loss_and_grad(h, target):
    T, D = h.shape
    tm = _tile(T, 256, 8)

    def fn(hv, tv):
        d = hv - tv
        return jnp.sum(d * d, keepdims=True).reshape(1, 1) * (0.5 / D), d * (1.0 / D)

    ins = [(h, (tm, D), lambda i: (i, 0)), (target, (tm, D), lambda i: (i, 0))]
    outs = [((1, 1), F32, (1, 1), lambda i: (0, 0), 0), ((T, D), F32, (tm, D), lambda i: (i, 0), None)]
    return bmap_fwd("loss_and_grad", fn, (T // tm,), ins, outs)


BIG = ("hg_w_in", "hg_w_out", "gm_w_in", "gm_w_out", "ffn_w_gate", "ffn_w_up", "ffn_w_down", "ple_w_proj", "ple_w_gate")
KIND = {"hg_w_in": "col", "hg_w_out": "row", "gm_w_in": "col", "gm_w_out": "row", "ffn_w_gate": "col",
        "ffn_w_up": "col", "ffn_w_down": "row", "ple_w_proj": "col", "ple_w_gate": "row"}
SMALL = ("hg_lb_logits", "hg_out_norm", "gm_ln_g", "gm_ln_b", "gm_w_s", "gm_b_s", "norm_mix_pre", "norm_mix_post",
         "norm_ffn_pre", "norm_ffn_post", "ple_norm")
WEIGHTS = ("hg_w_in", "hg_lb_logits", "hg_out_norm", "hg_w_out", "gm_w_in", "gm_ln_g", "gm_ln_b", "gm_w_s", "gm_b_s",
           "gm_w_out", "norm_mix_pre", "norm_mix_post", "norm_ffn_pre", "norm_ffn_post", "ffn_w_gate", "ffn_w_up",
           "ffn_w_down", "ple_w_proj", "ple_w_gate", "ple_norm")


def _pack(arrs):
    rows = []
    for a in arrs:
        flat = a.reshape(-1)
        pad = (-flat.shape[0]) % (8 * LANES)
        rows.append(jnp.pad(flat, (0, pad)).reshape(-1, LANES))
    n_rows = sum(r.shape[0] for r in rows)
    rows.append(jnp.zeros(((-n_rows) % PACK_ROWS, LANES), F32))
    return jnp.concatenate(rows, axis=0)


def _unpack(packed, shapes):
    out, r = [], 0
    for s in shapes:
        size = 1
        for d in s:
            size *= d
        nr = -(-size // (8 * LANES)) * 8
        out.append(packed[r:r + nr].reshape(-1)[:size].reshape(s))
        r += nr
    return out


def _step(x, p, W, M, V, loss_target):
    n_seq, S, D = x.shape
    T = n_seq * S
    depth = p.shape[0]
    n_hg = W["hg_w_in"].shape[0]
    x2 = x.reshape(T, D)
    p3 = p.reshape(depth, T, p.shape[-1])
    tgt = loss_target.reshape(T, D)
    xi, yi, ci = lax.axis_index("x"), lax.axis_index("y"), lax.axis_index("c")
    q_me = 2 * xi + yi
    c_arr = jnp.reshape(ci, (1,)).astype(jnp.int32)
    q_arr = jnp.reshape(q_me, (1,)).astype(jnp.int32)

    def layer_weights(i):
        mix = ("hg_w_in", "hg_w_out") if i % 2 == 0 else ("gm_w_in", "gm_w_out")
        return [(k, i // 2) for k in mix] + [(k, i) for k in ("ffn_w_gate", "ffn_w_up", "ffn_w_down", "ple_w_proj", "ple_w_gate")]

    groups = [layer_weights(i) for i in range(depth)]
    G = {k: {} for k in BIG}
    DW = {k: {l: lax.empty((N_CHIPS, 1) + W[k].shape[1:], BF16) for l in range(W[k].shape[0])} for k in BIG}
    in_flight, tokens = [], []
    for i in range(depth):
        send, recv, arrs, tok = split_start("gather_start_%d" % i, gather_build, [cast_bf16(W[k], l, q_arr) for k, l in groups[i]])
        in_flight.append((send, recv, arrs))
        tokens.append(tok)
    started = functools.reduce(lambda u, v: u + v, tokens)

    def finish_gather(i, after):
        send, recv, arrs = in_flight[i]
        arrs = split_wait("gather_wait_%d" % i, gather_build, send, recv, arrs, after)
        for (k, l), buf in zip(groups[i], gather_forward("gather_forward_%d" % i, arrs)):
            G[k][l] = buf

    finish_gather(0, started)
    me = 4 * xi + 2 * yi + ci
    ln_full = share_with_all("share_ln", _pack([W["gm_ln_g"], W["gm_ln_b"]]), me)
    n_gm, dq = W["gm_ln_g"].shape
    ln_parts = [_unpack(ln_full[4 * qx + 2 * qy + 0], [(n_gm, dq), (n_gm, dq)]) for qx in range(2) for qy in range(2)]
    ln_g = jnp.concatenate([lp[0] for lp in ln_parts], axis=1)
    ln_b = jnp.concatenate([lp[1] for lp in ln_parts], axis=1)

    row = lambda a, i: a[i][None, :]
    f_lb = _make_f_lb(n_hg)
    lb_rows = [row(W["hg_lb_logits"], j) for j in range(n_hg)]
    one = (1, D)
    lb_ins = [(r, one, lambda i: (0, 0)) for r in lb_rows]
    lb_out = bmap_fwd("hg_lower_bounds", f_lb, (1,), lb_ins, [(one, F32, one, lambda i: (0, 0), None)] * (3 * n_hg))

    saved = []
    h = x2
    a = rows_fwd("prenorm", f_prenorm, [h], [row(W["norm_mix_pre"], 0)], [BF16])[0]
    for i in range(depth):
        j = i // 2
        sv = {"h": h, "a": a}
        if i > 0:
            finish_gather(i, h)
        if i % 2 == 0:
            proj4 = mm_fwd("hg_in", a, G["hg_w_in"], j, "col", parts=True)
            lbp = lb_out[3 * j:3 * j + 3]
            onorm = row(W["hg_out_norm"], j)
            og, states = hgrn_fwd("hgrn_fwd", proj4, *lbp, onorm, n_seq)
            m = mm_fwd("hg_out", og, G["hg_w_out"], j, "row")
            sv.update(proj4=proj4, states=states, og=og, lbp=lbp, onorm=onorm)
        else:
            z = mm_fwd("gm_in", a, G["gm_w_in"], j, "col")
            lg, lb_ = row(ln_g, j), row(ln_b, j)
            u, vn = rows_fwd("gm_gelu_ln", f_gm_in, [z], [lg, lb_], [F32, BF16], tm=128)
            ws = W["gm_w_s"][j]
            bs = W["gm_b_s"][j][:, :, None]
            gb = min(GM_BLOCK, S)
            sp_grid = (D // LANES, T // gb)
            sp_ins = [(u, (gb, LANES), lambda g, n: (n, g)), (vn, (gb, LANES), lambda g, n: (n, g)),
                      (ws, (None, GM_CHUNK, GM_CHUNK), lambda g, n: (g, 0, 0)),
                      (bs, (None, GM_CHUNK, 1), lambda g, n: (g, 0, 0))]
            y = bmap_fwd("gm_spatial", f_gm_spatial, sp_grid, sp_ins,
                         [((T, D), BF16, (gb, LANES), lambda g, n: (n, g), None)])[0]
            m = mm_fwd("gm_out", y, G["gm_w_out"], j, "row")
            sv.update(z=z, lg=lg, lb_=lb_, sp_ins=sp_ins, sp_grid=sp_grid, y=y)
        g_post, g_fpre = row(W["norm_mix_post"], i), row(W["norm_ffn_pre"], i)
        h1, fin = rows_fwd("mix_post_ffn_pre", f_post_pre, [h, m], [g_post, g_fpre], [F32, BF16])
        gate = mm_fwd("ffn_gate", fin, G["ffn_w_gate"], i, "col")
        up = mm_fwd("ffn_up", fin, G["ffn_w_up"], i, "col")
        act = rows_fwd("swiglu", f_swiglu, [gate, up], [], [BF16], tm=128)[0]
        f = mm_fwd("ffn_down", act, G["ffn_w_down"], i, "row")
        g_fpost = row(W["norm_ffn_post"], i)
        h2 = rows_fwd("ffn_post", f_post, [h1, f], [g_fpost], [F32])[0]
        e = mm_fwd("ple_proj", p3, G["ple_w_proj"], i, "col", xl=i)
        zg = mm_fwd("ple_gate", h2, G["ple_w_gate"], i, "row")
        g_ple = row(W["ple_norm"], i)
        sv.update(m=m, h1=h1, fin=fin, gate=gate, up=up, act=act, f=f, h2=h2, e=e, zg=zg,
                  g_post=g_post, g_fpre=g_fpre, g_fpost=g_fpost, g_ple=g_ple)
        if i + 1 < depth:
            g_next = row(W["norm_mix_pre"], i + 1)
            h, a = rows_fwd("ple_next_pre", f_ple_pre, [h2, e, zg], [g_ple, g_next], [F32, BF16])
            sv["g_next"] = g_next
        else:
            h = rows_fwd("ple_last", f_ple, [h2, e, zg], [g_ple], [F32])[0]
        saved.append(sv)

    loss_part, dh = loss_and_grad(h, tgt)
    loss = lax.psum(loss_part[0, 0], ("x", "y", "c"))

    sg = {k: [None] * W[k].shape[0] for k in ("norm_mix_pre", "norm_mix_post", "norm_ffn_pre", "norm_ffn_post", "ple_norm",
                                              "hg_out_norm", "gm_ln_g", "gm_ln_b", "gm_w_s", "gm_b_s")}
    d_lbp = [None] * (3 * n_hg)
    da_next = None
    GRAD = {k: lax.empty(W[k].shape, F32) for k in BIG}
    scattering = {}

    def start_scatter(i):
        dws = [DW[k][l] for k, l in groups[i]]
        pairs = [add_sibling_half(g, b, c_arr) for g, b in zip(dws, swap_row_halves("swap_%d" % i, dws))]
        lands = [lax.empty((N_PEER_CHIPS,) + pr.shape[1:], BF16) for pr in pairs]
        send, recv, arrs, tok = split_start("scatter_start_%d" % i, scatter_build, pairs + lands)
        scattering[i] = (send, recv, arrs)
        return tok

    def finish_scatter(i, after):
        send, recv, arrs = scattering.pop(i)
        arrs = split_wait("scatter_wait_%d" % i, scatter_build, send, recv, arrs, after)
        n = len(groups[i])
        for (k, l), pr, ld in zip(groups[i], arrs[:n], arrs[n:]):
            GRAD[k] = sum_chips(pr, ld, GRAD[k], l, q_arr, c_arr)

    tok = None
    for i in reversed(range(depth)):
        j = i // 2
        sv = saved[i]
        if i + 1 < depth:
            g_ple_after = sv["g_ple"] + tok[0:1, 0:1]
            dh2, de, dzg, d_gple, d_gnext = rows_bwd("ple_next_pre_bwd", f_ple_pre, [sv["h2"], sv["e"], sv["zg"]],
                                                     [g_ple_after, sv["g_next"]], [dh, da_next], [F32, BF16, BF16])
            sg["norm_mix_pre"][i + 1] = d_gnext
        else:
            dh2, de, dzg, d_gple = rows_bwd("ple_last_bwd", f_ple, [sv["h2"], sv["e"], sv["zg"]], [sv["g_ple"]], [dh],
                                            [F32, BF16, BF16])
        sg["ple_norm"][i] = d_gple
        DW["ple_w_proj"] = mm_bwd_w("ple_proj_dw", p3, de, DW["ple_w_proj"], i, "col", xl=i)
        DW["ple_w_gate"] = mm_bwd_w("ple_gate_dw", sv["h2"], dzg, DW["ple_w_gate"], i, "row")
        dh2 = mm_bwd_x("ple_gate_dx", dzg, G["ple_w_gate"], i, "row", addend=dh2)
        dh1, df, d_gfpost = rows_bwd("ffn_post_bwd", f_post, [sv["h1"], sv["f"]], [sv["g_fpost"]], [dh2], [F32, BF16])
        sg["norm_ffn_post"][i] = d_gfpost
        dact = mm_bwd_x("ffn_down_dx", df, G["ffn_w_down"], i, "row")
        DW["ffn_w_down"] = mm_bwd_w("ffn_down_dw", sv["act"], df, DW["ffn_w_down"], i, "row")
        dgate, dup = rows_bwd("swiglu_bwd", f_swiglu, [sv["gate"], sv["up"]], [], [dact], [BF16, BF16], tm=128)
        dfin = mm_bwd_x("ffn_gate_dx", dgate, G["ffn_w_gate"], i, "col")
        dfin = mm_bwd_x("ffn_up_dx", dup, G["ffn_w_up"], i, "col", addend=dfin)
        DW["ffn_w_gate"] = mm_bwd_w("ffn_gate_dw", sv["fin"], dgate, DW["ffn_w_gate"], i, "col")
        DW["ffn_w_up"] = mm_bwd_w("ffn_up_dw", sv["fin"], dup, DW["ffn_w_up"], i, "col")
        dh, dm, d_gpost, d_gfpre = rows_bwd("mix_post_ffn_pre_bwd", f_post_pre, [sv["h"], sv["m"]],
                                            [sv["g_post"], sv["g_fpre"]], [dh1, dfin], [F32, BF16])
        sg["norm_mix_post"][i], sg["norm_ffn_pre"][i] = d_gpost, d_gfpre
        if i % 2 == 0:
            dog = mm_bwd_x("hg_out_dx", dm, G["hg_w_out"], j, "row")
            DW["hg_w_out"] = mm_bwd_w("hg_out_dw", sv["og"], dm, DW["hg_w_out"], j, "row")
            dproj4, d0, d1, d2, d_on = hgrn_bwd("hgrn_bwd", sv["proj4"], sv["states"], dog, *sv["lbp"], sv["onorm"], n_seq)
            d_lbp[3 * j:3 * j + 3] = [d0, d1, d2]
            sg["hg_out_norm"][j] = d_on
            da_next = mm_bwd_x("hg_in_dx", dproj4, G["hg_w_in"], j, "col", parts=True)
            DW["hg_w_in"] = mm_bwd_w("hg_in_dw", sv["a"], dproj4, DW["hg_w_in"], j, "col", parts=True)
        else:
            dy = mm_bwd_x("gm_out_dx", dm, G["gm_w_out"], j, "row")
            DW["gm_w_out"] = mm_bwd_w("gm_out_dw", sv["y"], dm, DW["gm_w_out"], j, "row")
            gb = sv["sp_ins"][0][1][0]
            du, dvn, dws, dbs = bmap_bwd("gm_spatial_bwd", f_gm_spatial, sv["sp_grid"], sv["sp_ins"],
                                         [(dy, (gb, LANES), lambda g, n: (n, g))],
                                         [(0, F32, None), (1, F32, None), (2, F32, 1), (3, F32, 1)])
            sg["gm_w_s"][j], sg["gm_b_s"][j] = dws, dbs[:, :, 0]
            dz, d_lg, d_lb = rows_bwd("gm_gelu_ln_bwd", f_gm_in, [sv["z"]], [sv["lg"], sv["lb_"]], [du, dvn], [BF16], tm=128)
            sg["gm_ln_g"][j], sg["gm_ln_b"][j] = d_lg, d_lb
            da_next = mm_bwd_x("gm_in_dx", dz, G["gm_w_in"], j, "col")
            DW["gm_w_in"] = mm_bwd_w("gm_in_dw", sv["a"], dz, DW["gm_w_in"], j, "col")
        tok = start_scatter(i)
        if i + 1 < depth:
            finish_scatter(i + 1, da_next)
    finish_scatter(0, tok)
    g0 = row(W["norm_mix_pre"], 0)
    grad_x, d_g0 = rows_bwd("prenorm_bwd", f_prenorm_thru, [saved[0]["h"]], [g0], [da_next, dh], [F32])
    sg["norm_mix_pre"][0] = d_g0
    d_logits = bmap_bwd("hg_lower_bounds_bwd", f_lb, (1,), lb_ins, [(d, one, lambda i: (0, 0)) for d in d_lbp],
                        [(jj, F32, None) for jj in range(n_hg)])

    small_g = {k: jnp.stack([v.reshape(W[k].shape[1:] if k not in ("gm_ln_g", "gm_ln_b") else (D,)) for v in sg[k]])
               for k in sg}
    small_g["hg_lb_logits"] = jnp.concatenate(d_logits, axis=0)
    small_shapes = [small_g[k].shape for k in SMALL]
    red = sum_devices(share_with_all("share_small_grads", _pack([small_g[k] for k in SMALL]), me))
    small_red = dict(zip(SMALL, _unpack(red, small_shapes)))
    for k in ("gm_ln_g", "gm_ln_b"):
        small_red[k] = lax.dynamic_slice_in_dim(small_red[k], q_me * dq, dq, axis=1)
    pk = lambda d: _pack([d[k] for k in SMALL])
    s_delta, s_m, s_v = adamw("adamw_small", pk(W), pk(small_red), pk(M), pk(V))
    shard_shapes = [W[k].shape for k in SMALL]
    out_g, out_d, out_m, out_v = dict(small_red), {}, {}, {}
    for dct, packed in ((out_d, s_delta), (out_m, s_m), (out_v, s_v)):
        dct.update(zip(SMALL, _unpack(packed, shard_shapes)))

    full = join_row_halves([GRAD[k] for k in BIG])
    for k, g in zip(BIG, full):
        out_g[k] = g
        out_d[k], out_m[k], out_v[k] = adamw("adamw_" + k, W[k], g, M[k], V[k])

    outs = [loss, grad_x.reshape(x.shape)]
    for dct in (out_g, out_d, out_m, out_v):
        outs += [dct[k] for k in WEIGHTS]
    return tuple(outs)


def kernel(x, p, hg_w_in, hg_lb_logits, hg_out_norm, hg_w_out, gm_w_in, gm_ln_g, gm_ln_b, gm_w_s, gm_b_s, gm_w_out, norm_mix_pre, norm_mix_post, norm_ffn_pre, norm_ffn_post, ffn_w_gate, ffn_w_up, ffn_w_down, ple_w_proj, ple_w_gate, ple_norm, loss_target, m_hg_w_in, m_hg_lb_logits, m_hg_out_norm, m_hg_w_out, m_gm_w_in, m_gm_ln_g, m_gm_ln_b, m_gm_w_s, m_gm_b_s, m_gm_w_out, m_norm_mix_pre, m_norm_mix_post, m_norm_ffn_pre, m_norm_ffn_post, m_ffn_w_gate, m_ffn_w_up, m_ffn_w_down, m_ple_w_proj, m_ple_w_gate, m_ple_norm, v_hg_w_in, v_hg_lb_logits, v_hg_out_norm, v_hg_w_out, v_gm_w_in, v_gm_ln_g, v_gm_ln_b, v_gm_w_s, v_gm_b_s, v_gm_w_out, v_norm_mix_pre, v_norm_mix_post, v_norm_ffn_pre, v_norm_ffn_post, v_ffn_w_gate, v_ffn_w_up, v_ffn_w_down, v_ple_w_proj, v_ple_w_gate, v_ple_norm):
    W = dict(zip(WEIGHTS, (hg_w_in, hg_lb_logits, hg_out_norm, hg_w_out, gm_w_in, gm_ln_g, gm_ln_b, gm_w_s, gm_b_s, gm_w_out,
                           norm_mix_pre, norm_mix_post, norm_ffn_pre, norm_ffn_post, ffn_w_gate, ffn_w_up, ffn_w_down,
                           ple_w_proj, ple_w_gate, ple_norm)))
    M = dict(zip(WEIGHTS, (m_hg_w_in, m_hg_lb_logits, m_hg_out_norm, m_hg_w_out, m_gm_w_in, m_gm_ln_g, m_gm_ln_b, m_gm_w_s,
                           m_gm_b_s, m_gm_w_out, m_norm_mix_pre, m_norm_mix_post, m_norm_ffn_pre, m_norm_ffn_post,
                           m_ffn_w_gate, m_ffn_w_up, m_ffn_w_down, m_ple_w_proj, m_ple_w_gate, m_ple_norm)))
    V = dict(zip(WEIGHTS, (v_hg_w_in, v_hg_lb_logits, v_hg_out_norm, v_hg_w_out, v_gm_w_in, v_gm_ln_g, v_gm_ln_b, v_gm_w_s,
                           v_gm_b_s, v_gm_w_out, v_norm_mix_pre, v_norm_mix_post, v_norm_ffn_pre, v_norm_ffn_post,
                           v_ffn_w_gate, v_ffn_w_up, v_ffn_w_down, v_ple_w_proj, v_ple_w_gate, v_ple_norm)))
    return _step(x, p, W, M, V, loss_target)
```

```python
import functools

import jax
import jax.numpy as jnp
from jax import lax
from jax.experimental import pallas as pl
from jax.experimental.pallas import tpu as pltpu

F32 = jnp.float32
BF16 = jnp.bfloat16
MESH_ID = pl.DeviceIdType.MESH

LANES = 128
N_CHIPS = 4
N_DEV = 8
VMEM_LIMIT = 56 * 1024 * 1024
HG_SUB = 64
HG_BLOCK = 256
HG_HEADS_PER = 2
GM_CHUNK = 128
GM_BLOCK = 512
PACK_ROWS = 512
LB_FLOOR = 1e-30
EPS = 1e-6
ADAM_LR, ADAM_B1, ADAM_B2, ADAM_EPS, ADAM_WD, ADAM_STEP = 0.001, 0.9, 0.999, 1e-08, 0.01, 10


def _tile(n, pref, mult=LANES):
    if n <= pref:
        return n
    t = (pref // mult) * mult
    while t >= mult:
        if n % t == 0:
            return t
        t -= mult
    return n


def _cp(n_axes):
    return pltpu.CompilerParams(dimension_semantics=("arbitrary",) * n_axes, vmem_limit_bytes=VMEM_LIMIT)


def _dense(block):
    return tuple(b for b in block if b is not None)


def _bmap(name, grid, ins, outs, compute, scalars=(), into=None):
    n_s, n_in = len(scalars), len(ins)
    n_extra = 0 if into is None else 1

    def body(*refs):
        in_refs = refs[n_s:n_s + n_in]
        out_refs = refs[n_s + n_in + n_extra:]
        vals = [r[...] for r in in_refs]
        res = compute(*vals)
        for r, o, spec in zip(out_refs, res, outs):
            keep = spec[4]
            if keep is None:
                r[...] = o.astype(r.dtype)
            else:
                first = functools.reduce(jnp.logical_and, [pl.program_id(a) == 0 for a in range(keep, len(grid))])

                @pl.when(first)
                def _():
                    r[...] = jnp.zeros(r.shape, r.dtype)

                r[...] += o.astype(r.dtype)

    grid_spec = pltpu.PrefetchScalarGridSpec(
        num_scalar_prefetch=n_s, grid=grid,
        in_specs=[pl.BlockSpec(b, m) for _, b, m in ins] + [pl.BlockSpec(memory_space=pl.ANY)] * n_extra,
        out_specs=[pl.BlockSpec(o[2], o[3]) for o in outs])
    return pl.pallas_call(
        body, name=name, grid_spec=grid_spec,
        out_shape=[jax.ShapeDtypeStruct(o[0], o[1]) for o in outs],
        input_output_aliases={n_s + n_in: 0} if n_extra else {},
        compiler_params=_cp(len(grid)),
    )(*scalars, *[a for a, _, _ in ins], *([into] if n_extra else []))


def bmap_fwd(name, fn, grid, ins, outs, scalars=(), into=None):
    return _bmap(name, grid, ins, outs, lambda *v: fn(*[x.astype(F32) for x in v]), scalars, into)


def bmap_bwd(name, fn, grid, ins, cots, grads, scalars=()):
    n_in = len(ins)
    diff = [g[0] for g in grads]
    cot_ins = [c for c in cots if c is not None]

    def compute(*vals):
        xs = [v.astype(F32) for v in vals[:n_in]]
        cvals = list(vals[n_in:])

        def f(*d):
            full = list(xs)
            for i, dv in zip(diff, d):
                full[i] = dv
            return tuple(fn(*full))

        res, pull = jax.vjp(f, *[xs[i] for i in diff])
        cts = []
        for r, c in zip(res, cots):
            cts.append(jnp.zeros_like(r) if c is None else cvals.pop(0).astype(F32))
        return pull(tuple(cts))

    outs = [(ins[i][0].shape, dt, ins[i][1], ins[i][2], keep) for i, dt, keep in grads]
    return _bmap(name, grid, list(ins) + cot_ins, outs, compute, scalars)


def _mm(name, a, b, out_shape, out_dtype, grid, a_spec, b_spec, o_spec, dims, addend=None, alias_out=None):
    nk = grid[2]
    o_dense = _dense(o_spec[0])
    has_add = addend is not None
    has_alias = alias_out is not None

    def body(*refs):
        a_ref, b_ref = refs[0], refs[1]
        pos = 2
        c_ref = None
        if has_add:
            c_ref = refs[pos]
            pos += 1
        if has_alias:
            pos += 1
        o_ref = refs[pos]
        acc_ref = refs[pos + 1] if nk > 1 else None
        p = lax.dot_general(a_ref[...].astype(BF16), b_ref[...].astype(BF16), (dims, ((), ())),
                            preferred_element_type=F32)

        def finish(total):
            if has_add:
                total = total + c_ref[...].astype(F32)
            o_ref[...] = total.astype(o_ref.dtype)

        if nk == 1:
            finish(p)
        else:
            k = pl.program_id(2)

            @pl.when(k == 0)
            def _():
                acc_ref[...] = p

            @pl.when(jnp.logical_and(k > 0, k < nk - 1))
            def _():
                acc_ref[...] += p

            @pl.when(k == nk - 1)
            def _():
                finish(acc_ref[...] + p)

    in_specs = [pl.BlockSpec(*a_spec), pl.BlockSpec(*b_spec)]
    operands = [a, b]
    if has_add:
        in_specs.append(pl.BlockSpec(o_spec[0], o_spec[1]))
        operands.append(addend)
    aliases = {}
    if has_alias:
        in_specs.append(pl.BlockSpec(memory_space=pl.ANY))
        aliases = {len(operands): 0}
        operands.append(alias_out)
    return pl.pallas_call(
        body, name=name, grid=grid, in_specs=in_specs, out_specs=pl.BlockSpec(*o_spec),
        out_shape=jax.ShapeDtypeStruct(out_shape, out_dtype),
        scratch_shapes=[pltpu.VMEM(o_dense, F32)] if nk > 1 else [],
        input_output_aliases=aliases,
        compiler_params=pltpu.CompilerParams(dimension_semantics=("parallel", "parallel", "arbitrary"),
                                             vmem_limit_bytes=VMEM_LIMIT),
    )(*operands)


NN, NT, TN = ((1,), (0,)), ((1,), (1,)), ((0,), (0,))
TM = 512
TT = 1024
TN_PREF = 1408


def mm_fwd(name, x, wg, l, kind, out_dtype=F32, parts=False, xl=None):
    if isinstance(wg, dict):
        wg, l = wg[l], 0
    _, _, R, C = wg.shape
    T = x.shape[-2]
    tm = _tile(T, TM, 8)
    if kind == "col":
        tn = _tile(C, TN_PREF)
        npc = C // tn
        grid = (T // tm, N_CHIPS * npc, 1)
        a_blk = (tm, R) if xl is None else (None, tm, R)
        a_map = (lambda i, j, k: (i, 0)) if xl is None else (lambda i, j, k: (xl, i, 0))
        b_spec = ((None, None, R, tn), lambda i, j, k: (j // npc, l, 0, j % npc))
        if parts:
            out_shape = (N_CHIPS, T, C)
            o_spec = ((None, tm, tn), lambda i, j, k: (j // npc, i, j % npc))
        else:
            out_shape = (T, N_CHIPS * C)
            o_spec = ((tm, tn), lambda i, j, k: (i, j))
    else:
        tn = _tile(C, 1024)
        grid = (T // tm, C // tn, N_CHIPS)
        a_blk = (tm, R)
        a_map = lambda i, j, k: (i, k)
        b_spec = ((None, None, R, tn), lambda i, j, k: (k, l, 0, j))
        out_shape = (T, C)
        o_spec = ((tm, tn), lambda i, j, k: (i, j))
    return _mm(name, x, wg, out_shape, out_dtype, grid, (a_blk, a_map), b_spec, o_spec, NN)


def mm_bwd_x(name, dy, wg, l, kind, out_dtype=F32, parts=False, addend=None):
    if isinstance(wg, dict):
        wg, l = wg[l], 0
    _, _, R, C = wg.shape
    T = dy.shape[-2]
    tm = _tile(T, TM, 8)
    if kind == "col":
        tk = _tile(C, TN_PREF)
        npc = C // tk
        tno = _tile(R, 2048)
        grid = (T // tm, R // tno, N_CHIPS * npc)
        if parts:
            a_spec = ((None, tm, tk), lambda i, j, k: (k // npc, i, k % npc))
        else:
            a_spec = ((tm, tk), lambda i, j, k: (i, k))
        b_spec = ((None, None, tno, tk), lambda i, j, k: (k // npc, l, j, k % npc))
        out_shape = (T, R)
        o_spec = ((tm, tno), lambda i, j, k: (i, j))
    else:
        grid = (T // tm, N_CHIPS, 1)
        a_spec = ((tm, C), lambda i, j, k: (i, 0))
        b_spec = ((None, None, R, C), lambda i, j, k: (j, l, 0, 0))
        out_shape = (T, N_CHIPS * R)
        o_spec = ((tm, R), lambda i, j, k: (i, j))
    return _mm(name, dy, wg, out_shape, out_dtype, grid, a_spec, b_spec, o_spec, NT, addend=addend)


def mm_bwd_w(name, x, dy, dwg, l, kind, parts=False, xl=None):
    if isinstance(dwg, dict):
        return {**dwg, l: mm_bwd_w(name, x, dy, dwg[l], 0, kind, parts=parts, xl=xl)}
    _, _, R, C = dwg.shape
    T = dy.shape[-2]
    tt = _tile(T, TT, 16)
    nt = T // tt
    if kind == "col":
        tn = _tile(C, TN_PREF)
        npc = C // tn
        tr = _tile(R, 1024)
        grid = (R // tr, N_CHIPS * npc, nt)
        if xl is None:
            a_spec = ((tt, tr), lambda i, j, t: (t, i))
        else:
            a_spec = ((None, tt, tr), lambda i, j, t: (xl, t, i))
        if parts:
            b_spec = ((None, tt, tn), lambda i, j, t: (j // npc, t, j % npc))
        else:
            b_spec = ((tt, tn), lambda i, j, t: (t, j))
        o_spec = ((None, None, tr, tn), lambda i, j, t: (j // npc, l, i, j % npc))
    else:
        tn = _tile(C, 1024)
        grid = (N_CHIPS, C // tn, nt)
        a_spec = ((tt, R), lambda i, j, t: (t, i))
        b_spec = ((tt, tn), lambda i, j, t: (t, j))
        o_spec = ((None, None, R, tn), lambda i, j, t: (i, l, 0, j))
    return _mm(name, x, dy, dwg.shape, dwg.dtype, grid, a_spec, b_spec, o_spec, TN, alias_out=dwg)


def _sigmoid(x):
    return 0.5 * jnp.tanh(0.5 * x) + 0.5


def _rms(x, g):
    return x * lax.rsqrt(jnp.mean(x * x, axis=-1, keepdims=True) + EPS) * g


def f_prenorm(h, g):
    return (_rms(h, g),)


def f_prenorm_thru(h, g):
    return _rms(h, g), h


def f_post_pre(h, m, g_post, g_pre):
    h1 = h + _rms(m, g_post)
    return h1, _rms(h1, g_pre)


def f_swiglu(gate, up):
    return (gate * _sigmoid(gate) * up,)


def f_post(h1, f, g):
    return (h1 + _rms(f, g),)


def f_ple(h2, e, zg, g):
    return (h2 + _rms(e * _sigmoid(zg), g),)


def f_ple_pre(h2, e, zg, g, g_next):
    h3 = h2 + _rms(e * _sigmoid(zg), g)
    return h3, _rms(h3, g_next)


def _gelu(x):
    return 0.5 * x * (1.0 + lax.erf(x * 0.7071067811865476))


def f_gm_in(z, ln_g, ln_b):
    w = z.shape[-1] // 2
    u = _gelu(z[:, :w])
    v = _gelu(z[:, w:])
    mu = jnp.mean(v, axis=-1, keepdims=True)
    vc = v - mu
    vn = vc * lax.rsqrt(jnp.mean(vc * vc, axis=-1, keepdims=True) + EPS) * ln_g + ln_b
    return u, vn


def f_gm_spatial(u, vn, ws, bs):
    t = lax.broadcasted_iota(jnp.int32, ws.shape, 0)
    s = lax.broadcasted_iota(jnp.int32, ws.shape, 1)
    wm = jnp.where(t >= s, ws, 0.0).astype(BF16)
    ys = []
    for n in range(u.shape[0] // GM_CHUNK):
        rows = slice(n * GM_CHUNK, (n + 1) * GM_CHUNK)
        sv = jnp.dot(wm, vn[rows].astype(BF16), preferred_element_type=F32) + bs
        ys.append(u[rows] * sv)
    return (jnp.concatenate(ys, axis=0) if len(ys) > 1 else ys[0],)


def f_adam(w, g, m, v):
    m = ADAM_B1 * m + (1.0 - ADAM_B1) * g
    v = ADAM_B2 * v + (1.0 - ADAM_B2) * jnp.square(g)
    m_hat = m / (1.0 - ADAM_B1 ** ADAM_STEP)
    v_hat = v / (1.0 - ADAM_B2 ** ADAM_STEP)
    delta = -ADAM_LR * (m_hat / (jnp.sqrt(v_hat) + ADAM_EPS) + ADAM_WD * w)
    return delta, m, v


def _make_f_lb(n_layers):
    def f_lb(*logits):
        mx = functools.reduce(jnp.maximum, logits)
        ex = [jnp.exp(r - mx) for r in logits]
        tot = functools.reduce(lambda a, b: a + b, ex)
        sm = [e / tot for e in ex]
        outs = []
        run = jnp.zeros_like(sm[0])
        for j in range(n_layers):
            if j > 0:
                run = run + sm[j]
            lb = run
            outs += [jnp.log(jnp.maximum(lb, LB_FLOOR)), jnp.log(1.0 - lb), 1.0 - lb]
        return tuple(outs)
    return f_lb


def rows_fwd(name, fn, rows, params, out_dtypes, tm=256):
    T = rows[0].shape[0]
    tm = _tile(T, tm, 16)
    ins = [(r, (tm, r.shape[1]), lambda i: (i, 0)) for r in rows]
    ins += [(p, p.shape, lambda i: (0, 0)) for p in params]
    shapes = jax.eval_shape(lambda *a: fn(*a), *[jax.ShapeDtypeStruct((tm, r.shape[1]), F32) for r in rows],
                            *[jax.ShapeDtypeStruct(p.shape, F32) for p in params])
    outs = [((T, s.shape[1]), dt, (tm, s.shape[1]), lambda i: (i, 0), None) for s, dt in zip(shapes, out_dtypes)]
    return bmap_fwd(name, fn, (T // tm,), ins, outs)


def rows_bwd(name, fn, rows, params, cots, row_grad_dtypes, tm=256):
    T = rows[0].shape[0]
    tm = _tile(T, tm, 16)
    ins = [(r, (tm, r.shape[1]), lambda i: (i, 0)) for r in rows]
    ins += [(p, p.shape, lambda i: (0, 0)) for p in params]
    cts = [None if c is None else (c, (tm, c.shape[1]), lambda i: (i, 0)) for c in cots]
    grads = [(i, dt, None) for i, dt in enumerate(row_grad_dtypes) if dt is not None]
    grads += [(len(rows) + j, F32, 0) for j in range(len(params))]
    return bmap_bwd(name, fn, (T // tm,), ins, cts, grads)


def _log_sigmoid(z):
    return jnp.minimum(z, 0.0) - jnp.log(1.0 + jnp.exp(-jnp.abs(z)))


def _hg_gates(zf, ll0, ll1, oml):
    x2 = ll1 + _log_sigmoid(zf)
    mx = jnp.maximum(ll0, x2)
    g = mx + jnp.log(jnp.exp(ll0 - mx) + jnp.exp(x2 - mx))
    return g, oml * _sigmoid(-zf)


def hg_constants(n):
    levels = n.bit_length() - 1
    r = jnp.arange(n, dtype=jnp.int32)
    bounds = [r] + [((r >> (s + 1)) << (s + 1)) + ((1 << s) - 1) for s in range(levels)]
    sel = jnp.concatenate([(r[None, :] <= bd[:, None]) for bd in bounds], axis=0).astype(BF16)
    later = jnp.stack([((r >> s) & 1) for s in range(levels)]).astype(F32)
    later = jnp.broadcast_to(later[:, :, None], (levels, n, LANES))
    pair = jnp.stack([(r[:, None] >> (s + 1)) == (r[None, :] >> (s + 1)) for s in range(levels)]).astype(F32)
    return sel, sel.T, later, pair


def _dot2(m, x):
    hi = x.astype(BF16)
    lo = (x - hi.astype(F32)).astype(BF16)
    p = jnp.dot(m, jnp.concatenate([hi, lo], axis=1), preferred_element_type=F32)
    w = x.shape[1]
    return p[:, :w] + p[:, w:]


@jax.custom_vjp
def _sel_dot(sel, selt, g):
    return _dot2(sel, g)


def _sel_dot_fwd(sel, selt, g):
    return _dot2(sel, g), (sel, selt)


def _sel_dot_bwd(res, d):
    sel, selt = res
    return jnp.zeros_like(sel), jnp.zeros_like(selt), _dot2(selt, d)


_sel_dot.defvjp(_sel_dot_fwd, _sel_dot_bwd)


def _hg_state(st, zf, zi, ll0, ll1, oml, tri):
    g, k = _hg_gates(zf, ll0, ll1, oml)
    b = _dot2(tri, g)
    tot = jnp.sum(g, axis=0, keepdims=True)
    kd = k * jnp.exp(tot - b)
    return st * jnp.exp(tot) + jnp.dot(zi.T.astype(BF16), kd.astype(BF16), preferred_element_type=F32)


def _hg_step(st, zq, zf, zi, zg, ll0, ll1, oml, onorm, sel, selt, later, pair):
    n = zq.shape[0]
    levels = n.bit_length() - 1
    q = zq * _sigmoid(zq)
    g, k = _hg_gates(zf, ll0, ll1, oml)
    sums = _sel_dot(sel, selt, g)
    b = sums[:n]
    tot = jnp.sum(g, axis=0, keepdims=True)
    o = lax.dot_general((q * jnp.exp(b)).astype(BF16), st.astype(BF16), (NT, ((), ())), preferred_element_type=F32)
    a = jnp.zeros((n, n), F32)
    for s in range(levels):
        lt = later[s]
        fs = 1.0 - lt
        rel = b - sums[(s + 1) * n:(s + 2) * n]
        ql = lt * q * jnp.exp(lt * rel)
        kl = fs * k * jnp.exp(-fs * rel)
        al = lax.dot_general(ql.astype(BF16), kl.astype(BF16), (NT, ((), ())), preferred_element_type=F32)
        a = a + pair[s] * al
    o = o + jnp.dot(a.astype(BF16), zi.astype(BF16), preferred_element_type=F32)
    o = o + jnp.sum(q * k, axis=1, keepdims=True) * zi
    kd = k * jnp.exp(tot - b)
    st_new = st * jnp.exp(tot) + jnp.dot(zi.T.astype(BF16), kd.astype(BF16), preferred_element_type=F32)
    og = _rms(o, onorm) * (zg * _sigmoid(zg))
    return og, st_new


def _whole(arr, n_grid):
    zeros = (0,) * arr.ndim
    return pl.BlockSpec(arr.shape, (lambda h, n: zeros) if n_grid == 2 else (lambda i: zeros))


def _hg_dims(proj4, n_seq):
    _, T, D = proj4.shape
    S = T // n_seq
    hp = HG_HEADS_PER if (D // LANES) % HG_HEADS_PER == 0 else 1
    tb = min(HG_BLOCK, S)
    streams = [(b, hl) for b in range(n_seq) for hl in range(hp)]
    return T, D, S, hp, D // (LANES * hp), LANES * hp, tb, S // tb, tb // HG_SUB, streams


def hgrn_fwd(name, proj4, ll0, ll1, oml, onorm, n_seq):
    T, D, S, hp, n_hg, W, tb, nblk, nsub, streams = _hg_dims(proj4, n_seq)
    ns = len(streams)

    def body(p_ref, ll0_ref, ll1_ref, oml_ref, on_ref, sel_ref, selt_ref, later_ref, pair_ref, og_ref, st_ref, st):
        @pl.when(pl.program_id(1) == 0)
        def _():
            st[...] = jnp.zeros(st.shape, F32)

        st_ref[...] = st[...]
        on = on_ref[...]

        def step(j, carry):
            r = pl.ds(pl.multiple_of(j * HG_SUB, HG_SUB), HG_SUB)
            consts = (sel_ref[...], selt_ref[...], later_ref[...], pair_ref[...])
            args = []
            for si, (b, hl) in enumerate(streams):
                ln = slice(hl * LANES, (hl + 1) * LANES)
                args.append((st[si], p_ref[0, b, r, ln], p_ref[1, b, r, ln], p_ref[2, b, r, ln], p_ref[3, b, r, ln],
                             ll0_ref[:, ln], ll1_ref[:, ln], oml_ref[:, ln], on) + consts)
            res = [_hg_step(*a) for a in args]
            for si, (b, hl) in enumerate(streams):
                og_ref[b, r, hl * LANES:(hl + 1) * LANES] = res[si][0].astype(og_ref.dtype)
                st[si] = res[si][1]
            return carry

        lax.fori_loop(0, nsub, step, 0)

    vec = pl.BlockSpec((1, W), lambda h, n: (0, h))
    consts = hg_constants(HG_SUB)
    og, states = pl.pallas_call(
        body, name=name, grid=(n_hg, nblk),
        in_specs=[pl.BlockSpec((4, n_seq, tb, W), lambda h, n: (0, 0, n, h)), vec, vec, vec,
                  pl.BlockSpec((1, LANES), lambda h, n: (0, 0))] + [_whole(c, 2) for c in consts],
        out_specs=[pl.BlockSpec((n_seq, tb, W), lambda h, n: (0, n, h)),
                   pl.BlockSpec((None, None, ns, LANES, LANES), lambda h, n: (h, n, 0, 0, 0))],
        out_shape=[jax.ShapeDtypeStruct((n_seq, S, D), BF16),
                   jax.ShapeDtypeStruct((n_hg, nblk, ns, LANES, LANES), F32)],
        scratch_shapes=[pltpu.VMEM((ns, LANES, LANES), F32)],
        compiler_params=_cp(2),
    )(proj4.reshape(4, n_seq, S, D), ll0, ll1, oml, onorm, *consts)
    return og.reshape(T, D), states


def hgrn_bwd(name, proj4, states, dog, ll0, ll1, oml, onorm, n_seq):
    T, D, S, hp, n_hg, W, tb, nblk, nsub, streams = _hg_dims(proj4, n_seq)
    ns = len(streams)

    def body(p_ref, st_ref, dog_ref, ll0_ref, ll1_ref, oml_ref, on_ref, sel_ref, selt_ref, later_ref, pair_ref,
             dp_ref, dll0_ref, dll1_ref, doml_ref, don_ref, sbuf, dst):
        n_id = pl.program_id(1)

        @pl.when(n_id == 0)
        def _():
            dst[...] = jnp.zeros(dst.shape, F32)
            for ref in (dll0_ref, dll1_ref, doml_ref):
                ref[...] = jnp.zeros(ref.shape, F32)

        @pl.when(jnp.logical_and(n_id == 0, pl.program_id(0) == 0))
        def _():
            don_ref[...] = jnp.zeros(don_ref.shape, F32)

        on = on_ref[...]

        def fwd(j, carry):
            r = pl.ds(pl.multiple_of(j * HG_SUB, HG_SUB), HG_SUB)
            tri = sel_ref[0:HG_SUB, :]
            args = []
            for si, (b, hl) in enumerate(streams):
                ln = slice(hl * LANES, (hl + 1) * LANES)
                args.append((carry[si], p_ref[1, b, r, ln], p_ref[2, b, r, ln],
                             ll0_ref[:, ln], ll1_ref[:, ln], oml_ref[:, ln], tri))
            for si in range(ns):
                sbuf[si, j] = carry[si]
            return tuple(_hg_state(*a) for a in args)

        lax.fori_loop(0, nsub, fwd, tuple(st_ref[si] for si in range(ns)))

        def bwd(jj, carry):
            j = nsub - 1 - jj
            r = pl.ds(pl.multiple_of(j * HG_SUB, HG_SUB), HG_SUB)
            args, cts = [], []
            for si, (b, hl) in enumerate(streams):
                ln = slice(hl * LANES, (hl + 1) * LANES)
                args.append((sbuf[si, j], p_ref[0, b, r, ln], p_ref[1, b, r, ln], p_ref[2, b, r, ln],
                             p_ref[3, b, r, ln], ll0_ref[:, ln], ll1_ref[:, ln], oml_ref[:, ln], on))
                cts.append((dog_ref[b, r, ln].astype(F32), dst[si]))
            consts = (sel_ref[...], selt_ref[...], later_ref[...], pair_ref[...])
            step_fn = lambda *a: _hg_step(*a, *consts)
            ds = [jax.vjp(step_fn, *a)[1](ct) for a, ct in zip(args, cts)]
            d_on = carry
            for si, (b, hl) in enumerate(streams):
                ln = slice(hl * LANES, (hl + 1) * LANES)
                d = ds[si]
                dst[si] = d[0]
                for part in range(4):
                    dp_ref[part, b, r, ln] = d[1 + part].astype(dp_ref.dtype)
                dll0_ref[:, ln] += d[5]
                dll1_ref[:, ln] += d[6]
                doml_ref[:, ln] += d[7]
                d_on = d_on + d[8]
            return d_on

        don_ref[...] += lax.fori_loop(0, nsub, bwd, jnp.zeros((1, LANES), F32))

    last = nblk - 1
    vec = pl.BlockSpec((1, W), lambda h, n: (0, h))
    one = pl.BlockSpec((1, LANES), lambda h, n: (0, 0))
    consts = hg_constants(HG_SUB)
    dproj, d0, d1, d2, d_on = pl.pallas_call(
        body, name=name, grid=(n_hg, nblk),
        in_specs=[pl.BlockSpec((4, n_seq, tb, W), lambda h, n: (0, 0, last - n, h)),
                  pl.BlockSpec((None, None, ns, LANES, LANES), lambda h, n: (h, last - n, 0, 0, 0)),
                  pl.BlockSpec((n_seq, tb, W), lambda h, n: (0, last - n, h)), vec, vec, vec, one]
        + [_whole(c, 2) for c in consts],
        out_specs=[pl.BlockSpec((4, n_seq, tb, W), lambda h, n: (0, 0, last - n, h)), vec, vec, vec, one],
        out_shape=[jax.ShapeDtypeStruct((4, n_seq, S, D), BF16)] + [jax.ShapeDtypeStruct((1, D), F32)] * 3
        + [jax.ShapeDtypeStruct((1, LANES), F32)],
        scratch_shapes=[pltpu.VMEM((ns, nsub, LANES, LANES), F32), pltpu.VMEM((ns, LANES, LANES), F32)],
        compiler_params=_cp(2),
    )(proj4.reshape(4, n_seq, S, D), states, dog.reshape(n_seq, S, D), ll0, ll1, oml, onorm, *consts)
    return dproj.reshape(4, T, D), d0, d1, d2, d_on


def _place():
    x, y, c = lax.axis_index("x"), lax.axis_index("y"), lax.axis_index("c")
    chips = [(1 - x, y), (x, 1 - y), (1 - x, 1 - y)]
    return x, y, c, chips


ANY = pl.BlockSpec(memory_space=pl.ANY)


def _comm_call(name, body, ins, out_shapes, sems, aliases=None):
    return pl.pallas_call(
        body, name=name, in_specs=[ANY] * len(ins), out_specs=[ANY] * len(out_shapes),
        out_shape=out_shapes, scratch_shapes=sems, input_output_aliases=aliases or {},
        compiler_params=pltpu.CompilerParams(has_side_effects=True),
    )(*ins)


HBM_SPEC = pl.BlockSpec(memory_space=pltpu.HBM)
SEM_SPEC = pl.BlockSpec(memory_space=pltpu.SEMAPHORE)
SPLIT_EFFECT = pltpu.SideEffectType.DATAFLOW_SIDE_EFFECTING
N_PEER_CHIPS = 3


def split_start(name, build, arrays):
    n = len(arrays)

    def body(*refs):
        send, recv = refs[n], refs[n + 1]
        token = refs[2 * n + 2]
        starts, _ = build(refs[:n], send, recv)
        for cp in starts:
            cp.start()
        token[...] = jnp.zeros_like(token)

    res = pl.pallas_call(
        body, name=name,
        out_shape=(pltpu.SemaphoreType.DMA((N_PEER_CHIPS,)), pltpu.SemaphoreType.DMA((N_PEER_CHIPS,)),
                   *[pltpu.HBM(a.shape, a.dtype) for a in arrays], jax.ShapeDtypeStruct((8, LANES), F32)),
        in_specs=[HBM_SPEC] * n,
        out_specs=(SEM_SPEC, SEM_SPEC, *[HBM_SPEC] * n, pl.BlockSpec(memory_space=pltpu.VMEM)),
        input_output_aliases={i: 2 + i for i in range(n)},
        compiler_params=pltpu.CompilerParams(has_side_effects=SPLIT_EFFECT),
    )(*[pltpu.with_memory_space_constraint(a, pltpu.HBM) for a in arrays])
    return res[0], res[1], list(res[2:2 + n]), res[2 + n]


def split_wait(name, build, send, recv, arrays, after):
    n = len(arrays)

    def body(*refs):
        starts, arrivals = build(refs[:n], refs[n], refs[n + 1])
        for cp in starts:
            cp.wait_send()
        for cp in arrivals:
            cp.wait_recv()

    return list(pl.pallas_call(
        body, name=name, out_shape=tuple(pltpu.HBM(a.shape, a.dtype) for a in arrays),
        in_specs=[HBM_SPEC] * n + [SEM_SPEC, SEM_SPEC, ANY], out_specs=tuple([HBM_SPEC] * n),
        input_output_aliases={i: i for i in range(n)},
        compiler_params=pltpu.CompilerParams(has_side_effects=SPLIT_EFFECT),
    )(*arrays, send, recv, after))


def _row_half(ref, dim, who):
    rh = ref.shape[dim] // 2
    return pl.ds(who * rh, rh)


def gather_build(refs, send, recv):
    x, y, c, chips = _place()
    q = 2 * x + y
    starts, arrivals = [], []
    for buf in refs:
        rows = _row_half(buf, 2, c)
        for j, (px, py) in enumerate(chips):
            mine, got = buf.at[q, :, rows], buf.at[2 * px + py, :, rows]
            starts.append(pltpu.make_async_remote_copy(src_ref=mine, dst_ref=mine, send_sem=send.at[j], recv_sem=recv.at[j],
                                                       device_id=(px, py, c), device_id_type=MESH_ID))
            arrivals.append(pltpu.make_async_remote_copy(src_ref=got, dst_ref=got, send_sem=send.at[j], recv_sem=recv.at[j],
                                                         device_id=(px, py, c), device_id_type=MESH_ID))
    return starts, arrivals


def gather_forward(name, bufs):
    n = len(bufs)

    def body(*refs):
        dst = refs[n:2 * n]
        send, recv = refs[2 * n:]
        x, y, c, chips = _place()
        sib = (x, y, 1 - c)
        cps = []
        for a in range(n):
            for j, (px, py) in enumerate(chips):
                got = dst[a].at[2 * px + py, :, _row_half(dst[a], 2, c)]
                cps.append(pltpu.make_async_remote_copy(src_ref=got, dst_ref=got, send_sem=send.at[a, j], recv_sem=recv.at[a, j],
                                                        device_id=sib, device_id_type=MESH_ID))
        for cp in cps:
            cp.start()
        for a in range(n):
            for j, (px, py) in enumerate(chips):
                theirs = dst[a].at[2 * px + py, :, _row_half(dst[a], 2, 1 - c)]
                pltpu.make_async_remote_copy(src_ref=theirs, dst_ref=theirs, send_sem=send.at[a, j], recv_sem=recv.at[a, j],
                                             device_id=sib, device_id_type=MESH_ID).wait_recv()
        for cp in cps:
            cp.wait_send()

    outs = [jax.ShapeDtypeStruct(s.shape, s.dtype) for s in bufs]
    sems = [pltpu.SemaphoreType.DMA((n, N_PEER_CHIPS))] * 2
    return _comm_call(name, body, bufs, outs, sems, aliases={a: a for a in range(n)})


def swap_row_halves(name, grads):
    n = len(grads)

    def body(*refs):
        src, dst = refs[:n], refs[n:2 * n]
        send, recv = refs[2 * n:]
        x, y, c, _ = _place()
        cps = []
        for a in range(n):
            cps.append(pltpu.make_async_remote_copy(
                src_ref=src[a].at[:, :, _row_half(src[a], 2, 1 - c)], dst_ref=dst[a], send_sem=send.at[a], recv_sem=recv.at[a],
                device_id=(x, y, 1 - c), device_id_type=MESH_ID))
        for cp in cps:
            cp.start()
        for cp in cps:
            cp.wait()

    outs = [jax.ShapeDtypeStruct(g.shape[:2] + (g.shape[2] // 2, g.shape[3]), g.dtype) for g in grads]
    return _comm_call(name, body, grads, outs, [pltpu.SemaphoreType.DMA((n,))] * 2)


def scatter_build(refs, send, recv):
    n = len(refs) // 2
    x, y, c, chips = _place()
    starts, arrivals = [], []
    for a in range(n):
        src, dst = refs[a], refs[n + a]
        for j, (px, py) in enumerate(chips):
            starts.append(pltpu.make_async_remote_copy(src_ref=src.at[2 * px + py], dst_ref=dst.at[j], send_sem=send.at[j],
                                                       recv_sem=recv.at[j], device_id=(px, py, c), device_id_type=MESH_ID))
            arrivals.append(pltpu.make_async_remote_copy(src_ref=dst.at[j], dst_ref=dst.at[j], send_sem=send.at[j],
                                                         recv_sem=recv.at[j], device_id=(px, py, c), device_id_type=MESH_ID))
    return starts, arrivals


def join_row_halves(bufs):
    n = len(bufs)

    def body(*refs):
        dst = refs[n:2 * n]
        send, recv = refs[2 * n:]
        x, y, c, _ = _place()
        cps = []
        for a in range(n):
            mine = dst[a].at[:, _row_half(dst[a], 1, c)]
            cps.append(pltpu.make_async_remote_copy(src_ref=mine, dst_ref=mine, send_sem=send.at[a], recv_sem=recv.at[a],
                                                    device_id=(x, y, 1 - c), device_id_type=MESH_ID))
        for cp in cps:
            cp.start()
        for a in range(n):
            theirs = dst[a].at[:, _row_half(dst[a], 1, 1 - c)]
            pltpu.make_async_remote_copy(src_ref=theirs, dst_ref=theirs, send_sem=send.at[a], recv_sem=recv.at[a],
                                         device_id=(x, y, 1 - c), device_id_type=MESH_ID).wait_recv()
        for cp in cps:
            cp.wait_send()

    outs = [jax.ShapeDtypeStruct(b.shape, b.dtype) for b in bufs]
    sems = [pltpu.SemaphoreType.DMA((n,))] * 2
    return _comm_call("join_row_halves", body, bufs, outs, sems, aliases={a: a for a in range(n)})


def share_with_all(name, packed, me):
    slots = lax.dynamic_update_slice(jnp.zeros((N_DEV,) + packed.shape, packed.dtype), packed[None], (me, 0, 0))

    def body(_, dst, send, recv):
        x, y, c, _ = _place()
        me = 4 * x + 2 * y + c
        cps = []
        for k in range(1, N_DEV):
            px, py, pc = x ^ (k >> 2), y ^ ((k >> 1) & 1), c ^ (k & 1)
            cps.append(pltpu.make_async_remote_copy(src_ref=dst.at[me], dst_ref=dst.at[me], send_sem=send.at[k - 1],
                                                    recv_sem=recv.at[k - 1], device_id=(px, py, pc), device_id_type=MESH_ID))
        for cp in cps:
            cp.start()
        for k in range(1, N_DEV):
            px, py, pc = x ^ (k >> 2), y ^ ((k >> 1) & 1), c ^ (k & 1)
            got = dst.at[4 * px + 2 * py + pc]
            pltpu.make_async_remote_copy(src_ref=got, dst_ref=got, send_sem=send.at[k - 1], recv_sem=recv.at[k - 1],
                                         device_id=(px, py, pc), device_id_type=MESH_ID).wait_recv()
        for cp in cps:
            cp.wait_send()

    outs = [jax.ShapeDtypeStruct(slots.shape, slots.dtype)]
    sems = [pltpu.SemaphoreType.DMA((N_DEV - 1,))] * 2
    return _comm_call(name, body, [slots], outs, sems, aliases={0: 0})[0]


def _w_tiles(R, C):
    return _tile(R, max(16, (1 << 20) // (4 * C) // 16 * 16), 16)


def cast_bf16(w, l, q_arr):
    _, R, C = w.shape
    tr = _w_tiles(R, C)
    ins = [(w, (None, tr, C), lambda r, q: (l, r, 0))]
    outs = [((N_CHIPS, 1, R, C), BF16, (None, None, tr, C), lambda r, q: (q[0], 0, r, 0), None)]
    return bmap_fwd("cast_bf16", lambda a: (a,), (R // tr,), ins, outs, scalars=(q_arr,))[0]


def add_sibling_half(g, b1, c_arr):
    Q, _, rh, C = b1.shape
    tr = _w_tiles(rh, C)
    nb = rh // tr
    blk = (None, None, tr, C)
    ins = [(g, blk, lambda q, r, c: (q, 0, c[0] * nb + r, 0)), (b1, blk, lambda q, r, c: (q, 0, r, 0))]
    outs = [(b1.shape, BF16, blk, lambda q, r, c: (q, 0, r, 0), None)]
    return bmap_fwd("add_sibling_half", lambda a, b: (a + b,), (Q, nb), ins, outs, scalars=(c_arr,))[0]


def sum_chips(pair, b2, into, l, q_arr, c_arr):
    _, _, rh, C = b2.shape
    tr = _w_tiles(rh, C)
    nb = rh // tr
    blk = (None, None, tr, C)
    ins = [(pair, blk, lambda r, q, c: (q[0], 0, r, 0))]
    ins += [(b2, blk, (lambda r, q, c, jj=jj: (jj, 0, r, 0))) for jj in range(3)]
    outs = [(into.shape, F32, (None, tr, C), lambda r, q, c: (l, c[0] * nb + r, 0), None)]
    return bmap_fwd("sum_chips", lambda a, b, c, d: (((a + b) + c) + d,), (nb,), ins, outs,
                    scalars=(q_arr, c_arr), into=into)[0]


def sum_devices(slots):
    nd, NR, C = slots.shape
    tr = _tile(NR, 512, 8)
    ins = [(slots, (None, tr, C), (lambda r, dd=dd: (dd, r, 0))) for dd in range(nd)]
    outs = [((NR, C), F32, (tr, C), lambda r: (r, 0), None)]
    return bmap_fwd("sum_devices", lambda *a: (functools.reduce(lambda u, v: u + v, a),), (NR // tr,), ins, outs)[0]


def adamw(name, w, g, m, v):
    if w.ndim == 2:
        R, C = w.shape
        tr = _w_tiles(R, C)
        spec = ((tr, C), lambda r: (r, 0))
        grid = (R // tr,)
    else:
        L, R, C = w.shape
        tr = _w_tiles(R, C)
        spec = ((None, tr, C), lambda l, r: (l, r, 0))
        grid = (L, R // tr)
    ins = [(a,) + spec for a in (w, g, m, v)]
    outs = [(w.shape, F32) + spec + (None,)] * 3
    return bmap_fwd(name, f_adam, grid, ins, outs)


def loss_and_grad(h, target):
    T, D = h.shape
    tm = _tile(T, 256, 8)

    def fn(hv, tv):
        d = hv - tv
        return jnp.sum(d * d, keepdims=True).reshape(1, 1) * (0.5 / D), d * (1.0 / D)

    ins = [(h, (tm, D), lambda i: (i, 0)), (target, (tm, D), lambda i: (i, 0))]
    outs = [((1, 1), F32, (1, 1), lambda i: (0, 0), 0), ((T, D), F32, (tm, D), lambda i: (i, 0), None)]
    return bmap_fwd("loss_and_grad", fn, (T // tm,), ins, outs)


BIG = ("hg_w_in", "hg_w_out", "gm_w_in", "gm_w_out", "ffn_w_gate", "ffn_w_up", "ffn_w_down", "ple_w_proj", "ple_w_gate")
KIND = {"hg_w_in": "col", "hg_w_out": "row", "gm_w_in": "col", "gm_w_out": "row", "ffn_w_gate": "col",
        "ffn_w_up": "col", "ffn_w_down": "row", "ple_w_proj": "col", "ple_w_gate": "row"}
SMALL = ("hg_lb_logits", "hg_out_norm", "gm_ln_g", "gm_ln_b", "gm_w_s", "gm_b_s", "norm_mix_pre", "norm_mix_post",
         "norm_ffn_pre", "norm_ffn_post", "ple_norm")
WEIGHTS = ("hg_w_in", "hg_lb_logits", "hg_out_norm", "hg_w_out", "gm_w_in", "gm_ln_g", "gm_ln_b", "gm_w_s", "gm_b_s",
           "gm_w_out", "norm_mix_pre", "norm_mix_post", "norm_ffn_pre", "norm_ffn_post", "ffn_w_gate", "ffn_w_up",
           "ffn_w_down", "ple_w_proj", "ple_w_gate", "ple_norm")


def _pack(arrs):
    rows = []
    for a in arrs:
        flat = a.reshape(-1)
        pad = (-flat.shape[0]) % (8 * LANES)
        rows.append(jnp.pad(flat, (0, pad)).reshape(-1, LANES))
    n_rows = sum(r.shape[0] for r in rows)
    rows.append(jnp.zeros(((-n_rows) % PACK_ROWS, LANES), F32))
    return jnp.concatenate(rows, axis=0)


def _unpack(packed, shapes):
    out, r = [], 0
    for s in shapes:
        size = 1
        for d in s:
            size *= d
        nr = -(-size // (8 * LANES)) * 8
        out.append(packed[r:r + nr].reshape(-1)[:size].reshape(s))
        r += nr
    return out


def _step(x, p, W, M, V, loss_target):
    n_seq, S, D = x.shape
    T = n_seq * S
    depth = p.shape[0]
    n_hg = W["hg_w_in"].shape[0]
    x2 = x.reshape(T, D)
    p3 = p.reshape(depth, T, p.shape[-1])
    tgt = loss_target.reshape(T, D)
    xi, yi, ci = lax.axis_index("x"), lax.axis_index("y"), lax.axis_index("c")
    q_me = 2 * xi + yi
    c_arr = jnp.reshape(ci, (1,)).astype(jnp.int32)
    q_arr = jnp.reshape(q_me, (1,)).astype(jnp.int32)

    groups = {}
    for i in range(depth):
        mix = ("hg_w_in", "hg_w_out") if i % 2 == 0 else ("gm_w_in", "gm_w_out")
        groups[i, "mix"] = [(k, i // 2) for k in mix]
        groups[i, "rest"] = [(k, i) for k in ("ffn_w_gate", "ffn_w_up", "ffn_w_down", "ple_w_proj", "ple_w_gate")]
    G = {k: {} for k in BIG}
    DW = {k: {l: lax.empty((N_CHIPS, 1) + W[k].shape[1:], BF16) for l in range(W[k].shape[0])} for k in BIG}
    in_flight = {}

    def start_gather(i, part, dep):
        qa = q_arr if dep is None else lax.optimization_barrier((q_arr, dep))[0]
        casts = [cast_bf16(W[k], l, qa) for k, l in groups[i, part]]
        send, recv, arrs, tok = split_start("gather_start_%d_%s" % (i, part), gather_build, casts)
        in_flight[i, part] = (send, recv, arrs)
        return tok

    def finish_gather(i, part, after):
        send, recv, arrs = in_flight.pop((i, part))
        arrs = split_wait("gather_wait_%d_%s" % (i, part), gather_build, send, recv, arrs, after)
        for (k, l), buf in zip(groups[i, part], gather_forward("gather_forward_%d_%s" % (i, part), arrs)):
            G[k][l] = buf
        return buf

    def after_token(row_arr, *toks):
        return functools.reduce(lambda u, t: u + t[0:1, 0:1], toks, row_arr)

    tok_mix = start_gather(0, "mix", None)
    tok_rest = start_gather(0, "rest", tok_mix)
    finish_gather(0, "mix", tok_mix + tok_rest)
    me = 4 * xi + 2 * yi + ci
    ln_full = share_with_all("share_ln", _pack([W["gm_ln_g"], W["gm_ln_b"]]), me)
    n_gm, dq = W["gm_ln_g"].shape
    ln_parts = [_unpack(ln_full[4 * qx + 2 * qy + 0], [(n_gm, dq), (n_gm, dq)]) for qx in range(2) for qy in range(2)]
    ln_g = jnp.concatenate([lp[0] for lp in ln_parts], axis=1)
    ln_b = jnp.concatenate([lp[1] for lp in ln_parts], axis=1)

    row = lambda a, i: a[i][None, :]
    f_lb = _make_f_lb(n_hg)
    lb_rows = [row(W["hg_lb_logits"], j) for j in range(n_hg)]
    one = (1, D)
    lb_ins = [(r, one, lambda i: (0, 0)) for r in lb_rows]
    lb_out = bmap_fwd("hg_lower_bounds", f_lb, (1,), lb_ins, [(one, F32, one, lambda i: (0, 0), None)] * (3 * n_hg))

    saved = []
    h = x2
    a = rows_fwd("prenorm", f_prenorm, [h], [row(W["norm_mix_pre"], 0)], [BF16])[0]
    for i in range(depth):
        j = i // 2
        sv = {"h": h, "a": a}
        if i > 0:
            finish_gather(i, "mix", h)
        if i % 2 == 0:
            proj4 = mm_fwd("hg_in", a, G["hg_w_in"], j, "col", parts=True)
            lbp = lb_out[3 * j:3 * j + 3]
            onorm = row(W["hg_out_norm"], j)
            og, states = hgrn_fwd("hgrn_fwd", proj4, *lbp, onorm, n_seq)
            m = mm_fwd("hg_out", og, G["hg_w_out"], j, "row")
            sv.update(proj4=proj4, states=states, og=og, lbp=lbp, onorm=onorm)
        else:
            z = mm_fwd("gm_in", a, G["gm_w_in"], j, "col")
            lg, lb_ = row(ln_g, j), row(ln_b, j)
            u, vn = rows_fwd("gm_gelu_ln", f_gm_in, [z], [lg, lb_], [F32, BF16], tm=128)
            ws = W["gm_w_s"][j]
            bs = W["gm_b_s"][j][:, :, None]
            gb = min(GM_BLOCK, S)
            sp_grid = (D // LANES, T // gb)
            sp_ins = [(u, (gb, LANES), lambda g, n: (n, g)), (vn, (gb, LANES), lambda g, n: (n, g)),
                      (ws, (None, GM_CHUNK, GM_CHUNK), lambda g, n: (g, 0, 0)),
                      (bs, (None, GM_CHUNK, 1), lambda g, n: (g, 0, 0))]
            y = bmap_fwd("gm_spatial", f_gm_spatial, sp_grid, sp_ins,
                         [((T, D), BF16, (gb, LANES), lambda g, n: (n, g), None)])[0]
            m = mm_fwd("gm_out", y, G["gm_w_out"], j, "row")
            sv.update(z=z, lg=lg, lb_=lb_, sp_ins=sp_ins, sp_grid=sp_grid, y=y)
        g_post, g_fpre = row(W["norm_mix_post"], i), row(W["norm_ffn_pre"], i)
        arrived = finish_gather(i, "rest", m)
        if i + 1 < depth:
            tok_mix = start_gather(i + 1, "mix", arrived)
            g_post = after_token(g_post, tok_mix, start_gather(i + 1, "rest", tok_mix))
        h1, fin = rows_fwd("mix_post_ffn_pre", f_post_pre, [h, m], [g_post, g_fpre], [F32, BF16])
        gate = mm_fwd("ffn_gate", fin, G["ffn_w_gate"], i, "col")
        up = mm_fwd("ffn_up", fin, G["ffn_w_up"], i, "col")
        act = rows_fwd("swiglu", f_swiglu, [gate, up], [], [BF16], tm=128)[0]
        f = mm_fwd("ffn_down", act, G["ffn_w_down"], i, "row")
        g_fpost = row(W["norm_ffn_post"], i)
        h2 = rows_fwd("ffn_post", f_post, [h1, f], [g_fpost], [F32])[0]
        e = mm_fwd("ple_proj", p3, G["ple_w_proj"], i, "col", xl=i)
        zg = mm_fwd("ple_gate", h2, G["ple_w_gate"], i, "row")
        g_ple = row(W["ple_norm"], i)
        sv.update(m=m, h1=h1, fin=fin, gate=gate, up=up, act=act, f=f, h2=h2, e=e, zg=zg,
                  g_post=g_post, g_fpre=g_fpre, g_fpost=g_fpost, g_ple=g_ple)
        if i + 1 < depth:
            g_next = row(W["norm_mix_pre"], i + 1)
            h, a = rows_fwd("ple_next_pre", f_ple_pre, [h2, e, zg], [g_ple, g_next], [F32, BF16])
            sv["g_next"] = g_next
        else:
            h = rows_fwd("ple_last", f_ple, [h2, e, zg], [g_ple], [F32])[0]
        saved.append(sv)

    loss_part, dh = loss_and_grad(h, tgt)
    loss = lax.psum(loss_part[0, 0], ("x", "y", "c"))

    sg = {k: [None] * W[k].shape[0] for k in ("norm_mix_pre", "norm_mix_post", "norm_ffn_pre", "norm_ffn_post", "ple_norm",
                                              "hg_out_norm", "gm_ln_g", "gm_ln_b", "gm_w_s", "gm_b_s")}
    d_lbp = [None] * (3 * n_hg)
    da_next = None
    GRAD = {k: lax.empty(W[k].shape, F32) for k in BIG}
    scattering = {}

    def start_scatter(i, part):
        tag = "%d_%s" % (i, part)
        dws = [DW[k][l] for k, l in groups[i, part]]
        pairs = [add_sibling_half(g, b, c_arr) for g, b in zip(dws, swap_row_halves("swap_" + tag, dws))]
        lands = [lax.empty((N_PEER_CHIPS,) + pr.shape[1:], BF16) for pr in pairs]
        send, recv, arrs, tok = split_start("scatter_start_" + tag, scatter_build, pairs + lands)
        scattering[i, part] = (send, recv, arrs)
        return tok

    def finish_scatter(i, part, after):
        send, recv, arrs = scattering.pop((i, part))
        arrs = split_wait("scatter_wait_%d_%s" % (i, part), scatter_build, send, recv, arrs, after)
        n = len(groups[i, part])
        for (k, l), pr, ld in zip(groups[i, part], arrs[:n], arrs[n:]):
            GRAD[k] = sum_chips(pr, ld, GRAD[k], l, q_arr, c_arr)

    tok = None
    for i in reversed(range(depth)):
        j = i // 2
        sv = saved[i]
        if i + 1 < depth:
            g_ple_after = sv["g_ple"] + tok[0:1, 0:1]
            dh2, de, dzg, d_gple, d_gnext = rows_bwd("ple_next_pre_bwd", f_ple_pre, [sv["h2"], sv["e"], sv["zg"]],
                                                     [g_ple_after, sv["g_next"]], [dh, da_next], [F32, BF16, BF16])
            sg["norm_mix_pre"][i + 1] = d_gnext
        else:
            dh2, de, dzg, d_gple = rows_bwd("ple_last_bwd", f_ple, [sv["h2"], sv["e"], sv["zg"]], [sv["g_ple"]], [dh],
                                            [F32, BF16, BF16])
        sg["ple_norm"][i] = d_gple
        DW["ple_w_proj"] = mm_bwd_w("ple_proj_dw", p3, de, DW["ple_w_proj"], i, "col", xl=i)
        DW["ple_w_gate"] = mm_bwd_w("ple_gate_dw", sv["h2"], dzg, DW["ple_w_gate"], i, "row")
        dh2 = mm_bwd_x("ple_gate_dx", dzg, G["ple_w_gate"], i, "row", addend=dh2)
        dh1, df, d_gfpost = rows_bwd("ffn_post_bwd", f_post, [sv["h1"], sv["f"]], [sv["g_fpost"]], [dh2], [F32, BF16])
        sg["norm_ffn_post"][i] = d_gfpost
        dact = mm_bwd_x("ffn_down_dx", df, G["ffn_w_down"], i, "row")
        DW["ffn_w_down"] = mm_bwd_w("ffn_down_dw", sv["act"], df, DW["ffn_w_down"], i, "row")
        dgate, dup = rows_bwd("swiglu_bwd", f_swiglu, [sv["gate"], sv["up"]], [], [dact], [BF16, BF16], tm=128)
        dfin = mm_bwd_x("ffn_gate_dx", dgate, G["ffn_w_gate"], i, "col")
        dfin = mm_bwd_x("ffn_up_dx", dup, G["ffn_w_up"], i, "col", addend=dfin)
        DW["ffn_w_gate"] = mm_bwd_w("ffn_gate_dw", sv["fin"], dgate, DW["ffn_w_gate"], i, "col")
        DW["ffn_w_up"] = mm_bwd_w("ffn_up_dw", sv["fin"], dup, DW["ffn_w_up"], i, "col")
        g_post_after = after_token(sv["g_post"], start_scatter(i, "rest"))
        dh, dm, d_gpost, d_gfpre = rows_bwd("mix_post_ffn_pre_bwd", f_post_pre, [sv["h"], sv["m"]],
                                            [g_post_after, sv["g_fpre"]], [dh1, dfin], [F32, BF16])
        sg["norm_mix_post"][i], sg["norm_ffn_pre"][i] = d_gpost, d_gfpre
        if i % 2 == 0:
            dog = mm_bwd_x("hg_out_dx", dm, G["hg_w_out"], j, "row")
            DW["hg_w_out"] = mm_bwd_w("hg_out_dw", sv["og"], dm, DW["hg_w_out"], j, "row")
            dproj4, d0, d1, d2, d_on = hgrn_bwd("hgrn_bwd", sv["proj4"], sv["states"], dog, *sv["lbp"], sv["onorm"], n_seq)
            d_lbp[3 * j:3 * j + 3] = [d0, d1, d2]
            sg["hg_out_norm"][j] = d_on
            da_next = mm_bwd_x("hg_in_dx", dproj4, G["hg_w_in"], j, "col", parts=True)
            DW["hg_w_in"] = mm_bwd_w("hg_in_dw", sv["a"], dproj4, DW["hg_w_in"], j, "col", parts=True)
        else:
            dy = mm_bwd_x("gm_out_dx", dm, G["gm_w_out"], j, "row")
            DW["gm_w_out"] = mm_bwd_w("gm_out_dw", sv["y"], dm, DW["gm_w_out"], j, "row")
            gb = sv["sp_ins"][0][1][0]
            du, dvn, dws, dbs = bmap_bwd("gm_spatial_bwd", f_gm_spatial, sv["sp_grid"], sv["sp_ins"],
                                         [(dy, (gb, LANES), lambda g, n: (n, g))],
                                         [(0, F32, None), (1, F32, None), (2, F32, 1), (3, F32, 1)])
            sg["gm_w_s"][j], sg["gm_b_s"][j] = dws, dbs[:, :, 0]
            dz, d_lg, d_lb = rows_bwd("gm_gelu_ln_bwd", f_gm_in, [sv["z"]], [sv["lg"], sv["lb_"]], [du, dvn], [BF16], tm=128)
            sg["gm_ln_g"][j], sg["gm_ln_b"][j] = d_lg, d_lb
            da_next = mm_bwd_x("gm_in_dx", dz, G["gm_w_in"], j, "col")
            DW["gm_w_in"] = mm_bwd_w("gm_in_dw", sv["a"], dz, DW["gm_w_in"], j, "col")
        tok = start_scatter(i, "mix")
        if i + 1 < depth:
            finish_scatter(i + 1, "rest", da_next)
            finish_scatter(i + 1, "mix", da_next)
    finish_scatter(0, "rest", tok)
    g0 = after_token(row(W["norm_mix_pre"], 0), tok)
    grad_x, d_g0 = rows_bwd("prenorm_bwd", f_prenorm_thru, [saved[0]["h"]], [g0], [da_next, dh], [F32])
    sg["norm_mix_pre"][0] = d_g0
    d_logits = bmap_bwd("hg_lower_bounds_bwd", f_lb, (1,), lb_ins, [(d, one, lambda i: (0, 0)) for d in d_lbp],
                        [(jj, F32, None) for jj in range(n_hg)])

    small_g = {k: jnp.stack([v.reshape(W[k].shape[1:] if k not in ("gm_ln_g", "gm_ln_b") else (D,)) for v in sg[k]])
               for k in sg}
    small_g["hg_lb_logits"] = jnp.concatenate(d_logits, axis=0)
    small_shapes = [small_g[k].shape for k in SMALL]
    red = sum_devices(share_with_all("share_small_grads", _pack([small_g[k] for k in SMALL]), me))
    small_red = dict(zip(SMALL, _unpack(red, small_shapes)))
    for k in ("gm_ln_g", "gm_ln_b"):
        small_red[k] = lax.dynamic_slice_in_dim(small_red[k], q_me * dq, dq, axis=1)
    pk = lambda d: _pack([d[k] for k in SMALL])
    s_delta, s_m, s_v = adamw("adamw_small", pk(W), pk(small_red), pk(M), pk(V))
    shard_shapes = [W[k].shape for k in SMALL]
    out_g, out_d, out_m, out_v = dict(small_red), {}, {}, {}
    for dct, packed in ((out_d, s_delta), (out_m, s_m), (out_v, s_v)):
        dct.update(zip(SMALL, _unpack(packed, shard_shapes)))

    finish_scatter(0, "mix", s_v)
    full = join_row_halves([GRAD[k] for k in BIG])
    for k, g in zip(BIG, full):
        out_g[k] = g
        out_d[k], out_m[k], out_v[k] = adamw("adamw_" + k, W[k], g, M[k], V[k])

    outs = [loss, grad_x.reshape(x.shape)]
    for dct in (out_g, out_d, out_m, out_v):
        outs += [dct[k] for k in WEIGHTS]
    return tuple(outs)


def kernel(x, p, hg_w_in, hg_lb_logits, hg_out_norm, hg_w_out, gm_w_in, gm_ln_g, gm_ln_b, gm_w_s, gm_b_s, gm_w_out, norm_mix_pre, norm_mix_post, norm_ffn_pre, norm_ffn_post, ffn_w_gate, ffn_w_up, ffn_w_down, ple_w_proj, ple_w_gate, ple_norm, loss_target, m_hg_w_in, m_hg_lb_logits, m_hg_out_norm, m_hg_w_out, m_gm_w_in, m_gm_ln_g, m_gm_ln_b, m_gm_w_s, m_gm_b_s, m_gm_w_out, m_norm_mix_pre, m_norm_mix_post, m_norm_ffn_pre, m_norm_ffn_post, m_ffn_w_gate, m_ffn_w_up, m_ffn_w_down, m_ple_w_proj, m_ple_w_gate, m_ple_norm, v_hg_w_in, v_hg_lb_logits, v_hg_out_norm, v_hg_w_out, v_gm_w_in, v_gm_ln_g, v_gm_ln_b, v_gm_w_s, v_gm_b_s, v_gm_w_out, v_norm_mix_pre, v_norm_mix_post, v_norm_ffn_pre, v_norm_ffn_post, v_ffn_w_gate, v_ffn_w_up, v_ffn_w_down, v_ple_w_proj, v_ple_w_gate, v_ple_norm):
    W = dict(zip(WEIGHTS, (hg_w_in, hg_lb_logits, hg_out_norm, hg_w_out, gm_w_in, gm_ln_g, gm_ln_b, gm_w_s, gm_b_s, gm_w_out,
                           norm_mix_pre, norm_mix_post, norm_ffn_pre, norm_ffn_post, ffn_w_gate, ffn_w_up, ffn_w_down,
                           ple_w_proj, ple_w_gate, ple_norm)))
    M = dict(zip(WEIGHTS, (m_hg_w_in, m_hg_lb_logits, m_hg_out_norm, m_hg_w_out, m_gm_w_in, m_gm_ln_g, m_gm_ln_b, m_gm_w_s,
                           m_gm_b_s, m_gm_w_out, m_norm_mix_pre, m_norm_mix_post, m_norm_ffn_pre, m_norm_ffn_post,
                           m_ffn_w_gate, m_ffn_w_up, m_ffn_w_down, m_ple_w_proj, m_ple_w_gate, m_ple_norm)))
    V = dict(zip(WEIGHTS, (v_hg_w_in, v_hg_lb_logits, v_hg_out_norm, v_hg_w_out, v_gm_w_in, v_gm_ln_g, v_gm_ln_b, v_gm_w_s,
                           v_gm_b_s, v_gm_w_out, v_norm_mix_pre, v_norm_mix_post, v_norm_ffn_pre, v_norm_ffn_post,
                           v_ffn_w_gate, v_ffn_w_up, v_ffn_w_down, v_ple_w_proj, v_ple_w_gate, v_ple_norm)))
    return _step(x, p, W, M, V, loss_target)
```

```python
import functools

import jax
import jax.numpy as jnp
from jax import lax
from jax.experimental import pallas as pl
from jax.experimental.pallas import tpu as pltpu

F32 = jnp.float32
BF16 = jnp.bfloat16
MESH_ID = pl.DeviceIdType.MESH

LANES = 128
N_CHIPS = 4
N_DEV = 8
VMEM_LIMIT = 56 * 1024 * 1024
HG_SUB = 64
HG_BLOCK = 256
HG_HEADS_PER = 2
GM_CHUNK = 128
GM_BLOCK = 512
PACK_ROWS = 512
LB_FLOOR = 1e-30
EPS = 1e-6
ADAM_LR, ADAM_B1, ADAM_B2, ADAM_EPS, ADAM_WD, ADAM_STEP = 0.001, 0.9, 0.999, 1e-08, 0.01, 10


def _tile(n, pref, mult=LANES):
    if n <= pref:
        return n
    t = (pref // mult) * mult
    while t >= mult:
        if n % t == 0:
            return t
        t -= mult
    return n


def _cp(n_axes):
    return pltpu.CompilerParams(dimension_semantics=("arbitrary",) * n_axes, vmem_limit_bytes=VMEM_LIMIT)


def _dense(block):
    return tuple(b for b in block if b is not None)


def _bmap(name, grid, ins, outs, compute, scalars=(), into=None):
    n_s, n_in = len(scalars), len(ins)
    n_extra = 0 if into is None else 1

    def body(*refs):
        in_refs = refs[n_s:n_s + n_in]
        out_refs = refs[n_s + n_in + n_extra:]
        vals = [r[...] for r in in_refs]
        res = compute(*vals)
        for r, o, spec in zip(out_refs, res, outs):
            keep = spec[4]
            if keep is None:
                r[...] = o.astype(r.dtype)
            else:
                first = functools.reduce(jnp.logical_and, [pl.program_id(a) == 0 for a in range(keep, len(grid))])

                @pl.when(first)
                def _():
                    r[...] = jnp.zeros(r.shape, r.dtype)

                r[...] += o.astype(r.dtype)

    grid_spec = pltpu.PrefetchScalarGridSpec(
        num_scalar_prefetch=n_s, grid=grid,
        in_specs=[pl.BlockSpec(b, m) for _, b, m in ins] + [pl.BlockSpec(memory_space=pl.ANY)] * n_extra,
        out_specs=[pl.BlockSpec(o[2], o[3]) for o in outs])
    return pl.pallas_call(
        body, name=name, grid_spec=grid_spec,
        out_shape=[jax.ShapeDtypeStruct(o[0], o[1]) for o in outs],
        input_output_aliases={n_s + n_in: 0} if n_extra else {},
        compiler_params=_cp(len(grid)),
    )(*scalars, *[a for a, _, _ in ins], *([into] if n_extra else []))


def bmap_fwd(name, fn, grid, ins, outs, scalars=(), into=None):
    return _bmap(name, grid, ins, outs, lambda *v: fn(*[x.astype(F32) for x in v]), scalars, into)


def bmap_bwd(name, fn, grid, ins, cots, grads, scalars=()):
    n_in = len(ins)
    diff = [g[0] for g in grads]
    cot_ins = [c for c in cots if c is not None]

    def compute(*vals):
        xs = [v.astype(F32) for v in vals[:n_in]]
        cvals = list(vals[n_in:])

        def f(*d):
            full = list(xs)
            for i, dv in zip(diff, d):
                full[i] = dv
            return tuple(fn(*full))

        res, pull = jax.vjp(f, *[xs[i] for i in diff])
        cts = []
        for r, c in zip(res, cots):
            cts.append(jnp.zeros_like(r) if c is None else cvals.pop(0).astype(F32))
        return pull(tuple(cts))

    outs = [(ins[i][0].shape, dt, ins[i][1], ins[i][2], keep) for i, dt, keep in grads]
    return _bmap(name, grid, list(ins) + cot_ins, outs, compute, scalars)


def _mm(name, a, b, out_shape, out_dtype, grid, a_spec, b_spec, o_spec, dims, addend=None, alias_out=None):
    nk = grid[2]
    o_dense = _dense(o_spec[0])
    has_add = addend is not None
    has_alias = alias_out is not None

    def body(*refs):
        a_ref, b_ref = refs[0], refs[1]
        pos = 2
        c_ref = None
        if has_add:
            c_ref = refs[pos]
            pos += 1
        if has_alias:
            pos += 1
        o_ref = refs[pos]
        acc_ref = refs[pos + 1] if nk > 1 else None
        p = lax.dot_general(a_ref[...].astype(BF16), b_ref[...].astype(BF16), (dims, ((), ())),
                            preferred_element_type=F32)

        def finish(total):
            if has_add:
                total = total + c_ref[...].astype(F32)
            o_ref[...] = total.astype(o_ref.dtype)

        if nk == 1:
            finish(p)
        else:
            k = pl.program_id(2)

            @pl.when(k == 0)
            def _():
                acc_ref[...] = p

            @pl.when(jnp.logical_and(k > 0, k < nk - 1))
            def _():
                acc_ref[...] += p

            @pl.when(k == nk - 1)
            def _():
                finish(acc_ref[...] + p)

    in_specs = [pl.BlockSpec(*a_spec), pl.BlockSpec(*b_spec)]
    operands = [a, b]
    if has_add:
        in_specs.append(pl.BlockSpec(o_spec[0], o_spec[1]))
        operands.append(addend)
    aliases = {}
    if has_alias:
        in_specs.append(pl.BlockSpec(memory_space=pl.ANY))
        aliases = {len(operands): 0}
        operands.append(alias_out)
    return pl.pallas_call(
        body, name=name, grid=grid, in_specs=in_specs, out_specs=pl.BlockSpec(*o_spec),
        out_shape=jax.ShapeDtypeStruct(out_shape, out_dtype),
        scratch_shapes=[pltpu.VMEM(o_dense, F32)] if nk > 1 else [],
        input_output_aliases=aliases,
        compiler_params=pltpu.CompilerParams(dimension_semantics=("parallel", "parallel", "arbitrary"),
                                             vmem_limit_bytes=VMEM_LIMIT),
    )(*operands)


NN, NT, TN = ((1,), (0,)), ((1,), (1,)), ((0,), (0,))
TM = 512
TT = 1024
TN_PREF = 1408


def mm_fwd(name, x, wg, l, kind, out_dtype=F32, parts=False, xl=None):
    if isinstance(wg, dict):
        wg, l = wg[l], 0
    _, _, R, C = wg.shape
    T = x.shape[-2]
    tm = _tile(T, TM, 8)
    if kind == "col":
        tn = _tile(C, TN_PREF)
        npc = C // tn
        grid = (T // tm, N_CHIPS * npc, 1)
        a_blk = (tm, R) if xl is None else (None, tm, R)
        a_map = (lambda i, j, k: (i, 0)) if xl is None else (lambda i, j, k: (xl, i, 0))
        b_spec = ((None, None, R, tn), lambda i, j, k: (j // npc, l, 0, j % npc))
        if parts:
            out_shape = (N_CHIPS, T, C)
            o_spec = ((None, tm, tn), lambda i, j, k: (j // npc, i, j % npc))
        else:
            out_shape = (T, N_CHIPS * C)
            o_spec = ((tm, tn), lambda i, j, k: (i, j))
    else:
        tn = _tile(C, 1024)
        grid = (T // tm, C // tn, N_CHIPS)
        a_blk = (tm, R)
        a_map = lambda i, j, k: (i, k)
        b_spec = ((None, None, R, tn), lambda i, j, k: (k, l, 0, j))
        out_shape = (T, C)
        o_spec = ((tm, tn), lambda i, j, k: (i, j))
    return _mm(name, x, wg, out_shape, out_dtype, grid, (a_blk, a_map), b_spec, o_spec, NN)


def mm_bwd_x(name, dy, wg, l, kind, out_dtype=F32, parts=False, addend=None):
    if isinstance(wg, dict):
        wg, l = wg[l], 0
    _, _, R, C = wg.shape
    T = dy.shape[-2]
    tm = _tile(T, TM, 8)
    if kind == "col":
        tk = _tile(C, TN_PREF)
        npc = C // tk
        tno = _tile(R, 2048)
        grid = (T // tm, R // tno, N_CHIPS * npc)
        if parts:
            a_spec = ((None, tm, tk), lambda i, j, k: (k // npc, i, k % npc))
        else:
            a_spec = ((tm, tk), lambda i, j, k: (i, k))
        b_spec = ((None, None, tno, tk), lambda i, j, k: (k // npc, l, j, k % npc))
        out_shape = (T, R)
        o_spec = ((tm, tno), lambda i, j, k: (i, j))
    else:
        grid = (T // tm, N_CHIPS, 1)
        a_spec = ((tm, C), lambda i, j, k: (i, 0))
        b_spec = ((None, None, R, C), lambda i, j, k: (j, l, 0, 0))
        out_shape = (T, N_CHIPS * R)
        o_spec = ((tm, R), lambda i, j, k: (i, j))
    return _mm(name, dy, wg, out_shape, out_dtype, grid, a_spec, b_spec, o_spec, NT, addend=addend)


def mm_bwd_w(name, x, dy, dwg, l, kind, parts=False, xl=None):
    if isinstance(dwg, dict):
        return {**dwg, l: mm_bwd_w(name, x, dy, dwg[l], 0, kind, parts=parts, xl=xl)}
    _, _, R, C = dwg.shape
    T = dy.shape[-2]
    tt = _tile(T, TT, 16)
    nt = T // tt
    if kind == "col":
        tn = _tile(C, TN_PREF)
        npc = C // tn
        tr = _tile(R, 1024)
        grid = (R // tr, N_CHIPS * npc, nt)
        if xl is None:
            a_spec = ((tt, tr), lambda i, j, t: (t, i))
        else:
            a_spec = ((None, tt, tr), lambda i, j, t: (xl, t, i))
        if parts:
            b_spec = ((None, tt, tn), lambda i, j, t: (j // npc, t, j % npc))
        else:
            b_spec = ((tt, tn), lambda i, j, t: (t, j))
        o_spec = ((None, None, tr, tn), lambda i, j, t: (j // npc, l, i, j % npc))
    else:
        tn = _tile(C, 1024)
        grid = (N_CHIPS, C // tn, nt)
        a_spec = ((tt, R), lambda i, j, t: (t, i))
        b_spec = ((tt, tn), lambda i, j, t: (t, j))
        o_spec = ((None, None, R, tn), lambda i, j, t: (i, l, 0, j))
    return _mm(name, x, dy, dwg.shape, dwg.dtype, grid, a_spec, b_spec, o_spec, TN, alias_out=dwg)


def _sigmoid(x):
    return 0.5 * jnp.tanh(0.5 * x) + 0.5


def f_swiglu(gate, up):
    return (gate * _sigmoid(gate) * up,)


def ffn_gate_up(name, x, wg_gate, wg_up, l):
    if isinstance(wg_gate, dict):
        wg_gate, wg_up, l = wg_gate[l], wg_up[l], 0
    _, _, R, C = wg_gate.shape
    T = x.shape[0]
    tm = _tile(T, TM, 8)
    tn = _tile(C, TN_PREF)
    npc = C // tn

    def body(x_ref, g_ref, u_ref, gate_ref, up_ref, act_ref):
        xv = x_ref[...].astype(BF16)
        gate = jnp.dot(xv, g_ref[...], preferred_element_type=F32)
        up = jnp.dot(xv, u_ref[...], preferred_element_type=F32)
        gate_ref[...] = gate
        up_ref[...] = up
        act_ref[...] = f_swiglu(gate, up)[0].astype(act_ref.dtype)

    w_spec = pl.BlockSpec((None, None, R, tn), lambda i, j: (j // npc, l, 0, j % npc))
    o_spec = pl.BlockSpec((tm, tn), lambda i, j: (i, j))
    N = N_CHIPS * C
    return pl.pallas_call(
        body, name=name, grid=(T // tm, N_CHIPS * npc),
        in_specs=[pl.BlockSpec((tm, R), lambda i, j: (i, 0)), w_spec, w_spec], out_specs=[o_spec, o_spec, o_spec],
        out_shape=[jax.ShapeDtypeStruct((T, N), F32), jax.ShapeDtypeStruct((T, N), F32), jax.ShapeDtypeStruct((T, N), BF16)],
        compiler_params=_cp(2),
    )(x, wg_gate, wg_up)


def ffn_down_dx(name, df, wg_down, l, gate, up):
    if isinstance(wg_down, dict):
        wg_down, l = wg_down[l], 0
    _, _, R, C = wg_down.shape
    T = df.shape[0]
    tm = _tile(T, TM, 8)

    def body(df_ref, w_ref, gate_ref, up_ref, dg_ref, du_ref):
        dact = lax.dot_general(df_ref[...].astype(BF16), w_ref[...], (NT, ((), ())), preferred_element_type=F32)
        _, pull = jax.vjp(lambda g, u: f_swiglu(g, u)[0], gate_ref[...], up_ref[...])
        dg, du = pull(dact)
        dg_ref[...] = dg.astype(dg_ref.dtype)
        du_ref[...] = du.astype(du_ref.dtype)

    t_spec = pl.BlockSpec((tm, R), lambda i, j: (i, j))
    return pl.pallas_call(
        body, name=name, grid=(T // tm, N_CHIPS),
        in_specs=[pl.BlockSpec((tm, C), lambda i, j: (i, 0)), pl.BlockSpec((None, None, R, C), lambda i, j: (j, l, 0, 0)),
                  t_spec, t_spec],
        out_specs=[t_spec, t_spec],
        out_shape=[jax.ShapeDtypeStruct((T, N_CHIPS * R), BF16)] * 2,
        compiler_params=_cp(2),
    )(df, wg_down, gate, up)


def _rms(x, g):
    return x * lax.rsqrt(jnp.mean(x * x, axis=-1, keepdims=True) + EPS) * g


def f_prenorm(h, g):
    return (_rms(h, g),)


def f_prenorm_thru(h, g):
    return _rms(h, g), h


def f_post_pre(h, m, g_post, g_pre):
    h1 = h + _rms(m, g_post)
    return h1, _rms(h1, g_pre)


def f_post(h1, f, g):
    return (h1 + _rms(f, g),)


def f_ple(h2, e, zg, g):
    return (h2 + _rms(e * _sigmoid(zg), g),)


def f_ple_pre(h2, e, zg, g, g_next):
    h3 = h2 + _rms(e * _sigmoid(zg), g)
    return h3, _rms(h3, g_next)


def _gelu(x):
    return 0.5 * x * (1.0 + lax.erf(x * 0.7071067811865476))


def f_gm_in(z, ln_g, ln_b):
    w = z.shape[-1] // 2
    u = _gelu(z[:, :w])
    v = _gelu(z[:, w:])
    mu = jnp.mean(v, axis=-1, keepdims=True)
    vc = v - mu
    vn = vc * lax.rsqrt(jnp.mean(vc * vc, axis=-1, keepdims=True) + EPS) * ln_g + ln_b
    return u, vn


def f_gm_spatial(u, vn, ws, bs):
    t = lax.broadcasted_iota(jnp.int32, ws.shape, 0)
    s = lax.broadcasted_iota(jnp.int32, ws.shape, 1)
    wm = jnp.where(t >= s, ws, 0.0).astype(BF16)
    ys = []
    for n in range(u.shape[0] // GM_CHUNK):
        rows = slice(n * GM_CHUNK, (n + 1) * GM_CHUNK)
        sv = jnp.dot(wm, vn[rows].astype(BF16), preferred_element_type=F32) + bs
        ys.append(u[rows] * sv)
    return (jnp.concatenate(ys, axis=0) if len(ys) > 1 else ys[0],)


def f_adam(w, g, m, v):
    m = ADAM_B1 * m + (1.0 - ADAM_B1) * g
    v = ADAM_B2 * v + (1.0 - ADAM_B2) * jnp.square(g)
    m_hat = m / (1.0 - ADAM_B1 ** ADAM_STEP)
    v_hat = v / (1.0 - ADAM_B2 ** ADAM_STEP)
    delta = -ADAM_LR * (m_hat / (jnp.sqrt(v_hat) + ADAM_EPS) + ADAM_WD * w)
    return delta, m, v


def _make_f_lb(n_layers):
    def f_lb(*logits):
        mx = functools.reduce(jnp.maximum, logits)
        ex = [jnp.exp(r - mx) for r in logits]
        tot = functools.reduce(lambda a, b: a + b, ex)
        sm = [e / tot for e in ex]
        outs = []
        run = jnp.zeros_like(sm[0])
        for j in range(n_layers):
            if j > 0:
                run = run + sm[j]
            lb = run
            outs += [jnp.log(jnp.maximum(lb, LB_FLOOR)), jnp.log(1.0 - lb), 1.0 - lb]
        return tuple(outs)
    return f_lb


def rows_fwd(name, fn, rows, params, out_dtypes, tm=256):
    T = rows[0].shape[0]
    tm = _tile(T, tm, 16)
    ins = [(r, (tm, r.shape[1]), lambda i: (i, 0)) for r in rows]
    ins += [(p, p.shape, lambda i: (0, 0)) for p in params]
    shapes = jax.eval_shape(lambda *a: fn(*a), *[jax.ShapeDtypeStruct((tm, r.shape[1]), F32) for r in rows],
                            *[jax.ShapeDtypeStruct(p.shape, F32) for p in params])
    outs = [((T, s.shape[1]), dt, (tm, s.shape[1]), lambda i: (i, 0), None) for s, dt in zip(shapes, out_dtypes)]
    return bmap_fwd(name, fn, (T // tm,), ins, outs)


def rows_bwd(name, fn, rows, params, cots, row_grad_dtypes, tm=256):
    T = rows[0].shape[0]
    tm = _tile(T, tm, 16)
    ins = [(r, (tm, r.shape[1]), lambda i: (i, 0)) for r in rows]
    ins += [(p, p.shape, lambda i: (0, 0)) for p in params]
    cts = [None if c is None else (c, (tm, c.shape[1]), lambda i: (i, 0)) for c in cots]
    grads = [(i, dt, None) for i, dt in enumerate(row_grad_dtypes) if dt is not None]
    grads += [(len(rows) + j, F32, 0) for j in range(len(params))]
    return bmap_bwd(name, fn, (T // tm,), ins, cts, grads)


def _log_sigmoid(z):
    return jnp.minimum(z, 0.0) - jnp.log(1.0 + jnp.exp(-jnp.abs(z)))


def _hg_gates(zf, ll0, ll1, oml):
    x2 = ll1 + _log_sigmoid(zf)
    mx = jnp.maximum(ll0, x2)
    g = mx + jnp.log(jnp.exp(ll0 - mx) + jnp.exp(x2 - mx))
    return g, oml * _sigmoid(-zf)


def hg_constants(n):
    levels = n.bit_length() - 1
    r = jnp.arange(n, dtype=jnp.int32)
    bounds = [r] + [((r >> (s + 1)) << (s + 1)) + ((1 << s) - 1) for s in range(levels)]
    sel = jnp.concatenate([(r[None, :] <= bd[:, None]) for bd in bounds], axis=0).astype(BF16)
    later = jnp.stack([((r >> s) & 1) for s in range(levels)]).astype(F32)
    later = jnp.broadcast_to(later[:, :, None], (levels, n, LANES))
    pair = jnp.stack([(r[:, None] >> (s + 1)) == (r[None, :] >> (s + 1)) for s in range(levels)]).astype(F32)
    return sel, sel.T, later, pair


def _dot2(m, x):
    hi = x.astype(BF16)
    lo = (x - hi.astype(F32)).astype(BF16)
    p = jnp.dot(m, jnp.concatenate([hi, lo], axis=1), preferred_element_type=F32)
    w = x.shape[1]
    return p[:, :w] + p[:, w:]


@jax.custom_vjp
def _sel_dot(sel, selt, g):
    return _dot2(sel, g)


def _sel_dot_fwd(sel, selt, g):
    return _dot2(sel, g), (sel, selt)


def _sel_dot_bwd(res, d):
    sel, selt = res
    return jnp.zeros_like(sel), jnp.zeros_like(selt), _dot2(selt, d)


_sel_dot.defvjp(_sel_dot_fwd, _sel_dot_bwd)


def _hg_state(st, zf, zi, ll0, ll1, oml, tri):
    g, k = _hg_gates(zf, ll0, ll1, oml)
    b = _dot2(tri, g)
    tot = jnp.sum(g, axis=0, keepdims=True)
    kd = k * jnp.exp(tot - b)
    return st * jnp.exp(tot) + jnp.dot(zi.T.astype(BF16), kd.astype(BF16), preferred_element_type=F32)


def _hg_step(st, zq, zf, zi, zg, ll0, ll1, oml, onorm, sel, selt, later, pair):
    n = zq.shape[0]
    levels = n.bit_length() - 1
    q = zq * _sigmoid(zq)
    g, k = _hg_gates(zf, ll0, ll1, oml)
    sums = _sel_dot(sel, selt, g)
    b = sums[:n]
    tot = jnp.sum(g, axis=0, keepdims=True)
    o = lax.dot_general((q * jnp.exp(b)).astype(BF16), st.astype(BF16), (NT, ((), ())), preferred_element_type=F32)
    a = jnp.zeros((n, n), F32)
    for s in range(levels):
        lt = later[s]
        fs = 1.0 - lt
        rel = b - sums[(s + 1) * n:(s + 2) * n]
        ql = lt * q * jnp.exp(lt * rel)
        kl = fs * k * jnp.exp(-fs * rel)
        al = lax.dot_general(ql.astype(BF16), kl.astype(BF16), (NT, ((), ())), preferred_element_type=F32)
        a = a + pair[s] * al
    o = o + jnp.dot(a.astype(BF16), zi.astype(BF16), preferred_element_type=F32)
    o = o + jnp.sum(q * k, axis=1, keepdims=True) * zi
    kd = k * jnp.exp(tot - b)
    st_new = st * jnp.exp(tot) + jnp.dot(zi.T.astype(BF16), kd.astype(BF16), preferred_element_type=F32)
    og = _rms(o, onorm) * (zg * _sigmoid(zg))
    return og, st_new


def _whole(arr, n_grid):
    zeros = (0,) * arr.ndim
    return pl.BlockSpec(arr.shape, (lambda h, n: zeros) if n_grid == 2 else (lambda i: zeros))


def _hg_dims(proj4, n_seq):
    _, T, D = proj4.shape
    S = T // n_seq
    hp = HG_HEADS_PER if (D // LANES) % HG_HEADS_PER == 0 else 1
    tb = min(HG_BLOCK, S)
    streams = [(b, hl) for b in range(n_seq) for hl in range(hp)]
    return T, D, S, hp, D // (LANES * hp), LANES * hp, tb, S // tb, tb // HG_SUB, streams


def hgrn_fwd(name, proj4, ll0, ll1, oml, onorm, n_seq):
    T, D, S, hp, n_hg, W, tb, nblk, nsub, streams = _hg_dims(proj4, n_seq)
    ns = len(streams)

    def body(p_ref, ll0_ref, ll1_ref, oml_ref, on_ref, sel_ref, selt_ref, later_ref, pair_ref, og_ref, st_ref, st):
        @pl.when(pl.program_id(1) == 0)
        def _():
            st[...] = jnp.zeros(st.shape, F32)

        st_ref[...] = st[...]
        on = on_ref[...]

        def step(j, carry):
            r = pl.ds(pl.multiple_of(j * HG_SUB, HG_SUB), HG_SUB)
            consts = (sel_ref[...], selt_ref[...], later_ref[...], pair_ref[...])
            args = []
            for si, (b, hl) in enumerate(streams):
                ln = slice(hl * LANES, (hl + 1) * LANES)
                args.append((st[si], p_ref[0, b, r, ln], p_ref[1, b, r, ln], p_ref[2, b, r, ln], p_ref[3, b, r, ln],
                             ll0_ref[:, ln], ll1_ref[:, ln], oml_ref[:, ln], on) + consts)
            res = [_hg_step(*a) for a in args]
            for si, (b, hl) in enumerate(streams):
                og_ref[b, r, hl * LANES:(hl + 1) * LANES] = res[si][0].astype(og_ref.dtype)
                st[si] = res[si][1]
            return carry

        lax.fori_loop(0, nsub, step, 0)

    vec = pl.BlockSpec((1, W), lambda h, n: (0, h))
    consts = hg_constants(HG_SUB)
    og, states = pl.pallas_call(
        body, name=name, grid=(n_hg, nblk),
        in_specs=[pl.BlockSpec((4, n_seq, tb, W), lambda h, n: (0, 0, n, h)), vec, vec, vec,
                  pl.BlockSpec((1, LANES), lambda h, n: (0, 0))] + [_whole(c, 2) for c in consts],
        out_specs=[pl.BlockSpec((n_seq, tb, W), lambda h, n: (0, n, h)),
                   pl.BlockSpec((None, None, ns, LANES, LANES), lambda h, n: (h, n, 0, 0, 0))],
        out_shape=[jax.ShapeDtypeStruct((n_seq, S, D), BF16),
                   jax.ShapeDtypeStruct((n_hg, nblk, ns, LANES, LANES), F32)],
        scratch_shapes=[pltpu.VMEM((ns, LANES, LANES), F32)],
        compiler_params=_cp(2),
    )(proj4.reshape(4, n_seq, S, D), ll0, ll1, oml, onorm, *consts)
    return og.reshape(T, D), states


def hgrn_bwd(name, proj4, states, dog, ll0, ll1, oml, onorm, n_seq):
    T, D, S, hp, n_hg, W, tb, nblk, nsub, streams = _hg_dims(proj4, n_seq)
    ns = len(streams)

    def body(p_ref, st_ref, dog_ref, ll0_ref, ll1_ref, oml_ref, on_ref, sel_ref, selt_ref, later_ref, pair_ref,
             dp_ref, dll0_ref, dll1_ref, doml_ref, don_ref, sbuf, dst):
        n_id = pl.program_id(1)

        @pl.when(n_id == 0)
        def _():
            dst[...] = jnp.zeros(dst.shape, F32)
            for ref in (dll0_ref, dll1_ref, doml_ref):
                ref[...] = jnp.zeros(ref.shape, F32)

        @pl.when(jnp.logical_and(n_id == 0, pl.program_id(0) == 0))
        def _():
            don_ref[...] = jnp.zeros(don_ref.shape, F32)

        on = on_ref[...]

        def fwd(j, carry):
            r = pl.ds(pl.multiple_of(j * HG_SUB, HG_SUB), HG_SUB)
            tri = sel_ref[0:HG_SUB, :]
            args = []
            for si, (b, hl) in enumerate(streams):
                ln = slice(hl * LANES, (hl + 1) * LANES)
                args.append((carry[si], p_ref[1, b, r, ln], p_ref[2, b, r, ln],
                             ll0_ref[:, ln], ll1_ref[:, ln], oml_ref[:, ln], tri))
            for si in range(ns):
                sbuf[si, j] = carry[si]
            return tuple(_hg_state(*a) for a in args)

        lax.fori_loop(0, nsub, fwd, tuple(st_ref[si] for si in range(ns)))

        def bwd(jj, carry):
            j = nsub - 1 - jj
            r = pl.ds(pl.multiple_of(j * HG_SUB, HG_SUB), HG_SUB)
            args, cts = [], []
            for si, (b, hl) in enumerate(streams):
                ln = slice(hl * LANES, (hl + 1) * LANES)
                args.append((sbuf[si, j], p_ref[0, b, r, ln], p_ref[1, b, r, ln], p_ref[2, b, r, ln],
                             p_ref[3, b, r, ln], ll0_ref[:, ln], ll1_ref[:, ln], oml_ref[:, ln], on))
                cts.append((dog_ref[b, r, ln].astype(F32), dst[si]))
            consts = (sel_ref[...], selt_ref[...], later_ref[...], pair_ref[...])
            step_fn = lambda *a: _hg_step(*a, *consts)
            ds = [jax.vjp(step_fn, *a)[1](ct) for a, ct in zip(args, cts)]
            d_on = carry
            for si, (b, hl) in enumerate(streams):
                ln = slice(hl * LANES, (hl + 1) * LANES)
                d = ds[si]
                dst[si] = d[0]
                for part in range(4):
                    dp_ref[part, b, r, ln] = d[1 + part].astype(dp_ref.dtype)
                dll0_ref[:, ln] += d[5]
                dll1_ref[:, ln] += d[6]
                doml_ref[:, ln] += d[7]
                d_on = d_on + d[8]
            return d_on

        don_ref[...] += lax.fori_loop(0, nsub, bwd, jnp.zeros((1, LANES), F32))

    last = nblk - 1
    vec = pl.BlockSpec((1, W), lambda h, n: (0, h))
    one = pl.BlockSpec((1, LANES), lambda h, n: (0, 0))
    consts = hg_constants(HG_SUB)
    dproj, d0, d1, d2, d_on = pl.pallas_call(
        body, name=name, grid=(n_hg, nblk),
        in_specs=[pl.BlockSpec((4, n_seq, tb, W), lambda h, n: (0, 0, last - n, h)),
                  pl.BlockSpec((None, None, ns, LANES, LANES), lambda h, n: (h, last - n, 0, 0, 0)),
                  pl.BlockSpec((n_seq, tb, W), lambda h, n: (0, last - n, h)), vec, vec, vec, one]
        + [_whole(c, 2) for c in consts],
        out_specs=[pl.BlockSpec((4, n_seq, tb, W), lambda h, n: (0, 0, last - n, h)), vec, vec, vec, one],
        out_shape=[jax.ShapeDtypeStruct((4, n_seq, S, D), BF16)] + [jax.ShapeDtypeStruct((1, D), F32)] * 3
        + [jax.ShapeDtypeStruct((1, LANES), F32)],
        scratch_shapes=[pltpu.VMEM((ns, nsub, LANES, LANES), F32), pltpu.VMEM((ns, LANES, LANES), F32)],
        compiler_params=_cp(2),
    )(proj4.reshape(4, n_seq, S, D), states, dog.reshape(n_seq, S, D), ll0, ll1, oml, onorm, *consts)
    return dproj.reshape(4, T, D), d0, d1, d2, d_on


def _place():
    x, y, c = lax.axis_index("x"), lax.axis_index("y"), lax.axis_index("c")
    chips = [(1 - x, y), (x, 1 - y), (1 - x, 1 - y)]
    return x, y, c, chips


ANY = pl.BlockSpec(memory_space=pl.ANY)


def _comm_call(name, body, ins, out_shapes, sems, aliases=None):
    return pl.pallas_call(
        body, name=name, in_specs=[ANY] * len(ins), out_specs=[ANY] * len(out_shapes),
        out_shape=out_shapes, scratch_shapes=sems, input_output_aliases=aliases or {},
        compiler_params=pltpu.CompilerParams(has_side_effects=True),
    )(*ins)


HBM_SPEC = pl.BlockSpec(memory_space=pltpu.HBM)
SEM_SPEC = pl.BlockSpec(memory_space=pltpu.SEMAPHORE)
SPLIT_EFFECT = pltpu.SideEffectType.DATAFLOW_SIDE_EFFECTING
N_PEER_CHIPS = 3


def split_start(name, build, arrays, n_peers=N_PEER_CHIPS):
    n = len(arrays)

    def body(*refs):
        send, recv = refs[n], refs[n + 1]
        token = refs[2 * n + 2]
        starts, _ = build(refs[:n], send, recv)
        for cp in starts:
            cp.start()
        token[...] = jnp.zeros_like(token)

    res = pl.pallas_call(
        body, name=name,
        out_shape=(pltpu.SemaphoreType.DMA((n_peers,)), pltpu.SemaphoreType.DMA((n_peers,)),
                   *[pltpu.HBM(a.shape, a.dtype) for a in arrays], jax.ShapeDtypeStruct((8, LANES), F32)),
        in_specs=[HBM_SPEC] * n,
        out_specs=(SEM_SPEC, SEM_SPEC, *[HBM_SPEC] * n, pl.BlockSpec(memory_space=pltpu.VMEM)),
        input_output_aliases={i: 2 + i for i in range(n)},
        compiler_params=pltpu.CompilerParams(has_side_effects=SPLIT_EFFECT),
    )(*[pltpu.with_memory_space_constraint(a, pltpu.HBM) for a in arrays])
    return res[0], res[1], list(res[2:2 + n]), res[2 + n]


def split_wait(name, build, send, recv, arrays, after):
    n = len(arrays)

    def body(*refs):
        starts, arrivals = build(refs[:n], refs[n], refs[n + 1])
        for cp in starts:
            cp.wait_send()
        for cp in arrivals:
            cp.wait_recv()

    return list(pl.pallas_call(
        body, name=name, out_shape=tuple(pltpu.HBM(a.shape, a.dtype) for a in arrays),
        in_specs=[HBM_SPEC] * n + [SEM_SPEC, SEM_SPEC, ANY], out_specs=tuple([HBM_SPEC] * n),
        input_output_aliases={i: i for i in range(n)},
        compiler_params=pltpu.CompilerParams(has_side_effects=SPLIT_EFFECT),
    )(*arrays, send, recv, after))


def _row_half(ref, dim, who):
    rh = ref.shape[dim] // 2
    return pl.ds(who * rh, rh)


def gather_build(refs, send, recv):
    x, y, c, chips = _place()
    q = 2 * x + y
    starts, arrivals = [], []
    for buf in refs:
        rows = _row_half(buf, 2, c)
        for j, (px, py) in enumerate(chips):
            mine, got = buf.at[q, :, rows], buf.at[2 * px + py, :, rows]
            starts.append(pltpu.make_async_remote_copy(src_ref=mine, dst_ref=mine, send_sem=send.at[j], recv_sem=recv.at[j],
                                                       device_id=(px, py, c), device_id_type=MESH_ID))
            arrivals.append(pltpu.make_async_remote_copy(src_ref=got, dst_ref=got, send_sem=send.at[j], recv_sem=recv.at[j],
                                                         device_id=(px, py, c), device_id_type=MESH_ID))
    return starts, arrivals


def gather_forward(name, bufs):
    n = len(bufs)

    def body(*refs):
        dst = refs[n:2 * n]
        send, recv = refs[2 * n:]
        x, y, c, chips = _place()
        sib = (x, y, 1 - c)
        cps = []
        for a in range(n):
            for j, (px, py) in enumerate(chips):
                got = dst[a].at[2 * px + py, :, _row_half(dst[a], 2, c)]
                cps.append(pltpu.make_async_remote_copy(src_ref=got, dst_ref=got, send_sem=send.at[a, j], recv_sem=recv.at[a, j],
                                                        device_id=sib, device_id_type=MESH_ID))
        for cp in cps:
            cp.start()
        for a in range(n):
            for j, (px, py) in enumerate(chips):
                theirs = dst[a].at[2 * px + py, :, _row_half(dst[a], 2, 1 - c)]
                pltpu.make_async_remote_copy(src_ref=theirs, dst_ref=theirs, send_sem=send.at[a, j], recv_sem=recv.at[a, j],
                                             device_id=sib, device_id_type=MESH_ID).wait_recv()
        for cp in cps:
            cp.wait_send()

    outs = [jax.ShapeDtypeStruct(s.shape, s.dtype) for s in bufs]
    sems = [pltpu.SemaphoreType.DMA((n, N_PEER_CHIPS))] * 2
    return _comm_call(name, body, bufs, outs, sems, aliases={a: a for a in range(n)})


def scatter_build(refs, send, recv):
    n = len(refs) // 2
    x, y, c, _ = _place()
    starts, arrivals = [], []
    for a in range(n):
        src, dst = refs[a], refs[n + a]
        for k in range(1, N_DEV):
            px, py, pc = x ^ (k >> 2), y ^ ((k >> 1) & 1), c ^ (k & 1)
            theirs = src.at[2 * px + py, :, _row_half(src, 2, pc)]
            starts.append(pltpu.make_async_remote_copy(src_ref=theirs, dst_ref=dst.at[k - 1], send_sem=send.at[k - 1],
                                                       recv_sem=recv.at[k - 1], device_id=(px, py, pc), device_id_type=MESH_ID))
            arrivals.append(pltpu.make_async_remote_copy(src_ref=dst.at[k - 1], dst_ref=dst.at[k - 1], send_sem=send.at[k - 1],
                                                         recv_sem=recv.at[k - 1], device_id=(px, py, pc), device_id_type=MESH_ID))
    return starts, arrivals


def join_row_halves(bufs):
    n = len(bufs)

    def body(*refs):
        dst = refs[n:2 * n]
        send, recv = refs[2 * n:]
        x, y, c, _ = _place()
        cps = []
        for a in range(n):
            mine = dst[a].at[:, _row_half(dst[a], 1, c)]
            cps.append(pltpu.make_async_remote_copy(src_ref=mine, dst_ref=mine, send_sem=send.at[a], recv_sem=recv.at[a],
                                                    device_id=(x, y, 1 - c), device_id_type=MESH_ID))
        for cp in cps:
            cp.start()
        for a in range(n):
            theirs = dst[a].at[:, _row_half(dst[a], 1, 1 - c)]
            pltpu.make_async_remote_copy(src_ref=theirs, dst_ref=theirs, send_sem=send.at[a], recv_sem=recv.at[a],
                                         device_id=(x, y, 1 - c), device_id_type=MESH_ID).wait_recv()
        for cp in cps:
            cp.wait_send()

    outs = [jax.ShapeDtypeStruct(b.shape, b.dtype) for b in bufs]
    sems = [pltpu.SemaphoreType.DMA((n,))] * 2
    return _comm_call("join_row_halves", body, bufs, outs, sems, aliases={a: a for a in range(n)})


def share_with_all(name, packed, me):
    slots = lax.dynamic_update_slice(jnp.zeros((N_DEV,) + packed.shape, packed.dtype), packed[None], (me, 0, 0))

    def body(_, dst, send, recv):
        x, y, c, _ = _place()
        me = 4 * x + 2 * y + c
        cps = []
        for k in range(1, N_DEV):
            px, py, pc = x ^ (k >> 2), y ^ ((k >> 1) & 1), c ^ (k & 1)
            cps.append(pltpu.make_async_remote_copy(src_ref=dst.at[me], dst_ref=dst.at[me], send_sem=send.at[k - 1],
                                                    recv_sem=recv.at[k - 1], device_id=(px, py, pc), device_id_type=MESH_ID))
        for cp in cps:
            cp.start()
        for k in range(1, N_DEV):
            px, py, pc = x ^ (k >> 2), y ^ ((k >> 1) & 1), c ^ (k & 1)
            got = dst.at[4 * px + 2 * py + pc]
            pltpu.make_async_remote_copy(src_ref=got, dst_ref=got, send_sem=send.at[k - 1], recv_sem=recv.at[k - 1],
                                         device_id=(px, py, pc), device_id_type=MESH_ID).wait_recv()
        for cp in cps:
            cp.wait_send()

    outs = [jax.ShapeDtypeStruct(slots.shape, slots.dtype)]
    sems = [pltpu.SemaphoreType.DMA((N_DEV - 1,))] * 2
    return _comm_call(name, body, [slots], outs, sems, aliases={0: 0})[0]


def _w_tiles(R, C):
    return _tile(R, max(16, (1 << 20) // (4 * C) // 16 * 16), 16)


def cast_bf16(w, l, q_arr):
    _, R, C = w.shape
    tr = _w_tiles(R, C)
    ins = [(w, (None, tr, C), lambda r, q: (l, r, 0))]
    outs = [((N_CHIPS, 1, R, C), BF16, (None, None, tr, C), lambda r, q: (q[0], 0, r, 0), None)]
    return bmap_fwd("cast_bf16", lambda a: (a,), (R // tr,), ins, outs, scalars=(q_arr,))[0]


def sum_partials(own, landed, into, l, q_arr, c_arr):
    n_land, _, rh, C = landed.shape
    tr = _w_tiles(rh, C)
    nb = rh // tr
    blk = (None, None, tr, C)
    ins = [(own, blk, lambda r, q, c: (q[0], 0, c[0] * nb + r, 0))]
    ins += [(landed, blk, (lambda r, q, c, kk=kk: (kk, 0, r, 0))) for kk in range(n_land)]
    outs = [(into.shape, F32, (None, tr, C), lambda r, q, c: (l, c[0] * nb + r, 0), None)]
    return bmap_fwd("sum_partials", lambda *t: (functools.reduce(lambda u, v: u + v, t),), (nb,), ins, outs,
                    scalars=(q_arr, c_arr), into=into)[0]


def sum_devices(slots):
    nd, NR, C = slots.shape
    tr = _tile(NR, 512, 8)
    ins = [(slots, (None, tr, C), (lambda r, dd=dd: (dd, r, 0))) for dd in range(nd)]
    outs = [((NR, C), F32, (tr, C), lambda r: (r, 0), None)]
    return bmap_fwd("sum_devices", lambda *a: (functools.reduce(lambda u, v: u + v, a),), (NR // tr,), ins, outs)[0]


def adamw(name, w, g, m, v, with_grad=False):
    if w.ndim == 2:
        R, C = w.shape
        tr = _w_tiles(R, C)
        spec = ((tr, C), lambda r: (r, 0))
        grid = (R // tr,)
    else:
        L, R, C = w.shape
        tr = _w_tiles(R, C)
        spec = ((None, tr, C), lambda l, r: (l, r, 0))
        grid = (L, R // tr)
    ins = [(a,) + spec for a in (w, g, m, v)]
    outs = [(w.shape, F32) + spec + (None,)] * (4 if with_grad else 3)
    fn = (lambda a, b, c, d: f_adam(a, b, c, d) + (b,)) if with_grad else f_adam
    return bmap_fwd(name, fn, grid, ins, outs)


def loss_and_grad(h, target):
    T, D = h.shape
    tm = _tile(T, 256, 8)

    def fn(hv, tv):
        d = hv - tv
        return jnp.sum(d * d, keepdims=True).reshape(1, 1) * (0.5 / D), d * (1.0 / D)

    ins = [(h, (tm, D), lambda i: (i, 0)), (target, (tm, D), lambda i: (i, 0))]
    outs = [((1, 1), F32, (1, 1), lambda i: (0, 0), 0), ((T, D), F32, (tm, D), lambda i: (i, 0), None)]
    return bmap_fwd("loss_and_grad", fn, (T // tm,), ins, outs)


BIG = ("hg_w_in", "hg_w_out", "gm_w_in", "gm_w_out", "ffn_w_gate", "ffn_w_up", "ffn_w_down", "ple_w_proj", "ple_w_gate")
KIND = {"hg_w_in": "col", "hg_w_out": "row", "gm_w_in": "col", "gm_w_out": "row", "ffn_w_gate": "col",
        "ffn_w_up": "col", "ffn_w_down": "row", "ple_w_proj": "col", "ple_w_gate": "row"}
SMALL = ("hg_lb_logits", "hg_out_norm", "gm_ln_g", "gm_ln_b", "gm_w_s", "gm_b_s", "norm_mix_pre", "norm_mix_post",
         "norm_ffn_pre", "norm_ffn_post", "ple_norm")
WEIGHTS = ("hg_w_in", "hg_lb_logits", "hg_out_norm", "hg_w_out", "gm_w_in", "gm_ln_g", "gm_ln_b", "gm_w_s", "gm_b_s",
           "gm_w_out", "norm_mix_pre", "norm_mix_post", "norm_ffn_pre", "norm_ffn_post", "ffn_w_gate", "ffn_w_up",
           "ffn_w_down", "ple_w_proj", "ple_w_gate", "ple_norm")


def _pack(arrs):
    rows = []
    for a in arrs:
        flat = a.reshape(-1)
        pad = (-flat.shape[0]) % (8 * LANES)
        rows.append(jnp.pad(flat, (0, pad)).reshape(-1, LANES))
    n_rows = sum(r.shape[0] for r in rows)
    rows.append(jnp.zeros(((-n_rows) % PACK_ROWS, LANES), F32))
    return jnp.concatenate(rows, axis=0)


def _unpack(packed, shapes):
    out, r = [], 0
    for s in shapes:
        size = 1
        for d in s:
            size *= d
        nr = -(-size // (8 * LANES)) * 8
        out.append(packed[r:r + nr].reshape(-1)[:size].reshape(s))
        r += nr
    return out


def _step(x, p, W, M, V, loss_target):
    n_seq, S, D = x.shape
    T = n_seq * S
    depth = p.shape[0]
    n_hg = W["hg_w_in"].shape[0]
    x2 = x.reshape(T, D)
    p3 = p.reshape(depth, T, p.shape[-1])
    tgt = loss_target.reshape(T, D)
    xi, yi, ci = lax.axis_index("x"), lax.axis_index("y"), lax.axis_index("c")
    q_me = 2 * xi + yi
    c_arr = jnp.reshape(ci, (1,)).astype(jnp.int32)
    q_arr = jnp.reshape(q_me, (1,)).astype(jnp.int32)

    groups = {}
    for i in range(depth):
        mix = ("hg_w_in", "hg_w_out") if i % 2 == 0 else ("gm_w_in", "gm_w_out")
        groups[i, "mix"] = [(k, i // 2) for k in mix]
        groups[i, "rest"] = [(k, i) for k in ("ffn_w_gate", "ffn_w_up", "ffn_w_down", "ple_w_proj", "ple_w_gate")]
    G = {k: {} for k in BIG}
    DW = {k: {l: lax.empty((N_CHIPS, 1) + W[k].shape[1:], BF16) for l in range(W[k].shape[0])} for k in BIG}
    in_flight = {}

    def start_gather(i, part, dep):
        qa = q_arr if dep is None else lax.optimization_barrier((q_arr, dep))[0]
        casts = [cast_bf16(W[k], l, qa) for k, l in groups[i, part]]
        send, recv, arrs, tok = split_start("gather_start_%d_%s" % (i, part), gather_build, casts)
        in_flight[i, part] = (send, recv, arrs)
        return tok

    def finish_gather(i, part, after):
        send, recv, arrs = in_flight.pop((i, part))
        arrs = split_wait("gather_wait_%d_%s" % (i, part), gather_build, send, recv, arrs, after)
        for (k, l), buf in zip(groups[i, part], gather_forward("gather_forward_%d_%s" % (i, part), arrs)):
            G[k][l] = buf
        return buf

    def after_token(row_arr, *toks):
        return functools.reduce(lambda u, t: u + t[0:1, 0:1], toks, row_arr)

    tok_mix = start_gather(0, "mix", None)
    tok_rest = start_gather(0, "rest", tok_mix)
    finish_gather(0, "mix", tok_mix + tok_rest)
    me = 4 * xi + 2 * yi + ci
    ln_full = share_with_all("share_ln", _pack([W["gm_ln_g"], W["gm_ln_b"]]), me)
    n_gm, dq = W["gm_ln_g"].shape
    ln_parts = [_unpack(ln_full[4 * qx + 2 * qy + 0], [(n_gm, dq), (n_gm, dq)]) for qx in range(2) for qy in range(2)]
    ln_g = jnp.concatenate([lp[0] for lp in ln_parts], axis=1)
    ln_b = jnp.concatenate([lp[1] for lp in ln_parts], axis=1)

    row = lambda a, i: a[i][None, :]
    f_lb = _make_f_lb(n_hg)
    lb_rows = [row(W["hg_lb_logits"], j) for j in range(n_hg)]
    one = (1, D)
    lb_ins = [(r, one, lambda i: (0, 0)) for r in lb_rows]
    lb_out = bmap_fwd("hg_lower_bounds", f_lb, (1,), lb_ins, [(one, F32, one, lambda i: (0, 0), None)] * (3 * n_hg))

    saved = []
    h = x2
    a = rows_fwd("prenorm", f_prenorm, [h], [row(W["norm_mix_pre"], 0)], [BF16])[0]
    for i in range(depth):
        j = i // 2
        sv = {"h": h, "a": a}
        if i > 0:
            finish_gather(i, "mix", h)
        if i % 2 == 0:
            proj4 = mm_fwd("hg_in", a, G["hg_w_in"], j, "col", parts=True)
            lbp = lb_out[3 * j:3 * j + 3]
            onorm = row(W["hg_out_norm"], j)
            og, states = hgrn_fwd("hgrn_fwd", proj4, *lbp, onorm, n_seq)
            m = mm_fwd("hg_out", og, G["hg_w_out"], j, "row")
            sv.update(proj4=proj4, states=states, og=og, lbp=lbp, onorm=onorm)
        else:
            z = mm_fwd("gm_in", a, G["gm_w_in"], j, "col")
            lg, lb_ = row(ln_g, j), row(ln_b, j)
            u, vn = rows_fwd("gm_gelu_ln", f_gm_in, [z], [lg, lb_], [F32, BF16], tm=128)
            ws = W["gm_w_s"][j]
            bs = W["gm_b_s"][j][:, :, None]
            gb = min(GM_BLOCK, S)
            sp_grid = (D // LANES, T // gb)
            sp_ins = [(u, (gb, LANES), lambda g, n: (n, g)), (vn, (gb, LANES), lambda g, n: (n, g)),
                      (ws, (None, GM_CHUNK, GM_CHUNK), lambda g, n: (g, 0, 0)),
                      (bs, (None, GM_CHUNK, 1), lambda g, n: (g, 0, 0))]
            y = bmap_fwd("gm_spatial", f_gm_spatial, sp_grid, sp_ins,
                         [((T, D), BF16, (gb, LANES), lambda g, n: (n, g), None)])[0]
            m = mm_fwd("gm_out", y, G["gm_w_out"], j, "row")
            sv.update(z=z, lg=lg, lb_=lb_, sp_ins=sp_ins, sp_grid=sp_grid, y=y)
        g_post, g_fpre = row(W["norm_mix_post"], i), row(W["norm_ffn_pre"], i)
        arrived = finish_gather(i, "rest", m)
        if i + 1 < depth:
            tok_mix = start_gather(i + 1, "mix", arrived)
            g_post = after_token(g_post, tok_mix, start_gather(i + 1, "rest", tok_mix))
        h1, fin = rows_fwd("mix_post_ffn_pre", f_post_pre, [h, m], [g_post, g_fpre], [F32, BF16])
        gate, up, act = ffn_gate_up("ffn_gate_up", fin, G["ffn_w_gate"], G["ffn_w_up"], i)
        f = mm_fwd("ffn_down", act, G["ffn_w_down"], i, "row")
        g_fpost = row(W["norm_ffn_post"], i)
        h2 = rows_fwd("ffn_post", f_post, [h1, f], [g_fpost], [F32])[0]
        e = mm_fwd("ple_proj", p3, G["ple_w_proj"], i, "col", xl=i)
        zg = mm_fwd("ple_gate", h2, G["ple_w_gate"], i, "row")
        g_ple = row(W["ple_norm"], i)
        sv.update(m=m, h1=h1, fin=fin, gate=gate, up=up, act=act, f=f, h2=h2, e=e, zg=zg,
                  g_post=g_post, g_fpre=g_fpre, g_fpost=g_fpost, g_ple=g_ple)
        if i + 1 < depth:
            g_next = row(W["norm_mix_pre"], i + 1)
            h, a = rows_fwd("ple_next_pre", f_ple_pre, [h2, e, zg], [g_ple, g_next], [F32, BF16])
            sv["g_next"] = g_next
        else:
            h = rows_fwd("ple_last", f_ple, [h2, e, zg], [g_ple], [F32])[0]
        saved.append(sv)

    loss_part, dh = loss_and_grad(h, tgt)
    loss = lax.psum(loss_part[0, 0], ("x", "y", "c"))

    sg = {k: [None] * W[k].shape[0] for k in ("norm_mix_pre", "norm_mix_post", "norm_ffn_pre", "norm_ffn_post", "ple_norm",
                                              "hg_out_norm", "gm_ln_g", "gm_ln_b", "gm_w_s", "gm_b_s")}
    d_lbp = [None] * (3 * n_hg)
    da_next = None
    GRAD = {k: lax.empty(W[k].shape, F32) for k in BIG}
    scattering = {}

    def start_scatter(i, part):
        dws = [DW[k][l] for k, l in groups[i, part]]
        lands = [lax.empty((N_DEV - 1, 1, g.shape[2] // 2, g.shape[3]), BF16) for g in dws]
        send, recv, arrs, tok = split_start("scatter_start_%d_%s" % (i, part), scatter_build, dws + lands, n_peers=N_DEV - 1)
        scattering[i, part] = (send, recv, arrs)
        return tok

    def finish_scatter(i, part, after):
        send, recv, arrs = scattering.pop((i, part))
        arrs = split_wait("scatter_wait_%d_%s" % (i, part), scatter_build, send, recv, arrs, after)
        n = len(groups[i, part])
        for (k, l), own, ld in zip(groups[i, part], arrs[:n], arrs[n:]):
            GRAD[k] = sum_partials(own, ld, GRAD[k], l, q_arr, c_arr)

    tok = None
    for i in reversed(range(depth)):
        j = i // 2
        sv = saved[i]
        if i + 1 < depth:
            g_ple_after = sv["g_ple"] + tok[0:1, 0:1]
            dh2, de, dzg, d_gple, d_gnext = rows_bwd("ple_next_pre_bwd", f_ple_pre, [sv["h2"], sv["e"], sv["zg"]],
                                                     [g_ple_after, sv["g_next"]], [dh, da_next], [F32, BF16, BF16])
            sg["norm_mix_pre"][i + 1] = d_gnext
        else:
            dh2, de, dzg, d_gple = rows_bwd("ple_last_bwd", f_ple, [sv["h2"], sv["e"], sv["zg"]], [sv["g_ple"]], [dh],
                                            [F32, BF16, BF16])
        sg["ple_norm"][i] = d_gple
        DW["ple_w_proj"] = mm_bwd_w("ple_proj_dw", p3, de, DW["ple_w_proj"], i, "col", xl=i)
        DW["ple_w_gate"] = mm_bwd_w("ple_gate_dw", sv["h2"], dzg, DW["ple_w_gate"], i, "row")
        dh2 = mm_bwd_x("ple_gate_dx", dzg, G["ple_w_gate"], i, "row", addend=dh2)
        dh1, df, d_gfpost = rows_bwd("ffn_post_bwd", f_post, [sv["h1"], sv["f"]], [sv["g_fpost"]], [dh2], [F32, BF16])
        sg["norm_ffn_post"][i] = d_gfpost
        dgate, dup = ffn_down_dx("ffn_down_dx", df, G["ffn_w_down"], i, sv["gate"], sv["up"])
        DW["ffn_w_down"] = mm_bwd_w("ffn_down_dw", sv["act"], df, DW["ffn_w_down"], i, "row")
        dfin = mm_bwd_x("ffn_gate_dx", dgate, G["ffn_w_gate"], i, "col")
        dfin = mm_bwd_x("ffn_up_dx", dup, G["ffn_w_up"], i, "col", addend=dfin)
        DW["ffn_w_gate"] = mm_bwd_w("ffn_gate_dw", sv["fin"], dgate, DW["ffn_w_gate"], i, "col")
        DW["ffn_w_up"] = mm_bwd_w("ffn_up_dw", sv["fin"], dup, DW["ffn_w_up"], i, "col")
        g_post_after = after_token(sv["g_post"], start_scatter(i, "rest"))
        dh, dm, d_gpost, d_gfpre = rows_bwd("mix_post_ffn_pre_bwd", f_post_pre, [sv["h"], sv["m"]],
                                            [g_post_after, sv["g_fpre"]], [dh1, dfin], [F32, BF16])
        sg["norm_mix_post"][i], sg["norm_ffn_pre"][i] = d_gpost, d_gfpre
        if i % 2 == 0:
            dog = mm_bwd_x("hg_out_dx", dm, G["hg_w_out"], j, "row")
            DW["hg_w_out"] = mm_bwd_w("hg_out_dw", sv["og"], dm, DW["hg_w_out"], j, "row")
            dproj4, d0, d1, d2, d_on = hgrn_bwd("hgrn_bwd", sv["proj4"], sv["states"], dog, *sv["lbp"], sv["onorm"], n_seq)
            d_lbp[3 * j:3 * j + 3] = [d0, d1, d2]
            sg["hg_out_norm"][j] = d_on
            da_next = mm_bwd_x("hg_in_dx", dproj4, G["hg_w_in"], j, "col", parts=True)
            DW["hg_w_in"] = mm_bwd_w("hg_in_dw", sv["a"], dproj4, DW["hg_w_in"], j, "col", parts=True)
        else:
            dy = mm_bwd_x("gm_out_dx", dm, G["gm_w_out"], j, "row")
            DW["gm_w_out"] = mm_bwd_w("gm_out_dw", sv["y"], dm, DW["gm_w_out"], j, "row")
            gb = sv["sp_ins"][0][1][0]
            du, dvn, dws, dbs = bmap_bwd("gm_spatial_bwd", f_gm_spatial, sv["sp_grid"], sv["sp_ins"],
                                         [(dy, (gb, LANES), lambda g, n: (n, g))],
                                         [(0, F32, None), (1, F32, None), (2, F32, 1), (3, F32, 1)])
            sg["gm_w_s"][j], sg["gm_b_s"][j] = dws, dbs[:, :, 0]
            dz, d_lg, d_lb = rows_bwd("gm_gelu_ln_bwd", f_gm_in, [sv["z"]], [sv["lg"], sv["lb_"]], [du, dvn], [BF16], tm=128)
            sg["gm_ln_g"][j], sg["gm_ln_b"][j] = d_lg, d_lb
            da_next = mm_bwd_x("gm_in_dx", dz, G["gm_w_in"], j, "col")
            DW["gm_w_in"] = mm_bwd_w("gm_in_dw", sv["a"], dz, DW["gm_w_in"], j, "col")
        tok = start_scatter(i, "mix")
        if i + 1 < depth:
            finish_scatter(i + 1, "rest", da_next)
            finish_scatter(i + 1, "mix", da_next)
    finish_scatter(0, "rest", tok)
    g0 = after_token(row(W["norm_mix_pre"], 0), tok)
    grad_x, d_g0 = rows_bwd("prenorm_bwd", f_prenorm_thru, [saved[0]["h"]], [g0], [da_next, dh], [F32])
    sg["norm_mix_pre"][0] = d_g0
    d_logits = bmap_bwd("hg_lower_bounds_bwd", f_lb, (1,), lb_ins, [(d, one, lambda i: (0, 0)) for d in d_lbp],
                        [(jj, F32, None) for jj in range(n_hg)])

    small_g = {k: jnp.stack([v.reshape(W[k].shape[1:] if k not in ("gm_ln_g", "gm_ln_b") else (D,)) for v in sg[k]])
               for k in sg}
    small_g["hg_lb_logits"] = jnp.concatenate(d_logits, axis=0)
    small_shapes = [small_g[k].shape for k in SMALL]
    red = sum_devices(share_with_all("share_small_grads", _pack([small_g[k] for k in SMALL]), me))
    small_red = dict(zip(SMALL, _unpack(red, small_shapes)))
    for k in ("gm_ln_g", "gm_ln_b"):
        small_red[k] = lax.dynamic_slice_in_dim(small_red[k], q_me * dq, dq, axis=1)
    pk = lambda d: _pack([d[k] for k in SMALL])
    s_delta, s_m, s_v = adamw("adamw_small", pk(W), pk(small_red), pk(M), pk(V))
    shard_shapes = [W[k].shape for k in SMALL]
    out_g, out_d, out_m, out_v = dict(small_red), {}, {}, {}
    for dct, packed in ((out_d, s_delta), (out_m, s_m), (out_v, s_v)):
        dct.update(zip(SMALL, _unpack(packed, shard_shapes)))

    finish_scatter(0, "mix", s_v)
    full = join_row_halves([GRAD[k] for k in BIG])
    for k, g in zip(BIG, full):
        out_d[k], out_m[k], out_v[k], out_g[k] = adamw("adamw_" + k, W[k], g, M[k], V[k], with_grad=True)

    outs = [loss, grad_x.reshape(x.shape)]
    for dct in (out_g, out_d, out_m, out_v):
        outs += [dct[k] for k in WEIGHTS]
    return tuple(outs)


def kernel(x, p, hg_w_in, hg_lb_logits, hg_out_norm, hg_w_out, gm_w_in, gm_ln_g, gm_ln_b, gm_w_s, gm_b_s, gm_w_out, norm_mix_pre, norm_mix_post, norm_ffn_pre, norm_ffn_post, ffn_w_gate, ffn_w_up, ffn_w_down, ple_w_proj, ple_w_gate, ple_norm, loss_target, m_hg_w_in, m_hg_lb_logits, m_hg_out_norm, m_hg_w_out, m_gm_w_in, m_gm_ln_g, m_gm_ln_b, m_gm_w_s, m_gm_b_s, m_gm_w_out, m_norm_mix_pre, m_norm_mix_post, m_norm_ffn_pre, m_norm_ffn_post, m_ffn_w_gate, m_ffn_w_up, m_ffn_w_down, m_ple_w_proj, m_ple_w_gate, m_ple_norm, v_hg_w_in, v_hg_lb_logits, v_hg_out_norm, v_hg_w_out, v_gm_w_in, v_gm_ln_g, v_gm_ln_b, v_gm_w_s, v_gm_b_s, v_gm_w_out, v_norm_mix_pre, v_norm_mix_post, v_norm_ffn_pre, v_norm_ffn_post, v_ffn_w_gate, v_ffn_w_up, v_ffn_w_down, v_ple_w_proj, v_ple_w_gate, v_ple_norm):
    W = dict(zip(WEIGHTS, (hg_w_in, hg_lb_logits, hg_out_norm, hg_w_out, gm_w_in, gm_ln_g, gm_ln_b, gm_w_s, gm_b_s, gm_w_out,
                           norm_mix_pre, norm_mix_post, norm_ffn_pre, norm_ffn_post, ffn_w_gate, ffn_w_up, ffn_w_down,
                           ple_w_proj, ple_w_gate, ple_norm)))
    M = dict(zip(WEIGHTS, (m_hg_w_in, m_hg_lb_logits, m_hg_out_norm, m_hg_w_out, m_gm_w_in, m_gm_ln_g, m_gm_ln_b, m_gm_w_s,
                           m_gm_b_s, m_gm_w_out, m_norm_mix_pre, m_norm_mix_post, m_norm_ffn_pre, m_norm_ffn_post,
                           m_ffn_w_gate, m_ffn_w_up, m_ffn_w_down, m_ple_w_proj, m_ple_w_gate, m_ple_norm)))
    V = dict(zip(WEIGHTS, (v_hg_w_in, v_hg_lb_logits, v_hg_out_norm, v_hg_w_out, v_gm_w_in, v_gm_ln_g, v_gm_ln_b, v_gm_w_s,
                           v_gm_b_s, v_gm_w_out, v_norm_mix_pre, v_norm_mix_post, v_norm_ffn_pre, v_norm_ffn_post,
                           v_ffn_w_gate, v_ffn_w_up, v_ffn_w_down, v_ple_w_proj, v_ple_w_gate, v_ple_norm)))
    return _step(x, p, W, M, V, loss_target)
```

```python
import functools

import jax
import jax.numpy as jnp
from jax import lax
from jax.experimental import pallas as pl
from jax.experimental.pallas import tpu as pltpu

F32 = jnp.float32
BF16 = jnp.bfloat16
MESH_ID = pl.DeviceIdType.MESH

LANES = 128
N_CHIPS = 4
N_DEV = 8
VMEM_LIMIT = 56 * 1024 * 1024
HG_SUB = 64
HG_BLOCK = 256
HG_HEADS_PER = 4
GM_CHUNK = 128
GM_BLOCK = 512
PACK_ROWS = 512
LB_FLOOR = 1e-30
EPS = 1e-6
ADAM_LR, ADAM_B1, ADAM_B2, ADAM_EPS, ADAM_WD, ADAM_STEP = 0.001, 0.9, 0.999, 1e-08, 0.01, 10


def _tile(n, pref, mult=LANES):
    if n <= pref:
        return n
    t = (pref // mult) * mult
    while t >= mult:
        if n % t == 0:
            return t
        t -= mult
    return n


def _cp(n_axes):
    return pltpu.CompilerParams(dimension_semantics=("arbitrary",) * n_axes, vmem_limit_bytes=VMEM_LIMIT)


def _dense(block):
    return tuple(b for b in block if b is not None)


def _bmap(name, grid, ins, outs, compute, scalars=(), into=None):
    n_s, n_in = len(scalars), len(ins)
    n_extra = 0 if into is None else 1

    def body(*refs):
        in_refs = refs[n_s:n_s + n_in]
        out_refs = refs[n_s + n_in + n_extra:]
        vals = [r[...] for r in in_refs]
        res = compute(*vals)
        for r, o, spec in zip(out_refs, res, outs):
            keep = spec[4]
            if keep is None:
                r[...] = o.astype(r.dtype)
            else:
                first = functools.reduce(jnp.logical_and, [pl.program_id(a) == 0 for a in range(keep, len(grid))])

                @pl.when(first)
                def _():
                    r[...] = jnp.zeros(r.shape, r.dtype)

                r[...] += o.astype(r.dtype)

    grid_spec = pltpu.PrefetchScalarGridSpec(
        num_scalar_prefetch=n_s, grid=grid,
        in_specs=[pl.BlockSpec(b, m) for _, b, m in ins] + [pl.BlockSpec(memory_space=pl.ANY)] * n_extra,
        out_specs=[pl.BlockSpec(o[2], o[3]) for o in outs])
    return pl.pallas_call(
        body, name=name, grid_spec=grid_spec,
        out_shape=[jax.ShapeDtypeStruct(o[0], o[1]) for o in outs],
        input_output_aliases={n_s + n_in: 0} if n_extra else {},
        compiler_params=_cp(len(grid)),
    )(*scalars, *[a for a, _, _ in ins], *([into] if n_extra else []))


def bmap_fwd(name, fn, grid, ins, outs, scalars=(), into=None):
    return _bmap(name, grid, ins, outs, lambda *v: fn(*[x.astype(F32) for x in v]), scalars, into)


def bmap_bwd(name, fn, grid, ins, cots, grads, scalars=()):
    n_in = len(ins)
    diff = [g[0] for g in grads]
    cot_ins = [c for c in cots if c is not None]

    def compute(*vals):
        xs = [v.astype(F32) for v in vals[:n_in]]
        cvals = list(vals[n_in:])

        def f(*d):
            full = list(xs)
            for i, dv in zip(diff, d):
                full[i] = dv
            return tuple(fn(*full))

        res, pull = jax.vjp(f, *[xs[i] for i in diff])
        cts = []
        for r, c in zip(res, cots):
            cts.append(jnp.zeros_like(r) if c is None else cvals.pop(0).astype(F32))
        return pull(tuple(cts))

    outs = [(ins[i][0].shape, dt, ins[i][1], ins[i][2], keep) for i, dt, keep in grads]
    return _bmap(name, grid, list(ins) + cot_ins, outs, compute, scalars)


def _mm(name, a, b, out_shape, out_dtype, grid, a_spec, b_spec, o_spec, dims, addend=None, alias_out=None):
    nk = grid[2]
    o_dense = _dense(o_spec[0])
    has_add = addend is not None
    has_alias = alias_out is not None

    def body(*refs):
        a_ref, b_ref = refs[0], refs[1]
        pos = 2
        c_ref = None
        if has_add:
            c_ref = refs[pos]
            pos += 1
        if has_alias:
            pos += 1
        o_ref = refs[pos]
        acc_ref = refs[pos + 1] if nk > 1 else None
        p = lax.dot_general(a_ref[...].astype(BF16), b_ref[...].astype(BF16), (dims, ((), ())),
                            preferred_element_type=F32)

        def finish(total):
            if has_add:
                total = total + c_ref[...].astype(F32)
            o_ref[...] = total.astype(o_ref.dtype)

        if nk == 1:
            finish(p)
        else:
            k = pl.program_id(2)

            @pl.when(k == 0)
            def _():
                acc_ref[...] = p

            @pl.when(jnp.logical_and(k > 0, k < nk - 1))
            def _():
                acc_ref[...] += p

            @pl.when(k == nk - 1)
            def _():
                finish(acc_ref[...] + p)

    in_specs = [pl.BlockSpec(*a_spec), pl.BlockSpec(*b_spec)]
    operands = [a, b]
    if has_add:
        in_specs.append(pl.BlockSpec(o_spec[0], o_spec[1]))
        operands.append(addend)
    aliases = {}
    if has_alias:
        in_specs.append(pl.BlockSpec(memory_space=pl.ANY))
        aliases = {len(operands): 0}
        operands.append(alias_out)
    return pl.pallas_call(
        body, name=name, grid=grid, in_specs=in_specs, out_specs=pl.BlockSpec(*o_spec),
        out_shape=jax.ShapeDtypeStruct(out_shape, out_dtype),
        scratch_shapes=[pltpu.VMEM(o_dense, F32)] if nk > 1 else [],
        input_output_aliases=aliases,
        compiler_params=pltpu.CompilerParams(dimension_semantics=("parallel", "parallel", "arbitrary"),
                                             vmem_limit_bytes=VMEM_LIMIT),
    )(*operands)


NN, NT, TN = ((1,), (0,)), ((1,), (1,)), ((0,), (0,))
TM = 512
TT = 1024
TN_PREF = 1408


def mm_fwd(name, x, wg, l, kind, out_dtype=F32, parts=False, xl=None):
    if isinstance(wg, dict):
        wg, l = wg[l], 0
    _, _, R, C = wg.shape
    T = x.shape[-2]
    tm = _tile(T, TM, 8)
    if kind == "col":
        tn = _tile(C, TN_PREF)
        npc = C // tn
        grid = (T // tm, N_CHIPS * npc, 1)
        a_blk = (tm, R) if xl is None else (None, tm, R)
        a_map = (lambda i, j, k: (i, 0)) if xl is None else (lambda i, j, k: (xl, i, 0))
        b_spec = ((None, None, R, tn), lambda i, j, k: (j // npc, l, 0, j % npc))
        if parts:
            out_shape = (N_CHIPS, T, C)
            o_spec = ((None, tm, tn), lambda i, j, k: (j // npc, i, j % npc))
        else:
            out_shape = (T, N_CHIPS * C)
            o_spec = ((tm, tn), lambda i, j, k: (i, j))
    else:
        tn = _tile(C, 1024)
        grid = (T // tm, C // tn, N_CHIPS)
        a_blk = (tm, R)
        a_map = lambda i, j, k: (i, k)
        b_spec = ((None, None, R, tn), lambda i, j, k: (k, l, 0, j))
        out_shape = (T, C)
        o_spec = ((tm, tn), lambda i, j, k: (i, j))
    return _mm(name, x, wg, out_shape, out_dtype, grid, (a_blk, a_map), b_spec, o_spec, NN)


def mm_bwd_x(name, dy, wg, l, kind, out_dtype=F32, parts=False, addend=None):
    if isinstance(wg, dict):
        wg, l = wg[l], 0
    _, _, R, C = wg.shape
    T = dy.shape[-2]
    tm = _tile(T, TM, 8)
    if kind == "col":
        tk = _tile(C, TN_PREF)
        npc = C // tk
        tno = _tile(R, 2048)
        grid = (T // tm, R // tno, N_CHIPS * npc)
        if parts:
            a_spec = ((None, tm, tk), lambda i, j, k: (k // npc, i, k % npc))
        else:
            a_spec = ((tm, tk), lambda i, j, k: (i, k))
        b_spec = ((None, None, tno, tk), lambda i, j, k: (k // npc, l, j, k % npc))
        out_shape = (T, R)
        o_spec = ((tm, tno), lambda i, j, k: (i, j))
    else:
        grid = (T // tm, N_CHIPS, 1)
        a_spec = ((tm, C), lambda i, j, k: (i, 0))
        b_spec = ((None, None, R, C), lambda i, j, k: (j, l, 0, 0))
        out_shape = (T, N_CHIPS * R)
        o_spec = ((tm, R), lambda i, j, k: (i, j))
    return _mm(name, dy, wg, out_shape, out_dtype, grid, a_spec, b_spec, o_spec, NT, addend=addend)


def mm_bwd_w(name, x, dy, dwg, l, kind, parts=False, xl=None):
    if isinstance(dwg, dict):
        return {**dwg, l: mm_bwd_w(name, x, dy, dwg[l], 0, kind, parts=parts, xl=xl)}
    _, _, R, C = dwg.shape
    T = dy.shape[-2]
    tt = _tile(T, TT, 16)
    nt = T // tt
    if kind == "col":
        tn = _tile(C, TN_PREF)
        npc = C // tn
        tr = _tile(R, 1024)
        grid = (R // tr, N_CHIPS * npc, nt)
        if xl is None:
            a_spec = ((tt, tr), lambda i, j, t: (t, i))
        else:
            a_spec = ((None, tt, tr), lambda i, j, t: (xl, t, i))
        if parts:
            b_spec = ((None, tt, tn), lambda i, j, t: (j // npc, t, j % npc))
        else:
            b_spec = ((tt, tn), lambda i, j, t: (t, j))
        o_spec = ((None, None, tr, tn), lambda i, j, t: (j // npc, l, i, j % npc))
    else:
        tn = _tile(C, 1024)
        grid = (N_CHIPS, C // tn, nt)
        a_spec = ((tt, R), lambda i, j, t: (t, i))
        b_spec = ((tt, tn), lambda i, j, t: (t, j))
        o_spec = ((None, None, R, tn), lambda i, j, t: (i, l, 0, j))
    return _mm(name, x, dy, dwg.shape, dwg.dtype, grid, a_spec, b_spec, o_spec, TN, alias_out=dwg)


def _sigmoid(x):
    return 0.5 * jnp.tanh(0.5 * x) + 0.5


def f_swiglu(gate, up):
    return (gate * _sigmoid(gate) * up,)


def ffn_gate_up(name, x, wg_gate, wg_up, l):
    if isinstance(wg_gate, dict):
        wg_gate, wg_up, l = wg_gate[l], wg_up[l], 0
    _, _, R, C = wg_gate.shape
    T = x.shape[0]
    tm = _tile(T, TM, 8)
    tn = _tile(C, TN_PREF)
    npc = C // tn

    def body(x_ref, g_ref, u_ref, gate_ref, up_ref, act_ref):
        xv = x_ref[...].astype(BF16)
        gate = jnp.dot(xv, g_ref[...], preferred_element_type=F32)
        up = jnp.dot(xv, u_ref[...], preferred_element_type=F32)
        gate_ref[...] = gate
        up_ref[...] = up
        act_ref[...] = f_swiglu(gate, up)[0].astype(act_ref.dtype)

    w_spec = pl.BlockSpec((None, None, R, tn), lambda i, j: (j // npc, l, 0, j % npc))
    o_spec = pl.BlockSpec((tm, tn), lambda i, j: (i, j))
    N = N_CHIPS * C
    return pl.pallas_call(
        body, name=name, grid=(T // tm, N_CHIPS * npc),
        in_specs=[pl.BlockSpec((tm, R), lambda i, j: (i, 0)), w_spec, w_spec], out_specs=[o_spec, o_spec, o_spec],
        out_shape=[jax.ShapeDtypeStruct((T, N), F32), jax.ShapeDtypeStruct((T, N), F32), jax.ShapeDtypeStruct((T, N), BF16)],
        compiler_params=_cp(2),
    )(x, wg_gate, wg_up)


def ffn_down_dx(name, df, wg_down, l, gate, up):
    if isinstance(wg_down, dict):
        wg_down, l = wg_down[l], 0
    _, _, R, C = wg_down.shape
    T = df.shape[0]
    tm = _tile(T, TM, 8)

    def body(df_ref, w_ref, gate_ref, up_ref, dg_ref, du_ref):
        dact = lax.dot_general(df_ref[...].astype(BF16), w_ref[...], (NT, ((), ())), preferred_element_type=F32)
        _, pull = jax.vjp(lambda g, u: f_swiglu(g, u)[0], gate_ref[...], up_ref[...])
        dg, du = pull(dact)
        dg_ref[...] = dg.astype(dg_ref.dtype)
        du_ref[...] = du.astype(du_ref.dtype)

    t_spec = pl.BlockSpec((tm, R), lambda i, j: (i, j))
    return pl.pallas_call(
        body, name=name, grid=(T // tm, N_CHIPS),
        in_specs=[pl.BlockSpec((tm, C), lambda i, j: (i, 0)), pl.BlockSpec((None, None, R, C), lambda i, j: (j, l, 0, 0)),
                  t_spec, t_spec],
        out_specs=[t_spec, t_spec],
        out_shape=[jax.ShapeDtypeStruct((T, N_CHIPS * R), BF16)] * 2,
        compiler_params=_cp(2),
    )(df, wg_down, gate, up)


def _rms(x, g):
    return x * lax.rsqrt(jnp.mean(x * x, axis=-1, keepdims=True) + EPS) * g


def f_prenorm(h, g):
    return (_rms(h, g),)


def f_prenorm_thru(h, g):
    return _rms(h, g), h


def f_post_pre(h, m, g_post, g_pre):
    h1 = h + _rms(m, g_post)
    return h1, _rms(h1, g_pre)


def f_post(h1, f, g):
    return (h1 + _rms(f, g),)


def f_ple(h2, e, zg, g):
    return (h2 + _rms(e * _sigmoid(zg), g),)


def f_ple_pre(h2, e, zg, g, g_next):
    h3 = h2 + _rms(e * _sigmoid(zg), g)
    return h3, _rms(h3, g_next)


def _gelu(x):
    return 0.5 * x * (1.0 + lax.erf(x * 0.7071067811865476))


def f_gm_in(z, ln_g, ln_b):
    w = z.shape[-1] // 2
    u = _gelu(z[:, :w])
    v = _gelu(z[:, w:])
    mu = jnp.mean(v, axis=-1, keepdims=True)
    vc = v - mu
    vn = vc * lax.rsqrt(jnp.mean(vc * vc, axis=-1, keepdims=True) + EPS) * ln_g + ln_b
    return u, vn


def f_gm_spatial(u, vn, ws, bs):
    t = lax.broadcasted_iota(jnp.int32, ws.shape, 0)
    s = lax.broadcasted_iota(jnp.int32, ws.shape, 1)
    wm = jnp.where(t >= s, ws, 0.0).astype(BF16)
    ys = []
    for n in range(u.shape[0] // GM_CHUNK):
        rows = slice(n * GM_CHUNK, (n + 1) * GM_CHUNK)
        sv = jnp.dot(wm, vn[rows].astype(BF16), preferred_element_type=F32) + bs
        ys.append(u[rows] * sv)
    return (jnp.concatenate(ys, axis=0) if len(ys) > 1 else ys[0],)


def f_adam(w, g, m, v):
    m = ADAM_B1 * m + (1.0 - ADAM_B1) * g
    v = ADAM_B2 * v + (1.0 - ADAM_B2) * jnp.square(g)
    m_hat = m / (1.0 - ADAM_B1 ** ADAM_STEP)
    v_hat = v / (1.0 - ADAM_B2 ** ADAM_STEP)
    delta = -ADAM_LR * (m_hat / (jnp.sqrt(v_hat) + ADAM_EPS) + ADAM_WD * w)
    return delta, m, v


def _make_f_lb(n_layers):
    def f_lb(*logits):
        mx = functools.reduce(jnp.maximum, logits)
        ex = [jnp.exp(r - mx) for r in logits]
        tot = functools.reduce(lambda a, b: a + b, ex)
        sm = [e / tot for e in ex]
        outs = []
        run = jnp.zeros_like(sm[0])
        for j in range(n_layers):
            if j > 0:
                run = run + sm[j]
            lb = run
            outs += [jnp.log(jnp.maximum(lb, LB_FLOOR)), jnp.log(1.0 - lb), 1.0 - lb]
        return tuple(outs)
    return f_lb


def rows_fwd(name, fn, rows, params, out_dtypes, tm=256):
    T = rows[0].shape[0]
    tm = _tile(T, tm, 16)
    ins = [(r, (tm, r.shape[1]), lambda i: (i, 0)) for r in rows]
    ins += [(p, p.shape, lambda i: (0, 0)) for p in params]
    shapes = jax.eval_shape(lambda *a: fn(*a), *[jax.ShapeDtypeStruct((tm, r.shape[1]), F32) for r in rows],
                            *[jax.ShapeDtypeStruct(p.shape, F32) for p in params])
    outs = [((T, s.shape[1]), dt, (tm, s.shape[1]), lambda i: (i, 0), None) for s, dt in zip(shapes, out_dtypes)]
    return bmap_fwd(name, fn, (T // tm,), ins, outs)


def rows_bwd(name, fn, rows, params, cots, row_grad_dtypes, tm=256):
    T = rows[0].shape[0]
    tm = _tile(T, tm, 16)
    ins = [(r, (tm, r.shape[1]), lambda i: (i, 0)) for r in rows]
    ins += [(p, p.shape, lambda i: (0, 0)) for p in params]
    cts = [None if c is None else (c, (tm, c.shape[1]), lambda i: (i, 0)) for c in cots]
    grads = [(i, dt, None) for i, dt in enumerate(row_grad_dtypes) if dt is not None]
    grads += [(len(rows) + j, F32, 0) for j in range(len(params))]
    return bmap_bwd(name, fn, (T // tm,), ins, cts, grads)


def _log_sigmoid(z):
    return jnp.minimum(z, 0.0) - jnp.log(1.0 + jnp.exp(-jnp.abs(z)))


def _hg_gates(zf, ll0, ll1, oml):
    x2 = ll1 + _log_sigmoid(zf)
    mx = jnp.maximum(ll0, x2)
    g = mx + jnp.log(jnp.exp(ll0 - mx) + jnp.exp(x2 - mx))
    return g, oml * _sigmoid(-zf)


def hg_constants(n):
    levels = n.bit_length() - 1
    r = jnp.arange(n, dtype=jnp.int32)
    bounds = [r] + [((r >> (s + 1)) << (s + 1)) + ((1 << s) - 1) for s in range(levels)]
    sel = jnp.concatenate([(r[None, :] <= bd[:, None]) for bd in bounds], axis=0).astype(BF16)
    later = jnp.stack([((r >> s) & 1) for s in range(levels)])
    sign = jnp.broadcast_to((2 * later - 1).astype(F32)[:, :, None], (levels, n, LANES))
    pair = jnp.stack([((r[:, None] >> (s + 1)) == (r[None, :] >> (s + 1))) & (later[s][:, None] == 1) & (later[s][None, :] == 0)
                      for s in range(levels)]).astype(F32)
    return sel, sel.T, sign, pair


def _dot2(m, x):
    hi = x.astype(BF16)
    lo = (x - hi.astype(F32)).astype(BF16)
    p = jnp.dot(m, jnp.concatenate([hi, lo], axis=1), preferred_element_type=F32)
    w = x.shape[1]
    return p[:, :w] + p[:, w:]


@jax.custom_vjp
def _sel_dot(sel, selt, g):
    return _dot2(sel, g)


def _sel_dot_fwd(sel, selt, g):
    return _dot2(sel, g), (sel, selt)


def _sel_dot_bwd(res, d):
    sel, selt = res
    return jnp.zeros_like(sel), jnp.zeros_like(selt), _dot2(selt, d)


_sel_dot.defvjp(_sel_dot_fwd, _sel_dot_bwd)


def _hg_state(st, zf, zi, ll0, ll1, oml, tri):
    g, k = _hg_gates(zf, ll0, ll1, oml)
    b = _dot2(tri, g)
    tot = jnp.sum(g, axis=0, keepdims=True)
    kd = k * jnp.exp(tot - b)
    return st * jnp.exp(tot) + jnp.dot(zi.T.astype(BF16), kd.astype(BF16), preferred_element_type=F32)


def _hg_step(st, zq, zf, zi, zg, ll0, ll1, oml, onorm, sel, selt, sign, pair):
    n = zq.shape[0]
    levels = n.bit_length() - 1
    q = zq * _sigmoid(zq)
    g, k = _hg_gates(zf, ll0, ll1, oml)
    sums = _sel_dot(sel, selt, g)
    b = sums[:n]
    tot = jnp.sum(g, axis=0, keepdims=True)
    o = lax.dot_general((q * jnp.exp(b)).astype(BF16), st.astype(BF16), (NT, ((), ())), preferred_element_type=F32)
    a = jnp.zeros((n, n), F32)
    for s in range(levels):
        e = jnp.exp(sign[s] * (b - sums[(s + 1) * n:(s + 2) * n]))
        al = lax.dot_general((q * e).astype(BF16), (k * e).astype(BF16), (NT, ((), ())), preferred_element_type=F32)
        a = a + pair[s] * al
    o = o + jnp.dot(a.astype(BF16), zi.astype(BF16), preferred_element_type=F32)
    o = o + jnp.sum(q * k, axis=1, keepdims=True) * zi
    kd = k * jnp.exp(tot - b)
    st_new = st * jnp.exp(tot) + jnp.dot(zi.T.astype(BF16), kd.astype(BF16), preferred_element_type=F32)
    og = _rms(o, onorm) * (zg * _sigmoid(zg))
    return og, st_new


def _whole(arr, n_grid):
    zeros = (0,) * arr.ndim
    return pl.BlockSpec(arr.shape, (lambda h, n: zeros) if n_grid == 2 else (lambda i: zeros))


def _hg_dims(proj4, n_seq):
    _, T, D = proj4.shape
    S = T // n_seq
    hp = HG_HEADS_PER if (D // LANES) % HG_HEADS_PER == 0 else 1
    tb = min(HG_BLOCK, S)
    streams = [(b, hl) for b in range(n_seq) for hl in range(hp)]
    return T, D, S, hp, D // (LANES * hp), LANES * hp, tb, S // tb, tb // HG_SUB, streams


def hgrn_fwd(name, proj4, ll0, ll1, oml, onorm, n_seq):
    T, D, S, hp, n_hg, W, tb, nblk, nsub, streams = _hg_dims(proj4, n_seq)
    ns = len(streams)

    def body(p_ref, ll0_ref, ll1_ref, oml_ref, on_ref, sel_ref, selt_ref, later_ref, pair_ref, og_ref, st_ref, st):
        @pl.when(pl.program_id(1) == 0)
        def _():
            st[...] = jnp.zeros(st.shape, F32)

        st_ref[...] = st[...]
        on = on_ref[...]

        def step(j, carry):
            r = pl.ds(pl.multiple_of(j * HG_SUB, HG_SUB), HG_SUB)
            consts = (sel_ref[...], selt_ref[...], later_ref[...], pair_ref[...])
            args = []
            for si, (b, hl) in enumerate(streams):
                ln = slice(hl * LANES, (hl + 1) * LANES)
                args.append((st[si], p_ref[0, b, r, ln], p_ref[1, b, r, ln], p_ref[2, b, r, ln], p_ref[3, b, r, ln],
                             ll0_ref[:, ln], ll1_ref[:, ln], oml_ref[:, ln], on) + consts)
            res = [_hg_step(*a) for a in args]
            for si, (b, hl) in enumerate(streams):
                og_ref[b, r, hl * LANES:(hl + 1) * LANES] = res[si][0].astype(og_ref.dtype)
                st[si] = res[si][1]
            return carry

        lax.fori_loop(0, nsub, step, 0)

    vec = pl.BlockSpec((1, W), lambda h, n: (0, h))
    consts = hg_constants(HG_SUB)
    og, states = pl.pallas_call(
        body, name=name, grid=(n_hg, nblk),
        in_specs=[pl.BlockSpec((4, n_seq, tb, W), lambda h, n: (0, 0, n, h)), vec, vec, vec,
                  pl.BlockSpec((1, LANES), lambda h, n: (0, 0))] + [_whole(c, 2) for c in consts],
        out_specs=[pl.BlockSpec((n_seq, tb, W), lambda h, n: (0, n, h)),
                   pl.BlockSpec((None, None, ns, LANES, LANES), lambda h, n: (h, n, 0, 0, 0))],
        out_shape=[jax.ShapeDtypeStruct((n_seq, S, D), BF16),
                   jax.ShapeDtypeStruct((n_hg, nblk, ns, LANES, LANES), F32)],
        scratch_shapes=[pltpu.VMEM((ns, LANES, LANES), F32)],
        compiler_params=_cp(2),
    )(proj4.reshape(4, n_seq, S, D), ll0, ll1, oml, onorm, *consts)
    return og.reshape(T, D), states


def hgrn_bwd(name, proj4, states, dog, ll0, ll1, oml, onorm, n_seq):
    T, D, S, hp, n_hg, W, tb, nblk, nsub, streams = _hg_dims(proj4, n_seq)
    ns = len(streams)

    def body(p_ref, st_ref, dog_ref, ll0_ref, ll1_ref, oml_ref, on_ref, sel_ref, selt_ref, later_ref, pair_ref,
             dp_ref, dll0_ref, dll1_ref, doml_ref, don_ref, sbuf, dst):
        n_id = pl.program_id(1)

        @pl.when(n_id == 0)
        def _():
            dst[...] = jnp.zeros(dst.shape, F32)
            for ref in (dll0_ref, dll1_ref, doml_ref):
                ref[...] = jnp.zeros(ref.shape, F32)

        @pl.when(jnp.logical_and(n_id == 0, pl.program_id(0) == 0))
        def _():
            don_ref[...] = jnp.zeros(don_ref.shape, F32)

        on = on_ref[...]

        def fwd(j, carry):
            r = pl.ds(pl.multiple_of(j * HG_SUB, HG_SUB), HG_SUB)
            tri = sel_ref[0:HG_SUB, :]
            args = []
            for si, (b, hl) in enumerate(streams):
                ln = slice(hl * LANES, (hl + 1) * LANES)
                args.append((carry[si], p_ref[1, b, r, ln], p_ref[2, b, r, ln],
                             ll0_ref[:, ln], ll1_ref[:, ln], oml_ref[:, ln], tri))
            for si in range(ns):
                sbuf[si, j] = carry[si]
            return tuple(_hg_state(*a) for a in args)

        lax.fori_loop(0, nsub, fwd, tuple(st_ref[si] for si in range(ns)))

        def bwd(jj, carry):
            j = nsub - 1 - jj
            r = pl.ds(pl.multiple_of(j * HG_SUB, HG_SUB), HG_SUB)
            args, cts = [], []
            for si, (b, hl) in enumerate(streams):
                ln = slice(hl * LANES, (hl + 1) * LANES)
                args.append((sbuf[si, j], p_ref[0, b, r, ln], p_ref[1, b, r, ln], p_ref[2, b, r, ln],
                             p_ref[3, b, r, ln], ll0_ref[:, ln], ll1_ref[:, ln], oml_ref[:, ln], on))
                cts.append((dog_ref[b, r, ln].astype(F32), dst[si]))
            consts = (sel_ref[...], selt_ref[...], later_ref[...], pair_ref[...])
            step_fn = lambda *a: _hg_step(*a, *consts)
            ds = [jax.vjp(step_fn, *a)[1](ct) for a, ct in zip(args, cts)]
            d_on = carry
            for si, (b, hl) in enumerate(streams):
                ln = slice(hl * LANES, (hl + 1) * LANES)
                d = ds[si]
                dst[si] = d[0]
                for part in range(4):
                    dp_ref[part, b, r, ln] = d[1 + part].astype(dp_ref.dtype)
                dll0_ref[:, ln] += d[5]
                dll1_ref[:, ln] += d[6]
                doml_ref[:, ln] += d[7]
                d_on = d_on + d[8]
            return d_on

        don_ref[...] += lax.fori_loop(0, nsub, bwd, jnp.zeros((1, LANES), F32))

    last = nblk - 1
    vec = pl.BlockSpec((1, W), lambda h, n: (0, h))
    one = pl.BlockSpec((1, LANES), lambda h, n: (0, 0))
    consts = hg_constants(HG_SUB)
    dproj, d0, d1, d2, d_on = pl.pallas_call(
        body, name=name, grid=(n_hg, nblk),
        in_specs=[pl.BlockSpec((4, n_seq, tb, W), lambda h, n: (0, 0, last - n, h)),
                  pl.BlockSpec((None, None, ns, LANES, LANES), lambda h, n: (h, last - n, 0, 0, 0)),
                  pl.BlockSpec((n_seq, tb, W), lambda h, n: (0, last - n, h)), vec, vec, vec, one]
        + [_whole(c, 2) for c in consts],
        out_specs=[pl.BlockSpec((4, n_seq, tb, W), lambda h, n: (0, 0, last - n, h)), vec, vec, vec, one],
        out_shape=[jax.ShapeDtypeStruct((4, n_seq, S, D), BF16)] + [jax.ShapeDtypeStruct((1, D), F32)] * 3
        + [jax.ShapeDtypeStruct((1, LANES), F32)],
        scratch_shapes=[pltpu.VMEM((ns, nsub, LANES, LANES), F32), pltpu.VMEM((ns, LANES, LANES), F32)],
        compiler_params=_cp(2),
    )(proj4.reshape(4, n_seq, S, D), states, dog.reshape(n_seq, S, D), ll0, ll1, oml, onorm, *consts)
    return dproj.reshape(4, T, D), d0, d1, d2, d_on


def _place():
    x, y, c = lax.axis_index("x"), lax.axis_index("y"), lax.axis_index("c")
    chips = [(1 - x, y), (x, 1 - y), (1 - x, 1 - y)]
    return x, y, c, chips


ANY = pl.BlockSpec(memory_space=pl.ANY)


def _comm_call(name, body, ins, out_shapes, sems, aliases=None):
    return pl.pallas_call(
        body, name=name, in_specs=[ANY] * len(ins), out_specs=[ANY] * len(out_shapes),
        out_shape=out_shapes, scratch_shapes=sems, input_output_aliases=aliases or {},
        compiler_params=pltpu.CompilerParams(has_side_effects=True),
    )(*ins)


HBM_SPEC = pl.BlockSpec(memory_space=pltpu.HBM)
SEM_SPEC = pl.BlockSpec(memory_space=pltpu.SEMAPHORE)
SPLIT_EFFECT = pltpu.SideEffectType.DATAFLOW_SIDE_EFFECTING
N_PEER_CHIPS = 3


def split_start(name, build, arrays, n_peers=N_PEER_CHIPS):
    n = len(arrays)

    def body(*refs):
        send, recv = refs[n], refs[n + 1]
        token = refs[2 * n + 2]
        starts, _ = build(refs[:n], send, recv)
        for cp in starts:
            cp.start()
        token[...] = jnp.zeros_like(token)

    res = pl.pallas_call(
        body, name=name,
        out_shape=(pltpu.SemaphoreType.DMA((n_peers,)), pltpu.SemaphoreType.DMA((n_peers,)),
                   *[pltpu.HBM(a.shape, a.dtype) for a in arrays], jax.ShapeDtypeStruct((8, LANES), F32)),
        in_specs=[HBM_SPEC] * n,
        out_specs=(SEM_SPEC, SEM_SPEC, *[HBM_SPEC] * n, pl.BlockSpec(memory_space=pltpu.VMEM)),
        input_output_aliases={i: 2 + i for i in range(n)},
        compiler_params=pltpu.CompilerParams(has_side_effects=SPLIT_EFFECT),
    )(*[pltpu.with_memory_space_constraint(a, pltpu.HBM) for a in arrays])
    return res[0], res[1], list(res[2:2 + n]), res[2 + n]


def split_wait(name, build, send, recv, arrays, after):
    n = len(arrays)

    def body(*refs):
        starts, arrivals = build(refs[:n], refs[n], refs[n + 1])
        for cp in starts:
            cp.wait_send()
        for cp in arrivals:
            cp.wait_recv()

    return list(pl.pallas_call(
        body, name=name, out_shape=tuple(pltpu.HBM(a.shape, a.dtype) for a in arrays),
        in_specs=[HBM_SPEC] * n + [SEM_SPEC, SEM_SPEC, ANY], out_specs=tuple([HBM_SPEC] * n),
        input_output_aliases={i: i for i in range(n)},
        compiler_params=pltpu.CompilerParams(has_side_effects=SPLIT_EFFECT),
    )(*arrays, send, recv, after))


def _row_half(ref, dim, who):
    rh = ref.shape[dim] // 2
    return pl.ds(who * rh, rh)


def gather_build(refs, send, recv):
    x, y, c, chips = _place()
    q = 2 * x + y
    starts, arrivals = [], []
    for buf in refs:
        rows = _row_half(buf, 2, c)
        for j, (px, py) in enumerate(chips):
            mine, got = buf.at[q, :, rows], buf.at[2 * px + py, :, rows]
            starts.append(pltpu.make_async_remote_copy(src_ref=mine, dst_ref=mine, send_sem=send.at[j], recv_sem=recv.at[j],
                                                       device_id=(px, py, c), device_id_type=MESH_ID))
            arrivals.append(pltpu.make_async_remote_copy(src_ref=got, dst_ref=got, send_sem=send.at[j], recv_sem=recv.at[j],
                                                         device_id=(px, py, c), device_id_type=MESH_ID))
    return starts, arrivals


def gather_forward(name, bufs):
    n = len(bufs)

    def body(*refs):
        dst = refs[n:2 * n]
        send, recv = refs[2 * n:]
        x, y, c, chips = _place()
        sib = (x, y, 1 - c)
        cps = []
        for a in range(n):
            for j, (px, py) in enumerate(chips):
                got = dst[a].at[2 * px + py, :, _row_half(dst[a], 2, c)]
                cps.append(pltpu.make_async_remote_copy(src_ref=got, dst_ref=got, send_sem=send.at[a, j], recv_sem=recv.at[a, j],
                                                        device_id=sib, device_id_type=MESH_ID))
        for cp in cps:
            cp.start()
        for a in range(n):
            for j, (px, py) in enumerate(chips):
                theirs = dst[a].at[2 * px + py, :, _row_half(dst[a], 2, 1 - c)]
                pltpu.make_async_remote_copy(src_ref=theirs, dst_ref=theirs, send_sem=send.at[a, j], recv_sem=recv.at[a, j],
                                             device_id=sib, device_id_type=MESH_ID).wait_recv()
        for cp in cps:
            cp.wait_send()

    outs = [jax.ShapeDtypeStruct(s.shape, s.dtype) for s in bufs]
    sems = [pltpu.SemaphoreType.DMA((n, N_PEER_CHIPS))] * 2
    return _comm_call(name, body, bufs, outs, sems, aliases={a: a for a in range(n)})


def scatter_build(refs, send, recv):
    n = len(refs) // 2
    x, y, c, _ = _place()
    starts, arrivals = [], []
    for a in range(n):
        src, dst = refs[a], refs[n + a]
        for k in range(1, N_DEV):
            px, py, pc = x ^ (k >> 2), y ^ ((k >> 1) & 1), c ^ (k & 1)
            theirs = src.at[2 * px + py, :, _row_half(src, 2, pc)]
            starts.append(pltpu.make_async_remote_copy(src_ref=theirs, dst_ref=dst.at[k - 1], send_sem=send.at[k - 1],
                                                       recv_sem=recv.at[k - 1], device_id=(px, py, pc), device_id_type=MESH_ID))
            arrivals.append(pltpu.make_async_remote_copy(src_ref=dst.at[k - 1], dst_ref=dst.at[k - 1], send_sem=send.at[k - 1],
                                                         recv_sem=recv.at[k - 1], device_id=(px, py, pc), device_id_type=MESH_ID))
    return starts, arrivals


def share_build(refs, send, recv):
    dst = refs[0]
    x, y, c, _ = _place()
    mine = dst.at[4 * x + 2 * y + c]
    starts, arrivals = [], []
    for k in range(1, N_DEV):
        px, py, pc = x ^ (k >> 2), y ^ ((k >> 1) & 1), c ^ (k & 1)
        got = dst.at[4 * px + 2 * py + pc]
        starts.append(pltpu.make_async_remote_copy(src_ref=mine, dst_ref=mine, send_sem=send.at[k - 1], recv_sem=recv.at[k - 1],
                                                   device_id=(px, py, pc), device_id_type=MESH_ID))
        arrivals.append(pltpu.make_async_remote_copy(src_ref=got, dst_ref=got, send_sem=send.at[k - 1], recv_sem=recv.at[k - 1],
                                                     device_id=(px, py, pc), device_id_type=MESH_ID))
    return starts, arrivals


def join_row_halves(name, bufs):
    n = len(bufs)

    def body(*refs):
        dst = refs[n:2 * n]
        send, recv = refs[2 * n:]
        x, y, c, _ = _place()
        cps = []
        for a in range(n):
            mine = dst[a].at[:, _row_half(dst[a], 1, c)]
            cps.append(pltpu.make_async_remote_copy(src_ref=mine, dst_ref=mine, send_sem=send.at[a], recv_sem=recv.at[a],
                                                    device_id=(x, y, 1 - c), device_id_type=MESH_ID))
        for cp in cps:
            cp.start()
        for a in range(n):
            theirs = dst[a].at[:, _row_half(dst[a], 1, 1 - c)]
            pltpu.make_async_remote_copy(src_ref=theirs, dst_ref=theirs, send_sem=send.at[a], recv_sem=recv.at[a],
                                         device_id=(x, y, 1 - c), device_id_type=MESH_ID).wait_recv()
        for cp in cps:
            cp.wait_send()

    outs = [jax.ShapeDtypeStruct(b.shape, b.dtype) for b in bufs]
    sems = [pltpu.SemaphoreType.DMA((n,))] * 2
    return _comm_call(name, body, bufs, outs, sems, aliases={a: a for a in range(n)})


def share_with_all(name, packed, me):
    slots = lax.dynamic_update_slice(jnp.zeros((N_DEV,) + packed.shape, packed.dtype), packed[None], (me, 0, 0))

    def body(_, dst, send, recv):
        x, y, c, _ = _place()
        me = 4 * x + 2 * y + c
        cps = []
        for k in range(1, N_DEV):
            px, py, pc = x ^ (k >> 2), y ^ ((k >> 1) & 1), c ^ (k & 1)
            cps.append(pltpu.make_async_remote_copy(src_ref=dst.at[me], dst_ref=dst.at[me], send_sem=send.at[k - 1],
                                                    recv_sem=recv.at[k - 1], device_id=(px, py, pc), device_id_type=MESH_ID))
        for cp in cps:
            cp.start()
        for k in range(1, N_DEV):
            px, py, pc = x ^ (k >> 2), y ^ ((k >> 1) & 1), c ^ (k & 1)
            got = dst.at[4 * px + 2 * py + pc]
            pltpu.make_async_remote_copy(src_ref=got, dst_ref=got, send_sem=send.at[k - 1], recv_sem=recv.at[k - 1],
                                         device_id=(px, py, pc), device_id_type=MESH_ID).wait_recv()
        for cp in cps:
            cp.wait_send()

    outs = [jax.ShapeDtypeStruct(slots.shape, slots.dtype)]
    sems = [pltpu.SemaphoreType.DMA((N_DEV - 1,))] * 2
    return _comm_call(name, body, [slots], outs, sems, aliases={0: 0})[0]


def _w_tiles(R, C):
    return _tile(R, max(16, (1 << 20) // (4 * C) // 16 * 16), 16)


def cast_bf16(w, l, q_arr):
    _, R, C = w.shape
    tr = _w_tiles(R, C)
    ins = [(w, (None, tr, C), lambda r, q: (l, r, 0))]
    outs = [((N_CHIPS, 1, R, C), BF16, (None, None, tr, C), lambda r, q: (q[0], 0, r, 0), None)]
    return bmap_fwd("cast_bf16", lambda a: (a,), (R // tr,), ins, outs, scalars=(q_arr,))[0]


def sum_partials(own, landed, into, l, q_arr, c_arr):
    n_land, _, rh, C = landed.shape
    tr = _w_tiles(rh, C)
    nb = rh // tr
    blk = (None, None, tr, C)
    ins = [(own, blk, lambda r, q, c: (q[0], 0, c[0] * nb + r, 0))]
    ins += [(landed, blk, (lambda r, q, c, kk=kk: (kk, 0, r, 0))) for kk in range(n_land)]
    outs = [(into.shape, F32, (None, tr, C), lambda r, q, c: (l, c[0] * nb + r, 0), None)]
    return bmap_fwd("sum_partials", lambda *t: (functools.reduce(lambda u, v: u + v, t),), (nb,), ins, outs,
                    scalars=(q_arr, c_arr), into=into)[0]


def sum_devices(slots):
    nd, NR, C = slots.shape
    tr = _tile(NR, 512, 8)
    ins = [(slots, (None, tr, C), (lambda r, dd=dd: (dd, r, 0))) for dd in range(nd)]
    outs = [((NR, C), F32, (tr, C), lambda r: (r, 0), None)]
    return bmap_fwd("sum_devices", lambda *a: (functools.reduce(lambda u, v: u + v, a),), (NR // tr,), ins, outs)[0]


def adamw(name, w, g, m, v, with_grad=False):
    if w.ndim == 2:
        R, C = w.shape
        tr = _w_tiles(R, C)
        spec = ((tr, C), lambda r: (r, 0))
        grid = (R // tr,)
    else:
        L, R, C = w.shape
        tr = _w_tiles(R, C)
        spec = ((None, tr, C), lambda l, r: (l, r, 0))
        grid = (L, R // tr)
    ins = [(a,) + spec for a in (w, g, m, v)]
    outs = [(w.shape, F32) + spec + (None,)] * (4 if with_grad else 3)
    fn = (lambda a, b, c, d: f_adam(a, b, c, d) + (b,)) if with_grad else f_adam
    return bmap_fwd(name, fn, grid, ins, outs)


def loss_and_grad(h, target):
    T, D = h.shape
    tm = _tile(T, 256, 8)

    def fn(hv, tv):
        d = hv - tv
        return jnp.sum(d * d, keepdims=True).reshape(1, 1) * (0.5 / D), d * (1.0 / D)

    ins = [(h, (tm, D), lambda i: (i, 0)), (target, (tm, D), lambda i: (i, 0))]
    outs = [((1, 1), F32, (1, 1), lambda i: (0, 0), 0), ((T, D), F32, (tm, D), lambda i: (i, 0), None)]
    return bmap_fwd("loss_and_grad", fn, (T // tm,), ins, outs)


BIG = ("hg_w_in", "hg_w_out", "gm_w_in", "gm_w_out", "ffn_w_gate", "ffn_w_up", "ffn_w_down", "ple_w_proj", "ple_w_gate")
KIND = {"hg_w_in": "col", "hg_w_out": "row", "gm_w_in": "col", "gm_w_out": "row", "ffn_w_gate": "col",
        "ffn_w_up": "col", "ffn_w_down": "row", "ple_w_proj": "col", "ple_w_gate": "row"}
SMALL = ("hg_lb_logits", "hg_out_norm", "gm_ln_g", "gm_ln_b", "gm_w_s", "gm_b_s", "norm_mix_pre", "norm_mix_post",
         "norm_ffn_pre", "norm_ffn_post", "ple_norm")
WEIGHTS = ("hg_w_in", "hg_lb_logits", "hg_out_norm", "hg_w_out", "gm_w_in", "gm_ln_g", "gm_ln_b", "gm_w_s", "gm_b_s",
           "gm_w_out", "norm_mix_pre", "norm_mix_post", "norm_ffn_pre", "norm_ffn_post", "ffn_w_gate", "ffn_w_up",
           "ffn_w_down", "ple_w_proj", "ple_w_gate", "ple_norm")


def _pack(arrs):
    rows = []
    for a in arrs:
        flat = a.reshape(-1)
        pad = (-flat.shape[0]) % (8 * LANES)
        rows.append(jnp.pad(flat, (0, pad)).reshape(-1, LANES))
    n_rows = sum(r.shape[0] for r in rows)
    rows.append(jnp.zeros(((-n_rows) % PACK_ROWS, LANES), F32))
    return jnp.concatenate(rows, axis=0)


def _unpack(packed, shapes):
    out, r = [], 0
    for s in shapes:
        size = 1
        for d in s:
            size *= d
        nr = -(-size // (8 * LANES)) * 8
        out.append(packed[r:r + nr].reshape(-1)[:size].reshape(s))
        r += nr
    return out


def _step(x, p, W, M, V, loss_target):
    n_seq, S, D = x.shape
    T = n_seq * S
    depth = p.shape[0]
    n_hg = W["hg_w_in"].shape[0]
    x2 = x.reshape(T, D)
    p3 = p.reshape(depth, T, p.shape[-1])
    tgt = loss_target.reshape(T, D)
    xi, yi, ci = lax.axis_index("x"), lax.axis_index("y"), lax.axis_index("c")
    q_me = 2 * xi + yi
    c_arr = jnp.reshape(ci, (1,)).astype(jnp.int32)
    q_arr = jnp.reshape(q_me, (1,)).astype(jnp.int32)

    groups = {}
    for i in range(depth):
        mix = ("hg_w_in", "hg_w_out") if i % 2 == 0 else ("gm_w_in", "gm_w_out")
        groups[i, "mix"] = [(k, i // 2) for k in mix]
        groups[i, "rest"] = [(k, i) for k in ("ffn_w_gate", "ffn_w_up", "ffn_w_down", "ple_w_proj", "ple_w_gate")]
    G = {k: {} for k in BIG}
    DW = {k: {l: lax.empty((N_CHIPS, 1) + W[k].shape[1:], BF16) for l in range(W[k].shape[0])} for k in BIG}
    in_flight = {}

    def start_gather(i, part, dep):
        qa = q_arr if dep is None else lax.optimization_barrier((q_arr, dep))[0]
        casts = [cast_bf16(W[k], l, qa) for k, l in groups[i, part]]
        send, recv, arrs, tok = split_start("gather_start_%d_%s" % (i, part), gather_build, casts)
        in_flight[i, part] = (send, recv, arrs)
        return tok

    def finish_gather(i, part, after):
        send, recv, arrs = in_flight.pop((i, part))
        arrs = split_wait("gather_wait_%d_%s" % (i, part), gather_build, send, recv, arrs, after)
        for (k, l), buf in zip(groups[i, part], gather_forward("gather_forward_%d_%s" % (i, part), arrs)):
            G[k][l] = buf
        return buf

    def after_token(row_arr, *toks):
        return functools.reduce(lambda u, t: u + t[0:1, 0:1], toks, row_arr)

    tok_mix = start_gather(0, "mix", None)
    tok_rest = start_gather(0, "rest", tok_mix)
    finish_gather(0, "mix", tok_mix + tok_rest)
    me = 4 * xi + 2 * yi + ci
    ln_full = share_with_all("share_ln", _pack([W["gm_ln_g"], W["gm_ln_b"]]), me)
    n_gm, dq = W["gm_ln_g"].shape
    ln_parts = [_unpack(ln_full[4 * qx + 2 * qy + 0], [(n_gm, dq), (n_gm, dq)]) for qx in range(2) for qy in range(2)]
    ln_g = jnp.concatenate([lp[0] for lp in ln_parts], axis=1)
    ln_b = jnp.concatenate([lp[1] for lp in ln_parts], axis=1)

    row = lambda a, i: a[i][None, :]
    f_lb = _make_f_lb(n_hg)
    lb_rows = [row(W["hg_lb_logits"], j) for j in range(n_hg)]
    one = (1, D)
    lb_ins = [(r, one, lambda i: (0, 0)) for r in lb_rows]
    lb_out = bmap_fwd("hg_lower_bounds", f_lb, (1,), lb_ins, [(one, F32, one, lambda i: (0, 0), None)] * (3 * n_hg))

    saved = []
    h = x2
    a = rows_fwd("prenorm", f_prenorm, [h], [row(W["norm_mix_pre"], 0)], [BF16])[0]
    for i in range(depth):
        j = i // 2
        sv = {"h": h, "a": a}
        if i > 0:
            finish_gather(i, "mix", h)
        if i % 2 == 0:
            proj4 = mm_fwd("hg_in", a, G["hg_w_in"], j, "col", parts=True)
            lbp = lb_out[3 * j:3 * j + 3]
            onorm = row(W["hg_out_norm"], j)
            og, states = hgrn_fwd("hgrn_fwd", proj4, *lbp, onorm, n_seq)
            m = mm_fwd("hg_out", og, G["hg_w_out"], j, "row")
            sv.update(proj4=proj4, states=states, og=og, lbp=lbp, onorm=onorm)
        else:
            z = mm_fwd("gm_in", a, G["gm_w_in"], j, "col")
            lg, lb_ = row(ln_g, j), row(ln_b, j)
            u, vn = rows_fwd("gm_gelu_ln", f_gm_in, [z], [lg, lb_], [F32, BF16], tm=128)
            ws = W["gm_w_s"][j]
            bs = W["gm_b_s"][j][:, :, None]
            gb = min(GM_BLOCK, S)
            sp_grid = (D // LANES, T // gb)
            sp_ins = [(u, (gb, LANES), lambda g, n: (n, g)), (vn, (gb, LANES), lambda g, n: (n, g)),
                      (ws, (None, GM_CHUNK, GM_CHUNK), lambda g, n: (g, 0, 0)),
                      (bs, (None, GM_CHUNK, 1), lambda g, n: (g, 0, 0))]
            y = bmap_fwd("gm_spatial", f_gm_spatial, sp_grid, sp_ins,
                         [((T, D), BF16, (gb, LANES), lambda g, n: (n, g), None)])[0]
            m = mm_fwd("gm_out", y, G["gm_w_out"], j, "row")
            sv.update(z=z, lg=lg, lb_=lb_, sp_ins=sp_ins, sp_grid=sp_grid, y=y)
        g_post, g_fpre = row(W["norm_mix_post"], i), row(W["norm_ffn_pre"], i)
        arrived = finish_gather(i, "rest", m)
        if i + 1 < depth:
            tok_mix = start_gather(i + 1, "mix", arrived)
            g_post = after_token(g_post, tok_mix, start_gather(i + 1, "rest", tok_mix))
        h1, fin = rows_fwd("mix_post_ffn_pre", f_post_pre, [h, m], [g_post, g_fpre], [F32, BF16])
        gate, up, act = ffn_gate_up("ffn_gate_up", fin, G["ffn_w_gate"], G["ffn_w_up"], i)
        f = mm_fwd("ffn_down", act, G["ffn_w_down"], i, "row")
        g_fpost = row(W["norm_ffn_post"], i)
        h2 = rows_fwd("ffn_post", f_post, [h1, f], [g_fpost], [F32])[0]
        e = mm_fwd("ple_proj", p3, G["ple_w_proj"], i, "col", xl=i)
        zg = mm_fwd("ple_gate", h2, G["ple_w_gate"], i, "row")
        g_ple = row(W["ple_norm"], i)
        sv.update(m=m, h1=h1, fin=fin, gate=gate, up=up, act=act, f=f, h2=h2, e=e, zg=zg,
                  g_post=g_post, g_fpre=g_fpre, g_fpost=g_fpost, g_ple=g_ple)
        if i + 1 < depth:
            g_next = row(W["norm_mix_pre"], i + 1)
            h, a = rows_fwd("ple_next_pre", f_ple_pre, [h2, e, zg], [g_ple, g_next], [F32, BF16])
            sv["g_next"] = g_next
        else:
            h = rows_fwd("ple_last", f_ple, [h2, e, zg], [g_ple], [F32])[0]
        saved.append(sv)

    loss_part, dh = loss_and_grad(h, tgt)
    loss = lax.psum(loss_part[0, 0], ("x", "y", "c"))

    sg = {k: [None] * W[k].shape[0] for k in ("norm_mix_pre", "norm_mix_post", "norm_ffn_pre", "norm_ffn_post", "ple_norm",
                                              "hg_out_norm", "gm_ln_g", "gm_ln_b", "gm_w_s", "gm_b_s")}
    d_lbp = [None] * (3 * n_hg)
    da_next = None
    GRAD = {k: lax.empty(W[k].shape, F32) for k in BIG}
    scattering = {}

    def start_scatter(i, part):
        dws = [DW[k][l] for k, l in groups[i, part]]
        lands = [lax.empty((N_DEV - 1, 1, g.shape[2] // 2, g.shape[3]), BF16) for g in dws]
        send, recv, arrs, tok = split_start("scatter_start_%d_%s" % (i, part), scatter_build, dws + lands, n_peers=N_DEV - 1)
        scattering[i, part] = (send, recv, arrs)
        return tok

    def finish_scatter(i, part, after):
        send, recv, arrs = scattering.pop((i, part))
        arrs = split_wait("scatter_wait_%d_%s" % (i, part), scatter_build, send, recv, arrs, after)
        n = len(groups[i, part])
        for (k, l), own, ld in zip(groups[i, part], arrs[:n], arrs[n:]):
            GRAD[k] = sum_partials(own, ld, GRAD[k], l, q_arr, c_arr)

    tok = None
    for i in reversed(range(depth)):
        j = i // 2
        sv = saved[i]
        if i + 1 < depth:
            g_ple_after = sv["g_ple"] + tok[0:1, 0:1]
            dh2, de, dzg, d_gple, d_gnext = rows_bwd("ple_next_pre_bwd", f_ple_pre, [sv["h2"], sv["e"], sv["zg"]],
                                                     [g_ple_after, sv["g_next"]], [dh, da_next], [F32, BF16, BF16])
            sg["norm_mix_pre"][i + 1] = d_gnext
        else:
            dh2, de, dzg, d_gple = rows_bwd("ple_last_bwd", f_ple, [sv["h2"], sv["e"], sv["zg"]], [sv["g_ple"]], [dh],
                                            [F32, BF16, BF16])
        sg["ple_norm"][i] = d_gple
        DW["ple_w_proj"] = mm_bwd_w("ple_proj_dw", p3, de, DW["ple_w_proj"], i, "col", xl=i)
        DW["ple_w_gate"] = mm_bwd_w("ple_gate_dw", sv["h2"], dzg, DW["ple_w_gate"], i, "row")
        dh2 = mm_bwd_x("ple_gate_dx", dzg, G["ple_w_gate"], i, "row", addend=dh2)
        dh1, df, d_gfpost = rows_bwd("ffn_post_bwd", f_post, [sv["h1"], sv["f"]], [sv["g_fpost"]], [dh2], [F32, BF16])
        sg["norm_ffn_post"][i] = d_gfpost
        dgate, dup = ffn_down_dx("ffn_down_dx", df, G["ffn_w_down"], i, sv["gate"], sv["up"])
        DW["ffn_w_down"] = mm_bwd_w("ffn_down_dw", sv["act"], df, DW["ffn_w_down"], i, "row")
        dfin = mm_bwd_x("ffn_gate_dx", dgate, G["ffn_w_gate"], i, "col")
        dfin = mm_bwd_x("ffn_up_dx", dup, G["ffn_w_up"], i, "col", addend=dfin)
        DW["ffn_w_gate"] = mm_bwd_w("ffn_gate_dw", sv["fin"], dgate, DW["ffn_w_gate"], i, "col")
        DW["ffn_w_up"] = mm_bwd_w("ffn_up_dw", sv["fin"], dup, DW["ffn_w_up"], i, "col")
        g_post_after = after_token(sv["g_post"], start_scatter(i, "rest"))
        dh, dm, d_gpost, d_gfpre = rows_bwd("mix_post_ffn_pre_bwd", f_post_pre, [sv["h"], sv["m"]],
                                            [g_post_after, sv["g_fpre"]], [dh1, dfin], [F32, BF16])
        sg["norm_mix_post"][i], sg["norm_ffn_pre"][i] = d_gpost, d_gfpre
        if i % 2 == 0:
            dog = mm_bwd_x("hg_out_dx", dm, G["hg_w_out"], j, "row")
            DW["hg_w_out"] = mm_bwd_w("hg_out_dw", sv["og"], dm, DW["hg_w_out"], j, "row")
            dproj4, d0, d1, d2, d_on = hgrn_bwd("hgrn_bwd", sv["proj4"], sv["states"], dog, *sv["lbp"], sv["onorm"], n_seq)
            d_lbp[3 * j:3 * j + 3] = [d0, d1, d2]
            sg["hg_out_norm"][j] = d_on
            da_next = mm_bwd_x("hg_in_dx", dproj4, G["hg_w_in"], j, "col", parts=True)
            DW["hg_w_in"] = mm_bwd_w("hg_in_dw", sv["a"], dproj4, DW["hg_w_in"], j, "col", parts=True)
        else:
            dy = mm_bwd_x("gm_out_dx", dm, G["gm_w_out"], j, "row")
            DW["gm_w_out"] = mm_bwd_w("gm_out_dw", sv["y"], dm, DW["gm_w_out"], j, "row")
            gb = sv["sp_ins"][0][1][0]
            du, dvn, dws, dbs = bmap_bwd("gm_spatial_bwd", f_gm_spatial, sv["sp_grid"], sv["sp_ins"],
                                         [(dy, (gb, LANES), lambda g, n: (n, g))],
                                         [(0, F32, None), (1, F32, None), (2, F32, 1), (3, F32, 1)])
            sg["gm_w_s"][j], sg["gm_b_s"][j] = dws, dbs[:, :, 0]
            dz, d_lg, d_lb = rows_bwd("gm_gelu_ln_bwd", f_gm_in, [sv["z"]], [sv["lg"], sv["lb_"]], [du, dvn], [BF16], tm=128)
            sg["gm_ln_g"][j], sg["gm_ln_b"][j] = d_lg, d_lb
            da_next = mm_bwd_x("gm_in_dx", dz, G["gm_w_in"], j, "col")
            DW["gm_w_in"] = mm_bwd_w("gm_in_dw", sv["a"], dz, DW["gm_w_in"], j, "col")
        tok = start_scatter(i, "mix")
        if i + 1 < depth:
            finish_scatter(i + 1, "rest", da_next)
            finish_scatter(i + 1, "mix", da_next)
    finish_scatter(0, "rest", tok)
    g0 = after_token(row(W["norm_mix_pre"], 0), tok)
    grad_x, d_g0 = rows_bwd("prenorm_bwd", f_prenorm_thru, [saved[0]["h"]], [g0], [da_next, dh], [F32])
    sg["norm_mix_pre"][0] = d_g0
    d_logits = bmap_bwd("hg_lower_bounds_bwd", f_lb, (1,), lb_ins, [(d, one, lambda i: (0, 0)) for d in d_lbp],
                        [(jj, F32, None) for jj in range(n_hg)])

    small_g = {k: jnp.stack([v.reshape(W[k].shape[1:] if k not in ("gm_ln_g", "gm_ln_b") else (D,)) for v in sg[k]])
               for k in sg}
    small_g["hg_lb_logits"] = jnp.concatenate(d_logits, axis=0)
    small_shapes = [small_g[k].shape for k in SMALL]
    packed_g = _pack([small_g[k] for k in SMALL])
    slots = lax.dynamic_update_slice(jnp.zeros((N_DEV,) + packed_g.shape, F32), packed_g[None], (me, 0, 0))
    s_send, s_recv, slots, tok_s = split_start("share_small_start", share_build, [slots], n_peers=N_DEV - 1)

    out_g, out_d, out_m, out_v = {}, {}, {}, {}
    late = [k for k, _ in groups[0, "mix"]]
    early = [k for k in BIG if k not in late]
    ready = lax.optimization_barrier(tuple(GRAD[k] for k in early) + (tok_s,))[:-1]
    for k, g in zip(early, join_row_halves("join_early", list(ready))):
        out_d[k], out_m[k], out_v[k], out_g[k] = adamw("adamw_" + k, W[k], g, M[k], V[k], with_grad=True)
    slots = split_wait("share_small_wait", share_build, s_send, s_recv, slots, out_v[early[-1]])[0]
    red = sum_devices(slots)
    small_red = dict(zip(SMALL, _unpack(red, small_shapes)))
    for k in ("gm_ln_g", "gm_ln_b"):
        small_red[k] = lax.dynamic_slice_in_dim(small_red[k], q_me * dq, dq, axis=1)
    pk = lambda d: _pack([d[k] for k in SMALL])
    s_delta, s_m, s_v = adamw("adamw_small", pk(W), pk(small_red), pk(M), pk(V))
    shard_shapes = [W[k].shape for k in SMALL]
    out_g.update(small_red)
    for dct, packed in ((out_d, s_delta), (out_m, s_m), (out_v, s_v)):
        dct.update(zip(SMALL, _unpack(packed, shard_shapes)))

    finish_scatter(0, "mix", s_v)
    for k, g in zip(late, join_row_halves("join_late", [GRAD[k] for k in late])):
        out_d[k], out_m[k], out_v[k], out_g[k] = adamw("adamw_" + k, W[k], g, M[k], V[k], with_grad=True)

    outs = [loss, grad_x.reshape(x.shape)]
    for dct in (out_g, out_d, out_m, out_v):
        outs += [dct[k] for k in WEIGHTS]
    return tuple(outs)


def kernel(x, p, hg_w_in, hg_lb_logits, hg_out_norm, hg_w_out, gm_w_in, gm_ln_g, gm_ln_b, gm_w_s, gm_b_s, gm_w_out, norm_mix_pre, norm_mix_post, norm_ffn_pre, norm_ffn_post, ffn_w_gate, ffn_w_up, ffn_w_down, ple_w_proj, ple_w_gate, ple_norm, loss_target, m_hg_w_in, m_hg_lb_logits, m_hg_out_norm, m_hg_w_out, m_gm_w_in, m_gm_ln_g, m_gm_ln_b, m_gm_w_s, m_gm_b_s, m_gm_w_out, m_norm_mix_pre, m_norm_mix_post, m_norm_ffn_pre, m_norm_ffn_post, m_ffn_w_gate, m_ffn_w_up, m_ffn_w_down, m_ple_w_proj, m_ple_w_gate, m_ple_norm, v_hg_w_in, v_hg_lb_logits, v_hg_out_norm, v_hg_w_out, v_gm_w_in, v_gm_ln_g, v_gm_ln_b, v_gm_w_s, v_gm_b_s, v_gm_w_out, v_norm_mix_pre, v_norm_mix_post, v_norm_ffn_pre, v_norm_ffn_post, v_ffn_w_gate, v_ffn_w_up, v_ffn_w_down, v_ple_w_proj, v_ple_w_gate, v_ple_norm):
    W = dict(zip(WEIGHTS, (hg_w_in, hg_lb_logits, hg_out_norm, hg_w_out, gm_w_in, gm_ln_g, gm_ln_b, gm_w_s, gm_b_s, gm_w_out,
                           norm_mix_pre, norm_mix_post, norm_ffn_pre, norm_ffn_post, ffn_w_gate, ffn_w_up, ffn_w_down,
                           ple_w_proj, ple_w_gate, ple_norm)))
    M = dict(zip(WEIGHTS, (m_hg_w_in, m_hg_lb_logits, m_hg_out_norm, m_hg_w_out, m_gm_w_in, m_gm_ln_g, m_gm_ln_b, m_gm_w_s,
                           m_gm_b_s, m_gm_w_out, m_norm_mix_pre, m_norm_mix_post, m_norm_ffn_pre, m_norm_ffn_post,
                           m_ffn_w_gate, m_ffn_w_up, m_ffn_w_down, m_ple_w_proj, m_ple_w_gate, m_ple_norm)))
    V = dict(zip(WEIGHTS, (v_hg_w_in, v_hg_lb_logits, v_hg_out_norm, v_hg_w_out, v_gm_w_in, v_gm_ln_g, v_gm_ln_b, v_gm_w_s,
                           v_gm_b_s, v_gm_w_out, v_norm_mix_pre, v_norm_mix_post, v_norm_ffn_pre, v_norm_ffn_post,
                           v_ffn_w_gate, v_ffn_w_up, v_ffn_w_down, v_ple_w_proj, v_ple_w_gate, v_ple_norm)))
    return _step(x, p, W, M, V, loss_target)
```

```python
import functools

import jax
import jax.numpy as jnp
from jax import lax
from jax.experimental import pallas as pl
from jax.experimental.pallas import tpu as pltpu

F32 = jnp.float32
BF16 = jnp.bfloat16
MESH_ID = pl.DeviceIdType.MESH

LANES = 128
N_CHIPS = 4
N_DEV = 8
VMEM_LIMIT = 56 * 1024 * 1024
HG_SUB = 64
HG_BLOCK = 256
HG_HEADS_PER = 4
GM_CHUNK = 128
GM_BLOCK = 512
PACK_ROWS = 512
LB_FLOOR = 1e-30
EPS = 1e-6
ADAM_LR, ADAM_B1, ADAM_B2, ADAM_EPS, ADAM_WD, ADAM_STEP = 0.001, 0.9, 0.999, 1e-08, 0.01, 10


def _tile(n, pref, mult=LANES):
    if n <= pref:
        return n
    t = (pref // mult) * mult
    while t >= mult:
        if n % t == 0:
            return t
        t -= mult
    return n


def _cp(n_axes):
    return pltpu.CompilerParams(dimension_semantics=("arbitrary",) * n_axes, vmem_limit_bytes=VMEM_LIMIT)


def _dense(block):
    return tuple(b for b in block if b is not None)


def _bmap(name, grid, ins, outs, compute, scalars=(), into=None):
    n_s, n_in = len(scalars), len(ins)
    n_extra = 0 if into is None else 1

    def body(*refs):
        in_refs = refs[n_s:n_s + n_in]
        out_refs = refs[n_s + n_in + n_extra:]
        vals = [r[...] for r in in_refs]
        res = compute(*vals)
        for r, o, spec in zip(out_refs, res, outs):
            keep = spec[4]
            if keep is None:
                r[...] = o.astype(r.dtype)
            else:
                first = functools.reduce(jnp.logical_and, [pl.program_id(a) == 0 for a in range(keep, len(grid))])

                @pl.when(first)
                def _():
                    r[...] = jnp.zeros(r.shape, r.dtype)

                r[...] += o.astype(r.dtype)

    grid_spec = pltpu.PrefetchScalarGridSpec(
        num_scalar_prefetch=n_s, grid=grid,
        in_specs=[pl.BlockSpec(b, m) for _, b, m in ins] + [pl.BlockSpec(memory_space=pl.ANY)] * n_extra,
        out_specs=[pl.BlockSpec(o[2], o[3]) for o in outs])
    return pl.pallas_call(
        body, name=name, grid_spec=grid_spec,
        out_shape=[jax.ShapeDtypeStruct(o[0], o[1]) for o in outs],
        input_output_aliases={n_s + n_in: 0} if n_extra else {},
        compiler_params=_cp(len(grid)),
    )(*scalars, *[a for a, _, _ in ins], *([into] if n_extra else []))


def bmap_fwd(name, fn, grid, ins, outs, scalars=(), into=None):
    return _bmap(name, grid, ins, outs, lambda *v: fn(*[x.astype(F32) for x in v]), scalars, into)


def bmap_bwd(name, fn, grid, ins, cots, grads, scalars=()):
    n_in = len(ins)
    diff = [g[0] for g in grads]
    cot_ins = [c for c in cots if c is not None]

    def compute(*vals):
        xs = [v.astype(F32) for v in vals[:n_in]]
        cvals = list(vals[n_in:])

        def f(*d):
            full = list(xs)
            for i, dv in zip(diff, d):
                full[i] = dv
            return tuple(fn(*full))

        res, pull = jax.vjp(f, *[xs[i] for i in diff])
        cts = []
        for r, c in zip(res, cots):
            cts.append(jnp.zeros_like(r) if c is None else cvals.pop(0).astype(F32))
        return pull(tuple(cts))

    outs = [(ins[i][0].shape, dt, ins[i][1], ins[i][2], keep) for i, dt, keep in grads]
    return _bmap(name, grid, list(ins) + cot_ins, outs, compute, scalars)


def _mm(name, a, b, out_shape, out_dtype, grid, a_spec, b_spec, o_spec, dims, addend=None, alias_out=None):
    nk = grid[2]
    o_dense = _dense(o_spec[0])
    o_dense = (o_dense[0] * o_dense[1], o_dense[2]) if len(o_dense) == 3 else o_dense
    has_add = addend is not None
    has_alias = alias_out is not None

    def body(*refs):
        a_ref, b_ref = refs[0], refs[1]
        pos = 2
        c_ref = None
        if has_add:
            c_ref = refs[pos]
            pos += 1
        if has_alias:
            pos += 1
        o_ref = refs[pos]
        acc_ref = refs[pos + 1] if nk > 1 else None
        bv = b_ref[...]
        if bv.ndim == 3:
            bv = bv.reshape(bv.shape[0] * bv.shape[1], bv.shape[2])
        p = lax.dot_general(a_ref[...].astype(BF16), bv.astype(BF16), (dims, ((), ())), preferred_element_type=F32)

        def finish(total):
            if has_add:
                total = total + c_ref[...].astype(F32)
            o_ref[...] = total.reshape(o_ref.shape).astype(o_ref.dtype)

        if nk == 1:
            finish(p)
        else:
            k = pl.program_id(2)

            @pl.when(k == 0)
            def _():
                acc_ref[...] = p

            @pl.when(jnp.logical_and(k > 0, k < nk - 1))
            def _():
                acc_ref[...] += p

            @pl.when(k == nk - 1)
            def _():
                finish(acc_ref[...] + p)

    in_specs = [pl.BlockSpec(*a_spec), pl.BlockSpec(*b_spec)]
    operands = [a, b]
    if has_add:
        in_specs.append(pl.BlockSpec(o_spec[0], o_spec[1]))
        operands.append(addend)
    aliases = {}
    if has_alias:
        in_specs.append(pl.BlockSpec(memory_space=pl.ANY))
        aliases = {len(operands): 0}
        operands.append(alias_out)
    return pl.pallas_call(
        body, name=name, grid=grid, in_specs=in_specs, out_specs=pl.BlockSpec(*o_spec),
        out_shape=jax.ShapeDtypeStruct(out_shape, out_dtype),
        scratch_shapes=[pltpu.VMEM(o_dense, F32)] if nk > 1 else [],
        input_output_aliases=aliases,
        compiler_params=pltpu.CompilerParams(dimension_semantics=("parallel", "parallel", "arbitrary"),
                                             vmem_limit_bytes=VMEM_LIMIT),
    )(*operands)


NN, NT, TN = ((1,), (0,)), ((1,), (1,)), ((0,), (0,))
TM = 512
TT = 1024
TN_PREF = 1408
WHOLE_K = 2048


def mm_fwd(name, x, wg, l, kind, out_dtype=F32, parts=False, xl=None):
    if isinstance(wg, dict):
        wg, l = wg[l], 0
    _, _, R, C = wg.shape
    T = x.shape[-2]
    tm = _tile(T, TM, 8)
    if kind == "col":
        tn = _tile(C, TN_PREF)
        npc = C // tn
        grid = (T // tm, N_CHIPS * npc, 1)
        a_blk = (tm, R) if xl is None else (None, tm, R)
        a_map = (lambda i, j, k: (i, 0)) if xl is None else (lambda i, j, k: (xl, i, 0))
        b_spec = ((None, None, R, tn), lambda i, j, k: (j // npc, l, 0, j % npc))
        if parts:
            out_shape = (N_CHIPS, T, C)
            o_spec = ((None, tm, tn), lambda i, j, k: (j // npc, i, j % npc))
        else:
            out_shape = (T, N_CHIPS * C)
            o_spec = ((tm, tn), lambda i, j, k: (i, j))
    elif N_CHIPS * R <= WHOLE_K:
        tn = _tile(C, 1024)
        grid = (T // tm, C // tn, 1)
        a_blk = (tm, N_CHIPS * R)
        a_map = lambda i, j, k: (i, 0)
        b_spec = ((N_CHIPS, None, R, tn), lambda i, j, k: (0, l, 0, j))
        out_shape = (T, C)
        o_spec = ((tm, tn), lambda i, j, k: (i, j))
    else:
        tn = _tile(C, 1024)
        grid = (T // tm, C // tn, N_CHIPS)
        a_blk = (tm, R)
        a_map = lambda i, j, k: (i, k)
        b_spec = ((None, None, R, tn), lambda i, j, k: (k, l, 0, j))
        out_shape = (T, C)
        o_spec = ((tm, tn), lambda i, j, k: (i, j))
    return _mm(name, x, wg, out_shape, out_dtype, grid, (a_blk, a_map), b_spec, o_spec, NN)


def mm_bwd_x(name, dy, wg, l, kind, out_dtype=F32, parts=False, addend=None):
    if isinstance(wg, dict):
        wg, l = wg[l], 0
    _, _, R, C = wg.shape
    T = dy.shape[-2]
    tm = _tile(T, TM, 8)
    if kind == "col":
        tk = _tile(C, TN_PREF)
        npc = C // tk
        tno = _tile(R, 2048)
        grid = (T // tm, R // tno, N_CHIPS * npc)
        if parts:
            a_spec = ((None, tm, tk), lambda i, j, k: (k // npc, i, k % npc))
        else:
            a_spec = ((tm, tk), lambda i, j, k: (i, k))
        b_spec = ((None, None, tno, tk), lambda i, j, k: (k // npc, l, j, k % npc))
        out_shape = (T, R)
        o_spec = ((tm, tno), lambda i, j, k: (i, j))
    elif N_CHIPS * R <= WHOLE_K:
        grid = (T // tm, 1, 1)
        a_spec = ((tm, C), lambda i, j, k: (i, 0))
        b_spec = ((N_CHIPS, None, R, C), lambda i, j, k: (0, l, 0, 0))
        out_shape = (T, N_CHIPS * R)
        o_spec = ((tm, N_CHIPS * R), lambda i, j, k: (i, 0))
    else:
        grid = (T // tm, N_CHIPS, 1)
        a_spec = ((tm, C), lambda i, j, k: (i, 0))
        b_spec = ((None, None, R, C), lambda i, j, k: (j, l, 0, 0))
        out_shape = (T, N_CHIPS * R)
        o_spec = ((tm, R), lambda i, j, k: (i, j))
    return _mm(name, dy, wg, out_shape, out_dtype, grid, a_spec, b_spec, o_spec, NT, addend=addend)


def mm_bwd_w(name, x, dy, dwg, l, kind, parts=False, xl=None):
    if isinstance(dwg, dict):
        return {**dwg, l: mm_bwd_w(name, x, dy, dwg[l], 0, kind, parts=parts, xl=xl)}
    _, _, R, C = dwg.shape
    T = dy.shape[-2]
    tt = _tile(T, TT, 16)
    nt = T // tt
    if kind == "col":
        tn = _tile(C, TN_PREF)
        npc = C // tn
        tr = _tile(R, 1024)
        grid = (R // tr, N_CHIPS * npc, nt)
        if xl is None:
            a_spec = ((tt, tr), lambda i, j, t: (t, i))
        else:
            a_spec = ((None, tt, tr), lambda i, j, t: (xl, t, i))
        if parts:
            b_spec = ((None, tt, tn), lambda i, j, t: (j // npc, t, j % npc))
        else:
            b_spec = ((tt, tn), lambda i, j, t: (t, j))
        o_spec = ((None, None, tr, tn), lambda i, j, t: (j // npc, l, i, j % npc))
    elif N_CHIPS * R <= WHOLE_K:
        tn = _tile(C, 1024)
        grid = (1, C // tn, nt)
        a_spec = ((tt, N_CHIPS * R), lambda i, j, t: (t, 0))
        b_spec = ((tt, tn), lambda i, j, t: (t, j))
        o_spec = ((N_CHIPS, None, R, tn), lambda i, j, t: (0, l, 0, j))
    else:
        tn = _tile(C, 1024)
        grid = (N_CHIPS, C // tn, nt)
        a_spec = ((tt, R), lambda i, j, t: (t, i))
        b_spec = ((tt, tn), lambda i, j, t: (t, j))
        o_spec = ((None, None, R, tn), lambda i, j, t: (i, l, 0, j))
    return _mm(name, x, dy, dwg.shape, dwg.dtype, grid, a_spec, b_spec, o_spec, TN, alias_out=dwg)


def _sigmoid(x):
    return 0.5 * jnp.tanh(0.5 * x) + 0.5


def f_swiglu(gate, up):
    return (gate * _sigmoid(gate) * up,)


def ffn_gate_up(name, x, wg_gate, wg_up, l):
    if isinstance(wg_gate, dict):
        wg_gate, wg_up, l = wg_gate[l], wg_up[l], 0
    _, _, R, C = wg_gate.shape
    T = x.shape[0]
    tm = _tile(T, TM, 8)
    tn = _tile(C, TN_PREF)
    npc = C // tn

    def body(x_ref, g_ref, u_ref, gate_ref, up_ref, act_ref):
        xv = x_ref[...].astype(BF16)
        gate = jnp.dot(xv, g_ref[...], preferred_element_type=F32)
        up = jnp.dot(xv, u_ref[...], preferred_element_type=F32)
        gate_ref[...] = gate
        up_ref[...] = up
        act_ref[...] = f_swiglu(gate, up)[0].astype(act_ref.dtype)

    w_spec = pl.BlockSpec((None, None, R, tn), lambda i, j: (j // npc, l, 0, j % npc))
    o_spec = pl.BlockSpec((tm, tn), lambda i, j: (i, j))
    N = N_CHIPS * C
    return pl.pallas_call(
        body, name=name, grid=(T // tm, N_CHIPS * npc),
        in_specs=[pl.BlockSpec((tm, R), lambda i, j: (i, 0)), w_spec, w_spec], out_specs=[o_spec, o_spec, o_spec],
        out_shape=[jax.ShapeDtypeStruct((T, N), F32), jax.ShapeDtypeStruct((T, N), F32), jax.ShapeDtypeStruct((T, N), BF16)],
        compiler_params=_cp(2),
    )(x, wg_gate, wg_up)


def ffn_down_dx(name, df, wg_down, l, gate, up):
    if isinstance(wg_down, dict):
        wg_down, l = wg_down[l], 0
    _, _, R, C = wg_down.shape
    T = df.shape[0]
    tm = _tile(T, TM, 8)

    def body(df_ref, w_ref, gate_ref, up_ref, dg_ref, du_ref):
        dact = lax.dot_general(df_ref[...].astype(BF16), w_ref[...], (NT, ((), ())), preferred_element_type=F32)
        _, pull = jax.vjp(lambda g, u: f_swiglu(g, u)[0], gate_ref[...], up_ref[...])
        dg, du = pull(dact)
        dg_ref[...] = dg.astype(dg_ref.dtype)
        du_ref[...] = du.astype(du_ref.dtype)

    t_spec = pl.BlockSpec((tm, R), lambda i, j: (i, j))
    return pl.pallas_call(
        body, name=name, grid=(T // tm, N_CHIPS),
        in_specs=[pl.BlockSpec((tm, C), lambda i, j: (i, 0)), pl.BlockSpec((None, None, R, C), lambda i, j: (j, l, 0, 0)),
                  t_spec, t_spec],
        out_specs=[t_spec, t_spec],
        out_shape=[jax.ShapeDtypeStruct((T, N_CHIPS * R), BF16)] * 2,
        compiler_params=_cp(2),
    )(df, wg_down, gate, up)


def _rms(x, g):
    return x * lax.rsqrt(jnp.mean(x * x, axis=-1, keepdims=True) + EPS) * g


def f_prenorm(h, g):
    return (_rms(h, g),)


def f_prenorm_thru(h, g):
    return _rms(h, g), h


def f_post_pre(h, m, g_post, g_pre):
    h1 = h + _rms(m, g_post)
    return h1, _rms(h1, g_pre)


def f_post(h1, f, g):
    return (h1 + _rms(f, g),)


def f_ple(h2, e, zg, g):
    return (h2 + _rms(e * _sigmoid(zg), g),)


def f_ple_pre(h2, e, zg, g, g_next):
    h3 = h2 + _rms(e * _sigmoid(zg), g)
    return h3, _rms(h3, g_next)


def _gelu(x):
    return 0.5 * x * (1.0 + lax.erf(x * 0.7071067811865476))


def f_gm_in(z, ln_g, ln_b):
    w = z.shape[-1] // 2
    u = _gelu(z[:, :w])
    v = _gelu(z[:, w:])
    mu = jnp.mean(v, axis=-1, keepdims=True)
    vc = v - mu
    vn = vc * lax.rsqrt(jnp.mean(vc * vc, axis=-1, keepdims=True) + EPS) * ln_g + ln_b
    return u, vn


def f_gm_spatial(u, vn, ws, bs):
    t = lax.broadcasted_iota(jnp.int32, ws.shape, 0)
    s = lax.broadcasted_iota(jnp.int32, ws.shape, 1)
    wm = jnp.where(t >= s, ws, 0.0).astype(BF16)
    ys = []
    for n in range(u.shape[0] // GM_CHUNK):
        rows = slice(n * GM_CHUNK, (n + 1) * GM_CHUNK)
        sv = jnp.dot(wm, vn[rows].astype(BF16), preferred_element_type=F32) + bs
        ys.append(u[rows] * sv)
    return (jnp.concatenate(ys, axis=0) if len(ys) > 1 else ys[0],)


def f_adam(w, g, m, v):
    m = ADAM_B1 * m + (1.0 - ADAM_B1) * g
    v = ADAM_B2 * v + (1.0 - ADAM_B2) * jnp.square(g)
    m_hat = m / (1.0 - ADAM_B1 ** ADAM_STEP)
    v_hat = v / (1.0 - ADAM_B2 ** ADAM_STEP)
    delta = -ADAM_LR * (m_hat / (jnp.sqrt(v_hat) + ADAM_EPS) + ADAM_WD * w)
    return delta, m, v


def _make_f_lb(n_layers):
    def f_lb(*logits):
        mx = functools.reduce(jnp.maximum, logits)
        ex = [jnp.exp(r - mx) for r in logits]
        tot = functools.reduce(lambda a, b: a + b, ex)
        sm = [e / tot for e in ex]
        outs = []
        run = jnp.zeros_like(sm[0])
        for j in range(n_layers):
            if j > 0:
                run = run + sm[j]
            lb = run
            outs += [jnp.log(jnp.maximum(lb, LB_FLOOR)), jnp.log(1.0 - lb), 1.0 - lb]
        return tuple(outs)
    return f_lb


def rows_fwd(name, fn, rows, params, out_dtypes, tm=256):
    T = rows[0].shape[0]
    tm = _tile(T, tm, 16)
    ins = [(r, (tm, r.shape[1]), lambda i: (i, 0)) for r in rows]
    ins += [(p, p.shape, lambda i: (0, 0)) for p in params]
    shapes = jax.eval_shape(lambda *a: fn(*a), *[jax.ShapeDtypeStruct((tm, r.shape[1]), F32) for r in rows],
                            *[jax.ShapeDtypeStruct(p.shape, F32) for p in params])
    outs = [((T, s.shape[1]), dt, (tm, s.shape[1]), lambda i: (i, 0), None) for s, dt in zip(shapes, out_dtypes)]
    return bmap_fwd(name, fn, (T // tm,), ins, outs)


def rows_bwd(name, fn, rows, params, cots, row_grad_dtypes, tm=256):
    T = rows[0].shape[0]
    tm = _tile(T, tm, 16)
    ins = [(r, (tm, r.shape[1]), lambda i: (i, 0)) for r in rows]
    ins += [(p, p.shape, lambda i: (0, 0)) for p in params]
    cts = [None if c is None else (c, (tm, c.shape[1]), lambda i: (i, 0)) for c in cots]
    grads = [(i, dt, None) for i, dt in enumerate(row_grad_dtypes) if dt is not None]
    grads += [(len(rows) + j, F32, 0) for j in range(len(params))]
    return bmap_bwd(name, fn, (T // tm,), ins, cts, grads)


def _log_sigmoid(z):
    return jnp.minimum(z, 0.0) - jnp.log(1.0 + jnp.exp(-jnp.abs(z)))


def _hg_gates(zf, ll0, ll1, oml):
    x2 = ll1 + _log_sigmoid(zf)
    mx = jnp.maximum(ll0, x2)
    g = mx + jnp.log(jnp.exp(ll0 - mx) + jnp.exp(x2 - mx))
    return g, oml * _sigmoid(-zf)


def hg_constants(n):
    levels = n.bit_length() - 1
    r = jnp.arange(n, dtype=jnp.int32)
    bounds = [r] + [((r >> (s + 1)) << (s + 1)) + ((1 << s) - 1) for s in range(levels)]
    sel = jnp.concatenate([(r[None, :] <= bd[:, None]) for bd in bounds], axis=0).astype(BF16)
    later = jnp.stack([((r >> s) & 1) for s in range(levels)])
    sign = jnp.broadcast_to((2 * later - 1).astype(F32)[:, :, None], (levels, n, LANES))
    pair = jnp.stack([((r[:, None] >> (s + 1)) == (r[None, :] >> (s + 1))) & (later[s][:, None] == 1) & (later[s][None, :] == 0)
                      for s in range(levels)]).astype(F32)
    return sel, sel.T, sign, pair


def _dot2(m, x):
    hi = x.astype(BF16)
    lo = (x - hi.astype(F32)).astype(BF16)
    p = jnp.dot(m, jnp.concatenate([hi, lo], axis=1), preferred_element_type=F32)
    w = x.shape[1]
    return p[:, :w] + p[:, w:]


@jax.custom_vjp
def _sel_dot(sel, selt, g):
    return _dot2(sel, g)


def _sel_dot_fwd(sel, selt, g):
    return _dot2(sel, g), (sel, selt)


def _sel_dot_bwd(res, d):
    sel, selt = res
    return jnp.zeros_like(sel), jnp.zeros_like(selt), _dot2(selt, d)


_sel_dot.defvjp(_sel_dot_fwd, _sel_dot_bwd)


def _hg_state(st, zf, zi, ll0, ll1, oml, tri):
    g, k = _hg_gates(zf, ll0, ll1, oml)
    b = _dot2(tri, g)
    tot = jnp.sum(g, axis=0, keepdims=True)
    kd = k * jnp.exp(tot - b)
    return st * jnp.exp(tot) + jnp.dot(zi.T.astype(BF16), kd.astype(BF16), preferred_element_type=F32)


def _hg_step(st, zq, zf, zi, zg, ll0, ll1, oml, onorm, sel, selt, sign, pair):
    n = zq.shape[0]
    levels = n.bit_length() - 1
    q = zq * _sigmoid(zq)
    g, k = _hg_gates(zf, ll0, ll1, oml)
    sums = _sel_dot(sel, selt, g)
    b = sums[:n]
    tot = jnp.sum(g, axis=0, keepdims=True)
    o = lax.dot_general((q * jnp.exp(b)).astype(BF16), st.astype(BF16), (NT, ((), ())), preferred_element_type=F32)
    a = jnp.zeros((n, n), F32)
    for s in range(levels):
        e = jnp.exp(sign[s] * (b - sums[(s + 1) * n:(s + 2) * n]))
        al = lax.dot_general((q * e).astype(BF16), (k * e).astype(BF16), (NT, ((), ())), preferred_element_type=F32)
        a = a + pair[s] * al
    o = o + jnp.dot(a.astype(BF16), zi.astype(BF16), preferred_element_type=F32)
    o = o + jnp.sum(q * k, axis=1, keepdims=True) * zi
    kd = k * jnp.exp(tot - b)
    st_new = st * jnp.exp(tot) + jnp.dot(zi.T.astype(BF16), kd.astype(BF16), preferred_element_type=F32)
    og = _rms(o, onorm) * (zg * _sigmoid(zg))
    return og, st_new


def _whole(arr, n_grid):
    zeros = (0,) * arr.ndim
    return pl.BlockSpec(arr.shape, (lambda h, n: zeros) if n_grid == 2 else (lambda i: zeros))


def _hg_dims(proj4, n_seq):
    _, T, D = proj4.shape
    S = T // n_seq
    hp = HG_HEADS_PER if (D // LANES) % HG_HEADS_PER == 0 else 1
    tb = min(HG_BLOCK, S)
    streams = [(b, hl) for b in range(n_seq) for hl in range(hp)]
    return T, D, S, hp, D // (LANES * hp), LANES * hp, tb, S // tb, tb // HG_SUB, streams


def hgrn_fwd(name, proj4, ll0, ll1, oml, onorm, n_seq):
    T, D, S, hp, n_hg, W, tb, nblk, nsub, streams = _hg_dims(proj4, n_seq)
    ns = len(streams)

    def body(p_ref, ll0_ref, ll1_ref, oml_ref, on_ref, sel_ref, selt_ref, later_ref, pair_ref, og_ref, st_ref, st):
        @pl.when(pl.program_id(1) == 0)
        def _():
            st[...] = jnp.zeros(st.shape, F32)

        st_ref[...] = st[...]
        on = on_ref[...]

        def step(j, carry):
            r = pl.ds(pl.multiple_of(j * HG_SUB, HG_SUB), HG_SUB)
            consts = (sel_ref[...], selt_ref[...], later_ref[...], pair_ref[...])
            args = []
            for si, (b, hl) in enumerate(streams):
                ln = slice(hl * LANES, (hl + 1) * LANES)
                args.append((st[si], p_ref[0, b, r, ln], p_ref[1, b, r, ln], p_ref[2, b, r, ln], p_ref[3, b, r, ln],
                             ll0_ref[:, ln], ll1_ref[:, ln], oml_ref[:, ln], on) + consts)
            res = [_hg_step(*a) for a in args]
            for si, (b, hl) in enumerate(streams):
                og_ref[b, r, hl * LANES:(hl + 1) * LANES] = res[si][0].astype(og_ref.dtype)
                st[si] = res[si][1]
            return carry

        lax.fori_loop(0, nsub, step, 0)

    vec = pl.BlockSpec((1, W), lambda h, n: (0, h))
    consts = hg_constants(HG_SUB)
    og, states = pl.pallas_call(
        body, name=name, grid=(n_hg, nblk),
        in_specs=[pl.BlockSpec((4, n_seq, tb, W), lambda h, n: (0, 0, n, h)), vec, vec, vec,
                  pl.BlockSpec((1, LANES), lambda h, n: (0, 0))] + [_whole(c, 2) for c in consts],
        out_specs=[pl.BlockSpec((n_seq, tb, W), lambda h, n: (0, n, h)),
                   pl.BlockSpec((None, None, ns, LANES, LANES), lambda h, n: (h, n, 0, 0, 0))],
        out_shape=[jax.ShapeDtypeStruct((n_seq, S, D), BF16),
                   jax.ShapeDtypeStruct((n_hg, nblk, ns, LANES, LANES), F32)],
        scratch_shapes=[pltpu.VMEM((ns, LANES, LANES), F32)],
        compiler_params=_cp(2),
    )(proj4.reshape(4, n_seq, S, D), ll0, ll1, oml, onorm, *consts)
    return og.reshape(T, D), states


def hgrn_bwd(name, proj4, states, dog, ll0, ll1, oml, onorm, n_seq):
    T, D, S, hp, n_hg, W, tb, nblk, nsub, streams = _hg_dims(proj4, n_seq)
    ns = len(streams)

    def body(p_ref, st_ref, dog_ref, ll0_ref, ll1_ref, oml_ref, on_ref, sel_ref, selt_ref, later_ref, pair_ref,
             dp_ref, dll0_ref, dll1_ref, doml_ref, don_ref, sbuf, dst):
        n_id = pl.program_id(1)

        @pl.when(n_id == 0)
        def _():
            dst[...] = jnp.zeros(dst.shape, F32)
            for ref in (dll0_ref, dll1_ref, doml_ref):
                ref[...] = jnp.zeros(ref.shape, F32)

        @pl.when(jnp.logical_and(n_id == 0, pl.program_id(0) == 0))
        def _():
            don_ref[...] = jnp.zeros(don_ref.shape, F32)

        on = on_ref[...]

        def fwd(j, carry):
            r = pl.ds(pl.multiple_of(j * HG_SUB, HG_SUB), HG_SUB)
            tri = sel_ref[0:HG_SUB, :]
            args = []
            for si, (b, hl) in enumerate(streams):
                ln = slice(hl * LANES, (hl + 1) * LANES)
                args.append((carry[si], p_ref[1, b, r, ln], p_ref[2, b, r, ln],
                             ll0_ref[:, ln], ll1_ref[:, ln], oml_ref[:, ln], tri))
            for si in range(ns):
                sbuf[si, j] = carry[si]
            return tuple(_hg_state(*a) for a in args)

        lax.fori_loop(0, nsub, fwd, tuple(st_ref[si] for si in range(ns)))

        def bwd(jj, carry):
            j = nsub - 1 - jj
            r = pl.ds(pl.multiple_of(j * HG_SUB, HG_SUB), HG_SUB)
            args, cts = [], []
            for si, (b, hl) in enumerate(streams):
                ln = slice(hl * LANES, (hl + 1) * LANES)
                args.append((sbuf[si, j], p_ref[0, b, r, ln], p_ref[1, b, r, ln], p_ref[2, b, r, ln],
                             p_ref[3, b, r, ln], ll0_ref[:, ln], ll1_ref[:, ln], oml_ref[:, ln], on))
                cts.append((dog_ref[b, r, ln].astype(F32), dst[si]))
            consts = (sel_ref[...], selt_ref[...], later_ref[...], pair_ref[...])
            step_fn = lambda *a: _hg_step(*a, *consts)
            ds = [jax.vjp(step_fn, *a)[1](ct) for a, ct in zip(args, cts)]
            d_on = carry
            for si, (b, hl) in enumerate(streams):
                ln = slice(hl * LANES, (hl + 1) * LANES)
                d = ds[si]
                dst[si] = d[0]
                for part in range(4):
                    dp_ref[part, b, r, ln] = d[1 + part].astype(dp_ref.dtype)
                dll0_ref[:, ln] += d[5]
                dll1_ref[:, ln] += d[6]
                doml_ref[:, ln] += d[7]
                d_on = d_on + d[8]
            return d_on

        don_ref[...] += lax.fori_loop(0, nsub, bwd, jnp.zeros((1, LANES), F32))

    last = nblk - 1
    vec = pl.BlockSpec((1, W), lambda h, n: (0, h))
    one = pl.BlockSpec((1, LANES), lambda h, n: (0, 0))
    consts = hg_constants(HG_SUB)
    dproj, d0, d1, d2, d_on = pl.pallas_call(
        body, name=name, grid=(n_hg, nblk),
        in_specs=[pl.BlockSpec((4, n_seq, tb, W), lambda h, n: (0, 0, last - n, h)),
                  pl.BlockSpec((None, None, ns, LANES, LANES), lambda h, n: (h, last - n, 0, 0, 0)),
                  pl.BlockSpec((n_seq, tb, W), lambda h, n: (0, last - n, h)), vec, vec, vec, one]
        + [_whole(c, 2) for c in consts],
        out_specs=[pl.BlockSpec((4, n_seq, tb, W), lambda h, n: (0, 0, last - n, h)), vec, vec, vec, one],
        out_shape=[jax.ShapeDtypeStruct((4, n_seq, S, D), BF16)] + [jax.ShapeDtypeStruct((1, D), F32)] * 3
        + [jax.ShapeDtypeStruct((1, LANES), F32)],
        scratch_shapes=[pltpu.VMEM((ns, nsub, LANES, LANES), F32), pltpu.VMEM((ns, LANES, LANES), F32)],
        compiler_params=_cp(2),
    )(proj4.reshape(4, n_seq, S, D), states, dog.reshape(n_seq, S, D), ll0, ll1, oml, onorm, *consts)
    return dproj.reshape(4, T, D), d0, d1, d2, d_on


def _place():
    x, y, c = lax.axis_index("x"), lax.axis_index("y"), lax.axis_index("c")
    chips = [(1 - x, y), (x, 1 - y), (1 - x, 1 - y)]
    return x, y, c, chips


ANY = pl.BlockSpec(memory_space=pl.ANY)


def _comm_call(name, body, ins, out_shapes, sems, aliases=None):
    return pl.pallas_call(
        body, name=name, in_specs=[ANY] * len(ins), out_specs=[ANY] * len(out_shapes),
        out_shape=out_shapes, scratch_shapes=sems, input_output_aliases=aliases or {},
        compiler_params=pltpu.CompilerParams(has_side_effects=True),
    )(*ins)


HBM_SPEC = pl.BlockSpec(memory_space=pltpu.HBM)
SEM_SPEC = pl.BlockSpec(memory_space=pltpu.SEMAPHORE)
SPLIT_EFFECT = pltpu.SideEffectType.DATAFLOW_SIDE_EFFECTING
N_PEER_CHIPS = 3


def split_start(name, build, arrays, n_peers=N_PEER_CHIPS):
    n = len(arrays)

    def body(*refs):
        send, recv = refs[n], refs[n + 1]
        token = refs[2 * n + 2]
        starts, _ = build(refs[:n], send, recv)
        for cp in starts:
            cp.start()
        token[...] = jnp.zeros_like(token)

    res = pl.pallas_call(
        body, name=name,
        out_shape=(pltpu.SemaphoreType.DMA((n_peers,)), pltpu.SemaphoreType.DMA((n_peers,)),
                   *[pltpu.HBM(a.shape, a.dtype) for a in arrays], jax.ShapeDtypeStruct((8, LANES), F32)),
        in_specs=[HBM_SPEC] * n,
        out_specs=(SEM_SPEC, SEM_SPEC, *[HBM_SPEC] * n, pl.BlockSpec(memory_space=pltpu.VMEM)),
        input_output_aliases={i: 2 + i for i in range(n)},
        compiler_params=pltpu.CompilerParams(has_side_effects=SPLIT_EFFECT),
    )(*[pltpu.with_memory_space_constraint(a, pltpu.HBM) for a in arrays])
    return res[0], res[1], list(res[2:2 + n]), res[2 + n]


def split_wait(name, build, send, recv, arrays, after):
    n = len(arrays)

    def body(*refs):
        starts, arrivals = build(refs[:n], refs[n], refs[n + 1])
        for cp in starts:
            cp.wait_send()
        for cp in arrivals:
            cp.wait_recv()

    return list(pl.pallas_call(
        body, name=name, out_shape=tuple(pltpu.HBM(a.shape, a.dtype) for a in arrays),
        in_specs=[HBM_SPEC] * n + [SEM_SPEC, SEM_SPEC, ANY], out_specs=tuple([HBM_SPEC] * n),
        input_output_aliases={i: i for i in range(n)},
        compiler_params=pltpu.CompilerParams(has_side_effects=SPLIT_EFFECT),
    )(*arrays, send, recv, after))


def _row_half(ref, dim, who):
    rh = ref.shape[dim] // 2
    return pl.ds(who * rh, rh)


def gather_build(refs, send, recv):
    x, y, c, chips = _place()
    q = 2 * x + y
    starts, arrivals = [], []
    for buf in refs:
        rows = _row_half(buf, 2, c)
        for j, (px, py) in enumerate(chips):
            mine, got = buf.at[q, :, rows], buf.at[2 * px + py, :, rows]
            starts.append(pltpu.make_async_remote_copy(src_ref=mine, dst_ref=mine, send_sem=send.at[j], recv_sem=recv.at[j],
                                                       device_id=(px, py, c), device_id_type=MESH_ID))
            arrivals.append(pltpu.make_async_remote_copy(src_ref=got, dst_ref=got, send_sem=send.at[j], recv_sem=recv.at[j],
                                                         device_id=(px, py, c), device_id_type=MESH_ID))
    return starts, arrivals


def gather_forward(name, bufs):
    n = len(bufs)

    def body(*refs):
        dst = refs[n:2 * n]
        send, recv = refs[2 * n:]
        x, y, c, chips = _place()
        sib = (x, y, 1 - c)
        cps = []
        for a in range(n):
            for j, (px, py) in enumerate(chips):
                got = dst[a].at[2 * px + py, :, _row_half(dst[a], 2, c)]
                cps.append(pltpu.make_async_remote_copy(src_ref=got, dst_ref=got, send_sem=send.at[a, j], recv_sem=recv.at[a, j],
                                                        device_id=sib, device_id_type=MESH_ID))
        for cp in cps:
            cp.start()
        for a in range(n):
            for j, (px, py) in enumerate(chips):
                theirs = dst[a].at[2 * px + py, :, _row_half(dst[a], 2, 1 - c)]
                pltpu.make_async_remote_copy(src_ref=theirs, dst_ref=theirs, send_sem=send.at[a, j], recv_sem=recv.at[a, j],
                                             device_id=sib, device_id_type=MESH_ID).wait_recv()
        for cp in cps:
            cp.wait_send()

    outs = [jax.ShapeDtypeStruct(s.shape, s.dtype) for s in bufs]
    sems = [pltpu.SemaphoreType.DMA((n, N_PEER_CHIPS))] * 2
    return _comm_call(name, body, bufs, outs, sems, aliases={a: a for a in range(n)})


def scatter_build(refs, send, recv):
    n = len(refs) // 2
    x, y, c, _ = _place()
    starts, arrivals = [], []
    for a in range(n):
        src, dst = refs[a], refs[n + a]
        for k in range(1, N_DEV):
            px, py, pc = x ^ (k >> 2), y ^ ((k >> 1) & 1), c ^ (k & 1)
            theirs = src.at[2 * px + py, :, _row_half(src, 2, pc)]
            starts.append(pltpu.make_async_remote_copy(src_ref=theirs, dst_ref=dst.at[k - 1], send_sem=send.at[k - 1],
                                                       recv_sem=recv.at[k - 1], device_id=(px, py, pc), device_id_type=MESH_ID))
            arrivals.append(pltpu.make_async_remote_copy(src_ref=dst.at[k - 1], dst_ref=dst.at[k - 1], send_sem=send.at[k - 1],
                                                         recv_sem=recv.at[k - 1], device_id=(px, py, pc), device_id_type=MESH_ID))
    return starts, arrivals


def share_build(refs, send, recv):
    dst = refs[0]
    x, y, c, _ = _place()
    mine = dst.at[4 * x + 2 * y + c]
    starts, arrivals = [], []
    for k in range(1, N_DEV):
        px, py, pc = x ^ (k >> 2), y ^ ((k >> 1) & 1), c ^ (k & 1)
        got = dst.at[4 * px + 2 * py + pc]
        starts.append(pltpu.make_async_remote_copy(src_ref=mine, dst_ref=mine, send_sem=send.at[k - 1], recv_sem=recv.at[k - 1],
                                                   device_id=(px, py, pc), device_id_type=MESH_ID))
        arrivals.append(pltpu.make_async_remote_copy(src_ref=got, dst_ref=got, send_sem=send.at[k - 1], recv_sem=recv.at[k - 1],
                                                     device_id=(px, py, pc), device_id_type=MESH_ID))
    return starts, arrivals


def join_row_halves(name, bufs):
    n = len(bufs)

    def body(*refs):
        dst = refs[n:2 * n]
        send, recv = refs[2 * n:]
        x, y, c, _ = _place()
        cps = []
        for a in range(n):
            mine = dst[a].at[:, _row_half(dst[a], 1, c)]
            cps.append(pltpu.make_async_remote_copy(src_ref=mine, dst_ref=mine, send_sem=send.at[a], recv_sem=recv.at[a],
                                                    device_id=(x, y, 1 - c), device_id_type=MESH_ID))
        for cp in cps:
            cp.start()
        for a in range(n):
            theirs = dst[a].at[:, _row_half(dst[a], 1, 1 - c)]
            pltpu.make_async_remote_copy(src_ref=theirs, dst_ref=theirs, send_sem=send.at[a], recv_sem=recv.at[a],
                                         device_id=(x, y, 1 - c), device_id_type=MESH_ID).wait_recv()
        for cp in cps:
            cp.wait_send()

    outs = [jax.ShapeDtypeStruct(b.shape, b.dtype) for b in bufs]
    sems = [pltpu.SemaphoreType.DMA((n,))] * 2
    return _comm_call(name, body, bufs, outs, sems, aliases={a: a for a in range(n)})


def share_with_all(name, packed, me):
    slots = lax.dynamic_update_slice(jnp.zeros((N_DEV,) + packed.shape, packed.dtype), packed[None], (me, 0, 0))

    def body(_, dst, send, recv):
        x, y, c, _ = _place()
        me = 4 * x + 2 * y + c
        cps = []
        for k in range(1, N_DEV):
            px, py, pc = x ^ (k >> 2), y ^ ((k >> 1) & 1), c ^ (k & 1)
            cps.append(pltpu.make_async_remote_copy(src_ref=dst.at[me], dst_ref=dst.at[me], send_sem=send.at[k - 1],
                                                    recv_sem=recv.at[k - 1], device_id=(px, py, pc), device_id_type=MESH_ID))
        for cp in cps:
            cp.start()
        for k in range(1, N_DEV):
            px, py, pc = x ^ (k >> 2), y ^ ((k >> 1) & 1), c ^ (k & 1)
            got = dst.at[4 * px + 2 * py + pc]
            pltpu.make_async_remote_copy(src_ref=got, dst_ref=got, send_sem=send.at[k - 1], recv_sem=recv.at[k - 1],
                                         device_id=(px, py, pc), device_id_type=MESH_ID).wait_recv()
        for cp in cps:
            cp.wait_send()

    outs = [jax.ShapeDtypeStruct(slots.shape, slots.dtype)]
    sems = [pltpu.SemaphoreType.DMA((N_DEV - 1,))] * 2
    return _comm_call(name, body, [slots], outs, sems, aliases={0: 0})[0]


def _w_tiles(R, C):
    return _tile(R, max(16, (1 << 20) // (4 * C) // 16 * 16), 16)


def cast_bf16(w, l, q_arr):
    _, R, C = w.shape
    tr = _w_tiles(R, C)
    ins = [(w, (None, tr, C), lambda r, q: (l, r, 0))]
    outs = [((N_CHIPS, 1, R, C), BF16, (None, None, tr, C), lambda r, q: (q[0], 0, r, 0), None)]
    return bmap_fwd("cast_bf16", lambda a: (a,), (R // tr,), ins, outs, scalars=(q_arr,))[0]


def sum_partials(own, landed, into, l, q_arr, c_arr):
    n_land, _, rh, C = landed.shape
    tr = _w_tiles(rh, C)
    nb = rh // tr
    blk = (None, None, tr, C)
    ins = [(own, blk, lambda r, q, c: (q[0], 0, c[0] * nb + r, 0))]
    ins += [(landed, blk, (lambda r, q, c, kk=kk: (kk, 0, r, 0))) for kk in range(n_land)]
    outs = [(into.shape, F32, (None, tr, C), lambda r, q, c: (l, c[0] * nb + r, 0), None)]
    return bmap_fwd("sum_partials", lambda *t: (functools.reduce(lambda u, v: u + v, t),), (nb,), ins, outs,
                    scalars=(q_arr, c_arr), into=into)[0]


def sum_devices(slots):
    nd, NR, C = slots.shape
    tr = _tile(NR, 512, 8)
    ins = [(slots, (None, tr, C), (lambda r, dd=dd: (dd, r, 0))) for dd in range(nd)]
    outs = [((NR, C), F32, (tr, C), lambda r: (r, 0), None)]
    return bmap_fwd("sum_devices", lambda *a: (functools.reduce(lambda u, v: u + v, a),), (NR // tr,), ins, outs)[0]


def adamw(name, w, g, m, v, with_grad=False):
    if w.ndim == 2:
        R, C = w.shape
        tr = _w_tiles(R, C)
        spec = ((tr, C), lambda r: (r, 0))
        grid = (R // tr,)
    else:
        L, R, C = w.shape
        tr = _w_tiles(R, C)
        spec = ((None, tr, C), lambda l, r: (l, r, 0))
        grid = (L, R // tr)
    ins = [(a,) + spec for a in (w, g, m, v)]
    outs = [(w.shape, F32) + spec + (None,)] * (4 if with_grad else 3)
    fn = (lambda a, b, c, d: f_adam(a, b, c, d) + (b,)) if with_grad else f_adam
    return bmap_fwd(name, fn, grid, ins, outs)


def loss_and_grad(h, target):
    T, D = h.shape
    tm = _tile(T, 256, 8)

    def fn(hv, tv):
        d = hv - tv
        return jnp.sum(d * d, keepdims=True).reshape(1, 1) * (0.5 / D), d * (1.0 / D)

    ins = [(h, (tm, D), lambda i: (i, 0)), (target, (tm, D), lambda i: (i, 0))]
    outs = [((1, 1), F32, (1, 1), lambda i: (0, 0), 0), ((T, D), F32, (tm, D), lambda i: (i, 0), None)]
    return bmap_fwd("loss_and_grad", fn, (T // tm,), ins, outs)


BIG = ("hg_w_in", "hg_w_out", "gm_w_in", "gm_w_out", "ffn_w_gate", "ffn_w_up", "ffn_w_down", "ple_w_proj", "ple_w_gate")
KIND = {"hg_w_in": "col", "hg_w_out": "row", "gm_w_in": "col", "gm_w_out": "row", "ffn_w_gate": "col",
        "ffn_w_up": "col", "ffn_w_down": "row", "ple_w_proj": "col", "ple_w_gate": "row"}
SMALL = ("hg_lb_logits", "hg_out_norm", "gm_ln_g", "gm_ln_b", "gm_w_s", "gm_b_s", "norm_mix_pre", "norm_mix_post",
         "norm_ffn_pre", "norm_ffn_post", "ple_norm")
WEIGHTS = ("hg_w_in", "hg_lb_logits", "hg_out_norm", "hg_w_out", "gm_w_in", "gm_ln_g", "gm_ln_b", "gm_w_s", "gm_b_s",
           "gm_w_out", "norm_mix_pre", "norm_mix_post", "norm_ffn_pre", "norm_ffn_post", "ffn_w_gate", "ffn_w_up",
           "ffn_w_down", "ple_w_proj", "ple_w_gate", "ple_norm")


def _pack(arrs):
    rows = []
    for a in arrs:
        flat = a.reshape(-1)
        pad = (-flat.shape[0]) % (8 * LANES)
        rows.append(jnp.pad(flat, (0, pad)).reshape(-1, LANES))
    n_rows = sum(r.shape[0] for r in rows)
    rows.append(jnp.zeros(((-n_rows) % PACK_ROWS, LANES), F32))
    return jnp.concatenate(rows, axis=0)


def _unpack(packed, shapes):
    out, r = [], 0
    for s in shapes:
        size = 1
        for d in s:
            size *= d
        nr = -(-size // (8 * LANES)) * 8
        out.append(packed[r:r + nr].reshape(-1)[:size].reshape(s))
        r += nr
    return out


def _step(x, p, W, M, V, loss_target):
    n_seq, S, D = x.shape
    T = n_seq * S
    depth = p.shape[0]
    n_hg = W["hg_w_in"].shape[0]
    x2 = x.reshape(T, D)
    p3 = p.reshape(depth, T, p.shape[-1])
    tgt = loss_target.reshape(T, D)
    xi, yi, ci = lax.axis_index("x"), lax.axis_index("y"), lax.axis_index("c")
    q_me = 2 * xi + yi
    c_arr = jnp.reshape(ci, (1,)).astype(jnp.int32)
    q_arr = jnp.reshape(q_me, (1,)).astype(jnp.int32)

    groups = {}
    for i in range(depth):
        mix = ("hg_w_in", "hg_w_out") if i % 2 == 0 else ("gm_w_in", "gm_w_out")
        groups[i, "mix"] = [(k, i // 2) for k in mix]
        groups[i, "rest"] = [(k, i) for k in ("ffn_w_gate", "ffn_w_up", "ffn_w_down", "ple_w_proj", "ple_w_gate")]
    G = {k: {} for k in BIG}
    DW = {k: {l: lax.empty((N_CHIPS, 1) + W[k].shape[1:], BF16) for l in range(W[k].shape[0])} for k in BIG}
    in_flight = {}

    casts = {}

    def cast_group(i, part, dep):
        qa = q_arr if dep is None else lax.optimization_barrier((q_arr, dep))[0]
        casts[i, part] = [cast_bf16(W[k], l, qa) for k, l in groups[i, part]]

    def start_gather(i, part, dep):
        bufs = casts.pop((i, part))
        if dep is not None:
            bufs = list(lax.optimization_barrier((tuple(bufs), dep))[0])
        send, recv, arrs, tok = split_start("gather_start_%d_%s" % (i, part), gather_build, bufs)
        in_flight[i, part] = (send, recv, arrs)
        return tok

    def finish_gather(i, part, after):
        send, recv, arrs = in_flight.pop((i, part))
        arrs = split_wait("gather_wait_%d_%s" % (i, part), gather_build, send, recv, arrs, after)
        for (k, l), buf in zip(groups[i, part], gather_forward("gather_forward_%d_%s" % (i, part), arrs)):
            G[k][l] = buf
        return buf

    def after_token(row_arr, *toks):
        return functools.reduce(lambda u, t: u + t[0:1, 0:1], toks, row_arr)

    cast_group(0, "mix", None)
    tok_mix = start_gather(0, "mix", None)
    cast_group(0, "rest", tok_mix)
    tok_rest = start_gather(0, "rest", None)
    for i in range(1, depth):
        cast_group(i, "mix", tok_rest)
        cast_group(i, "rest", tok_rest)
    finish_gather(0, "mix", casts[depth - 1, "rest"][-1])
    me = 4 * xi + 2 * yi + ci
    ln_full = share_with_all("share_ln", _pack([W["gm_ln_g"], W["gm_ln_b"]]), me)
    n_gm, dq = W["gm_ln_g"].shape
    ln_parts = [_unpack(ln_full[4 * qx + 2 * qy + 0], [(n_gm, dq), (n_gm, dq)]) for qx in range(2) for qy in range(2)]
    ln_g = jnp.concatenate([lp[0] for lp in ln_parts], axis=1)
    ln_b = jnp.concatenate([lp[1] for lp in ln_parts], axis=1)

    row = lambda a, i: a[i][None, :]
    f_lb = _make_f_lb(n_hg)
    lb_rows = [row(W["hg_lb_logits"], j) for j in range(n_hg)]
    one = (1, D)
    lb_ins = [(r, one, lambda i: (0, 0)) for r in lb_rows]
    lb_out = bmap_fwd("hg_lower_bounds", f_lb, (1,), lb_ins, [(one, F32, one, lambda i: (0, 0), None)] * (3 * n_hg))

    saved = []
    h = x2
    a = rows_fwd("prenorm", f_prenorm, [h], [row(W["norm_mix_pre"], 0)], [BF16])[0]
    for i in range(depth):
        j = i // 2
        sv = {"h": h, "a": a}
        if i > 0:
            finish_gather(i, "mix", h)
        if i % 2 == 0:
            proj4 = mm_fwd("hg_in", a, G["hg_w_in"], j, "col", parts=True)
            lbp = lb_out[3 * j:3 * j + 3]
            onorm = row(W["hg_out_norm"], j)
            og, states = hgrn_fwd("hgrn_fwd", proj4, *lbp, onorm, n_seq)
            m = mm_fwd("hg_out", og, G["hg_w_out"], j, "row")
            sv.update(proj4=proj4, states=states, og=og, lbp=lbp, onorm=onorm)
        else:
            z = mm_fwd("gm_in", a, G["gm_w_in"], j, "col")
            lg, lb_ = row(ln_g, j), row(ln_b, j)
            u, vn = rows_fwd("gm_gelu_ln", f_gm_in, [z], [lg, lb_], [F32, BF16], tm=128)
            ws = W["gm_w_s"][j]
            bs = W["gm_b_s"][j][:, :, None]
            gb = min(GM_BLOCK, S)
            sp_grid = (D // LANES, T // gb)
            sp_ins = [(u, (gb, LANES), lambda g, n: (n, g)), (vn, (gb, LANES), lambda g, n: (n, g)),
                      (ws, (None, GM_CHUNK, GM_CHUNK), lambda g, n: (g, 0, 0)),
                      (bs, (None, GM_CHUNK, 1), lambda g, n: (g, 0, 0))]
            y = bmap_fwd("gm_spatial", f_gm_spatial, sp_grid, sp_ins,
                         [((T, D), BF16, (gb, LANES), lambda g, n: (n, g), None)])[0]
            m = mm_fwd("gm_out", y, G["gm_w_out"], j, "row")
            sv.update(z=z, lg=lg, lb_=lb_, sp_ins=sp_ins, sp_grid=sp_grid, y=y)
        g_post, g_fpre = row(W["norm_mix_post"], i), row(W["norm_ffn_pre"], i)
        arrived = finish_gather(i, "rest", m)
        if i + 1 < depth:
            tok_mix = start_gather(i + 1, "mix", arrived)
            g_post = after_token(g_post, tok_mix, start_gather(i + 1, "rest", tok_mix))
        h1, fin = rows_fwd("mix_post_ffn_pre", f_post_pre, [h, m], [g_post, g_fpre], [F32, BF16])
        gate, up, act = ffn_gate_up("ffn_gate_up", fin, G["ffn_w_gate"], G["ffn_w_up"], i)
        f = mm_fwd("ffn_down", act, G["ffn_w_down"], i, "row")
        g_fpost = row(W["norm_ffn_post"], i)
        h2 = rows_fwd("ffn_post", f_post, [h1, f], [g_fpost], [F32])[0]
        e = mm_fwd("ple_proj", p3, G["ple_w_proj"], i, "col", xl=i)
        zg = mm_fwd("ple_gate", h2, G["ple_w_gate"], i, "row")
        g_ple = row(W["ple_norm"], i)
        sv.update(m=m, h1=h1, fin=fin, gate=gate, up=up, act=act, f=f, h2=h2, e=e, zg=zg,
                  g_post=g_post, g_fpre=g_fpre, g_fpost=g_fpost, g_ple=g_ple)
        if i + 1 < depth:
            g_next = row(W["norm_mix_pre"], i + 1)
            h, a = rows_fwd("ple_next_pre", f_ple_pre, [h2, e, zg], [g_ple, g_next], [F32, BF16])
            sv["g_next"] = g_next
        else:
            h = rows_fwd("ple_last", f_ple, [h2, e, zg], [g_ple], [F32])[0]
        saved.append(sv)

    loss_part, dh = loss_and_grad(h, tgt)
    loss = lax.psum(loss_part[0, 0], ("x", "y", "c"))

    sg = {k: [None] * W[k].shape[0] for k in ("norm_mix_pre", "norm_mix_post", "norm_ffn_pre", "norm_ffn_post", "ple_norm",
                                              "hg_out_norm", "gm_ln_g", "gm_ln_b", "gm_w_s", "gm_b_s")}
    d_lbp = [None] * (3 * n_hg)
    da_next = None
    GRAD = {k: lax.empty(W[k].shape, F32) for k in BIG}
    scattering = {}

    def start_scatter(i, part):
        dws = [DW[k][l] for k, l in groups[i, part]]
        lands = [lax.empty((N_DEV - 1, 1, g.shape[2] // 2, g.shape[3]), BF16) for g in dws]
        send, recv, arrs, tok = split_start("scatter_start_%d_%s" % (i, part), scatter_build, dws + lands, n_peers=N_DEV - 1)
        scattering[i, part] = (send, recv, arrs)
        return tok

    def finish_scatter(i, part, after):
        send, recv, arrs = scattering.pop((i, part))
        arrs = split_wait("scatter_wait_%d_%s" % (i, part), scatter_build, send, recv, arrs, after)
        n = len(groups[i, part])
        for (k, l), own, ld in zip(groups[i, part], arrs[:n], arrs[n:]):
            GRAD[k] = sum_partials(own, ld, GRAD[k], l, q_arr, c_arr)

    tok = None
    for i in reversed(range(depth)):
        j = i // 2
        sv = saved[i]
        if i + 1 < depth:
            g_ple_after = sv["g_ple"] + tok[0:1, 0:1]
            dh2, de, dzg, d_gple, d_gnext = rows_bwd("ple_next_pre_bwd", f_ple_pre, [sv["h2"], sv["e"], sv["zg"]],
                                                     [g_ple_after, sv["g_next"]], [dh, da_next], [F32, BF16, BF16])
            sg["norm_mix_pre"][i + 1] = d_gnext
        else:
            dh2, de, dzg, d_gple = rows_bwd("ple_last_bwd", f_ple, [sv["h2"], sv["e"], sv["zg"]], [sv["g_ple"]], [dh],
                                            [F32, BF16, BF16])
        sg["ple_norm"][i] = d_gple
        DW["ple_w_proj"] = mm_bwd_w("ple_proj_dw", p3, de, DW["ple_w_proj"], i, "col", xl=i)
        DW["ple_w_gate"] = mm_bwd_w("ple_gate_dw", sv["h2"], dzg, DW["ple_w_gate"], i, "row")
        dh2 = mm_bwd_x("ple_gate_dx", dzg, G["ple_w_gate"], i, "row", addend=dh2)
        dh1, df, d_gfpost = rows_bwd("ffn_post_bwd", f_post, [sv["h1"], sv["f"]], [sv["g_fpost"]], [dh2], [F32, BF16])
        sg["norm_ffn_post"][i] = d_gfpost
        dgate, dup = ffn_down_dx("ffn_down_dx", df, G["ffn_w_down"], i, sv["gate"], sv["up"])
        DW["ffn_w_down"] = mm_bwd_w("ffn_down_dw", sv["act"], df, DW["ffn_w_down"], i, "row")
        dfin = mm_bwd_x("ffn_gate_dx", dgate, G["ffn_w_gate"], i, "col")
        dfin = mm_bwd_x("ffn_up_dx", dup, G["ffn_w_up"], i, "col", addend=dfin)
        DW["ffn_w_gate"] = mm_bwd_w("ffn_gate_dw", sv["fin"], dgate, DW["ffn_w_gate"], i, "col")
        DW["ffn_w_up"] = mm_bwd_w("ffn_up_dw", sv["fin"], dup, DW["ffn_w_up"], i, "col")
        g_post_after = after_token(sv["g_post"], start_scatter(i, "rest"))
        dh, dm, d_gpost, d_gfpre = rows_bwd("mix_post_ffn_pre_bwd", f_post_pre, [sv["h"], sv["m"]],
                                            [g_post_after, sv["g_fpre"]], [dh1, dfin], [F32, BF16])
        sg["norm_mix_post"][i], sg["norm_ffn_pre"][i] = d_gpost, d_gfpre
        if i % 2 == 0:
            dog = mm_bwd_x("hg_out_dx", dm, G["hg_w_out"], j, "row")
            DW["hg_w_out"] = mm_bwd_w("hg_out_dw", sv["og"], dm, DW["hg_w_out"], j, "row")
            dproj4, d0, d1, d2, d_on = hgrn_bwd("hgrn_bwd", sv["proj4"], sv["states"], dog, *sv["lbp"], sv["onorm"], n_seq)
            d_lbp[3 * j:3 * j + 3] = [d0, d1, d2]
            sg["hg_out_norm"][j] = d_on
            da_next = mm_bwd_x("hg_in_dx", dproj4, G["hg_w_in"], j, "col", parts=True)
            DW["hg_w_in"] = mm_bwd_w("hg_in_dw", sv["a"], dproj4, DW["hg_w_in"], j, "col", parts=True)
        else:
            dy = mm_bwd_x("gm_out_dx", dm, G["gm_w_out"], j, "row")
            DW["gm_w_out"] = mm_bwd_w("gm_out_dw", sv["y"], dm, DW["gm_w_out"], j, "row")
            gb = sv["sp_ins"][0][1][0]
            du, dvn, dws, dbs = bmap_bwd("gm_spatial_bwd", f_gm_spatial, sv["sp_grid"], sv["sp_ins"],
                                         [(dy, (gb, LANES), lambda g, n: (n, g))],
                                         [(0, F32, None), (1, F32, None), (2, F32, 1), (3, F32, 1)])
            sg["gm_w_s"][j], sg["gm_b_s"][j] = dws, dbs[:, :, 0]
            dz, d_lg, d_lb = rows_bwd("gm_gelu_ln_bwd", f_gm_in, [sv["z"]], [sv["lg"], sv["lb_"]], [du, dvn], [BF16], tm=128)
            sg["gm_ln_g"][j], sg["gm_ln_b"][j] = d_lg, d_lb
            da_next = mm_bwd_x("gm_in_dx", dz, G["gm_w_in"], j, "col")
            DW["gm_w_in"] = mm_bwd_w("gm_in_dw", sv["a"], dz, DW["gm_w_in"], j, "col")
        tok = start_scatter(i, "mix")
        if i + 1 < depth:
            finish_scatter(i + 1, "rest", da_next)
            finish_scatter(i + 1, "mix", da_next)
    finish_scatter(0, "rest", tok)
    g0 = after_token(row(W["norm_mix_pre"], 0), tok)
    grad_x, d_g0 = rows_bwd("prenorm_bwd", f_prenorm_thru, [saved[0]["h"]], [g0], [da_next, dh], [F32])
    sg["norm_mix_pre"][0] = d_g0
    d_logits = bmap_bwd("hg_lower_bounds_bwd", f_lb, (1,), lb_ins, [(d, one, lambda i: (0, 0)) for d in d_lbp],
                        [(jj, F32, None) for jj in range(n_hg)])

    small_g = {k: jnp.stack([v.reshape(W[k].shape[1:] if k not in ("gm_ln_g", "gm_ln_b") else (D,)) for v in sg[k]])
               for k in sg}
    small_g["hg_lb_logits"] = jnp.concatenate(d_logits, axis=0)
    small_shapes = [small_g[k].shape for k in SMALL]
    packed_g = _pack([small_g[k] for k in SMALL])
    slots = lax.dynamic_update_slice(jnp.zeros((N_DEV,) + packed_g.shape, F32), packed_g[None], (me, 0, 0))
    s_send, s_recv, slots, tok_s = split_start("share_small_start", share_build, [slots], n_peers=N_DEV - 1)

    out_g, out_d, out_m, out_v = {}, {}, {}, {}
    late = [k for k, _ in groups[0, "mix"]]
    early = [k for k in BIG if k not in late]
    ready = lax.optimization_barrier(tuple(GRAD[k] for k in early) + (tok_s,))[:-1]
    for k, g in zip(early, join_row_halves("join_early", list(ready))):
        out_d[k], out_m[k], out_v[k], out_g[k] = adamw("adamw_" + k, W[k], g, M[k], V[k], with_grad=True)
    slots = split_wait("share_small_wait", share_build, s_send, s_recv, slots, out_v[early[-1]])[0]
    red = sum_devices(slots)
    small_red = dict(zip(SMALL, _unpack(red, small_shapes)))
    for k in ("gm_ln_g", "gm_ln_b"):
        small_red[k] = lax.dynamic_slice_in_dim(small_red[k], q_me * dq, dq, axis=1)
    pk = lambda d: _pack([d[k] for k in SMALL])
    s_delta, s_m, s_v = adamw("adamw_small", pk(W), pk(small_red), pk(M), pk(V))
    shard_shapes = [W[k].shape for k in SMALL]
    out_g.update(small_red)
    for dct, packed in ((out_d, s_delta), (out_m, s_m), (out_v, s_v)):
        dct.update(zip(SMALL, _unpack(packed, shard_shapes)))

    finish_scatter(0, "mix", s_v)
    for k, g in zip(late, join_row_halves("join_late", [GRAD[k] for k in late])):
        out_d[k], out_m[k], out_v[k], out_g[k] = adamw("adamw_" + k, W[k], g, M[k], V[k], with_grad=True)

    outs = [loss, grad_x.reshape(x.shape)]
    for dct in (out_g, out_d, out_m, out_v):
        outs += [dct[k] for k in WEIGHTS]
    return tuple(outs)


def kernel(x, p, hg_w_in, hg_lb_logits, hg_out_norm, hg_w_out, gm_w_in, gm_ln_g, gm_ln_b, gm_w_s, gm_b_s, gm_w_out, norm_mix_pre, norm_mix_post, norm_ffn_pre, norm_ffn_post, ffn_w_gate, ffn_w_up, ffn_w_down, ple_w_proj, ple_w_gate, ple_norm, loss_target, m_hg_w_in, m_hg_lb_logits, m_hg_out_norm, m_hg_w_out, m_gm_w_in, m_gm_ln_g, m_gm_ln_b, m_gm_w_s, m_gm_b_s, m_gm_w_out, m_norm_mix_pre, m_norm_mix_post, m_norm_ffn_pre, m_norm_ffn_post, m_ffn_w_gate, m_ffn_w_up, m_ffn_w_down, m_ple_w_proj, m_ple_w_gate, m_ple_norm, v_hg_w_in, v_hg_lb_logits, v_hg_out_norm, v_hg_w_out, v_gm_w_in, v_gm_ln_g, v_gm_ln_b, v_gm_w_s, v_gm_b_s, v_gm_w_out, v_norm_mix_pre, v_norm_mix_post, v_norm_ffn_pre, v_norm_ffn_post, v_ffn_w_gate, v_ffn_w_up, v_ffn_w_down, v_ple_w_proj, v_ple_w_gate, v_ple_norm):
    W = dict(zip(WEIGHTS, (hg_w_in, hg_lb_logits, hg_out_norm, hg_w_out, gm_w_in, gm_ln_g, gm_ln_b, gm_w_s, gm_b_s, gm_w_out,
                           norm_mix_pre, norm_mix_post, norm_ffn_pre, norm_ffn_post, ffn_w_gate, ffn_w_up, ffn_w_down,
                           ple_w_proj, ple_w_gate, ple_norm)))
    M = dict(zip(WEIGHTS, (m_hg_w_in, m_hg_lb_logits, m_hg_out_norm, m_hg_w_out, m_gm_w_in, m_gm_ln_g, m_gm_ln_b, m_gm_w_s,
                           m_gm_b_s, m_gm_w_out, m_norm_mix_pre, m_norm_mix_post, m_norm_ffn_pre, m_norm_ffn_post,
                           m_ffn_w_gate, m_ffn_w_up, m_ffn_w_down, m_ple_w_proj, m_ple_w_gate, m_ple_norm)))
    V = dict(zip(WEIGHTS, (v_hg_w_in, v_hg_lb_logits, v_hg_out_norm, v_hg_w_out, v_gm_w_in, v_gm_ln_g, v_gm_ln_b, v_gm_w_s,
                           v_gm_b_s, v_gm_w_out, v_norm_mix_pre, v_norm_mix_post, v_norm_ffn_pre, v_norm_ffn_post,
                           v_ffn_w_gate, v_ffn_w_up, v_ffn_w_down, v_ple_w_proj, v_ple_w_gate, v_ple_norm)))
    return _step(x, p, W, M, V, loss_target)
```

```python
import functools

import jax
import jax.numpy as jnp
from jax import lax
from jax.experimental import pallas as pl
from jax.experimental.pallas import tpu as pltpu

F32 = jnp.float32
BF16 = jnp.bfloat16
MESH_ID = pl.DeviceIdType.MESH

LANES = 128
N_CHIPS = 4
N_DEV = 8
VMEM_LIMIT = 56 * 1024 * 1024
HG_SUB = 64
HG_BLOCK = 256
HG_HEADS_PER = 4
GM_CHUNK = 128
GM_BLOCK = 512
PACK_ROWS = 512
LB_FLOOR = 1e-30
EPS = 1e-6
ADAM_LR, ADAM_B1, ADAM_B2, ADAM_EPS, ADAM_WD, ADAM_STEP = 0.001, 0.9, 0.999, 1e-08, 0.01, 10


def _tile(n, pref, mult=LANES):
    if n <= pref:
        return n
    t = (pref // mult) * mult
    while t >= mult:
        if n % t == 0:
            return t
        t -= mult
    return n


def _cp(n_axes):
    return pltpu.CompilerParams(dimension_semantics=("arbitrary",) * n_axes, vmem_limit_bytes=VMEM_LIMIT)


def _dense(block):
    return tuple(b for b in block if b is not None)


def _bmap(name, grid, ins, outs, compute, scalars=(), into=None):
    n_s, n_in = len(scalars), len(ins)
    n_extra = 0 if into is None else 1

    def body(*refs):
        in_refs = refs[n_s:n_s + n_in]
        out_refs = refs[n_s + n_in + n_extra:]
        vals = [r[...] for r in in_refs]
        res = compute(*vals)
        for r, o, spec in zip(out_refs, res, outs):
            keep = spec[4]
            if keep is None:
                r[...] = o.astype(r.dtype)
            else:
                first = functools.reduce(jnp.logical_and, [pl.program_id(a) == 0 for a in range(keep, len(grid))])

                @pl.when(first)
                def _():
                    r[...] = jnp.zeros(r.shape, r.dtype)

                r[...] += o.astype(r.dtype)

    grid_spec = pltpu.PrefetchScalarGridSpec(
        num_scalar_prefetch=n_s, grid=grid,
        in_specs=[pl.BlockSpec(b, m) for _, b, m in ins] + [pl.BlockSpec(memory_space=pl.ANY)] * n_extra,
        out_specs=[pl.BlockSpec(o[2], o[3]) for o in outs])
    return pl.pallas_call(
        body, name=name, grid_spec=grid_spec,
        out_shape=[jax.ShapeDtypeStruct(o[0], o[1]) for o in outs],
        input_output_aliases={n_s + n_in: 0} if n_extra else {},
        compiler_params=_cp(len(grid)),
    )(*scalars, *[a for a, _, _ in ins], *([into] if n_extra else []))


def bmap_fwd(name, fn, grid, ins, outs, scalars=(), into=None):
    return _bmap(name, grid, ins, outs, lambda *v: fn(*[x.astype(F32) for x in v]), scalars, into)


def bmap_bwd(name, fn, grid, ins, cots, grads, scalars=()):
    n_in = len(ins)
    diff = [g[0] for g in grads]
    cot_ins = [c for c in cots if c is not None]

    def compute(*vals):
        xs = [v.astype(F32) for v in vals[:n_in]]
        cvals = list(vals[n_in:])

        def f(*d):
            full = list(xs)
            for i, dv in zip(diff, d):
                full[i] = dv
            return tuple(fn(*full))

        res, pull = jax.vjp(f, *[xs[i] for i in diff])
        cts = []
        for r, c in zip(res, cots):
            cts.append(jnp.zeros_like(r) if c is None else cvals.pop(0).astype(F32))
        return pull(tuple(cts))

    outs = [(ins[i][0].shape, dt, ins[i][1], ins[i][2], keep) for i, dt, keep in grads]
    return _bmap(name, grid, list(ins) + cot_ins, outs, compute, scalars)


def _mm(name, a, b, out_shape, out_dtype, grid, a_spec, b_spec, o_spec, dims, addend=None, alias_out=None):
    nk = grid[2]
    o_dense = _dense(o_spec[0])
    o_dense = (o_dense[0] * o_dense[1], o_dense[2]) if len(o_dense) == 3 else o_dense
    has_add = addend is not None
    has_alias = alias_out is not None

    def body(*refs):
        a_ref, b_ref = refs[0], refs[1]
        pos = 2
        c_ref = None
        if has_add:
            c_ref = refs[pos]
            pos += 1
        if has_alias:
            pos += 1
        o_ref = refs[pos]
        acc_ref = refs[pos + 1] if nk > 1 else None
        bv = b_ref[...]
        if bv.ndim == 3:
            bv = bv.reshape(bv.shape[0] * bv.shape[1], bv.shape[2])
        p = lax.dot_general(a_ref[...].astype(BF16), bv.astype(BF16), (dims, ((), ())), preferred_element_type=F32)

        def finish(total):
            if has_add:
                total = total + c_ref[...].astype(F32)
            o_ref[...] = total.reshape(o_ref.shape).astype(o_ref.dtype)

        if nk == 1:
            finish(p)
        else:
            k = pl.program_id(2)

            @pl.when(k == 0)
            def _():
                acc_ref[...] = p

            @pl.when(jnp.logical_and(k > 0, k < nk - 1))
            def _():
                acc_ref[...] += p

            @pl.when(k == nk - 1)
            def _():
                finish(acc_ref[...] + p)

    in_specs = [pl.BlockSpec(*a_spec), pl.BlockSpec(*b_spec)]
    operands = [a, b]
    if has_add:
        in_specs.append(pl.BlockSpec(o_spec[0], o_spec[1]))
        operands.append(addend)
    aliases = {}
    if has_alias:
        in_specs.append(pl.BlockSpec(memory_space=pl.ANY))
        aliases = {len(operands): 0}
        operands.append(alias_out)
    return pl.pallas_call(
        body, name=name, grid=grid, in_specs=in_specs, out_specs=pl.BlockSpec(*o_spec),
        out_shape=jax.ShapeDtypeStruct(out_shape, out_dtype),
        scratch_shapes=[pltpu.VMEM(o_dense, F32)] if nk > 1 else [],
        input_output_aliases=aliases,
        compiler_params=pltpu.CompilerParams(dimension_semantics=("parallel", "parallel", "arbitrary"),
                                             vmem_limit_bytes=VMEM_LIMIT),
    )(*operands)


NN, NT, TN = ((1,), (0,)), ((1,), (1,)), ((0,), (0,))
TM = 512
TT = 1024
TN_PREF = 1408
WHOLE_K = 2048


def mm_fwd(name, x, wg, l, kind, out_dtype=F32, parts=False, xl=None):
    if isinstance(wg, dict):
        wg, l = wg[l], 0
    _, _, R, C = wg.shape
    T = x.shape[-2]
    tm = _tile(T, TM, 8)
    if kind == "col":
        tn = _tile(C, TN_PREF)
        npc = C // tn
        grid = (T // tm, N_CHIPS * npc, 1)
        a_blk = (tm, R) if xl is None else (None, tm, R)
        a_map = (lambda i, j, k: (i, 0)) if xl is None else (lambda i, j, k: (xl, i, 0))
        b_spec = ((None, None, R, tn), lambda i, j, k: (j // npc, l, 0, j % npc))
        if parts:
            out_shape = (N_CHIPS, T, C)
            o_spec = ((None, tm, tn), lambda i, j, k: (j // npc, i, j % npc))
        else:
            out_shape = (T, N_CHIPS * C)
            o_spec = ((tm, tn), lambda i, j, k: (i, j))
    elif N_CHIPS * R <= WHOLE_K:
        tn = _tile(C, 1024)
        grid = (T // tm, C // tn, 1)
        a_blk = (tm, N_CHIPS * R)
        a_map = lambda i, j, k: (i, 0)
        b_spec = ((N_CHIPS, None, R, tn), lambda i, j, k: (0, l, 0, j))
        out_shape = (T, C)
        o_spec = ((tm, tn), lambda i, j, k: (i, j))
    else:
        tn = _tile(C, 1024)
        grid = (T // tm, C // tn, N_CHIPS)
        a_blk = (tm, R)
        a_map = lambda i, j, k: (i, k)
        b_spec = ((None, None, R, tn), lambda i, j, k: (k, l, 0, j))
        out_shape = (T, C)
        o_spec = ((tm, tn), lambda i, j, k: (i, j))
    return _mm(name, x, wg, out_shape, out_dtype, grid, (a_blk, a_map), b_spec, o_spec, NN)


def mm_bwd_x(name, dy, wg, l, kind, out_dtype=F32, parts=False, addend=None):
    if isinstance(wg, dict):
        wg, l = wg[l], 0
    _, _, R, C = wg.shape
    T = dy.shape[-2]
    tm = _tile(T, TM, 8)
    if kind == "col":
        tk = _tile(C, TN_PREF)
        npc = C // tk
        tno = _tile(R, 2048)
        grid = (T // tm, R // tno, N_CHIPS * npc)
        if parts:
            a_spec = ((None, tm, tk), lambda i, j, k: (k // npc, i, k % npc))
        else:
            a_spec = ((tm, tk), lambda i, j, k: (i, k))
        b_spec = ((None, None, tno, tk), lambda i, j, k: (k // npc, l, j, k % npc))
        out_shape = (T, R)
        o_spec = ((tm, tno), lambda i, j, k: (i, j))
    elif N_CHIPS * R <= WHOLE_K:
        grid = (T // tm, 1, 1)
        a_spec = ((tm, C), lambda i, j, k: (i, 0))
        b_spec = ((N_CHIPS, None, R, C), lambda i, j, k: (0, l, 0, 0))
        out_shape = (T, N_CHIPS * R)
        o_spec = ((tm, N_CHIPS * R), lambda i, j, k: (i, 0))
    else:
        grid = (T // tm, N_CHIPS, 1)
        a_spec = ((tm, C), lambda i, j, k: (i, 0))
        b_spec = ((None, None, R, C), lambda i, j, k: (j, l, 0, 0))
        out_shape = (T, N_CHIPS * R)
        o_spec = ((tm, R), lambda i, j, k: (i, j))
    return _mm(name, dy, wg, out_shape, out_dtype, grid, a_spec, b_spec, o_spec, NT, addend=addend)


def mm_bwd_w(name, x, dy, dwg, l, kind, parts=False, xl=None):
    if isinstance(dwg, dict):
        return {**dwg, l: mm_bwd_w(name, x, dy, dwg[l], 0, kind, parts=parts, xl=xl)}
    _, _, R, C = dwg.shape
    T = dy.shape[-2]
    tt = _tile(T, TT, 16)
    nt = T // tt
    if kind == "col":
        tn = _tile(C, TN_PREF)
        npc = C // tn
        tr = _tile(R, 1024)
        grid = (R // tr, N_CHIPS * npc, nt)
        if xl is None:
            a_spec = ((tt, tr), lambda i, j, t: (t, i))
        else:
            a_spec = ((None, tt, tr), lambda i, j, t: (xl, t, i))
        if parts:
            b_spec = ((None, tt, tn), lambda i, j, t: (j // npc, t, j % npc))
        else:
            b_spec = ((tt, tn), lambda i, j, t: (t, j))
        o_spec = ((None, None, tr, tn), lambda i, j, t: (j // npc, l, i, j % npc))
    elif N_CHIPS * R <= WHOLE_K:
        tn = _tile(C, 1024)
        grid = (1, C // tn, nt)
        a_spec = ((tt, N_CHIPS * R), lambda i, j, t: (t, 0))
        b_spec = ((tt, tn), lambda i, j, t: (t, j))
        o_spec = ((N_CHIPS, None, R, tn), lambda i, j, t: (0, l, 0, j))
    else:
        tn = _tile(C, 1024)
        grid = (N_CHIPS, C // tn, nt)
        a_spec = ((tt, R), lambda i, j, t: (t, i))
        b_spec = ((tt, tn), lambda i, j, t: (t, j))
        o_spec = ((None, None, R, tn), lambda i, j, t: (i, l, 0, j))
    return _mm(name, x, dy, dwg.shape, dwg.dtype, grid, a_spec, b_spec, o_spec, TN, alias_out=dwg)


def _sigmoid(x):
    return 0.5 * jnp.tanh(0.5 * x) + 0.5


def f_swiglu(gate, up):
    return (gate * _sigmoid(gate) * up,)


def ffn_gate_up(name, x, wg_gate, wg_up, l):
    if isinstance(wg_gate, dict):
        wg_gate, wg_up, l = wg_gate[l], wg_up[l], 0
    _, _, R, C = wg_gate.shape
    T = x.shape[0]
    tm = _tile(T, TM, 8)
    tn = _tile(C, TN_PREF)
    npc = C // tn

    def body(x_ref, g_ref, u_ref, gate_ref, up_ref, act_ref):
        xv = x_ref[...].astype(BF16)
        gate = jnp.dot(xv, g_ref[...], preferred_element_type=F32)
        up = jnp.dot(xv, u_ref[...], preferred_element_type=F32)
        gate_ref[...] = gate
        up_ref[...] = up
        act_ref[...] = f_swiglu(gate, up)[0].astype(act_ref.dtype)

    w_spec = pl.BlockSpec((None, None, R, tn), lambda i, j: (j // npc, l, 0, j % npc))
    o_spec = pl.BlockSpec((tm, tn), lambda i, j: (i, j))
    N = N_CHIPS * C
    return pl.pallas_call(
        body, name=name, grid=(T // tm, N_CHIPS * npc),
        in_specs=[pl.BlockSpec((tm, R), lambda i, j: (i, 0)), w_spec, w_spec], out_specs=[o_spec, o_spec, o_spec],
        out_shape=[jax.ShapeDtypeStruct((T, N), F32), jax.ShapeDtypeStruct((T, N), F32), jax.ShapeDtypeStruct((T, N), BF16)],
        compiler_params=_cp(2),
    )(x, wg_gate, wg_up)


def ffn_in_dx(name, dgate, dup, wg_gate, wg_up, l):
    if isinstance(wg_gate, dict):
        wg_gate, wg_up, l = wg_gate[l], wg_up[l], 0
    _, _, R, C = wg_gate.shape
    T = dgate.shape[0]
    tm = _tile(T, TM, 8)
    tk = _tile(C, TN_PREF)
    npc = C // tk
    nk = N_CHIPS * npc

    def body(dg_ref, du_ref, wg_ref, wu_ref, o_ref, acc):
        k = pl.program_id(1)

        @pl.when(k == 0)
        def _():
            acc[...] = jnp.zeros(acc.shape, F32)

        @pl.when(k < nk)
        def _():
            acc[...] += lax.dot_general(dg_ref[...], wg_ref[...], (NT, ((), ())), preferred_element_type=F32)

        @pl.when(k >= nk)
        def _():
            acc[...] += lax.dot_general(du_ref[...], wu_ref[...], (NT, ((), ())), preferred_element_type=F32)

        @pl.when(k == 2 * nk - 1)
        def _():
            o_ref[...] = acc[...]

    first = lambda k: jnp.minimum(k, nk - 1)
    second = lambda k: jnp.maximum(k - nk, 0)
    return pl.pallas_call(
        body, name=name, grid=(T // tm, 2 * nk),
        in_specs=[pl.BlockSpec((tm, tk), lambda i, k: (i, first(k))), pl.BlockSpec((tm, tk), lambda i, k: (i, second(k))),
                  pl.BlockSpec((None, None, R, tk), lambda i, k: (first(k) // npc, l, 0, first(k) % npc)),
                  pl.BlockSpec((None, None, R, tk), lambda i, k: (second(k) // npc, l, 0, second(k) % npc))],
        out_specs=pl.BlockSpec((tm, R), lambda i, k: (i, 0)),
        out_shape=jax.ShapeDtypeStruct((T, R), F32),
        scratch_shapes=[pltpu.VMEM((tm, R), F32)],
        compiler_params=_cp(2),
    )(dgate, dup, wg_gate, wg_up)


def ffn_down_dx(name, df, wg_down, l, gate, up):
    if isinstance(wg_down, dict):
        wg_down, l = wg_down[l], 0
    _, _, R, C = wg_down.shape
    T = df.shape[0]
    tm = _tile(T, TM, 8)

    def body(df_ref, w_ref, gate_ref, up_ref, dg_ref, du_ref):
        dact = lax.dot_general(df_ref[...].astype(BF16), w_ref[...], (NT, ((), ())), preferred_element_type=F32)
        _, pull = jax.vjp(lambda g, u: f_swiglu(g, u)[0], gate_ref[...], up_ref[...])
        dg, du = pull(dact)
        dg_ref[...] = dg.astype(dg_ref.dtype)
        du_ref[...] = du.astype(du_ref.dtype)

    t_spec = pl.BlockSpec((tm, R), lambda i, j: (i, j))
    return pl.pallas_call(
        body, name=name, grid=(T // tm, N_CHIPS),
        in_specs=[pl.BlockSpec((tm, C), lambda i, j: (i, 0)), pl.BlockSpec((None, None, R, C), lambda i, j: (j, l, 0, 0)),
                  t_spec, t_spec],
        out_specs=[t_spec, t_spec],
        out_shape=[jax.ShapeDtypeStruct((T, N_CHIPS * R), BF16)] * 2,
        compiler_params=_cp(2),
    )(df, wg_down, gate, up)


def _rms(x, g):
    return x * lax.rsqrt(jnp.mean(x * x, axis=-1, keepdims=True) + EPS) * g


def f_prenorm(h, g):
    return (_rms(h, g),)


def f_prenorm_thru(h, g):
    return _rms(h, g), h


def f_post_pre(h, m, g_post, g_pre):
    h1 = h + _rms(m, g_post)
    return h1, _rms(h1, g_pre)


def f_post(h1, f, g):
    return (h1 + _rms(f, g),)


def f_ple(h2, e, zg, g):
    return (h2 + _rms(e * _sigmoid(zg), g),)


def f_ple_pre(h2, e, zg, g, g_next):
    h3 = h2 + _rms(e * _sigmoid(zg), g)
    return h3, _rms(h3, g_next)


def _gelu(x):
    return 0.5 * x * (1.0 + lax.erf(x * 0.7071067811865476))


def f_gm_in(z, ln_g, ln_b):
    w = z.shape[-1] // 2
    u = _gelu(z[:, :w])
    v = _gelu(z[:, w:])
    mu = jnp.mean(v, axis=-1, keepdims=True)
    vc = v - mu
    vn = vc * lax.rsqrt(jnp.mean(vc * vc, axis=-1, keepdims=True) + EPS) * ln_g + ln_b
    return u, vn


def f_gm_spatial(u, vn, ws, bs):
    t = lax.broadcasted_iota(jnp.int32, ws.shape, 0)
    s = lax.broadcasted_iota(jnp.int32, ws.shape, 1)
    wm = jnp.where(t >= s, ws, 0.0).astype(BF16)
    ys = []
    for n in range(u.shape[0] // GM_CHUNK):
        rows = slice(n * GM_CHUNK, (n + 1) * GM_CHUNK)
        sv = jnp.dot(wm, vn[rows].astype(BF16), preferred_element_type=F32) + bs
        ys.append(u[rows] * sv)
    return (jnp.concatenate(ys, axis=0) if len(ys) > 1 else ys[0],)


def f_adam(w, g, m, v):
    m = ADAM_B1 * m + (1.0 - ADAM_B1) * g
    v = ADAM_B2 * v + (1.0 - ADAM_B2) * jnp.square(g)
    m_hat = m / (1.0 - ADAM_B1 ** ADAM_STEP)
    v_hat = v / (1.0 - ADAM_B2 ** ADAM_STEP)
    delta = -ADAM_LR * (m_hat / (jnp.sqrt(v_hat) + ADAM_EPS) + ADAM_WD * w)
    return delta, m, v


def _make_f_lb(n_layers):
    def f_lb(*logits):
        mx = functools.reduce(jnp.maximum, logits)
        ex = [jnp.exp(r - mx) for r in logits]
        tot = functools.reduce(lambda a, b: a + b, ex)
        sm = [e / tot for e in ex]
        outs = []
        run = jnp.zeros_like(sm[0])
        for j in range(n_layers):
            if j > 0:
                run = run + sm[j]
            lb = run
            outs += [jnp.log(jnp.maximum(lb, LB_FLOOR)), jnp.log(1.0 - lb), 1.0 - lb]
        return tuple(outs)
    return f_lb


def rows_fwd(name, fn, rows, params, out_dtypes, tm=256):
    T = rows[0].shape[0]
    tm = _tile(T, tm, 16)
    ins = [(r, (tm, r.shape[1]), lambda i: (i, 0)) for r in rows]
    ins += [(p, p.shape, lambda i: (0, 0)) for p in params]
    shapes = jax.eval_shape(lambda *a: fn(*a), *[jax.ShapeDtypeStruct((tm, r.shape[1]), F32) for r in rows],
                            *[jax.ShapeDtypeStruct(p.shape, F32) for p in params])
    outs = [((T, s.shape[1]), dt, (tm, s.shape[1]), lambda i: (i, 0), None) for s, dt in zip(shapes, out_dtypes)]
    return bmap_fwd(name, fn, (T // tm,), ins, outs)


def rows_bwd(name, fn, rows, params, cots, row_grad_dtypes, tm=256):
    T = rows[0].shape[0]
    tm = _tile(T, tm, 16)
    ins = [(r, (tm, r.shape[1]), lambda i: (i, 0)) for r in rows]
    ins += [(p, p.shape, lambda i: (0, 0)) for p in params]
    cts = [None if c is None else (c, (tm, c.shape[1]), lambda i: (i, 0)) for c in cots]
    grads = [(i, dt, None) for i, dt in enumerate(row_grad_dtypes) if dt is not None]
    grads += [(len(rows) + j, F32, 0) for j in range(len(params))]
    return bmap_bwd(name, fn, (T // tm,), ins, cts, grads)


def _log_sigmoid(z):
    return jnp.minimum(z, 0.0) - jnp.log(1.0 + jnp.exp(-jnp.abs(z)))


def _hg_gates(zf, ll0, ll1, oml):
    x2 = ll1 + _log_sigmoid(zf)
    mx = jnp.maximum(ll0, x2)
    g = mx + jnp.log(jnp.exp(ll0 - mx) + jnp.exp(x2 - mx))
    return g, oml * _sigmoid(-zf)


def hg_constants(n):
    levels = n.bit_length() - 1
    r = jnp.arange(n, dtype=jnp.int32)
    bounds = [r] + [((r >> (s + 1)) << (s + 1)) + ((1 << s) - 1) for s in range(levels)]
    sel = jnp.concatenate([(r[None, :] <= bd[:, None]) for bd in bounds], axis=0).astype(BF16)
    later = jnp.stack([((r >> s) & 1) for s in range(levels)])
    sign = jnp.broadcast_to((2 * later - 1).astype(F32)[:, :, None], (levels, n, LANES))
    pair = jnp.stack([((r[:, None] >> (s + 1)) == (r[None, :] >> (s + 1))) & (later[s][:, None] == 1) & (later[s][None, :] == 0)
                      for s in range(levels)]).astype(F32)
    return sel, sel.T, sign, pair


def _dot2(m, x):
    hi = x.astype(BF16)
    lo = (x - hi.astype(F32)).astype(BF16)
    p = jnp.dot(m, jnp.concatenate([hi, lo], axis=1), preferred_element_type=F32)
    w = x.shape[1]
    return p[:, :w] + p[:, w:]


@jax.custom_vjp
def _sel_dot(sel, selt, g):
    return _dot2(sel, g)


def _sel_dot_fwd(sel, selt, g):
    return _dot2(sel, g), (sel, selt)


def _sel_dot_bwd(res, d):
    sel, selt = res
    return jnp.zeros_like(sel), jnp.zeros_like(selt), _dot2(selt, d)


_sel_dot.defvjp(_sel_dot_fwd, _sel_dot_bwd)


def _hg_state(st, zf, zi, ll0, ll1, oml, tri):
    g, k = _hg_gates(zf, ll0, ll1, oml)
    b = _dot2(tri, g)
    tot = jnp.sum(g, axis=0, keepdims=True)
    kd = k * jnp.exp(tot - b)
    return st * jnp.exp(tot) + jnp.dot(zi.T.astype(BF16), kd.astype(BF16), preferred_element_type=F32)


def _hg_step(st, zq, zf, zi, zg, ll0, ll1, oml, onorm, sel, selt, sign, pair):
    n = zq.shape[0]
    levels = n.bit_length() - 1
    q = zq * _sigmoid(zq)
    g, k = _hg_gates(zf, ll0, ll1, oml)
    sums = _sel_dot(sel, selt, g)
    b = sums[:n]
    tot = jnp.sum(g, axis=0, keepdims=True)
    o = lax.dot_general((q * jnp.exp(b)).astype(BF16), st.astype(BF16), (NT, ((), ())), preferred_element_type=F32)
    a = jnp.zeros((n, n), F32)
    for s in range(levels):
        e = jnp.exp(sign[s] * (b - sums[(s + 1) * n:(s + 2) * n]))
        al = lax.dot_general((q * e).astype(BF16), (k * e).astype(BF16), (NT, ((), ())), preferred_element_type=F32)
        a = a + pair[s] * al
    o = o + jnp.dot(a.astype(BF16), zi.astype(BF16), preferred_element_type=F32)
    o = o + jnp.sum(q * k, axis=1, keepdims=True) * zi
    kd = k * jnp.exp(tot - b)
    st_new = st * jnp.exp(tot) + jnp.dot(zi.T.astype(BF16), kd.astype(BF16), preferred_element_type=F32)
    og = _rms(o, onorm) * (zg * _sigmoid(zg))
    return og, st_new


def _whole(arr, n_grid):
    zeros = (0,) * arr.ndim
    return pl.BlockSpec(arr.shape, (lambda h, n: zeros) if n_grid == 2 else (lambda i: zeros))


def _hg_dims(proj4, n_seq):
    _, T, D = proj4.shape
    S = T // n_seq
    hp = HG_HEADS_PER if (D // LANES) % HG_HEADS_PER == 0 else 1
    tb = min(HG_BLOCK, S)
    streams = [(b, hl) for b in range(n_seq) for hl in range(hp)]
    return T, D, S, hp, D // (LANES * hp), LANES * hp, tb, S // tb, tb // HG_SUB, streams


def hgrn_fwd(name, proj4, ll0, ll1, oml, onorm, n_seq):
    T, D, S, hp, n_hg, W, tb, nblk, nsub, streams = _hg_dims(proj4, n_seq)
    ns = len(streams)

    def body(p_ref, ll0_ref, ll1_ref, oml_ref, on_ref, sel_ref, selt_ref, later_ref, pair_ref, og_ref, st_ref, st):
        @pl.when(pl.program_id(1) == 0)
        def _():
            st[...] = jnp.zeros(st.shape, F32)

        st_ref[...] = st[...]
        on = on_ref[...]

        def step(j, carry):
            r = pl.ds(pl.multiple_of(j * HG_SUB, HG_SUB), HG_SUB)
            consts = (sel_ref[...], selt_ref[...], later_ref[...], pair_ref[...])
            args = []
            for si, (b, hl) in enumerate(streams):
                ln = slice(hl * LANES, (hl + 1) * LANES)
                args.append((st[si], p_ref[0, b, r, ln], p_ref[1, b, r, ln], p_ref[2, b, r, ln], p_ref[3, b, r, ln],
                             ll0_ref[:, ln], ll1_ref[:, ln], oml_ref[:, ln], on) + consts)
            res = [_hg_step(*a) for a in args]
            for si, (b, hl) in enumerate(streams):
                og_ref[b, r, hl * LANES:(hl + 1) * LANES] = res[si][0].astype(og_ref.dtype)
                st[si] = res[si][1]
            return carry

        lax.fori_loop(0, nsub, step, 0)

    vec = pl.BlockSpec((1, W), lambda h, n: (0, h))
    consts = hg_constants(HG_SUB)
    og, states = pl.pallas_call(
        body, name=name, grid=(n_hg, nblk),
        in_specs=[pl.BlockSpec((4, n_seq, tb, W), lambda h, n: (0, 0, n, h)), vec, vec, vec,
                  pl.BlockSpec((1, LANES), lambda h, n: (0, 0))] + [_whole(c, 2) for c in consts],
        out_specs=[pl.BlockSpec((n_seq, tb, W), lambda h, n: (0, n, h)),
                   pl.BlockSpec((None, None, ns, LANES, LANES), lambda h, n: (h, n, 0, 0, 0))],
        out_shape=[jax.ShapeDtypeStruct((n_seq, S, D), BF16),
                   jax.ShapeDtypeStruct((n_hg, nblk, ns, LANES, LANES), F32)],
        scratch_shapes=[pltpu.VMEM((ns, LANES, LANES), F32)],
        compiler_params=_cp(2),
    )(proj4.reshape(4, n_seq, S, D), ll0, ll1, oml, onorm, *consts)
    return og.reshape(T, D), states


def hgrn_bwd(name, proj4, states, dog, ll0, ll1, oml, onorm, n_seq):
    T, D, S, hp, n_hg, W, tb, nblk, nsub, streams = _hg_dims(proj4, n_seq)
    ns = len(streams)

    def body(p_ref, st_ref, dog_ref, ll0_ref, ll1_ref, oml_ref, on_ref, sel_ref, selt_ref, later_ref, pair_ref,
             dp_ref, dll0_ref, dll1_ref, doml_ref, don_ref, sbuf, dst):
        n_id = pl.program_id(1)

        @pl.when(n_id == 0)
        def _():
            dst[...] = jnp.zeros(dst.shape, F32)
            for ref in (dll0_ref, dll1_ref, doml_ref):
                ref[...] = jnp.zeros(ref.shape, F32)

        @pl.when(jnp.logical_and(n_id == 0, pl.program_id(0) == 0))
        def _():
            don_ref[...] = jnp.zeros(don_ref.shape, F32)

        on = on_ref[...]

        def fwd(j, carry):
            r = pl.ds(pl.multiple_of(j * HG_SUB, HG_SUB), HG_SUB)
            tri = sel_ref[0:HG_SUB, :]
            args = []
            for si, (b, hl) in enumerate(streams):
                ln = slice(hl * LANES, (hl + 1) * LANES)
                args.append((carry[si], p_ref[1, b, r, ln], p_ref[2, b, r, ln],
                             ll0_ref[:, ln], ll1_ref[:, ln], oml_ref[:, ln], tri))
            for si in range(ns):
                sbuf[si, j] = carry[si]
            return tuple(_hg_state(*a) for a in args)

        lax.fori_loop(0, nsub, fwd, tuple(st_ref[si] for si in range(ns)))

        def bwd(jj, carry):
            j = nsub - 1 - jj
            r = pl.ds(pl.multiple_of(j * HG_SUB, HG_SUB), HG_SUB)
            args, cts = [], []
            for si, (b, hl) in enumerate(streams):
                ln = slice(hl * LANES, (hl + 1) * LANES)
                args.append((sbuf[si, j], p_ref[0, b, r, ln], p_ref[1, b, r, ln], p_ref[2, b, r, ln],
                             p_ref[3, b, r, ln], ll0_ref[:, ln], ll1_ref[:, ln], oml_ref[:, ln], on))
                cts.append((dog_ref[b, r, ln].astype(F32), dst[si]))
            consts = (sel_ref[...], selt_ref[...], later_ref[...], pair_ref[...])
            step_fn = lambda *a: _hg_step(*a, *consts)
            ds = [jax.vjp(step_fn, *a)[1](ct) for a, ct in zip(args, cts)]
            d_on = carry
            for si, (b, hl) in enumerate(streams):
                ln = slice(hl * LANES, (hl + 1) * LANES)
                d = ds[si]
                dst[si] = d[0]
                for part in range(4):
                    dp_ref[part, b, r, ln] = d[1 + part].astype(dp_ref.dtype)
                dll0_ref[:, ln] += d[5]
                dll1_ref[:, ln] += d[6]
                doml_ref[:, ln] += d[7]
                d_on = d_on + d[8]
            return d_on

        don_ref[...] += lax.fori_loop(0, nsub, bwd, jnp.zeros((1, LANES), F32))

    last = nblk - 1
    vec = pl.BlockSpec((1, W), lambda h, n: (0, h))
    one = pl.BlockSpec((1, LANES), lambda h, n: (0, 0))
    consts = hg_constants(HG_SUB)
    dproj, d0, d1, d2, d_on = pl.pallas_call(
        body, name=name, grid=(n_hg, nblk),
        in_specs=[pl.BlockSpec((4, n_seq, tb, W), lambda h, n: (0, 0, last - n, h)),
                  pl.BlockSpec((None, None, ns, LANES, LANES), lambda h, n: (h, last - n, 0, 0, 0)),
                  pl.BlockSpec((n_seq, tb, W), lambda h, n: (0, last - n, h)), vec, vec, vec, one]
        + [_whole(c, 2) for c in consts],
        out_specs=[pl.BlockSpec((4, n_seq, tb, W), lambda h, n: (0, 0, last - n, h)), vec, vec, vec, one],
        out_shape=[jax.ShapeDtypeStruct((4, n_seq, S, D), BF16)] + [jax.ShapeDtypeStruct((1, D), F32)] * 3
        + [jax.ShapeDtypeStruct((1, LANES), F32)],
        scratch_shapes=[pltpu.VMEM((ns, nsub, LANES, LANES), F32), pltpu.VMEM((ns, LANES, LANES), F32)],
        compiler_params=_cp(2),
    )(proj4.reshape(4, n_seq, S, D), states, dog.reshape(n_seq, S, D), ll0, ll1, oml, onorm, *consts)
    return dproj.reshape(4, T, D), d0, d1, d2, d_on


def _place():
    x, y, c = lax.axis_index("x"), lax.axis_index("y"), lax.axis_index("c")
    chips = [(1 - x, y), (x, 1 - y), (1 - x, 1 - y)]
    return x, y, c, chips


ANY = pl.BlockSpec(memory_space=pl.ANY)


def _comm_call(name, body, ins, out_shapes, sems, aliases=None):
    return pl.pallas_call(
        body, name=name, in_specs=[ANY] * len(ins), out_specs=[ANY] * len(out_shapes),
        out_shape=out_shapes, scratch_shapes=sems, input_output_aliases=aliases or {},
        compiler_params=pltpu.CompilerParams(has_side_effects=True),
    )(*ins)


HBM_SPEC = pl.BlockSpec(memory_space=pltpu.HBM)
SEM_SPEC = pl.BlockSpec(memory_space=pltpu.SEMAPHORE)
SPLIT_EFFECT = pltpu.SideEffectType.DATAFLOW_SIDE_EFFECTING
N_PEER_CHIPS = 3


def split_start(name, build, arrays, n_peers=N_PEER_CHIPS):
    n = len(arrays)

    def body(*refs):
        send, recv = refs[n], refs[n + 1]
        token = refs[2 * n + 2]
        starts, _ = build(refs[:n], send, recv)
        for cp in starts:
            cp.start()
        token[...] = jnp.zeros_like(token)

    res = pl.pallas_call(
        body, name=name,
        out_shape=(pltpu.SemaphoreType.DMA((n_peers,)), pltpu.SemaphoreType.DMA((n_peers,)),
                   *[pltpu.HBM(a.shape, a.dtype) for a in arrays], jax.ShapeDtypeStruct((8, LANES), F32)),
        in_specs=[HBM_SPEC] * n,
        out_specs=(SEM_SPEC, SEM_SPEC, *[HBM_SPEC] * n, pl.BlockSpec(memory_space=pltpu.VMEM)),
        input_output_aliases={i: 2 + i for i in range(n)},
        compiler_params=pltpu.CompilerParams(has_side_effects=SPLIT_EFFECT),
    )(*[pltpu.with_memory_space_constraint(a, pltpu.HBM) for a in arrays])
    return res[0], res[1], list(res[2:2 + n]), res[2 + n]


def split_wait(name, build, send, recv, arrays, after):
    n = len(arrays)

    def body(*refs):
        starts, arrivals = build(refs[:n], refs[n], refs[n + 1])
        for cp in starts:
            cp.wait_send()
        for cp in arrivals:
            cp.wait_recv()

    return list(pl.pallas_call(
        body, name=name, out_shape=tuple(pltpu.HBM(a.shape, a.dtype) for a in arrays),
        in_specs=[HBM_SPEC] * n + [SEM_SPEC, SEM_SPEC, ANY], out_specs=tuple([HBM_SPEC] * n),
        input_output_aliases={i: i for i in range(n)},
        compiler_params=pltpu.CompilerParams(has_side_effects=SPLIT_EFFECT),
    )(*arrays, send, recv, after))


def _row_half(ref, dim, who):
    rh = ref.shape[dim] // 2
    return pl.ds(who * rh, rh)


def gather_build(refs, send, recv):
    x, y, c, chips = _place()
    q = 2 * x + y
    starts, arrivals = [], []
    for buf in refs:
        rows = _row_half(buf, 2, c)
        for j, (px, py) in enumerate(chips):
            mine, got = buf.at[q, :, rows], buf.at[2 * px + py, :, rows]
            starts.append(pltpu.make_async_remote_copy(src_ref=mine, dst_ref=mine, send_sem=send.at[j], recv_sem=recv.at[j],
                                                       device_id=(px, py, c), device_id_type=MESH_ID))
            arrivals.append(pltpu.make_async_remote_copy(src_ref=got, dst_ref=got, send_sem=send.at[j], recv_sem=recv.at[j],
                                                         device_id=(px, py, c), device_id_type=MESH_ID))
    return starts, arrivals


def gather_forward(name, bufs):
    n = len(bufs)

    def body(*refs):
        dst = refs[n:2 * n]
        send, recv = refs[2 * n:]
        x, y, c, chips = _place()
        sib = (x, y, 1 - c)
        cps = []
        for a in range(n):
            for j, (px, py) in enumerate(chips):
                got = dst[a].at[2 * px + py, :, _row_half(dst[a], 2, c)]
                cps.append(pltpu.make_async_remote_copy(src_ref=got, dst_ref=got, send_sem=send.at[a, j], recv_sem=recv.at[a, j],
                                                        device_id=sib, device_id_type=MESH_ID))
        for cp in cps:
            cp.start()
        for a in range(n):
            for j, (px, py) in enumerate(chips):
                theirs = dst[a].at[2 * px + py, :, _row_half(dst[a], 2, 1 - c)]
                pltpu.make_async_remote_copy(src_ref=theirs, dst_ref=theirs, send_sem=send.at[a, j], recv_sem=recv.at[a, j],
                                             device_id=sib, device_id_type=MESH_ID).wait_recv()
        for cp in cps:
            cp.wait_send()

    outs = [jax.ShapeDtypeStruct(s.shape, s.dtype) for s in bufs]
    sems = [pltpu.SemaphoreType.DMA((n, N_PEER_CHIPS))] * 2
    return _comm_call(name, body, bufs, outs, sems, aliases={a: a for a in range(n)})


def scatter_build(refs, send, recv):
    n = len(refs) // 2
    x, y, c, _ = _place()
    starts, arrivals = [], []
    for a in range(n):
        src, dst = refs[a], refs[n + a]
        for k in range(1, N_DEV):
            px, py, pc = x ^ (k >> 2), y ^ ((k >> 1) & 1), c ^ (k & 1)
            theirs = src.at[2 * px + py, :, _row_half(src, 2, pc)]
            starts.append(pltpu.make_async_remote_copy(src_ref=theirs, dst_ref=dst.at[k - 1], send_sem=send.at[k - 1],
                                                       recv_sem=recv.at[k - 1], device_id=(px, py, pc), device_id_type=MESH_ID))
            arrivals.append(pltpu.make_async_remote_copy(src_ref=dst.at[k - 1], dst_ref=dst.at[k - 1], send_sem=send.at[k - 1],
                                                         recv_sem=recv.at[k - 1], device_id=(px, py, pc), device_id_type=MESH_ID))
    return starts, arrivals


def share_build(refs, send, recv):
    dst = refs[0]
    x, y, c, _ = _place()
    mine = dst.at[4 * x + 2 * y + c]
    starts, arrivals = [], []
    for k in range(1, N_DEV):
        px, py, pc = x ^ (k >> 2), y ^ ((k >> 1) & 1), c ^ (k & 1)
        got = dst.at[4 * px + 2 * py + pc]
        starts.append(pltpu.make_async_remote_copy(src_ref=mine, dst_ref=mine, send_sem=send.at[k - 1], recv_sem=recv.at[k - 1],
                                                   device_id=(px, py, pc), device_id_type=MESH_ID))
        arrivals.append(pltpu.make_async_remote_copy(src_ref=got, dst_ref=got, send_sem=send.at[k - 1], recv_sem=recv.at[k - 1],
                                                     device_id=(px, py, pc), device_id_type=MESH_ID))
    return starts, arrivals


def join_row_halves(name, bufs):
    n = len(bufs)

    def body(*refs):
        dst = refs[n:2 * n]
        send, recv = refs[2 * n:]
        x, y, c, _ = _place()
        cps = []
        for a in range(n):
            mine = dst[a].at[:, _row_half(dst[a], 1, c)]
            cps.append(pltpu.make_async_remote_copy(src_ref=mine, dst_ref=mine, send_sem=send.at[a], recv_sem=recv.at[a],
                                                    device_id=(x, y, 1 - c), device_id_type=MESH_ID))
        for cp in cps:
            cp.start()
        for a in range(n):
            theirs = dst[a].at[:, _row_half(dst[a], 1, 1 - c)]
            pltpu.make_async_remote_copy(src_ref=theirs, dst_ref=theirs, send_sem=send.at[a], recv_sem=recv.at[a],
                                         device_id=(x, y, 1 - c), device_id_type=MESH_ID).wait_recv()
        for cp in cps:
            cp.wait_send()

    outs = [jax.ShapeDtypeStruct(b.shape, b.dtype) for b in bufs]
    sems = [pltpu.SemaphoreType.DMA((n,))] * 2
    return _comm_call(name, body, bufs, outs, sems, aliases={a: a for a in range(n)})


def share_with_all(name, packed, me):
    slots = lax.dynamic_update_slice(jnp.zeros((N_DEV,) + packed.shape, packed.dtype), packed[None], (me, 0, 0))

    def body(_, dst, send, recv):
        x, y, c, _ = _place()
        me = 4 * x + 2 * y + c
        cps = []
        for k in range(1, N_DEV):
            px, py, pc = x ^ (k >> 2), y ^ ((k >> 1) & 1), c ^ (k & 1)
            cps.append(pltpu.make_async_remote_copy(src_ref=dst.at[me], dst_ref=dst.at[me], send_sem=send.at[k - 1],
                                                    recv_sem=recv.at[k - 1], device_id=(px, py, pc), device_id_type=MESH_ID))
        for cp in cps:
            cp.start()
        for k in range(1, N_DEV):
            px, py, pc = x ^ (k >> 2), y ^ ((k >> 1) & 1), c ^ (k & 1)
            got = dst.at[4 * px + 2 * py + pc]
            pltpu.make_async_remote_copy(src_ref=got, dst_ref=got, send_sem=send.at[k - 1], recv_sem=recv.at[k - 1],
                                         device_id=(px, py, pc), device_id_type=MESH_ID).wait_recv()
        for cp in cps:
            cp.wait_send()

    outs = [jax.ShapeDtypeStruct(slots.shape, slots.dtype)]
    sems = [pltpu.SemaphoreType.DMA((N_DEV - 1,))] * 2
    return _comm_call(name, body, [slots], outs, sems, aliases={0: 0})[0]


def _w_tiles(R, C):
    return _tile(R, max(16, (1 << 20) // (4 * C) // 16 * 16), 16)


def cast_bf16(w, l, q_arr):
    _, R, C = w.shape
    tr = _w_tiles(R, C)
    ins = [(w, (None, tr, C), lambda r, q: (l, r, 0))]
    outs = [((N_CHIPS, 1, R, C), BF16, (None, None, tr, C), lambda r, q: (q[0], 0, r, 0), None)]
    return bmap_fwd("cast_bf16", lambda a: (a,), (R // tr,), ins, outs, scalars=(q_arr,))[0]


def sum_partials(own, landed, into, l, q_arr, c_arr):
    n_land, _, rh, C = landed.shape
    tr = _w_tiles(rh, C)
    nb = rh // tr
    blk = (None, None, tr, C)
    ins = [(own, blk, lambda r, q, c: (q[0], 0, c[0] * nb + r, 0))]
    ins += [(landed, blk, (lambda r, q, c, kk=kk: (kk, 0, r, 0))) for kk in range(n_land)]
    outs = [(into.shape, F32, (None, tr, C), lambda r, q, c: (l, c[0] * nb + r, 0), None)]
    return bmap_fwd("sum_partials", lambda *t: (functools.reduce(lambda u, v: u + v, t),), (nb,), ins, outs,
                    scalars=(q_arr, c_arr), into=into)[0]


def sum_devices(slots):
    nd, NR, C = slots.shape
    tr = _tile(NR, 512, 8)
    ins = [(slots, (None, tr, C), (lambda r, dd=dd: (dd, r, 0))) for dd in range(nd)]
    outs = [((NR, C), F32, (tr, C), lambda r: (r, 0), None)]
    return bmap_fwd("sum_devices", lambda *a: (functools.reduce(lambda u, v: u + v, a),), (NR // tr,), ins, outs)[0]


def adamw(name, w, g, m, v, with_grad=False):
    if w.ndim == 2:
        R, C = w.shape
        tr = _w_tiles(R, C)
        spec = ((tr, C), lambda r: (r, 0))
        grid = (R // tr,)
    else:
        L, R, C = w.shape
        tr = _w_tiles(R, C)
        spec = ((None, tr, C), lambda l, r: (l, r, 0))
        grid = (L, R // tr)
    ins = [(a,) + spec for a in (w, g, m, v)]
    outs = [(w.shape, F32) + spec + (None,)] * (4 if with_grad else 3)
    fn = (lambda a, b, c, d: f_adam(a, b, c, d) + (b,)) if with_grad else f_adam
    return bmap_fwd(name, fn, grid, ins, outs)


def loss_and_grad(h, target):
    T, D = h.shape
    tm = _tile(T, 256, 8)

    def fn(hv, tv):
        d = hv - tv
        return jnp.sum(d * d, keepdims=True).reshape(1, 1) * (0.5 / D), d * (1.0 / D)

    ins = [(h, (tm, D), lambda i: (i, 0)), (target, (tm, D), lambda i: (i, 0))]
    outs = [((1, 1), F32, (1, 1), lambda i: (0, 0), 0), ((T, D), F32, (tm, D), lambda i: (i, 0), None)]
    return bmap_fwd("loss_and_grad", fn, (T // tm,), ins, outs)


BIG = ("hg_w_in", "hg_w_out", "gm_w_in", "gm_w_out", "ffn_w_gate", "ffn_w_up", "ffn_w_down", "ple_w_proj", "ple_w_gate")
KIND = {"hg_w_in": "col", "hg_w_out": "row", "gm_w_in": "col", "gm_w_out": "row", "ffn_w_gate": "col",
        "ffn_w_up": "col", "ffn_w_down": "row", "ple_w_proj": "col", "ple_w_gate": "row"}
SMALL = ("hg_lb_logits", "hg_out_norm", "gm_ln_g", "gm_ln_b", "gm_w_s", "gm_b_s", "norm_mix_pre", "norm_mix_post",
         "norm_ffn_pre", "norm_ffn_post", "ple_norm")
WEIGHTS = ("hg_w_in", "hg_lb_logits", "hg_out_norm", "hg_w_out", "gm_w_in", "gm_ln_g", "gm_ln_b", "gm_w_s", "gm_b_s",
           "gm_w_out", "norm_mix_pre", "norm_mix_post", "norm_ffn_pre", "norm_ffn_post", "ffn_w_gate", "ffn_w_up",
           "ffn_w_down", "ple_w_proj", "ple_w_gate", "ple_norm")


def _pack(arrs):
    rows = []
    for a in arrs:
        flat = a.reshape(-1)
        pad = (-flat.shape[0]) % (8 * LANES)
        rows.append(jnp.pad(flat, (0, pad)).reshape(-1, LANES))
    n_rows = sum(r.shape[0] for r in rows)
    rows.append(jnp.zeros(((-n_rows) % PACK_ROWS, LANES), F32))
    return jnp.concatenate(rows, axis=0)


def _unpack(packed, shapes):
    out, r = [], 0
    for s in shapes:
        size = 1
        for d in s:
            size *= d
        nr = -(-size // (8 * LANES)) * 8
        out.append(packed[r:r + nr].reshape(-1)[:size].reshape(s))
        r += nr
    return out


def _step(x, p, W, M, V, loss_target):
    n_seq, S, D = x.shape
    T = n_seq * S
    depth = p.shape[0]
    n_hg = W["hg_w_in"].shape[0]
    x2 = x.reshape(T, D)
    p3 = p.reshape(depth, T, p.shape[-1])
    tgt = loss_target.reshape(T, D)
    xi, yi, ci = lax.axis_index("x"), lax.axis_index("y"), lax.axis_index("c")
    q_me = 2 * xi + yi
    c_arr = jnp.reshape(ci, (1,)).astype(jnp.int32)
    q_arr = jnp.reshape(q_me, (1,)).astype(jnp.int32)

    groups = {}
    for i in range(depth):
        mix = ("hg_w_in", "hg_w_out") if i % 2 == 0 else ("gm_w_in", "gm_w_out")
        groups[i, "mix"] = [(k, i // 2) for k in mix]
        groups[i, "rest"] = [(k, i) for k in ("ffn_w_gate", "ffn_w_up", "ffn_w_down", "ple_w_proj", "ple_w_gate")]
    G = {k: {} for k in BIG}
    DW = {k: {l: lax.empty((N_CHIPS, 1) + W[k].shape[1:], BF16) for l in range(W[k].shape[0])} for k in BIG}
    in_flight = {}

    casts = {}

    def cast_group(i, part, dep):
        qa = q_arr if dep is None else lax.optimization_barrier((q_arr, dep))[0]
        casts[i, part] = [cast_bf16(W[k], l, qa) for k, l in groups[i, part]]

    def start_gather(i, part, dep):
        bufs = casts.pop((i, part))
        if dep is not None:
            bufs = list(lax.optimization_barrier((tuple(bufs), dep))[0])
        send, recv, arrs, tok = split_start("gather_start_%d_%s" % (i, part), gather_build, bufs)
        in_flight[i, part] = (send, recv, arrs)
        return tok

    def finish_gather(i, part, after):
        send, recv, arrs = in_flight.pop((i, part))
        arrs = split_wait("gather_wait_%d_%s" % (i, part), gather_build, send, recv, arrs, after)
        for (k, l), buf in zip(groups[i, part], gather_forward("gather_forward_%d_%s" % (i, part), arrs)):
            G[k][l] = buf
        return buf

    def after_token(row_arr, *toks):
        return functools.reduce(lambda u, t: u + t[0:1, 0:1], toks, row_arr)

    cast_group(0, "mix", None)
    tok_mix = start_gather(0, "mix", None)
    cast_group(0, "rest", tok_mix)
    tok_rest = start_gather(0, "rest", None)
    finish_gather(0, "mix", tok_mix + tok_rest)
    me = 4 * xi + 2 * yi + ci
    ln_full = share_with_all("share_ln", _pack([W["gm_ln_g"], W["gm_ln_b"]]), me)
    n_gm, dq = W["gm_ln_g"].shape
    ln_parts = [_unpack(ln_full[4 * qx + 2 * qy + 0], [(n_gm, dq), (n_gm, dq)]) for qx in range(2) for qy in range(2)]
    ln_g = jnp.concatenate([lp[0] for lp in ln_parts], axis=1)
    ln_b = jnp.concatenate([lp[1] for lp in ln_parts], axis=1)

    row = lambda a, i: a[i][None, :]
    f_lb = _make_f_lb(n_hg)
    lb_rows = [row(W["hg_lb_logits"], j) for j in range(n_hg)]
    one = (1, D)
    lb_ins = [(r, one, lambda i: (0, 0)) for r in lb_rows]
    lb_out = bmap_fwd("hg_lower_bounds", f_lb, (1,), lb_ins, [(one, F32, one, lambda i: (0, 0), None)] * (3 * n_hg))

    saved = []
    h = x2
    a = rows_fwd("prenorm", f_prenorm, [h], [row(W["norm_mix_pre"], 0)], [BF16])[0]
    for i in range(depth):
        j = i // 2
        sv = {"h": h, "a": a}
        if i > 0:
            finish_gather(i, "mix", h)
        if i % 2 == 0:
            proj4 = mm_fwd("hg_in", a, G["hg_w_in"], j, "col", parts=True)
            lbp = lb_out[3 * j:3 * j + 3]
            onorm = row(W["hg_out_norm"], j)
            og, states = hgrn_fwd("hgrn_fwd", proj4, *lbp, onorm, n_seq)
            m = mm_fwd("hg_out", og, G["hg_w_out"], j, "row")
            sv.update(proj4=proj4, states=states, og=og, lbp=lbp, onorm=onorm)
        else:
            z = mm_fwd("gm_in", a, G["gm_w_in"], j, "col")
            lg, lb_ = row(ln_g, j), row(ln_b, j)
            u, vn = rows_fwd("gm_gelu_ln", f_gm_in, [z], [lg, lb_], [F32, BF16], tm=128)
            ws = W["gm_w_s"][j]
            bs = W["gm_b_s"][j][:, :, None]
            gb = min(GM_BLOCK, S)
            sp_grid = (D // LANES, T // gb)
            sp_ins = [(u, (gb, LANES), lambda g, n: (n, g)), (vn, (gb, LANES), lambda g, n: (n, g)),
                      (ws, (None, GM_CHUNK, GM_CHUNK), lambda g, n: (g, 0, 0)),
                      (bs, (None, GM_CHUNK, 1), lambda g, n: (g, 0, 0))]
            y = bmap_fwd("gm_spatial", f_gm_spatial, sp_grid, sp_ins,
                         [((T, D), BF16, (gb, LANES), lambda g, n: (n, g), None)])[0]
            m = mm_fwd("gm_out", y, G["gm_w_out"], j, "row")
            sv.update(z=z, lg=lg, lb_=lb_, sp_ins=sp_ins, sp_grid=sp_grid, y=y)
        g_post, g_fpre = row(W["norm_mix_post"], i), row(W["norm_ffn_pre"], i)
        arrived = finish_gather(i, "rest", m)
        if i + 1 < depth:
            cast_group(i + 1, "mix", arrived)
            tok_mix = start_gather(i + 1, "mix", None)
            cast_group(i + 1, "rest", tok_mix)
            g_post = after_token(g_post, tok_mix, start_gather(i + 1, "rest", None))
        h1, fin = rows_fwd("mix_post_ffn_pre", f_post_pre, [h, m], [g_post, g_fpre], [F32, BF16])
        gate, up, act = ffn_gate_up("ffn_gate_up", fin, G["ffn_w_gate"], G["ffn_w_up"], i)
        f = mm_fwd("ffn_down", act, G["ffn_w_down"], i, "row")
        g_fpost = row(W["norm_ffn_post"], i)
        h2 = rows_fwd("ffn_post", f_post, [h1, f], [g_fpost], [F32])[0]
        e = mm_fwd("ple_proj", p3, G["ple_w_proj"], i, "col", xl=i)
        zg = mm_fwd("ple_gate", h2, G["ple_w_gate"], i, "row")
        g_ple = row(W["ple_norm"], i)
        sv.update(m=m, h1=h1, fin=fin, gate=gate, up=up, act=act, f=f, h2=h2, e=e, zg=zg,
                  g_post=g_post, g_fpre=g_fpre, g_fpost=g_fpost, g_ple=g_ple)
        if i + 1 < depth:
            g_next = row(W["norm_mix_pre"], i + 1)
            h, a = rows_fwd("ple_next_pre", f_ple_pre, [h2, e, zg], [g_ple, g_next], [F32, BF16])
            sv["g_next"] = g_next
        else:
            h = rows_fwd("ple_last", f_ple, [h2, e, zg], [g_ple], [F32])[0]
        saved.append(sv)

    loss_part, dh = loss_and_grad(h, tgt)
    loss = lax.psum(loss_part[0, 0], ("x", "y", "c"))

    sg = {k: [None] * W[k].shape[0] for k in ("norm_mix_pre", "norm_mix_post", "norm_ffn_pre", "norm_ffn_post", "ple_norm",
                                              "hg_out_norm", "gm_ln_g", "gm_ln_b", "gm_w_s", "gm_b_s")}
    d_lbp = [None] * (3 * n_hg)
    da_next = None
    GRAD = {k: lax.empty(W[k].shape, F32) for k in BIG}
    scattering = {}

    def start_scatter(i, part):
        dws = [DW[k][l] for k, l in groups[i, part]]
        lands = [lax.empty((N_DEV - 1, 1, g.shape[2] // 2, g.shape[3]), BF16) for g in dws]
        send, recv, arrs, tok = split_start("scatter_start_%d_%s" % (i, part), scatter_build, dws + lands, n_peers=N_DEV - 1)
        scattering[i, part] = (send, recv, arrs)
        return tok

    def finish_scatter(i, part, after):
        send, recv, arrs = scattering.pop((i, part))
        arrs = split_wait("scatter_wait_%d_%s" % (i, part), scatter_build, send, recv, arrs, after)
        n = len(groups[i, part])
        for (k, l), own, ld in zip(groups[i, part], arrs[:n], arrs[n:]):
            GRAD[k] = sum_partials(own, ld, GRAD[k], l, q_arr, c_arr)

    tok = None
    for i in reversed(range(depth)):
        j = i // 2
        sv = saved[i]
        if i + 1 < depth:
            g_ple_after = sv["g_ple"] + tok[0:1, 0:1]
            dh2, de, dzg, d_gple, d_gnext = rows_bwd("ple_next_pre_bwd", f_ple_pre, [sv["h2"], sv["e"], sv["zg"]],
                                                     [g_ple_after, sv["g_next"]], [dh, da_next], [F32, BF16, BF16])
            sg["norm_mix_pre"][i + 1] = d_gnext
        else:
            dh2, de, dzg, d_gple = rows_bwd("ple_last_bwd", f_ple, [sv["h2"], sv["e"], sv["zg"]], [sv["g_ple"]], [dh],
                                            [F32, BF16, BF16])
        sg["ple_norm"][i] = d_gple
        DW["ple_w_proj"] = mm_bwd_w("ple_proj_dw", p3, de, DW["ple_w_proj"], i, "col", xl=i)
        DW["ple_w_gate"] = mm_bwd_w("ple_gate_dw", sv["h2"], dzg, DW["ple_w_gate"], i, "row")
        dh2 = mm_bwd_x("ple_gate_dx", dzg, G["ple_w_gate"], i, "row", addend=dh2)
        dh1, df, d_gfpost = rows_bwd("ffn_post_bwd", f_post, [sv["h1"], sv["f"]], [sv["g_fpost"]], [dh2], [F32, BF16])
        sg["norm_ffn_post"][i] = d_gfpost
        dgate, dup = ffn_down_dx("ffn_down_dx", df, G["ffn_w_down"], i, sv["gate"], sv["up"])
        DW["ffn_w_down"] = mm_bwd_w("ffn_down_dw", sv["act"], df, DW["ffn_w_down"], i, "row")
        dfin = ffn_in_dx("ffn_in_dx", dgate, dup, G["ffn_w_gate"], G["ffn_w_up"], i)
        DW["ffn_w_gate"] = mm_bwd_w("ffn_gate_dw", sv["fin"], dgate, DW["ffn_w_gate"], i, "col")
        DW["ffn_w_up"] = mm_bwd_w("ffn_up_dw", sv["fin"], dup, DW["ffn_w_up"], i, "col")
        g_post_after = after_token(sv["g_post"], start_scatter(i, "rest"))
        dh, dm, d_gpost, d_gfpre = rows_bwd("mix_post_ffn_pre_bwd", f_post_pre, [sv["h"], sv["m"]],
                                            [g_post_after, sv["g_fpre"]], [dh1, dfin], [F32, BF16])
        sg["norm_mix_post"][i], sg["norm_ffn_pre"][i] = d_gpost, d_gfpre
        if i % 2 == 0:
            dog = mm_bwd_x("hg_out_dx", dm, G["hg_w_out"], j, "row")
            DW["hg_w_out"] = mm_bwd_w("hg_out_dw", sv["og"], dm, DW["hg_w_out"], j, "row")
            dproj4, d0, d1, d2, d_on = hgrn_bwd("hgrn_bwd", sv["proj4"], sv["states"], dog, *sv["lbp"], sv["onorm"], n_seq)
            d_lbp[3 * j:3 * j + 3] = [d0, d1, d2]
            sg["hg_out_norm"][j] = d_on
            da_next = mm_bwd_x("hg_in_dx", dproj4, G["hg_w_in"], j, "col", parts=True)
            DW["hg_w_in"] = mm_bwd_w("hg_in_dw", sv["a"], dproj4, DW["hg_w_in"], j, "col", parts=True)
        else:
            dy = mm_bwd_x("gm_out_dx", dm, G["gm_w_out"], j, "row")
            DW["gm_w_out"] = mm_bwd_w("gm_out_dw", sv["y"], dm, DW["gm_w_out"], j, "row")
            gb = sv["sp_ins"][0][1][0]
            du, dvn, dws, dbs = bmap_bwd("gm_spatial_bwd", f_gm_spatial, sv["sp_grid"], sv["sp_ins"],
                                         [(dy, (gb, LANES), lambda g, n: (n, g))],
                                         [(0, F32, None), (1, F32, None), (2, F32, 1), (3, F32, 1)])
            sg["gm_w_s"][j], sg["gm_b_s"][j] = dws, dbs[:, :, 0]
            dz, d_lg, d_lb = rows_bwd("gm_gelu_ln_bwd", f_gm_in, [sv["z"]], [sv["lg"], sv["lb_"]], [du, dvn], [BF16], tm=128)
            sg["gm_ln_g"][j], sg["gm_ln_b"][j] = d_lg, d_lb
            da_next = mm_bwd_x("gm_in_dx", dz, G["gm_w_in"], j, "col")
            DW["gm_w_in"] = mm_bwd_w("gm_in_dw", sv["a"], dz, DW["gm_w_in"], j, "col")
        tok = start_scatter(i, "mix")
        if i + 1 < depth:
            finish_scatter(i + 1, "rest", da_next)
            finish_scatter(i + 1, "mix", da_next)
    finish_scatter(0, "rest", tok)
    g0 = after_token(row(W["norm_mix_pre"], 0), tok)
    grad_x, d_g0 = rows_bwd("prenorm_bwd", f_prenorm_thru, [saved[0]["h"]], [g0], [da_next, dh], [F32])
    sg["norm_mix_pre"][0] = d_g0
    d_logits = bmap_bwd("hg_lower_bounds_bwd", f_lb, (1,), lb_ins, [(d, one, lambda i: (0, 0)) for d in d_lbp],
                        [(jj, F32, None) for jj in range(n_hg)])

    small_g = {k: jnp.stack([v.reshape(W[k].shape[1:] if k not in ("gm_ln_g", "gm_ln_b") else (D,)) for v in sg[k]])
               for k in sg}
    small_g["hg_lb_logits"] = jnp.concatenate(d_logits, axis=0)
    small_shapes = [small_g[k].shape for k in SMALL]
    packed_g = _pack([small_g[k] for k in SMALL])
    slots = lax.dynamic_update_slice(jnp.zeros((N_DEV,) + packed_g.shape, F32), packed_g[None], (me, 0, 0))
    s_send, s_recv, slots, tok_s = split_start("share_small_start", share_build, [slots], n_peers=N_DEV - 1)

    out_g, out_d, out_m, out_v = {}, {}, {}, {}
    late = [k for k, _ in groups[0, "mix"]]
    early = [k for k in BIG if k not in late]
    ready = lax.optimization_barrier(tuple(GRAD[k] for k in early) + (tok_s,))[:-1]
    for k, g in zip(early, join_row_halves("join_early", list(ready))):
        out_d[k], out_m[k], out_v[k], out_g[k] = adamw("adamw_" + k, W[k], g, M[k], V[k], with_grad=True)
    slots = split_wait("share_small_wait", share_build, s_send, s_recv, slots, out_v[early[-1]])[0]
    red = sum_devices(slots)
    small_red = dict(zip(SMALL, _unpack(red, small_shapes)))
    for k in ("gm_ln_g", "gm_ln_b"):
        small_red[k] = lax.dynamic_slice_in_dim(small_red[k], q_me * dq, dq, axis=1)
    pk = lambda d: _pack([d[k] for k in SMALL])
    s_delta, s_m, s_v = adamw("adamw_small", pk(W), pk(small_red), pk(M), pk(V))
    shard_shapes = [W[k].shape for k in SMALL]
    out_g.update(small_red)
    for dct, packed in ((out_d, s_delta), (out_m, s_m), (out_v, s_v)):
        dct.update(zip(SMALL, _unpack(packed, shard_shapes)))

    finish_scatter(0, "mix", s_v)
    for k, g in zip(late, join_row_halves("join_late", [GRAD[k] for k in late])):
        out_d[k], out_m[k], out_v[k], out_g[k] = adamw("adamw_" + k, W[k], g, M[k], V[k], with_grad=True)

    outs = [loss, grad_x.reshape(x.shape)]
    for dct in (out_g, out_d, out_m, out_v):
        outs += [dct[k] for k in WEIGHTS]
    return tuple(outs)


def kernel(x, p, hg_w_in, hg_lb_logits, hg_out_norm, hg_w_out, gm_w_in, gm_ln_g, gm_ln_b, gm_w_s, gm_b_s, gm_w_out, norm_mix_pre, norm_mix_post, norm_ffn_pre, norm_ffn_post, ffn_w_gate, ffn_w_up, ffn_w_down, ple_w_proj, ple_w_gate, ple_norm, loss_target, m_hg_w_in, m_hg_lb_logits, m_hg_out_norm, m_hg_w_out, m_gm_w_in, m_gm_ln_g, m_gm_ln_b, m_gm_w_s, m_gm_b_s, m_gm_w_out, m_norm_mix_pre, m_norm_mix_post, m_norm_ffn_pre, m_norm_ffn_post, m_ffn_w_gate, m_ffn_w_up, m_ffn_w_down, m_ple_w_proj, m_ple_w_gate, m_ple_norm, v_hg_w_in, v_hg_lb_logits, v_hg_out_norm, v_hg_w_out, v_gm_w_in, v_gm_ln_g, v_gm_ln_b, v_gm_w_s, v_gm_b_s, v_gm_w_out, v_norm_mix_pre, v_norm_mix_post, v_norm_ffn_pre, v_norm_ffn_post, v_ffn_w_gate, v_ffn_w_up, v_ffn_w_down, v_ple_w_proj, v_ple_w_gate, v_ple_norm):
    W = dict(zip(WEIGHTS, (hg_w_in, hg_lb_logits, hg_out_norm, hg_w_out, gm_w_in, gm_ln_g, gm_ln_b, gm_w_s, gm_b_s, gm_w_out,
                           norm_mix_pre, norm_mix_post, norm_ffn_pre, norm_ffn_post, ffn_w_gate, ffn_w_up, ffn_w_down,
                           ple_w_proj, ple_w_gate, ple_norm)))
    M = dict(zip(WEIGHTS, (m_hg_w_in, m_hg_lb_logits, m_hg_out_norm, m_hg_w_out, m_gm_w_in, m_gm_ln_g, m_gm_ln_b, m_gm_w_s,
                           m_gm_b_s, m_gm_w_out, m_norm_mix_pre, m_norm_mix_post, m_norm_ffn_pre, m_norm_ffn_post,
                           m_ffn_w_gate, m_ffn_w_up, m_ffn_w_down, m_ple_w_proj, m_ple_w_gate, m_ple_norm)))
    V = dict(zip(WEIGHTS, (v_hg_w_in, v_hg_lb_logits, v_hg_out_norm, v_hg_w_out, v_gm_w_in, v_gm_ln_g, v_gm_ln_b, v_gm_w_s,
                           v_gm_b_s, v_gm_w_out, v_norm_mix_pre, v_norm_mix_post, v_norm_ffn_pre, v_norm_ffn_post,
                           v_ffn_w_gate, v_ffn_w_up, v_ffn_w_down, v_ple_w_proj, v_ple_w_gate, v_ple_norm)))
    return _step(x, p, W, M, V, loss_target)
```

```python
import functools

import jax
import jax.numpy as jnp
from jax import lax
from jax.experimental import pallas as pl
from jax.experimental.pallas import tpu as pltpu

F32 = jnp.float32
BF16 = jnp.bfloat16
MESH_ID = pl.DeviceIdType.MESH

LANES = 128
N_CHIPS = 4
N_DEV = 8
VMEM_LIMIT = 56 * 1024 * 1024
HG_SUB = 256
HG_BLOCK = 512
HG_HEADS_PER = 2
GM_CHUNK = 128
GM_BLOCK = 512
PACK_ROWS = 512
LB_FLOOR = 1e-30
EPS = 1e-6
ADAM_LR, ADAM_B1, ADAM_B2, ADAM_EPS, ADAM_WD, ADAM_STEP = 0.001, 0.9, 0.999, 1e-08, 0.01, 10


def _tile(n, pref, mult=LANES):
    if n <= pref:
        return n
    t = (pref // mult) * mult
    while t >= mult:
        if n % t == 0:
            return t
        t -= mult
    return n


def _cp(n_axes):
    return pltpu.CompilerParams(dimension_semantics=("arbitrary",) * n_axes, vmem_limit_bytes=VMEM_LIMIT)


def _dense(block):
    return tuple(b for b in block if b is not None)


def _bmap(name, grid, ins, outs, compute, scalars=(), into=None):
    n_s, n_in = len(scalars), len(ins)
    n_extra = 0 if into is None else 1

    def body(*refs):
        in_refs = refs[n_s:n_s + n_in]
        out_refs = refs[n_s + n_in + n_extra:]
        vals = [r[...] for r in in_refs]
        res = compute(*vals)
        for r, o, spec in zip(out_refs, res, outs):
            keep = spec[4]
            if keep is None:
                r[...] = o.astype(r.dtype)
            else:
                first = functools.reduce(jnp.logical_and, [pl.program_id(a) == 0 for a in range(keep, len(grid))])

                @pl.when(first)
                def _():
                    r[...] = jnp.zeros(r.shape, r.dtype)

                r[...] += o.astype(r.dtype)

    grid_spec = pltpu.PrefetchScalarGridSpec(
        num_scalar_prefetch=n_s, grid=grid,
        in_specs=[pl.BlockSpec(b, m) for _, b, m in ins] + [pl.BlockSpec(memory_space=pl.ANY)] * n_extra,
        out_specs=[pl.BlockSpec(o[2], o[3]) for o in outs])
    return pl.pallas_call(
        body, name=name, grid_spec=grid_spec,
        out_shape=[jax.ShapeDtypeStruct(o[0], o[1]) for o in outs],
        input_output_aliases={n_s + n_in: 0} if n_extra else {},
        compiler_params=_cp(len(grid)),
    )(*scalars, *[a for a, _, _ in ins], *([into] if n_extra else []))


def bmap_fwd(name, fn, grid, ins, outs, scalars=(), into=None):
    return _bmap(name, grid, ins, outs, lambda *v: fn(*[x.astype(F32) for x in v]), scalars, into)


def bmap_bwd(name, fn, grid, ins, cots, grads, scalars=()):
    n_in = len(ins)
    diff = [g[0] for g in grads]
    cot_ins = [c for c in cots if c is not None]

    def compute(*vals):
        xs = [v.astype(F32) for v in vals[:n_in]]
        cvals = list(vals[n_in:])

        def f(*d):
            full = list(xs)
            for i, dv in zip(diff, d):
                full[i] = dv
            return tuple(fn(*full))

        res, pull = jax.vjp(f, *[xs[i] for i in diff])
        cts = []
        for r, c in zip(res, cots):
            cts.append(jnp.zeros_like(r) if c is None else cvals.pop(0).astype(F32))
        return pull(tuple(cts))

    outs = [(ins[i][0].shape, dt, ins[i][1], ins[i][2], keep) for i, dt, keep in grads]
    return _bmap(name, grid, list(ins) + cot_ins, outs, compute, scalars)


def _mm(name, a, b, out_shape, out_dtype, grid, a_spec, b_spec, o_spec, dims, addend=None, alias_out=None):
    nk = grid[2]
    o_dense = _dense(o_spec[0])
    o_dense = (o_dense[0] * o_dense[1], o_dense[2]) if len(o_dense) == 3 else o_dense
    has_add = addend is not None
    has_alias = alias_out is not None

    def body(*refs):
        a_ref, b_ref = refs[0], refs[1]
        pos = 2
        c_ref = None
        if has_add:
            c_ref = refs[pos]
            pos += 1
        if has_alias:
            pos += 1
        o_ref = refs[pos]
        acc_ref = refs[pos + 1] if nk > 1 else None
        bv = b_ref[...]
        if bv.ndim == 3:
            bv = bv.reshape(bv.shape[0] * bv.shape[1], bv.shape[2])
        p = lax.dot_general(a_ref[...].astype(BF16), bv.astype(BF16), (dims, ((), ())), preferred_element_type=F32)

        def finish(total):
            if has_add:
                total = total + c_ref[...].astype(F32)
            o_ref[...] = total.reshape(o_ref.shape).astype(o_ref.dtype)

        if nk == 1:
            finish(p)
        else:
            k = pl.program_id(2)

            @pl.when(k == 0)
            def _():
                acc_ref[...] = p

            @pl.when(jnp.logical_and(k > 0, k < nk - 1))
            def _():
                acc_ref[...] += p

            @pl.when(k == nk - 1)
            def _():
                finish(acc_ref[...] + p)

    in_specs = [pl.BlockSpec(*a_spec), pl.BlockSpec(*b_spec)]
    operands = [a, b]
    if has_add:
        in_specs.append(pl.BlockSpec(o_spec[0], o_spec[1]))
        operands.append(addend)
    aliases = {}
    if has_alias:
        in_specs.append(pl.BlockSpec(memory_space=pl.ANY))
        aliases = {len(operands): 0}
        operands.append(alias_out)
    return pl.pallas_call(
        body, name=name, grid=grid, in_specs=in_specs, out_specs=pl.BlockSpec(*o_spec),
        out_shape=jax.ShapeDtypeStruct(out_shape, out_dtype),
        scratch_shapes=[pltpu.VMEM(o_dense, F32)] if nk > 1 else [],
        input_output_aliases=aliases,
        compiler_params=pltpu.CompilerParams(dimension_semantics=("parallel", "parallel", "arbitrary"),
                                             vmem_limit_bytes=VMEM_LIMIT),
    )(*operands)


NN, NT, TN = ((1,), (0,)), ((1,), (1,)), ((0,), (0,))
TM = 512
TT = 1024
TN_PREF = 1408
WHOLE_K = 2048


def mm_fwd(name, x, wg, l, kind, out_dtype=F32, parts=False, xl=None):
    if isinstance(wg, dict):
        wg, l = wg[l], 0
    _, _, R, C = wg.shape
    T = x.shape[-2]
    tm = _tile(T, TM, 8)
    if kind == "col":
        tn = _tile(C, TN_PREF)
        npc = C // tn
        grid = (T // tm, N_CHIPS * npc, 1)
        a_blk = (tm, R) if xl is None else (None, tm, R)
        a_map = (lambda i, j, k: (i, 0)) if xl is None else (lambda i, j, k: (xl, i, 0))
        b_spec = ((None, None, R, tn), lambda i, j, k: (j // npc, l, 0, j % npc))
        if parts:
            out_shape = (N_CHIPS, T, C)
            o_spec = ((None, tm, tn), lambda i, j, k: (j // npc, i, j % npc))
        else:
            out_shape = (T, N_CHIPS * C)
            o_spec = ((tm, tn), lambda i, j, k: (i, j))
    elif N_CHIPS * R <= WHOLE_K:
        tn = _tile(C, 1024)
        grid = (T // tm, C // tn, 1)
        a_blk = (tm, N_CHIPS * R)
        a_map = lambda i, j, k: (i, 0)
        b_spec = ((N_CHIPS, None, R, tn), lambda i, j, k: (0, l, 0, j))
        out_shape = (T, C)
        o_spec = ((tm, tn), lambda i, j, k: (i, j))
    else:
        tn = _tile(C, 1024)
        grid = (T // tm, C // tn, N_CHIPS)
        a_blk = (tm, R)
        a_map = lambda i, j, k: (i, k)
        b_spec = ((None, None, R, tn), lambda i, j, k: (k, l, 0, j))
        out_shape = (T, C)
        o_spec = ((tm, tn), lambda i, j, k: (i, j))
    return _mm(name, x, wg, out_shape, out_dtype, grid, (a_blk, a_map), b_spec, o_spec, NN)


def mm_bwd_x(name, dy, wg, l, kind, out_dtype=F32, parts=False, addend=None):
    if isinstance(wg, dict):
        wg, l = wg[l], 0
    _, _, R, C = wg.shape
    T = dy.shape[-2]
    tm = _tile(T, TM, 8)
    if kind == "col":
        tk = _tile(C, TN_PREF)
        npc = C // tk
        tno = _tile(R, 2048)
        grid = (T // tm, R // tno, N_CHIPS * npc)
        if parts:
            a_spec = ((None, tm, tk), lambda i, j, k: (k // npc, i, k % npc))
        else:
            a_spec = ((tm, tk), lambda i, j, k: (i, k))
        b_spec = ((None, None, tno, tk), lambda i, j, k: (k // npc, l, j, k % npc))
        out_shape = (T, R)
        o_spec = ((tm, tno), lambda i, j, k: (i, j))
    elif N_CHIPS * R <= WHOLE_K:
        grid = (T // tm, 1, 1)
        a_spec = ((tm, C), lambda i, j, k: (i, 0))
        b_spec = ((N_CHIPS, None, R, C), lambda i, j, k: (0, l, 0, 0))
        out_shape = (T, N_CHIPS * R)
        o_spec = ((tm, N_CHIPS * R), lambda i, j, k: (i, 0))
    else:
        grid = (T // tm, N_CHIPS, 1)
        a_spec = ((tm, C), lambda i, j, k: (i, 0))
        b_spec = ((None, None, R, C), lambda i, j, k: (j, l, 0, 0))
        out_shape = (T, N_CHIPS * R)
        o_spec = ((tm, R), lambda i, j, k: (i, j))
    return _mm(name, dy, wg, out_shape, out_dtype, grid, a_spec, b_spec, o_spec, NT, addend=addend)


def mm_bwd_w(name, x, dy, dwg, l, kind, parts=False, xl=None):
    if isinstance(dwg, dict):
        return {**dwg, l: mm_bwd_w(name, x, dy, dwg[l], 0, kind, parts=parts, xl=xl)}
    _, _, R, C = dwg.shape
    T = dy.shape[-2]
    tt = _tile(T, TT, 16)
    nt = T // tt
    if kind == "col":
        tn = _tile(C, TN_PREF)
        npc = C // tn
        tr = _tile(R, 1024)
        grid = (R // tr, N_CHIPS * npc, nt)
        if xl is None:
            a_spec = ((tt, tr), lambda i, j, t: (t, i))
        else:
            a_spec = ((None, tt, tr), lambda i, j, t: (xl, t, i))
        if parts:
            b_spec = ((None, tt, tn), lambda i, j, t: (j // npc, t, j % npc))
        else:
            b_spec = ((tt, tn), lambda i, j, t: (t, j))
        o_spec = ((None, None, tr, tn), lambda i, j, t: (j // npc, l, i, j % npc))
    elif N_CHIPS * R <= WHOLE_K:
        tn = _tile(C, 1024)
        grid = (1, C // tn, nt)
        a_spec = ((tt, N_CHIPS * R), lambda i, j, t: (t, 0))
        b_spec = ((tt, tn), lambda i, j, t: (t, j))
        o_spec = ((N_CHIPS, None, R, tn), lambda i, j, t: (0, l, 0, j))
    else:
        tn = _tile(C, 1024)
        grid = (N_CHIPS, C // tn, nt)
        a_spec = ((tt, R), lambda i, j, t: (t, i))
        b_spec = ((tt, tn), lambda i, j, t: (t, j))
        o_spec = ((None, None, R, tn), lambda i, j, t: (i, l, 0, j))
    return _mm(name, x, dy, dwg.shape, dwg.dtype, grid, a_spec, b_spec, o_spec, TN, alias_out=dwg)


def _sigmoid(x):
    return 0.5 * jnp.tanh(0.5 * x) + 0.5


def f_swiglu(gate, up):
    return (gate * _sigmoid(gate) * up,)


def ffn_gate_up(name, x, wg_gate, wg_up, l):
    if isinstance(wg_gate, dict):
        wg_gate, wg_up, l = wg_gate[l], wg_up[l], 0
    _, _, R, C = wg_gate.shape
    T = x.shape[0]
    tm = _tile(T, TM, 8)
    tn = _tile(C, TN_PREF)
    npc = C // tn

    def body(x_ref, g_ref, u_ref, gate_ref, up_ref, act_ref):
        xv = x_ref[...].astype(BF16)
        gate = jnp.dot(xv, g_ref[...], preferred_element_type=F32)
        up = jnp.dot(xv, u_ref[...], preferred_element_type=F32)
        gate_ref[...] = gate
        up_ref[...] = up
        act_ref[...] = f_swiglu(gate, up)[0].astype(act_ref.dtype)

    w_spec = pl.BlockSpec((None, None, R, tn), lambda i, j: (j // npc, l, 0, j % npc))
    o_spec = pl.BlockSpec((tm, tn), lambda i, j: (i, j))
    N = N_CHIPS * C
    return pl.pallas_call(
        body, name=name, grid=(T // tm, N_CHIPS * npc),
        in_specs=[pl.BlockSpec((tm, R), lambda i, j: (i, 0)), w_spec, w_spec], out_specs=[o_spec, o_spec, o_spec],
        out_shape=[jax.ShapeDtypeStruct((T, N), F32), jax.ShapeDtypeStruct((T, N), F32), jax.ShapeDtypeStruct((T, N), BF16)],
        compiler_params=_cp(2),
    )(x, wg_gate, wg_up)


def ffn_in_dx(name, dgate, dup, wg_gate, wg_up, l):
    if isinstance(wg_gate, dict):
        wg_gate, wg_up, l = wg_gate[l], wg_up[l], 0
    _, _, R, C = wg_gate.shape
    T = dgate.shape[0]
    tm = _tile(T, TM, 8)
    tk = _tile(C, TN_PREF)
    npc = C // tk
    nk = N_CHIPS * npc

    def body(dg_ref, du_ref, wg_ref, wu_ref, o_ref, acc):
        k = pl.program_id(1)

        @pl.when(k == 0)
        def _():
            acc[...] = jnp.zeros(acc.shape, F32)

        @pl.when(k < nk)
        def _():
            acc[...] += lax.dot_general(dg_ref[...], wg_ref[...], (NT, ((), ())), preferred_element_type=F32)

        @pl.when(k >= nk)
        def _():
            acc[...] += lax.dot_general(du_ref[...], wu_ref[...], (NT, ((), ())), preferred_element_type=F32)

        @pl.when(k == 2 * nk - 1)
        def _():
            o_ref[...] = acc[...]

    first = lambda k: jnp.minimum(k, nk - 1)
    second = lambda k: jnp.maximum(k - nk, 0)
    return pl.pallas_call(
        body, name=name, grid=(T // tm, 2 * nk),
        in_specs=[pl.BlockSpec((tm, tk), lambda i, k: (i, first(k))), pl.BlockSpec((tm, tk), lambda i, k: (i, second(k))),
                  pl.BlockSpec((None, None, R, tk), lambda i, k: (first(k) // npc, l, 0, first(k) % npc)),
                  pl.BlockSpec((None, None, R, tk), lambda i, k: (second(k) // npc, l, 0, second(k) % npc))],
        out_specs=pl.BlockSpec((tm, R), lambda i, k: (i, 0)),
        out_shape=jax.ShapeDtypeStruct((T, R), F32),
        scratch_shapes=[pltpu.VMEM((tm, R), F32)],
        compiler_params=_cp(2),
    )(dgate, dup, wg_gate, wg_up)


def ffn_down_dx(name, df, wg_down, l, gate, up):
    if isinstance(wg_down, dict):
        wg_down, l = wg_down[l], 0
    _, _, R, C = wg_down.shape
    T = df.shape[0]
    tm = _tile(T, TM, 8)

    def body(df_ref, w_ref, gate_ref, up_ref, dg_ref, du_ref):
        dact = lax.dot_general(df_ref[...].astype(BF16), w_ref[...], (NT, ((), ())), preferred_element_type=F32)
        _, pull = jax.vjp(lambda g, u: f_swiglu(g, u)[0], gate_ref[...], up_ref[...])
        dg, du = pull(dact)
        dg_ref[...] = dg.astype(dg_ref.dtype)
        du_ref[...] = du.astype(du_ref.dtype)

    t_spec = pl.BlockSpec((tm, R), lambda i, j: (i, j))
    return pl.pallas_call(
        body, name=name, grid=(T // tm, N_CHIPS),
        in_specs=[pl.BlockSpec((tm, C), lambda i, j: (i, 0)), pl.BlockSpec((None, None, R, C), lambda i, j: (j, l, 0, 0)),
                  t_spec, t_spec],
        out_specs=[t_spec, t_spec],
        out_shape=[jax.ShapeDtypeStruct((T, N_CHIPS * R), BF16)] * 2,
        compiler_params=_cp(2),
    )(df, wg_down, gate, up)


def _rms(x, g):
    return x * lax.rsqrt(jnp.mean(x * x, axis=-1, keepdims=True) + EPS) * g


def f_prenorm(h, g):
    return (_rms(h, g),)


def f_prenorm_thru(h, g):
    return _rms(h, g), h


def f_post_pre(h, m, g_post, g_pre):
    h1 = h + _rms(m, g_post)
    return h1, _rms(h1, g_pre)


def f_post(h1, f, g):
    return (h1 + _rms(f, g),)


def f_ple(h2, e, zg, g):
    return (h2 + _rms(e * _sigmoid(zg), g),)


def f_ple_pre(h2, e, zg, g, g_next):
    h3 = h2 + _rms(e * _sigmoid(zg), g)
    return h3, _rms(h3, g_next)


def _gelu(x):
    return 0.5 * x * (1.0 + lax.erf(x * 0.7071067811865476))


def f_gm_in(z, ln_g, ln_b):
    w = z.shape[-1] // 2
    u = _gelu(z[:, :w])
    v = _gelu(z[:, w:])
    mu = jnp.mean(v, axis=-1, keepdims=True)
    vc = v - mu
    vn = vc * lax.rsqrt(jnp.mean(vc * vc, axis=-1, keepdims=True) + EPS) * ln_g + ln_b
    return u, vn


def f_gm_spatial(u, vn, ws, bs):
    t = lax.broadcasted_iota(jnp.int32, ws.shape, 0)
    s = lax.broadcasted_iota(jnp.int32, ws.shape, 1)
    wm = jnp.where(t >= s, ws, 0.0).astype(BF16)
    ys = []
    for n in range(u.shape[0] // GM_CHUNK):
        rows = slice(n * GM_CHUNK, (n + 1) * GM_CHUNK)
        sv = jnp.dot(wm, vn[rows].astype(BF16), preferred_element_type=F32) + bs
        ys.append(u[rows] * sv)
    return (jnp.concatenate(ys, axis=0) if len(ys) > 1 else ys[0],)


def f_adam(w, g, m, v):
    m = ADAM_B1 * m + (1.0 - ADAM_B1) * g
    v = ADAM_B2 * v + (1.0 - ADAM_B2) * jnp.square(g)
    m_hat = m / (1.0 - ADAM_B1 ** ADAM_STEP)
    v_hat = v / (1.0 - ADAM_B2 ** ADAM_STEP)
    delta = -ADAM_LR * (m_hat / (jnp.sqrt(v_hat) + ADAM_EPS) + ADAM_WD * w)
    return delta, m, v


def _make_f_lb(n_layers):
    def f_lb(*logits):
        mx = functools.reduce(jnp.maximum, logits)
        ex = [jnp.exp(r - mx) for r in logits]
        tot = functools.reduce(lambda a, b: a + b, ex)
        sm = [e / tot for e in ex]
        outs = []
        run = jnp.zeros_like(sm[0])
        for j in range(n_layers):
            if j > 0:
                run = run + sm[j]
            lb = run
            outs += [jnp.log(jnp.maximum(lb, LB_FLOOR)), jnp.log(1.0 - lb), 1.0 - lb]
        return tuple(outs)
    return f_lb


def rows_fwd(name, fn, rows, params, out_dtypes, tm=256):
    T = rows[0].shape[0]
    tm = _tile(T, tm, 16)
    ins = [(r, (tm, r.shape[1]), lambda i: (i, 0)) for r in rows]
    ins += [(p, p.shape, lambda i: (0, 0)) for p in params]
    shapes = jax.eval_shape(lambda *a: fn(*a), *[jax.ShapeDtypeStruct((tm, r.shape[1]), F32) for r in rows],
                            *[jax.ShapeDtypeStruct(p.shape, F32) for p in params])
    outs = [((T, s.shape[1]), dt, (tm, s.shape[1]), lambda i: (i, 0), None) for s, dt in zip(shapes, out_dtypes)]
    return bmap_fwd(name, fn, (T // tm,), ins, outs)


def rows_bwd(name, fn, rows, params, cots, row_grad_dtypes, tm=256):
    T = rows[0].shape[0]
    tm = _tile(T, tm, 16)
    ins = [(r, (tm, r.shape[1]), lambda i: (i, 0)) for r in rows]
    ins += [(p, p.shape, lambda i: (0, 0)) for p in params]
    cts = [None if c is None else (c, (tm, c.shape[1]), lambda i: (i, 0)) for c in cots]
    grads = [(i, dt, None) for i, dt in enumerate(row_grad_dtypes) if dt is not None]
    grads += [(len(rows) + j, F32, 0) for j in range(len(params))]
    return bmap_bwd(name, fn, (T // tm,), ins, cts, grads)


def _log_sigmoid(z):
    return jnp.minimum(z, 0.0) - jnp.log(1.0 + jnp.exp(-jnp.abs(z)))


def _hg_gates(zf, ll0, ll1, oml):
    x2 = ll1 + _log_sigmoid(zf)
    mx = jnp.maximum(ll0, x2)
    g = mx + jnp.log(jnp.exp(ll0 - mx) + jnp.exp(x2 - mx))
    return g, oml * _sigmoid(-zf)


def hg_constants(n):
    levels = n.bit_length() - 1
    r = jnp.arange(n, dtype=jnp.int32)
    bounds = [r] + [((r >> (s + 1)) << (s + 1)) + ((1 << s) - 1) for s in range(levels)]
    sel = jnp.concatenate([(r[None, :] <= bd[:, None]) for bd in bounds], axis=0).astype(BF16)
    later = jnp.stack([((r >> s) & 1) for s in range(levels)])
    sign = jnp.broadcast_to((2 * later - 1).astype(F32)[:, :, None], (levels, n, LANES))
    pair = jnp.stack([((r[:, None] >> (s + 1)) == (r[None, :] >> (s + 1))) & (later[s][:, None] == 1) & (later[s][None, :] == 0)
                      for s in range(levels)]).astype(F32)
    return sel, sel.T, sign, pair


def _dot2(m, x):
    hi = x.astype(BF16)
    lo = (x - hi.astype(F32)).astype(BF16)
    p = jnp.dot(m, jnp.concatenate([hi, lo], axis=1), preferred_element_type=F32)
    w = x.shape[1]
    return p[:, :w] + p[:, w:]


@jax.custom_vjp
def _sel_dot(sel, selt, g):
    return _dot2(sel, g)


def _sel_dot_fwd(sel, selt, g):
    return _dot2(sel, g), (sel, selt)


def _sel_dot_bwd(res, d):
    sel, selt = res
    return jnp.zeros_like(sel), jnp.zeros_like(selt), _dot2(selt, d)


_sel_dot.defvjp(_sel_dot_fwd, _sel_dot_bwd)


def _hg_state(st, zf, zi, ll0, ll1, oml, tri):
    g, k = _hg_gates(zf, ll0, ll1, oml)
    b = _dot2(tri, g)
    tot = jnp.sum(g, axis=0, keepdims=True)
    kd = k * jnp.exp(tot - b)
    return st * jnp.exp(tot) + jnp.dot(zi.T.astype(BF16), kd.astype(BF16), preferred_element_type=F32)


def _hg_step(st, zq, zf, zi, zg, ll0, ll1, oml, onorm, sel, selt, sign, pair):
    n = zq.shape[0]
    levels = n.bit_length() - 1
    q = zq * _sigmoid(zq)
    g, k = _hg_gates(zf, ll0, ll1, oml)
    sums = _sel_dot(sel, selt, g)
    b = sums[:n]
    tot = jnp.sum(g, axis=0, keepdims=True)
    o = lax.dot_general((q * jnp.exp(b)).astype(BF16), st.astype(BF16), (NT, ((), ())), preferred_element_type=F32)
    a = jnp.zeros((n, n), F32)
    for s in range(levels):
        e = jnp.exp(sign[s] * (b - sums[(s + 1) * n:(s + 2) * n]))
        al = lax.dot_general((q * e).astype(BF16), (k * e).astype(BF16), (NT, ((), ())), preferred_element_type=F32)
        a = a + pair[s] * al
    o = o + jnp.dot(a.astype(BF16), zi.astype(BF16), preferred_element_type=F32)
    o = o + jnp.sum(q * k, axis=1, keepdims=True) * zi
    kd = k * jnp.exp(tot - b)
    st_new = st * jnp.exp(tot) + jnp.dot(zi.T.astype(BF16), kd.astype(BF16), preferred_element_type=F32)
    og = _rms(o, onorm) * (zg * _sigmoid(zg))
    return og, st_new


def _whole(arr, n_grid):
    zeros = (0,) * arr.ndim
    return pl.BlockSpec(arr.shape, (lambda h, n: zeros) if n_grid == 2 else (lambda i: zeros))


def _hg_dims(proj4, n_seq):
    _, T, D = proj4.shape
    S = T // n_seq
    hp = HG_HEADS_PER if (D // LANES) % HG_HEADS_PER == 0 else 1
    tb = min(HG_BLOCK, S)
    streams = [(b, hl) for b in range(n_seq) for hl in range(hp)]
    return T, D, S, hp, D // (LANES * hp), LANES * hp, tb, S // tb, tb // HG_SUB, streams


def hgrn_fwd(name, proj4, ll0, ll1, oml, onorm, n_seq):
    T, D, S, hp, n_hg, W, tb, nblk, nsub, streams = _hg_dims(proj4, n_seq)
    ns = len(streams)

    def body(p_ref, ll0_ref, ll1_ref, oml_ref, on_ref, sel_ref, selt_ref, later_ref, pair_ref, og_ref, st_ref, st):
        @pl.when(pl.program_id(1) == 0)
        def _():
            st[...] = jnp.zeros(st.shape, F32)

        st_ref[...] = st[...]
        on = on_ref[...]

        def step(j, carry):
            r = pl.ds(pl.multiple_of(j * HG_SUB, HG_SUB), HG_SUB)
            consts = (sel_ref[...], selt_ref[...], later_ref[...], pair_ref[...])
            args = []
            for si, (b, hl) in enumerate(streams):
                ln = slice(hl * LANES, (hl + 1) * LANES)
                args.append((st[si], p_ref[0, b, r, ln], p_ref[1, b, r, ln], p_ref[2, b, r, ln], p_ref[3, b, r, ln],
                             ll0_ref[:, ln], ll1_ref[:, ln], oml_ref[:, ln], on) + consts)
            res = [_hg_step(*a) for a in args]
            for si, (b, hl) in enumerate(streams):
                og_ref[b, r, hl * LANES:(hl + 1) * LANES] = res[si][0].astype(og_ref.dtype)
                st[si] = res[si][1]
            return carry

        lax.fori_loop(0, nsub, step, 0)

    vec = pl.BlockSpec((1, W), lambda h, n: (0, h))
    consts = hg_constants(HG_SUB)
    og, states = pl.pallas_call(
        body, name=name, grid=(n_hg, nblk),
        in_specs=[pl.BlockSpec((4, n_seq, tb, W), lambda h, n: (0, 0, n, h)), vec, vec, vec,
                  pl.BlockSpec((1, LANES), lambda h, n: (0, 0))] + [_whole(c, 2) for c in consts],
        out_specs=[pl.BlockSpec((n_seq, tb, W), lambda h, n: (0, n, h)),
                   pl.BlockSpec((None, None, ns, LANES, LANES), lambda h, n: (h, n, 0, 0, 0))],
        out_shape=[jax.ShapeDtypeStruct((n_seq, S, D), BF16),
                   jax.ShapeDtypeStruct((n_hg, nblk, ns, LANES, LANES), F32)],
        scratch_shapes=[pltpu.VMEM((ns, LANES, LANES), F32)],
        compiler_params=_cp(2),
    )(proj4.reshape(4, n_seq, S, D), ll0, ll1, oml, onorm, *consts)
    return og.reshape(T, D), states


def hgrn_bwd(name, proj4, states, dog, ll0, ll1, oml, onorm, n_seq):
    T, D, S, hp, n_hg, W, tb, nblk, nsub, streams = _hg_dims(proj4, n_seq)
    ns = len(streams)

    def body(p_ref, st_ref, dog_ref, ll0_ref, ll1_ref, oml_ref, on_ref, sel_ref, selt_ref, later_ref, pair_ref,
             dp_ref, dll0_ref, dll1_ref, doml_ref, don_ref, sbuf, dst):
        n_id = pl.program_id(1)

        @pl.when(n_id == 0)
        def _():
            dst[...] = jnp.zeros(dst.shape, F32)
            for ref in (dll0_ref, dll1_ref, doml_ref):
                ref[...] = jnp.zeros(ref.shape, F32)

        @pl.when(jnp.logical_and(n_id == 0, pl.program_id(0) == 0))
        def _():
            don_ref[...] = jnp.zeros(don_ref.shape, F32)

        on = on_ref[...]

        def fwd(j, carry):
            r = pl.ds(pl.multiple_of(j * HG_SUB, HG_SUB), HG_SUB)
            tri = sel_ref[0:HG_SUB, :]
            args = []
            for si, (b, hl) in enumerate(streams):
                ln = slice(hl * LANES, (hl + 1) * LANES)
                args.append((carry[si], p_ref[1, b, r, ln], p_ref[2, b, r, ln],
                             ll0_ref[:, ln], ll1_ref[:, ln], oml_ref[:, ln], tri))
            for si in range(ns):
                sbuf[si, j] = carry[si]
            return tuple(_hg_state(*a) for a in args)

        lax.fori_loop(0, nsub, fwd, tuple(st_ref[si] for si in range(ns)))

        def bwd(jj, carry):
            j = nsub - 1 - jj
            r = pl.ds(pl.multiple_of(j * HG_SUB, HG_SUB), HG_SUB)
            args, cts = [], []
            for si, (b, hl) in enumerate(streams):
                ln = slice(hl * LANES, (hl + 1) * LANES)
                args.append((sbuf[si, j], p_ref[0, b, r, ln], p_ref[1, b, r, ln], p_ref[2, b, r, ln],
                             p_ref[3, b, r, ln], ll0_ref[:, ln], ll1_ref[:, ln], oml_ref[:, ln], on))
                cts.append((dog_ref[b, r, ln].astype(F32), dst[si]))
            consts = (sel_ref[...], selt_ref[...], later_ref[...], pair_ref[...])
            step_fn = lambda *a: _hg_step(*a, *consts)
            ds = [jax.vjp(step_fn, *a)[1](ct) for a, ct in zip(args, cts)]
            d_on = carry
            for si, (b, hl) in enumerate(streams):
                ln = slice(hl * LANES, (hl + 1) * LANES)
                d = ds[si]
                dst[si] = d[0]
                for part in range(4):
                    dp_ref[part, b, r, ln] = d[1 + part].astype(dp_ref.dtype)
                dll0_ref[:, ln] += d[5]
                dll1_ref[:, ln] += d[6]
                doml_ref[:, ln] += d[7]
                d_on = d_on + d[8]
            return d_on

        don_ref[...] += lax.fori_loop(0, nsub, bwd, jnp.zeros((1, LANES), F32))

    last = nblk - 1
    vec = pl.BlockSpec((1, W), lambda h, n: (0, h))
    one = pl.BlockSpec((1, LANES), lambda h, n: (0, 0))
    consts = hg_constants(HG_SUB)
    dproj, d0, d1, d2, d_on = pl.pallas_call(
        body, name=name, grid=(n_hg, nblk),
        in_specs=[pl.BlockSpec((4, n_seq, tb, W), lambda h, n: (0, 0, last - n, h)),
                  pl.BlockSpec((None, None, ns, LANES, LANES), lambda h, n: (h, last - n, 0, 0, 0)),
                  pl.BlockSpec((n_seq, tb, W), lambda h, n: (0, last - n, h)), vec, vec, vec, one]
        + [_whole(c, 2) for c in consts],
        out_specs=[pl.BlockSpec((4, n_seq, tb, W), lambda h, n: (0, 0, last - n, h)), vec, vec, vec, one],
        out_shape=[jax.ShapeDtypeStruct((4, n_seq, S, D), BF16)] + [jax.ShapeDtypeStruct((1, D), F32)] * 3
        + [jax.ShapeDtypeStruct((1, LANES), F32)],
        scratch_shapes=[pltpu.VMEM((ns, nsub, LANES, LANES), F32), pltpu.VMEM((ns, LANES, LANES), F32)],
        compiler_params=_cp(2),
    )(proj4.reshape(4, n_seq, S, D), states, dog.reshape(n_seq, S, D), ll0, ll1, oml, onorm, *consts)
    return dproj.reshape(4, T, D), d0, d1, d2, d_on


def _place():
    x, y, c = lax.axis_index("x"), lax.axis_index("y"), lax.axis_index("c")
    chips = [(1 - x, y), (x, 1 - y), (1 - x, 1 - y)]
    return x, y, c, chips


ANY = pl.BlockSpec(memory_space=pl.ANY)


def _comm_call(name, body, ins, out_shapes, sems, aliases=None):
    return pl.pallas_call(
        body, name=name, in_specs=[ANY] * len(ins), out_specs=[ANY] * len(out_shapes),
        out_shape=out_shapes, scratch_shapes=sems, input_output_aliases=aliases or {},
        compiler_params=pltpu.CompilerParams(has_side_effects=True),
    )(*ins)


HBM_SPEC = pl.BlockSpec(memory_space=pltpu.HBM)
SEM_SPEC = pl.BlockSpec(memory_space=pltpu.SEMAPHORE)
SPLIT_EFFECT = pltpu.SideEffectType.DATAFLOW_SIDE_EFFECTING
N_PEER_CHIPS = 3


def split_start(name, build, arrays, n_peers=N_PEER_CHIPS):
    n = len(arrays)

    def body(*refs):
        send, recv = refs[n], refs[n + 1]
        token = refs[2 * n + 2]
        starts, _ = build(refs[:n], send, recv)
        for cp in starts:
            cp.start()
        token[...] = jnp.zeros_like(token)

    res = pl.pallas_call(
        body, name=name,
        out_shape=(pltpu.SemaphoreType.DMA((n_peers,)), pltpu.SemaphoreType.DMA((n_peers,)),
                   *[pltpu.HBM(a.shape, a.dtype) for a in arrays], jax.ShapeDtypeStruct((8, LANES), F32)),
        in_specs=[HBM_SPEC] * n,
        out_specs=(SEM_SPEC, SEM_SPEC, *[HBM_SPEC] * n, pl.BlockSpec(memory_space=pltpu.VMEM)),
        input_output_aliases={i: 2 + i for i in range(n)},
        compiler_params=pltpu.CompilerParams(has_side_effects=SPLIT_EFFECT),
    )(*[pltpu.with_memory_space_constraint(a, pltpu.HBM) for a in arrays])
    return res[0], res[1], list(res[2:2 + n]), res[2 + n]


def split_wait(name, build, send, recv, arrays, after):
    n = len(arrays)

    def body(*refs):
        starts, arrivals = build(refs[:n], refs[n], refs[n + 1])
        for cp in starts:
            cp.wait_send()
        for cp in arrivals:
            cp.wait_recv()

    return list(pl.pallas_call(
        body, name=name, out_shape=tuple(pltpu.HBM(a.shape, a.dtype) for a in arrays),
        in_specs=[HBM_SPEC] * n + [SEM_SPEC, SEM_SPEC, ANY], out_specs=tuple([HBM_SPEC] * n),
        input_output_aliases={i: i for i in range(n)},
        compiler_params=pltpu.CompilerParams(has_side_effects=SPLIT_EFFECT),
    )(*arrays, send, recv, after))


def _row_half(ref, dim, who):
    rh = ref.shape[dim] // 2
    return pl.ds(who * rh, rh)


def gather_build(refs, send, recv):
    x, y, c, chips = _place()
    q = 2 * x + y
    starts, arrivals = [], []
    for buf in refs:
        rows = _row_half(buf, 2, c)
        for j, (px, py) in enumerate(chips):
            mine, got = buf.at[q, :, rows], buf.at[2 * px + py, :, rows]
            starts.append(pltpu.make_async_remote_copy(src_ref=mine, dst_ref=mine, send_sem=send.at[j], recv_sem=recv.at[j],
                                                       device_id=(px, py, c), device_id_type=MESH_ID))
            arrivals.append(pltpu.make_async_remote_copy(src_ref=got, dst_ref=got, send_sem=send.at[j], recv_sem=recv.at[j],
                                                         device_id=(px, py, c), device_id_type=MESH_ID))
    return starts, arrivals


def gather_forward(name, bufs):
    n = len(bufs)

    def body(*refs):
        dst = refs[n:2 * n]
        send, recv = refs[2 * n:]
        x, y, c, chips = _place()
        sib = (x, y, 1 - c)
        cps = []
        for a in range(n):
            for j, (px, py) in enumerate(chips):
                got = dst[a].at[2 * px + py, :, _row_half(dst[a], 2, c)]
                cps.append(pltpu.make_async_remote_copy(src_ref=got, dst_ref=got, send_sem=send.at[a, j], recv_sem=recv.at[a, j],
                                                        device_id=sib, device_id_type=MESH_ID))
        for cp in cps:
            cp.start()
        for a in range(n):
            for j, (px, py) in enumerate(chips):
                theirs = dst[a].at[2 * px + py, :, _row_half(dst[a], 2, 1 - c)]
                pltpu.make_async_remote_copy(src_ref=theirs, dst_ref=theirs, send_sem=send.at[a, j], recv_sem=recv.at[a, j],
                                             device_id=sib, device_id_type=MESH_ID).wait_recv()
        for cp in cps:
            cp.wait_send()

    outs = [jax.ShapeDtypeStruct(s.shape, s.dtype) for s in bufs]
    sems = [pltpu.SemaphoreType.DMA((n, N_PEER_CHIPS))] * 2
    return _comm_call(name, body, bufs, outs, sems, aliases={a: a for a in range(n)})


def scatter_build(refs, send, recv):
    n = len(refs) // 2
    x, y, c, _ = _place()
    starts, arrivals = [], []
    for a in range(n):
        src, dst = refs[a], refs[n + a]
        for k in range(1, N_DEV):
            px, py, pc = x ^ (k >> 2), y ^ ((k >> 1) & 1), c ^ (k & 1)
            theirs = src.at[2 * px + py, :, _row_half(src, 2, pc)]
            starts.append(pltpu.make_async_remote_copy(src_ref=theirs, dst_ref=dst.at[k - 1], send_sem=send.at[k - 1],
                                                       recv_sem=recv.at[k - 1], device_id=(px, py, pc), device_id_type=MESH_ID))
            arrivals.append(pltpu.make_async_remote_copy(src_ref=dst.at[k - 1], dst_ref=dst.at[k - 1], send_sem=send.at[k - 1],
                                                         recv_sem=recv.at[k - 1], device_id=(px, py, pc), device_id_type=MESH_ID))
    return starts, arrivals


def share_build(refs, send, recv):
    dst = refs[0]
    x, y, c, _ = _place()
    mine = dst.at[4 * x + 2 * y + c]
    starts, arrivals = [], []
    for k in range(1, N_DEV):
        px, py, pc = x ^ (k >> 2), y ^ ((k >> 1) & 1), c ^ (k & 1)
        got = dst.at[4 * px + 2 * py + pc]
        starts.append(pltpu.make_async_remote_copy(src_ref=mine, dst_ref=mine, send_sem=send.at[k - 1], recv_sem=recv.at[k - 1],
                                                   device_id=(px, py, pc), device_id_type=MESH_ID))
        arrivals.append(pltpu.make_async_remote_copy(src_ref=got, dst_ref=got, send_sem=send.at[k - 1], recv_sem=recv.at[k - 1],
                                                     device_id=(px, py, pc), device_id_type=MESH_ID))
    return starts, arrivals


def join_row_halves(name, bufs):
    n = len(bufs)

    def body(*refs):
        dst = refs[n:2 * n]
        send, recv = refs[2 * n:]
        x, y, c, _ = _place()
        cps = []
        for a in range(n):
            mine = dst[a].at[:, _row_half(dst[a], 1, c)]
            cps.append(pltpu.make_async_remote_copy(src_ref=mine, dst_ref=mine, send_sem=send.at[a], recv_sem=recv.at[a],
                                                    device_id=(x, y, 1 - c), device_id_type=MESH_ID))
        for cp in cps:
            cp.start()
        for a in range(n):
            theirs = dst[a].at[:, _row_half(dst[a], 1, 1 - c)]
            pltpu.make_async_remote_copy(src_ref=theirs, dst_ref=theirs, send_sem=send.at[a], recv_sem=recv.at[a],
                                         device_id=(x, y, 1 - c), device_id_type=MESH_ID).wait_recv()
        for cp in cps:
            cp.wait_send()

    outs = [jax.ShapeDtypeStruct(b.shape, b.dtype) for b in bufs]
    sems = [pltpu.SemaphoreType.DMA((n,))] * 2
    return _comm_call(name, body, bufs, outs, sems, aliases={a: a for a in range(n)})


def share_with_all(name, packed, me):
    slots = lax.dynamic_update_slice(jnp.zeros((N_DEV,) + packed.shape, packed.dtype), packed[None], (me, 0, 0))

    def body(_, dst, send, recv):
        x, y, c, _ = _place()
        me = 4 * x + 2 * y + c
        cps = []
        for k in range(1, N_DEV):
            px, py, pc = x ^ (k >> 2), y ^ ((k >> 1) & 1), c ^ (k & 1)
            cps.append(pltpu.make_async_remote_copy(src_ref=dst.at[me], dst_ref=dst.at[me], send_sem=send.at[k - 1],
                                                    recv_sem=recv.at[k - 1], device_id=(px, py, pc), device_id_type=MESH_ID))
        for cp in cps:
            cp.start()
        for k in range(1, N_DEV):
            px, py, pc = x ^ (k >> 2), y ^ ((k >> 1) & 1), c ^ (k & 1)
            got = dst.at[4 * px + 2 * py + pc]
            pltpu.make_async_remote_copy(src_ref=got, dst_ref=got, send_sem=send.at[k - 1], recv_sem=recv.at[k - 1],
                                         device_id=(px, py, pc), device_id_type=MESH_ID).wait_recv()
        for cp in cps:
            cp.wait_send()

    outs = [jax.ShapeDtypeStruct(slots.shape, slots.dtype)]
    sems = [pltpu.SemaphoreType.DMA((N_DEV - 1,))] * 2
    return _comm_call(name, body, [slots], outs, sems, aliases={0: 0})[0]


def _w_tiles(R, C):
    return _tile(R, max(16, (1 << 20) // (4 * C) // 16 * 16), 16)


def cast_bf16(w, l, q_arr):
    _, R, C = w.shape
    tr = _w_tiles(R, C)
    ins = [(w, (None, tr, C), lambda r, q: (l, r, 0))]
    outs = [((N_CHIPS, 1, R, C), BF16, (None, None, tr, C), lambda r, q: (q[0], 0, r, 0), None)]
    return bmap_fwd("cast_bf16", lambda a: (a,), (R // tr,), ins, outs, scalars=(q_arr,))[0]


def sum_partials(own, landed, into, l, q_arr, c_arr):
    n_land, _, rh, C = landed.shape
    tr = _w_tiles(rh, C)
    nb = rh // tr
    blk = (None, None, tr, C)
    ins = [(own, blk, lambda r, q, c: (q[0], 0, c[0] * nb + r, 0))]
    ins += [(landed, blk, (lambda r, q, c, kk=kk: (kk, 0, r, 0))) for kk in range(n_land)]
    outs = [(into.shape, F32, (None, tr, C), lambda r, q, c: (l, c[0] * nb + r, 0), None)]
    return bmap_fwd("sum_partials", lambda *t: (functools.reduce(lambda u, v: u + v, t),), (nb,), ins, outs,
                    scalars=(q_arr, c_arr), into=into)[0]


def sum_devices(slots):
    nd, NR, C = slots.shape
    tr = _tile(NR, 512, 8)
    ins = [(slots, (None, tr, C), (lambda r, dd=dd: (dd, r, 0))) for dd in range(nd)]
    outs = [((NR, C), F32, (tr, C), lambda r: (r, 0), None)]
    return bmap_fwd("sum_devices", lambda *a: (functools.reduce(lambda u, v: u + v, a),), (NR // tr,), ins, outs)[0]


def adamw(name, w, g, m, v, with_grad=False):
    if w.ndim == 2:
        R, C = w.shape
        tr = _w_tiles(R, C)
        spec = ((tr, C), lambda r: (r, 0))
        grid = (R // tr,)
    else:
        L, R, C = w.shape
        tr = _w_tiles(R, C)
        spec = ((None, tr, C), lambda l, r: (l, r, 0))
        grid = (L, R // tr)
    ins = [(a,) + spec for a in (w, g, m, v)]
    outs = [(w.shape, F32) + spec + (None,)] * (4 if with_grad else 3)
    fn = (lambda a, b, c, d: f_adam(a, b, c, d) + (b,)) if with_grad else f_adam
    return bmap_fwd(name, fn, grid, ins, outs)


def loss_and_grad(h, target):
    T, D = h.shape
    tm = _tile(T, 256, 8)

    def fn(hv, tv):
        d = hv - tv
        return jnp.sum(d * d, keepdims=True).reshape(1, 1) * (0.5 / D), d * (1.0 / D)

    ins = [(h, (tm, D), lambda i: (i, 0)), (target, (tm, D), lambda i: (i, 0))]
    outs = [((1, 1), F32, (1, 1), lambda i: (0, 0), 0), ((T, D), F32, (tm, D), lambda i: (i, 0), None)]
    return bmap_fwd("loss_and_grad", fn, (T // tm,), ins, outs)


BIG = ("hg_w_in", "hg_w_out", "gm_w_in", "gm_w_out", "ffn_w_gate", "ffn_w_up", "ffn_w_down", "ple_w_proj", "ple_w_gate")
KIND = {"hg_w_in": "col", "hg_w_out": "row", "gm_w_in": "col", "gm_w_out": "row", "ffn_w_gate": "col",
        "ffn_w_up": "col", "ffn_w_down": "row", "ple_w_proj": "col", "ple_w_gate": "row"}
SMALL = ("hg_lb_logits", "hg_out_norm", "gm_ln_g", "gm_ln_b", "gm_w_s", "gm_b_s", "norm_mix_pre", "norm_mix_post",
         "norm_ffn_pre", "norm_ffn_post", "ple_norm")
WEIGHTS = ("hg_w_in", "hg_lb_logits", "hg_out_norm", "hg_w_out", "gm_w_in", "gm_ln_g", "gm_ln_b", "gm_w_s", "gm_b_s",
           "gm_w_out", "norm_mix_pre", "norm_mix_post", "norm_ffn_pre", "norm_ffn_post", "ffn_w_gate", "ffn_w_up",
           "ffn_w_down", "ple_w_proj", "ple_w_gate", "ple_norm")


def _pack(arrs):
    rows = []
    for a in arrs:
        flat = a.reshape(-1)
        pad = (-flat.shape[0]) % (8 * LANES)
        rows.append(jnp.pad(flat, (0, pad)).reshape(-1, LANES))
    n_rows = sum(r.shape[0] for r in rows)
    rows.append(jnp.zeros(((-n_rows) % PACK_ROWS, LANES), F32))
    return jnp.concatenate(rows, axis=0)


def _unpack(packed, shapes):
    out, r = [], 0
    for s in shapes:
        size = 1
        for d in s:
            size *= d
        nr = -(-size // (8 * LANES)) * 8
        out.append(packed[r:r + nr].reshape(-1)[:size].reshape(s))
        r += nr
    return out


def _step(x, p, W, M, V, loss_target):
    n_seq, S, D = x.shape
    T = n_seq * S
    depth = p.shape[0]
    n_hg = W["hg_w_in"].shape[0]
    x2 = x.reshape(T, D)
    p3 = p.reshape(depth, T, p.shape[-1])
    tgt = loss_target.reshape(T, D)
    xi, yi, ci = lax.axis_index("x"), lax.axis_index("y"), lax.axis_index("c")
    q_me = 2 * xi + yi
    c_arr = jnp.reshape(ci, (1,)).astype(jnp.int32)
    q_arr = jnp.reshape(q_me, (1,)).astype(jnp.int32)

    groups = {}
    for i in range(depth):
        mix = ("hg_w_in", "hg_w_out") if i % 2 == 0 else ("gm_w_in", "gm_w_out")
        groups[i, "mix"] = [(k, i // 2) for k in mix]
        groups[i, "rest"] = [(k, i) for k in ("ffn_w_gate", "ffn_w_up", "ffn_w_down", "ple_w_proj", "ple_w_gate")]
    G = {k: {} for k in BIG}
    DW = {k: {l: lax.empty((N_CHIPS, 1) + W[k].shape[1:], BF16) for l in range(W[k].shape[0])} for k in BIG}
    in_flight = {}

    casts = {}

    def cast_group(i, part, dep):
        qa = q_arr if dep is None else lax.optimization_barrier((q_arr, dep))[0]
        casts[i, part] = [cast_bf16(W[k], l, qa) for k, l in groups[i, part]]

    def start_gather(i, part, dep):
        bufs = casts.pop((i, part))
        if dep is not None:
            bufs = list(lax.optimization_barrier((tuple(bufs), dep))[0])
        send, recv, arrs, tok = split_start("gather_start_%d_%s" % (i, part), gather_build, bufs)
        in_flight[i, part] = (send, recv, arrs)
        return tok

    def finish_gather(i, part, after):
        send, recv, arrs = in_flight.pop((i, part))
        arrs = split_wait("gather_wait_%d_%s" % (i, part), gather_build, send, recv, arrs, after)
        for (k, l), buf in zip(groups[i, part], gather_forward("gather_forward_%d_%s" % (i, part), arrs)):
            G[k][l] = buf
        return buf

    def after_token(row_arr, *toks):
        return functools.reduce(lambda u, t: u + t[0:1, 0:1], toks, row_arr)

    cast_group(0, "mix", None)
    tok_mix = start_gather(0, "mix", None)
    cast_group(0, "rest", tok_mix)
    tok_rest = start_gather(0, "rest", None)
    finish_gather(0, "mix", tok_mix + tok_rest)
    me = 4 * xi + 2 * yi + ci
    ln_full = share_with_all("share_ln", _pack([W["gm_ln_g"], W["gm_ln_b"]]), me)
    n_gm, dq = W["gm_ln_g"].shape
    ln_parts = [_unpack(ln_full[4 * qx + 2 * qy + 0], [(n_gm, dq), (n_gm, dq)]) for qx in range(2) for qy in range(2)]
    ln_g = jnp.concatenate([lp[0] for lp in ln_parts], axis=1)
    ln_b = jnp.concatenate([lp[1] for lp in ln_parts], axis=1)

    row = lambda a, i: a[i][None, :]
    f_lb = _make_f_lb(n_hg)
    lb_rows = [row(W["hg_lb_logits"], j) for j in range(n_hg)]
    one = (1, D)
    lb_ins = [(r, one, lambda i: (0, 0)) for r in lb_rows]
    lb_out = bmap_fwd("hg_lower_bounds", f_lb, (1,), lb_ins, [(one, F32, one, lambda i: (0, 0), None)] * (3 * n_hg))

    saved = []
    h = x2
    a = rows_fwd("prenorm", f_prenorm, [h], [row(W["norm_mix_pre"], 0)], [BF16])[0]
    for i in range(depth):
        j = i // 2
        sv = {"h": h, "a": a}
        if i > 0:
            finish_gather(i, "mix", h)
        if i % 2 == 0:
            proj4 = mm_fwd("hg_in", a, G["hg_w_in"], j, "col", parts=True)
            lbp = lb_out[3 * j:3 * j + 3]
            onorm = row(W["hg_out_norm"], j)
            og, states = hgrn_fwd("hgrn_fwd", proj4, *lbp, onorm, n_seq)
            m = mm_fwd("hg_out", og, G["hg_w_out"], j, "row")
            sv.update(proj4=proj4, states=states, og=og, lbp=lbp, onorm=onorm)
        else:
            z = mm_fwd("gm_in", a, G["gm_w_in"], j, "col")
            lg, lb_ = row(ln_g, j), row(ln_b, j)
            u, vn = rows_fwd("gm_gelu_ln", f_gm_in, [z], [lg, lb_], [F32, BF16], tm=128)
            ws = W["gm_w_s"][j]
            bs = W["gm_b_s"][j][:, :, None]
            gb = min(GM_BLOCK, S)
            sp_grid = (D // LANES, T // gb)
            sp_ins = [(u, (gb, LANES), lambda g, n: (n, g)), (vn, (gb, LANES), lambda g, n: (n, g)),
                      (ws, (None, GM_CHUNK, GM_CHUNK), lambda g, n: (g, 0, 0)),
                      (bs, (None, GM_CHUNK, 1), lambda g, n: (g, 0, 0))]
            y = bmap_fwd("gm_spatial", f_gm_spatial, sp_grid, sp_ins,
                         [((T, D), BF16, (gb, LANES), lambda g, n: (n, g), None)])[0]
            m = mm_fwd("gm_out", y, G["gm_w_out"], j, "row")
            sv.update(z=z, lg=lg, lb_=lb_, sp_ins=sp_ins, sp_grid=sp_grid, y=y)
        g_post, g_fpre = row(W["norm_mix_post"], i), row(W["norm_ffn_pre"], i)
        arrived = finish_gather(i, "rest", m)
        if i + 1 < depth:
            cast_group(i + 1, "mix", arrived)
            tok_mix = start_gather(i + 1, "mix", None)
            cast_group(i + 1, "rest", tok_mix)
            g_post = after_token(g_post, tok_mix, start_gather(i + 1, "rest", None))
        h1, fin = rows_fwd("mix_post_ffn_pre", f_post_pre, [h, m], [g_post, g_fpre], [F32, BF16])
        gate, up, act = ffn_gate_up("ffn_gate_up", fin, G["ffn_w_gate"], G["ffn_w_up"], i)
        f = mm_fwd("ffn_down", act, G["ffn_w_down"], i, "row")
        g_fpost = row(W["norm_ffn_post"], i)
        h2 = rows_fwd("ffn_post", f_post, [h1, f], [g_fpost], [F32])[0]
        e = mm_fwd("ple_proj", p3, G["ple_w_proj"], i, "col", xl=i)
        zg = mm_fwd("ple_gate", h2, G["ple_w_gate"], i, "row")
        g_ple = row(W["ple_norm"], i)
        sv.update(m=m, h1=h1, fin=fin, gate=gate, up=up, act=act, f=f, h2=h2, e=e, zg=zg,
                  g_post=g_post, g_fpre=g_fpre, g_fpost=g_fpost, g_ple=g_ple)
        if i + 1 < depth:
            g_next = row(W["norm_mix_pre"], i + 1)
            h, a = rows_fwd("ple_next_pre", f_ple_pre, [h2, e, zg], [g_ple, g_next], [F32, BF16])
            sv["g_next"] = g_next
        else:
            h = rows_fwd("ple_last", f_ple, [h2, e, zg], [g_ple], [F32])[0]
        saved.append(sv)

    loss_part, dh = loss_and_grad(h, tgt)
    loss = lax.psum(loss_part[0, 0], ("x", "y", "c"))

    sg = {k: [None] * W[k].shape[0] for k in ("norm_mix_pre", "norm_mix_post", "norm_ffn_pre", "norm_ffn_post", "ple_norm",
                                              "hg_out_norm", "gm_ln_g", "gm_ln_b", "gm_w_s", "gm_b_s")}
    d_lbp = [None] * (3 * n_hg)
    da_next = None
    GRAD = {k: lax.empty(W[k].shape, F32) for k in BIG}
    scattering = {}

    def start_scatter(i, part):
        dws = [DW[k][l] for k, l in groups[i, part]]
        lands = [lax.empty((N_DEV - 1, 1, g.shape[2] // 2, g.shape[3]), BF16) for g in dws]
        send, recv, arrs, tok = split_start("scatter_start_%d_%s" % (i, part), scatter_build, dws + lands, n_peers=N_DEV - 1)
        scattering[i, part] = (send, recv, arrs)
        return tok

    def finish_scatter(i, part, after):
        send, recv, arrs = scattering.pop((i, part))
        arrs = split_wait("scatter_wait_%d_%s" % (i, part), scatter_build, send, recv, arrs, after)
        n = len(groups[i, part])
        for (k, l), own, ld in zip(groups[i, part], arrs[:n], arrs[n:]):
            GRAD[k] = sum_partials(own, ld, GRAD[k], l, q_arr, c_arr)

    tok = None
    for i in reversed(range(depth)):
        j = i // 2
        sv = saved[i]
        if i + 1 < depth:
            g_ple_after = sv["g_ple"] + tok[0:1, 0:1]
            dh2, de, dzg, d_gple, d_gnext = rows_bwd("ple_next_pre_bwd", f_ple_pre, [sv["h2"], sv["e"], sv["zg"]],
                                                     [g_ple_after, sv["g_next"]], [dh, da_next], [F32, BF16, BF16])
            sg["norm_mix_pre"][i + 1] = d_gnext
        else:
            dh2, de, dzg, d_gple = rows_bwd("ple_last_bwd", f_ple, [sv["h2"], sv["e"], sv["zg"]], [sv["g_ple"]], [dh],
                                            [F32, BF16, BF16])
        sg["ple_norm"][i] = d_gple
        DW["ple_w_proj"] = mm_bwd_w("ple_proj_dw", p3, de, DW["ple_w_proj"], i, "col", xl=i)
        DW["ple_w_gate"] = mm_bwd_w("ple_gate_dw", sv["h2"], dzg, DW["ple_w_gate"], i, "row")
        dh2 = mm_bwd_x("ple_gate_dx", dzg, G["ple_w_gate"], i, "row", addend=dh2)
        dh1, df, d_gfpost = rows_bwd("ffn_post_bwd", f_post, [sv["h1"], sv["f"]], [sv["g_fpost"]], [dh2], [F32, BF16])
        sg["norm_ffn_post"][i] = d_gfpost
        dgate, dup = ffn_down_dx("ffn_down_dx", df, G["ffn_w_down"], i, sv["gate"], sv["up"])
        DW["ffn_w_down"] = mm_bwd_w("ffn_down_dw", sv["act"], df, DW["ffn_w_down"], i, "row")
        dfin = ffn_in_dx("ffn_in_dx", dgate, dup, G["ffn_w_gate"], G["ffn_w_up"], i)
        DW["ffn_w_gate"] = mm_bwd_w("ffn_gate_dw", sv["fin"], dgate, DW["ffn_w_gate"], i, "col")
        DW["ffn_w_up"] = mm_bwd_w("ffn_up_dw", sv["fin"], dup, DW["ffn_w_up"], i, "col")
        g_post_after = after_token(sv["g_post"], start_scatter(i, "rest"))
        dh, dm, d_gpost, d_gfpre = rows_bwd("mix_post_ffn_pre_bwd", f_post_pre, [sv["h"], sv["m"]],
                                            [g_post_after, sv["g_fpre"]], [dh1, dfin], [F32, BF16])
        sg["norm_mix_post"][i], sg["norm_ffn_pre"][i] = d_gpost, d_gfpre
        if i % 2 == 0:
            dog = mm_bwd_x("hg_out_dx", dm, G["hg_w_out"], j, "row")
            DW["hg_w_out"] = mm_bwd_w("hg_out_dw", sv["og"], dm, DW["hg_w_out"], j, "row")
            dproj4, d0, d1, d2, d_on = hgrn_bwd("hgrn_bwd", sv["proj4"], sv["states"], dog, *sv["lbp"], sv["onorm"], n_seq)
            d_lbp[3 * j:3 * j + 3] = [d0, d1, d2]
            sg["hg_out_norm"][j] = d_on
            da_next = mm_bwd_x("hg_in_dx", dproj4, G["hg_w_in"], j, "col", parts=True)
            DW["hg_w_in"] = mm_bwd_w("hg_in_dw", sv["a"], dproj4, DW["hg_w_in"], j, "col", parts=True)
        else:
            dy = mm_bwd_x("gm_out_dx", dm, G["gm_w_out"], j, "row")
            DW["gm_w_out"] = mm_bwd_w("gm_out_dw", sv["y"], dm, DW["gm_w_out"], j, "row")
            gb = sv["sp_ins"][0][1][0]
            du, dvn, dws, dbs = bmap_bwd("gm_spatial_bwd", f_gm_spatial, sv["sp_grid"], sv["sp_ins"],
                                         [(dy, (gb, LANES), lambda g, n: (n, g))],
                                         [(0, F32, None), (1, F32, None), (2, F32, 1), (3, F32, 1)])
            sg["gm_w_s"][j], sg["gm_b_s"][j] = dws, dbs[:, :, 0]
            dz, d_lg, d_lb = rows_bwd("gm_gelu_ln_bwd", f_gm_in, [sv["z"]], [sv["lg"], sv["lb_"]], [du, dvn], [BF16], tm=128)
            sg["gm_ln_g"][j], sg["gm_ln_b"][j] = d_lg, d_lb
            da_next = mm_bwd_x("gm_in_dx", dz, G["gm_w_in"], j, "col")
            DW["gm_w_in"] = mm_bwd_w("gm_in_dw", sv["a"], dz, DW["gm_w_in"], j, "col")
        tok = start_scatter(i, "mix")
        if i + 1 < depth:
            finish_scatter(i + 1, "rest", da_next)
            finish_scatter(i + 1, "mix", da_next)
    finish_scatter(0, "rest", tok)
    g0 = after_token(row(W["norm_mix_pre"], 0), tok)
    grad_x, d_g0 = rows_bwd("prenorm_bwd", f_prenorm_thru, [saved[0]["h"]], [g0], [da_next, dh], [F32])
    sg["norm_mix_pre"][0] = d_g0
    d_logits = bmap_bwd("hg_lower_bounds_bwd", f_lb, (1,), lb_ins, [(d, one, lambda i: (0, 0)) for d in d_lbp],
                        [(jj, F32, None) for jj in range(n_hg)])

    small_g = {k: jnp.stack([v.reshape(W[k].shape[1:] if k not in ("gm_ln_g", "gm_ln_b") else (D,)) for v in sg[k]])
               for k in sg}
    small_g["hg_lb_logits"] = jnp.concatenate(d_logits, axis=0)
    small_shapes = [small_g[k].shape for k in SMALL]
    packed_g = _pack([small_g[k] for k in SMALL])
    slots = lax.dynamic_update_slice(jnp.zeros((N_DEV,) + packed_g.shape, F32), packed_g[None], (me, 0, 0))
    s_send, s_recv, slots, tok_s = split_start("share_small_start", share_build, [slots], n_peers=N_DEV - 1)

    out_g, out_d, out_m, out_v = {}, {}, {}, {}
    late = [k for k, _ in groups[0, "mix"]]
    early = [k for k in BIG if k not in late]
    ready = lax.optimization_barrier(tuple(GRAD[k] for k in early) + (tok_s,))[:-1]
    for k, g in zip(early, join_row_halves("join_early", list(ready))):
        out_d[k], out_m[k], out_v[k], out_g[k] = adamw("adamw_" + k, W[k], g, M[k], V[k], with_grad=True)
    slots = split_wait("share_small_wait", share_build, s_send, s_recv, slots, out_v[early[-1]])[0]
    red = sum_devices(slots)
    small_red = dict(zip(SMALL, _unpack(red, small_shapes)))
    for k in ("gm_ln_g", "gm_ln_b"):
        small_red[k] = lax.dynamic_slice_in_dim(small_red[k], q_me * dq, dq, axis=1)
    pk = lambda d: _pack([d[k] for k in SMALL])
    s_delta, s_m, s_v = adamw("adamw_small", pk(W), pk(small_red), pk(M), pk(V))
    shard_shapes = [W[k].shape for k in SMALL]
    out_g.update(small_red)
    for dct, packed in ((out_d, s_delta), (out_m, s_m), (out_v, s_v)):
        dct.update(zip(SMALL, _unpack(packed, shard_shapes)))

    finish_scatter(0, "mix", s_v)
    for k, g in zip(late, join_row_halves("join_late", [GRAD[k] for k in late])):
        out_d[k], out_m[k], out_v[k], out_g[k] = adamw("adamw_" + k, W[k], g, M[k], V[k], with_grad=True)

    outs = [loss, grad_x.reshape(x.shape)]
    for dct in (out_g, out_d, out_m, out_v):
        outs += [dct[k] for k in WEIGHTS]
    return tuple(outs)


def kernel(x, p, hg_w_in, hg_lb_logits, hg_out_norm, hg_w_out, gm_w_in, gm_ln_g, gm_ln_b, gm_w_s, gm_b_s, gm_w_out, norm_mix_pre, norm_mix_post, norm_ffn_pre, norm_ffn_post, ffn_w_gate, ffn_w_up, ffn_w_down, ple_w_proj, ple_w_gate, ple_norm, loss_target, m_hg_w_in, m_hg_lb_logits, m_hg_out_norm, m_hg_w_out, m_gm_w_in, m_gm_ln_g, m_gm_ln_b, m_gm_w_s, m_gm_b_s, m_gm_w_out, m_norm_mix_pre, m_norm_mix_post, m_norm_ffn_pre, m_norm_ffn_post, m_ffn_w_gate, m_ffn_w_up, m_ffn_w_down, m_ple_w_proj, m_ple_w_gate, m_ple_norm, v_hg_w_in, v_hg_lb_logits, v_hg_out_norm, v_hg_w_out, v_gm_w_in, v_gm_ln_g, v_gm_ln_b, v_gm_w_s, v_gm_b_s, v_gm_w_out, v_norm_mix_pre, v_norm_mix_post, v_norm_ffn_pre, v_norm_ffn_post, v_ffn_w_gate, v_ffn_w_up, v_ffn_w_down, v_ple_w_proj, v_ple_w_gate, v_ple_norm):
    W = dict(zip(WEIGHTS, (hg_w_in, hg_lb_logits, hg_out_norm, hg_w_out, gm_w_in, gm_ln_g, gm_ln_b, gm_w_s, gm_b_s, gm_w_out,
                           norm_mix_pre, norm_mix_post, norm_ffn_pre, norm_ffn_post, ffn_w_gate, ffn_w_up, ffn_w_down,
                           ple_w_proj, ple_w_gate, ple_norm)))
    M = dict(zip(WEIGHTS, (m_hg_w_in, m_hg_lb_logits, m_hg_out_norm, m_hg_w_out, m_gm_w_in, m_gm_ln_g, m_gm_ln_b, m_gm_w_s,
                           m_gm_b_s, m_gm_w_out, m_norm_mix_pre, m_norm_mix_post, m_norm_ffn_pre, m_norm_ffn_post,
                           m_ffn_w_gate, m_ffn_w_up, m_ffn_w_down, m_ple_w_proj, m_ple_w_gate, m_ple_norm)))
    V = dict(zip(WEIGHTS, (v_hg_w_in, v_hg_lb_logits, v_hg_out_norm, v_hg_w_out, v_gm_w_in, v_gm_ln_g, v_gm_ln_b, v_gm_w_s,
                           v_gm_b_s, v_gm_w_out, v_norm_mix_pre, v_norm_mix_post, v_norm_ffn_pre, v_norm_ffn_post,
                           v_ffn_w_gate, v_ffn_w_up, v_ffn_w_down, v_ple_w_proj, v_ple_w_gate, v_ple_norm)))
    return _step(x, p, W, M, V, loss_target)
```

```python
import functools

import jax
import jax.numpy as jnp
from jax import lax
from jax.experimental import pallas as pl
from jax.experimental.pallas import tpu as pltpu

F32 = jnp.float32
BF16 = jnp.bfloat16
MESH_ID = pl.DeviceIdType.MESH

LANES = 128
N_CHIPS = 4
N_DEV = 8
VMEM_LIMIT = 56 * 1024 * 1024
HG_SUB = 256
HG_BLOCK = 512
HG_HEADS_PER = 2
GM_CHUNK = 128
GM_BLOCK = 512
PACK_ROWS = 512
LB_FLOOR = 1e-30
EPS = 1e-6
ADAM_LR, ADAM_B1, ADAM_B2, ADAM_EPS, ADAM_WD, ADAM_STEP = 0.001, 0.9, 0.999, 1e-08, 0.01, 10


def _tile(n, pref, mult=LANES):
    if n <= pref:
        return n
    t = (pref // mult) * mult
    while t >= mult:
        if n % t == 0:
            return t
        t -= mult
    return n


def _cp(n_axes):
    return pltpu.CompilerParams(dimension_semantics=("arbitrary",) * n_axes, vmem_limit_bytes=VMEM_LIMIT)


def _dense(block):
    return tuple(b for b in block if b is not None)


def _bmap(name, grid, ins, outs, compute, scalars=(), into=None):
    n_s, n_in = len(scalars), len(ins)
    n_extra = 0 if into is None else 1

    def body(*refs):
        in_refs = refs[n_s:n_s + n_in]
        out_refs = refs[n_s + n_in + n_extra:]
        vals = [r[...] for r in in_refs]
        res = compute(*vals)
        for r, o, spec in zip(out_refs, res, outs):
            keep = spec[4]
            if keep is None:
                r[...] = o.astype(r.dtype)
            else:
                first = functools.reduce(jnp.logical_and, [pl.program_id(a) == 0 for a in range(keep, len(grid))])

                @pl.when(first)
                def _():
                    r[...] = jnp.zeros(r.shape, r.dtype)

                r[...] += o.astype(r.dtype)

    grid_spec = pltpu.PrefetchScalarGridSpec(
        num_scalar_prefetch=n_s, grid=grid,
        in_specs=[pl.BlockSpec(b, m) for _, b, m in ins] + [pl.BlockSpec(memory_space=pl.ANY)] * n_extra,
        out_specs=[pl.BlockSpec(o[2], o[3]) for o in outs])
    return pl.pallas_call(
        body, name=name, grid_spec=grid_spec,
        out_shape=[jax.ShapeDtypeStruct(o[0], o[1]) for o in outs],
        input_output_aliases={n_s + n_in: 0} if n_extra else {},
        compiler_params=_cp(len(grid)),
    )(*scalars, *[a for a, _, _ in ins], *([into] if n_extra else []))


def bmap_fwd(name, fn, grid, ins, outs, scalars=(), into=None):
    return _bmap(name, grid, ins, outs, lambda *v: fn(*[x.astype(F32) for x in v]), scalars, into)


def bmap_bwd(name, fn, grid, ins, cots, grads, scalars=()):
    n_in = len(ins)
    diff = [g[0] for g in grads]
    cot_ins = [c for c in cots if c is not None]

    def compute(*vals):
        xs = [v.astype(F32) for v in vals[:n_in]]
        cvals = list(vals[n_in:])

        def f(*d):
            full = list(xs)
            for i, dv in zip(diff, d):
                full[i] = dv
            return tuple(fn(*full))

        res, pull = jax.vjp(f, *[xs[i] for i in diff])
        cts = []
        for r, c in zip(res, cots):
            cts.append(jnp.zeros_like(r) if c is None else cvals.pop(0).astype(F32))
        return pull(tuple(cts))

    outs = [(ins[i][0].shape, dt, ins[i][1], ins[i][2], keep) for i, dt, keep in grads]
    return _bmap(name, grid, list(ins) + cot_ins, outs, compute, scalars)


def _mm(name, a, b, out_shape, out_dtype, grid, a_spec, b_spec, o_spec, dims, addend=None, alias_out=None):
    nk = grid[2]
    o_dense = _dense(o_spec[0])
    o_dense = (o_dense[0] * o_dense[1], o_dense[2]) if len(o_dense) == 3 else o_dense
    has_add = addend is not None
    has_alias = alias_out is not None

    def body(*refs):
        a_ref, b_ref = refs[0], refs[1]
        pos = 2
        c_ref = None
        if has_add:
            c_ref = refs[pos]
            pos += 1
        if has_alias:
            pos += 1
        o_ref = refs[pos]
        acc_ref = refs[pos + 1] if nk > 1 else None
        bv = b_ref[...]
        if bv.ndim == 3:
            bv = bv.reshape(bv.shape[0] * bv.shape[1], bv.shape[2])
        p = lax.dot_general(a_ref[...].astype(BF16), bv.astype(BF16), (dims, ((), ())), preferred_element_type=F32)

        def finish(total):
            if has_add:
                total = total + c_ref[...].astype(F32)
            o_ref[...] = total.reshape(o_ref.shape).astype(o_ref.dtype)

        if nk == 1:
            finish(p)
        else:
            k = pl.program_id(2)

            @pl.when(k == 0)
            def _():
                acc_ref[...] = p

            @pl.when(jnp.logical_and(k > 0, k < nk - 1))
            def _():
                acc_ref[...] += p

            @pl.when(k == nk - 1)
            def _():
                finish(acc_ref[...] + p)

    in_specs = [pl.BlockSpec(*a_spec), pl.BlockSpec(*b_spec)]
    operands = [a, b]
    if has_add:
        in_specs.append(pl.BlockSpec(o_spec[0], o_spec[1]))
        operands.append(addend)
    aliases = {}
    if has_alias:
        in_specs.append(pl.BlockSpec(memory_space=pl.ANY))
        aliases = {len(operands): 0}
        operands.append(alias_out)
    return pl.pallas_call(
        body, name=name, grid=grid, in_specs=in_specs, out_specs=pl.BlockSpec(*o_spec),
        out_shape=jax.ShapeDtypeStruct(out_shape, out_dtype),
        scratch_shapes=[pltpu.VMEM(o_dense, F32)] if nk > 1 else [],
        input_output_aliases=aliases,
        compiler_params=pltpu.CompilerParams(dimension_semantics=("parallel", "parallel", "arbitrary"),
                                             vmem_limit_bytes=VMEM_LIMIT),
    )(*operands)


NN, NT, TN = ((1,), (0,)), ((1,), (1,)), ((0,), (0,))
TM = 512
TT = 1024
TN_PREF = 1408
WHOLE_K = 2048


def mm_fwd(name, x, wg, l, kind, out_dtype=F32, parts=False, xl=None):
    if isinstance(wg, dict):
        wg, l = wg[l], 0
    _, _, R, C = wg.shape
    T = x.shape[-2]
    tm = _tile(T, TM, 8)
    if kind == "col":
        tn = _tile(C, TN_PREF)
        npc = C // tn
        grid = (T // tm, N_CHIPS * npc, 1)
        a_blk = (tm, R) if xl is None else (None, tm, R)
        a_map = (lambda i, j, k: (i, 0)) if xl is None else (lambda i, j, k: (xl, i, 0))
        b_spec = ((None, None, R, tn), lambda i, j, k: (j // npc, l, 0, j % npc))
        if parts:
            out_shape = (N_CHIPS, T, C)
            o_spec = ((None, tm, tn), lambda i, j, k: (j // npc, i, j % npc))
        else:
            out_shape = (T, N_CHIPS * C)
            o_spec = ((tm, tn), lambda i, j, k: (i, j))
    elif N_CHIPS * R <= WHOLE_K:
        tn = _tile(C, 1024)
        grid = (T // tm, C // tn, 1)
        a_blk = (tm, N_CHIPS * R)
        a_map = lambda i, j, k: (i, 0)
        b_spec = ((N_CHIPS, None, R, tn), lambda i, j, k: (0, l, 0, j))
        out_shape = (T, C)
        o_spec = ((tm, tn), lambda i, j, k: (i, j))
    else:
        tn = _tile(C, 2048)
        grid = (T // tm, C // tn, N_CHIPS)
        a_blk = (tm, R)
        a_map = lambda i, j, k: (i, k)
        b_spec = ((None, None, R, tn), lambda i, j, k: (k, l, 0, j))
        out_shape = (T, C)
        o_spec = ((tm, tn), lambda i, j, k: (i, j))
    return _mm(name, x, wg, out_shape, out_dtype, grid, (a_blk, a_map), b_spec, o_spec, NN)


def mm_bwd_x(name, dy, wg, l, kind, out_dtype=F32, parts=False, addend=None):
    if isinstance(wg, dict):
        wg, l = wg[l], 0
    _, _, R, C = wg.shape
    T = dy.shape[-2]
    tm = _tile(T, TM, 8)
    if kind == "col":
        tk = _tile(C, TN_PREF)
        npc = C // tk
        tno = _tile(R, 2048)
        grid = (T // tm, R // tno, N_CHIPS * npc)
        if parts:
            a_spec = ((None, tm, tk), lambda i, j, k: (k // npc, i, k % npc))
        else:
            a_spec = ((tm, tk), lambda i, j, k: (i, k))
        b_spec = ((None, None, tno, tk), lambda i, j, k: (k // npc, l, j, k % npc))
        out_shape = (T, R)
        o_spec = ((tm, tno), lambda i, j, k: (i, j))
    elif N_CHIPS * R <= WHOLE_K:
        grid = (T // tm, 1, 1)
        a_spec = ((tm, C), lambda i, j, k: (i, 0))
        b_spec = ((N_CHIPS, None, R, C), lambda i, j, k: (0, l, 0, 0))
        out_shape = (T, N_CHIPS * R)
        o_spec = ((tm, N_CHIPS * R), lambda i, j, k: (i, 0))
    else:
        grid = (T // tm, N_CHIPS, 1)
        a_spec = ((tm, C), lambda i, j, k: (i, 0))
        b_spec = ((None, None, R, C), lambda i, j, k: (j, l, 0, 0))
        out_shape = (T, N_CHIPS * R)
        o_spec = ((tm, R), lambda i, j, k: (i, j))
    return _mm(name, dy, wg, out_shape, out_dtype, grid, a_spec, b_spec, o_spec, NT, addend=addend)


def mm_bwd_w(name, x, dy, dwg, l, kind, parts=False, xl=None):
    if isinstance(dwg, dict):
        return {**dwg, l: mm_bwd_w(name, x, dy, dwg[l], 0, kind, parts=parts, xl=xl)}
    _, _, R, C = dwg.shape
    T = dy.shape[-2]
    tt = _tile(T, TT, 16)
    nt = T // tt
    if kind == "col":
        tn = _tile(C, TN_PREF)
        npc = C // tn
        tr = _tile(R, 1024)
        grid = (R // tr, N_CHIPS * npc, nt)
        if xl is None:
            a_spec = ((tt, tr), lambda i, j, t: (t, i))
        else:
            a_spec = ((None, tt, tr), lambda i, j, t: (xl, t, i))
        if parts:
            b_spec = ((None, tt, tn), lambda i, j, t: (j // npc, t, j % npc))
        else:
            b_spec = ((tt, tn), lambda i, j, t: (t, j))
        o_spec = ((None, None, tr, tn), lambda i, j, t: (j // npc, l, i, j % npc))
    elif N_CHIPS * R <= WHOLE_K:
        tn = _tile(C, 1024)
        grid = (1, C // tn, nt)
        a_spec = ((tt, N_CHIPS * R), lambda i, j, t: (t, 0))
        b_spec = ((tt, tn), lambda i, j, t: (t, j))
        o_spec = ((N_CHIPS, None, R, tn), lambda i, j, t: (0, l, 0, j))
    else:
        tn = _tile(C, 1024)
        grid = (N_CHIPS, C // tn, nt)
        a_spec = ((tt, R), lambda i, j, t: (t, i))
        b_spec = ((tt, tn), lambda i, j, t: (t, j))
        o_spec = ((None, None, R, tn), lambda i, j, t: (i, l, 0, j))
    return _mm(name, x, dy, dwg.shape, dwg.dtype, grid, a_spec, b_spec, o_spec, TN, alias_out=dwg)


def _sigmoid(x):
    return 0.5 * jnp.tanh(0.5 * x) + 0.5


def f_swiglu(gate, up):
    return (gate * _sigmoid(gate) * up,)


def ffn_gate_up(name, x, wg_gate, wg_up, l):
    if isinstance(wg_gate, dict):
        wg_gate, wg_up, l = wg_gate[l], wg_up[l], 0
    _, _, R, C = wg_gate.shape
    T = x.shape[0]
    tm = _tile(T, TM, 8)
    tn = _tile(C, TN_PREF)
    npc = C // tn

    def body(x_ref, g_ref, u_ref, gate_ref, up_ref, act_ref):
        xv = x_ref[...].astype(BF16)
        gate = jnp.dot(xv, g_ref[...], preferred_element_type=F32)
        up = jnp.dot(xv, u_ref[...], preferred_element_type=F32)
        gate_ref[...] = gate
        up_ref[...] = up
        act_ref[...] = f_swiglu(gate, up)[0].astype(act_ref.dtype)

    w_spec = pl.BlockSpec((None, None, R, tn), lambda i, j: (j // npc, l, 0, j % npc))
    o_spec = pl.BlockSpec((tm, tn), lambda i, j: (i, j))
    N = N_CHIPS * C
    return pl.pallas_call(
        body, name=name, grid=(T // tm, N_CHIPS * npc),
        in_specs=[pl.BlockSpec((tm, R), lambda i, j: (i, 0)), w_spec, w_spec], out_specs=[o_spec, o_spec, o_spec],
        out_shape=[jax.ShapeDtypeStruct((T, N), F32), jax.ShapeDtypeStruct((T, N), F32), jax.ShapeDtypeStruct((T, N), BF16)],
        compiler_params=_cp(2),
    )(x, wg_gate, wg_up)


def ffn_in_dx(name, dgate, dup, wg_gate, wg_up, l):
    if isinstance(wg_gate, dict):
        wg_gate, wg_up, l = wg_gate[l], wg_up[l], 0
    _, _, R, C = wg_gate.shape
    T = dgate.shape[0]
    tm = _tile(T, TM, 8)
    tk = _tile(C, TN_PREF)
    npc = C // tk
    nk = N_CHIPS * npc

    def body(dg_ref, du_ref, wg_ref, wu_ref, o_ref, acc):
        k = pl.program_id(1)

        @pl.when(k == 0)
        def _():
            acc[...] = jnp.zeros(acc.shape, F32)

        @pl.when(k < nk)
        def _():
            acc[...] += lax.dot_general(dg_ref[...], wg_ref[...], (NT, ((), ())), preferred_element_type=F32)

        @pl.when(k >= nk)
        def _():
            acc[...] += lax.dot_general(du_ref[...], wu_ref[...], (NT, ((), ())), preferred_element_type=F32)

        @pl.when(k == 2 * nk - 1)
        def _():
            o_ref[...] = acc[...]

    first = lambda k: jnp.minimum(k, nk - 1)
    second = lambda k: jnp.maximum(k - nk, 0)
    return pl.pallas_call(
        body, name=name, grid=(T // tm, 2 * nk),
        in_specs=[pl.BlockSpec((tm, tk), lambda i, k: (i, first(k))), pl.BlockSpec((tm, tk), lambda i, k: (i, second(k))),
                  pl.BlockSpec((None, None, R, tk), lambda i, k: (first(k) // npc, l, 0, first(k) % npc)),
                  pl.BlockSpec((None, None, R, tk), lambda i, k: (second(k) // npc, l, 0, second(k) % npc))],
        out_specs=pl.BlockSpec((tm, R), lambda i, k: (i, 0)),
        out_shape=jax.ShapeDtypeStruct((T, R), F32),
        scratch_shapes=[pltpu.VMEM((tm, R), F32)],
        compiler_params=_cp(2),
    )(dgate, dup, wg_gate, wg_up)


def ffn_down_dx(name, df, wg_down, l, gate, up):
    if isinstance(wg_down, dict):
        wg_down, l = wg_down[l], 0
    _, _, R, C = wg_down.shape
    T = df.shape[0]
    tm = _tile(T, TM, 8)

    def body(df_ref, w_ref, gate_ref, up_ref, dg_ref, du_ref):
        dact = lax.dot_general(df_ref[...].astype(BF16), w_ref[...], (NT, ((), ())), preferred_element_type=F32)
        _, pull = jax.vjp(lambda g, u: f_swiglu(g, u)[0], gate_ref[...], up_ref[...])
        dg, du = pull(dact)
        dg_ref[...] = dg.astype(dg_ref.dtype)
        du_ref[...] = du.astype(du_ref.dtype)

    t_spec = pl.BlockSpec((tm, R), lambda i, j: (i, j))
    return pl.pallas_call(
        body, name=name, grid=(T // tm, N_CHIPS),
        in_specs=[pl.BlockSpec((tm, C), lambda i, j: (i, 0)), pl.BlockSpec((None, None, R, C), lambda i, j: (j, l, 0, 0)),
                  t_spec, t_spec],
        out_specs=[t_spec, t_spec],
        out_shape=[jax.ShapeDtypeStruct((T, N_CHIPS * R), BF16)] * 2,
        compiler_params=_cp(2),
    )(df, wg_down, gate, up)


def _rms(x, g):
    return x * lax.rsqrt(jnp.mean(x * x, axis=-1, keepdims=True) + EPS) * g


def f_prenorm(h, g):
    return (_rms(h, g),)


def f_prenorm_thru(h, g):
    return _rms(h, g), h


def f_post_pre(h, m, g_post, g_pre):
    h1 = h + _rms(m, g_post)
    return h1, _rms(h1, g_pre)


def f_post(h1, f, g):
    return (h1 + _rms(f, g),)


def f_ple(h2, e, zg, g):
    return (h2 + _rms(e * _sigmoid(zg), g),)


def f_ple_pre(h2, e, zg, g, g_next):
    h3 = h2 + _rms(e * _sigmoid(zg), g)
    return h3, _rms(h3, g_next)


def _gelu(x):
    return 0.5 * x * (1.0 + lax.erf(x * 0.7071067811865476))


def f_gm_in(z, ln_g, ln_b):
    w = z.shape[-1] // 2
    u = _gelu(z[:, :w])
    v = _gelu(z[:, w:])
    mu = jnp.mean(v, axis=-1, keepdims=True)
    vc = v - mu
    vn = vc * lax.rsqrt(jnp.mean(vc * vc, axis=-1, keepdims=True) + EPS) * ln_g + ln_b
    return u, vn


def f_gm_spatial(u, vn, ws, bs):
    t = lax.broadcasted_iota(jnp.int32, ws.shape, 0)
    s = lax.broadcasted_iota(jnp.int32, ws.shape, 1)
    wm = jnp.where(t >= s, ws, 0.0).astype(BF16)
    ys = []
    for n in range(u.shape[0] // GM_CHUNK):
        rows = slice(n * GM_CHUNK, (n + 1) * GM_CHUNK)
        sv = jnp.dot(wm, vn[rows].astype(BF16), preferred_element_type=F32) + bs
        ys.append(u[rows] * sv)
    return (jnp.concatenate(ys, axis=0) if len(ys) > 1 else ys[0],)


def f_adam(w, g, m, v):
    m = ADAM_B1 * m + (1.0 - ADAM_B1) * g
    v = ADAM_B2 * v + (1.0 - ADAM_B2) * jnp.square(g)
    m_hat = m / (1.0 - ADAM_B1 ** ADAM_STEP)
    v_hat = v / (1.0 - ADAM_B2 ** ADAM_STEP)
    delta = -ADAM_LR * (m_hat / (jnp.sqrt(v_hat) + ADAM_EPS) + ADAM_WD * w)
    return delta, m, v


def _make_f_lb(n_layers):
    def f_lb(*logits):
        mx = functools.reduce(jnp.maximum, logits)
        ex = [jnp.exp(r - mx) for r in logits]
        tot = functools.reduce(lambda a, b: a + b, ex)
        sm = [e / tot for e in ex]
        outs = []
        run = jnp.zeros_like(sm[0])
        for j in range(n_layers):
            if j > 0:
                run = run + sm[j]
            lb = run
            outs += [jnp.log(jnp.maximum(lb, LB_FLOOR)), jnp.log(1.0 - lb), 1.0 - lb]
        return tuple(outs)
    return f_lb


def rows_fwd(name, fn, rows, params, out_dtypes, tm=256):
    T = rows[0].shape[0]
    tm = _tile(T, tm, 16)
    ins = [(r, (tm, r.shape[1]), lambda i: (i, 0)) for r in rows]
    ins += [(p, p.shape, lambda i: (0, 0)) for p in params]
    shapes = jax.eval_shape(lambda *a: fn(*a), *[jax.ShapeDtypeStruct((tm, r.shape[1]), F32) for r in rows],
                            *[jax.ShapeDtypeStruct(p.shape, F32) for p in params])
    outs = [((T, s.shape[1]), dt, (tm, s.shape[1]), lambda i: (i, 0), None) for s, dt in zip(shapes, out_dtypes)]
    return bmap_fwd(name, fn, (T // tm,), ins, outs)


def rows_bwd(name, fn, rows, params, cots, row_grad_dtypes, tm=256):
    T = rows[0].shape[0]
    tm = _tile(T, tm, 16)
    ins = [(r, (tm, r.shape[1]), lambda i: (i, 0)) for r in rows]
    ins += [(p, p.shape, lambda i: (0, 0)) for p in params]
    cts = [None if c is None else (c, (tm, c.shape[1]), lambda i: (i, 0)) for c in cots]
    grads = [(i, dt, None) for i, dt in enumerate(row_grad_dtypes) if dt is not None]
    grads += [(len(rows) + j, F32, 0) for j in range(len(params))]
    return bmap_bwd(name, fn, (T // tm,), ins, cts, grads)


def _log_sigmoid(z):
    return jnp.minimum(z, 0.0) - jnp.log(1.0 + jnp.exp(-jnp.abs(z)))


def _hg_gates(zf, ll0, ll1, oml):
    x2 = ll1 + _log_sigmoid(zf)
    mx = jnp.maximum(ll0, x2)
    g = mx + jnp.log(jnp.exp(ll0 - mx) + jnp.exp(x2 - mx))
    return g, oml * _sigmoid(-zf)


def hg_constants(n):
    levels = n.bit_length() - 1
    r = jnp.arange(n, dtype=jnp.int32)
    bounds = [r] + [((r >> (s + 1)) << (s + 1)) + ((1 << s) - 1) for s in range(levels)]
    sel = jnp.concatenate([(r[None, :] <= bd[:, None]) for bd in bounds], axis=0).astype(BF16)
    later = jnp.stack([((r >> s) & 1) for s in range(levels)])
    sign = jnp.broadcast_to((2 * later - 1).astype(F32)[:, :, None], (levels, n, LANES))
    pair = jnp.stack([((r[:, None] >> (s + 1)) == (r[None, :] >> (s + 1))) & (later[s][:, None] == 1) & (later[s][None, :] == 0)
                      for s in range(levels)]).astype(F32)
    return sel, sel.T, sign, pair


def _dot2(m, x):
    hi = x.astype(BF16)
    lo = (x - hi.astype(F32)).astype(BF16)
    p = jnp.dot(m, jnp.concatenate([hi, lo], axis=1), preferred_element_type=F32)
    w = x.shape[1]
    return p[:, :w] + p[:, w:]


@jax.custom_vjp
def _sel_dot(sel, selt, g):
    return _dot2(sel, g)


def _sel_dot_fwd(sel, selt, g):
    return _dot2(sel, g), (sel, selt)


def _sel_dot_bwd(res, d):
    sel, selt = res
    return jnp.zeros_like(sel), jnp.zeros_like(selt), _dot2(selt, d)


_sel_dot.defvjp(_sel_dot_fwd, _sel_dot_bwd)


def _hg_state(st, zf, zi, ll0, ll1, oml, tri):
    g, k = _hg_gates(zf, ll0, ll1, oml)
    b = _dot2(tri, g)
    tot = jnp.sum(g, axis=0, keepdims=True)
    kd = k * jnp.exp(tot - b)
    return st * jnp.exp(tot) + jnp.dot(zi.T.astype(BF16), kd.astype(BF16), preferred_element_type=F32)


def _hg_step(st, zq, zf, zi, zg, ll0, ll1, oml, onorm, sel, selt, sign, pair):
    n = zq.shape[0]
    levels = n.bit_length() - 1
    q = zq * _sigmoid(zq)
    g, k = _hg_gates(zf, ll0, ll1, oml)
    sums = _sel_dot(sel, selt, g)
    b = sums[:n]
    tot = jnp.sum(g, axis=0, keepdims=True)
    o = lax.dot_general((q * jnp.exp(b)).astype(BF16), st.astype(BF16), (NT, ((), ())), preferred_element_type=F32)
    a = jnp.zeros((n, n), F32)
    for s in range(levels):
        e = jnp.exp(sign[s] * (b - sums[(s + 1) * n:(s + 2) * n]))
        al = lax.dot_general((q * e).astype(BF16), (k * e).astype(BF16), (NT, ((), ())), preferred_element_type=F32)
        a = a + pair[s] * al
    o = o + jnp.dot(a.astype(BF16), zi.astype(BF16), preferred_element_type=F32)
    o = o + jnp.sum(q * k, axis=1, keepdims=True) * zi
    kd = k * jnp.exp(tot - b)
    st_new = st * jnp.exp(tot) + jnp.dot(zi.T.astype(BF16), kd.astype(BF16), preferred_element_type=F32)
    og = _rms(o, onorm) * (zg * _sigmoid(zg))
    return og, st_new


def _whole(arr, n_grid):
    zeros = (0,) * arr.ndim
    return pl.BlockSpec(arr.shape, (lambda h, n: zeros) if n_grid == 2 else (lambda i: zeros))


def _hg_dims(proj4, n_seq):
    _, T, D = proj4.shape
    S = T // n_seq
    hp = HG_HEADS_PER if (D // LANES) % HG_HEADS_PER == 0 else 1
    tb = min(HG_BLOCK, S)
    streams = [(b, hl) for b in range(n_seq) for hl in range(hp)]
    return T, D, S, hp, D // (LANES * hp), LANES * hp, tb, S // tb, tb // HG_SUB, streams


def hgrn_fwd(name, proj4, ll0, ll1, oml, onorm, n_seq):
    T, D, S, hp, n_hg, W, tb, nblk, nsub, streams = _hg_dims(proj4, n_seq)
    ns = len(streams)

    def body(p_ref, ll0_ref, ll1_ref, oml_ref, on_ref, sel_ref, selt_ref, later_ref, pair_ref, og_ref, st_ref, st):
        @pl.when(pl.program_id(1) == 0)
        def _():
            st[...] = jnp.zeros(st.shape, F32)

        st_ref[...] = st[...]
        on = on_ref[...]

        def step(j, carry):
            r = pl.ds(pl.multiple_of(j * HG_SUB, HG_SUB), HG_SUB)
            consts = (sel_ref[...], selt_ref[...], later_ref[...], pair_ref[...])
            args = []
            for si, (b, hl) in enumerate(streams):
                ln = slice(hl * LANES, (hl + 1) * LANES)
                args.append((st[si], p_ref[0, b, r, ln], p_ref[1, b, r, ln], p_ref[2, b, r, ln], p_ref[3, b, r, ln],
                             ll0_ref[:, ln], ll1_ref[:, ln], oml_ref[:, ln], on) + consts)
            res = [_hg_step(*a) for a in args]
            for si, (b, hl) in enumerate(streams):
                og_ref[b, r, hl * LANES:(hl + 1) * LANES] = res[si][0].astype(og_ref.dtype)
                st[si] = res[si][1]
            return carry

        lax.fori_loop(0, nsub, step, 0)

    vec = pl.BlockSpec((1, W), lambda h, n: (0, h))
    consts = hg_constants(HG_SUB)
    og, states = pl.pallas_call(
        body, name=name, grid=(n_hg, nblk),
        in_specs=[pl.BlockSpec((4, n_seq, tb, W), lambda h, n: (0, 0, n, h)), vec, vec, vec,
                  pl.BlockSpec((1, LANES), lambda h, n: (0, 0))] + [_whole(c, 2) for c in consts],
        out_specs=[pl.BlockSpec((n_seq, tb, W), lambda h, n: (0, n, h)),
                   pl.BlockSpec((None, None, ns, LANES, LANES), lambda h, n: (h, n, 0, 0, 0))],
        out_shape=[jax.ShapeDtypeStruct((n_seq, S, D), BF16),
                   jax.ShapeDtypeStruct((n_hg, nblk, ns, LANES, LANES), F32)],
        scratch_shapes=[pltpu.VMEM((ns, LANES, LANES), F32)],
        compiler_params=_cp(2),
    )(proj4.reshape(4, n_seq, S, D), ll0, ll1, oml, onorm, *consts)
    return og.reshape(T, D), states


def hgrn_bwd(name, proj4, states, dog, ll0, ll1, oml, onorm, n_seq):
    T, D, S, hp, n_hg, W, tb, nblk, nsub, streams = _hg_dims(proj4, n_seq)
    ns = len(streams)

    def body(p_ref, st_ref, dog_ref, ll0_ref, ll1_ref, oml_ref, on_ref, sel_ref, selt_ref, later_ref, pair_ref,
             dp_ref, dll0_ref, dll1_ref, doml_ref, don_ref, sbuf, dst):
        n_id = pl.program_id(1)

        @pl.when(n_id == 0)
        def _():
            dst[...] = jnp.zeros(dst.shape, F32)
            for ref in (dll0_ref, dll1_ref, doml_ref):
                ref[...] = jnp.zeros(ref.shape, F32)

        @pl.when(jnp.logical_and(n_id == 0, pl.program_id(0) == 0))
        def _():
            don_ref[...] = jnp.zeros(don_ref.shape, F32)

        on = on_ref[...]

        def fwd(j, carry):
            r = pl.ds(pl.multiple_of(j * HG_SUB, HG_SUB), HG_SUB)
            tri = sel_ref[0:HG_SUB, :]
            args = []
            for si, (b, hl) in enumerate(streams):
                ln = slice(hl * LANES, (hl + 1) * LANES)
                args.append((carry[si], p_ref[1, b, r, ln], p_ref[2, b, r, ln],
                             ll0_ref[:, ln], ll1_ref[:, ln], oml_ref[:, ln], tri))
            for si in range(ns):
                sbuf[si, j] = carry[si]
            return tuple(_hg_state(*a) for a in args)

        lax.fori_loop(0, nsub, fwd, tuple(st_ref[si] for si in range(ns)))

        def bwd(jj, carry):
            j = nsub - 1 - jj
            r = pl.ds(pl.multiple_of(j * HG_SUB, HG_SUB), HG_SUB)
            args, cts = [], []
            for si, (b, hl) in enumerate(streams):
                ln = slice(hl * LANES, (hl + 1) * LANES)
                args.append((sbuf[si, j], p_ref[0, b, r, ln], p_ref[1, b, r, ln], p_ref[2, b, r, ln],
                             p_ref[3, b, r, ln], ll0_ref[:, ln], ll1_ref[:, ln], oml_ref[:, ln], on))
                cts.append((dog_ref[b, r, ln].astype(F32), dst[si]))
            consts = (sel_ref[...], selt_ref[...], later_ref[...], pair_ref[...])
            step_fn = lambda *a: _hg_step(*a, *consts)
            ds = [jax.vjp(step_fn, *a)[1](ct) for a, ct in zip(args, cts)]
            d_on = carry
            for si, (b, hl) in enumerate(streams):
                ln = slice(hl * LANES, (hl + 1) * LANES)
                d = ds[si]
                dst[si] = d[0]
                for part in range(4):
                    dp_ref[part, b, r, ln] = d[1 + part].astype(dp_ref.dtype)
                dll0_ref[:, ln] += d[5]
                dll1_ref[:, ln] += d[6]
                doml_ref[:, ln] += d[7]
                d_on = d_on + d[8]
            return d_on

        don_ref[...] += lax.fori_loop(0, nsub, bwd, jnp.zeros((1, LANES), F32))

    last = nblk - 1
    vec = pl.BlockSpec((1, W), lambda h, n: (0, h))
    one = pl.BlockSpec((1, LANES), lambda h, n: (0, 0))
    consts = hg_constants(HG_SUB)
    dproj, d0, d1, d2, d_on = pl.pallas_call(
        body, name=name, grid=(n_hg, nblk),
        in_specs=[pl.BlockSpec((4, n_seq, tb, W), lambda h, n: (0, 0, last - n, h)),
                  pl.BlockSpec((None, None, ns, LANES, LANES), lambda h, n: (h, last - n, 0, 0, 0)),
                  pl.BlockSpec((n_seq, tb, W), lambda h, n: (0, last - n, h)), vec, vec, vec, one]
        + [_whole(c, 2) for c in consts],
        out_specs=[pl.BlockSpec((4, n_seq, tb, W), lambda h, n: (0, 0, last - n, h)), vec, vec, vec, one],
        out_shape=[jax.ShapeDtypeStruct((4, n_seq, S, D), BF16)] + [jax.ShapeDtypeStruct((1, D), F32)] * 3
        + [jax.ShapeDtypeStruct((1, LANES), F32)],
        scratch_shapes=[pltpu.VMEM((ns, nsub, LANES, LANES), F32), pltpu.VMEM((ns, LANES, LANES), F32)],
        compiler_params=_cp(2),
    )(proj4.reshape(4, n_seq, S, D), states, dog.reshape(n_seq, S, D), ll0, ll1, oml, onorm, *consts)
    return dproj.reshape(4, T, D), d0, d1, d2, d_on


def _place():
    x, y, c = lax.axis_index("x"), lax.axis_index("y"), lax.axis_index("c")
    chips = [(1 - x, y), (x, 1 - y), (1 - x, 1 - y)]
    return x, y, c, chips


ANY = pl.BlockSpec(memory_space=pl.ANY)


def _comm_call(name, body, ins, out_shapes, sems, aliases=None):
    return pl.pallas_call(
        body, name=name, in_specs=[ANY] * len(ins), out_specs=[ANY] * len(out_shapes),
        out_shape=out_shapes, scratch_shapes=sems, input_output_aliases=aliases or {},
        compiler_params=pltpu.CompilerParams(has_side_effects=True),
    )(*ins)


HBM_SPEC = pl.BlockSpec(memory_space=pltpu.HBM)
SEM_SPEC = pl.BlockSpec(memory_space=pltpu.SEMAPHORE)
SPLIT_EFFECT = pltpu.SideEffectType.DATAFLOW_SIDE_EFFECTING
N_PEER_CHIPS = 3


def split_start(name, build, arrays, n_peers=N_PEER_CHIPS):
    n = len(arrays)

    def body(*refs):
        send, recv = refs[n], refs[n + 1]
        token = refs[2 * n + 2]
        starts, _ = build(refs[:n], send, recv)
        for cp in starts:
            cp.start()
        token[...] = jnp.zeros_like(token)

    res = pl.pallas_call(
        body, name=name,
        out_shape=(pltpu.SemaphoreType.DMA((n_peers,)), pltpu.SemaphoreType.DMA((n_peers,)),
                   *[pltpu.HBM(a.shape, a.dtype) for a in arrays], jax.ShapeDtypeStruct((8, LANES), F32)),
        in_specs=[HBM_SPEC] * n,
        out_specs=(SEM_SPEC, SEM_SPEC, *[HBM_SPEC] * n, pl.BlockSpec(memory_space=pltpu.VMEM)),
        input_output_aliases={i: 2 + i for i in range(n)},
        compiler_params=pltpu.CompilerParams(has_side_effects=SPLIT_EFFECT),
    )(*[pltpu.with_memory_space_constraint(a, pltpu.HBM) for a in arrays])
    return res[0], res[1], list(res[2:2 + n]), res[2 + n]


def split_wait(name, build, send, recv, arrays, after):
    n = len(arrays)

    def body(*refs):
        starts, arrivals = build(refs[:n], refs[n], refs[n + 1])
        for cp in starts:
            cp.wait_send()
        for cp in arrivals:
            cp.wait_recv()

    return list(pl.pallas_call(
        body, name=name, out_shape=tuple(pltpu.HBM(a.shape, a.dtype) for a in arrays),
        in_specs=[HBM_SPEC] * n + [SEM_SPEC, SEM_SPEC, ANY], out_specs=tuple([HBM_SPEC] * n),
        input_output_aliases={i: i for i in range(n)},
        compiler_params=pltpu.CompilerParams(has_side_effects=SPLIT_EFFECT),
    )(*arrays, send, recv, after))


def _row_half(ref, dim, who):
    rh = ref.shape[dim] // 2
    return pl.ds(who * rh, rh)


def gather_build(refs, send, recv):
    x, y, c, chips = _place()
    q = 2 * x + y
    starts, arrivals = [], []
    for buf in refs:
        rows = _row_half(buf, 2, c)
        for j, (px, py) in enumerate(chips):
            mine, got = buf.at[q, :, rows], buf.at[2 * px + py, :, rows]
            starts.append(pltpu.make_async_remote_copy(src_ref=mine, dst_ref=mine, send_sem=send.at[j], recv_sem=recv.at[j],
                                                       device_id=(px, py, c), device_id_type=MESH_ID))
            arrivals.append(pltpu.make_async_remote_copy(src_ref=got, dst_ref=got, send_sem=send.at[j], recv_sem=recv.at[j],
                                                         device_id=(px, py, c), device_id_type=MESH_ID))
    return starts, arrivals


def forward_build(refs, send, recv):
    x, y, c, chips = _place()
    sib = (x, y, 1 - c)
    starts, arrivals = [], []
    for buf in refs:
        for j, (px, py) in enumerate(chips):
            got = buf.at[2 * px + py, :, _row_half(buf, 2, c)]
            theirs = buf.at[2 * px + py, :, _row_half(buf, 2, 1 - c)]
            starts.append(pltpu.make_async_remote_copy(src_ref=got, dst_ref=got, send_sem=send.at[j], recv_sem=recv.at[j],
                                                       device_id=sib, device_id_type=MESH_ID))
            arrivals.append(pltpu.make_async_remote_copy(src_ref=theirs, dst_ref=theirs, send_sem=send.at[j], recv_sem=recv.at[j],
                                                         device_id=sib, device_id_type=MESH_ID))
    return starts, arrivals


def scatter_build(refs, send, recv):
    n = len(refs) // 2
    x, y, c, _ = _place()
    starts, arrivals = [], []
    for a in range(n):
        src, dst = refs[a], refs[n + a]
        for k in range(1, N_DEV):
            px, py, pc = x ^ (k >> 2), y ^ ((k >> 1) & 1), c ^ (k & 1)
            theirs = src.at[2 * px + py, :, _row_half(src, 2, pc)]
            starts.append(pltpu.make_async_remote_copy(src_ref=theirs, dst_ref=dst.at[k - 1], send_sem=send.at[k - 1],
                                                       recv_sem=recv.at[k - 1], device_id=(px, py, pc), device_id_type=MESH_ID))
            arrivals.append(pltpu.make_async_remote_copy(src_ref=dst.at[k - 1], dst_ref=dst.at[k - 1], send_sem=send.at[k - 1],
                                                         recv_sem=recv.at[k - 1], device_id=(px, py, pc), device_id_type=MESH_ID))
    return starts, arrivals


def share_build(refs, send, recv):
    dst = refs[0]
    x, y, c, _ = _place()
    mine = dst.at[4 * x + 2 * y + c]
    starts, arrivals = [], []
    for k in range(1, N_DEV):
        px, py, pc = x ^ (k >> 2), y ^ ((k >> 1) & 1), c ^ (k & 1)
        got = dst.at[4 * px + 2 * py + pc]
        starts.append(pltpu.make_async_remote_copy(src_ref=mine, dst_ref=mine, send_sem=send.at[k - 1], recv_sem=recv.at[k - 1],
                                                   device_id=(px, py, pc), device_id_type=MESH_ID))
        arrivals.append(pltpu.make_async_remote_copy(src_ref=got, dst_ref=got, send_sem=send.at[k - 1], recv_sem=recv.at[k - 1],
                                                     device_id=(px, py, pc), device_id_type=MESH_ID))
    return starts, arrivals


def join_row_halves(name, bufs):
    n = len(bufs)

    def body(*refs):
        dst = refs[n:2 * n]
        send, recv = refs[2 * n:]
        x, y, c, _ = _place()
        cps = []
        for a in range(n):
            mine = dst[a].at[:, _row_half(dst[a], 1, c)]
            cps.append(pltpu.make_async_remote_copy(src_ref=mine, dst_ref=mine, send_sem=send.at[a], recv_sem=recv.at[a],
                                                    device_id=(x, y, 1 - c), device_id_type=MESH_ID))
        for cp in cps:
            cp.start()
        for a in range(n):
            theirs = dst[a].at[:, _row_half(dst[a], 1, 1 - c)]
            pltpu.make_async_remote_copy(src_ref=theirs, dst_ref=theirs, send_sem=send.at[a], recv_sem=recv.at[a],
                                         device_id=(x, y, 1 - c), device_id_type=MESH_ID).wait_recv()
        for cp in cps:
            cp.wait_send()

    outs = [jax.ShapeDtypeStruct(b.shape, b.dtype) for b in bufs]
    sems = [pltpu.SemaphoreType.DMA((n,))] * 2
    return _comm_call(name, body, bufs, outs, sems, aliases={a: a for a in range(n)})


def share_with_all(name, packed, me):
    slots = lax.dynamic_update_slice(jnp.zeros((N_DEV,) + packed.shape, packed.dtype), packed[None], (me, 0, 0))

    def body(_, dst, send, recv):
        x, y, c, _ = _place()
        me = 4 * x + 2 * y + c
        cps = []
        for k in range(1, N_DEV):
            px, py, pc = x ^ (k >> 2), y ^ ((k >> 1) & 1), c ^ (k & 1)
            cps.append(pltpu.make_async_remote_copy(src_ref=dst.at[me], dst_ref=dst.at[me], send_sem=send.at[k - 1],
                                                    recv_sem=recv.at[k - 1], device_id=(px, py, pc), device_id_type=MESH_ID))
        for cp in cps:
            cp.start()
        for k in range(1, N_DEV):
            px, py, pc = x ^ (k >> 2), y ^ ((k >> 1) & 1), c ^ (k & 1)
            got = dst.at[4 * px + 2 * py + pc]
            pltpu.make_async_remote_copy(src_ref=got, dst_ref=got, send_sem=send.at[k - 1], recv_sem=recv.at[k - 1],
                                         device_id=(px, py, pc), device_id_type=MESH_ID).wait_recv()
        for cp in cps:
            cp.wait_send()

    outs = [jax.ShapeDtypeStruct(slots.shape, slots.dtype)]
    sems = [pltpu.SemaphoreType.DMA((N_DEV - 1,))] * 2
    return _comm_call(name, body, [slots], outs, sems, aliases={0: 0})[0]


def _w_tiles(R, C):
    return _tile(R, max(16, (1 << 20) // (4 * C) // 16 * 16), 16)


def cast_bf16(w, l, q_arr):
    _, R, C = w.shape
    tr = _w_tiles(R, C)
    ins = [(w, (None, tr, C), lambda r, q: (l, r, 0))]
    outs = [((N_CHIPS, 1, R, C), BF16, (None, None, tr, C), lambda r, q: (q[0], 0, r, 0), None)]
    return bmap_fwd("cast_bf16", lambda a: (a,), (R // tr,), ins, outs, scalars=(q_arr,))[0]


def sum_partials(own, landed, into, l, q_arr, c_arr):
    n_land, _, rh, C = landed.shape
    tr = _w_tiles(rh, C)
    nb = rh // tr
    blk = (None, None, tr, C)
    ins = [(own, blk, lambda r, q, c: (q[0], 0, c[0] * nb + r, 0))]
    ins += [(landed, blk, (lambda r, q, c, kk=kk: (kk, 0, r, 0))) for kk in range(n_land)]
    outs = [(into.shape, F32, (None, tr, C), lambda r, q, c: (l, c[0] * nb + r, 0), None)]
    return bmap_fwd("sum_partials", lambda *t: (functools.reduce(lambda u, v: u + v, t),), (nb,), ins, outs,
                    scalars=(q_arr, c_arr), into=into)[0]


def sum_devices(slots):
    nd, NR, C = slots.shape
    tr = _tile(NR, 512, 8)
    ins = [(slots, (None, tr, C), (lambda r, dd=dd: (dd, r, 0))) for dd in range(nd)]
    outs = [((NR, C), F32, (tr, C), lambda r: (r, 0), None)]
    return bmap_fwd("sum_devices", lambda *a: (functools.reduce(lambda u, v: u + v, a),), (NR // tr,), ins, outs)[0]


def adamw(name, w, g, m, v, with_grad=False):
    if w.ndim == 2:
        R, C = w.shape
        tr = _w_tiles(R, C)
        spec = ((tr, C), lambda r: (r, 0))
        grid = (R // tr,)
    else:
        L, R, C = w.shape
        tr = _w_tiles(R, C)
        spec = ((None, tr, C), lambda l, r: (l, r, 0))
        grid = (L, R // tr)
    ins = [(a,) + spec for a in (w, g, m, v)]
    outs = [(w.shape, F32) + spec + (None,)] * (4 if with_grad else 3)
    fn = (lambda a, b, c, d: f_adam(a, b, c, d) + (b,)) if with_grad else f_adam
    return bmap_fwd(name, fn, grid, ins, outs)


def loss_and_grad(h, target):
    T, D = h.shape
    tm = _tile(T, 256, 8)

    def fn(hv, tv):
        d = hv - tv
        return jnp.sum(d * d, keepdims=True).reshape(1, 1) * (0.5 / D), d * (1.0 / D)

    ins = [(h, (tm, D), lambda i: (i, 0)), (target, (tm, D), lambda i: (i, 0))]
    outs = [((1, 1), F32, (1, 1), lambda i: (0, 0), 0), ((T, D), F32, (tm, D), lambda i: (i, 0), None)]
    return bmap_fwd("loss_and_grad", fn, (T // tm,), ins, outs)


BIG = ("hg_w_in", "hg_w_out", "gm_w_in", "gm_w_out", "ffn_w_gate", "ffn_w_up", "ffn_w_down", "ple_w_proj", "ple_w_gate")
KIND = {"hg_w_in": "col", "hg_w_out": "row", "gm_w_in": "col", "gm_w_out": "row", "ffn_w_gate": "col",
        "ffn_w_up": "col", "ffn_w_down": "row", "ple_w_proj": "col", "ple_w_gate": "row"}
SMALL = ("hg_lb_logits", "hg_out_norm", "gm_ln_g", "gm_ln_b", "gm_w_s", "gm_b_s", "norm_mix_pre", "norm_mix_post",
         "norm_ffn_pre", "norm_ffn_post", "ple_norm")
WEIGHTS = ("hg_w_in", "hg_lb_logits", "hg_out_norm", "hg_w_out", "gm_w_in", "gm_ln_g", "gm_ln_b", "gm_w_s", "gm_b_s",
           "gm_w_out", "norm_mix_pre", "norm_mix_post", "norm_ffn_pre", "norm_ffn_post", "ffn_w_gate", "ffn_w_up",
           "ffn_w_down", "ple_w_proj", "ple_w_gate", "ple_norm")


def _pack(arrs):
    rows = []
    for a in arrs:
        flat = a.reshape(-1)
        pad = (-flat.shape[0]) % (8 * LANES)
        rows.append(jnp.pad(flat, (0, pad)).reshape(-1, LANES))
    n_rows = sum(r.shape[0] for r in rows)
    rows.append(jnp.zeros(((-n_rows) % PACK_ROWS, LANES), F32))
    return jnp.concatenate(rows, axis=0)


def _unpack(packed, shapes):
    out, r = [], 0
    for s in shapes:
        size = 1
        for d in s:
            size *= d
        nr = -(-size // (8 * LANES)) * 8
        out.append(packed[r:r + nr].reshape(-1)[:size].reshape(s))
        r += nr
    return out


def _step(x, p, W, M, V, loss_target):
    n_seq, S, D = x.shape
    T = n_seq * S
    depth = p.shape[0]
    n_hg = W["hg_w_in"].shape[0]
    x2 = x.reshape(T, D)
    p3 = p.reshape(depth, T, p.shape[-1])
    tgt = loss_target.reshape(T, D)
    xi, yi, ci = lax.axis_index("x"), lax.axis_index("y"), lax.axis_index("c")
    q_me = 2 * xi + yi
    c_arr = jnp.reshape(ci, (1,)).astype(jnp.int32)
    q_arr = jnp.reshape(q_me, (1,)).astype(jnp.int32)

    groups = {}
    for i in range(depth):
        mix = ("hg_w_in", "hg_w_out") if i % 2 == 0 else ("gm_w_in", "gm_w_out")
        groups[i, "mix"] = [(k, i // 2) for k in mix]
        groups[i, "rest"] = [(k, i) for k in ("ffn_w_gate", "ffn_w_up", "ffn_w_down", "ple_w_proj", "ple_w_gate")]
    G = {k: {} for k in BIG}
    DW = {k: {l: lax.empty((N_CHIPS, 1) + W[k].shape[1:], BF16) for l in range(W[k].shape[0])} for k in BIG}
    in_flight = {}

    casts = {}

    def cast_group(i, part, dep):
        qa = q_arr if dep is None else lax.optimization_barrier((q_arr, dep))[0]
        casts[i, part] = [cast_bf16(W[k], l, qa) for k, l in groups[i, part]]

    def start_gather(i, part, dep):
        bufs = casts.pop((i, part))
        if dep is not None:
            bufs = list(lax.optimization_barrier((tuple(bufs), dep))[0])
        send, recv, arrs, tok = split_start("gather_start_%d_%s" % (i, part), gather_build, bufs)
        in_flight[i, part] = (send, recv, arrs)
        return tok

    forwarding = {}

    def arrive_gather(i, part, after):
        send, recv, arrs = in_flight.pop((i, part))
        arrs = split_wait("gather_wait_%d_%s" % (i, part), gather_build, send, recv, arrs, after)
        send, recv, arrs, tok = split_start("gather_pass_%d_%s" % (i, part), forward_build, arrs)
        forwarding[i, part] = (send, recv, arrs)
        return tok

    def finish_gather(i, part, after):
        send, recv, arrs = forwarding.pop((i, part))
        arrs = split_wait("gather_done_%d_%s" % (i, part), forward_build, send, recv, arrs, after)
        for (k, l), buf in zip(groups[i, part], arrs):
            G[k][l] = buf
        return buf

    def after_token(row_arr, *toks):
        return functools.reduce(lambda u, t: u + t[0:1, 0:1], toks, row_arr)

    cast_group(0, "mix", None)
    tok_mix = start_gather(0, "mix", None)
    cast_group(0, "rest", tok_mix)
    tok_rest = start_gather(0, "rest", None)
    finish_gather(0, "mix", arrive_gather(0, "mix", tok_mix + tok_rest))
    tie = lambda v, tok: lax.optimization_barrier((v, tok))[0]
    me = 4 * xi + 2 * yi + ci
    ln_full = share_with_all("share_ln", _pack([W["gm_ln_g"], W["gm_ln_b"]]), me)
    n_gm, dq = W["gm_ln_g"].shape
    ln_parts = [_unpack(ln_full[4 * qx + 2 * qy + 0], [(n_gm, dq), (n_gm, dq)]) for qx in range(2) for qy in range(2)]
    ln_g = jnp.concatenate([lp[0] for lp in ln_parts], axis=1)
    ln_b = jnp.concatenate([lp[1] for lp in ln_parts], axis=1)

    row = lambda a, i: a[i][None, :]
    f_lb = _make_f_lb(n_hg)
    lb_rows = [row(W["hg_lb_logits"], j) for j in range(n_hg)]
    one = (1, D)
    lb_ins = [(r, one, lambda i: (0, 0)) for r in lb_rows]
    lb_out = bmap_fwd("hg_lower_bounds", f_lb, (1,), lb_ins, [(one, F32, one, lambda i: (0, 0), None)] * (3 * n_hg))

    saved = []
    h = x2
    a = rows_fwd("prenorm", f_prenorm, [h], [row(W["norm_mix_pre"], 0)], [BF16])[0]
    for i in range(depth):
        j = i // 2
        sv = {"h": h, "a": a}
        if i > 0:
            finish_gather(i, "mix", h)
            a = tie(a, arrive_gather(i, "rest", h))
        if i % 2 == 0:
            proj4 = mm_fwd("hg_in", a, G["hg_w_in"], j, "col", parts=True)
            lbp = lb_out[3 * j:3 * j + 3]
            onorm = row(W["hg_out_norm"], j)
            og, states = hgrn_fwd("hgrn_fwd", proj4, *lbp, onorm, n_seq)
            if i == 0:
                og = tie(og, arrive_gather(0, "rest", og))
            m = mm_fwd("hg_out", og, G["hg_w_out"], j, "row")
            sv.update(proj4=proj4, states=states, og=og, lbp=lbp, onorm=onorm)
        else:
            z = mm_fwd("gm_in", a, G["gm_w_in"], j, "col")
            lg, lb_ = row(ln_g, j), row(ln_b, j)
            u, vn = rows_fwd("gm_gelu_ln", f_gm_in, [z], [lg, lb_], [F32, BF16], tm=128)
            ws = W["gm_w_s"][j]
            bs = W["gm_b_s"][j][:, :, None]
            gb = min(GM_BLOCK, S)
            sp_grid = (D // LANES, T // gb)
            sp_ins = [(u, (gb, LANES), lambda g, n: (n, g)), (vn, (gb, LANES), lambda g, n: (n, g)),
                      (ws, (None, GM_CHUNK, GM_CHUNK), lambda g, n: (g, 0, 0)),
                      (bs, (None, GM_CHUNK, 1), lambda g, n: (g, 0, 0))]
            y = bmap_fwd("gm_spatial", f_gm_spatial, sp_grid, sp_ins,
                         [((T, D), BF16, (gb, LANES), lambda g, n: (n, g), None)])[0]
            m = mm_fwd("gm_out", y, G["gm_w_out"], j, "row")
            sv.update(z=z, lg=lg, lb_=lb_, sp_ins=sp_ins, sp_grid=sp_grid, y=y)
        g_post, g_fpre = row(W["norm_mix_post"], i), row(W["norm_ffn_pre"], i)
        arrived = finish_gather(i, "rest", m)
        if i + 1 < depth:
            cast_group(i + 1, "mix", arrived)
            tok_mix = start_gather(i + 1, "mix", None)
            cast_group(i + 1, "rest", tok_mix)
            g_post = after_token(g_post, tok_mix, start_gather(i + 1, "rest", None))
        h1, fin = rows_fwd("mix_post_ffn_pre", f_post_pre, [h, m], [g_post, g_fpre], [F32, BF16])
        gate, up, act = ffn_gate_up("ffn_gate_up", fin, G["ffn_w_gate"], G["ffn_w_up"], i)
        f = mm_fwd("ffn_down", act, G["ffn_w_down"], i, "row")
        g_fpost = row(W["norm_ffn_post"], i)
        if i + 1 < depth:
            g_fpost = after_token(g_fpost, arrive_gather(i + 1, "mix", f))
        h2 = rows_fwd("ffn_post", f_post, [h1, f], [g_fpost], [F32])[0]
        e = mm_fwd("ple_proj", p3, G["ple_w_proj"], i, "col", xl=i)
        zg = mm_fwd("ple_gate", h2, G["ple_w_gate"], i, "row")
        g_ple = row(W["ple_norm"], i)
        sv.update(m=m, h1=h1, fin=fin, gate=gate, up=up, act=act, f=f, h2=h2, e=e, zg=zg,
                  g_post=g_post, g_fpre=g_fpre, g_fpost=g_fpost, g_ple=g_ple)
        if i + 1 < depth:
            g_next = row(W["norm_mix_pre"], i + 1)
            h, a = rows_fwd("ple_next_pre", f_ple_pre, [h2, e, zg], [g_ple, g_next], [F32, BF16])
            sv["g_next"] = g_next
        else:
            h = rows_fwd("ple_last", f_ple, [h2, e, zg], [g_ple], [F32])[0]
        saved.append(sv)

    loss_part, dh = loss_and_grad(h, tgt)
    loss = lax.psum(loss_part[0, 0], ("x", "y", "c"))

    sg = {k: [None] * W[k].shape[0] for k in ("norm_mix_pre", "norm_mix_post", "norm_ffn_pre", "norm_ffn_post", "ple_norm",
                                              "hg_out_norm", "gm_ln_g", "gm_ln_b", "gm_w_s", "gm_b_s")}
    d_lbp = [None] * (3 * n_hg)
    da_next = None
    GRAD = {k: lax.empty(W[k].shape, F32) for k in BIG}
    scattering = {}

    def start_scatter(i, part):
        dws = [DW[k][l] for k, l in groups[i, part]]
        lands = [lax.empty((N_DEV - 1, 1, g.shape[2] // 2, g.shape[3]), BF16) for g in dws]
        send, recv, arrs, tok = split_start("scatter_start_%d_%s" % (i, part), scatter_build, dws + lands, n_peers=N_DEV - 1)
        scattering[i, part] = (send, recv, arrs)
        return tok

    def finish_scatter(i, part, after):
        send, recv, arrs = scattering.pop((i, part))
        arrs = split_wait("scatter_wait_%d_%s" % (i, part), scatter_build, send, recv, arrs, after)
        n = len(groups[i, part])
        for (k, l), own, ld in zip(groups[i, part], arrs[:n], arrs[n:]):
            GRAD[k] = sum_partials(own, ld, GRAD[k], l, q_arr, c_arr)

    tok = None
    for i in reversed(range(depth)):
        j = i // 2
        sv = saved[i]
        if i + 1 < depth:
            g_ple_after = sv["g_ple"] + tok[0:1, 0:1]
            dh2, de, dzg, d_gple, d_gnext = rows_bwd("ple_next_pre_bwd", f_ple_pre, [sv["h2"], sv["e"], sv["zg"]],
                                                     [g_ple_after, sv["g_next"]], [dh, da_next], [F32, BF16, BF16])
            sg["norm_mix_pre"][i + 1] = d_gnext
        else:
            dh2, de, dzg, d_gple = rows_bwd("ple_last_bwd", f_ple, [sv["h2"], sv["e"], sv["zg"]], [sv["g_ple"]], [dh],
                                            [F32, BF16, BF16])
        sg["ple_norm"][i] = d_gple
        DW["ple_w_proj"] = mm_bwd_w("ple_proj_dw", p3, de, DW["ple_w_proj"], i, "col", xl=i)
        DW["ple_w_gate"] = mm_bwd_w("ple_gate_dw", sv["h2"], dzg, DW["ple_w_gate"], i, "row")
        dh2 = mm_bwd_x("ple_gate_dx", dzg, G["ple_w_gate"], i, "row", addend=dh2)
        dh1, df, d_gfpost = rows_bwd("ffn_post_bwd", f_post, [sv["h1"], sv["f"]], [sv["g_fpost"]], [dh2], [F32, BF16])
        sg["norm_ffn_post"][i] = d_gfpost
        dgate, dup = ffn_down_dx("ffn_down_dx", df, G["ffn_w_down"], i, sv["gate"], sv["up"])
        DW["ffn_w_down"] = mm_bwd_w("ffn_down_dw", sv["act"], df, DW["ffn_w_down"], i, "row")
        dfin = ffn_in_dx("ffn_in_dx", dgate, dup, G["ffn_w_gate"], G["ffn_w_up"], i)
        DW["ffn_w_gate"] = mm_bwd_w("ffn_gate_dw", sv["fin"], dgate, DW["ffn_w_gate"], i, "col")
        DW["ffn_w_up"] = mm_bwd_w("ffn_up_dw", sv["fin"], dup, DW["ffn_w_up"], i, "col")
        g_post_after = after_token(sv["g_post"], start_scatter(i, "rest"))
        dh, dm, d_gpost, d_gfpre = rows_bwd("mix_post_ffn_pre_bwd", f_post_pre, [sv["h"], sv["m"]],
                                            [g_post_after, sv["g_fpre"]], [dh1, dfin], [F32, BF16])
        sg["norm_mix_post"][i], sg["norm_ffn_pre"][i] = d_gpost, d_gfpre
        if i % 2 == 0:
            dog = mm_bwd_x("hg_out_dx", dm, G["hg_w_out"], j, "row")
            DW["hg_w_out"] = mm_bwd_w("hg_out_dw", sv["og"], dm, DW["hg_w_out"], j, "row")
            dproj4, d0, d1, d2, d_on = hgrn_bwd("hgrn_bwd", sv["proj4"], sv["states"], dog, *sv["lbp"], sv["onorm"], n_seq)
            d_lbp[3 * j:3 * j + 3] = [d0, d1, d2]
            sg["hg_out_norm"][j] = d_on
            da_next = mm_bwd_x("hg_in_dx", dproj4, G["hg_w_in"], j, "col", parts=True)
            DW["hg_w_in"] = mm_bwd_w("hg_in_dw", sv["a"], dproj4, DW["hg_w_in"], j, "col", parts=True)
        else:
            dy = mm_bwd_x("gm_out_dx", dm, G["gm_w_out"], j, "row")
            DW["gm_w_out"] = mm_bwd_w("gm_out_dw", sv["y"], dm, DW["gm_w_out"], j, "row")
            gb = sv["sp_ins"][0][1][0]
            du, dvn, dws, dbs = bmap_bwd("gm_spatial_bwd", f_gm_spatial, sv["sp_grid"], sv["sp_ins"],
                                         [(dy, (gb, LANES), lambda g, n: (n, g))],
                                         [(0, F32, None), (1, F32, None), (2, F32, 1), (3, F32, 1)])
            sg["gm_w_s"][j], sg["gm_b_s"][j] = dws, dbs[:, :, 0]
            dz, d_lg, d_lb = rows_bwd("gm_gelu_ln_bwd", f_gm_in, [sv["z"]], [sv["lg"], sv["lb_"]], [du, dvn], [BF16], tm=128)
            sg["gm_ln_g"][j], sg["gm_ln_b"][j] = d_lg, d_lb
            da_next = mm_bwd_x("gm_in_dx", dz, G["gm_w_in"], j, "col")
            DW["gm_w_in"] = mm_bwd_w("gm_in_dw", sv["a"], dz, DW["gm_w_in"], j, "col")
        tok = start_scatter(i, "mix")
        if i + 1 < depth:
            finish_scatter(i + 1, "rest", da_next)
            finish_scatter(i + 1, "mix", da_next)
    finish_scatter(0, "rest", tok)
    g0 = after_token(row(W["norm_mix_pre"], 0), tok)
    grad_x, d_g0 = rows_bwd("prenorm_bwd", f_prenorm_thru, [saved[0]["h"]], [g0], [da_next, dh], [F32])
    sg["norm_mix_pre"][0] = d_g0
    d_logits = bmap_bwd("hg_lower_bounds_bwd", f_lb, (1,), lb_ins, [(d, one, lambda i: (0, 0)) for d in d_lbp],
                        [(jj, F32, None) for jj in range(n_hg)])

    small_g = {k: jnp.stack([v.reshape(W[k].shape[1:] if k not in ("gm_ln_g", "gm_ln_b") else (D,)) for v in sg[k]])
               for k in sg}
    small_g["hg_lb_logits"] = jnp.concatenate(d_logits, axis=0)
    small_shapes = [small_g[k].shape for k in SMALL]
    packed_g = _pack([small_g[k] for k in SMALL])
    slots = lax.dynamic_update_slice(jnp.zeros((N_DEV,) + packed_g.shape, F32), packed_g[None], (me, 0, 0))
    s_send, s_recv, slots, tok_s = split_start("share_small_start", share_build, [slots], n_peers=N_DEV - 1)

    out_g, out_d, out_m, out_v = {}, {}, {}, {}
    late = [k for k, _ in groups[0, "mix"]]
    early = [k for k in BIG if k not in late]
    ready = lax.optimization_barrier(tuple(GRAD[k] for k in early) + (tok_s,))[:-1]
    for k, g in zip(early, join_row_halves("join_early", list(ready))):
        out_d[k], out_m[k], out_v[k], out_g[k] = adamw("adamw_" + k, W[k], g, M[k], V[k], with_grad=True)
    slots = split_wait("share_small_wait", share_build, s_send, s_recv, slots, out_v[early[-1]])[0]
    red = sum_devices(slots)
    small_red = dict(zip(SMALL, _unpack(red, small_shapes)))
    for k in ("gm_ln_g", "gm_ln_b"):
        small_red[k] = lax.dynamic_slice_in_dim(small_red[k], q_me * dq, dq, axis=1)
    pk = lambda d: _pack([d[k] for k in SMALL])
    s_delta, s_m, s_v = adamw("adamw_small", pk(W), pk(small_red), pk(M), pk(V))
    shard_shapes = [W[k].shape for k in SMALL]
    out_g.update(small_red)
    for dct, packed in ((out_d, s_delta), (out_m, s_m), (out_v, s_v)):
        dct.update(zip(SMALL, _unpack(packed, shard_shapes)))

    finish_scatter(0, "mix", s_v)
    for k, g in zip(late, join_row_halves("join_late", [GRAD[k] for k in late])):
        out_d[k], out_m[k], out_v[k], out_g[k] = adamw("adamw_" + k, W[k], g, M[k], V[k], with_grad=True)

    outs = [loss, grad_x.reshape(x.shape)]
    for dct in (out_g, out_d, out_m, out_v):
        outs += [dct[k] for k in WEIGHTS]
    return tuple(outs)


def kernel(x, p, hg_w_in, hg_lb_logits, hg_out_norm, hg_w_out, gm_w_in, gm_ln_g, gm_ln_b, gm_w_s, gm_b_s, gm_w_out, norm_mix_pre, norm_mix_post, norm_ffn_pre, norm_ffn_post, ffn_w_gate, ffn_w_up, ffn_w_down, ple_w_proj, ple_w_gate, ple_norm, loss_target, m_hg_w_in, m_hg_lb_logits, m_hg_out_norm, m_hg_w_out, m_gm_w_in, m_gm_ln_g, m_gm_ln_b, m_gm_w_s, m_gm_b_s, m_gm_w_out, m_norm_mix_pre, m_norm_mix_post, m_norm_ffn_pre, m_norm_ffn_post, m_ffn_w_gate, m_ffn_w_up, m_ffn_w_down, m_ple_w_proj, m_ple_w_gate, m_ple_norm, v_hg_w_in, v_hg_lb_logits, v_hg_out_norm, v_hg_w_out, v_gm_w_in, v_gm_ln_g, v_gm_ln_b, v_gm_w_s, v_gm_b_s, v_gm_w_out, v_norm_mix_pre, v_norm_mix_post, v_norm_ffn_pre, v_norm_ffn_post, v_ffn_w_gate, v_ffn_w_up, v_ffn_w_down, v_ple_w_proj, v_ple_w_gate, v_ple_norm):
    W = dict(zip(WEIGHTS, (hg_w_in, hg_lb_logits, hg_out_norm, hg_w_out, gm_w_in, gm_ln_g, gm_ln_b, gm_w_s, gm_b_s, gm_w_out,
                           norm_mix_pre, norm_mix_post, norm_ffn_pre, norm_ffn_post, ffn_w_gate, ffn_w_up, ffn_w_down,
                           ple_w_proj, ple_w_gate, ple_norm)))
    M = dict(zip(WEIGHTS, (m_hg_w_in, m_hg_lb_logits, m_hg_out_norm, m_hg_w_out, m_gm_w_in, m_gm_ln_g, m_gm_ln_b, m_gm_w_s,
                           m_gm_b_s, m_gm_w_out, m_norm_mix_pre, m_norm_mix_post, m_norm_ffn_pre, m_norm_ffn_post,
                           m_ffn_w_gate, m_ffn_w_up, m_ffn_w_down, m_ple_w_proj, m_ple_w_gate, m_ple_norm)))
    V = dict(zip(WEIGHTS, (v_hg_w_in, v_hg_lb_logits, v_hg_out_norm, v_hg_w_out, v_gm_w_in, v_gm_ln_g, v_gm_ln_b, v_gm_w_s,
                           v_gm_b_s, v_gm_w_out, v_norm_mix_pre, v_norm_mix_post, v_norm_ffn_pre, v_norm_ffn_post,
                           v_ffn_w_gate, v_ffn_w_up, v_ffn_w_down, v_ple_w_proj, v_ple_w_gate, v_ple_norm)))
    return _step(x, p, W, M, V, loss_target)
```

```python
import functools

import jax
import jax.numpy as jnp
from jax import lax
from jax.experimental import pallas as pl
from jax.experimental.pallas import tpu as pltpu

F32 = jnp.float32
BF16 = jnp.bfloat16
MESH_ID = pl.DeviceIdType.MESH

LANES = 128
N_CHIPS = 4
N_DEV = 8
VMEM_LIMIT = 56 * 1024 * 1024
HG_SUB = 256
HG_BLOCK = 512
HG_HEADS_PER = 2
GM_CHUNK = 128
GM_BLOCK = 512
PACK_ROWS = 512
LB_FLOOR = 1e-30
EPS = 1e-6
ADAM_LR, ADAM_B1, ADAM_B2, ADAM_EPS, ADAM_WD, ADAM_STEP = 0.001, 0.9, 0.999, 1e-08, 0.01, 10


def _tile(n, pref, mult=LANES):
    if n <= pref:
        return n
    t = (pref // mult) * mult
    while t >= mult:
        if n % t == 0:
            return t
        t -= mult
    return n


def _cp(n_axes):
    return pltpu.CompilerParams(dimension_semantics=("arbitrary",) * n_axes, vmem_limit_bytes=VMEM_LIMIT)


def _dense(block):
    return tuple(b for b in block if b is not None)


def _bmap(name, grid, ins, outs, compute, scalars=(), into=None):
    n_s, n_in = len(scalars), len(ins)
    n_extra = 0 if into is None else 1

    def body(*refs):
        in_refs = refs[n_s:n_s + n_in]
        out_refs = refs[n_s + n_in + n_extra:]
        vals = [r[...] for r in in_refs]
        res = compute(*vals)
        for r, o, spec in zip(out_refs, res, outs):
            keep = spec[4]
            if keep is None:
                r[...] = o.astype(r.dtype)
            else:
                first = functools.reduce(jnp.logical_and, [pl.program_id(a) == 0 for a in range(keep, len(grid))])

                @pl.when(first)
                def _():
                    r[...] = jnp.zeros(r.shape, r.dtype)

                r[...] += o.astype(r.dtype)

    grid_spec = pltpu.PrefetchScalarGridSpec(
        num_scalar_prefetch=n_s, grid=grid,
        in_specs=[pl.BlockSpec(b, m) for _, b, m in ins] + [pl.BlockSpec(memory_space=pl.ANY)] * n_extra,
        out_specs=[pl.BlockSpec(o[2], o[3]) for o in outs])
    return pl.pallas_call(
        body, name=name, grid_spec=grid_spec,
        out_shape=[jax.ShapeDtypeStruct(o[0], o[1]) for o in outs],
        input_output_aliases={n_s + n_in: 0} if n_extra else {},
        compiler_params=_cp(len(grid)),
    )(*scalars, *[a for a, _, _ in ins], *([into] if n_extra else []))


def bmap_fwd(name, fn, grid, ins, outs, scalars=(), into=None):
    return _bmap(name, grid, ins, outs, lambda *v: fn(*[x.astype(F32) for x in v]), scalars, into)


def bmap_bwd(name, fn, grid, ins, cots, grads, scalars=()):
    n_in = len(ins)
    diff = [g[0] for g in grads]
    cot_ins = [c for c in cots if c is not None]

    def compute(*vals):
        xs = [v.astype(F32) for v in vals[:n_in]]
        cvals = list(vals[n_in:])

        def f(*d):
            full = list(xs)
            for i, dv in zip(diff, d):
                full[i] = dv
            return tuple(fn(*full))

        res, pull = jax.vjp(f, *[xs[i] for i in diff])
        cts = []
        for r, c in zip(res, cots):
            cts.append(jnp.zeros_like(r) if c is None else cvals.pop(0).astype(F32))
        return pull(tuple(cts))

    outs = [(ins[i][0].shape, dt, ins[i][1], ins[i][2], keep) for i, dt, keep in grads]
    return _bmap(name, grid, list(ins) + cot_ins, outs, compute, scalars)


def _mm(name, a, b, out_shape, out_dtype, grid, a_spec, b_spec, o_spec, dims, addend=None, alias_out=None):
    nk = grid[2]
    o_dense = _dense(o_spec[0])
    o_dense = (o_dense[0] * o_dense[1], o_dense[2]) if len(o_dense) == 3 else o_dense
    has_add = addend is not None
    has_alias = alias_out is not None

    def body(*refs):
        a_ref, b_ref = refs[0], refs[1]
        pos = 2
        c_ref = None
        if has_add:
            c_ref = refs[pos]
            pos += 1
        if has_alias:
            pos += 1
        o_ref = refs[pos]
        acc_ref = refs[pos + 1] if nk > 1 else None
        bv = b_ref[...]
        if bv.ndim == 3:
            bv = bv.reshape(bv.shape[0] * bv.shape[1], bv.shape[2])
        p = lax.dot_general(a_ref[...].astype(BF16), bv.astype(BF16), (dims, ((), ())), preferred_element_type=F32)

        def finish(total):
            if has_add:
                total = total + c_ref[...].astype(F32)
            o_ref[...] = total.reshape(o_ref.shape).astype(o_ref.dtype)

        if nk == 1:
            finish(p)
        else:
            k = pl.program_id(2)

            @pl.when(k == 0)
            def _():
                acc_ref[...] = p

            @pl.when(jnp.logical_and(k > 0, k < nk - 1))
            def _():
                acc_ref[...] += p

            @pl.when(k == nk - 1)
            def _():
                finish(acc_ref[...] + p)

    in_specs = [pl.BlockSpec(*a_spec), pl.BlockSpec(*b_spec)]
    operands = [a, b]
    if has_add:
        in_specs.append(pl.BlockSpec(o_spec[0], o_spec[1]))
        operands.append(addend)
    aliases = {}
    if has_alias:
        in_specs.append(pl.BlockSpec(memory_space=pl.ANY))
        aliases = {len(operands): 0}
        operands.append(alias_out)
    return pl.pallas_call(
        body, name=name, grid=grid, in_specs=in_specs, out_specs=pl.BlockSpec(*o_spec),
        out_shape=jax.ShapeDtypeStruct(out_shape, out_dtype),
        scratch_shapes=[pltpu.VMEM(o_dense, F32)] if nk > 1 else [],
        input_output_aliases=aliases,
        compiler_params=pltpu.CompilerParams(dimension_semantics=("parallel", "parallel", "arbitrary"),
                                             vmem_limit_bytes=VMEM_LIMIT),
    )(*operands)


NN, NT, TN = ((1,), (0,)), ((1,), (1,)), ((0,), (0,))
TM = 512
TT = 1024
TN_PREF = 1408
WHOLE_K = 2048


def mm_fwd(name, x, wg, l, kind, out_dtype=F32, parts=False, xl=None):
    if isinstance(wg, dict):
        wg, l = wg[l], 0
    _, _, R, C = wg.shape
    T = x.shape[-2]
    tm = _tile(T, TM, 8)
    if kind == "col":
        tn = _tile(C, TN_PREF)
        npc = C // tn
        grid = (T // tm, N_CHIPS * npc, 1)
        a_blk = (tm, R) if xl is None else (None, tm, R)
        a_map = (lambda i, j, k: (i, 0)) if xl is None else (lambda i, j, k: (xl, i, 0))
        b_spec = ((None, None, R, tn), lambda i, j, k: (j // npc, l, 0, j % npc))
        if parts:
            out_shape = (N_CHIPS, T, C)
            o_spec = ((None, tm, tn), lambda i, j, k: (j // npc, i, j % npc))
        else:
            out_shape = (T, N_CHIPS * C)
            o_spec = ((tm, tn), lambda i, j, k: (i, j))
    elif N_CHIPS * R <= WHOLE_K:
        tn = _tile(C, 1024)
        grid = (T // tm, C // tn, 1)
        a_blk = (tm, N_CHIPS * R)
        a_map = lambda i, j, k: (i, 0)
        b_spec = ((N_CHIPS, None, R, tn), lambda i, j, k: (0, l, 0, j))
        out_shape = (T, C)
        o_spec = ((tm, tn), lambda i, j, k: (i, j))
    else:
        tn = _tile(C, 2048)
        grid = (T // tm, C // tn, N_CHIPS)
        a_blk = (tm, R)
        a_map = lambda i, j, k: (i, k)
        b_spec = ((None, None, R, tn), lambda i, j, k: (k, l, 0, j))
        out_shape = (T, C)
        o_spec = ((tm, tn), lambda i, j, k: (i, j))
    return _mm(name, x, wg, out_shape, out_dtype, grid, (a_blk, a_map), b_spec, o_spec, NN)


def mm_bwd_x(name, dy, wg, l, kind, out_dtype=F32, parts=False, addend=None):
    if isinstance(wg, dict):
        wg, l = wg[l], 0
    _, _, R, C = wg.shape
    T = dy.shape[-2]
    tm = _tile(T, TM, 8)
    if kind == "col":
        tk = _tile(C, TN_PREF)
        npc = C // tk
        tno = _tile(R, 2048)
        grid = (T // tm, R // tno, N_CHIPS * npc)
        if parts:
            a_spec = ((None, tm, tk), lambda i, j, k: (k // npc, i, k % npc))
        else:
            a_spec = ((tm, tk), lambda i, j, k: (i, k))
        b_spec = ((None, None, tno, tk), lambda i, j, k: (k // npc, l, j, k % npc))
        out_shape = (T, R)
        o_spec = ((tm, tno), lambda i, j, k: (i, j))
    elif N_CHIPS * R <= WHOLE_K:
        grid = (T // tm, 1, 1)
        a_spec = ((tm, C), lambda i, j, k: (i, 0))
        b_spec = ((N_CHIPS, None, R, C), lambda i, j, k: (0, l, 0, 0))
        out_shape = (T, N_CHIPS * R)
        o_spec = ((tm, N_CHIPS * R), lambda i, j, k: (i, 0))
    else:
        grid = (T // tm, N_CHIPS, 1)
        a_spec = ((tm, C), lambda i, j, k: (i, 0))
        b_spec = ((None, None, R, C), lambda i, j, k: (j, l, 0, 0))
        out_shape = (T, N_CHIPS * R)
        o_spec = ((tm, R), lambda i, j, k: (i, j))
    return _mm(name, dy, wg, out_shape, out_dtype, grid, a_spec, b_spec, o_spec, NT, addend=addend)


def mm_bwd_w(name, x, dy, dwg, l, kind, parts=False, xl=None):
    if isinstance(dwg, dict):
        return {**dwg, l: mm_bwd_w(name, x, dy, dwg[l], 0, kind, parts=parts, xl=xl)}
    _, _, R, C = dwg.shape
    T = dy.shape[-2]
    tt = _tile(T, TT, 16)
    nt = T // tt
    if kind == "col":
        tn = _tile(C, TN_PREF)
        npc = C // tn
        tr = _tile(R, 1024)
        grid = (R // tr, N_CHIPS * npc, nt)
        if xl is None:
            a_spec = ((tt, tr), lambda i, j, t: (t, i))
        else:
            a_spec = ((None, tt, tr), lambda i, j, t: (xl, t, i))
        if parts:
            b_spec = ((None, tt, tn), lambda i, j, t: (j // npc, t, j % npc))
        else:
            b_spec = ((tt, tn), lambda i, j, t: (t, j))
        o_spec = ((None, None, tr, tn), lambda i, j, t: (j // npc, l, i, j % npc))
    elif N_CHIPS * R <= WHOLE_K:
        tn = _tile(C, 1024)
        grid = (1, C // tn, nt)
        a_spec = ((tt, N_CHIPS * R), lambda i, j, t: (t, 0))
        b_spec = ((tt, tn), lambda i, j, t: (t, j))
        o_spec = ((N_CHIPS, None, R, tn), lambda i, j, t: (0, l, 0, j))
    else:
        tn = _tile(C, 1024)
        grid = (N_CHIPS, C // tn, nt)
        a_spec = ((tt, R), lambda i, j, t: (t, i))
        b_spec = ((tt, tn), lambda i, j, t: (t, j))
        o_spec = ((None, None, R, tn), lambda i, j, t: (i, l, 0, j))
    return _mm(name, x, dy, dwg.shape, dwg.dtype, grid, a_spec, b_spec, o_spec, TN, alias_out=dwg)


def _sigmoid(x):
    return 0.5 * jnp.tanh(0.5 * x) + 0.5


def f_swiglu(gate, up):
    return (gate * _sigmoid(gate) * up,)


def ffn_gate_up(name, x, wg_gate, wg_up, l):
    if isinstance(wg_gate, dict):
        wg_gate, wg_up, l = wg_gate[l], wg_up[l], 0
    _, _, R, C = wg_gate.shape
    T = x.shape[0]
    tm = _tile(T, TM, 8)
    tn = _tile(C, TN_PREF)
    npc = C // tn

    def body(x_ref, g_ref, u_ref, gate_ref, up_ref, act_ref):
        xv = x_ref[...].astype(BF16)
        gate = jnp.dot(xv, g_ref[...], preferred_element_type=F32)
        up = jnp.dot(xv, u_ref[...], preferred_element_type=F32)
        gate_ref[...] = gate
        up_ref[...] = up
        act_ref[...] = f_swiglu(gate, up)[0].astype(act_ref.dtype)

    w_spec = pl.BlockSpec((None, None, R, tn), lambda i, j: (j // npc, l, 0, j % npc))
    o_spec = pl.BlockSpec((tm, tn), lambda i, j: (i, j))
    N = N_CHIPS * C
    return pl.pallas_call(
        body, name=name, grid=(T // tm, N_CHIPS * npc),
        in_specs=[pl.BlockSpec((tm, R), lambda i, j: (i, 0)), w_spec, w_spec], out_specs=[o_spec, o_spec, o_spec],
        out_shape=[jax.ShapeDtypeStruct((T, N), F32), jax.ShapeDtypeStruct((T, N), F32), jax.ShapeDtypeStruct((T, N), BF16)],
        compiler_params=_cp(2),
    )(x, wg_gate, wg_up)


def ffn_in_dx(name, dgate, dup, wg_gate, wg_up, l):
    if isinstance(wg_gate, dict):
        wg_gate, wg_up, l = wg_gate[l], wg_up[l], 0
    _, _, R, C = wg_gate.shape
    T = dgate.shape[0]
    tm = _tile(T, TM, 8)
    tk = _tile(C, TN_PREF)
    npc = C // tk
    nk = N_CHIPS * npc

    def body(dg_ref, du_ref, wg_ref, wu_ref, o_ref, acc):
        k = pl.program_id(1)

        @pl.when(k == 0)
        def _():
            acc[...] = jnp.zeros(acc.shape, F32)

        @pl.when(k < nk)
        def _():
            acc[...] += lax.dot_general(dg_ref[...], wg_ref[...], (NT, ((), ())), preferred_element_type=F32)

        @pl.when(k >= nk)
        def _():
            acc[...] += lax.dot_general(du_ref[...], wu_ref[...], (NT, ((), ())), preferred_element_type=F32)

        @pl.when(k == 2 * nk - 1)
        def _():
            o_ref[...] = acc[...]

    first = lambda k: jnp.minimum(k, nk - 1)
    second = lambda k: jnp.maximum(k - nk, 0)
    return pl.pallas_call(
        body, name=name, grid=(T // tm, 2 * nk),
        in_specs=[pl.BlockSpec((tm, tk), lambda i, k: (i, first(k))), pl.BlockSpec((tm, tk), lambda i, k: (i, second(k))),
                  pl.BlockSpec((None, None, R, tk), lambda i, k: (first(k) // npc, l, 0, first(k) % npc)),
                  pl.BlockSpec((None, None, R, tk), lambda i, k: (second(k) // npc, l, 0, second(k) % npc))],
        out_specs=pl.BlockSpec((tm, R), lambda i, k: (i, 0)),
        out_shape=jax.ShapeDtypeStruct((T, R), F32),
        scratch_shapes=[pltpu.VMEM((tm, R), F32)],
        compiler_params=_cp(2),
    )(dgate, dup, wg_gate, wg_up)


def ffn_down_dx(name, df, wg_down, l, gate, up):
    if isinstance(wg_down, dict):
        wg_down, l = wg_down[l], 0
    _, _, R, C = wg_down.shape
    T = df.shape[0]
    tm = _tile(T, TM, 8)

    def body(df_ref, w_ref, gate_ref, up_ref, dg_ref, du_ref):
        dact = lax.dot_general(df_ref[...].astype(BF16), w_ref[...], (NT, ((), ())), preferred_element_type=F32)
        _, pull = jax.vjp(lambda g, u: f_swiglu(g, u)[0], gate_ref[...], up_ref[...])
        dg, du = pull(dact)
        dg_ref[...] = dg.astype(dg_ref.dtype)
        du_ref[...] = du.astype(du_ref.dtype)

    t_spec = pl.BlockSpec((tm, R), lambda i, j: (i, j))
    return pl.pallas_call(
        body, name=name, grid=(T // tm, N_CHIPS),
        in_specs=[pl.BlockSpec((tm, C), lambda i, j: (i, 0)), pl.BlockSpec((None, None, R, C), lambda i, j: (j, l, 0, 0)),
                  t_spec, t_spec],
        out_specs=[t_spec, t_spec],
        out_shape=[jax.ShapeDtypeStruct((T, N_CHIPS * R), BF16)] * 2,
        compiler_params=_cp(2),
    )(df, wg_down, gate, up)


def _rms(x, g):
    return x * lax.rsqrt(jnp.mean(x * x, axis=-1, keepdims=True) + EPS) * g


def f_prenorm(h, g):
    return (_rms(h, g),)


def f_prenorm_thru(h, g):
    return _rms(h, g), h


def f_post_pre(h, m, g_post, g_pre):
    h1 = h + _rms(m, g_post)
    return h1, _rms(h1, g_pre)


def f_post(h1, f, g):
    return (h1 + _rms(f, g),)


def f_ple(h2, e, zg, g):
    return (h2 + _rms(e * _sigmoid(zg), g),)


def f_ple_pre(h2, e, zg, g, g_next):
    h3 = h2 + _rms(e * _sigmoid(zg), g)
    return h3, _rms(h3, g_next)


def _gelu(x):
    return 0.5 * x * (1.0 + lax.erf(x * 0.7071067811865476))


def f_gm_in(z, ln_g, ln_b):
    w = z.shape[-1] // 2
    u = _gelu(z[:, :w])
    v = _gelu(z[:, w:])
    mu = jnp.mean(v, axis=-1, keepdims=True)
    vc = v - mu
    vn = vc * lax.rsqrt(jnp.mean(vc * vc, axis=-1, keepdims=True) + EPS) * ln_g + ln_b
    return u, vn


def f_gm_spatial(u, vn, ws, bs):
    t = lax.broadcasted_iota(jnp.int32, ws.shape, 0)
    s = lax.broadcasted_iota(jnp.int32, ws.shape, 1)
    wm = jnp.where(t >= s, ws, 0.0).astype(BF16)
    ys = []
    for n in range(u.shape[0] // GM_CHUNK):
        rows = slice(n * GM_CHUNK, (n + 1) * GM_CHUNK)
        sv = jnp.dot(wm, vn[rows].astype(BF16), preferred_element_type=F32) + bs
        ys.append(u[rows] * sv)
    return (jnp.concatenate(ys, axis=0) if len(ys) > 1 else ys[0],)


def f_adam(w, g, m, v):
    m = ADAM_B1 * m + (1.0 - ADAM_B1) * g
    v = ADAM_B2 * v + (1.0 - ADAM_B2) * jnp.square(g)
    m_hat = m / (1.0 - ADAM_B1 ** ADAM_STEP)
    v_hat = v / (1.0 - ADAM_B2 ** ADAM_STEP)
    delta = -ADAM_LR * (m_hat / (jnp.sqrt(v_hat) + ADAM_EPS) + ADAM_WD * w)
    return delta, m, v


def _make_f_lb(n_layers):
    def f_lb(*logits):
        mx = functools.reduce(jnp.maximum, logits)
        ex = [jnp.exp(r - mx) for r in logits]
        tot = functools.reduce(lambda a, b: a + b, ex)
        sm = [e / tot for e in ex]
        outs = []
        run = jnp.zeros_like(sm[0])
        for j in range(n_layers):
            if j > 0:
                run = run + sm[j]
            lb = run
            outs += [jnp.log(jnp.maximum(lb, LB_FLOOR)), jnp.log(1.0 - lb), 1.0 - lb]
        return tuple(outs)
    return f_lb


def rows_fwd(name, fn, rows, params, out_dtypes, tm=256):
    T = rows[0].shape[0]
    tm = _tile(T, tm, 16)
    ins = [(r, (tm, r.shape[1]), lambda i: (i, 0)) for r in rows]
    ins += [(p, p.shape, lambda i: (0, 0)) for p in params]
    shapes = jax.eval_shape(lambda *a: fn(*a), *[jax.ShapeDtypeStruct((tm, r.shape[1]), F32) for r in rows],
                            *[jax.ShapeDtypeStruct(p.shape, F32) for p in params])
    outs = [((T, s.shape[1]), dt, (tm, s.shape[1]), lambda i: (i, 0), None) for s, dt in zip(shapes, out_dtypes)]
    return bmap_fwd(name, fn, (T // tm,), ins, outs)


def rows_bwd(name, fn, rows, params, cots, row_grad_dtypes, tm=256):
    T = rows[0].shape[0]
    tm = _tile(T, tm, 16)
    ins = [(r, (tm, r.shape[1]), lambda i: (i, 0)) for r in rows]
    ins += [(p, p.shape, lambda i: (0, 0)) for p in params]
    cts = [None if c is None else (c, (tm, c.shape[1]), lambda i: (i, 0)) for c in cots]
    grads = [(i, dt, None) for i, dt in enumerate(row_grad_dtypes) if dt is not None]
    grads += [(len(rows) + j, F32, 0) for j in range(len(params))]
    return bmap_bwd(name, fn, (T // tm,), ins, cts, grads)


def _log_sigmoid(z):
    return jnp.minimum(z, 0.0) - jnp.log(1.0 + jnp.exp(-jnp.abs(z)))


def _hg_gates(zf, ll0, ll1, oml):
    x2 = ll1 + _log_sigmoid(zf)
    mx = jnp.maximum(ll0, x2)
    g = mx + jnp.log(jnp.exp(ll0 - mx) + jnp.exp(x2 - mx))
    return g, oml * _sigmoid(-zf)


def hg_constants(n):
    levels = n.bit_length() - 1
    r = jnp.arange(n, dtype=jnp.int32)
    bounds = [r] + [((r >> (s + 1)) << (s + 1)) + ((1 << s) - 1) for s in range(levels)]
    sel = jnp.concatenate([(r[None, :] <= bd[:, None]) for bd in bounds], axis=0).astype(BF16)
    later = jnp.stack([((r >> s) & 1) for s in range(levels)])
    sign = jnp.broadcast_to((2 * later - 1).astype(F32)[:, :, None], (levels, n, LANES))
    pair = jnp.stack([((r[:, None] >> (s + 1)) == (r[None, :] >> (s + 1))) & (later[s][:, None] == 1) & (later[s][None, :] == 0)
                      for s in range(levels)]).astype(F32)
    return sel, sel.T, sign, pair


def _dot2(m, x):
    hi = x.astype(BF16)
    lo = (x - hi.astype(F32)).astype(BF16)
    p = jnp.dot(m, jnp.concatenate([hi, lo], axis=1), preferred_element_type=F32)
    w = x.shape[1]
    return p[:, :w] + p[:, w:]


@jax.custom_vjp
def _sel_dot(sel, selt, g):
    return _dot2(sel, g)


def _sel_dot_fwd(sel, selt, g):
    return _dot2(sel, g), (sel, selt)


def _sel_dot_bwd(res, d):
    sel, selt = res
    return jnp.zeros_like(sel), jnp.zeros_like(selt), _dot2(selt, d)


_sel_dot.defvjp(_sel_dot_fwd, _sel_dot_bwd)


def _hg_state(st, zf, zi, ll0, ll1, oml, tri):
    g, k = _hg_gates(zf, ll0, ll1, oml)
    b = _dot2(tri, g)
    tot = jnp.sum(g, axis=0, keepdims=True)
    kd = k * jnp.exp(tot - b)
    return st * jnp.exp(tot) + jnp.dot(zi.T.astype(BF16), kd.astype(BF16), preferred_element_type=F32)


def _hg_step(st, zq, zf, zi, zg, ll0, ll1, oml, onorm, sel, selt, sign, pair):
    n = zq.shape[0]
    levels = n.bit_length() - 1
    q = zq * _sigmoid(zq)
    g, k = _hg_gates(zf, ll0, ll1, oml)
    sums = _sel_dot(sel, selt, g)
    b = sums[:n]
    tot = jnp.sum(g, axis=0, keepdims=True)
    o = lax.dot_general((q * jnp.exp(b)).astype(BF16), st.astype(BF16), (NT, ((), ())), preferred_element_type=F32)
    a = jnp.zeros((n, n), F32)
    for s in range(levels):
        e = jnp.exp(sign[s] * (b - sums[(s + 1) * n:(s + 2) * n]))
        al = lax.dot_general((q * e).astype(BF16), (k * e).astype(BF16), (NT, ((), ())), preferred_element_type=F32)
        a = a + pair[s] * al
    o = o + jnp.dot(a.astype(BF16), zi.astype(BF16), preferred_element_type=F32)
    o = o + jnp.sum(q * k, axis=1, keepdims=True) * zi
    kd = k * jnp.exp(tot - b)
    st_new = st * jnp.exp(tot) + jnp.dot(zi.T.astype(BF16), kd.astype(BF16), preferred_element_type=F32)
    og = _rms(o, onorm) * (zg * _sigmoid(zg))
    return og, st_new


def _whole(arr, n_grid):
    zeros = (0,) * arr.ndim
    return pl.BlockSpec(arr.shape, (lambda h, n: zeros) if n_grid == 2 else (lambda i: zeros))


def _hg_dims(proj4, n_seq):
    _, T, D = proj4.shape
    S = T // n_seq
    hp = HG_HEADS_PER if (D // LANES) % HG_HEADS_PER == 0 else 1
    tb = min(HG_BLOCK, S)
    streams = [(b, hl) for b in range(n_seq) for hl in range(hp)]
    return T, D, S, hp, D // (LANES * hp), LANES * hp, tb, S // tb, tb // HG_SUB, streams


def hgrn_fwd(name, proj4, ll0, ll1, oml, onorm, n_seq):
    T, D, S, hp, n_hg, W, tb, nblk, nsub, streams = _hg_dims(proj4, n_seq)
    ns = len(streams)

    def body(p_ref, ll0_ref, ll1_ref, oml_ref, on_ref, sel_ref, selt_ref, later_ref, pair_ref, og_ref, st_ref, st):
        @pl.when(pl.program_id(1) == 0)
        def _():
            st[...] = jnp.zeros(st.shape, F32)

        st_ref[...] = st[...]
        on = on_ref[...]

        def step(j, carry):
            r = pl.ds(pl.multiple_of(j * HG_SUB, HG_SUB), HG_SUB)
            consts = (sel_ref[...], selt_ref[...], later_ref[...], pair_ref[...])
            args = []
            for si, (b, hl) in enumerate(streams):
                ln = slice(hl * LANES, (hl + 1) * LANES)
                args.append((st[si], p_ref[0, b, r, ln], p_ref[1, b, r, ln], p_ref[2, b, r, ln], p_ref[3, b, r, ln],
                             ll0_ref[:, ln], ll1_ref[:, ln], oml_ref[:, ln], on) + consts)
            res = [_hg_step(*a) for a in args]
            for si, (b, hl) in enumerate(streams):
                og_ref[b, r, hl * LANES:(hl + 1) * LANES] = res[si][0].astype(og_ref.dtype)
                st[si] = res[si][1]
            return carry

        lax.fori_loop(0, nsub, step, 0)

    vec = pl.BlockSpec((1, W), lambda h, n: (0, h))
    consts = hg_constants(HG_SUB)
    og, states = pl.pallas_call(
        body, name=name, grid=(n_hg, nblk),
        in_specs=[pl.BlockSpec((4, n_seq, tb, W), lambda h, n: (0, 0, n, h)), vec, vec, vec,
                  pl.BlockSpec((1, LANES), lambda h, n: (0, 0))] + [_whole(c, 2) for c in consts],
        out_specs=[pl.BlockSpec((n_seq, tb, W), lambda h, n: (0, n, h)),
                   pl.BlockSpec((None, None, ns, LANES, LANES), lambda h, n: (h, n, 0, 0, 0))],
        out_shape=[jax.ShapeDtypeStruct((n_seq, S, D), BF16),
                   jax.ShapeDtypeStruct((n_hg, nblk, ns, LANES, LANES), F32)],
        scratch_shapes=[pltpu.VMEM((ns, LANES, LANES), F32)],
        compiler_params=_cp(2),
    )(proj4.reshape(4, n_seq, S, D), ll0, ll1, oml, onorm, *consts)
    return og.reshape(T, D), states


def hgrn_bwd(name, proj4, states, dog, ll0, ll1, oml, onorm, n_seq):
    T, D, S, hp, n_hg, W, tb, nblk, nsub, streams = _hg_dims(proj4, n_seq)
    ns = len(streams)

    def body(p_ref, st_ref, dog_ref, ll0_ref, ll1_ref, oml_ref, on_ref, sel_ref, selt_ref, later_ref, pair_ref,
             dp_ref, dll0_ref, dll1_ref, doml_ref, don_ref, sbuf, dst):
        n_id = pl.program_id(1)

        @pl.when(n_id == 0)
        def _():
            dst[...] = jnp.zeros(dst.shape, F32)
            for ref in (dll0_ref, dll1_ref, doml_ref):
                ref[...] = jnp.zeros(ref.shape, F32)

        @pl.when(jnp.logical_and(n_id == 0, pl.program_id(0) == 0))
        def _():
            don_ref[...] = jnp.zeros(don_ref.shape, F32)

        on = on_ref[...]

        def fwd(j, carry):
            r = pl.ds(pl.multiple_of(j * HG_SUB, HG_SUB), HG_SUB)
            tri = sel_ref[0:HG_SUB, :]
            args = []
            for si, (b, hl) in enumerate(streams):
                ln = slice(hl * LANES, (hl + 1) * LANES)
                args.append((carry[si], p_ref[1, b, r, ln], p_ref[2, b, r, ln],
                             ll0_ref[:, ln], ll1_ref[:, ln], oml_ref[:, ln], tri))
            for si in range(ns):
                sbuf[si, j] = carry[si]
            return tuple(_hg_state(*a) for a in args)

        lax.fori_loop(0, nsub, fwd, tuple(st_ref[si] for si in range(ns)))

        def bwd(jj, carry):
            j = nsub - 1 - jj
            r = pl.ds(pl.multiple_of(j * HG_SUB, HG_SUB), HG_SUB)
            args, cts = [], []
            for si, (b, hl) in enumerate(streams):
                ln = slice(hl * LANES, (hl + 1) * LANES)
                args.append((sbuf[si, j], p_ref[0, b, r, ln], p_ref[1, b, r, ln], p_ref[2, b, r, ln],
                             p_ref[3, b, r, ln], ll0_ref[:, ln], ll1_ref[:, ln], oml_ref[:, ln], on))
                cts.append((dog_ref[b, r, ln].astype(F32), dst[si]))
            consts = (sel_ref[...], selt_ref[...], later_ref[...], pair_ref[...])
            step_fn = lambda *a: _hg_step(*a, *consts)
            ds = [jax.vjp(step_fn, *a)[1](ct) for a, ct in zip(args, cts)]
            d_on = carry
            for si, (b, hl) in enumerate(streams):
                ln = slice(hl * LANES, (hl + 1) * LANES)
                d = ds[si]
                dst[si] = d[0]
                for part in range(4):
                    dp_ref[part, b, r, ln] = d[1 + part].astype(dp_ref.dtype)
                dll0_ref[:, ln] += d[5]
                dll1_ref[:, ln] += d[6]
                doml_ref[:, ln] += d[7]
                d_on = d_on + d[8]
            return d_on

        don_ref[...] += lax.fori_loop(0, nsub, bwd, jnp.zeros((1, LANES), F32))

    last = nblk - 1
    vec = pl.BlockSpec((1, W), lambda h, n: (0, h))
    one = pl.BlockSpec((1, LANES), lambda h, n: (0, 0))
    consts = hg_constants(HG_SUB)
    dproj, d0, d1, d2, d_on = pl.pallas_call(
        body, name=name, grid=(n_hg, nblk),
        in_specs=[pl.BlockSpec((4, n_seq, tb, W), lambda h, n: (0, 0, last - n, h)),
                  pl.BlockSpec((None, None, ns, LANES, LANES), lambda h, n: (h, last - n, 0, 0, 0)),
                  pl.BlockSpec((n_seq, tb, W), lambda h, n: (0, last - n, h)), vec, vec, vec, one]
        + [_whole(c, 2) for c in consts],
        out_specs=[pl.BlockSpec((4, n_seq, tb, W), lambda h, n: (0, 0, last - n, h)), vec, vec, vec, one],
        out_shape=[jax.ShapeDtypeStruct((4, n_seq, S, D), BF16)] + [jax.ShapeDtypeStruct((1, D), F32)] * 3
        + [jax.ShapeDtypeStruct((1, LANES), F32)],
        scratch_shapes=[pltpu.VMEM((ns, nsub, LANES, LANES), F32), pltpu.VMEM((ns, LANES, LANES), F32)],
        compiler_params=_cp(2),
    )(proj4.reshape(4, n_seq, S, D), states, dog.reshape(n_seq, S, D), ll0, ll1, oml, onorm, *consts)
    return dproj.reshape(4, T, D), d0, d1, d2, d_on


def _place():
    x, y, c = lax.axis_index("x"), lax.axis_index("y"), lax.axis_index("c")
    chips = [(1 - x, y), (x, 1 - y), (1 - x, 1 - y)]
    return x, y, c, chips


ANY = pl.BlockSpec(memory_space=pl.ANY)


def _comm_call(name, body, ins, out_shapes, sems, aliases=None):
    return pl.pallas_call(
        body, name=name, in_specs=[ANY] * len(ins), out_specs=[ANY] * len(out_shapes),
        out_shape=out_shapes, scratch_shapes=sems, input_output_aliases=aliases or {},
        compiler_params=pltpu.CompilerParams(has_side_effects=True),
    )(*ins)


HBM_SPEC = pl.BlockSpec(memory_space=pltpu.HBM)
SEM_SPEC = pl.BlockSpec(memory_space=pltpu.SEMAPHORE)
SPLIT_EFFECT = pltpu.SideEffectType.DATAFLOW_SIDE_EFFECTING
N_PEER_CHIPS = 3


def split_start(name, build, arrays, n_peers=N_PEER_CHIPS):
    n = len(arrays)

    def body(*refs):
        send, recv = refs[n], refs[n + 1]
        token = refs[2 * n + 2]
        starts, _ = build(refs[:n], send, recv)
        for cp in starts:
            cp.start()
        token[...] = jnp.zeros_like(token)

    res = pl.pallas_call(
        body, name=name,
        out_shape=(pltpu.SemaphoreType.DMA((n_peers,)), pltpu.SemaphoreType.DMA((n_peers,)),
                   *[pltpu.HBM(a.shape, a.dtype) for a in arrays], jax.ShapeDtypeStruct((8, LANES), F32)),
        in_specs=[HBM_SPEC] * n,
        out_specs=(SEM_SPEC, SEM_SPEC, *[HBM_SPEC] * n, pl.BlockSpec(memory_space=pltpu.VMEM)),
        input_output_aliases={i: 2 + i for i in range(n)},
        compiler_params=pltpu.CompilerParams(has_side_effects=SPLIT_EFFECT),
    )(*[pltpu.with_memory_space_constraint(a, pltpu.HBM) for a in arrays])
    return res[0], res[1], list(res[2:2 + n]), res[2 + n]


def split_wait(name, build, send, recv, arrays, after):
    n = len(arrays)

    def body(*refs):
        starts, arrivals = build(refs[:n], refs[n], refs[n + 1])
        for cp in starts:
            cp.wait_send()
        for cp in arrivals:
            cp.wait_recv()

    return list(pl.pallas_call(
        body, name=name, out_shape=tuple(pltpu.HBM(a.shape, a.dtype) for a in arrays),
        in_specs=[HBM_SPEC] * n + [SEM_SPEC, SEM_SPEC, ANY], out_specs=tuple([HBM_SPEC] * n),
        input_output_aliases={i: i for i in range(n)},
        compiler_params=pltpu.CompilerParams(has_side_effects=SPLIT_EFFECT),
    )(*arrays, send, recv, after))


def _row_half(ref, dim, who):
    rh = ref.shape[dim] // 2
    return pl.ds(who * rh, rh)


def gather_build(refs, send, recv):
    x, y, c, chips = _place()
    q = 2 * x + y
    starts, arrivals = [], []
    for buf in refs:
        rows = _row_half(buf, 2, c)
        for j, (px, py) in enumerate(chips):
            mine, got = buf.at[q, :, rows], buf.at[2 * px + py, :, rows]
            starts.append(pltpu.make_async_remote_copy(src_ref=mine, dst_ref=mine, send_sem=send.at[j], recv_sem=recv.at[j],
                                                       device_id=(px, py, c), device_id_type=MESH_ID))
            arrivals.append(pltpu.make_async_remote_copy(src_ref=got, dst_ref=got, send_sem=send.at[j], recv_sem=recv.at[j],
                                                         device_id=(px, py, c), device_id_type=MESH_ID))
    return starts, arrivals


def forward_build(refs, send, recv):
    x, y, c, chips = _place()
    sib = (x, y, 1 - c)
    starts, arrivals = [], []
    for buf in refs:
        for j, (px, py) in enumerate(chips):
            got = buf.at[2 * px + py, :, _row_half(buf, 2, c)]
            theirs = buf.at[2 * px + py, :, _row_half(buf, 2, 1 - c)]
            starts.append(pltpu.make_async_remote_copy(src_ref=got, dst_ref=got, send_sem=send.at[j], recv_sem=recv.at[j],
                                                       device_id=sib, device_id_type=MESH_ID))
            arrivals.append(pltpu.make_async_remote_copy(src_ref=theirs, dst_ref=theirs, send_sem=send.at[j], recv_sem=recv.at[j],
                                                         device_id=sib, device_id_type=MESH_ID))
    return starts, arrivals


def scatter_build(refs, send, recv):
    n = len(refs) // 2
    x, y, c, _ = _place()
    starts, arrivals = [], []
    for a in range(n):
        src, dst = refs[a], refs[n + a]
        for k in range(1, N_DEV):
            px, py, pc = x ^ (k >> 2), y ^ ((k >> 1) & 1), c ^ (k & 1)
            theirs = src.at[2 * px + py, :, _row_half(src, 2, pc)]
            starts.append(pltpu.make_async_remote_copy(src_ref=theirs, dst_ref=dst.at[k - 1], send_sem=send.at[k - 1],
                                                       recv_sem=recv.at[k - 1], device_id=(px, py, pc), device_id_type=MESH_ID))
            arrivals.append(pltpu.make_async_remote_copy(src_ref=dst.at[k - 1], dst_ref=dst.at[k - 1], send_sem=send.at[k - 1],
                                                         recv_sem=recv.at[k - 1], device_id=(px, py, pc), device_id_type=MESH_ID))
    return starts, arrivals


def share_build(refs, send, recv):
    dst = refs[0]
    x, y, c, _ = _place()
    mine = dst.at[4 * x + 2 * y + c]
    starts, arrivals = [], []
    for k in range(1, N_DEV):
        px, py, pc = x ^ (k >> 2), y ^ ((k >> 1) & 1), c ^ (k & 1)
        got = dst.at[4 * px + 2 * py + pc]
        starts.append(pltpu.make_async_remote_copy(src_ref=mine, dst_ref=mine, send_sem=send.at[k - 1], recv_sem=recv.at[k - 1],
                                                   device_id=(px, py, pc), device_id_type=MESH_ID))
        arrivals.append(pltpu.make_async_remote_copy(src_ref=got, dst_ref=got, send_sem=send.at[k - 1], recv_sem=recv.at[k - 1],
                                                     device_id=(px, py, pc), device_id_type=MESH_ID))
    return starts, arrivals


def join_build(refs, send, recv, layers):
    x, y, c, _ = _place()
    sib = (x, y, 1 - c)
    starts, arrivals = [], []
    for buf, l in zip(refs, layers):
        mine, theirs = buf.at[l, _row_half(buf, 1, c)], buf.at[l, _row_half(buf, 1, 1 - c)]
        starts.append(pltpu.make_async_remote_copy(src_ref=mine, dst_ref=mine, send_sem=send.at[0], recv_sem=recv.at[0],
                                                   device_id=sib, device_id_type=MESH_ID))
        arrivals.append(pltpu.make_async_remote_copy(src_ref=theirs, dst_ref=theirs, send_sem=send.at[0], recv_sem=recv.at[0],
                                                     device_id=sib, device_id_type=MESH_ID))
    return starts, arrivals


def join_row_halves(name, bufs):
    n = len(bufs)

    def body(*refs):
        dst = refs[n:2 * n]
        send, recv = refs[2 * n:]
        x, y, c, _ = _place()
        cps = []
        for a in range(n):
            mine = dst[a].at[:, _row_half(dst[a], 1, c)]
            cps.append(pltpu.make_async_remote_copy(src_ref=mine, dst_ref=mine, send_sem=send.at[a], recv_sem=recv.at[a],
                                                    device_id=(x, y, 1 - c), device_id_type=MESH_ID))
        for cp in cps:
            cp.start()
        for a in range(n):
            theirs = dst[a].at[:, _row_half(dst[a], 1, 1 - c)]
            pltpu.make_async_remote_copy(src_ref=theirs, dst_ref=theirs, send_sem=send.at[a], recv_sem=recv.at[a],
                                         device_id=(x, y, 1 - c), device_id_type=MESH_ID).wait_recv()
        for cp in cps:
            cp.wait_send()

    outs = [jax.ShapeDtypeStruct(b.shape, b.dtype) for b in bufs]
    sems = [pltpu.SemaphoreType.DMA((n,))] * 2
    return _comm_call(name, body, bufs, outs, sems, aliases={a: a for a in range(n)})


def share_with_all(name, packed, me):
    slots = lax.dynamic_update_slice(jnp.zeros((N_DEV,) + packed.shape, packed.dtype), packed[None], (me, 0, 0))

    def body(_, dst, send, recv):
        x, y, c, _ = _place()
        me = 4 * x + 2 * y + c
        cps = []
        for k in range(1, N_DEV):
            px, py, pc = x ^ (k >> 2), y ^ ((k >> 1) & 1), c ^ (k & 1)
            cps.append(pltpu.make_async_remote_copy(src_ref=dst.at[me], dst_ref=dst.at[me], send_sem=send.at[k - 1],
                                                    recv_sem=recv.at[k - 1], device_id=(px, py, pc), device_id_type=MESH_ID))
        for cp in cps:
            cp.start()
        for k in range(1, N_DEV):
            px, py, pc = x ^ (k >> 2), y ^ ((k >> 1) & 1), c ^ (k & 1)
            got = dst.at[4 * px + 2 * py + pc]
            pltpu.make_async_remote_copy(src_ref=got, dst_ref=got, send_sem=send.at[k - 1], recv_sem=recv.at[k - 1],
                                         device_id=(px, py, pc), device_id_type=MESH_ID).wait_recv()
        for cp in cps:
            cp.wait_send()

    outs = [jax.ShapeDtypeStruct(slots.shape, slots.dtype)]
    sems = [pltpu.SemaphoreType.DMA((N_DEV - 1,))] * 2
    return _comm_call(name, body, [slots], outs, sems, aliases={0: 0})[0]


def _w_tiles(R, C):
    return _tile(R, max(16, (1 << 20) // (4 * C) // 16 * 16), 16)


def cast_bf16(w, l, q_arr):
    _, R, C = w.shape
    tr = _w_tiles(R, C)
    ins = [(w, (None, tr, C), lambda r, q: (l, r, 0))]
    outs = [((N_CHIPS, 1, R, C), BF16, (None, None, tr, C), lambda r, q: (q[0], 0, r, 0), None)]
    return bmap_fwd("cast_bf16", lambda a: (a,), (R // tr,), ins, outs, scalars=(q_arr,))[0]


def sum_partials(own, landed, into, l, q_arr, c_arr):
    n_land, _, rh, C = landed.shape
    tr = _w_tiles(rh, C)
    nb = rh // tr
    blk = (None, None, tr, C)
    ins = [(own, blk, lambda r, q, c: (q[0], 0, c[0] * nb + r, 0))]
    ins += [(landed, blk, (lambda r, q, c, kk=kk: (kk, 0, r, 0))) for kk in range(n_land)]
    outs = [(into.shape, F32, (None, tr, C), lambda r, q, c: (l, c[0] * nb + r, 0), None)]
    return bmap_fwd("sum_partials", lambda *t: (functools.reduce(lambda u, v: u + v, t),), (nb,), ins, outs,
                    scalars=(q_arr, c_arr), into=into)[0]


def sum_devices(slots):
    nd, NR, C = slots.shape
    tr = _tile(NR, 512, 8)
    ins = [(slots, (None, tr, C), (lambda r, dd=dd: (dd, r, 0))) for dd in range(nd)]
    outs = [((NR, C), F32, (tr, C), lambda r: (r, 0), None)]
    return bmap_fwd("sum_devices", lambda *a: (functools.reduce(lambda u, v: u + v, a),), (NR // tr,), ins, outs)[0]


def adamw(name, w, g, m, v, with_grad=False):
    if w.ndim == 2:
        R, C = w.shape
        tr = _w_tiles(R, C)
        spec = ((tr, C), lambda r: (r, 0))
        grid = (R // tr,)
    else:
        L, R, C = w.shape
        tr = _w_tiles(R, C)
        spec = ((None, tr, C), lambda l, r: (l, r, 0))
        grid = (L, R // tr)
    ins = [(a,) + spec for a in (w, g, m, v)]
    outs = [(w.shape, F32) + spec + (None,)] * (4 if with_grad else 3)
    fn = (lambda a, b, c, d: f_adam(a, b, c, d) + (b,)) if with_grad else f_adam
    return bmap_fwd(name, fn, grid, ins, outs)


def loss_and_grad(h, target):
    T, D = h.shape
    tm = _tile(T, 256, 8)

    def fn(hv, tv):
        d = hv - tv
        return jnp.sum(d * d, keepdims=True).reshape(1, 1) * (0.5 / D), d * (1.0 / D)

    ins = [(h, (tm, D), lambda i: (i, 0)), (target, (tm, D), lambda i: (i, 0))]
    outs = [((1, 1), F32, (1, 1), lambda i: (0, 0), 0), ((T, D), F32, (tm, D), lambda i: (i, 0), None)]
    return bmap_fwd("loss_and_grad", fn, (T // tm,), ins, outs)


BIG = ("hg_w_in", "hg_w_out", "gm_w_in", "gm_w_out", "ffn_w_gate", "ffn_w_up", "ffn_w_down", "ple_w_proj", "ple_w_gate")
KIND = {"hg_w_in": "col", "hg_w_out": "row", "gm_w_in": "col", "gm_w_out": "row", "ffn_w_gate": "col",
        "ffn_w_up": "col", "ffn_w_down": "row", "ple_w_proj": "col", "ple_w_gate": "row"}
SMALL = ("hg_lb_logits", "hg_out_norm", "gm_ln_g", "gm_ln_b", "gm_w_s", "gm_b_s", "norm_mix_pre", "norm_mix_post",
         "norm_ffn_pre", "norm_ffn_post", "ple_norm")
WEIGHTS = ("hg_w_in", "hg_lb_logits", "hg_out_norm", "hg_w_out", "gm_w_in", "gm_ln_g", "gm_ln_b", "gm_w_s", "gm_b_s",
           "gm_w_out", "norm_mix_pre", "norm_mix_post", "norm_ffn_pre", "norm_ffn_post", "ffn_w_gate", "ffn_w_up",
           "ffn_w_down", "ple_w_proj", "ple_w_gate", "ple_norm")


def _pack(arrs):
    rows = []
    for a in arrs:
        flat = a.reshape(-1)
        pad = (-flat.shape[0]) % (8 * LANES)
        rows.append(jnp.pad(flat, (0, pad)).reshape(-1, LANES))
    n_rows = sum(r.shape[0] for r in rows)
    rows.append(jnp.zeros(((-n_rows) % PACK_ROWS, LANES), F32))
    return jnp.concatenate(rows, axis=0)


def _unpack(packed, shapes):
    out, r = [], 0
    for s in shapes:
        size = 1
        for d in s:
            size *= d
        nr = -(-size // (8 * LANES)) * 8
        out.append(packed[r:r + nr].reshape(-1)[:size].reshape(s))
        r += nr
    return out


def _step(x, p, W, M, V, loss_target):
    n_seq, S, D = x.shape
    T = n_seq * S
    depth = p.shape[0]
    n_hg = W["hg_w_in"].shape[0]
    x2 = x.reshape(T, D)
    p3 = p.reshape(depth, T, p.shape[-1])
    tgt = loss_target.reshape(T, D)
    xi, yi, ci = lax.axis_index("x"), lax.axis_index("y"), lax.axis_index("c")
    q_me = 2 * xi + yi
    c_arr = jnp.reshape(ci, (1,)).astype(jnp.int32)
    q_arr = jnp.reshape(q_me, (1,)).astype(jnp.int32)

    groups = {}
    for i in range(depth):
        mix = ("hg_w_in", "hg_w_out") if i % 2 == 0 else ("gm_w_in", "gm_w_out")
        groups[i, "mix"] = [(k, i // 2) for k in mix]
        groups[i, "rest"] = [(k, i) for k in ("ffn_w_gate", "ffn_w_up", "ffn_w_down", "ple_w_proj", "ple_w_gate")]
    G = {k: {} for k in BIG}
    DW = {k: {l: lax.empty((N_CHIPS, 1) + W[k].shape[1:], BF16) for l in range(W[k].shape[0])} for k in BIG}
    in_flight = {}

    casts = {}

    def cast_group(i, part, dep):
        qa = q_arr if dep is None else lax.optimization_barrier((q_arr, dep))[0]
        casts[i, part] = [cast_bf16(W[k], l, qa) for k, l in groups[i, part]]

    def start_gather(i, part, dep):
        bufs = casts.pop((i, part))
        if dep is not None:
            bufs = list(lax.optimization_barrier((tuple(bufs), dep))[0])
        send, recv, arrs, tok = split_start("gather_start_%d_%s" % (i, part), gather_build, bufs)
        in_flight[i, part] = (send, recv, arrs)
        return tok

    forwarding = {}

    def arrive_gather(i, part, after):
        send, recv, arrs = in_flight.pop((i, part))
        arrs = split_wait("gather_wait_%d_%s" % (i, part), gather_build, send, recv, arrs, after)
        send, recv, arrs, tok = split_start("gather_pass_%d_%s" % (i, part), forward_build, arrs)
        forwarding[i, part] = (send, recv, arrs)
        return tok

    def finish_gather(i, part, after):
        send, recv, arrs = forwarding.pop((i, part))
        arrs = split_wait("gather_done_%d_%s" % (i, part), forward_build, send, recv, arrs, after)
        for (k, l), buf in zip(groups[i, part], arrs):
            G[k][l] = buf
        return buf

    def after_token(row_arr, *toks):
        return functools.reduce(lambda u, t: u + t[0:1, 0:1], toks, row_arr)

    cast_group(0, "mix", None)
    tok_mix = start_gather(0, "mix", None)
    cast_group(0, "rest", tok_mix)
    tok_rest = start_gather(0, "rest", None)
    finish_gather(0, "mix", arrive_gather(0, "mix", tok_mix + tok_rest))
    tie = lambda v, tok: lax.optimization_barrier((v, tok))[0]
    me = 4 * xi + 2 * yi + ci
    ln_full = share_with_all("share_ln", _pack([W["gm_ln_g"], W["gm_ln_b"]]), me)
    n_gm, dq = W["gm_ln_g"].shape
    ln_parts = [_unpack(ln_full[4 * qx + 2 * qy + 0], [(n_gm, dq), (n_gm, dq)]) for qx in range(2) for qy in range(2)]
    ln_g = jnp.concatenate([lp[0] for lp in ln_parts], axis=1)
    ln_b = jnp.concatenate([lp[1] for lp in ln_parts], axis=1)

    row = lambda a, i: a[i][None, :]
    f_lb = _make_f_lb(n_hg)
    lb_rows = [row(W["hg_lb_logits"], j) for j in range(n_hg)]
    one = (1, D)
    lb_ins = [(r, one, lambda i: (0, 0)) for r in lb_rows]
    lb_out = bmap_fwd("hg_lower_bounds", f_lb, (1,), lb_ins, [(one, F32, one, lambda i: (0, 0), None)] * (3 * n_hg))

    saved = []
    h = x2
    a = rows_fwd("prenorm", f_prenorm, [h], [row(W["norm_mix_pre"], 0)], [BF16])[0]
    for i in range(depth):
        j = i // 2
        sv = {"h": h, "a": a}
        if i > 0:
            finish_gather(i, "mix", h)
            a = tie(a, arrive_gather(i, "rest", h))
        if i % 2 == 0:
            proj4 = mm_fwd("hg_in", a, G["hg_w_in"], j, "col", parts=True)
            lbp = lb_out[3 * j:3 * j + 3]
            onorm = row(W["hg_out_norm"], j)
            og, states = hgrn_fwd("hgrn_fwd", proj4, *lbp, onorm, n_seq)
            if i == 0:
                og = tie(og, arrive_gather(0, "rest", og))
            m = mm_fwd("hg_out", og, G["hg_w_out"], j, "row")
            sv.update(proj4=proj4, states=states, og=og, lbp=lbp, onorm=onorm)
        else:
            z = mm_fwd("gm_in", a, G["gm_w_in"], j, "col")
            lg, lb_ = row(ln_g, j), row(ln_b, j)
            u, vn = rows_fwd("gm_gelu_ln", f_gm_in, [z], [lg, lb_], [F32, BF16], tm=128)
            ws = W["gm_w_s"][j]
            bs = W["gm_b_s"][j][:, :, None]
            gb = min(GM_BLOCK, S)
            sp_grid = (D // LANES, T // gb)
            sp_ins = [(u, (gb, LANES), lambda g, n: (n, g)), (vn, (gb, LANES), lambda g, n: (n, g)),
                      (ws, (None, GM_CHUNK, GM_CHUNK), lambda g, n: (g, 0, 0)),
                      (bs, (None, GM_CHUNK, 1), lambda g, n: (g, 0, 0))]
            y = bmap_fwd("gm_spatial", f_gm_spatial, sp_grid, sp_ins,
                         [((T, D), BF16, (gb, LANES), lambda g, n: (n, g), None)])[0]
            m = mm_fwd("gm_out", y, G["gm_w_out"], j, "row")
            sv.update(z=z, lg=lg, lb_=lb_, sp_ins=sp_ins, sp_grid=sp_grid, y=y)
        g_post, g_fpre = row(W["norm_mix_post"], i), row(W["norm_ffn_pre"], i)
        arrived = finish_gather(i, "rest", m)
        if i + 1 < depth:
            cast_group(i + 1, "mix", arrived)
            tok_mix = start_gather(i + 1, "mix", None)
            cast_group(i + 1, "rest", tok_mix)
            g_post = after_token(g_post, tok_mix, start_gather(i + 1, "rest", None))
        h1, fin = rows_fwd("mix_post_ffn_pre", f_post_pre, [h, m], [g_post, g_fpre], [F32, BF16])
        gate, up, act = ffn_gate_up("ffn_gate_up", fin, G["ffn_w_gate"], G["ffn_w_up"], i)
        f = mm_fwd("ffn_down", act, G["ffn_w_down"], i, "row")
        g_fpost = row(W["norm_ffn_post"], i)
        if i + 1 < depth:
            g_fpost = after_token(g_fpost, arrive_gather(i + 1, "mix", f))
        h2 = rows_fwd("ffn_post", f_post, [h1, f], [g_fpost], [F32])[0]
        e = mm_fwd("ple_proj", p3, G["ple_w_proj"], i, "col", xl=i)
        zg = mm_fwd("ple_gate", h2, G["ple_w_gate"], i, "row")
        g_ple = row(W["ple_norm"], i)
        sv.update(m=m, h1=h1, fin=fin, gate=gate, up=up, act=act, f=f, h2=h2, e=e, zg=zg,
                  g_post=g_post, g_fpre=g_fpre, g_fpost=g_fpost, g_ple=g_ple)
        if i + 1 < depth:
            g_next = row(W["norm_mix_pre"], i + 1)
            h, a = rows_fwd("ple_next_pre", f_ple_pre, [h2, e, zg], [g_ple, g_next], [F32, BF16])
            sv["g_next"] = g_next
        else:
            h = rows_fwd("ple_last", f_ple, [h2, e, zg], [g_ple], [F32])[0]
        saved.append(sv)

    loss_part, dh = loss_and_grad(h, tgt)
    loss = lax.psum(loss_part[0, 0], ("x", "y", "c"))

    sg = {k: [None] * W[k].shape[0] for k in ("norm_mix_pre", "norm_mix_post", "norm_ffn_pre", "norm_ffn_post", "ple_norm",
                                              "hg_out_norm", "gm_ln_g", "gm_ln_b", "gm_w_s", "gm_b_s")}
    d_lbp = [None] * (3 * n_hg)
    da_next = None
    GRAD = {k: lax.empty(W[k].shape, F32) for k in BIG}
    scattering = {}

    def start_scatter(i, part):
        dws = [DW[k][l] for k, l in groups[i, part]]
        lands = [lax.empty((N_DEV - 1, 1, g.shape[2] // 2, g.shape[3]), BF16) for g in dws]
        send, recv, arrs, tok = split_start("scatter_start_%d_%s" % (i, part), scatter_build, dws + lands, n_peers=N_DEV - 1)
        scattering[i, part] = (send, recv, arrs)
        return tok

    def finish_scatter(i, part, after):
        send, recv, arrs = scattering.pop((i, part))
        arrs = split_wait("scatter_wait_%d_%s" % (i, part), scatter_build, send, recv, arrs, after)
        n = len(groups[i, part])
        for (k, l), own, ld in zip(groups[i, part], arrs[:n], arrs[n:]):
            GRAD[k] = sum_partials(own, ld, GRAD[k], l, q_arr, c_arr)
        kinds = [k for k, _ in groups[i, part]]
        build = functools.partial(join_build, layers=tuple(l for _, l in groups[i, part]))
        send, recv, arrs, tok_j = split_start("join_start_%d_%s" % (i, part), build, [GRAD[k] for k in kinds], n_peers=1)
        GRAD.update(zip(kinds, arrs))
        joining.append(("join_wait_%d_%s" % (i, part), build, send, recv, kinds))
        return tok_j

    def finish_joins(after):
        while joining:
            name, build, send, recv, kinds = joining.pop(0)
            GRAD.update(zip(kinds, split_wait(name, build, send, recv, [GRAD[k] for k in kinds], after)))

    joining = []
    tok = None
    for i in reversed(range(depth)):
        j = i // 2
        sv = saved[i]
        if i + 1 < depth:
            g_ple_after = sv["g_ple"] + tok[0:1, 0:1]
            dh2, de, dzg, d_gple, d_gnext = rows_bwd("ple_next_pre_bwd", f_ple_pre, [sv["h2"], sv["e"], sv["zg"]],
                                                     [g_ple_after, sv["g_next"]], [dh, da_next], [F32, BF16, BF16])
            sg["norm_mix_pre"][i + 1] = d_gnext
        else:
            dh2, de, dzg, d_gple = rows_bwd("ple_last_bwd", f_ple, [sv["h2"], sv["e"], sv["zg"]], [sv["g_ple"]], [dh],
                                            [F32, BF16, BF16])
        sg["ple_norm"][i] = d_gple
        DW["ple_w_proj"] = mm_bwd_w("ple_proj_dw", p3, de, DW["ple_w_proj"], i, "col", xl=i)
        DW["ple_w_gate"] = mm_bwd_w("ple_gate_dw", sv["h2"], dzg, DW["ple_w_gate"], i, "row")
        dh2 = mm_bwd_x("ple_gate_dx", dzg, G["ple_w_gate"], i, "row", addend=dh2)
        dh1, df, d_gfpost = rows_bwd("ffn_post_bwd", f_post, [sv["h1"], sv["f"]], [sv["g_fpost"]], [dh2], [F32, BF16])
        sg["norm_ffn_post"][i] = d_gfpost
        dgate, dup = ffn_down_dx("ffn_down_dx", df, G["ffn_w_down"], i, sv["gate"], sv["up"])
        DW["ffn_w_down"] = mm_bwd_w("ffn_down_dw", sv["act"], df, DW["ffn_w_down"], i, "row")
        dfin = ffn_in_dx("ffn_in_dx", dgate, dup, G["ffn_w_gate"], G["ffn_w_up"], i)
        DW["ffn_w_gate"] = mm_bwd_w("ffn_gate_dw", sv["fin"], dgate, DW["ffn_w_gate"], i, "col")
        DW["ffn_w_up"] = mm_bwd_w("ffn_up_dw", sv["fin"], dup, DW["ffn_w_up"], i, "col")
        g_post_after = after_token(sv["g_post"], start_scatter(i, "rest"))
        dh, dm, d_gpost, d_gfpre = rows_bwd("mix_post_ffn_pre_bwd", f_post_pre, [sv["h"], sv["m"]],
                                            [g_post_after, sv["g_fpre"]], [dh1, dfin], [F32, BF16])
        sg["norm_mix_post"][i], sg["norm_ffn_pre"][i] = d_gpost, d_gfpre
        if i % 2 == 0:
            dog = mm_bwd_x("hg_out_dx", dm, G["hg_w_out"], j, "row")
            DW["hg_w_out"] = mm_bwd_w("hg_out_dw", sv["og"], dm, DW["hg_w_out"], j, "row")
            dproj4, d0, d1, d2, d_on = hgrn_bwd("hgrn_bwd", sv["proj4"], sv["states"], dog, *sv["lbp"], sv["onorm"], n_seq)
            d_lbp[3 * j:3 * j + 3] = [d0, d1, d2]
            sg["hg_out_norm"][j] = d_on
            da_next = mm_bwd_x("hg_in_dx", dproj4, G["hg_w_in"], j, "col", parts=True)
            DW["hg_w_in"] = mm_bwd_w("hg_in_dw", sv["a"], dproj4, DW["hg_w_in"], j, "col", parts=True)
        else:
            dy = mm_bwd_x("gm_out_dx", dm, G["gm_w_out"], j, "row")
            DW["gm_w_out"] = mm_bwd_w("gm_out_dw", sv["y"], dm, DW["gm_w_out"], j, "row")
            gb = sv["sp_ins"][0][1][0]
            du, dvn, dws, dbs = bmap_bwd("gm_spatial_bwd", f_gm_spatial, sv["sp_grid"], sv["sp_ins"],
                                         [(dy, (gb, LANES), lambda g, n: (n, g))],
                                         [(0, F32, None), (1, F32, None), (2, F32, 1), (3, F32, 1)])
            sg["gm_w_s"][j], sg["gm_b_s"][j] = dws, dbs[:, :, 0]
            dz, d_lg, d_lb = rows_bwd("gm_gelu_ln_bwd", f_gm_in, [sv["z"]], [sv["lg"], sv["lb_"]], [du, dvn], [BF16], tm=128)
            sg["gm_ln_g"][j], sg["gm_ln_b"][j] = d_lg, d_lb
            da_next = mm_bwd_x("gm_in_dx", dz, G["gm_w_in"], j, "col")
            DW["gm_w_in"] = mm_bwd_w("gm_in_dw", sv["a"], dz, DW["gm_w_in"], j, "col")
        tok = start_scatter(i, "mix")
        if i + 1 < depth:
            tok = tok + finish_scatter(i + 1, "rest", da_next) + finish_scatter(i + 1, "mix", da_next)
    tok = tok + finish_scatter(0, "rest", tok)
    g0 = after_token(row(W["norm_mix_pre"], 0), tok)
    grad_x, d_g0 = rows_bwd("prenorm_bwd", f_prenorm_thru, [saved[0]["h"]], [g0], [da_next, dh], [F32])
    sg["norm_mix_pre"][0] = d_g0
    d_logits = bmap_bwd("hg_lower_bounds_bwd", f_lb, (1,), lb_ins, [(d, one, lambda i: (0, 0)) for d in d_lbp],
                        [(jj, F32, None) for jj in range(n_hg)])

    small_g = {k: jnp.stack([v.reshape(W[k].shape[1:] if k not in ("gm_ln_g", "gm_ln_b") else (D,)) for v in sg[k]])
               for k in sg}
    small_g["hg_lb_logits"] = jnp.concatenate(d_logits, axis=0)
    small_shapes = [small_g[k].shape for k in SMALL]
    packed_g = _pack([small_g[k] for k in SMALL])
    slots = lax.dynamic_update_slice(jnp.zeros((N_DEV,) + packed_g.shape, F32), packed_g[None], (me, 0, 0))
    s_send, s_recv, slots, tok_s = split_start("share_small_start", share_build, [slots], n_peers=N_DEV - 1)

    out_g, out_d, out_m, out_v = {}, {}, {}, {}
    late = [k for k, _ in groups[0, "mix"]]
    early = [k for k in BIG if k not in late]
    finish_joins(tok_s)
    for k in early:
        out_d[k], out_m[k], out_v[k], out_g[k] = adamw("adamw_" + k, W[k], GRAD[k], M[k], V[k], with_grad=True)
    slots = split_wait("share_small_wait", share_build, s_send, s_recv, slots, out_v[early[-1]])[0]
    red = sum_devices(slots)
    small_red = dict(zip(SMALL, _unpack(red, small_shapes)))
    for k in ("gm_ln_g", "gm_ln_b"):
        small_red[k] = lax.dynamic_slice_in_dim(small_red[k], q_me * dq, dq, axis=1)
    pk = lambda d: _pack([d[k] for k in SMALL])
    s_delta, s_m, s_v = adamw("adamw_small", pk(W), pk(small_red), pk(M), pk(V))
    shard_shapes = [W[k].shape for k in SMALL]
    out_g.update(small_red)
    for dct, packed in ((out_d, s_delta), (out_m, s_m), (out_v, s_v)):
        dct.update(zip(SMALL, _unpack(packed, shard_shapes)))

    finish_joins(finish_scatter(0, "mix", s_v))
    for k in late:
        out_d[k], out_m[k], out_v[k], out_g[k] = adamw("adamw_" + k, W[k], GRAD[k], M[k], V[k], with_grad=True)

    outs = [loss, grad_x.reshape(x.shape)]
    for dct in (out_g, out_d, out_m, out_v):
        outs += [dct[k] for k in WEIGHTS]
    return tuple(outs)


def kernel(x, p, hg_w_in, hg_lb_logits, hg_out_norm, hg_w_out, gm_w_in, gm_ln_g, gm_ln_b, gm_w_s, gm_b_s, gm_w_out, norm_mix_pre, norm_mix_post, norm_ffn_pre, norm_ffn_post, ffn_w_gate, ffn_w_up, ffn_w_down, ple_w_proj, ple_w_gate, ple_norm, loss_target, m_hg_w_in, m_hg_lb_logits, m_hg_out_norm, m_hg_w_out, m_gm_w_in, m_gm_ln_g, m_gm_ln_b, m_gm_w_s, m_gm_b_s, m_gm_w_out, m_norm_mix_pre, m_norm_mix_post, m_norm_ffn_pre, m_norm_ffn_post, m_ffn_w_gate, m_ffn_w_up, m_ffn_w_down, m_ple_w_proj, m_ple_w_gate, m_ple_norm, v_hg_w_in, v_hg_lb_logits, v_hg_out_norm, v_hg_w_out, v_gm_w_in, v_gm_ln_g, v_gm_ln_b, v_gm_w_s, v_gm_b_s, v_gm_w_out, v_norm_mix_pre, v_norm_mix_post, v_norm_ffn_pre, v_norm_ffn_post, v_ffn_w_gate, v_ffn_w_up, v_ffn_w_down, v_ple_w_proj, v_ple_w_gate, v_ple_norm):
    W = dict(zip(WEIGHTS, (hg_w_in, hg_lb_logits, hg_out_norm, hg_w_out, gm_w_in, gm_ln_g, gm_ln_b, gm_w_s, gm_b_s, gm_w_out,
                           norm_mix_pre, norm_mix_post, norm_ffn_pre, norm_ffn_post, ffn_w_gate, ffn_w_up, ffn_w_down,
                           ple_w_proj, ple_w_gate, ple_norm)))
    M = dict(zip(WEIGHTS, (m_hg_w_in, m_hg_lb_logits, m_hg_out_norm, m_hg_w_out, m_gm_w_in, m_gm_ln_g, m_gm_ln_b, m_gm_w_s,
                           m_gm_b_s, m_gm_w_out, m_norm_mix_pre, m_norm_mix_post, m_norm_ffn_pre, m_norm_ffn_post,
                           m_ffn_w_gate, m_ffn_w_up, m_ffn_w_down, m_ple_w_proj, m_ple_w_gate, m_ple_norm)))
    V = dict(zip(WEIGHTS, (v_hg_w_in, v_hg_lb_logits, v_hg_out_norm, v_hg_w_out, v_gm_w_in, v_gm_ln_g, v_gm_ln_b, v_gm_w_s,
                           v_gm_b_s, v_gm_w_out, v_norm_mix_pre, v_norm_mix_post, v_norm_ffn_pre, v_norm_ffn_post,
                           v_ffn_w_gate, v_ffn_w_up, v_ffn_w_down, v_ple_w_proj, v_ple_w_gate, v_ple_norm)))
    return _step(x, p, W, M, V, loss_target)
```

```python
import functools

import jax
import jax.numpy as jnp
from jax import lax
from jax.experimental import pallas as pl
from jax.experimental.pallas import tpu as pltpu

F32 = jnp.float32
BF16 = jnp.bfloat16
MESH_ID = pl.DeviceIdType.MESH

LANES = 128
N_CHIPS = 4
N_DEV = 8
VMEM_LIMIT = 56 * 1024 * 1024
HG_SUB = 256
HG_BLOCK = 512
HG_HEADS_PER = 2
GM_CHUNK = 128
GM_BLOCK = 512
PACK_ROWS = 512
LB_FLOOR = 1e-30
EPS = 1e-6
ADAM_LR, ADAM_B1, ADAM_B2, ADAM_EPS, ADAM_WD, ADAM_STEP = 0.001, 0.9, 0.999, 1e-08, 0.01, 10


def _tile(n, pref, mult=LANES):
    if n <= pref:
        return n
    t = (pref // mult) * mult
    while t >= mult:
        if n % t == 0:
            return t
        t -= mult
    return n


def _cp(n_axes):
    return pltpu.CompilerParams(dimension_semantics=("arbitrary",) * n_axes, vmem_limit_bytes=VMEM_LIMIT)


def _dense(block):
    return tuple(b for b in block if b is not None)


def _bmap(name, grid, ins, outs, compute, scalars=(), into=None):
    n_s, n_in = len(scalars), len(ins)
    n_extra = 0 if into is None else 1

    def body(*refs):
        in_refs = refs[n_s:n_s + n_in]
        out_refs = refs[n_s + n_in + n_extra:]
        vals = [r[...] for r in in_refs]
        res = compute(*vals)
        for r, o, spec in zip(out_refs, res, outs):
            keep = spec[4]
            if keep is None:
                r[...] = o.astype(r.dtype)
            else:
                first = functools.reduce(jnp.logical_and, [pl.program_id(a) == 0 for a in range(keep, len(grid))])

                @pl.when(first)
                def _():
                    r[...] = jnp.zeros(r.shape, r.dtype)

                r[...] += o.astype(r.dtype)

    grid_spec = pltpu.PrefetchScalarGridSpec(
        num_scalar_prefetch=n_s, grid=grid,
        in_specs=[pl.BlockSpec(b, m) for _, b, m in ins] + [pl.BlockSpec(memory_space=pl.ANY)] * n_extra,
        out_specs=[pl.BlockSpec(o[2], o[3]) for o in outs])
    return pl.pallas_call(
        body, name=name, grid_spec=grid_spec,
        out_shape=[jax.ShapeDtypeStruct(o[0], o[1]) for o in outs],
        input_output_aliases={n_s + n_in: 0} if n_extra else {},
        compiler_params=_cp(len(grid)),
    )(*scalars, *[a for a, _, _ in ins], *([into] if n_extra else []))


def bmap_fwd(name, fn, grid, ins, outs, scalars=(), into=None):
    return _bmap(name, grid, ins, outs, lambda *v: fn(*[x.astype(F32) for x in v]), scalars, into)


def bmap_bwd(name, fn, grid, ins, cots, grads, scalars=()):
    n_in = len(ins)
    diff = [g[0] for g in grads]
    cot_ins = [c for c in cots if c is not None]

    def compute(*vals):
        xs = [v.astype(F32) for v in vals[:n_in]]
        cvals = list(vals[n_in:])

        def f(*d):
            full = list(xs)
            for i, dv in zip(diff, d):
                full[i] = dv
            return tuple(fn(*full))

        res, pull = jax.vjp(f, *[xs[i] for i in diff])
        cts = []
        for r, c in zip(res, cots):
            cts.append(jnp.zeros_like(r) if c is None else cvals.pop(0).astype(F32))
        return pull(tuple(cts))

    outs = [(ins[i][0].shape, dt, ins[i][1], ins[i][2], keep) for i, dt, keep in grads]
    return _bmap(name, grid, list(ins) + cot_ins, outs, compute, scalars)


def _mm(name, a, b, out_shape, out_dtype, grid, a_spec, b_spec, o_spec, dims, addend=None, alias_out=None):
    nk = grid[2]
    o_dense = _dense(o_spec[0])
    o_dense = (o_dense[0] * o_dense[1], o_dense[2]) if len(o_dense) == 3 else o_dense
    has_add = addend is not None
    has_alias = alias_out is not None

    def body(*refs):
        a_ref, b_ref = refs[0], refs[1]
        pos = 2
        c_ref = None
        if has_add:
            c_ref = refs[pos]
            pos += 1
        if has_alias:
            pos += 1
        o_ref = refs[pos]
        acc_ref = refs[pos + 1] if nk > 1 else None
        bv = b_ref[...]
        if bv.ndim == 3:
            bv = bv.reshape(bv.shape[0] * bv.shape[1], bv.shape[2])
        p = lax.dot_general(a_ref[...].astype(BF16), bv.astype(BF16), (dims, ((), ())), preferred_element_type=F32)

        def finish(total):
            if has_add:
                total = total + c_ref[...].astype(F32)
            o_ref[...] = total.reshape(o_ref.shape).astype(o_ref.dtype)

        if nk == 1:
            finish(p)
        else:
            k = pl.program_id(2)

            @pl.when(k == 0)
            def _():
                acc_ref[...] = p

            @pl.when(jnp.logical_and(k > 0, k < nk - 1))
            def _():
                acc_ref[...] += p

            @pl.when(k == nk - 1)
            def _():
                finish(acc_ref[...] + p)

    in_specs = [pl.BlockSpec(*a_spec), pl.BlockSpec(*b_spec)]
    operands = [a, b]
    if has_add:
        in_specs.append(pl.BlockSpec(o_spec[0], o_spec[1]))
        operands.append(addend)
    aliases = {}
    if has_alias:
        in_specs.append(pl.BlockSpec(memory_space=pl.ANY))
        aliases = {len(operands): 0}
        operands.append(alias_out)
    return pl.pallas_call(
        body, name=name, grid=grid, in_specs=in_specs, out_specs=pl.BlockSpec(*o_spec),
        out_shape=jax.ShapeDtypeStruct(out_shape, out_dtype),
        scratch_shapes=[pltpu.VMEM(o_dense, F32)] if nk > 1 else [],
        input_output_aliases=aliases,
        compiler_params=pltpu.CompilerParams(dimension_semantics=("parallel", "parallel", "arbitrary"),
                                             vmem_limit_bytes=VMEM_LIMIT),
    )(*operands)


NN, NT, TN = ((1,), (0,)), ((1,), (1,)), ((0,), (0,))
TM = 512
TT = 1024
TN_PREF = 1408
WHOLE_K = 2048


def mm_fwd(name, x, wg, l, kind, out_dtype=F32, parts=False, xl=None):
    if isinstance(wg, dict):
        wg, l = wg[l], 0
    _, _, R, C = wg.shape
    T = x.shape[-2]
    tm = _tile(T, TM, 8)
    if kind == "col":
        tn = _tile(C, TN_PREF)
        npc = C // tn
        grid = (T // tm, N_CHIPS * npc, 1)
        a_blk = (tm, R) if xl is None else (None, tm, R)
        a_map = (lambda i, j, k: (i, 0)) if xl is None else (lambda i, j, k: (xl, i, 0))
        b_spec = ((None, None, R, tn), lambda i, j, k: (j // npc, l, 0, j % npc))
        if parts:
            out_shape = (N_CHIPS, T, C)
            o_spec = ((None, tm, tn), lambda i, j, k: (j // npc, i, j % npc))
        else:
            out_shape = (T, N_CHIPS * C)
            o_spec = ((tm, tn), lambda i, j, k: (i, j))
    elif N_CHIPS * R <= WHOLE_K:
        tn = _tile(C, 1024)
        grid = (T // tm, C // tn, 1)
        a_blk = (tm, N_CHIPS * R)
        a_map = lambda i, j, k: (i, 0)
        b_spec = ((N_CHIPS, None, R, tn), lambda i, j, k: (0, l, 0, j))
        out_shape = (T, C)
        o_spec = ((tm, tn), lambda i, j, k: (i, j))
    else:
        tn = _tile(C, 2048)
        grid = (T // tm, C // tn, N_CHIPS)
        a_blk = (tm, R)
        a_map = lambda i, j, k: (i, k)
        b_spec = ((None, None, R, tn), lambda i, j, k: (k, l, 0, j))
        out_shape = (T, C)
        o_spec = ((tm, tn), lambda i, j, k: (i, j))
    return _mm(name, x, wg, out_shape, out_dtype, grid, (a_blk, a_map), b_spec, o_spec, NN)


def mm_bwd_x(name, dy, wg, l, kind, out_dtype=F32, parts=False, addend=None):
    if isinstance(wg, dict):
        wg, l = wg[l], 0
    _, _, R, C = wg.shape
    T = dy.shape[-2]
    tm = _tile(T, TM, 8)
    if kind == "col":
        tk = _tile(C, TN_PREF)
        npc = C // tk
        tno = _tile(R, 2048)
        grid = (T // tm, R // tno, N_CHIPS * npc)
        if parts:
            a_spec = ((None, tm, tk), lambda i, j, k: (k // npc, i, k % npc))
        else:
            a_spec = ((tm, tk), lambda i, j, k: (i, k))
        b_spec = ((None, None, tno, tk), lambda i, j, k: (k // npc, l, j, k % npc))
        out_shape = (T, R)
        o_spec = ((tm, tno), lambda i, j, k: (i, j))
    elif N_CHIPS * R <= WHOLE_K:
        grid = (T // tm, 1, 1)
        a_spec = ((tm, C), lambda i, j, k: (i, 0))
        b_spec = ((N_CHIPS, None, R, C), lambda i, j, k: (0, l, 0, 0))
        out_shape = (T, N_CHIPS * R)
        o_spec = ((tm, N_CHIPS * R), lambda i, j, k: (i, 0))
    else:
        grid = (T // tm, N_CHIPS, 1)
        a_spec = ((tm, C), lambda i, j, k: (i, 0))
        b_spec = ((None, None, R, C), lambda i, j, k: (j, l, 0, 0))
        out_shape = (T, N_CHIPS * R)
        o_spec = ((tm, R), lambda i, j, k: (i, j))
    return _mm(name, dy, wg, out_shape, out_dtype, grid, a_spec, b_spec, o_spec, NT, addend=addend)


def mm_bwd_w(name, x, dy, dwg, l, kind, parts=False, xl=None):
    if isinstance(dwg, dict):
        return {**dwg, l: mm_bwd_w(name, x, dy, dwg[l], 0, kind, parts=parts, xl=xl)}
    _, _, R, C = dwg.shape
    T = dy.shape[-2]
    tt = _tile(T, TT, 16)
    nt = T // tt
    if kind == "col":
        tn = _tile(C, TN_PREF)
        npc = C // tn
        tr = _tile(R, 1024)
        grid = (R // tr, N_CHIPS * npc, nt)
        if xl is None:
            a_spec = ((tt, tr), lambda i, j, t: (t, i))
        else:
            a_spec = ((None, tt, tr), lambda i, j, t: (xl, t, i))
        if parts:
            b_spec = ((None, tt, tn), lambda i, j, t: (j // npc, t, j % npc))
        else:
            b_spec = ((tt, tn), lambda i, j, t: (t, j))
        o_spec = ((None, None, tr, tn), lambda i, j, t: (j // npc, l, i, j % npc))
    elif N_CHIPS * R <= WHOLE_K:
        tn = _tile(C, 1024)
        grid = (1, C // tn, nt)
        a_spec = ((tt, N_CHIPS * R), lambda i, j, t: (t, 0))
        b_spec = ((tt, tn), lambda i, j, t: (t, j))
        o_spec = ((N_CHIPS, None, R, tn), lambda i, j, t: (0, l, 0, j))
    else:
        tn = _tile(C, 1024)
        grid = (N_CHIPS, C // tn, nt)
        a_spec = ((tt, R), lambda i, j, t: (t, i))
        b_spec = ((tt, tn), lambda i, j, t: (t, j))
        o_spec = ((None, None, R, tn), lambda i, j, t: (i, l, 0, j))
    return _mm(name, x, dy, dwg.shape, dwg.dtype, grid, a_spec, b_spec, o_spec, TN, alias_out=dwg)


def _sigmoid(x):
    return 0.5 * jnp.tanh(0.5 * x) + 0.5


def f_swiglu(gate, up):
    return (gate * _sigmoid(gate) * up,)


def ffn_gate_up(name, x, wg_gate, wg_up, l):
    if isinstance(wg_gate, dict):
        wg_gate, wg_up, l = wg_gate[l], wg_up[l], 0
    _, _, R, C = wg_gate.shape
    T = x.shape[0]
    tm = _tile(T, TM, 8)
    tn = _tile(C, TN_PREF)
    npc = C // tn

    def body(x_ref, g_ref, u_ref, gate_ref, up_ref, act_ref):
        xv = x_ref[...].astype(BF16)
        gate = jnp.dot(xv, g_ref[...], preferred_element_type=F32)
        up = jnp.dot(xv, u_ref[...], preferred_element_type=F32)
        gate_ref[...] = gate
        up_ref[...] = up
        act_ref[...] = f_swiglu(gate, up)[0].astype(act_ref.dtype)

    w_spec = pl.BlockSpec((None, None, R, tn), lambda i, j: (j // npc, l, 0, j % npc))
    o_spec = pl.BlockSpec((tm, tn), lambda i, j: (i, j))
    N = N_CHIPS * C
    return pl.pallas_call(
        body, name=name, grid=(T // tm, N_CHIPS * npc),
        in_specs=[pl.BlockSpec((tm, R), lambda i, j: (i, 0)), w_spec, w_spec], out_specs=[o_spec, o_spec, o_spec],
        out_shape=[jax.ShapeDtypeStruct((T, N), F32), jax.ShapeDtypeStruct((T, N), F32), jax.ShapeDtypeStruct((T, N), BF16)],
        compiler_params=_cp(2),
    )(x, wg_gate, wg_up)


def ffn_in_dx(name, dgate, dup, wg_gate, wg_up, l):
    if isinstance(wg_gate, dict):
        wg_gate, wg_up, l = wg_gate[l], wg_up[l], 0
    _, _, R, C = wg_gate.shape
    T = dgate.shape[0]
    tm = _tile(T, TM, 8)
    tk = _tile(C, TN_PREF)
    npc = C // tk
    nk = N_CHIPS * npc

    def body(dg_ref, du_ref, wg_ref, wu_ref, o_ref, acc):
        k = pl.program_id(1)

        @pl.when(k == 0)
        def _():
            acc[...] = jnp.zeros(acc.shape, F32)

        @pl.when(k < nk)
        def _():
            acc[...] += lax.dot_general(dg_ref[...], wg_ref[...], (NT, ((), ())), preferred_element_type=F32)

        @pl.when(k >= nk)
        def _():
            acc[...] += lax.dot_general(du_ref[...], wu_ref[...], (NT, ((), ())), preferred_element_type=F32)

        @pl.when(k == 2 * nk - 1)
        def _():
            o_ref[...] = acc[...]

    first = lambda k: jnp.minimum(k, nk - 1)
    second = lambda k: jnp.maximum(k - nk, 0)
    return pl.pallas_call(
        body, name=name, grid=(T // tm, 2 * nk),
        in_specs=[pl.BlockSpec((tm, tk), lambda i, k: (i, first(k))), pl.BlockSpec((tm, tk), lambda i, k: (i, second(k))),
                  pl.BlockSpec((None, None, R, tk), lambda i, k: (first(k) // npc, l, 0, first(k) % npc)),
                  pl.BlockSpec((None, None, R, tk), lambda i, k: (second(k) // npc, l, 0, second(k) % npc))],
        out_specs=pl.BlockSpec((tm, R), lambda i, k: (i, 0)),
        out_shape=jax.ShapeDtypeStruct((T, R), F32),
        scratch_shapes=[pltpu.VMEM((tm, R), F32)],
        compiler_params=_cp(2),
    )(dgate, dup, wg_gate, wg_up)


def ffn_down_dx(name, df, wg_down, l, gate, up):
    if isinstance(wg_down, dict):
        wg_down, l = wg_down[l], 0
    _, _, R, C = wg_down.shape
    T = df.shape[0]
    tm = _tile(T, TM, 8)

    def body(df_ref, w_ref, gate_ref, up_ref, dg_ref, du_ref):
        dact = lax.dot_general(df_ref[...].astype(BF16), w_ref[...], (NT, ((), ())), preferred_element_type=F32)
        _, pull = jax.vjp(lambda g, u: f_swiglu(g, u)[0], gate_ref[...], up_ref[...])
        dg, du = pull(dact)
        dg_ref[...] = dg.astype(dg_ref.dtype)
        du_ref[...] = du.astype(du_ref.dtype)

    t_spec = pl.BlockSpec((tm, R), lambda i, j: (i, j))
    return pl.pallas_call(
        body, name=name, grid=(T // tm, N_CHIPS),
        in_specs=[pl.BlockSpec((tm, C), lambda i, j: (i, 0)), pl.BlockSpec((None, None, R, C), lambda i, j: (j, l, 0, 0)),
                  t_spec, t_spec],
        out_specs=[t_spec, t_spec],
        out_shape=[jax.ShapeDtypeStruct((T, N_CHIPS * R), BF16)] * 2,
        compiler_params=_cp(2),
    )(df, wg_down, gate, up)


def _rms(x, g):
    return x * lax.rsqrt(jnp.mean(x * x, axis=-1, keepdims=True) + EPS) * g


def f_prenorm(h, g):
    return (_rms(h, g),)


def f_prenorm_thru(h, g):
    return _rms(h, g), h


def f_post_pre(h, m, g_post, g_pre):
    h1 = h + _rms(m, g_post)
    return h1, _rms(h1, g_pre)


def f_post(h1, f, g):
    return (h1 + _rms(f, g),)


def f_ple(h2, e, zg, g):
    return (h2 + _rms(e * _sigmoid(zg), g),)


def f_ple_pre(h2, e, zg, g, g_next):
    h3 = h2 + _rms(e * _sigmoid(zg), g)
    return h3, _rms(h3, g_next)


def _gelu(x):
    return 0.5 * x * (1.0 + lax.erf(x * 0.7071067811865476))


def f_gm_in(z, ln_g, ln_b):
    w = z.shape[-1] // 2
    u = _gelu(z[:, :w])
    v = _gelu(z[:, w:])
    mu = jnp.mean(v, axis=-1, keepdims=True)
    vc = v - mu
    vn = vc * lax.rsqrt(jnp.mean(vc * vc, axis=-1, keepdims=True) + EPS) * ln_g + ln_b
    return u, vn


def f_gm_spatial(u, vn, ws, bs):
    t = lax.broadcasted_iota(jnp.int32, ws.shape, 0)
    s = lax.broadcasted_iota(jnp.int32, ws.shape, 1)
    wm = jnp.where(t >= s, ws, 0.0).astype(BF16)
    ys = []
    for n in range(u.shape[0] // GM_CHUNK):
        rows = slice(n * GM_CHUNK, (n + 1) * GM_CHUNK)
        sv = jnp.dot(wm, vn[rows].astype(BF16), preferred_element_type=F32) + bs
        ys.append(u[rows] * sv)
    return (jnp.concatenate(ys, axis=0) if len(ys) > 1 else ys[0],)


def f_adam(w, g, m, v):
    m = ADAM_B1 * m + (1.0 - ADAM_B1) * g
    v = ADAM_B2 * v + (1.0 - ADAM_B2) * jnp.square(g)
    m_hat = m / (1.0 - ADAM_B1 ** ADAM_STEP)
    v_hat = v / (1.0 - ADAM_B2 ** ADAM_STEP)
    delta = -ADAM_LR * (m_hat / (jnp.sqrt(v_hat) + ADAM_EPS) + ADAM_WD * w)
    return delta, m, v


def _make_f_lb(n_layers):
    def f_lb(*logits):
        mx = functools.reduce(jnp.maximum, logits)
        ex = [jnp.exp(r - mx) for r in logits]
        tot = functools.reduce(lambda a, b: a + b, ex)
        sm = [e / tot for e in ex]
        outs = []
        run = jnp.zeros_like(sm[0])
        for j in range(n_layers):
            if j > 0:
                run = run + sm[j]
            lb = run
            outs += [jnp.log(jnp.maximum(lb, LB_FLOOR)), jnp.log(1.0 - lb), 1.0 - lb]
        return tuple(outs)
    return f_lb


def rows_fwd(name, fn, rows, params, out_dtypes, tm=256):
    T = rows[0].shape[0]
    tm = _tile(T, tm, 16)
    ins = [(r, (tm, r.shape[1]), lambda i: (i, 0)) for r in rows]
    ins += [(p, p.shape, lambda i: (0, 0)) for p in params]
    shapes = jax.eval_shape(lambda *a: fn(*a), *[jax.ShapeDtypeStruct((tm, r.shape[1]), F32) for r in rows],
                            *[jax.ShapeDtypeStruct(p.shape, F32) for p in params])
    outs = [((T, s.shape[1]), dt, (tm, s.shape[1]), lambda i: (i, 0), None) for s, dt in zip(shapes, out_dtypes)]
    return bmap_fwd(name, fn, (T // tm,), ins, outs)


def rows_bwd(name, fn, rows, params, cots, row_grad_dtypes, tm=256):
    T = rows[0].shape[0]
    tm = _tile(T, tm, 16)
    ins = [(r, (tm, r.shape[1]), lambda i: (i, 0)) for r in rows]
    ins += [(p, p.shape, lambda i: (0, 0)) for p in params]
    cts = [None if c is None else (c, (tm, c.shape[1]), lambda i: (i, 0)) for c in cots]
    grads = [(i, dt, None) for i, dt in enumerate(row_grad_dtypes) if dt is not None]
    grads += [(len(rows) + j, F32, 0) for j in range(len(params))]
    return bmap_bwd(name, fn, (T // tm,), ins, cts, grads)


def _log_sigmoid(z):
    return jnp.minimum(z, 0.0) - jnp.log(1.0 + jnp.exp(-jnp.abs(z)))


def _hg_gates(zf, ll0, ll1, oml):
    x2 = ll1 + _log_sigmoid(zf)
    mx = jnp.maximum(ll0, x2)
    g = mx + jnp.log(jnp.exp(ll0 - mx) + jnp.exp(x2 - mx))
    return g, oml * _sigmoid(-zf)


def hg_constants(n):
    levels = n.bit_length() - 1
    r = jnp.arange(n, dtype=jnp.int32)
    bounds = [r] + [((r >> (s + 1)) << (s + 1)) + ((1 << s) - 1) for s in range(levels)]
    sel = jnp.concatenate([(r[None, :] <= bd[:, None]) for bd in bounds], axis=0).astype(BF16)
    later = jnp.stack([((r >> s) & 1) for s in range(levels)])
    sign = jnp.broadcast_to((2 * later - 1).astype(F32)[:, :, None], (levels, n, LANES))
    pair = jnp.stack([((r[:, None] >> (s + 1)) == (r[None, :] >> (s + 1))) & (later[s][:, None] == 1) & (later[s][None, :] == 0)
                      for s in range(levels)]).astype(F32)
    return sel, sel.T, sign, pair


def _dot2(m, x):
    hi = x.astype(BF16)
    lo = (x - hi.astype(F32)).astype(BF16)
    p = jnp.dot(m, jnp.concatenate([hi, lo], axis=1), preferred_element_type=F32)
    w = x.shape[1]
    return p[:, :w] + p[:, w:]


@jax.custom_vjp
def _sel_dot(sel, selt, g):
    return _dot2(sel, g)


def _sel_dot_fwd(sel, selt, g):
    return _dot2(sel, g), (sel, selt)


def _sel_dot_bwd(res, d):
    sel, selt = res
    return jnp.zeros_like(sel), jnp.zeros_like(selt), _dot2(selt, d)


_sel_dot.defvjp(_sel_dot_fwd, _sel_dot_bwd)


def _hg_state(st, zf, zi, ll0, ll1, oml, tri):
    g, k = _hg_gates(zf, ll0, ll1, oml)
    b = _dot2(tri, g)
    tot = jnp.sum(g, axis=0, keepdims=True)
    kd = k * jnp.exp(tot - b)
    return st * jnp.exp(tot) + jnp.dot(zi.T.astype(BF16), kd.astype(BF16), preferred_element_type=F32)


def _hg_step(st, zq, zf, zi, zg, ll0, ll1, oml, onorm, sel, selt, sign, pair):
    n = zq.shape[0]
    levels = n.bit_length() - 1
    q = zq * _sigmoid(zq)
    g, k = _hg_gates(zf, ll0, ll1, oml)
    sums = _sel_dot(sel, selt, g)
    b = sums[:n]
    tot = jnp.sum(g, axis=0, keepdims=True)
    o = lax.dot_general((q * jnp.exp(b)).astype(BF16), st.astype(BF16), (NT, ((), ())), preferred_element_type=F32)
    a = jnp.zeros((n, n), F32)
    for s in range(levels):
        e = jnp.exp(sign[s] * (b - sums[(s + 1) * n:(s + 2) * n]))
        al = lax.dot_general((q * e).astype(BF16), (k * e).astype(BF16), (NT, ((), ())), preferred_element_type=F32)
        a = a + pair[s] * al
    o = o + jnp.dot(a.astype(BF16), zi.astype(BF16), preferred_element_type=F32)
    o = o + jnp.sum(q * k, axis=1, keepdims=True) * zi
    kd = k * jnp.exp(tot - b)
    st_new = st * jnp.exp(tot) + jnp.dot(zi.T.astype(BF16), kd.astype(BF16), preferred_element_type=F32)
    og = _rms(o, onorm) * (zg * _sigmoid(zg))
    return og, st_new


def _whole(arr, n_grid):
    zeros = (0,) * arr.ndim
    return pl.BlockSpec(arr.shape, (lambda h, n: zeros) if n_grid == 2 else (lambda i: zeros))


def _hg_dims(proj4, n_seq):
    _, T, D = proj4.shape
    S = T // n_seq
    hp = HG_HEADS_PER if (D // LANES) % HG_HEADS_PER == 0 else 1
    tb = min(HG_BLOCK, S)
    streams = [(b, hl) for b in range(n_seq) for hl in range(hp)]
    return T, D, S, hp, D // (LANES * hp), LANES * hp, tb, S // tb, tb // HG_SUB, streams


def hgrn_fwd(name, proj4, ll0, ll1, oml, onorm, n_seq):
    T, D, S, hp, n_hg, W, tb, nblk, nsub, streams = _hg_dims(proj4, n_seq)
    ns = len(streams)

    def body(p_ref, ll0_ref, ll1_ref, oml_ref, on_ref, sel_ref, selt_ref, later_ref, pair_ref, og_ref, st_ref, st):
        @pl.when(pl.program_id(1) == 0)
        def _():
            st[...] = jnp.zeros(st.shape, F32)

        st_ref[...] = st[...]
        on = on_ref[...]

        def step(j, carry):
            r = pl.ds(pl.multiple_of(j * HG_SUB, HG_SUB), HG_SUB)
            consts = (sel_ref[...], selt_ref[...], later_ref[...], pair_ref[...])
            args = []
            for si, (b, hl) in enumerate(streams):
                ln = slice(hl * LANES, (hl + 1) * LANES)
                args.append((st[si], p_ref[0, b, r, ln], p_ref[1, b, r, ln], p_ref[2, b, r, ln], p_ref[3, b, r, ln],
                             ll0_ref[:, ln], ll1_ref[:, ln], oml_ref[:, ln], on) + consts)
            res = [_hg_step(*a) for a in args]
            for si, (b, hl) in enumerate(streams):
                og_ref[b, r, hl * LANES:(hl + 1) * LANES] = res[si][0].astype(og_ref.dtype)
                st[si] = res[si][1]
            return carry

        lax.fori_loop(0, nsub, step, 0)

    vec = pl.BlockSpec((1, W), lambda h, n: (0, h))
    consts = hg_constants(HG_SUB)
    og, states = pl.pallas_call(
        body, name=name, grid=(n_hg, nblk),
        in_specs=[pl.BlockSpec((4, n_seq, tb, W), lambda h, n: (0, 0, n, h)), vec, vec, vec,
                  pl.BlockSpec((1, LANES), lambda h, n: (0, 0))] + [_whole(c, 2) for c in consts],
        out_specs=[pl.BlockSpec((n_seq, tb, W), lambda h, n: (0, n, h)),
                   pl.BlockSpec((None, None, ns, LANES, LANES), lambda h, n: (h, n, 0, 0, 0))],
        out_shape=[jax.ShapeDtypeStruct((n_seq, S, D), BF16),
                   jax.ShapeDtypeStruct((n_hg, nblk, ns, LANES, LANES), F32)],
        scratch_shapes=[pltpu.VMEM((ns, LANES, LANES), F32)],
        compiler_params=_cp(2),
    )(proj4.reshape(4, n_seq, S, D), ll0, ll1, oml, onorm, *consts)
    return og.reshape(T, D), states


def hgrn_bwd(name, proj4, states, dog, ll0, ll1, oml, onorm, n_seq):
    T, D, S, hp, n_hg, W, tb, nblk, nsub, streams = _hg_dims(proj4, n_seq)
    ns = len(streams)

    def body(p_ref, st_ref, dog_ref, ll0_ref, ll1_ref, oml_ref, on_ref, sel_ref, selt_ref, later_ref, pair_ref,
             dp_ref, dll0_ref, dll1_ref, doml_ref, don_ref, sbuf, dst):
        n_id = pl.program_id(1)

        @pl.when(n_id == 0)
        def _():
            dst[...] = jnp.zeros(dst.shape, F32)
            for ref in (dll0_ref, dll1_ref, doml_ref):
                ref[...] = jnp.zeros(ref.shape, F32)

        @pl.when(jnp.logical_and(n_id == 0, pl.program_id(0) == 0))
        def _():
            don_ref[...] = jnp.zeros(don_ref.shape, F32)

        on = on_ref[...]

        def fwd(j, carry):
            r = pl.ds(pl.multiple_of(j * HG_SUB, HG_SUB), HG_SUB)
            tri = sel_ref[0:HG_SUB, :]
            args = []
            for si, (b, hl) in enumerate(streams):
                ln = slice(hl * LANES, (hl + 1) * LANES)
                args.append((carry[si], p_ref[1, b, r, ln], p_ref[2, b, r, ln],
                             ll0_ref[:, ln], ll1_ref[:, ln], oml_ref[:, ln], tri))
            for si in range(ns):
                sbuf[si, j] = carry[si]
            return tuple(_hg_state(*a) for a in args)

        lax.fori_loop(0, nsub, fwd, tuple(st_ref[si] for si in range(ns)))

        def bwd(jj, carry):
            j = nsub - 1 - jj
            r = pl.ds(pl.multiple_of(j * HG_SUB, HG_SUB), HG_SUB)
            args, cts = [], []
            for si, (b, hl) in enumerate(streams):
                ln = slice(hl * LANES, (hl + 1) * LANES)
                args.append((sbuf[si, j], p_ref[0, b, r, ln], p_ref[1, b, r, ln], p_ref[2, b, r, ln],
                             p_ref[3, b, r, ln], ll0_ref[:, ln], ll1_ref[:, ln], oml_ref[:, ln], on))
                cts.append((dog_ref[b, r, ln].astype(F32), dst[si]))
            consts = (sel_ref[...], selt_ref[...], later_ref[...], pair_ref[...])
            step_fn = lambda *a: _hg_step(*a, *consts)
            ds = [jax.vjp(step_fn, *a)[1](ct) for a, ct in zip(args, cts)]
            d_on = carry
            for si, (b, hl) in enumerate(streams):
                ln = slice(hl * LANES, (hl + 1) * LANES)
                d = ds[si]
                dst[si] = d[0]
                for part in range(4):
                    dp_ref[part, b, r, ln] = d[1 + part].astype(dp_ref.dtype)
                dll0_ref[:, ln] += d[5]
                dll1_ref[:, ln] += d[6]
                doml_ref[:, ln] += d[7]
                d_on = d_on + d[8]
            return d_on

        don_ref[...] += lax.fori_loop(0, nsub, bwd, jnp.zeros((1, LANES), F32))

    last = nblk - 1
    vec = pl.BlockSpec((1, W), lambda h, n: (0, h))
    one = pl.BlockSpec((1, LANES), lambda h, n: (0, 0))
    consts = hg_constants(HG_SUB)
    dproj, d0, d1, d2, d_on = pl.pallas_call(
        body, name=name, grid=(n_hg, nblk),
        in_specs=[pl.BlockSpec((4, n_seq, tb, W), lambda h, n: (0, 0, last - n, h)),
                  pl.BlockSpec((None, None, ns, LANES, LANES), lambda h, n: (h, last - n, 0, 0, 0)),
                  pl.BlockSpec((n_seq, tb, W), lambda h, n: (0, last - n, h)), vec, vec, vec, one]
        + [_whole(c, 2) for c in consts],
        out_specs=[pl.BlockSpec((4, n_seq, tb, W), lambda h, n: (0, 0, last - n, h)), vec, vec, vec, one],
        out_shape=[jax.ShapeDtypeStruct((4, n_seq, S, D), BF16)] + [jax.ShapeDtypeStruct((1, D), F32)] * 3
        + [jax.ShapeDtypeStruct((1, LANES), F32)],
        scratch_shapes=[pltpu.VMEM((ns, nsub, LANES, LANES), F32), pltpu.VMEM((ns, LANES, LANES), F32)],
        compiler_params=_cp(2),
    )(proj4.reshape(4, n_seq, S, D), states, dog.reshape(n_seq, S, D), ll0, ll1, oml, onorm, *consts)
    return dproj.reshape(4, T, D), d0, d1, d2, d_on


def _place():
    x, y, c = lax.axis_index("x"), lax.axis_index("y"), lax.axis_index("c")
    chips = [(1 - x, y), (x, 1 - y), (1 - x, 1 - y)]
    return x, y, c, chips


ANY = pl.BlockSpec(memory_space=pl.ANY)


def _comm_call(name, body, ins, out_shapes, sems, aliases=None):
    return pl.pallas_call(
        body, name=name, in_specs=[ANY] * len(ins), out_specs=[ANY] * len(out_shapes),
        out_shape=out_shapes, scratch_shapes=sems, input_output_aliases=aliases or {},
        compiler_params=pltpu.CompilerParams(has_side_effects=True),
    )(*ins)


HBM_SPEC = pl.BlockSpec(memory_space=pltpu.HBM)
SEM_SPEC = pl.BlockSpec(memory_space=pltpu.SEMAPHORE)
SPLIT_EFFECT = pltpu.SideEffectType.DATAFLOW_SIDE_EFFECTING
N_PEER_CHIPS = 3


def split_start(name, build, arrays, n_peers=N_PEER_CHIPS):
    n = len(arrays)

    def body(*refs):
        send, recv = refs[n], refs[n + 1]
        token = refs[2 * n + 2]
        starts, _ = build(refs[:n], send, recv)
        for cp in starts:
            cp.start()
        token[...] = jnp.zeros_like(token)

    res = pl.pallas_call(
        body, name=name,
        out_shape=(pltpu.SemaphoreType.DMA((n_peers,)), pltpu.SemaphoreType.DMA((n_peers,)),
                   *[pltpu.HBM(a.shape, a.dtype) for a in arrays], jax.ShapeDtypeStruct((8, LANES), F32)),
        in_specs=[HBM_SPEC] * n,
        out_specs=(SEM_SPEC, SEM_SPEC, *[HBM_SPEC] * n, pl.BlockSpec(memory_space=pltpu.VMEM)),
        input_output_aliases={i: 2 + i for i in range(n)},
        compiler_params=pltpu.CompilerParams(has_side_effects=SPLIT_EFFECT),
    )(*[pltpu.with_memory_space_constraint(a, pltpu.HBM) for a in arrays])
    return res[0], res[1], list(res[2:2 + n]), res[2 + n]


def split_wait(name, build, send, recv, arrays, after):
    n = len(arrays)

    def body(*refs):
        starts, arrivals = build(refs[:n], refs[n], refs[n + 1])
        for cp in starts:
            cp.wait_send()
        for cp in arrivals:
            cp.wait_recv()

    return list(pl.pallas_call(
        body, name=name, out_shape=tuple(pltpu.HBM(a.shape, a.dtype) for a in arrays),
        in_specs=[HBM_SPEC] * n + [SEM_SPEC, SEM_SPEC, ANY], out_specs=tuple([HBM_SPEC] * n),
        input_output_aliases={i: i for i in range(n)},
        compiler_params=pltpu.CompilerParams(has_side_effects=SPLIT_EFFECT),
    )(*arrays, send, recv, after))


def _row_half(ref, dim, who):
    rh = ref.shape[dim] // 2
    return pl.ds(who * rh, rh)


def gather_build(refs, send, recv):
    x, y, c, chips = _place()
    q = 2 * x + y
    starts, arrivals = [], []
    for buf in refs:
        rows = _row_half(buf, 2, c)
        for j, (px, py) in enumerate(chips):
            mine, got = buf.at[q, :, rows], buf.at[2 * px + py, :, rows]
            starts.append(pltpu.make_async_remote_copy(src_ref=mine, dst_ref=mine, send_sem=send.at[j], recv_sem=recv.at[j],
                                                       device_id=(px, py, c), device_id_type=MESH_ID))
            arrivals.append(pltpu.make_async_remote_copy(src_ref=got, dst_ref=got, send_sem=send.at[j], recv_sem=recv.at[j],
                                                         device_id=(px, py, c), device_id_type=MESH_ID))
    return starts, arrivals


def forward_build(refs, send, recv):
    x, y, c, chips = _place()
    sib = (x, y, 1 - c)
    starts, arrivals = [], []
    for buf in refs:
        for j, (px, py) in enumerate(chips):
            got = buf.at[2 * px + py, :, _row_half(buf, 2, c)]
            theirs = buf.at[2 * px + py, :, _row_half(buf, 2, 1 - c)]
            starts.append(pltpu.make_async_remote_copy(src_ref=got, dst_ref=got, send_sem=send.at[j], recv_sem=recv.at[j],
                                                       device_id=sib, device_id_type=MESH_ID))
            arrivals.append(pltpu.make_async_remote_copy(src_ref=theirs, dst_ref=theirs, send_sem=send.at[j], recv_sem=recv.at[j],
                                                         device_id=sib, device_id_type=MESH_ID))
    return starts, arrivals


def scatter_build(refs, send, recv):
    n = len(refs) // 2
    x, y, c, _ = _place()
    starts, arrivals = [], []
    for a in range(n):
        src, dst = refs[a], refs[n + a]
        for k in range(1, N_DEV):
            px, py, pc = x ^ (k >> 2), y ^ ((k >> 1) & 1), c ^ (k & 1)
            theirs = src.at[2 * px + py, :, _row_half(src, 2, pc)]
            starts.append(pltpu.make_async_remote_copy(src_ref=theirs, dst_ref=dst.at[k - 1], send_sem=send.at[k - 1],
                                                       recv_sem=recv.at[k - 1], device_id=(px, py, pc), device_id_type=MESH_ID))
            arrivals.append(pltpu.make_async_remote_copy(src_ref=dst.at[k - 1], dst_ref=dst.at[k - 1], send_sem=send.at[k - 1],
                                                         recv_sem=recv.at[k - 1], device_id=(px, py, pc), device_id_type=MESH_ID))
    return starts, arrivals


def share_build(refs, send, recv):
    dst = refs[0]
    x, y, c, _ = _place()
    mine = dst.at[4 * x + 2 * y + c]
    starts, arrivals = [], []
    for k in range(1, N_DEV):
        px, py, pc = x ^ (k >> 2), y ^ ((k >> 1) & 1), c ^ (k & 1)
        got = dst.at[4 * px + 2 * py + pc]
        starts.append(pltpu.make_async_remote_copy(src_ref=mine, dst_ref=mine, send_sem=send.at[k - 1], recv_sem=recv.at[k - 1],
                                                   device_id=(px, py, pc), device_id_type=MESH_ID))
        arrivals.append(pltpu.make_async_remote_copy(src_ref=got, dst_ref=got, send_sem=send.at[k - 1], recv_sem=recv.at[k - 1],
                                                     device_id=(px, py, pc), device_id_type=MESH_ID))
    return starts, arrivals


def join_build(refs, send, recv, layers):
    x, y, c, _ = _place()
    sib = (x, y, 1 - c)
    starts, arrivals = [], []
    for buf, l in zip(refs, layers):
        mine, theirs = buf.at[l, _row_half(buf, 1, c)], buf.at[l, _row_half(buf, 1, 1 - c)]
        starts.append(pltpu.make_async_remote_copy(src_ref=mine, dst_ref=mine, send_sem=send.at[0], recv_sem=recv.at[0],
                                                   device_id=sib, device_id_type=MESH_ID))
        arrivals.append(pltpu.make_async_remote_copy(src_ref=theirs, dst_ref=theirs, send_sem=send.at[0], recv_sem=recv.at[0],
                                                     device_id=sib, device_id_type=MESH_ID))
    return starts, arrivals


def join_row_halves(name, bufs):
    n = len(bufs)

    def body(*refs):
        dst = refs[n:2 * n]
        send, recv = refs[2 * n:]
        x, y, c, _ = _place()
        cps = []
        for a in range(n):
            mine = dst[a].at[:, _row_half(dst[a], 1, c)]
            cps.append(pltpu.make_async_remote_copy(src_ref=mine, dst_ref=mine, send_sem=send.at[a], recv_sem=recv.at[a],
                                                    device_id=(x, y, 1 - c), device_id_type=MESH_ID))
        for cp in cps:
            cp.start()
        for a in range(n):
            theirs = dst[a].at[:, _row_half(dst[a], 1, 1 - c)]
            pltpu.make_async_remote_copy(src_ref=theirs, dst_ref=theirs, send_sem=send.at[a], recv_sem=recv.at[a],
                                         device_id=(x, y, 1 - c), device_id_type=MESH_ID).wait_recv()
        for cp in cps:
            cp.wait_send()

    outs = [jax.ShapeDtypeStruct(b.shape, b.dtype) for b in bufs]
    sems = [pltpu.SemaphoreType.DMA((n,))] * 2
    return _comm_call(name, body, bufs, outs, sems, aliases={a: a for a in range(n)})


def share_with_all(name, packed, me):
    slots = lax.dynamic_update_slice(jnp.zeros((N_DEV,) + packed.shape, packed.dtype), packed[None], (me, 0, 0))

    def body(_, dst, send, recv):
        x, y, c, _ = _place()
        me = 4 * x + 2 * y + c
        cps = []
        for k in range(1, N_DEV):
            px, py, pc = x ^ (k >> 2), y ^ ((k >> 1) & 1), c ^ (k & 1)
            cps.append(pltpu.make_async_remote_copy(src_ref=dst.at[me], dst_ref=dst.at[me], send_sem=send.at[k - 1],
                                                    recv_sem=recv.at[k - 1], device_id=(px, py, pc), device_id_type=MESH_ID))
        for cp in cps:
            cp.start()
        for k in range(1, N_DEV):
            px, py, pc = x ^ (k >> 2), y ^ ((k >> 1) & 1), c ^ (k & 1)
            got = dst.at[4 * px + 2 * py + pc]
            pltpu.make_async_remote_copy(src_ref=got, dst_ref=got, send_sem=send.at[k - 1], recv_sem=recv.at[k - 1],
                                         device_id=(px, py, pc), device_id_type=MESH_ID).wait_recv()
        for cp in cps:
            cp.wait_send()

    outs = [jax.ShapeDtypeStruct(slots.shape, slots.dtype)]
    sems = [pltpu.SemaphoreType.DMA((N_DEV - 1,))] * 2
    return _comm_call(name, body, [slots], outs, sems, aliases={0: 0})[0]


def _w_tiles(R, C):
    return _tile(R, max(16, (1 << 20) // (4 * C) // 16 * 16), 16)


def cast_bf16(w, l, q_arr):
    _, R, C = w.shape
    tr = _w_tiles(R, C)
    ins = [(w, (None, tr, C), lambda r, q: (l, r, 0))]
    outs = [((N_CHIPS, 1, R, C), BF16, (None, None, tr, C), lambda r, q: (q[0], 0, r, 0), None)]
    return bmap_fwd("cast_bf16", lambda a: (a,), (R // tr,), ins, outs, scalars=(q_arr,))[0]


def sum_partials(own, landed, into, l, q_arr, c_arr):
    n_land, _, rh, C = landed.shape
    tr = _w_tiles(rh, C)
    nb = rh // tr
    blk = (None, None, tr, C)
    ins = [(own, blk, lambda r, q, c: (q[0], 0, c[0] * nb + r, 0))]
    ins += [(landed, blk, (lambda r, q, c, kk=kk: (kk, 0, r, 0))) for kk in range(n_land)]
    outs = [(into.shape, F32, (None, tr, C), lambda r, q, c: (l, c[0] * nb + r, 0), None)]
    return bmap_fwd("sum_partials", lambda *t: (functools.reduce(lambda u, v: u + v, t),), (nb,), ins, outs,
                    scalars=(q_arr, c_arr), into=into)[0]


def sum_devices(slots):
    nd, NR, C = slots.shape
    tr = _tile(NR, 512, 8)
    ins = [(slots, (None, tr, C), (lambda r, dd=dd: (dd, r, 0))) for dd in range(nd)]
    outs = [((NR, C), F32, (tr, C), lambda r: (r, 0), None)]
    return bmap_fwd("sum_devices", lambda *a: (functools.reduce(lambda u, v: u + v, a),), (NR // tr,), ins, outs)[0]


def adamw(name, w, g, m, v, with_grad=False):
    if w.ndim == 2:
        R, C = w.shape
        tr = _w_tiles(R, C)
        spec = ((tr, C), lambda r: (r, 0))
        grid = (R // tr,)
    else:
        L, R, C = w.shape
        tr = _w_tiles(R, C)
        spec = ((None, tr, C), lambda l, r: (l, r, 0))
        grid = (L, R // tr)
    ins = [(a,) + spec for a in (w, g, m, v)]
    outs = [(w.shape, F32) + spec + (None,)] * (4 if with_grad else 3)
    fn = (lambda a, b, c, d: f_adam(a, b, c, d) + (b,)) if with_grad else f_adam
    return bmap_fwd(name, fn, grid, ins, outs)


def loss_and_grad(h, target):
    T, D = h.shape
    tm = _tile(T, 256, 8)

    def fn(hv, tv):
        d = hv - tv
        return jnp.sum(d * d, keepdims=True).reshape(1, 1) * (0.5 / D), d * (1.0 / D)

    ins = [(h, (tm, D), lambda i: (i, 0)), (target, (tm, D), lambda i: (i, 0))]
    outs = [((1, 1), F32, (1, 1), lambda i: (0, 0), 0), ((T, D), F32, (tm, D), lambda i: (i, 0), None)]
    return bmap_fwd("loss_and_grad", fn, (T // tm,), ins, outs)


BIG = ("hg_w_in", "hg_w_out", "gm_w_in", "gm_w_out", "ffn_w_gate", "ffn_w_up", "ffn_w_down", "ple_w_proj", "ple_w_gate")
KIND = {"hg_w_in": "col", "hg_w_out": "row", "gm_w_in": "col", "gm_w_out": "row", "ffn_w_gate": "col",
        "ffn_w_up": "col", "ffn_w_down": "row", "ple_w_proj": "col", "ple_w_gate": "row"}
SMALL = ("hg_lb_logits", "hg_out_norm", "gm_ln_g", "gm_ln_b", "gm_w_s", "gm_b_s", "norm_mix_pre", "norm_mix_post",
         "norm_ffn_pre", "norm_ffn_post", "ple_norm")
SMALL_GM = ("gm_ln_g", "gm_ln_b", "gm_w_s", "gm_b_s")
SMALL_REST = tuple(k for k in SMALL if k not in SMALL_GM)
WEIGHTS = ("hg_w_in", "hg_lb_logits", "hg_out_norm", "hg_w_out", "gm_w_in", "gm_ln_g", "gm_ln_b", "gm_w_s", "gm_b_s",
           "gm_w_out", "norm_mix_pre", "norm_mix_post", "norm_ffn_pre", "norm_ffn_post", "ffn_w_gate", "ffn_w_up",
           "ffn_w_down", "ple_w_proj", "ple_w_gate", "ple_norm")


def _pack(arrs):
    rows = []
    for a in arrs:
        flat = a.reshape(-1)
        pad = (-flat.shape[0]) % (8 * LANES)
        rows.append(jnp.pad(flat, (0, pad)).reshape(-1, LANES))
    n_rows = sum(r.shape[0] for r in rows)
    rows.append(jnp.zeros(((-n_rows) % PACK_ROWS, LANES), F32))
    return jnp.concatenate(rows, axis=0)


def _unpack(packed, shapes):
    out, r = [], 0
    for s in shapes:
        size = 1
        for d in s:
            size *= d
        nr = -(-size // (8 * LANES)) * 8
        out.append(packed[r:r + nr].reshape(-1)[:size].reshape(s))
        r += nr
    return out


def _step(x, p, W, M, V, loss_target):
    n_seq, S, D = x.shape
    T = n_seq * S
    depth = p.shape[0]
    n_hg = W["hg_w_in"].shape[0]
    x2 = x.reshape(T, D)
    p3 = p.reshape(depth, T, p.shape[-1])
    tgt = loss_target.reshape(T, D)
    xi, yi, ci = lax.axis_index("x"), lax.axis_index("y"), lax.axis_index("c")
    q_me = 2 * xi + yi
    c_arr = jnp.reshape(ci, (1,)).astype(jnp.int32)
    q_arr = jnp.reshape(q_me, (1,)).astype(jnp.int32)

    groups = {}
    for i in range(depth):
        mix = ("hg_w_in", "hg_w_out") if i % 2 == 0 else ("gm_w_in", "gm_w_out")
        groups[i, "mix"] = [(k, i // 2) for k in mix]
        groups[i, "rest"] = [(k, i) for k in ("ffn_w_gate", "ffn_w_up", "ffn_w_down", "ple_w_proj", "ple_w_gate")]
    G = {k: {} for k in BIG}
    DW = {k: {l: lax.empty((N_CHIPS, 1) + W[k].shape[1:], BF16) for l in range(W[k].shape[0])} for k in BIG}
    in_flight = {}

    casts = {}

    def cast_group(i, part, dep):
        qa = q_arr if dep is None else lax.optimization_barrier((q_arr, dep))[0]
        casts[i, part] = [cast_bf16(W[k], l, qa) for k, l in groups[i, part]]

    def start_gather(i, part, dep):
        bufs = casts.pop((i, part))
        if dep is not None:
            bufs = list(lax.optimization_barrier((tuple(bufs), dep))[0])
        send, recv, arrs, tok = split_start("gather_start_%d_%s" % (i, part), gather_build, bufs)
        in_flight[i, part] = (send, recv, arrs)
        return tok

    forwarding = {}

    def arrive_gather(i, part, after):
        send, recv, arrs = in_flight.pop((i, part))
        arrs = split_wait("gather_wait_%d_%s" % (i, part), gather_build, send, recv, arrs, after)
        send, recv, arrs, tok = split_start("gather_pass_%d_%s" % (i, part), forward_build, arrs)
        forwarding[i, part] = (send, recv, arrs)
        return tok

    def finish_gather(i, part, after):
        send, recv, arrs = forwarding.pop((i, part))
        arrs = split_wait("gather_done_%d_%s" % (i, part), forward_build, send, recv, arrs, after)
        for (k, l), buf in zip(groups[i, part], arrs):
            G[k][l] = buf
        return buf

    def after_token(row_arr, *toks):
        return functools.reduce(lambda u, t: u + t[0:1, 0:1], toks, row_arr)

    cast_group(0, "mix", None)
    tok_mix = start_gather(0, "mix", None)
    cast_group(0, "rest", tok_mix)
    tok_rest = start_gather(0, "rest", None)
    finish_gather(0, "mix", arrive_gather(0, "mix", tok_mix + tok_rest))
    tie = lambda v, tok: lax.optimization_barrier((v, tok))[0]
    me = 4 * xi + 2 * yi + ci
    ln_full = share_with_all("share_ln", _pack([W["gm_ln_g"], W["gm_ln_b"]]), me)
    n_gm, dq = W["gm_ln_g"].shape
    ln_parts = [_unpack(ln_full[4 * qx + 2 * qy + 0], [(n_gm, dq), (n_gm, dq)]) for qx in range(2) for qy in range(2)]
    ln_g = jnp.concatenate([lp[0] for lp in ln_parts], axis=1)
    ln_b = jnp.concatenate([lp[1] for lp in ln_parts], axis=1)

    row = lambda a, i: a[i][None, :]
    f_lb = _make_f_lb(n_hg)
    lb_rows = [row(W["hg_lb_logits"], j) for j in range(n_hg)]
    one = (1, D)
    lb_ins = [(r, one, lambda i: (0, 0)) for r in lb_rows]
    lb_out = bmap_fwd("hg_lower_bounds", f_lb, (1,), lb_ins, [(one, F32, one, lambda i: (0, 0), None)] * (3 * n_hg))

    saved = []
    h = x2
    a = rows_fwd("prenorm", f_prenorm, [h], [row(W["norm_mix_pre"], 0)], [BF16])[0]
    for i in range(depth):
        j = i // 2
        sv = {"h": h, "a": a}
        if i > 0:
            finish_gather(i, "mix", h)
            a = tie(a, arrive_gather(i, "rest", h))
        if i % 2 == 0:
            proj4 = mm_fwd("hg_in", a, G["hg_w_in"], j, "col", parts=True)
            lbp = lb_out[3 * j:3 * j + 3]
            onorm = row(W["hg_out_norm"], j)
            og, states = hgrn_fwd("hgrn_fwd", proj4, *lbp, onorm, n_seq)
            if i == 0:
                og = tie(og, arrive_gather(0, "rest", og))
            m = mm_fwd("hg_out", og, G["hg_w_out"], j, "row")
            sv.update(proj4=proj4, states=states, og=og, lbp=lbp, onorm=onorm)
        else:
            z = mm_fwd("gm_in", a, G["gm_w_in"], j, "col")
            lg, lb_ = row(ln_g, j), row(ln_b, j)
            u, vn = rows_fwd("gm_gelu_ln", f_gm_in, [z], [lg, lb_], [F32, BF16], tm=128)
            ws = W["gm_w_s"][j]
            bs = W["gm_b_s"][j][:, :, None]
            gb = min(GM_BLOCK, S)
            sp_grid = (D // LANES, T // gb)
            sp_ins = [(u, (gb, LANES), lambda g, n: (n, g)), (vn, (gb, LANES), lambda g, n: (n, g)),
                      (ws, (None, GM_CHUNK, GM_CHUNK), lambda g, n: (g, 0, 0)),
                      (bs, (None, GM_CHUNK, 1), lambda g, n: (g, 0, 0))]
            y = bmap_fwd("gm_spatial", f_gm_spatial, sp_grid, sp_ins,
                         [((T, D), BF16, (gb, LANES), lambda g, n: (n, g), None)])[0]
            m = mm_fwd("gm_out", y, G["gm_w_out"], j, "row")
            sv.update(z=z, lg=lg, lb_=lb_, sp_ins=sp_ins, sp_grid=sp_grid, y=y)
        g_post, g_fpre = row(W["norm_mix_post"], i), row(W["norm_ffn_pre"], i)
        arrived = finish_gather(i, "rest", m)
        if i + 1 < depth:
            cast_group(i + 1, "mix", arrived)
            tok_mix = start_gather(i + 1, "mix", None)
            cast_group(i + 1, "rest", tok_mix)
            g_post = after_token(g_post, tok_mix, start_gather(i + 1, "rest", None))
        h1, fin = rows_fwd("mix_post_ffn_pre", f_post_pre, [h, m], [g_post, g_fpre], [F32, BF16])
        gate, up, act = ffn_gate_up("ffn_gate_up", fin, G["ffn_w_gate"], G["ffn_w_up"], i)
        f = mm_fwd("ffn_down", act, G["ffn_w_down"], i, "row")
        g_fpost = row(W["norm_ffn_post"], i)
        if i + 1 < depth:
            g_fpost = after_token(g_fpost, arrive_gather(i + 1, "mix", f))
        h2 = rows_fwd("ffn_post", f_post, [h1, f], [g_fpost], [F32])[0]
        e = mm_fwd("ple_proj", p3, G["ple_w_proj"], i, "col", xl=i)
        zg = mm_fwd("ple_gate", h2, G["ple_w_gate"], i, "row")
        g_ple = row(W["ple_norm"], i)
        sv.update(m=m, h1=h1, fin=fin, gate=gate, up=up, act=act, f=f, h2=h2, e=e, zg=zg,
                  g_post=g_post, g_fpre=g_fpre, g_fpost=g_fpost, g_ple=g_ple)
        if i + 1 < depth:
            g_next = row(W["norm_mix_pre"], i + 1)
            h, a = rows_fwd("ple_next_pre", f_ple_pre, [h2, e, zg], [g_ple, g_next], [F32, BF16])
            sv["g_next"] = g_next
        else:
            h = rows_fwd("ple_last", f_ple, [h2, e, zg], [g_ple], [F32])[0]
        saved.append(sv)

    loss_part, dh = loss_and_grad(h, tgt)
    loss = lax.psum(loss_part[0, 0], ("x", "y", "c"))

    sg = {k: [None] * W[k].shape[0] for k in ("norm_mix_pre", "norm_mix_post", "norm_ffn_pre", "norm_ffn_post", "ple_norm",
                                              "hg_out_norm", "gm_ln_g", "gm_ln_b", "gm_w_s", "gm_b_s")}
    d_lbp = [None] * (3 * n_hg)
    da_next = None
    GRAD = {k: lax.empty(W[k].shape, F32) for k in BIG}
    scattering = {}

    def start_scatter(i, part):
        dws = [DW[k][l] for k, l in groups[i, part]]
        lands = [lax.empty((N_DEV - 1, 1, g.shape[2] // 2, g.shape[3]), BF16) for g in dws]
        send, recv, arrs, tok = split_start("scatter_start_%d_%s" % (i, part), scatter_build, dws + lands, n_peers=N_DEV - 1)
        scattering[i, part] = (send, recv, arrs)
        return tok

    def finish_scatter(i, part, after):
        send, recv, arrs = scattering.pop((i, part))
        arrs = split_wait("scatter_wait_%d_%s" % (i, part), scatter_build, send, recv, arrs, after)
        n = len(groups[i, part])
        for (k, l), own, ld in zip(groups[i, part], arrs[:n], arrs[n:]):
            GRAD[k] = sum_partials(own, ld, GRAD[k], l, q_arr, c_arr)
        kinds = [k for k, _ in groups[i, part]]
        build = functools.partial(join_build, layers=tuple(l for _, l in groups[i, part]))
        send, recv, arrs, tok_j = split_start("join_start_%d_%s" % (i, part), build, [GRAD[k] for k in kinds], n_peers=1)
        GRAD.update(zip(kinds, arrs))
        joining.append(("join_wait_%d_%s" % (i, part), build, send, recv, kinds))
        return tok_j

    def finish_joins(after):
        while joining:
            name, build, send, recv, kinds = joining.pop(0)
            GRAD.update(zip(kinds, split_wait(name, build, send, recv, [GRAD[k] for k in kinds], after)))

    def start_share(name, keys, extra):
        grads = dict(extra)
        for k in keys:
            if k not in grads:
                grads[k] = jnp.stack([v.reshape((D,) if k in ("gm_ln_g", "gm_ln_b") else W[k].shape[1:]) for v in sg[k]])
        packed = _pack([grads[k] for k in keys])
        slots = lax.dynamic_update_slice(jnp.zeros((N_DEV,) + packed.shape, F32), packed[None], (me, 0, 0))
        s_send, s_recv, slots, tok_sh = split_start(name + "_start", share_build, [slots], n_peers=N_DEV - 1)
        return name, [grads[k].shape for k in keys], s_send, s_recv, slots, tok_sh

    joining = []
    sharing = {}
    last_gm = 1
    tok = None
    for i in reversed(range(depth)):
        j = i // 2
        sv = saved[i]
        if i + 1 < depth:
            g_ple_after = sv["g_ple"] + tok[0:1, 0:1]
            dh2, de, dzg, d_gple, d_gnext = rows_bwd("ple_next_pre_bwd", f_ple_pre, [sv["h2"], sv["e"], sv["zg"]],
                                                     [g_ple_after, sv["g_next"]], [dh, da_next], [F32, BF16, BF16])
            sg["norm_mix_pre"][i + 1] = d_gnext
        else:
            dh2, de, dzg, d_gple = rows_bwd("ple_last_bwd", f_ple, [sv["h2"], sv["e"], sv["zg"]], [sv["g_ple"]], [dh],
                                            [F32, BF16, BF16])
        sg["ple_norm"][i] = d_gple
        DW["ple_w_proj"] = mm_bwd_w("ple_proj_dw", p3, de, DW["ple_w_proj"], i, "col", xl=i)
        DW["ple_w_gate"] = mm_bwd_w("ple_gate_dw", sv["h2"], dzg, DW["ple_w_gate"], i, "row")
        dh2 = mm_bwd_x("ple_gate_dx", dzg, G["ple_w_gate"], i, "row", addend=dh2)
        dh1, df, d_gfpost = rows_bwd("ffn_post_bwd", f_post, [sv["h1"], sv["f"]], [sv["g_fpost"]], [dh2], [F32, BF16])
        sg["norm_ffn_post"][i] = d_gfpost
        dgate, dup = ffn_down_dx("ffn_down_dx", df, G["ffn_w_down"], i, sv["gate"], sv["up"])
        DW["ffn_w_down"] = mm_bwd_w("ffn_down_dw", sv["act"], df, DW["ffn_w_down"], i, "row")
        dfin = ffn_in_dx("ffn_in_dx", dgate, dup, G["ffn_w_gate"], G["ffn_w_up"], i)
        DW["ffn_w_gate"] = mm_bwd_w("ffn_gate_dw", sv["fin"], dgate, DW["ffn_w_gate"], i, "col")
        DW["ffn_w_up"] = mm_bwd_w("ffn_up_dw", sv["fin"], dup, DW["ffn_w_up"], i, "col")
        g_post_after = after_token(sv["g_post"], start_scatter(i, "rest"))
        dh, dm, d_gpost, d_gfpre = rows_bwd("mix_post_ffn_pre_bwd", f_post_pre, [sv["h"], sv["m"]],
                                            [g_post_after, sv["g_fpre"]], [dh1, dfin], [F32, BF16])
        sg["norm_mix_post"][i], sg["norm_ffn_pre"][i] = d_gpost, d_gfpre
        if i % 2 == 0:
            dog = mm_bwd_x("hg_out_dx", dm, G["hg_w_out"], j, "row")
            DW["hg_w_out"] = mm_bwd_w("hg_out_dw", sv["og"], dm, DW["hg_w_out"], j, "row")
            dproj4, d0, d1, d2, d_on = hgrn_bwd("hgrn_bwd", sv["proj4"], sv["states"], dog, *sv["lbp"], sv["onorm"], n_seq)
            d_lbp[3 * j:3 * j + 3] = [d0, d1, d2]
            sg["hg_out_norm"][j] = d_on
            da_next = mm_bwd_x("hg_in_dx", dproj4, G["hg_w_in"], j, "col", parts=True)
            DW["hg_w_in"] = mm_bwd_w("hg_in_dw", sv["a"], dproj4, DW["hg_w_in"], j, "col", parts=True)
        else:
            dy = mm_bwd_x("gm_out_dx", dm, G["gm_w_out"], j, "row")
            DW["gm_w_out"] = mm_bwd_w("gm_out_dw", sv["y"], dm, DW["gm_w_out"], j, "row")
            gb = sv["sp_ins"][0][1][0]
            du, dvn, dws, dbs = bmap_bwd("gm_spatial_bwd", f_gm_spatial, sv["sp_grid"], sv["sp_ins"],
                                         [(dy, (gb, LANES), lambda g, n: (n, g))],
                                         [(0, F32, None), (1, F32, None), (2, F32, 1), (3, F32, 1)])
            sg["gm_w_s"][j], sg["gm_b_s"][j] = dws, dbs[:, :, 0]
            dz, d_lg, d_lb = rows_bwd("gm_gelu_ln_bwd", f_gm_in, [sv["z"]], [sv["lg"], sv["lb_"]], [du, dvn], [BF16], tm=128)
            sg["gm_ln_g"][j], sg["gm_ln_b"][j] = d_lg, d_lb
            da_next = mm_bwd_x("gm_in_dx", dz, G["gm_w_in"], j, "col")
            DW["gm_w_in"] = mm_bwd_w("gm_in_dw", sv["a"], dz, DW["gm_w_in"], j, "col")
        tok = start_scatter(i, "mix")
        if i + 1 < depth:
            tok = tok + finish_scatter(i + 1, "rest", da_next) + finish_scatter(i + 1, "mix", da_next)
        if i == last_gm:
            sharing["gm"] = start_share("share_gm", SMALL_GM, {})
            tok = tok + sharing["gm"][-1]
    tok = tok + finish_scatter(0, "rest", tok)
    g0 = after_token(row(W["norm_mix_pre"], 0), tok)
    grad_x, d_g0 = rows_bwd("prenorm_bwd", f_prenorm_thru, [saved[0]["h"]], [g0], [da_next, dh], [F32])
    sg["norm_mix_pre"][0] = d_g0
    d_logits = bmap_bwd("hg_lower_bounds_bwd", f_lb, (1,), lb_ins, [(d, one, lambda i: (0, 0)) for d in d_lbp],
                        [(jj, F32, None) for jj in range(n_hg)])

    sharing["rest"] = start_share("share_small", SMALL_REST, {"hg_lb_logits": jnp.concatenate(d_logits, axis=0)})
    tok_s = sharing["rest"][-1]

    out_g, out_d, out_m, out_v = {}, {}, {}, {}
    late = [k for k, _ in groups[0, "mix"]]
    early = [k for k in BIG if k not in late]
    finish_joins(tok_s)
    for k in early:
        out_d[k], out_m[k], out_v[k], out_g[k] = adamw("adamw_" + k, W[k], GRAD[k], M[k], V[k], with_grad=True)
    small_red = {}
    for part, keys in (("gm", SMALL_GM), ("rest", SMALL_REST)):
        name, shapes, s_send, s_recv, slots, _ = sharing[part]
        slots = split_wait(name + "_wait", share_build, s_send, s_recv, slots, out_v[early[-1]])[0]
        small_red.update(zip(keys, _unpack(sum_devices(slots), shapes)))
    for k in ("gm_ln_g", "gm_ln_b"):
        small_red[k] = lax.dynamic_slice_in_dim(small_red[k], q_me * dq, dq, axis=1)
    pk = lambda d: _pack([d[k] for k in SMALL])
    s_delta, s_m, s_v = adamw("adamw_small", pk(W), pk(small_red), pk(M), pk(V))
    shard_shapes = [W[k].shape for k in SMALL]
    out_g.update(small_red)
    for dct, packed in ((out_d, s_delta), (out_m, s_m), (out_v, s_v)):
        dct.update(zip(SMALL, _unpack(packed, shard_shapes)))

    finish_joins(finish_scatter(0, "mix", s_v))
    for k in late:
        out_d[k], out_m[k], out_v[k], out_g[k] = adamw("adamw_" + k, W[k], GRAD[k], M[k], V[k], with_grad=True)

    outs = [loss, grad_x.reshape(x.shape)]
    for dct in (out_g, out_d, out_m, out_v):
        outs += [dct[k] for k in WEIGHTS]
    return tuple(outs)


def kernel(x, p, hg_w_in, hg_lb_logits, hg_out_norm, hg_w_out, gm_w_in, gm_ln_g, gm_ln_b, gm_w_s, gm_b_s, gm_w_out, norm_mix_pre, norm_mix_post, norm_ffn_pre, norm_ffn_post, ffn_w_gate, ffn_w_up, ffn_w_down, ple_w_proj, ple_w_gate, ple_norm, loss_target, m_hg_w_in, m_hg_lb_logits, m_hg_out_norm, m_hg_w_out, m_gm_w_in, m_gm_ln_g, m_gm_ln_b, m_gm_w_s, m_gm_b_s, m_gm_w_out, m_norm_mix_pre, m_norm_mix_post, m_norm_ffn_pre, m_norm_ffn_post, m_ffn_w_gate, m_ffn_w_up, m_ffn_w_down, m_ple_w_proj, m_ple_w_gate, m_ple_norm, v_hg_w_in, v_hg_lb_logits, v_hg_out_norm, v_hg_w_out, v_gm_w_in, v_gm_ln_g, v_gm_ln_b, v_gm_w_s, v_gm_b_s, v_gm_w_out, v_norm_mix_pre, v_norm_mix_post, v_norm_ffn_pre, v_norm_ffn_post, v_ffn_w_gate, v_ffn_w_up, v_ffn_w_down, v_ple_w_proj, v_ple_w_gate, v_ple_norm):
    W = dict(zip(WEIGHTS, (hg_w_in, hg_lb_logits, hg_out_norm, hg_w_out, gm_w_in, gm_ln_g, gm_ln_b, gm_w_s, gm_b_s, gm_w_out,
                           norm_mix_pre, norm_mix_post, norm_ffn_pre, norm_ffn_post, ffn_w_gate, ffn_w_up, ffn_w_down,
                           ple_w_proj, ple_w_gate, ple_norm)))
    M = dict(zip(WEIGHTS, (m_hg_w_in, m_hg_lb_logits, m_hg_out_norm, m_hg_w_out, m_gm_w_in, m_gm_ln_g, m_gm_ln_b, m_gm_w_s,
                           m_gm_b_s, m_gm_w_out, m_norm_mix_pre, m_norm_mix_post, m_norm_ffn_pre, m_norm_ffn_post,
                           m_ffn_w_gate, m_ffn_w_up, m_ffn_w_down, m_ple_w_proj, m_ple_w_gate, m_ple_norm)))
    V = dict(zip(WEIGHTS, (v_hg_w_in, v_hg_lb_logits, v_hg_out_norm, v_hg_w_out, v_gm_w_in, v_gm_ln_g, v_gm_ln_b, v_gm_w_s,
                           v_gm_b_s, v_gm_w_out, v_norm_mix_pre, v_norm_mix_post, v_norm_ffn_pre, v_norm_ffn_post,
                           v_ffn_w_gate, v_ffn_w_up, v_ffn_w_down, v_ple_w_proj, v_ple_w_gate, v_ple_norm)))
    return _step(x, p, W, M, V, loss_target)
```

```python
import functools

import jax
import jax.numpy as jnp
from jax import lax
from jax.experimental import pallas as pl
from jax.experimental.pallas import tpu as pltpu

F32 = jnp.float32
BF16 = jnp.bfloat16
MESH_ID = pl.DeviceIdType.MESH

LANES = 128
N_CHIPS = 4
N_DEV = 8
VMEM_LIMIT = 56 * 1024 * 1024
HG_SUB = 256
HG_BLOCK = 512
HG_HEADS_PER = 2
GM_CHUNK = 128
GM_BLOCK = 512
PACK_ROWS = 512
LB_FLOOR = 1e-30
EPS = 1e-6
ADAM_LR, ADAM_B1, ADAM_B2, ADAM_EPS, ADAM_WD, ADAM_STEP = 0.001, 0.9, 0.999, 1e-08, 0.01, 10


def _tile(n, pref, mult=LANES):
    if n <= pref:
        return n
    t = (pref // mult) * mult
    while t >= mult:
        if n % t == 0:
            return t
        t -= mult
    return n


def _cp(n_axes):
    return pltpu.CompilerParams(dimension_semantics=("arbitrary",) * n_axes, vmem_limit_bytes=VMEM_LIMIT)


def _dense(block):
    return tuple(b for b in block if b is not None)


def _bmap(name, grid, ins, outs, compute, scalars=(), into=None):
    n_s, n_in = len(scalars), len(ins)
    n_extra = 0 if into is None else 1

    def body(*refs):
        in_refs = refs[n_s:n_s + n_in]
        out_refs = refs[n_s + n_in + n_extra:]
        vals = [r[...] for r in in_refs]
        res = compute(*vals)
        for r, o, spec in zip(out_refs, res, outs):
            keep = spec[4]
            if keep is None:
                r[...] = o.astype(r.dtype)
            else:
                first = functools.reduce(jnp.logical_and, [pl.program_id(a) == 0 for a in range(keep, len(grid))])

                @pl.when(first)
                def _():
                    r[...] = jnp.zeros(r.shape, r.dtype)

                r[...] += o.astype(r.dtype)

    grid_spec = pltpu.PrefetchScalarGridSpec(
        num_scalar_prefetch=n_s, grid=grid,
        in_specs=[pl.BlockSpec(b, m) for _, b, m in ins] + [pl.BlockSpec(memory_space=pl.ANY)] * n_extra,
        out_specs=[pl.BlockSpec(o[2], o[3]) for o in outs])
    return pl.pallas_call(
        body, name=name, grid_spec=grid_spec,
        out_shape=[jax.ShapeDtypeStruct(o[0], o[1]) for o in outs],
        input_output_aliases={n_s + n_in: 0} if n_extra else {},
        compiler_params=_cp(len(grid)),
    )(*scalars, *[a for a, _, _ in ins], *([into] if n_extra else []))


def bmap_fwd(name, fn, grid, ins, outs, scalars=(), into=None):
    return _bmap(name, grid, ins, outs, lambda *v: fn(*[x.astype(F32) for x in v]), scalars, into)


def bmap_bwd(name, fn, grid, ins, cots, grads, scalars=()):
    n_in = len(ins)
    diff = [g[0] for g in grads]
    cot_ins = [c for c in cots if c is not None]

    def compute(*vals):
        xs = [v.astype(F32) for v in vals[:n_in]]
        cvals = list(vals[n_in:])

        def f(*d):
            full = list(xs)
            for i, dv in zip(diff, d):
                full[i] = dv
            return tuple(fn(*full))

        res, pull = jax.vjp(f, *[xs[i] for i in diff])
        cts = []
        for r, c in zip(res, cots):
            cts.append(jnp.zeros_like(r) if c is None else cvals.pop(0).astype(F32))
        return pull(tuple(cts))

    outs = [(ins[i][0].shape, dt, ins[i][1], ins[i][2], keep) for i, dt, keep in grads]
    return _bmap(name, grid, list(ins) + cot_ins, outs, compute, scalars)


def _mm(name, a, b, out_shape, out_dtype, grid, a_spec, b_spec, o_spec, dims, addend=None, alias_out=None):
    nk = grid[2]
    o_dense = _dense(o_spec[0])
    o_dense = (o_dense[0] * o_dense[1], o_dense[2]) if len(o_dense) == 3 else o_dense
    has_add = addend is not None
    has_alias = alias_out is not None

    def body(*refs):
        a_ref, b_ref = refs[0], refs[1]
        pos = 2
        c_ref = None
        if has_add:
            c_ref = refs[pos]
            pos += 1
        if has_alias:
            pos += 1
        o_ref = refs[pos]
        acc_ref = refs[pos + 1] if nk > 1 else None
        bv = b_ref[...]
        if bv.ndim == 3:
            bv = bv.reshape(bv.shape[0] * bv.shape[1], bv.shape[2])
        p = lax.dot_general(a_ref[...].astype(BF16), bv.astype(BF16), (dims, ((), ())), preferred_element_type=F32)

        def finish(total):
            if has_add:
                total = total + c_ref[...].astype(F32)
            o_ref[...] = total.reshape(o_ref.shape).astype(o_ref.dtype)

        if nk == 1:
            finish(p)
        else:
            k = pl.program_id(2)

            @pl.when(k == 0)
            def _():
                acc_ref[...] = p

            @pl.when(jnp.logical_and(k > 0, k < nk - 1))
            def _():
                acc_ref[...] += p

            @pl.when(k == nk - 1)
            def _():
                finish(acc_ref[...] + p)

    in_specs = [pl.BlockSpec(*a_spec), pl.BlockSpec(*b_spec)]
    operands = [a, b]
    if has_add:
        in_specs.append(pl.BlockSpec(o_spec[0], o_spec[1]))
        operands.append(addend)
    aliases = {}
    if has_alias:
        in_specs.append(pl.BlockSpec(memory_space=pl.ANY))
        aliases = {len(operands): 0}
        operands.append(alias_out)
    return pl.pallas_call(
        body, name=name, grid=grid, in_specs=in_specs, out_specs=pl.BlockSpec(*o_spec),
        out_shape=jax.ShapeDtypeStruct(out_shape, out_dtype),
        scratch_shapes=[pltpu.VMEM(o_dense, F32)] if nk > 1 else [],
        input_output_aliases=aliases,
        compiler_params=pltpu.CompilerParams(dimension_semantics=("parallel", "parallel", "arbitrary"),
                                             vmem_limit_bytes=VMEM_LIMIT),
    )(*operands)


NN, NT, TN = ((1,), (0,)), ((1,), (1,)), ((0,), (0,))
TM = 512
TT = 1024
TN_PREF = 1408
WHOLE_K = 2048


def mm_fwd(name, x, wg, l, kind, out_dtype=F32, parts=False, xl=None):
    if isinstance(wg, dict):
        wg, l = wg[l], 0
    _, _, R, C = wg.shape
    T = x.shape[-2]
    tm = _tile(T, TM, 8)
    if kind == "col":
        tn = _tile(C, TN_PREF)
        npc = C // tn
        grid = (T // tm, N_CHIPS * npc, 1)
        a_blk = (tm, R) if xl is None else (None, tm, R)
        a_map = (lambda i, j, k: (i, 0)) if xl is None else (lambda i, j, k: (xl, i, 0))
        b_spec = ((None, None, R, tn), lambda i, j, k: (j // npc, l, 0, j % npc))
        if parts:
            out_shape = (N_CHIPS, T, C)
            o_spec = ((None, tm, tn), lambda i, j, k: (j // npc, i, j % npc))
        else:
            out_shape = (T, N_CHIPS * C)
            o_spec = ((tm, tn), lambda i, j, k: (i, j))
    elif N_CHIPS * R <= WHOLE_K:
        tn = _tile(C, 1024)
        grid = (T // tm, C // tn, 1)
        a_blk = (tm, N_CHIPS * R)
        a_map = lambda i, j, k: (i, 0)
        b_spec = ((N_CHIPS, None, R, tn), lambda i, j, k: (0, l, 0, j))
        out_shape = (T, C)
        o_spec = ((tm, tn), lambda i, j, k: (i, j))
    else:
        tn = _tile(C, 2048)
        grid = (T // tm, C // tn, N_CHIPS)
        a_blk = (tm, R)
        a_map = lambda i, j, k: (i, k)
        b_spec = ((None, None, R, tn), lambda i, j, k: (k, l, 0, j))
        out_shape = (T, C)
        o_spec = ((tm, tn), lambda i, j, k: (i, j))
    return _mm(name, x, wg, out_shape, out_dtype, grid, (a_blk, a_map), b_spec, o_spec, NN)


def mm_bwd_x(name, dy, wg, l, kind, out_dtype=F32, parts=False, addend=None):
    if isinstance(wg, dict):
        wg, l = wg[l], 0
    _, _, R, C = wg.shape
    T = dy.shape[-2]
    tm = _tile(T, TM, 8)
    if kind == "col":
        tk = _tile(C, TN_PREF)
        npc = C // tk
        tno = _tile(R, 2048)
        grid = (T // tm, R // tno, N_CHIPS * npc)
        if parts:
            a_spec = ((None, tm, tk), lambda i, j, k: (k // npc, i, k % npc))
        else:
            a_spec = ((tm, tk), lambda i, j, k: (i, k))
        b_spec = ((None, None, tno, tk), lambda i, j, k: (k // npc, l, j, k % npc))
        out_shape = (T, R)
        o_spec = ((tm, tno), lambda i, j, k: (i, j))
    elif N_CHIPS * R <= WHOLE_K:
        grid = (T // tm, 1, 1)
        a_spec = ((tm, C), lambda i, j, k: (i, 0))
        b_spec = ((N_CHIPS, None, R, C), lambda i, j, k: (0, l, 0, 0))
        out_shape = (T, N_CHIPS * R)
        o_spec = ((tm, N_CHIPS * R), lambda i, j, k: (i, 0))
    else:
        grid = (T // tm, N_CHIPS, 1)
        a_spec = ((tm, C), lambda i, j, k: (i, 0))
        b_spec = ((None, None, R, C), lambda i, j, k: (j, l, 0, 0))
        out_shape = (T, N_CHIPS * R)
        o_spec = ((tm, R), lambda i, j, k: (i, j))
    return _mm(name, dy, wg, out_shape, out_dtype, grid, a_spec, b_spec, o_spec, NT, addend=addend)


def mm_bwd_w(name, x, dy, dwg, l, kind, parts=False, xl=None):
    if isinstance(dwg, dict):
        return {**dwg, l: mm_bwd_w(name, x, dy, dwg[l], 0, kind, parts=parts, xl=xl)}
    _, _, R, C = dwg.shape
    T = dy.shape[-2]
    tt = _tile(T, TT, 16)
    nt = T // tt
    if kind == "col":
        tn = _tile(C, TN_PREF)
        npc = C // tn
        tr = _tile(R, 1024)
        grid = (R // tr, N_CHIPS * npc, nt)
        if xl is None:
            a_spec = ((tt, tr), lambda i, j, t: (t, i))
        else:
            a_spec = ((None, tt, tr), lambda i, j, t: (xl, t, i))
        if parts:
            b_spec = ((None, tt, tn), lambda i, j, t: (j // npc, t, j % npc))
        else:
            b_spec = ((tt, tn), lambda i, j, t: (t, j))
        o_spec = ((None, None, tr, tn), lambda i, j, t: (j // npc, l, i, j % npc))
    elif N_CHIPS * R <= WHOLE_K:
        tn = _tile(C, 1024)
        grid = (1, C // tn, nt)
        a_spec = ((tt, N_CHIPS * R), lambda i, j, t: (t, 0))
        b_spec = ((tt, tn), lambda i, j, t: (t, j))
        o_spec = ((N_CHIPS, None, R, tn), lambda i, j, t: (0, l, 0, j))
    else:
        tn = _tile(C, 1024)
        grid = (N_CHIPS, C // tn, nt)
        a_spec = ((tt, R), lambda i, j, t: (t, i))
        b_spec = ((tt, tn), lambda i, j, t: (t, j))
        o_spec = ((None, None, R, tn), lambda i, j, t: (i, l, 0, j))
    return _mm(name, x, dy, dwg.shape, dwg.dtype, grid, a_spec, b_spec, o_spec, TN, alias_out=dwg)


def _sigmoid(x):
    return 0.5 * jnp.tanh(0.5 * x) + 0.5


def f_swiglu(gate, up):
    return (gate * _sigmoid(gate) * up,)


def ffn_gate_up(name, x, wg_gate, wg_up, l):
    if isinstance(wg_gate, dict):
        wg_gate, wg_up, l = wg_gate[l], wg_up[l], 0
    _, _, R, C = wg_gate.shape
    T = x.shape[0]
    tm = _tile(T, TM, 8)
    tn = _tile(C, TN_PREF)
    npc = C // tn

    def body(x_ref, g_ref, u_ref, gate_ref, up_ref, act_ref):
        xv = x_ref[...].astype(BF16)
        gate = jnp.dot(xv, g_ref[...], preferred_element_type=F32)
        up = jnp.dot(xv, u_ref[...], preferred_element_type=F32)
        gate_ref[...] = gate
        up_ref[...] = up
        act_ref[...] = f_swiglu(gate, up)[0].astype(act_ref.dtype)

    w_spec = pl.BlockSpec((None, None, R, tn), lambda i, j: (j // npc, l, 0, j % npc))
    o_spec = pl.BlockSpec((tm, tn), lambda i, j: (i, j))
    N = N_CHIPS * C
    return pl.pallas_call(
        body, name=name, grid=(T // tm, N_CHIPS * npc),
        in_specs=[pl.BlockSpec((tm, R), lambda i, j: (i, 0)), w_spec, w_spec], out_specs=[o_spec, o_spec, o_spec],
        out_shape=[jax.ShapeDtypeStruct((T, N), F32), jax.ShapeDtypeStruct((T, N), F32), jax.ShapeDtypeStruct((T, N), BF16)],
        compiler_params=_cp(2),
    )(x, wg_gate, wg_up)


def ffn_in_dx(name, dgate, dup, wg_gate, wg_up, l):
    if isinstance(wg_gate, dict):
        wg_gate, wg_up, l = wg_gate[l], wg_up[l], 0
    _, _, R, C = wg_gate.shape
    T = dgate.shape[0]
    tm = _tile(T, TM, 8)
    tk = _tile(C, TN_PREF)
    npc = C // tk
    nk = N_CHIPS * npc

    def body(dg_ref, du_ref, wg_ref, wu_ref, o_ref, acc):
        k = pl.program_id(1)

        @pl.when(k == 0)
        def _():
            acc[...] = jnp.zeros(acc.shape, F32)

        @pl.when(k < nk)
        def _():
            acc[...] += lax.dot_general(dg_ref[...], wg_ref[...], (NT, ((), ())), preferred_element_type=F32)

        @pl.when(k >= nk)
        def _():
            acc[...] += lax.dot_general(du_ref[...], wu_ref[...], (NT, ((), ())), preferred_element_type=F32)

        @pl.when(k == 2 * nk - 1)
        def _():
            o_ref[...] = acc[...]

    first = lambda k: jnp.minimum(k, nk - 1)
    second = lambda k: jnp.maximum(k - nk, 0)
    return pl.pallas_call(
        body, name=name, grid=(T // tm, 2 * nk),
        in_specs=[pl.BlockSpec((tm, tk), lambda i, k: (i, first(k))), pl.BlockSpec((tm, tk), lambda i, k: (i, second(k))),
                  pl.BlockSpec((None, None, R, tk), lambda i, k: (first(k) // npc, l, 0, first(k) % npc)),
                  pl.BlockSpec((None, None, R, tk), lambda i, k: (second(k) // npc, l, 0, second(k) % npc))],
        out_specs=pl.BlockSpec((tm, R), lambda i, k: (i, 0)),
        out_shape=jax.ShapeDtypeStruct((T, R), F32),
        scratch_shapes=[pltpu.VMEM((tm, R), F32)],
        compiler_params=_cp(2),
    )(dgate, dup, wg_gate, wg_up)


def ffn_down_dx(name, df, wg_down, l, gate, up):
    if isinstance(wg_down, dict):
        wg_down, l = wg_down[l], 0
    _, _, R, C = wg_down.shape
    T = df.shape[0]
    tm = _tile(T, TM, 8)

    def body(df_ref, w_ref, gate_ref, up_ref, dg_ref, du_ref):
        dact = lax.dot_general(df_ref[...].astype(BF16), w_ref[...], (NT, ((), ())), preferred_element_type=F32)
        _, pull = jax.vjp(lambda g, u: f_swiglu(g, u)[0], gate_ref[...], up_ref[...])
        dg, du = pull(dact)
        dg_ref[...] = dg.astype(dg_ref.dtype)
        du_ref[...] = du.astype(du_ref.dtype)

    t_spec = pl.BlockSpec((tm, R), lambda i, j: (i, j))
    return pl.pallas_call(
        body, name=name, grid=(T // tm, N_CHIPS),
        in_specs=[pl.BlockSpec((tm, C), lambda i, j: (i, 0)), pl.BlockSpec((None, None, R, C), lambda i, j: (j, l, 0, 0)),
                  t_spec, t_spec],
        out_specs=[t_spec, t_spec],
        out_shape=[jax.ShapeDtypeStruct((T, N_CHIPS * R), BF16)] * 2,
        compiler_params=_cp(2),
    )(df, wg_down, gate, up)


def _rms(x, g):
    return x * lax.rsqrt(jnp.mean(x * x, axis=-1, keepdims=True) + EPS) * g


def f_prenorm(h, g):
    return (_rms(h, g),)


def f_prenorm_thru(h, g):
    return _rms(h, g), h


def f_post_pre(h, m, g_post, g_pre):
    h1 = h + _rms(m, g_post)
    return h1, _rms(h1, g_pre)


def f_post(h1, f, g):
    return (h1 + _rms(f, g),)


def f_ple(h2, e, zg, g):
    return (h2 + _rms(e * _sigmoid(zg), g),)


def f_ple_pre(h2, e, zg, g, g_next):
    h3 = h2 + _rms(e * _sigmoid(zg), g)
    return h3, _rms(h3, g_next)


def _gelu(x):
    return 0.5 * x * (1.0 + lax.erf(x * 0.7071067811865476))


def f_gm_in(z, ln_g, ln_b):
    w = z.shape[-1] // 2
    u = _gelu(z[:, :w])
    v = _gelu(z[:, w:])
    mu = jnp.mean(v, axis=-1, keepdims=True)
    vc = v - mu
    vn = vc * lax.rsqrt(jnp.mean(vc * vc, axis=-1, keepdims=True) + EPS) * ln_g + ln_b
    return u, vn


def f_gm_spatial(u, vn, ws, bs):
    t = lax.broadcasted_iota(jnp.int32, ws.shape, 0)
    s = lax.broadcasted_iota(jnp.int32, ws.shape, 1)
    wm = jnp.where(t >= s, ws, 0.0).astype(BF16)
    ys = []
    for n in range(u.shape[0] // GM_CHUNK):
        rows = slice(n * GM_CHUNK, (n + 1) * GM_CHUNK)
        sv = jnp.dot(wm, vn[rows].astype(BF16), preferred_element_type=F32) + bs
        ys.append(u[rows] * sv)
    return (jnp.concatenate(ys, axis=0) if len(ys) > 1 else ys[0],)


def f_adam(w, g, m, v):
    m = ADAM_B1 * m + (1.0 - ADAM_B1) * g
    v = ADAM_B2 * v + (1.0 - ADAM_B2) * jnp.square(g)
    m_hat = m / (1.0 - ADAM_B1 ** ADAM_STEP)
    v_hat = v / (1.0 - ADAM_B2 ** ADAM_STEP)
    delta = -ADAM_LR * (m_hat / (jnp.sqrt(v_hat) + ADAM_EPS) + ADAM_WD * w)
    return delta, m, v


def _make_f_lb(n_layers):
    def f_lb(*logits):
        mx = functools.reduce(jnp.maximum, logits)
        ex = [jnp.exp(r - mx) for r in logits]
        tot = functools.reduce(lambda a, b: a + b, ex)
        sm = [e / tot for e in ex]
        outs = []
        run = jnp.zeros_like(sm[0])
        for j in range(n_layers):
            if j > 0:
                run = run + sm[j]
            lb = run
            outs += [jnp.log(jnp.maximum(lb, LB_FLOOR)), jnp.log(1.0 - lb), 1.0 - lb]
        return tuple(outs)
    return f_lb


def rows_fwd(name, fn, rows, params, out_dtypes, tm=256):
    T = rows[0].shape[0]
    tm = _tile(T, tm, 16)
    ins = [(r, (tm, r.shape[1]), lambda i: (i, 0)) for r in rows]
    ins += [(p, p.shape, lambda i: (0, 0)) for p in params]
    shapes = jax.eval_shape(lambda *a: fn(*a), *[jax.ShapeDtypeStruct((tm, r.shape[1]), F32) for r in rows],
                            *[jax.ShapeDtypeStruct(p.shape, F32) for p in params])
    outs = [((T, s.shape[1]), dt, (tm, s.shape[1]), lambda i: (i, 0), None) for s, dt in zip(shapes, out_dtypes)]
    return bmap_fwd(name, fn, (T // tm,), ins, outs)


def rows_bwd(name, fn, rows, params, cots, row_grad_dtypes, tm=256):
    T = rows[0].shape[0]
    tm = _tile(T, tm, 16)
    ins = [(r, (tm, r.shape[1]), lambda i: (i, 0)) for r in rows]
    ins += [(p, p.shape, lambda i: (0, 0)) for p in params]
    cts = [None if c is None else (c, (tm, c.shape[1]), lambda i: (i, 0)) for c in cots]
    grads = [(i, dt, None) for i, dt in enumerate(row_grad_dtypes) if dt is not None]
    grads += [(len(rows) + j, F32, 0) for j in range(len(params))]
    return bmap_bwd(name, fn, (T // tm,), ins, cts, grads)


def _log_sigmoid(z):
    return jnp.minimum(z, 0.0) - jnp.log(1.0 + jnp.exp(-jnp.abs(z)))


def _hg_gates(zf, ll0, ll1, oml):
    x2 = ll1 + _log_sigmoid(zf)
    mx = jnp.maximum(ll0, x2)
    g = mx + jnp.log(jnp.exp(ll0 - mx) + jnp.exp(x2 - mx))
    return g, oml * _sigmoid(-zf)


def hg_constants(n):
    levels = n.bit_length() - 1
    r = jnp.arange(n, dtype=jnp.int32)
    bounds = [r] + [((r >> (s + 1)) << (s + 1)) + ((1 << s) - 1) for s in range(levels)]
    sel = jnp.concatenate([(r[None, :] <= bd[:, None]) for bd in bounds], axis=0).astype(BF16)
    later = jnp.stack([((r >> s) & 1) for s in range(levels)])
    sign = jnp.broadcast_to((2 * later - 1).astype(F32)[:, :, None], (levels, n, LANES))
    pair = jnp.stack([((r[:, None] >> (s + 1)) == (r[None, :] >> (s + 1))) & (later[s][:, None] == 1) & (later[s][None, :] == 0)
                      for s in range(levels)]).astype(F32)
    return sel, sel.T, sign, pair


def _dot2(m, x):
    hi = x.astype(BF16)
    lo = (x - hi.astype(F32)).astype(BF16)
    p = jnp.dot(m, jnp.concatenate([hi, lo], axis=1), preferred_element_type=F32)
    w = x.shape[1]
    return p[:, :w] + p[:, w:]


@jax.custom_vjp
def _sel_dot(sel, selt, g):
    return _dot2(sel, g)


def _sel_dot_fwd(sel, selt, g):
    return _dot2(sel, g), (sel, selt)


def _sel_dot_bwd(res, d):
    sel, selt = res
    return jnp.zeros_like(sel), jnp.zeros_like(selt), _dot2(selt, d)


_sel_dot.defvjp(_sel_dot_fwd, _sel_dot_bwd)


def _hg_state(st, zf, zi, ll0, ll1, oml, tri):
    g, k = _hg_gates(zf, ll0, ll1, oml)
    b = _dot2(tri, g)
    tot = jnp.sum(g, axis=0, keepdims=True)
    kd = k * jnp.exp(tot - b)
    return st * jnp.exp(tot) + jnp.dot(zi.T.astype(BF16), kd.astype(BF16), preferred_element_type=F32)


def _hg_step(st, zq, zf, zi, zg, ll0, ll1, oml, onorm, sel, selt, sign, pair):
    n = zq.shape[0]
    levels = n.bit_length() - 1
    q = zq * _sigmoid(zq)
    g, k = _hg_gates(zf, ll0, ll1, oml)
    sums = _sel_dot(sel, selt, g)
    b = sums[:n]
    tot = jnp.sum(g, axis=0, keepdims=True)
    o = lax.dot_general((q * jnp.exp(b)).astype(BF16), st.astype(BF16), (NT, ((), ())), preferred_element_type=F32)
    a = jnp.zeros((n, n), F32)
    for s in range(levels):
        e = jnp.exp(sign[s] * (b - sums[(s + 1) * n:(s + 2) * n]))
        al = lax.dot_general((q * e).astype(BF16), (k * e).astype(BF16), (NT, ((), ())), preferred_element_type=F32)
        a = a + pair[s] * al
    o = o + jnp.dot(a.astype(BF16), zi.astype(BF16), preferred_element_type=F32)
    o = o + jnp.sum(q * k, axis=1, keepdims=True) * zi
    kd = k * jnp.exp(tot - b)
    st_new = st * jnp.exp(tot) + jnp.dot(zi.T.astype(BF16), kd.astype(BF16), preferred_element_type=F32)
    og = _rms(o, onorm) * (zg * _sigmoid(zg))
    return og, st_new


def _whole(arr, n_grid):
    zeros = (0,) * arr.ndim
    return pl.BlockSpec(arr.shape, (lambda h, n: zeros) if n_grid == 2 else (lambda i: zeros))


def _hg_dims(proj4, n_seq):
    _, T, D = proj4.shape
    S = T // n_seq
    hp = HG_HEADS_PER if (D // LANES) % HG_HEADS_PER == 0 else 1
    tb = min(HG_BLOCK, S)
    streams = [(b, hl) for b in range(n_seq) for hl in range(hp)]
    return T, D, S, hp, D // (LANES * hp), LANES * hp, tb, S // tb, tb // HG_SUB, streams


def hgrn_fwd(name, proj4, ll0, ll1, oml, onorm, n_seq):
    T, D, S, hp, n_hg, W, tb, nblk, nsub, streams = _hg_dims(proj4, n_seq)
    ns = len(streams)

    def body(p_ref, ll0_ref, ll1_ref, oml_ref, on_ref, sel_ref, selt_ref, later_ref, pair_ref, og_ref, st_ref, st):
        @pl.when(pl.program_id(1) == 0)
        def _():
            st[...] = jnp.zeros(st.shape, F32)

        st_ref[...] = st[...]
        on = on_ref[...]

        def step(j, carry):
            r = pl.ds(pl.multiple_of(j * HG_SUB, HG_SUB), HG_SUB)
            consts = (sel_ref[...], selt_ref[...], later_ref[...], pair_ref[...])
            args = []
            for si, (b, hl) in enumerate(streams):
                ln = slice(hl * LANES, (hl + 1) * LANES)
                args.append((st[si], p_ref[0, b, r, ln], p_ref[1, b, r, ln], p_ref[2, b, r, ln], p_ref[3, b, r, ln],
                             ll0_ref[:, ln], ll1_ref[:, ln], oml_ref[:, ln], on) + consts)
            res = [_hg_step(*a) for a in args]
            for si, (b, hl) in enumerate(streams):
                og_ref[b, r, hl * LANES:(hl + 1) * LANES] = res[si][0].astype(og_ref.dtype)
                st[si] = res[si][1]
            return carry

        lax.fori_loop(0, nsub, step, 0)

    vec = pl.BlockSpec((1, W), lambda h, n: (0, h))
    consts = hg_constants(HG_SUB)
    og, states = pl.pallas_call(
        body, name=name, grid=(n_hg, nblk),
        in_specs=[pl.BlockSpec((4, n_seq, tb, W), lambda h, n: (0, 0, n, h)), vec, vec, vec,
                  pl.BlockSpec((1, LANES), lambda h, n: (0, 0))] + [_whole(c, 2) for c in consts],
        out_specs=[pl.BlockSpec((n_seq, tb, W), lambda h, n: (0, n, h)),
                   pl.BlockSpec((None, None, ns, LANES, LANES), lambda h, n: (h, n, 0, 0, 0))],
        out_shape=[jax.ShapeDtypeStruct((n_seq, S, D), BF16),
                   jax.ShapeDtypeStruct((n_hg, nblk, ns, LANES, LANES), F32)],
        scratch_shapes=[pltpu.VMEM((ns, LANES, LANES), F32)],
        compiler_params=_cp(2),
    )(proj4.reshape(4, n_seq, S, D), ll0, ll1, oml, onorm, *consts)
    return og.reshape(T, D), states


def hgrn_bwd(name, proj4, states, dog, ll0, ll1, oml, onorm, n_seq):
    T, D, S, hp, n_hg, W, tb, nblk, nsub, streams = _hg_dims(proj4, n_seq)
    ns = len(streams)

    def body(p_ref, st_ref, dog_ref, ll0_ref, ll1_ref, oml_ref, on_ref, sel_ref, selt_ref, later_ref, pair_ref,
             dp_ref, dll0_ref, dll1_ref, doml_ref, don_ref, sbuf, dst):
        n_id = pl.program_id(1)

        @pl.when(n_id == 0)
        def _():
            dst[...] = jnp.zeros(dst.shape, F32)
            for ref in (dll0_ref, dll1_ref, doml_ref):
                ref[...] = jnp.zeros(ref.shape, F32)

        @pl.when(jnp.logical_and(n_id == 0, pl.program_id(0) == 0))
        def _():
            don_ref[...] = jnp.zeros(don_ref.shape, F32)

        on = on_ref[...]

        def fwd(j, carry):
            r = pl.ds(pl.multiple_of(j * HG_SUB, HG_SUB), HG_SUB)
            tri = sel_ref[0:HG_SUB, :]
            args = []
            for si, (b, hl) in enumerate(streams):
                ln = slice(hl * LANES, (hl + 1) * LANES)
                args.append((carry[si], p_ref[1, b, r, ln], p_ref[2, b, r, ln],
                             ll0_ref[:, ln], ll1_ref[:, ln], oml_ref[:, ln], tri))
            for si in range(ns):
                sbuf[si, j] = carry[si]
            return tuple(_hg_state(*a) for a in args)

        lax.fori_loop(0, nsub, fwd, tuple(st_ref[si] for si in range(ns)))

        def bwd(jj, carry):
            j = nsub - 1 - jj
            r = pl.ds(pl.multiple_of(j * HG_SUB, HG_SUB), HG_SUB)
            args, cts = [], []
            for si, (b, hl) in enumerate(streams):
                ln = slice(hl * LANES, (hl + 1) * LANES)
                args.append((sbuf[si, j], p_ref[0, b, r, ln], p_ref[1, b, r, ln], p_ref[2, b, r, ln],
                             p_ref[3, b, r, ln], ll0_ref[:, ln], ll1_ref[:, ln], oml_ref[:, ln], on))
                cts.append((dog_ref[b, r, ln].astype(F32), dst[si]))
            consts = (sel_ref[...], selt_ref[...], later_ref[...], pair_ref[...])
            step_fn = lambda *a: _hg_step(*a, *consts)
            ds = [jax.vjp(step_fn, *a)[1](ct) for a, ct in zip(args, cts)]
            d_on = carry
            for si, (b, hl) in enumerate(streams):
                ln = slice(hl * LANES, (hl + 1) * LANES)
                d = ds[si]
                dst[si] = d[0]
                for part in range(4):
                    dp_ref[part, b, r, ln] = d[1 + part].astype(dp_ref.dtype)
                dll0_ref[:, ln] += d[5]
                dll1_ref[:, ln] += d[6]
                doml_ref[:, ln] += d[7]
                d_on = d_on + d[8]
            return d_on

        don_ref[...] += lax.fori_loop(0, nsub, bwd, jnp.zeros((1, LANES), F32))

    last = nblk - 1
    vec = pl.BlockSpec((1, W), lambda h, n: (0, h))
    one = pl.BlockSpec((1, LANES), lambda h, n: (0, 0))
    consts = hg_constants(HG_SUB)
    dproj, d0, d1, d2, d_on = pl.pallas_call(
        body, name=name, grid=(n_hg, nblk),
        in_specs=[pl.BlockSpec((4, n_seq, tb, W), lambda h, n: (0, 0, last - n, h)),
                  pl.BlockSpec((None, None, ns, LANES, LANES), lambda h, n: (h, last - n, 0, 0, 0)),
                  pl.BlockSpec((n_seq, tb, W), lambda h, n: (0, last - n, h)), vec, vec, vec, one]
        + [_whole(c, 2) for c in consts],
        out_specs=[pl.BlockSpec((4, n_seq, tb, W), lambda h, n: (0, 0, last - n, h)), vec, vec, vec, one],
        out_shape=[jax.ShapeDtypeStruct((4, n_seq, S, D), BF16)] + [jax.ShapeDtypeStruct((1, D), F32)] * 3
        + [jax.ShapeDtypeStruct((1, LANES), F32)],
        scratch_shapes=[pltpu.VMEM((ns, nsub, LANES, LANES), F32), pltpu.VMEM((ns, LANES, LANES), F32)],
        compiler_params=_cp(2),
    )(proj4.reshape(4, n_seq, S, D), states, dog.reshape(n_seq, S, D), ll0, ll1, oml, onorm, *consts)
    return dproj.reshape(4, T, D), d0, d1, d2, d_on


def _place():
    x, y, c = lax.axis_index("x"), lax.axis_index("y"), lax.axis_index("c")
    chips = [(1 - x, y), (x, 1 - y), (1 - x, 1 - y)]
    return x, y, c, chips


ANY = pl.BlockSpec(memory_space=pl.ANY)


def _comm_call(name, body, ins, out_shapes, sems, aliases=None):
    return pl.pallas_call(
        body, name=name, in_specs=[ANY] * len(ins), out_specs=[ANY] * len(out_shapes),
        out_shape=out_shapes, scratch_shapes=sems, input_output_aliases=aliases or {},
        compiler_params=pltpu.CompilerParams(has_side_effects=True),
    )(*ins)


HBM_SPEC = pl.BlockSpec(memory_space=pltpu.HBM)
SEM_SPEC = pl.BlockSpec(memory_space=pltpu.SEMAPHORE)
SPLIT_EFFECT = pltpu.SideEffectType.DATAFLOW_SIDE_EFFECTING
N_PEER_CHIPS = 3


def split_start(name, build, arrays, n_peers=N_PEER_CHIPS):
    n = len(arrays)

    def body(*refs):
        send, recv = refs[n], refs[n + 1]
        token = refs[2 * n + 2]
        starts, _ = build(refs[:n], send, recv)
        for cp in starts:
            cp.start()
        token[...] = jnp.zeros_like(token)

    res = pl.pallas_call(
        body, name=name,
        out_shape=(pltpu.SemaphoreType.DMA((n_peers,)), pltpu.SemaphoreType.DMA((n_peers,)),
                   *[pltpu.HBM(a.shape, a.dtype) for a in arrays], jax.ShapeDtypeStruct((8, LANES), F32)),
        in_specs=[HBM_SPEC] * n,
        out_specs=(SEM_SPEC, SEM_SPEC, *[HBM_SPEC] * n, pl.BlockSpec(memory_space=pltpu.VMEM)),
        input_output_aliases={i: 2 + i for i in range(n)},
        compiler_params=pltpu.CompilerParams(has_side_effects=SPLIT_EFFECT),
    )(*[pltpu.with_memory_space_constraint(a, pltpu.HBM) for a in arrays])
    return res[0], res[1], list(res[2:2 + n]), res[2 + n]


def split_wait(name, build, send, recv, arrays, after):
    n = len(arrays)

    def body(*refs):
        starts, arrivals = build(refs[:n], refs[n], refs[n + 1])
        for cp in starts:
            cp.wait_send()
        for cp in arrivals:
            cp.wait_recv()

    return list(pl.pallas_call(
        body, name=name, out_shape=tuple(pltpu.HBM(a.shape, a.dtype) for a in arrays),
        in_specs=[HBM_SPEC] * n + [SEM_SPEC, SEM_SPEC, ANY], out_specs=tuple([HBM_SPEC] * n),
        input_output_aliases={i: i for i in range(n)},
        compiler_params=pltpu.CompilerParams(has_side_effects=SPLIT_EFFECT),
    )(*arrays, send, recv, after))


def _row_half(ref, dim, who):
    rh = ref.shape[dim] // 2
    return pl.ds(who * rh, rh)


def gather_build(refs, send, recv):
    x, y, c, chips = _place()
    q = 2 * x + y
    starts, arrivals = [], []
    for buf in refs:
        rows = _row_half(buf, 2, c)
        for j, (px, py) in enumerate(chips):
            mine, got = buf.at[q, :, rows], buf.at[2 * px + py, :, rows]
            starts.append(pltpu.make_async_remote_copy(src_ref=mine, dst_ref=mine, send_sem=send.at[j], recv_sem=recv.at[j],
                                                       device_id=(px, py, c), device_id_type=MESH_ID))
            arrivals.append(pltpu.make_async_remote_copy(src_ref=got, dst_ref=got, send_sem=send.at[j], recv_sem=recv.at[j],
                                                         device_id=(px, py, c), device_id_type=MESH_ID))
    return starts, arrivals


def forward_build(refs, send, recv):
    x, y, c, chips = _place()
    sib = (x, y, 1 - c)
    starts, arrivals = [], []
    for buf in refs:
        for j, (px, py) in enumerate(chips):
            got = buf.at[2 * px + py, :, _row_half(buf, 2, c)]
            theirs = buf.at[2 * px + py, :, _row_half(buf, 2, 1 - c)]
            starts.append(pltpu.make_async_remote_copy(src_ref=got, dst_ref=got, send_sem=send.at[j], recv_sem=recv.at[j],
                                                       device_id=sib, device_id_type=MESH_ID))
            arrivals.append(pltpu.make_async_remote_copy(src_ref=theirs, dst_ref=theirs, send_sem=send.at[j], recv_sem=recv.at[j],
                                                         device_id=sib, device_id_type=MESH_ID))
    return starts, arrivals


def scatter_build(refs, send, recv):
    n = len(refs) // 2
    x, y, c, _ = _place()
    starts, arrivals = [], []
    for a in range(n):
        src, dst = refs[a], refs[n + a]
        for k in range(1, N_DEV):
            px, py, pc = x ^ (k >> 2), y ^ ((k >> 1) & 1), c ^ (k & 1)
            theirs = src.at[2 * px + py, :, _row_half(src, 2, pc)]
            starts.append(pltpu.make_async_remote_copy(src_ref=theirs, dst_ref=dst.at[k - 1], send_sem=send.at[k - 1],
                                                       recv_sem=recv.at[k - 1], device_id=(px, py, pc), device_id_type=MESH_ID))
            arrivals.append(pltpu.make_async_remote_copy(src_ref=dst.at[k - 1], dst_ref=dst.at[k - 1], send_sem=send.at[k - 1],
                                                         recv_sem=recv.at[k - 1], device_id=(px, py, pc), device_id_type=MESH_ID))
    return starts, arrivals


def share_build(refs, send, recv):
    dst = refs[0]
    x, y, c, _ = _place()
    mine = dst.at[4 * x + 2 * y + c]
    starts, arrivals = [], []
    for k in range(1, N_DEV):
        px, py, pc = x ^ (k >> 2), y ^ ((k >> 1) & 1), c ^ (k & 1)
        got = dst.at[4 * px + 2 * py + pc]
        starts.append(pltpu.make_async_remote_copy(src_ref=mine, dst_ref=mine, send_sem=send.at[k - 1], recv_sem=recv.at[k - 1],
                                                   device_id=(px, py, pc), device_id_type=MESH_ID))
        arrivals.append(pltpu.make_async_remote_copy(src_ref=got, dst_ref=got, send_sem=send.at[k - 1], recv_sem=recv.at[k - 1],
                                                     device_id=(px, py, pc), device_id_type=MESH_ID))
    return starts, arrivals


def join_build(refs, send, recv, layers):
    x, y, c, _ = _place()
    sib = (x, y, 1 - c)
    starts, arrivals = [], []
    for buf, l in zip(refs, layers):
        mine, theirs = buf.at[l, _row_half(buf, 1, c)], buf.at[l, _row_half(buf, 1, 1 - c)]
        starts.append(pltpu.make_async_remote_copy(src_ref=mine, dst_ref=mine, send_sem=send.at[0], recv_sem=recv.at[0],
                                                   device_id=sib, device_id_type=MESH_ID))
        arrivals.append(pltpu.make_async_remote_copy(src_ref=theirs, dst_ref=theirs, send_sem=send.at[0], recv_sem=recv.at[0],
                                                     device_id=sib, device_id_type=MESH_ID))
    return starts, arrivals


def join_row_halves(name, bufs):
    n = len(bufs)

    def body(*refs):
        dst = refs[n:2 * n]
        send, recv = refs[2 * n:]
        x, y, c, _ = _place()
        cps = []
        for a in range(n):
            mine = dst[a].at[:, _row_half(dst[a], 1, c)]
            cps.append(pltpu.make_async_remote_copy(src_ref=mine, dst_ref=mine, send_sem=send.at[a], recv_sem=recv.at[a],
                                                    device_id=(x, y, 1 - c), device_id_type=MESH_ID))
        for cp in cps:
            cp.start()
        for a in range(n):
            theirs = dst[a].at[:, _row_half(dst[a], 1, 1 - c)]
            pltpu.make_async_remote_copy(src_ref=theirs, dst_ref=theirs, send_sem=send.at[a], recv_sem=recv.at[a],
                                         device_id=(x, y, 1 - c), device_id_type=MESH_ID).wait_recv()
        for cp in cps:
            cp.wait_send()

    outs = [jax.ShapeDtypeStruct(b.shape, b.dtype) for b in bufs]
    sems = [pltpu.SemaphoreType.DMA((n,))] * 2
    return _comm_call(name, body, bufs, outs, sems, aliases={a: a for a in range(n)})


def share_with_all(name, packed, me):
    slots = lax.dynamic_update_slice(jnp.zeros((N_DEV,) + packed.shape, packed.dtype), packed[None], (me, 0, 0))

    def body(_, dst, send, recv):
        x, y, c, _ = _place()
        me = 4 * x + 2 * y + c
        cps = []
        for k in range(1, N_DEV):
            px, py, pc = x ^ (k >> 2), y ^ ((k >> 1) & 1), c ^ (k & 1)
            cps.append(pltpu.make_async_remote_copy(src_ref=dst.at[me], dst_ref=dst.at[me], send_sem=send.at[k - 1],
                                                    recv_sem=recv.at[k - 1], device_id=(px, py, pc), device_id_type=MESH_ID))
        for cp in cps:
            cp.start()
        for k in range(1, N_DEV):
            px, py, pc = x ^ (k >> 2), y ^ ((k >> 1) & 1), c ^ (k & 1)
            got = dst.at[4 * px + 2 * py + pc]
            pltpu.make_async_remote_copy(src_ref=got, dst_ref=got, send_sem=send.at[k - 1], recv_sem=recv.at[k - 1],
                                         device_id=(px, py, pc), device_id_type=MESH_ID).wait_recv()
        for cp in cps:
            cp.wait_send()

    outs = [jax.ShapeDtypeStruct(slots.shape, slots.dtype)]
    sems = [pltpu.SemaphoreType.DMA((N_DEV - 1,))] * 2
    return _comm_call(name, body, [slots], outs, sems, aliases={0: 0})[0]


def _w_tiles(R, C):
    return _tile(R, max(16, (1 << 20) // (4 * C) // 16 * 16), 16)


def cast_bf16(w, l, q_arr):
    _, R, C = w.shape
    tr = _w_tiles(R, C)
    ins = [(w, (None, tr, C), lambda r, q: (l, r, 0))]
    outs = [((N_CHIPS, 1, R, C), BF16, (None, None, tr, C), lambda r, q: (q[0], 0, r, 0), None)]
    return bmap_fwd("cast_bf16", lambda a: (a,), (R // tr,), ins, outs, scalars=(q_arr,))[0]


def sum_partials(own, landed, into, l, q_arr, c_arr):
    n_land, _, rh, C = landed.shape
    tr = _w_tiles(rh, C)
    nb = rh // tr
    blk = (None, None, tr, C)
    ins = [(own, blk, lambda r, q, c: (q[0], 0, c[0] * nb + r, 0))]
    ins += [(landed, blk, (lambda r, q, c, kk=kk: (kk, 0, r, 0))) for kk in range(n_land)]
    outs = [(into.shape, F32, (None, tr, C), lambda r, q, c: (l, c[0] * nb + r, 0), None)]
    return bmap_fwd("sum_partials", lambda *t: (functools.reduce(lambda u, v: u + v, t),), (nb,), ins, outs,
                    scalars=(q_arr, c_arr), into=into)[0]


def sum_devices(slots):
    nd, NR, C = slots.shape
    tr = _tile(NR, 512, 8)
    ins = [(slots, (None, tr, C), (lambda r, dd=dd: (dd, r, 0))) for dd in range(nd)]
    outs = [((NR, C), F32, (tr, C), lambda r: (r, 0), None)]
    return bmap_fwd("sum_devices", lambda *a: (functools.reduce(lambda u, v: u + v, a),), (NR // tr,), ins, outs)[0]


def adamw(name, w, g, m, v, with_grad=False):
    if w.ndim == 2:
        R, C = w.shape
        tr = _w_tiles(R, C)
        spec = ((tr, C), lambda r: (r, 0))
        grid = (R // tr,)
    else:
        L, R, C = w.shape
        tr = _w_tiles(R, C)
        spec = ((None, tr, C), lambda l, r: (l, r, 0))
        grid = (L, R // tr)
    ins = [(a,) + spec for a in (w, g, m, v)]
    outs = [(w.shape, F32) + spec + (None,)] * (4 if with_grad else 3)
    fn = (lambda a, b, c, d: f_adam(a, b, c, d) + (b,)) if with_grad else f_adam
    return bmap_fwd(name, fn, grid, ins, outs)


def loss_and_grad(h, target):
    T, D = h.shape
    tm = _tile(T, 256, 8)

    def fn(hv, tv):
        d = hv - tv
        return jnp.sum(d * d, keepdims=True).reshape(1, 1) * (0.5 / D), d * (1.0 / D)

    ins = [(h, (tm, D), lambda i: (i, 0)), (target, (tm, D), lambda i: (i, 0))]
    outs = [((1, 1), F32, (1, 1), lambda i: (0, 0), 0), ((T, D), F32, (tm, D), lambda i: (i, 0), None)]
    return bmap_fwd("loss_and_grad", fn, (T // tm,), ins, outs)


BIG = ("hg_w_in", "hg_w_out", "gm_w_in", "gm_w_out", "ffn_w_gate", "ffn_w_up", "ffn_w_down", "ple_w_proj", "ple_w_gate")
KIND = {"hg_w_in": "col", "hg_w_out": "row", "gm_w_in": "col", "gm_w_out": "row", "ffn_w_gate": "col",
        "ffn_w_up": "col", "ffn_w_down": "row", "ple_w_proj": "col", "ple_w_gate": "row"}
SMALL = ("hg_lb_logits", "hg_out_norm", "gm_ln_g", "gm_ln_b", "gm_w_s", "gm_b_s", "norm_mix_pre", "norm_mix_post",
         "norm_ffn_pre", "norm_ffn_post", "ple_norm")
SMALL_GM = ("gm_ln_g", "gm_ln_b", "gm_w_s", "gm_b_s")
SMALL_REST = tuple(k for k in SMALL if k not in SMALL_GM)
WEIGHTS = ("hg_w_in", "hg_lb_logits", "hg_out_norm", "hg_w_out", "gm_w_in", "gm_ln_g", "gm_ln_b", "gm_w_s", "gm_b_s",
           "gm_w_out", "norm_mix_pre", "norm_mix_post", "norm_ffn_pre", "norm_ffn_post", "ffn_w_gate", "ffn_w_up",
           "ffn_w_down", "ple_w_proj", "ple_w_gate", "ple_norm")


def _pack(arrs):
    rows = []
    for a in arrs:
        flat = a.reshape(-1)
        pad = (-flat.shape[0]) % (8 * LANES)
        rows.append(jnp.pad(flat, (0, pad)).reshape(-1, LANES))
    n_rows = sum(r.shape[0] for r in rows)
    rows.append(jnp.zeros(((-n_rows) % PACK_ROWS, LANES), F32))
    return jnp.concatenate(rows, axis=0)


def _unpack(packed, shapes):
    out, r = [], 0
    for s in shapes:
        size = 1
        for d in s:
            size *= d
        nr = -(-size // (8 * LANES)) * 8
        out.append(packed[r:r + nr].reshape(-1)[:size].reshape(s))
        r += nr
    return out


def _step(x, p, W, M, V, loss_target):
    n_seq, S, D = x.shape
    T = n_seq * S
    depth = p.shape[0]
    n_hg = W["hg_w_in"].shape[0]
    x2 = x.reshape(T, D)
    p3 = p.reshape(depth, T, p.shape[-1])
    tgt = loss_target.reshape(T, D)
    xi, yi, ci = lax.axis_index("x"), lax.axis_index("y"), lax.axis_index("c")
    q_me = 2 * xi + yi
    c_arr = jnp.reshape(ci, (1,)).astype(jnp.int32)
    q_arr = jnp.reshape(q_me, (1,)).astype(jnp.int32)

    groups = {}
    for i in range(depth):
        mix = ("hg_w_in", "hg_w_out") if i % 2 == 0 else ("gm_w_in", "gm_w_out")
        groups[i, "mix"] = [(k, i // 2) for k in mix]
        groups[i, "rest"] = [(k, i) for k in ("ffn_w_gate", "ffn_w_up", "ffn_w_down", "ple_w_proj", "ple_w_gate")]
    G = {k: {} for k in BIG}
    DW = {k: {l: lax.empty((N_CHIPS, 1) + W[k].shape[1:], BF16) for l in range(W[k].shape[0])} for k in BIG}
    in_flight = {}

    casts = {}

    def cast_group(i, part, dep):
        qa = q_arr if dep is None else lax.optimization_barrier((q_arr, dep))[0]
        casts[i, part] = [cast_bf16(W[k], l, qa) for k, l in groups[i, part]]

    def start_gather(i, part, dep):
        bufs = casts.pop((i, part))
        if dep is not None:
            bufs = list(lax.optimization_barrier((tuple(bufs), dep))[0])
        send, recv, arrs, tok = split_start("gather_start_%d_%s" % (i, part), gather_build, bufs)
        in_flight[i, part] = (send, recv, arrs)
        return tok

    forwarding = {}

    def arrive_gather(i, part, after):
        send, recv, arrs = in_flight.pop((i, part))
        arrs = split_wait("gather_wait_%d_%s" % (i, part), gather_build, send, recv, arrs, after)
        send, recv, arrs, tok = split_start("gather_pass_%d_%s" % (i, part), forward_build, arrs)
        forwarding[i, part] = (send, recv, arrs)
        return tok

    def finish_gather(i, part, after):
        send, recv, arrs = forwarding.pop((i, part))
        arrs = split_wait("gather_done_%d_%s" % (i, part), forward_build, send, recv, arrs, after)
        for (k, l), buf in zip(groups[i, part], arrs):
            G[k][l] = buf
        return buf

    def after_token(row_arr, *toks):
        return functools.reduce(lambda u, t: u + t[0:1, 0:1], toks, row_arr)

    cast_group(0, "mix", None)
    tok_mix = start_gather(0, "mix", None)
    cast_group(0, "rest", tok_mix)
    tok_rest = start_gather(0, "rest", None)
    finish_gather(0, "mix", arrive_gather(0, "mix", tok_mix + tok_rest))
    tie = lambda v, tok: lax.optimization_barrier((v, tok))[0]
    me = 4 * xi + 2 * yi + ci
    ln_full = share_with_all("share_ln", _pack([W["gm_ln_g"], W["gm_ln_b"]]), me)
    n_gm, dq = W["gm_ln_g"].shape
    ln_parts = [_unpack(ln_full[4 * qx + 2 * qy + 0], [(n_gm, dq), (n_gm, dq)]) for qx in range(2) for qy in range(2)]
    ln_g = jnp.concatenate([lp[0] for lp in ln_parts], axis=1)
    ln_b = jnp.concatenate([lp[1] for lp in ln_parts], axis=1)

    row = lambda a, i: a[i][None, :]
    f_lb = _make_f_lb(n_hg)
    lb_rows = [row(W["hg_lb_logits"], j) for j in range(n_hg)]
    one = (1, D)
    lb_ins = [(r, one, lambda i: (0, 0)) for r in lb_rows]
    lb_out = bmap_fwd("hg_lower_bounds", f_lb, (1,), lb_ins, [(one, F32, one, lambda i: (0, 0), None)] * (3 * n_hg))

    saved = []
    h = x2
    a = rows_fwd("prenorm", f_prenorm, [h], [row(W["norm_mix_pre"], 0)], [BF16])[0]
    for i in range(depth):
        j = i // 2
        sv = {"h": h, "a": a}
        if i > 0:
            finish_gather(i, "mix", h)
            a = tie(a, arrive_gather(i, "rest", h))
        if i % 2 == 0:
            proj4 = mm_fwd("hg_in", a, G["hg_w_in"], j, "col", parts=True)
            lbp = lb_out[3 * j:3 * j + 3]
            onorm = row(W["hg_out_norm"], j)
            og, states = hgrn_fwd("hgrn_fwd", proj4, *lbp, onorm, n_seq)
            if i == 0:
                og = tie(og, arrive_gather(0, "rest", og))
            m = mm_fwd("hg_out", og, G["hg_w_out"], j, "row")
            sv.update(proj4=proj4, states=states, og=og, lbp=lbp, onorm=onorm)
        else:
            z = mm_fwd("gm_in", a, G["gm_w_in"], j, "col")
            lg, lb_ = row(ln_g, j), row(ln_b, j)
            u, vn = rows_fwd("gm_gelu_ln", f_gm_in, [z], [lg, lb_], [F32, BF16], tm=128)
            ws = W["gm_w_s"][j]
            bs = W["gm_b_s"][j][:, :, None]
            gb = min(GM_BLOCK, S)
            sp_grid = (D // LANES, T // gb)
            sp_ins = [(u, (gb, LANES), lambda g, n: (n, g)), (vn, (gb, LANES), lambda g, n: (n, g)),
                      (ws, (None, GM_CHUNK, GM_CHUNK), lambda g, n: (g, 0, 0)),
                      (bs, (None, GM_CHUNK, 1), lambda g, n: (g, 0, 0))]
            y = bmap_fwd("gm_spatial", f_gm_spatial, sp_grid, sp_ins,
                         [((T, D), BF16, (gb, LANES), lambda g, n: (n, g), None)])[0]
            m = mm_fwd("gm_out", y, G["gm_w_out"], j, "row")
            sv.update(z=z, lg=lg, lb_=lb_, sp_ins=sp_ins, sp_grid=sp_grid, y=y)
        g_post, g_fpre = row(W["norm_mix_post"], i), row(W["norm_ffn_pre"], i)
        arrived = finish_gather(i, "rest", m)
        if i + 1 < depth:
            cast_group(i + 1, "mix", arrived)
            tok_mix = start_gather(i + 1, "mix", None)
            cast_group(i + 1, "rest", tok_mix)
            g_post = after_token(g_post, tok_mix, start_gather(i + 1, "rest", None))
        h1, fin = rows_fwd("mix_post_ffn_pre", f_post_pre, [h, m], [g_post, g_fpre], [F32, BF16])
        gate, up, act = ffn_gate_up("ffn_gate_up", fin, G["ffn_w_gate"], G["ffn_w_up"], i)
        f = mm_fwd("ffn_down", act, G["ffn_w_down"], i, "row")
        g_fpost = row(W["norm_ffn_post"], i)
        if i + 1 < depth:
            g_fpost = after_token(g_fpost, arrive_gather(i + 1, "mix", f))
        h2 = rows_fwd("ffn_post", f_post, [h1, f], [g_fpost], [F32])[0]
        e = mm_fwd("ple_proj", p3, G["ple_w_proj"], i, "col", xl=i)
        zg = mm_fwd("ple_gate", h2, G["ple_w_gate"], i, "row")
        g_ple = row(W["ple_norm"], i)
        sv.update(m=m, h1=h1, fin=fin, gate=gate, up=up, act=act, f=f, h2=h2, e=e, zg=zg,
                  g_post=g_post, g_fpre=g_fpre, g_fpost=g_fpost, g_ple=g_ple)
        if i + 1 < depth:
            g_next = row(W["norm_mix_pre"], i + 1)
            h, a = rows_fwd("ple_next_pre", f_ple_pre, [h2, e, zg], [g_ple, g_next], [F32, BF16])
            sv["g_next"] = g_next
        else:
            h = rows_fwd("ple_last", f_ple, [h2, e, zg], [g_ple], [F32])[0]
        saved.append(sv)

    loss_part, dh = loss_and_grad(h, tgt)
    loss = lax.psum(loss_part[0, 0], ("x", "y", "c"))

    sg = {k: [None] * W[k].shape[0] for k in ("norm_mix_pre", "norm_mix_post", "norm_ffn_pre", "norm_ffn_post", "ple_norm",
                                              "hg_out_norm", "gm_ln_g", "gm_ln_b", "gm_w_s", "gm_b_s")}
    d_lbp = [None] * (3 * n_hg)
    da_next = None
    GRAD = {k: lax.empty(W[k].shape, F32) for k in BIG}
    scattering = {}

    def start_scatter(i, part):
        dws = [DW[k][l] for k, l in groups[i, part]]
        lands = [lax.empty((N_DEV - 1, 1, g.shape[2] // 2, g.shape[3]), BF16) for g in dws]
        send, recv, arrs, tok = split_start("scatter_start_%d_%s" % (i, part), scatter_build, dws + lands, n_peers=N_DEV - 1)
        scattering[i, part] = (send, recv, arrs)
        return tok

    def finish_scatter(i, part, after):
        send, recv, arrs = scattering.pop((i, part))
        arrs = split_wait("scatter_wait_%d_%s" % (i, part), scatter_build, send, recv, arrs, after)
        n = len(groups[i, part])
        for (k, l), own, ld in zip(groups[i, part], arrs[:n], arrs[n:]):
            GRAD[k] = sum_partials(own, ld, GRAD[k], l, q_arr, c_arr)
        kinds = [k for k, _ in groups[i, part]]
        build = functools.partial(join_build, layers=tuple(l for _, l in groups[i, part]))
        send, recv, arrs, tok_j = split_start("join_start_%d_%s" % (i, part), build, [GRAD[k] for k in kinds], n_peers=1)
        GRAD.update(zip(kinds, arrs))
        joining.append(("join_wait_%d_%s" % (i, part), build, send, recv, kinds))
        return tok_j

    def finish_joins(after):
        while joining:
            name, build, send, recv, kinds = joining.pop(0)
            GRAD.update(zip(kinds, split_wait(name, build, send, recv, [GRAD[k] for k in kinds], after)))

    def start_share(name, keys, extra):
        grads = dict(extra)
        for k in keys:
            if k not in grads:
                grads[k] = jnp.stack([v.reshape((D,) if k in ("gm_ln_g", "gm_ln_b") else W[k].shape[1:]) for v in sg[k]])
        packed = _pack([grads[k] for k in keys])
        slots = lax.dynamic_update_slice(jnp.zeros((N_DEV,) + packed.shape, F32), packed[None], (me, 0, 0))
        s_send, s_recv, slots, tok_sh = split_start(name + "_start", share_build, [slots], n_peers=N_DEV - 1)
        return name, [grads[k].shape for k in keys], s_send, s_recv, slots, tok_sh

    tail = {}

    def share_rest(tok_before, da0, dh0):
        g0 = after_token(row(W["norm_mix_pre"], 0), tok_before)
        tail["grad_x"], d_g0 = rows_bwd("prenorm_bwd", f_prenorm_thru, [saved[0]["h"]], [g0], [da0, dh0], [F32])
        sg["norm_mix_pre"][0] = d_g0
        d_logits = bmap_bwd("hg_lower_bounds_bwd", f_lb, (1,), lb_ins, [(d, one, lambda i: (0, 0)) for d in d_lbp],
                            [(jj, F32, None) for jj in range(n_hg)])
        sharing["rest"] = start_share("share_small", SMALL_REST, {"hg_lb_logits": jnp.concatenate(d_logits, axis=0)})
        tail["tok_s"] = sharing["rest"][-1]

    joining = []
    sharing = {}
    last_gm = 1
    tok = None
    for i in reversed(range(depth)):
        j = i // 2
        sv = saved[i]
        if i + 1 < depth:
            g_ple_after = sv["g_ple"] + tok[0:1, 0:1]
            dh2, de, dzg, d_gple, d_gnext = rows_bwd("ple_next_pre_bwd", f_ple_pre, [sv["h2"], sv["e"], sv["zg"]],
                                                     [g_ple_after, sv["g_next"]], [dh, da_next], [F32, BF16, BF16])
            sg["norm_mix_pre"][i + 1] = d_gnext
        else:
            dh2, de, dzg, d_gple = rows_bwd("ple_last_bwd", f_ple, [sv["h2"], sv["e"], sv["zg"]], [sv["g_ple"]], [dh],
                                            [F32, BF16, BF16])
        sg["ple_norm"][i] = d_gple
        DW["ple_w_proj"] = mm_bwd_w("ple_proj_dw", p3, de, DW["ple_w_proj"], i, "col", xl=i)
        DW["ple_w_gate"] = mm_bwd_w("ple_gate_dw", sv["h2"], dzg, DW["ple_w_gate"], i, "row")
        dh2 = mm_bwd_x("ple_gate_dx", dzg, G["ple_w_gate"], i, "row", addend=dh2)
        dh1, df, d_gfpost = rows_bwd("ffn_post_bwd", f_post, [sv["h1"], sv["f"]], [sv["g_fpost"]], [dh2], [F32, BF16])
        sg["norm_ffn_post"][i] = d_gfpost
        dgate, dup = ffn_down_dx("ffn_down_dx", df, G["ffn_w_down"], i, sv["gate"], sv["up"])
        DW["ffn_w_down"] = mm_bwd_w("ffn_down_dw", sv["act"], df, DW["ffn_w_down"], i, "row")
        dfin = ffn_in_dx("ffn_in_dx", dgate, dup, G["ffn_w_gate"], G["ffn_w_up"], i)
        DW["ffn_w_gate"] = mm_bwd_w("ffn_gate_dw", sv["fin"], dgate, DW["ffn_w_gate"], i, "col")
        DW["ffn_w_up"] = mm_bwd_w("ffn_up_dw", sv["fin"], dup, DW["ffn_w_up"], i, "col")
        g_post_after = after_token(sv["g_post"], start_scatter(i, "rest"))
        dh, dm, d_gpost, d_gfpre = rows_bwd("mix_post_ffn_pre_bwd", f_post_pre, [sv["h"], sv["m"]],
                                            [g_post_after, sv["g_fpre"]], [dh1, dfin], [F32, BF16])
        sg["norm_mix_post"][i], sg["norm_ffn_pre"][i] = d_gpost, d_gfpre
        if i % 2 == 0:
            dog = mm_bwd_x("hg_out_dx", dm, G["hg_w_out"], j, "row")
            DW["hg_w_out"] = mm_bwd_w("hg_out_dw", sv["og"], dm, DW["hg_w_out"], j, "row")
            dproj4, d0, d1, d2, d_on = hgrn_bwd("hgrn_bwd", sv["proj4"], sv["states"], dog, *sv["lbp"], sv["onorm"], n_seq)
            d_lbp[3 * j:3 * j + 3] = [d0, d1, d2]
            sg["hg_out_norm"][j] = d_on
            da_next = mm_bwd_x("hg_in_dx", dproj4, G["hg_w_in"], j, "col", parts=True)
            a_in = sv["a"]
            if i == 0:
                share_rest(tok, da_next, dh)
                a_in = tie(a_in, tail["tok_s"])
            DW["hg_w_in"] = mm_bwd_w("hg_in_dw", a_in, dproj4, DW["hg_w_in"], j, "col", parts=True)
        else:
            dy = mm_bwd_x("gm_out_dx", dm, G["gm_w_out"], j, "row")
            DW["gm_w_out"] = mm_bwd_w("gm_out_dw", sv["y"], dm, DW["gm_w_out"], j, "row")
            gb = sv["sp_ins"][0][1][0]
            du, dvn, dws, dbs = bmap_bwd("gm_spatial_bwd", f_gm_spatial, sv["sp_grid"], sv["sp_ins"],
                                         [(dy, (gb, LANES), lambda g, n: (n, g))],
                                         [(0, F32, None), (1, F32, None), (2, F32, 1), (3, F32, 1)])
            sg["gm_w_s"][j], sg["gm_b_s"][j] = dws, dbs[:, :, 0]
            dz, d_lg, d_lb = rows_bwd("gm_gelu_ln_bwd", f_gm_in, [sv["z"]], [sv["lg"], sv["lb_"]], [du, dvn], [BF16], tm=128)
            sg["gm_ln_g"][j], sg["gm_ln_b"][j] = d_lg, d_lb
            da_next = mm_bwd_x("gm_in_dx", dz, G["gm_w_in"], j, "col")
            DW["gm_w_in"] = mm_bwd_w("gm_in_dw", sv["a"], dz, DW["gm_w_in"], j, "col")
        tok = start_scatter(i, "mix")
        if i + 1 < depth:
            tok = tok + finish_scatter(i + 1, "rest", da_next) + finish_scatter(i + 1, "mix", da_next)
        if i == last_gm:
            sharing["gm"] = start_share("share_gm", SMALL_GM, {})
            tok = tok + sharing["gm"][-1]
    tok = tok + finish_scatter(0, "rest", tok)
    if not tail:
        share_rest(tok, da_next, dh)
    grad_x, tok_s = tail["grad_x"], tail["tok_s"]

    out_g, out_d, out_m, out_v = {}, {}, {}, {}
    late = [k for k, _ in groups[0, "mix"]]
    early = [k for k in BIG if k not in late]
    finish_joins(tok_s)
    for k in early:
        out_d[k], out_m[k], out_v[k], out_g[k] = adamw("adamw_" + k, W[k], GRAD[k], M[k], V[k], with_grad=True)
    small_red = {}
    for part, keys in (("gm", SMALL_GM), ("rest", SMALL_REST)):
        name, shapes, s_send, s_recv, slots, _ = sharing[part]
        slots = split_wait(name + "_wait", share_build, s_send, s_recv, slots, out_v[early[-1]])[0]
        small_red.update(zip(keys, _unpack(sum_devices(slots), shapes)))
    for k in ("gm_ln_g", "gm_ln_b"):
        small_red[k] = lax.dynamic_slice_in_dim(small_red[k], q_me * dq, dq, axis=1)
    pk = lambda d: _pack([d[k] for k in SMALL])
    s_delta, s_m, s_v = adamw("adamw_small", pk(W), pk(small_red), pk(M), pk(V))
    shard_shapes = [W[k].shape for k in SMALL]
    out_g.update(small_red)
    for dct, packed in ((out_d, s_delta), (out_m, s_m), (out_v, s_v)):
        dct.update(zip(SMALL, _unpack(packed, shard_shapes)))

    finish_joins(finish_scatter(0, "mix", s_v))
    for k in late:
        out_d[k], out_m[k], out_v[k], out_g[k] = adamw("adamw_" + k, W[k], GRAD[k], M[k], V[k], with_grad=True)

    outs = [loss, grad_x.reshape(x.shape)]
    for dct in (out_g, out_d, out_m, out_v):
        outs += [dct[k] for k in WEIGHTS]
    return tuple(outs)


def kernel(x, p, hg_w_in, hg_lb_logits, hg_out_norm, hg_w_out, gm_w_in, gm_ln_g, gm_ln_b, gm_w_s, gm_b_s, gm_w_out, norm_mix_pre, norm_mix_post, norm_ffn_pre, norm_ffn_post, ffn_w_gate, ffn_w_up, ffn_w_down, ple_w_proj, ple_w_gate, ple_norm, loss_target, m_hg_w_in, m_hg_lb_logits, m_hg_out_norm, m_hg_w_out, m_gm_w_in, m_gm_ln_g, m_gm_ln_b, m_gm_w_s, m_gm_b_s, m_gm_w_out, m_norm_mix_pre, m_norm_mix_post, m_norm_ffn_pre, m_norm_ffn_post, m_ffn_w_gate, m_ffn_w_up, m_ffn_w_down, m_ple_w_proj, m_ple_w_gate, m_ple_norm, v_hg_w_in, v_hg_lb_logits, v_hg_out_norm, v_hg_w_out, v_gm_w_in, v_gm_ln_g, v_gm_ln_b, v_gm_w_s, v_gm_b_s, v_gm_w_out, v_norm_mix_pre, v_norm_mix_post, v_norm_ffn_pre, v_norm_ffn_post, v_ffn_w_gate, v_ffn_w_up, v_ffn_w_down, v_ple_w_proj, v_ple_w_gate, v_ple_norm):
    W = dict(zip(WEIGHTS, (hg_w_in, hg_lb_logits, hg_out_norm, hg_w_out, gm_w_in, gm_ln_g, gm_ln_b, gm_w_s, gm_b_s, gm_w_out,
                           norm_mix_pre, norm_mix_post, norm_ffn_pre, norm_ffn_post, ffn_w_gate, ffn_w_up, ffn_w_down,
                           ple_w_proj, ple_w_gate, ple_norm)))
    M = dict(zip(WEIGHTS, (m_hg_w_in, m_hg_lb_logits, m_hg_out_norm, m_hg_w_out, m_gm_w_in, m_gm_ln_g, m_gm_ln_b, m_gm_w_s,
                           m_gm_b_s, m_gm_w_out, m_norm_mix_pre, m_norm_mix_post, m_norm_ffn_pre, m_norm_ffn_post,
                           m_ffn_w_gate, m_ffn_w_up, m_ffn_w_down, m_ple_w_proj, m_ple_w_gate, m_ple_norm)))
    V = dict(zip(WEIGHTS, (v_hg_w_in, v_hg_lb_logits, v_hg_out_norm, v_hg_w_out, v_gm_w_in, v_gm_ln_g, v_gm_ln_b, v_gm_w_s,
                           v_gm_b_s, v_gm_w_out, v_norm_mix_pre, v_norm_mix_post, v_norm_ffn_pre, v_norm_ffn_post,
                           v_ffn_w_gate, v_ffn_w_up, v_ffn_w_down, v_ple_w_proj, v_ple_w_gate, v_ple_norm)))
    return _step(x, p, W, M, V, loss_target)
```

```python
import functools

import jax
import jax.numpy as jnp
from jax import lax
from jax.experimental import pallas as pl
from jax.experimental.pallas import tpu as pltpu

F32 = jnp.float32
BF16 = jnp.bfloat16
MESH_ID = pl.DeviceIdType.MESH

LANES = 128
N_CHIPS = 4
N_DEV = 8
VMEM_LIMIT = 56 * 1024 * 1024
HG_SUB = 256
HG_BLOCK = 512
HG_HEADS_PER = 2
GM_CHUNK = 128
GM_BLOCK = 512
PACK_ROWS = 512
LB_FLOOR = 1e-30
EPS = 1e-6
ADAM_LR, ADAM_B1, ADAM_B2, ADAM_EPS, ADAM_WD, ADAM_STEP = 0.001, 0.9, 0.999, 1e-08, 0.01, 10


def _tile(n, pref, mult=LANES):
    if n <= pref:
        return n
    t = (pref // mult) * mult
    while t >= mult:
        if n % t == 0:
            return t
        t -= mult
    return n


def _cp(n_axes):
    return pltpu.CompilerParams(dimension_semantics=("arbitrary",) * n_axes, vmem_limit_bytes=VMEM_LIMIT)


def _dense(block):
    return tuple(b for b in block if b is not None)


def _bmap(name, grid, ins, outs, compute, scalars=(), into=None):
    n_s, n_in = len(scalars), len(ins)
    n_extra = 0 if into is None else 1

    def body(*refs):
        in_refs = refs[n_s:n_s + n_in]
        out_refs = refs[n_s + n_in + n_extra:]
        vals = [r[...] for r in in_refs]
        res = compute(*vals)
        for r, o, spec in zip(out_refs, res, outs):
            keep = spec[4]
            if keep is None:
                r[...] = o.astype(r.dtype)
            else:
                first = functools.reduce(jnp.logical_and, [pl.program_id(a) == 0 for a in range(keep, len(grid))])

                @pl.when(first)
                def _():
                    r[...] = jnp.zeros(r.shape, r.dtype)

                r[...] += o.astype(r.dtype)

    grid_spec = pltpu.PrefetchScalarGridSpec(
        num_scalar_prefetch=n_s, grid=grid,
        in_specs=[pl.BlockSpec(b, m) for _, b, m in ins] + [pl.BlockSpec(memory_space=pl.ANY)] * n_extra,
        out_specs=[pl.BlockSpec(o[2], o[3]) for o in outs])
    return pl.pallas_call(
        body, name=name, grid_spec=grid_spec,
        out_shape=[jax.ShapeDtypeStruct(o[0], o[1]) for o in outs],
        input_output_aliases={n_s + n_in: 0} if n_extra else {},
        compiler_params=_cp(len(grid)),
    )(*scalars, *[a for a, _, _ in ins], *([into] if n_extra else []))


def bmap_fwd(name, fn, grid, ins, outs, scalars=(), into=None):
    return _bmap(name, grid, ins, outs, lambda *v: fn(*[x.astype(F32) for x in v]), scalars, into)


def bmap_bwd(name, fn, grid, ins, cots, grads, scalars=()):
    n_in = len(ins)
    diff = [g[0] for g in grads]
    cot_ins = [c for c in cots if c is not None]

    def compute(*vals):
        xs = [v.astype(F32) for v in vals[:n_in]]
        cvals = list(vals[n_in:])

        def f(*d):
            full = list(xs)
            for i, dv in zip(diff, d):
                full[i] = dv
            return tuple(fn(*full))

        res, pull = jax.vjp(f, *[xs[i] for i in diff])
        cts = []
        for r, c in zip(res, cots):
            cts.append(jnp.zeros_like(r) if c is None else cvals.pop(0).astype(F32))
        return pull(tuple(cts))

    outs = [(ins[i][0].shape, dt, ins[i][1], ins[i][2], keep) for i, dt, keep in grads]
    return _bmap(name, grid, list(ins) + cot_ins, outs, compute, scalars)


def _mm(name, a, b, out_shape, out_dtype, grid, a_spec, b_spec, o_spec, dims, addend=None, alias_out=None):
    nk = grid[2]
    o_dense = _dense(o_spec[0])
    o_dense = (o_dense[0] * o_dense[1], o_dense[2]) if len(o_dense) == 3 else o_dense
    has_add = addend is not None
    has_alias = alias_out is not None

    def body(*refs):
        a_ref, b_ref = refs[0], refs[1]
        pos = 2
        c_ref = None
        if has_add:
            c_ref = refs[pos]
            pos += 1
        if has_alias:
            pos += 1
        o_ref = refs[pos]
        acc_ref = refs[pos + 1] if nk > 1 else None
        bv = b_ref[...]
        if bv.ndim == 3:
            bv = bv.reshape(bv.shape[0] * bv.shape[1], bv.shape[2])
        p = lax.dot_general(a_ref[...].astype(BF16), bv.astype(BF16), (dims, ((), ())), preferred_element_type=F32)

        def finish(total):
            if has_add:
                total = total + c_ref[...].astype(F32)
            o_ref[...] = total.reshape(o_ref.shape).astype(o_ref.dtype)

        if nk == 1:
            finish(p)
        else:
            k = pl.program_id(2)

            @pl.when(k == 0)
            def _():
                acc_ref[...] = p

            @pl.when(jnp.logical_and(k > 0, k < nk - 1))
            def _():
                acc_ref[...] += p

            @pl.when(k == nk - 1)
            def _():
                finish(acc_ref[...] + p)

    in_specs = [pl.BlockSpec(*a_spec), pl.BlockSpec(*b_spec)]
    operands = [a, b]
    if has_add:
        in_specs.append(pl.BlockSpec(o_spec[0], o_spec[1]))
        operands.append(addend)
    aliases = {}
    if has_alias:
        in_specs.append(pl.BlockSpec(memory_space=pl.ANY))
        aliases = {len(operands): 0}
        operands.append(alias_out)
    return pl.pallas_call(
        body, name=name, grid=grid, in_specs=in_specs, out_specs=pl.BlockSpec(*o_spec),
        out_shape=jax.ShapeDtypeStruct(out_shape, out_dtype),
        scratch_shapes=[pltpu.VMEM(o_dense, F32)] if nk > 1 else [],
        input_output_aliases=aliases,
        compiler_params=pltpu.CompilerParams(dimension_semantics=("parallel", "parallel", "arbitrary"),
                                             vmem_limit_bytes=VMEM_LIMIT),
    )(*operands)


NN, NT, TN = ((1,), (0,)), ((1,), (1,)), ((0,), (0,))
TM = 512
TT = 1024
TN_PREF = 1408
WHOLE_K = 2048


def mm_fwd(name, x, wg, l, kind, out_dtype=F32, parts=False, xl=None):
    if isinstance(wg, dict):
        wg, l = wg[l], 0
    _, _, R, C = wg.shape
    T = x.shape[-2]
    tm = _tile(T, TM, 8)
    if kind == "col":
        tn = _tile(C, TN_PREF)
        npc = C // tn
        grid = (T // tm, N_CHIPS * npc, 1)
        a_blk = (tm, R) if xl is None else (None, tm, R)
        a_map = (lambda i, j, k: (i, 0)) if xl is None else (lambda i, j, k: (xl, i, 0))
        b_spec = ((None, None, R, tn), lambda i, j, k: (j // npc, l, 0, j % npc))
        if parts:
            out_shape = (N_CHIPS, T, C)
            o_spec = ((None, tm, tn), lambda i, j, k: (j // npc, i, j % npc))
        else:
            out_shape = (T, N_CHIPS * C)
            o_spec = ((tm, tn), lambda i, j, k: (i, j))
    elif N_CHIPS * R <= WHOLE_K:
        tn = _tile(C, 1024)
        grid = (T // tm, C // tn, 1)
        a_blk = (tm, N_CHIPS * R)
        a_map = lambda i, j, k: (i, 0)
        b_spec = ((N_CHIPS, None, R, tn), lambda i, j, k: (0, l, 0, j))
        out_shape = (T, C)
        o_spec = ((tm, tn), lambda i, j, k: (i, j))
    else:
        tn = _tile(C, 2048)
        grid = (T // tm, C // tn, N_CHIPS)
        a_blk = (tm, R)
        a_map = lambda i, j, k: (i, k)
        b_spec = ((None, None, R, tn), lambda i, j, k: (k, l, 0, j))
        out_shape = (T, C)
        o_spec = ((tm, tn), lambda i, j, k: (i, j))
    return _mm(name, x, wg, out_shape, out_dtype, grid, (a_blk, a_map), b_spec, o_spec, NN)


def mm_bwd_x(name, dy, wg, l, kind, out_dtype=F32, parts=False, addend=None):
    if isinstance(wg, dict):
        wg, l = wg[l], 0
    _, _, R, C = wg.shape
    T = dy.shape[-2]
    tm = _tile(T, TM, 8)
    if kind == "col":
        tk = _tile(C, TN_PREF)
        npc = C // tk
        tno = _tile(R, 2048)
        grid = (T // tm, R // tno, N_CHIPS * npc)
        if parts:
            a_spec = ((None, tm, tk), lambda i, j, k: (k // npc, i, k % npc))
        else:
            a_spec = ((tm, tk), lambda i, j, k: (i, k))
        b_spec = ((None, None, tno, tk), lambda i, j, k: (k // npc, l, j, k % npc))
        out_shape = (T, R)
        o_spec = ((tm, tno), lambda i, j, k: (i, j))
    elif N_CHIPS * R <= WHOLE_K:
        grid = (T // tm, 1, 1)
        a_spec = ((tm, C), lambda i, j, k: (i, 0))
        b_spec = ((N_CHIPS, None, R, C), lambda i, j, k: (0, l, 0, 0))
        out_shape = (T, N_CHIPS * R)
        o_spec = ((tm, N_CHIPS * R), lambda i, j, k: (i, 0))
    else:
        grid = (T // tm, N_CHIPS, 1)
        a_spec = ((tm, C), lambda i, j, k: (i, 0))
        b_spec = ((None, None, R, C), lambda i, j, k: (j, l, 0, 0))
        out_shape = (T, N_CHIPS * R)
        o_spec = ((tm, R), lambda i, j, k: (i, j))
    return _mm(name, dy, wg, out_shape, out_dtype, grid, a_spec, b_spec, o_spec, NT, addend=addend)


def mm_bwd_w(name, x, dy, dwg, l, kind, parts=False, xl=None):
    if isinstance(dwg, dict):
        return {**dwg, l: mm_bwd_w(name, x, dy, dwg[l], 0, kind, parts=parts, xl=xl)}
    _, _, R, C = dwg.shape
    T = dy.shape[-2]
    tt = _tile(T, TT, 16)
    nt = T // tt
    if kind == "col":
        tn = _tile(C, TN_PREF)
        npc = C // tn
        tr = _tile(R, 1024)
        grid = (R // tr, N_CHIPS * npc, nt)
        if xl is None:
            a_spec = ((tt, tr), lambda i, j, t: (t, i))
        else:
            a_spec = ((None, tt, tr), lambda i, j, t: (xl, t, i))
        if parts:
            b_spec = ((None, tt, tn), lambda i, j, t: (j // npc, t, j % npc))
        else:
            b_spec = ((tt, tn), lambda i, j, t: (t, j))
        o_spec = ((None, None, tr, tn), lambda i, j, t: (j // npc, l, i, j % npc))
    elif N_CHIPS * R <= WHOLE_K:
        tn = _tile(C, 1024)
        grid = (1, C // tn, nt)
        a_spec = ((tt, N_CHIPS * R), lambda i, j, t: (t, 0))
        b_spec = ((tt, tn), lambda i, j, t: (t, j))
        o_spec = ((N_CHIPS, None, R, tn), lambda i, j, t: (0, l, 0, j))
    else:
        tn = _tile(C, 1024)
        grid = (N_CHIPS, C // tn, nt)
        a_spec = ((tt, R), lambda i, j, t: (t, i))
        b_spec = ((tt, tn), lambda i, j, t: (t, j))
        o_spec = ((None, None, R, tn), lambda i, j, t: (i, l, 0, j))
    return _mm(name, x, dy, dwg.shape, dwg.dtype, grid, a_spec, b_spec, o_spec, TN, alias_out=dwg)


def _sigmoid(x):
    return 0.5 * jnp.tanh(0.5 * x) + 0.5


def f_swiglu(gate, up):
    return (gate * _sigmoid(gate) * up,)


def ffn_gate_up(name, x, wg_gate, wg_up, l):
    if isinstance(wg_gate, dict):
        wg_gate, wg_up, l = wg_gate[l], wg_up[l], 0
    _, _, R, C = wg_gate.shape
    T = x.shape[0]
    tm = _tile(T, TM, 8)
    tn = _tile(C, TN_PREF)
    npc = C // tn

    def body(x_ref, g_ref, u_ref, gate_ref, up_ref, act_ref):
        xv = x_ref[...].astype(BF16)
        gate = jnp.dot(xv, g_ref[...], preferred_element_type=F32)
        up = jnp.dot(xv, u_ref[...], preferred_element_type=F32)
        gate_ref[...] = gate
        up_ref[...] = up
        act_ref[...] = f_swiglu(gate, up)[0].astype(act_ref.dtype)

    w_spec = pl.BlockSpec((None, None, R, tn), lambda i, j: (j // npc, l, 0, j % npc))
    o_spec = pl.BlockSpec((tm, tn), lambda i, j: (i, j))
    N = N_CHIPS * C
    return pl.pallas_call(
        body, name=name, grid=(T // tm, N_CHIPS * npc),
        in_specs=[pl.BlockSpec((tm, R), lambda i, j: (i, 0)), w_spec, w_spec], out_specs=[o_spec, o_spec, o_spec],
        out_shape=[jax.ShapeDtypeStruct((T, N), F32), jax.ShapeDtypeStruct((T, N), F32), jax.ShapeDtypeStruct((T, N), BF16)],
        compiler_params=_cp(2),
    )(x, wg_gate, wg_up)


def ffn_in_dx(name, dgate, dup, wg_gate, wg_up, l):
    if isinstance(wg_gate, dict):
        wg_gate, wg_up, l = wg_gate[l], wg_up[l], 0
    _, _, R, C = wg_gate.shape
    T = dgate.shape[0]
    tm = _tile(T, TM, 8)
    tk = _tile(C, TN_PREF)
    npc = C // tk
    nk = N_CHIPS * npc

    def body(dg_ref, du_ref, wg_ref, wu_ref, o_ref, acc):
        k = pl.program_id(1)

        @pl.when(k == 0)
        def _():
            acc[...] = jnp.zeros(acc.shape, F32)

        @pl.when(k < nk)
        def _():
            acc[...] += lax.dot_general(dg_ref[...], wg_ref[...], (NT, ((), ())), preferred_element_type=F32)

        @pl.when(k >= nk)
        def _():
            acc[...] += lax.dot_general(du_ref[...], wu_ref[...], (NT, ((), ())), preferred_element_type=F32)

        @pl.when(k == 2 * nk - 1)
        def _():
            o_ref[...] = acc[...]

    first = lambda k: jnp.minimum(k, nk - 1)
    second = lambda k: jnp.maximum(k - nk, 0)
    return pl.pallas_call(
        body, name=name, grid=(T // tm, 2 * nk),
        in_specs=[pl.BlockSpec((tm, tk), lambda i, k: (i, first(k))), pl.BlockSpec((tm, tk), lambda i, k: (i, second(k))),
                  pl.BlockSpec((None, None, R, tk), lambda i, k: (first(k) // npc, l, 0, first(k) % npc)),
                  pl.BlockSpec((None, None, R, tk), lambda i, k: (second(k) // npc, l, 0, second(k) % npc))],
        out_specs=pl.BlockSpec((tm, R), lambda i, k: (i, 0)),
        out_shape=jax.ShapeDtypeStruct((T, R), F32),
        scratch_shapes=[pltpu.VMEM((tm, R), F32)],
        compiler_params=_cp(2),
    )(dgate, dup, wg_gate, wg_up)


def ffn_down_dx(name, df, wg_down, l, gate, up):
    if isinstance(wg_down, dict):
        wg_down, l = wg_down[l], 0
    _, _, R, C = wg_down.shape
    T = df.shape[0]
    tm = _tile(T, TM, 8)

    def body(df_ref, w_ref, gate_ref, up_ref, dg_ref, du_ref):
        dact = lax.dot_general(df_ref[...].astype(BF16), w_ref[...], (NT, ((), ())), preferred_element_type=F32)
        _, pull = jax.vjp(lambda g, u: f_swiglu(g, u)[0], gate_ref[...], up_ref[...])
        dg, du = pull(dact)
        dg_ref[...] = dg.astype(dg_ref.dtype)
        du_ref[...] = du.astype(du_ref.dtype)

    t_spec = pl.BlockSpec((tm, R), lambda i, j: (i, j))
    return pl.pallas_call(
        body, name=name, grid=(T // tm, N_CHIPS),
        in_specs=[pl.BlockSpec((tm, C), lambda i, j: (i, 0)), pl.BlockSpec((None, None, R, C), lambda i, j: (j, l, 0, 0)),
                  t_spec, t_spec],
        out_specs=[t_spec, t_spec],
        out_shape=[jax.ShapeDtypeStruct((T, N_CHIPS * R), BF16)] * 2,
        compiler_params=_cp(2),
    )(df, wg_down, gate, up)


def _rms(x, g):
    return x * lax.rsqrt(jnp.mean(x * x, axis=-1, keepdims=True) + EPS) * g


def f_prenorm(h, g):
    return (_rms(h, g),)


def f_prenorm_thru(h, g):
    return _rms(h, g), h


def f_post_pre(h, m, g_post, g_pre):
    h1 = h + _rms(m, g_post)
    return h1, _rms(h1, g_pre)


def f_post(h1, f, g):
    return (h1 + _rms(f, g),)


def f_ple(h2, e, zg, g):
    return (h2 + _rms(e * _sigmoid(zg), g),)


def f_ple_pre(h2, e, zg, g, g_next):
    h3 = h2 + _rms(e * _sigmoid(zg), g)
    return h3, _rms(h3, g_next)


def _gelu(x):
    return 0.5 * x * (1.0 + lax.erf(x * 0.7071067811865476))


def f_gm_in(z, ln_g, ln_b):
    w = z.shape[-1] // 2
    u = _gelu(z[:, :w])
    v = _gelu(z[:, w:])
    mu = jnp.mean(v, axis=-1, keepdims=True)
    vc = v - mu
    vn = vc * lax.rsqrt(jnp.mean(vc * vc, axis=-1, keepdims=True) + EPS) * ln_g + ln_b
    return u, vn


def f_gm_spatial(u, vn, ws, bs):
    t = lax.broadcasted_iota(jnp.int32, ws.shape, 0)
    s = lax.broadcasted_iota(jnp.int32, ws.shape, 1)
    wm = jnp.where(t >= s, ws, 0.0).astype(BF16)
    ys = []
    for n in range(u.shape[0] // GM_CHUNK):
        rows = slice(n * GM_CHUNK, (n + 1) * GM_CHUNK)
        sv = jnp.dot(wm, vn[rows].astype(BF16), preferred_element_type=F32) + bs
        ys.append(u[rows] * sv)
    return (jnp.concatenate(ys, axis=0) if len(ys) > 1 else ys[0],)


def f_adam(w, g, m, v):
    m = ADAM_B1 * m + (1.0 - ADAM_B1) * g
    v = ADAM_B2 * v + (1.0 - ADAM_B2) * jnp.square(g)
    m_hat = m / (1.0 - ADAM_B1 ** ADAM_STEP)
    v_hat = v / (1.0 - ADAM_B2 ** ADAM_STEP)
    delta = -ADAM_LR * (m_hat / (jnp.sqrt(v_hat) + ADAM_EPS) + ADAM_WD * w)
    return delta, m, v


def _make_f_lb(n_layers):
    def f_lb(*logits):
        mx = functools.reduce(jnp.maximum, logits)
        ex = [jnp.exp(r - mx) for r in logits]
        tot = functools.reduce(lambda a, b: a + b, ex)
        sm = [e / tot for e in ex]
        outs = []
        run = jnp.zeros_like(sm[0])
        for j in range(n_layers):
            if j > 0:
                run = run + sm[j]
            lb = run
            outs += [jnp.log(jnp.maximum(lb, LB_FLOOR)), jnp.log(1.0 - lb), 1.0 - lb]
        return tuple(outs)
    return f_lb


def rows_fwd(name, fn, rows, params, out_dtypes, tm=256):
    T = rows[0].shape[0]
    tm = _tile(T, tm, 16)
    ins = [(r, (tm, r.shape[1]), lambda i: (i, 0)) for r in rows]
    ins += [(p, p.shape, lambda i: (0, 0)) for p in params]
    shapes = jax.eval_shape(lambda *a: fn(*a), *[jax.ShapeDtypeStruct((tm, r.shape[1]), F32) for r in rows],
                            *[jax.ShapeDtypeStruct(p.shape, F32) for p in params])
    outs = [((T, s.shape[1]), dt, (tm, s.shape[1]), lambda i: (i, 0), None) for s, dt in zip(shapes, out_dtypes)]
    return bmap_fwd(name, fn, (T // tm,), ins, outs)


def rows_bwd(name, fn, rows, params, cots, row_grad_dtypes, tm=256):
    T = rows[0].shape[0]
    tm = _tile(T, tm, 16)
    ins = [(r, (tm, r.shape[1]), lambda i: (i, 0)) for r in rows]
    ins += [(p, p.shape, lambda i: (0, 0)) for p in params]
    cts = [None if c is None else (c, (tm, c.shape[1]), lambda i: (i, 0)) for c in cots]
    grads = [(i, dt, None) for i, dt in enumerate(row_grad_dtypes) if dt is not None]
    grads += [(len(rows) + j, F32, 0) for j in range(len(params))]
    return bmap_bwd(name, fn, (T // tm,), ins, cts, grads)


def _log_sigmoid(z):
    return jnp.minimum(z, 0.0) - jnp.log(1.0 + jnp.exp(-jnp.abs(z)))


def _hg_gates(zf, ll0, ll1, oml):
    x2 = ll1 + _log_sigmoid(zf)
    mx = jnp.maximum(ll0, x2)
    g = mx + jnp.log(jnp.exp(ll0 - mx) + jnp.exp(x2 - mx))
    return g, oml * _sigmoid(-zf)


def hg_constants(n):
    levels = n.bit_length() - 1
    r = jnp.arange(n, dtype=jnp.int32)
    bounds = [r] + [((r >> (s + 1)) << (s + 1)) + ((1 << s) - 1) for s in range(levels)]
    sel = jnp.concatenate([(r[None, :] <= bd[:, None]) for bd in bounds], axis=0).astype(BF16)
    later = jnp.stack([((r >> s) & 1) for s in range(levels)])
    sign = jnp.broadcast_to((2 * later - 1).astype(F32)[:, :, None], (levels, n, LANES))
    pair = jnp.stack([((r[:, None] >> (s + 1)) == (r[None, :] >> (s + 1))) & (later[s][:, None] == 1) & (later[s][None, :] == 0)
                      for s in range(levels)]).astype(F32)
    return sel, sel.T, sign, pair


def _dot2(m, x):
    hi = x.astype(BF16)
    lo = (x - hi.astype(F32)).astype(BF16)
    p = jnp.dot(m, jnp.concatenate([hi, lo], axis=1), preferred_element_type=F32)
    w = x.shape[1]
    return p[:, :w] + p[:, w:]


@jax.custom_vjp
def _sel_dot(sel, selt, g):
    return _dot2(sel, g)


def _sel_dot_fwd(sel, selt, g):
    return _dot2(sel, g), (sel, selt)


def _sel_dot_bwd(res, d):
    sel, selt = res
    return jnp.zeros_like(sel), jnp.zeros_like(selt), _dot2(selt, d)


_sel_dot.defvjp(_sel_dot_fwd, _sel_dot_bwd)


def _hg_state(st, zf, zi, ll0, ll1, oml, tri):
    g, k = _hg_gates(zf, ll0, ll1, oml)
    b = _dot2(tri, g)
    tot = jnp.sum(g, axis=0, keepdims=True)
    kd = k * jnp.exp(tot - b)
    return st * jnp.exp(tot) + jnp.dot(zi.T.astype(BF16), kd.astype(BF16), preferred_element_type=F32)


def _hg_step(st, zq, zf, zi, zg, ll0, ll1, oml, onorm, sel, selt, sign, pair):
    n = zq.shape[0]
    levels = n.bit_length() - 1
    q = zq * _sigmoid(zq)
    g, k = _hg_gates(zf, ll0, ll1, oml)
    sums = _sel_dot(sel, selt, g)
    b = sums[:n]
    tot = jnp.sum(g, axis=0, keepdims=True)
    o = lax.dot_general((q * jnp.exp(b)).astype(BF16), st.astype(BF16), (NT, ((), ())), preferred_element_type=F32)
    a = jnp.zeros((n, n), F32)
    for s in range(levels):
        e = jnp.exp(sign[s] * (b - sums[(s + 1) * n:(s + 2) * n]))
        al = lax.dot_general((q * e).astype(BF16), (k * e).astype(BF16), (NT, ((), ())), preferred_element_type=F32)
        a = a + pair[s] * al
    o = o + jnp.dot(a.astype(BF16), zi.astype(BF16), preferred_element_type=F32)
    o = o + jnp.sum(q * k, axis=1, keepdims=True) * zi
    kd = k * jnp.exp(tot - b)
    st_new = st * jnp.exp(tot) + jnp.dot(zi.T.astype(BF16), kd.astype(BF16), preferred_element_type=F32)
    og = _rms(o, onorm) * (zg * _sigmoid(zg))
    return og, st_new


def _whole(arr, n_grid):
    zeros = (0,) * arr.ndim
    return pl.BlockSpec(arr.shape, (lambda h, n: zeros) if n_grid == 2 else (lambda i: zeros))


def _hg_dims(proj4, n_seq):
    _, T, D = proj4.shape
    S = T // n_seq
    hp = HG_HEADS_PER if (D // LANES) % HG_HEADS_PER == 0 else 1
    tb = min(HG_BLOCK, S)
    streams = [(b, hl) for b in range(n_seq) for hl in range(hp)]
    return T, D, S, hp, D // (LANES * hp), LANES * hp, tb, S // tb, tb // HG_SUB, streams


def hgrn_fwd(name, proj4, ll0, ll1, oml, onorm, n_seq):
    T, D, S, hp, n_hg, W, tb, nblk, nsub, streams = _hg_dims(proj4, n_seq)
    ns = len(streams)

    def body(p_ref, ll0_ref, ll1_ref, oml_ref, on_ref, sel_ref, selt_ref, later_ref, pair_ref, og_ref, st_ref, st):
        @pl.when(pl.program_id(1) == 0)
        def _():
            st[...] = jnp.zeros(st.shape, F32)

        st_ref[...] = st[...]
        on = on_ref[...]

        def step(j, carry):
            r = pl.ds(pl.multiple_of(j * HG_SUB, HG_SUB), HG_SUB)
            consts = (sel_ref[...], selt_ref[...], later_ref[...], pair_ref[...])
            args = []
            for si, (b, hl) in enumerate(streams):
                ln = slice(hl * LANES, (hl + 1) * LANES)
                args.append((st[si], p_ref[0, b, r, ln], p_ref[1, b, r, ln], p_ref[2, b, r, ln], p_ref[3, b, r, ln],
                             ll0_ref[:, ln], ll1_ref[:, ln], oml_ref[:, ln], on) + consts)
            res = [_hg_step(*a) for a in args]
            for si, (b, hl) in enumerate(streams):
                og_ref[b, r, hl * LANES:(hl + 1) * LANES] = res[si][0].astype(og_ref.dtype)
                st[si] = res[si][1]
            return carry

        lax.fori_loop(0, nsub, step, 0)

    vec = pl.BlockSpec((1, W), lambda h, n: (0, h))
    consts = hg_constants(HG_SUB)
    og, states = pl.pallas_call(
        body, name=name, grid=(n_hg, nblk),
        in_specs=[pl.BlockSpec((4, n_seq, tb, W), lambda h, n: (0, 0, n, h)), vec, vec, vec,
                  pl.BlockSpec((1, LANES), lambda h, n: (0, 0))] + [_whole(c, 2) for c in consts],
        out_specs=[pl.BlockSpec((n_seq, tb, W), lambda h, n: (0, n, h)),
                   pl.BlockSpec((None, None, ns, LANES, LANES), lambda h, n: (h, n, 0, 0, 0))],
        out_shape=[jax.ShapeDtypeStruct((n_seq, S, D), BF16),
                   jax.ShapeDtypeStruct((n_hg, nblk, ns, LANES, LANES), F32)],
        scratch_shapes=[pltpu.VMEM((ns, LANES, LANES), F32)],
        compiler_params=_cp(2),
    )(proj4.reshape(4, n_seq, S, D), ll0, ll1, oml, onorm, *consts)
    return og.reshape(T, D), states


def hgrn_bwd(name, proj4, states, dog, ll0, ll1, oml, onorm, n_seq):
    T, D, S, hp, n_hg, W, tb, nblk, nsub, streams = _hg_dims(proj4, n_seq)
    ns = len(streams)

    def body(p_ref, st_ref, dog_ref, ll0_ref, ll1_ref, oml_ref, on_ref, sel_ref, selt_ref, later_ref, pair_ref,
             dp_ref, dll0_ref, dll1_ref, doml_ref, don_ref, sbuf, dst):
        n_id = pl.program_id(1)

        @pl.when(n_id == 0)
        def _():
            dst[...] = jnp.zeros(dst.shape, F32)
            for ref in (dll0_ref, dll1_ref, doml_ref):
                ref[...] = jnp.zeros(ref.shape, F32)

        @pl.when(jnp.logical_and(n_id == 0, pl.program_id(0) == 0))
        def _():
            don_ref[...] = jnp.zeros(don_ref.shape, F32)

        on = on_ref[...]

        def fwd(j, carry):
            r = pl.ds(pl.multiple_of(j * HG_SUB, HG_SUB), HG_SUB)
            tri = sel_ref[0:HG_SUB, :]
            args = []
            for si, (b, hl) in enumerate(streams):
                ln = slice(hl * LANES, (hl + 1) * LANES)
                args.append((carry[si], p_ref[1, b, r, ln], p_ref[2, b, r, ln],
                             ll0_ref[:, ln], ll1_ref[:, ln], oml_ref[:, ln], tri))
            for si in range(ns):
                sbuf[si, j] = carry[si]
            return tuple(_hg_state(*a) for a in args)

        lax.fori_loop(0, nsub, fwd, tuple(st_ref[si] for si in range(ns)))

        def bwd(jj, carry):
            j = nsub - 1 - jj
            r = pl.ds(pl.multiple_of(j * HG_SUB, HG_SUB), HG_SUB)
            args, cts = [], []
            for si, (b, hl) in enumerate(streams):
                ln = slice(hl * LANES, (hl + 1) * LANES)
                args.append((sbuf[si, j], p_ref[0, b, r, ln], p_ref[1, b, r, ln], p_ref[2, b, r, ln],
                             p_ref[3, b, r, ln], ll0_ref[:, ln], ll1_ref[:, ln], oml_ref[:, ln], on))
                cts.append((dog_ref[b, r, ln].astype(F32), dst[si]))
            consts = (sel_ref[...], selt_ref[...], later_ref[...], pair_ref[...])
            step_fn = lambda *a: _hg_step(*a, *consts)
            ds = [jax.vjp(step_fn, *a)[1](ct) for a, ct in zip(args, cts)]
            d_on = carry
            for si, (b, hl) in enumerate(streams):
                ln = slice(hl * LANES, (hl + 1) * LANES)
                d = ds[si]
                dst[si] = d[0]
                for part in range(4):
                    dp_ref[part, b, r, ln] = d[1 + part].astype(dp_ref.dtype)
                dll0_ref[:, ln] += d[5]
                dll1_ref[:, ln] += d[6]
                doml_ref[:, ln] += d[7]
                d_on = d_on + d[8]
            return d_on

        don_ref[...] += lax.fori_loop(0, nsub, bwd, jnp.zeros((1, LANES), F32))

    last = nblk - 1
    vec = pl.BlockSpec((1, W), lambda h, n: (0, h))
    one = pl.BlockSpec((1, LANES), lambda h, n: (0, 0))
    consts = hg_constants(HG_SUB)
    dproj, d0, d1, d2, d_on = pl.pallas_call(
        body, name=name, grid=(n_hg, nblk),
        in_specs=[pl.BlockSpec((4, n_seq, tb, W), lambda h, n: (0, 0, last - n, h)),
                  pl.BlockSpec((None, None, ns, LANES, LANES), lambda h, n: (h, last - n, 0, 0, 0)),
                  pl.BlockSpec((n_seq, tb, W), lambda h, n: (0, last - n, h)), vec, vec, vec, one]
        + [_whole(c, 2) for c in consts],
        out_specs=[pl.BlockSpec((4, n_seq, tb, W), lambda h, n: (0, 0, last - n, h)), vec, vec, vec, one],
        out_shape=[jax.ShapeDtypeStruct((4, n_seq, S, D), BF16)] + [jax.ShapeDtypeStruct((1, D), F32)] * 3
        + [jax.ShapeDtypeStruct((1, LANES), F32)],
        scratch_shapes=[pltpu.VMEM((ns, nsub, LANES, LANES), F32), pltpu.VMEM((ns, LANES, LANES), F32)],
        compiler_params=_cp(2),
    )(proj4.reshape(4, n_seq, S, D), states, dog.reshape(n_seq, S, D), ll0, ll1, oml, onorm, *consts)
    return dproj.reshape(4, T, D), d0, d1, d2, d_on


def _place():
    x, y, c = lax.axis_index("x"), lax.axis_index("y"), lax.axis_index("c")
    chips = [(1 - x, y), (x, 1 - y), (1 - x, 1 - y)]
    return x, y, c, chips


ANY = pl.BlockSpec(memory_space=pl.ANY)


def _comm_call(name, body, ins, out_shapes, sems, aliases=None):
    return pl.pallas_call(
        body, name=name, in_specs=[ANY] * len(ins), out_specs=[ANY] * len(out_shapes),
        out_shape=out_shapes, scratch_shapes=sems, input_output_aliases=aliases or {},
        compiler_params=pltpu.CompilerParams(has_side_effects=True),
    )(*ins)


HBM_SPEC = pl.BlockSpec(memory_space=pltpu.HBM)
SEM_SPEC = pl.BlockSpec(memory_space=pltpu.SEMAPHORE)
SPLIT_EFFECT = pltpu.SideEffectType.DATAFLOW_SIDE_EFFECTING
N_PEER_CHIPS = 3


def split_start(name, build, arrays, n_peers=N_PEER_CHIPS):
    n = len(arrays)

    def body(*refs):
        send, recv = refs[n], refs[n + 1]
        token = refs[2 * n + 2]
        starts, _ = build(refs[:n], send, recv)
        for cp in starts:
            cp.start()
        token[...] = jnp.zeros_like(token)

    res = pl.pallas_call(
        body, name=name,
        out_shape=(pltpu.SemaphoreType.DMA((n_peers,)), pltpu.SemaphoreType.DMA((n_peers,)),
                   *[pltpu.HBM(a.shape, a.dtype) for a in arrays], jax.ShapeDtypeStruct((8, LANES), F32)),
        in_specs=[HBM_SPEC] * n,
        out_specs=(SEM_SPEC, SEM_SPEC, *[HBM_SPEC] * n, pl.BlockSpec(memory_space=pltpu.VMEM)),
        input_output_aliases={i: 2 + i for i in range(n)},
        compiler_params=pltpu.CompilerParams(has_side_effects=SPLIT_EFFECT),
    )(*[pltpu.with_memory_space_constraint(a, pltpu.HBM) for a in arrays])
    return res[0], res[1], list(res[2:2 + n]), res[2 + n]


def split_wait(name, build, send, recv, arrays, after):
    n = len(arrays)

    def body(*refs):
        starts, arrivals = build(refs[:n], refs[n], refs[n + 1])
        for cp in starts:
            cp.wait_send()
        for cp in arrivals:
            cp.wait_recv()

    return list(pl.pallas_call(
        body, name=name, out_shape=tuple(pltpu.HBM(a.shape, a.dtype) for a in arrays),
        in_specs=[HBM_SPEC] * n + [SEM_SPEC, SEM_SPEC, ANY], out_specs=tuple([HBM_SPEC] * n),
        input_output_aliases={i: i for i in range(n)},
        compiler_params=pltpu.CompilerParams(has_side_effects=SPLIT_EFFECT),
    )(*arrays, send, recv, after))


def _row_half(ref, dim, who):
    rh = ref.shape[dim] // 2
    return pl.ds(who * rh, rh)


def gather_build(refs, send, recv):
    x, y, c, chips = _place()
    q = 2 * x + y
    starts, arrivals = [], []
    for buf in refs:
        rows = _row_half(buf, 2, c)
        for j, (px, py) in enumerate(chips):
            mine, got = buf.at[q, :, rows], buf.at[2 * px + py, :, rows]
            starts.append(pltpu.make_async_remote_copy(src_ref=mine, dst_ref=mine, send_sem=send.at[j], recv_sem=recv.at[j],
                                                       device_id=(px, py, c), device_id_type=MESH_ID))
            arrivals.append(pltpu.make_async_remote_copy(src_ref=got, dst_ref=got, send_sem=send.at[j], recv_sem=recv.at[j],
                                                         device_id=(px, py, c), device_id_type=MESH_ID))
    return starts, arrivals


def forward_build(refs, send, recv):
    x, y, c, chips = _place()
    sib = (x, y, 1 - c)
    starts, arrivals = [], []
    for buf in refs:
        for j, (px, py) in enumerate(chips):
            got = buf.at[2 * px + py, :, _row_half(buf, 2, c)]
            theirs = buf.at[2 * px + py, :, _row_half(buf, 2, 1 - c)]
            starts.append(pltpu.make_async_remote_copy(src_ref=got, dst_ref=got, send_sem=send.at[j], recv_sem=recv.at[j],
                                                       device_id=sib, device_id_type=MESH_ID))
            arrivals.append(pltpu.make_async_remote_copy(src_ref=theirs, dst_ref=theirs, send_sem=send.at[j], recv_sem=recv.at[j],
                                                         device_id=sib, device_id_type=MESH_ID))
    return starts, arrivals


def scatter_build(refs, send, recv):
    n = len(refs) // 2
    x, y, c, _ = _place()
    starts, arrivals = [], []
    for a in range(n):
        src, dst = refs[a], refs[n + a]
        for k in range(1, N_DEV):
            px, py, pc = x ^ (k >> 2), y ^ ((k >> 1) & 1), c ^ (k & 1)
            theirs = src.at[2 * px + py, :, _row_half(src, 2, pc)]
            starts.append(pltpu.make_async_remote_copy(src_ref=theirs, dst_ref=dst.at[k - 1], send_sem=send.at[k - 1],
                                                       recv_sem=recv.at[k - 1], device_id=(px, py, pc), device_id_type=MESH_ID))
            arrivals.append(pltpu.make_async_remote_copy(src_ref=dst.at[k - 1], dst_ref=dst.at[k - 1], send_sem=send.at[k - 1],
                                                         recv_sem=recv.at[k - 1], device_id=(px, py, pc), device_id_type=MESH_ID))
    return starts, arrivals


def share_build(refs, send, recv):
    dst = refs[0]
    x, y, c, _ = _place()
    mine = dst.at[4 * x + 2 * y + c]
    starts, arrivals = [], []
    for k in range(1, N_DEV):
        px, py, pc = x ^ (k >> 2), y ^ ((k >> 1) & 1), c ^ (k & 1)
        got = dst.at[4 * px + 2 * py + pc]
        starts.append(pltpu.make_async_remote_copy(src_ref=mine, dst_ref=mine, send_sem=send.at[k - 1], recv_sem=recv.at[k - 1],
                                                   device_id=(px, py, pc), device_id_type=MESH_ID))
        arrivals.append(pltpu.make_async_remote_copy(src_ref=got, dst_ref=got, send_sem=send.at[k - 1], recv_sem=recv.at[k - 1],
                                                     device_id=(px, py, pc), device_id_type=MESH_ID))
    return starts, arrivals


def join_build(refs, send, recv, layers):
    x, y, c, _ = _place()
    sib = (x, y, 1 - c)
    starts, arrivals = [], []
    for buf, l in zip(refs, layers):
        mine, theirs = buf.at[l, _row_half(buf, 1, c)], buf.at[l, _row_half(buf, 1, 1 - c)]
        starts.append(pltpu.make_async_remote_copy(src_ref=mine, dst_ref=mine, send_sem=send.at[0], recv_sem=recv.at[0],
                                                   device_id=sib, device_id_type=MESH_ID))
        arrivals.append(pltpu.make_async_remote_copy(src_ref=theirs, dst_ref=theirs, send_sem=send.at[0], recv_sem=recv.at[0],
                                                     device_id=sib, device_id_type=MESH_ID))
    return starts, arrivals


def join_row_halves(name, bufs):
    n = len(bufs)

    def body(*refs):
        dst = refs[n:2 * n]
        send, recv = refs[2 * n:]
        x, y, c, _ = _place()
        cps = []
        for a in range(n):
            mine = dst[a].at[:, _row_half(dst[a], 1, c)]
            cps.append(pltpu.make_async_remote_copy(src_ref=mine, dst_ref=mine, send_sem=send.at[a], recv_sem=recv.at[a],
                                                    device_id=(x, y, 1 - c), device_id_type=MESH_ID))
        for cp in cps:
            cp.start()
        for a in range(n):
            theirs = dst[a].at[:, _row_half(dst[a], 1, 1 - c)]
            pltpu.make_async_remote_copy(src_ref=theirs, dst_ref=theirs, send_sem=send.at[a], recv_sem=recv.at[a],
                                         device_id=(x, y, 1 - c), device_id_type=MESH_ID).wait_recv()
        for cp in cps:
            cp.wait_send()

    outs = [jax.ShapeDtypeStruct(b.shape, b.dtype) for b in bufs]
    sems = [pltpu.SemaphoreType.DMA((n,))] * 2
    return _comm_call(name, body, bufs, outs, sems, aliases={a: a for a in range(n)})


def share_with_all(name, packed, me):
    slots = lax.dynamic_update_slice(jnp.zeros((N_DEV,) + packed.shape, packed.dtype), packed[None], (me, 0, 0))

    def body(_, dst, send, recv):
        x, y, c, _ = _place()
        me = 4 * x + 2 * y + c
        cps = []
        for k in range(1, N_DEV):
            px, py, pc = x ^ (k >> 2), y ^ ((k >> 1) & 1), c ^ (k & 1)
            cps.append(pltpu.make_async_remote_copy(src_ref=dst.at[me], dst_ref=dst.at[me], send_sem=send.at[k - 1],
                                                    recv_sem=recv.at[k - 1], device_id=(px, py, pc), device_id_type=MESH_ID))
        for cp in cps:
            cp.start()
        for k in range(1, N_DEV):
            px, py, pc = x ^ (k >> 2), y ^ ((k >> 1) & 1), c ^ (k & 1)
            got = dst.at[4 * px + 2 * py + pc]
            pltpu.make_async_remote_copy(src_ref=got, dst_ref=got, send_sem=send.at[k - 1], recv_sem=recv.at[k - 1],
                                         device_id=(px, py, pc), device_id_type=MESH_ID).wait_recv()
        for cp in cps:
            cp.wait_send()

    outs = [jax.ShapeDtypeStruct(slots.shape, slots.dtype)]
    sems = [pltpu.SemaphoreType.DMA((N_DEV - 1,))] * 2
    return _comm_call(name, body, [slots], outs, sems, aliases={0: 0})[0]


def _w_tiles(R, C):
    return _tile(R, max(16, (1 << 20) // (4 * C) // 16 * 16), 16)


def cast_bf16(w, l, q_arr):
    _, R, C = w.shape
    tr = _w_tiles(R, C)
    ins = [(w, (None, tr, C), lambda r, q: (l, r, 0))]
    outs = [((N_CHIPS, 1, R, C), BF16, (None, None, tr, C), lambda r, q: (q[0], 0, r, 0), None)]
    return bmap_fwd("cast_bf16", lambda a: (a,), (R // tr,), ins, outs, scalars=(q_arr,))[0]


def sum_partials(own, landed, into, l, q_arr, c_arr):
    n_land, _, rh, C = landed.shape
    tr = _w_tiles(rh, C)
    nb = rh // tr
    blk = (None, None, tr, C)
    ins = [(own, blk, lambda r, q, c: (q[0], 0, c[0] * nb + r, 0))]
    ins += [(landed, blk, (lambda r, q, c, kk=kk: (kk, 0, r, 0))) for kk in range(n_land)]
    outs = [(into.shape, F32, (None, tr, C), lambda r, q, c: (l, c[0] * nb + r, 0), None)]
    return bmap_fwd("sum_partials", lambda *t: (functools.reduce(lambda u, v: u + v, t),), (nb,), ins, outs,
                    scalars=(q_arr, c_arr), into=into)[0]


def sum_devices(slots):
    nd, NR, C = slots.shape
    tr = _tile(NR, 512, 8)
    ins = [(slots, (None, tr, C), (lambda r, dd=dd: (dd, r, 0))) for dd in range(nd)]
    outs = [((NR, C), F32, (tr, C), lambda r: (r, 0), None)]
    return bmap_fwd("sum_devices", lambda *a: (functools.reduce(lambda u, v: u + v, a),), (NR // tr,), ins, outs)[0]


def adamw(name, w, g, m, v, with_grad=False):
    if w.ndim == 2:
        R, C = w.shape
        tr = _w_tiles(R, C)
        spec = ((tr, C), lambda r: (r, 0))
        grid = (R // tr,)
    else:
        L, R, C = w.shape
        tr = _w_tiles(R, C)
        spec = ((None, tr, C), lambda l, r: (l, r, 0))
        grid = (L, R // tr)
    ins = [(a,) + spec for a in (w, g, m, v)]
    outs = [(w.shape, F32) + spec + (None,)] * (4 if with_grad else 3)
    fn = (lambda a, b, c, d: f_adam(a, b, c, d) + (b,)) if with_grad else f_adam
    return bmap_fwd(name, fn, grid, ins, outs)


def loss_and_grad(h, target):
    T, D = h.shape
    tm = _tile(T, 256, 8)

    def fn(hv, tv):
        d = hv - tv
        return jnp.sum(d * d, keepdims=True).reshape(1, 1) * (0.5 / D), d * (1.0 / D)

    ins = [(h, (tm, D), lambda i: (i, 0)), (target, (tm, D), lambda i: (i, 0))]
    outs = [((1, 1), F32, (1, 1), lambda i: (0, 0), 0), ((T, D), F32, (tm, D), lambda i: (i, 0), None)]
    return bmap_fwd("loss_and_grad", fn, (T // tm,), ins, outs)


BIG = ("hg_w_in", "hg_w_out", "gm_w_in", "gm_w_out", "ffn_w_gate", "ffn_w_up", "ffn_w_down", "ple_w_proj", "ple_w_gate")
KIND = {"hg_w_in": "col", "hg_w_out": "row", "gm_w_in": "col", "gm_w_out": "row", "ffn_w_gate": "col",
        "ffn_w_up": "col", "ffn_w_down": "row", "ple_w_proj": "col", "ple_w_gate": "row"}
SMALL = ("hg_lb_logits", "hg_out_norm", "gm_ln_g", "gm_ln_b", "gm_w_s", "gm_b_s", "norm_mix_pre", "norm_mix_post",
         "norm_ffn_pre", "norm_ffn_post", "ple_norm")
SMALL_GM = ("gm_ln_g", "gm_ln_b", "gm_w_s", "gm_b_s")
SMALL_REST = tuple(k for k in SMALL if k not in SMALL_GM)
WEIGHTS = ("hg_w_in", "hg_lb_logits", "hg_out_norm", "hg_w_out", "gm_w_in", "gm_ln_g", "gm_ln_b", "gm_w_s", "gm_b_s",
           "gm_w_out", "norm_mix_pre", "norm_mix_post", "norm_ffn_pre", "norm_ffn_post", "ffn_w_gate", "ffn_w_up",
           "ffn_w_down", "ple_w_proj", "ple_w_gate", "ple_norm")


def _pack(arrs):
    rows = []
    for a in arrs:
        flat = a.reshape(-1)
        pad = (-flat.shape[0]) % (8 * LANES)
        rows.append(jnp.pad(flat, (0, pad)).reshape(-1, LANES))
    n_rows = sum(r.shape[0] for r in rows)
    rows.append(jnp.zeros(((-n_rows) % PACK_ROWS, LANES), F32))
    return jnp.concatenate(rows, axis=0)


def _unpack(packed, shapes):
    out, r = [], 0
    for s in shapes:
        size = 1
        for d in s:
            size *= d
        nr = -(-size // (8 * LANES)) * 8
        out.append(packed[r:r + nr].reshape(-1)[:size].reshape(s))
        r += nr
    return out


def _step(x, p, W, M, V, loss_target):
    n_seq, S, D = x.shape
    T = n_seq * S
    depth = p.shape[0]
    n_hg = W["hg_w_in"].shape[0]
    x2 = x.reshape(T, D)
    p3 = p.reshape(depth, T, p.shape[-1])
    tgt = loss_target.reshape(T, D)
    xi, yi, ci = lax.axis_index("x"), lax.axis_index("y"), lax.axis_index("c")
    q_me = 2 * xi + yi
    c_arr = jnp.reshape(ci, (1,)).astype(jnp.int32)
    q_arr = jnp.reshape(q_me, (1,)).astype(jnp.int32)

    groups = {}
    for i in range(depth):
        mix = ("hg_w_in", "hg_w_out") if i % 2 == 0 else ("gm_w_in", "gm_w_out")
        groups[i, "mix"] = [(k, i // 2) for k in mix]
        groups[i, "rest"] = [(k, i) for k in ("ffn_w_gate", "ffn_w_up", "ffn_w_down", "ple_w_proj", "ple_w_gate")]
    G = {k: {} for k in BIG}
    DW = {k: {l: lax.empty((N_CHIPS, 1) + W[k].shape[1:], BF16) for l in range(W[k].shape[0])} for k in BIG}
    in_flight = {}

    casts = {}

    def cast_group(i, part, dep):
        qa = q_arr if dep is None else lax.optimization_barrier((q_arr, dep))[0]
        casts[i, part] = [cast_bf16(W[k], l, qa) for k, l in groups[i, part]]

    def start_gather(i, part, dep):
        bufs = casts.pop((i, part))
        if dep is not None:
            bufs = list(lax.optimization_barrier((tuple(bufs), dep))[0])
        send, recv, arrs, tok = split_start("gather_start_%d_%s" % (i, part), gather_build, bufs)
        in_flight[i, part] = (send, recv, arrs)
        return tok

    forwarding = {}

    def arrive_gather(i, part, after):
        send, recv, arrs = in_flight.pop((i, part))
        arrs = split_wait("gather_wait_%d_%s" % (i, part), gather_build, send, recv, arrs, after)
        send, recv, arrs, tok = split_start("gather_pass_%d_%s" % (i, part), forward_build, arrs)
        forwarding[i, part] = (send, recv, arrs)
        return tok

    def finish_gather(i, part, after):
        send, recv, arrs = forwarding.pop((i, part))
        arrs = split_wait("gather_done_%d_%s" % (i, part), forward_build, send, recv, arrs, after)
        for (k, l), buf in zip(groups[i, part], arrs):
            G[k][l] = buf
        return buf

    def after_token(row_arr, *toks):
        return functools.reduce(lambda u, t: u + t[0:1, 0:1], toks, row_arr)

    cast_group(0, "mix", None)
    tok_mix = start_gather(0, "mix", None)
    cast_group(0, "rest", tok_mix)
    tok_rest = start_gather(0, "rest", None)
    finish_gather(0, "mix", arrive_gather(0, "mix", tok_mix + tok_rest))
    tie = lambda v, tok: lax.optimization_barrier((v, tok))[0]
    me = 4 * xi + 2 * yi + ci
    ln_full = share_with_all("share_ln", _pack([W["gm_ln_g"], W["gm_ln_b"]]), me)
    n_gm, dq = W["gm_ln_g"].shape
    ln_parts = [_unpack(ln_full[4 * qx + 2 * qy + 0], [(n_gm, dq), (n_gm, dq)]) for qx in range(2) for qy in range(2)]
    ln_g = jnp.concatenate([lp[0] for lp in ln_parts], axis=1)
    ln_b = jnp.concatenate([lp[1] for lp in ln_parts], axis=1)

    row = lambda a, i: a[i][None, :]
    f_lb = _make_f_lb(n_hg)
    lb_rows = [row(W["hg_lb_logits"], j) for j in range(n_hg)]
    one = (1, D)
    lb_ins = [(r, one, lambda i: (0, 0)) for r in lb_rows]
    lb_out = bmap_fwd("hg_lower_bounds", f_lb, (1,), lb_ins, [(one, F32, one, lambda i: (0, 0), None)] * (3 * n_hg))

    saved = []
    h = x2
    a = rows_fwd("prenorm", f_prenorm, [h], [row(W["norm_mix_pre"], 0)], [BF16])[0]
    for i in range(depth):
        j = i // 2
        sv = {"h": h, "a": a}
        if i > 0:
            finish_gather(i, "mix", h)
            a = tie(a, arrive_gather(i, "rest", h))
        if i % 2 == 0:
            proj4 = mm_fwd("hg_in", a, G["hg_w_in"], j, "col", parts=True)
            lbp = lb_out[3 * j:3 * j + 3]
            onorm = row(W["hg_out_norm"], j)
            og, states = hgrn_fwd("hgrn_fwd", proj4, *lbp, onorm, n_seq)
            if i == 0:
                og = tie(og, arrive_gather(0, "rest", og))
            m = mm_fwd("hg_out", og, G["hg_w_out"], j, "row")
            sv.update(proj4=proj4, states=states, og=og, lbp=lbp, onorm=onorm)
        else:
            z = mm_fwd("gm_in", a, G["gm_w_in"], j, "col")
            lg, lb_ = row(ln_g, j), row(ln_b, j)
            u, vn = rows_fwd("gm_gelu_ln", f_gm_in, [z], [lg, lb_], [F32, BF16], tm=128)
            ws = W["gm_w_s"][j]
            bs = W["gm_b_s"][j][:, :, None]
            gb = min(GM_BLOCK, S)
            sp_grid = (D // LANES, T // gb)
            sp_ins = [(u, (gb, LANES), lambda g, n: (n, g)), (vn, (gb, LANES), lambda g, n: (n, g)),
                      (ws, (None, GM_CHUNK, GM_CHUNK), lambda g, n: (g, 0, 0)),
                      (bs, (None, GM_CHUNK, 1), lambda g, n: (g, 0, 0))]
            y = bmap_fwd("gm_spatial", f_gm_spatial, sp_grid, sp_ins,
                         [((T, D), BF16, (gb, LANES), lambda g, n: (n, g), None)])[0]
            m = mm_fwd("gm_out", y, G["gm_w_out"], j, "row")
            sv.update(z=z, lg=lg, lb_=lb_, sp_ins=sp_ins, sp_grid=sp_grid, y=y)
        g_post, g_fpre = row(W["norm_mix_post"], i), row(W["norm_ffn_pre"], i)
        arrived = finish_gather(i, "rest", m)
        if i + 1 < depth:
            cast_group(i + 1, "mix", arrived)
            tok_mix = start_gather(i + 1, "mix", None)
            cast_group(i + 1, "rest", tok_mix)
            g_post = after_token(g_post, tok_mix, start_gather(i + 1, "rest", None))
        h1, fin = rows_fwd("mix_post_ffn_pre", f_post_pre, [h, m], [g_post, g_fpre], [F32, BF16])
        gate, up, act = ffn_gate_up("ffn_gate_up", fin, G["ffn_w_gate"], G["ffn_w_up"], i)
        f = mm_fwd("ffn_down", act, G["ffn_w_down"], i, "row")
        g_fpost = row(W["norm_ffn_post"], i)
        if i + 1 < depth:
            g_fpost = after_token(g_fpost, arrive_gather(i + 1, "mix", f))
        h2 = rows_fwd("ffn_post", f_post, [h1, f], [g_fpost], [F32])[0]
        e = mm_fwd("ple_proj", p3, G["ple_w_proj"], i, "col", xl=i)
        zg = mm_fwd("ple_gate", h2, G["ple_w_gate"], i, "row")
        g_ple = row(W["ple_norm"], i)
        sv.update(m=m, h1=h1, fin=fin, gate=gate, up=up, act=act, f=f, h2=h2, e=e, zg=zg,
                  g_post=g_post, g_fpre=g_fpre, g_fpost=g_fpost, g_ple=g_ple)
        if i + 1 < depth:
            g_next = row(W["norm_mix_pre"], i + 1)
            h, a = rows_fwd("ple_next_pre", f_ple_pre, [h2, e, zg], [g_ple, g_next], [F32, BF16])
            sv["g_next"] = g_next
        else:
            h = rows_fwd("ple_last", f_ple, [h2, e, zg], [g_ple], [F32])[0]
        saved.append(sv)

    loss_part, dh = loss_and_grad(h, tgt)
    loss = lax.psum(loss_part[0, 0], ("x", "y", "c"))

    sg = {k: [None] * W[k].shape[0] for k in ("norm_mix_pre", "norm_mix_post", "norm_ffn_pre", "norm_ffn_post", "ple_norm",
                                              "hg_out_norm", "gm_ln_g", "gm_ln_b", "gm_w_s", "gm_b_s")}
    d_lbp = [None] * (3 * n_hg)
    da_next = None
    GRAD = {k: lax.empty(W[k].shape, F32) for k in BIG}
    scattering = {}

    def start_scatter(i, part):
        dws = [DW[k][l] for k, l in groups[i, part]]
        lands = [lax.empty((N_DEV - 1, 1, g.shape[2] // 2, g.shape[3]), BF16) for g in dws]
        send, recv, arrs, tok = split_start("scatter_start_%d_%s" % (i, part), scatter_build, dws + lands, n_peers=N_DEV - 1)
        scattering[i, part] = (send, recv, arrs)
        return tok

    def finish_scatter(i, part, after):
        send, recv, arrs = scattering.pop((i, part))
        arrs = split_wait("scatter_wait_%d_%s" % (i, part), scatter_build, send, recv, arrs, after)
        n = len(groups[i, part])
        for (k, l), own, ld in zip(groups[i, part], arrs[:n], arrs[n:]):
            GRAD[k] = sum_partials(own, ld, GRAD[k], l, q_arr, c_arr)
        kinds = [k for k, _ in groups[i, part]]
        build = functools.partial(join_build, layers=tuple(l for _, l in groups[i, part]))
        send, recv, arrs, tok_j = split_start("join_start_%d_%s" % (i, part), build, [GRAD[k] for k in kinds], n_peers=1)
        GRAD.update(zip(kinds, arrs))
        joining.append(("join_wait_%d_%s" % (i, part), build, send, recv, kinds))
        return tok_j

    def finish_joins(after):
        while joining:
            name, build, send, recv, kinds = joining.pop(0)
            GRAD.update(zip(kinds, split_wait(name, build, send, recv, [GRAD[k] for k in kinds], after)))

    def start_share(name, keys, extra):
        grads = dict(extra)
        for k in keys:
            if k not in grads:
                grads[k] = jnp.stack([v.reshape((D,) if k in ("gm_ln_g", "gm_ln_b") else W[k].shape[1:]) for v in sg[k]])
        packed = _pack([grads[k] for k in keys])
        slots = lax.dynamic_update_slice(jnp.zeros((N_DEV,) + packed.shape, F32), packed[None], (me, 0, 0))
        s_send, s_recv, slots, tok_sh = split_start(name + "_start", share_build, [slots], n_peers=N_DEV - 1)
        return name, [grads[k].shape for k in keys], s_send, s_recv, slots, tok_sh

    tail = {}

    def share_rest(tok_before, da0, dh0):
        g0 = after_token(row(W["norm_mix_pre"], 0), tok_before)
        tail["grad_x"], d_g0 = rows_bwd("prenorm_bwd", f_prenorm_thru, [saved[0]["h"]], [g0], [da0, dh0], [F32])
        sg["norm_mix_pre"][0] = d_g0
        d_logits = bmap_bwd("hg_lower_bounds_bwd", f_lb, (1,), lb_ins, [(d, one, lambda i: (0, 0)) for d in d_lbp],
                            [(jj, F32, None) for jj in range(n_hg)])
        sharing["rest"] = start_share("share_small", SMALL_REST, {"hg_lb_logits": jnp.concatenate(d_logits, axis=0)})
        tail["tok_s"] = sharing["rest"][-1]

    joining = []
    sharing = {}
    last_gm = 1
    tok = None
    for i in reversed(range(depth)):
        j = i // 2
        sv = saved[i]
        if i + 1 < depth:
            g_ple_after = sv["g_ple"] + tok[0:1, 0:1]
            dh2, de, dzg, d_gple, d_gnext = rows_bwd("ple_next_pre_bwd", f_ple_pre, [sv["h2"], sv["e"], sv["zg"]],
                                                     [g_ple_after, sv["g_next"]], [dh, da_next], [F32, BF16, BF16])
            sg["norm_mix_pre"][i + 1] = d_gnext
        else:
            dh2, de, dzg, d_gple = rows_bwd("ple_last_bwd", f_ple, [sv["h2"], sv["e"], sv["zg"]], [sv["g_ple"]], [dh],
                                            [F32, BF16, BF16])
        sg["ple_norm"][i] = d_gple
        DW["ple_w_proj"] = mm_bwd_w("ple_proj_dw", p3, de, DW["ple_w_proj"], i, "col", xl=i)
        DW["ple_w_gate"] = mm_bwd_w("ple_gate_dw", sv["h2"], dzg, DW["ple_w_gate"], i, "row")
        dh2 = mm_bwd_x("ple_gate_dx", dzg, G["ple_w_gate"], i, "row", addend=dh2)
        dh1, df, d_gfpost = rows_bwd("ffn_post_bwd", f_post, [sv["h1"], sv["f"]], [sv["g_fpost"]], [dh2], [F32, BF16])
        sg["norm_ffn_post"][i] = d_gfpost
        dgate, dup = ffn_down_dx("ffn_down_dx", df, G["ffn_w_down"], i, sv["gate"], sv["up"])
        DW["ffn_w_down"] = mm_bwd_w("ffn_down_dw", sv["act"], df, DW["ffn_w_down"], i, "row")
        dfin = ffn_in_dx("ffn_in_dx", dgate, dup, G["ffn_w_gate"], G["ffn_w_up"], i)
        DW["ffn_w_gate"] = mm_bwd_w("ffn_gate_dw", sv["fin"], dgate, DW["ffn_w_gate"], i, "col")
        DW["ffn_w_up"] = mm_bwd_w("ffn_up_dw", sv["fin"], dup, DW["ffn_w_up"], i, "col")
        g_post_after = after_token(sv["g_post"], start_scatter(i, "rest"))
        dh, dm, d_gpost, d_gfpre = rows_bwd("mix_post_ffn_pre_bwd", f_post_pre, [sv["h"], sv["m"]],
                                            [g_post_after, sv["g_fpre"]], [dh1, dfin], [F32, BF16])
        sg["norm_mix_post"][i], sg["norm_ffn_pre"][i] = d_gpost, d_gfpre
        if i % 2 == 0:
            dog = mm_bwd_x("hg_out_dx", dm, G["hg_w_out"], j, "row")
            DW["hg_w_out"] = mm_bwd_w("hg_out_dw", sv["og"], dm, DW["hg_w_out"], j, "row")
            dproj4, d0, d1, d2, d_on = hgrn_bwd("hgrn_bwd", sv["proj4"], sv["states"], dog, *sv["lbp"], sv["onorm"], n_seq)
            d_lbp[3 * j:3 * j + 3] = [d0, d1, d2]
            sg["hg_out_norm"][j] = d_on
            DW["hg_w_in"] = mm_bwd_w("hg_in_dw", sv["a"], dproj4, DW["hg_w_in"], j, "col", parts=True)
            if i == 0:
                tok_last = start_scatter(0, "mix")
                dproj4 = tie(dproj4, tok_last)
            da_next = mm_bwd_x("hg_in_dx", dproj4, G["hg_w_in"], j, "col", parts=True)
            if i == 0:
                share_rest(tok, da_next, dh)
        else:
            dy = mm_bwd_x("gm_out_dx", dm, G["gm_w_out"], j, "row")
            DW["gm_w_out"] = mm_bwd_w("gm_out_dw", sv["y"], dm, DW["gm_w_out"], j, "row")
            gb = sv["sp_ins"][0][1][0]
            du, dvn, dws, dbs = bmap_bwd("gm_spatial_bwd", f_gm_spatial, sv["sp_grid"], sv["sp_ins"],
                                         [(dy, (gb, LANES), lambda g, n: (n, g))],
                                         [(0, F32, None), (1, F32, None), (2, F32, 1), (3, F32, 1)])
            sg["gm_w_s"][j], sg["gm_b_s"][j] = dws, dbs[:, :, 0]
            dz, d_lg, d_lb = rows_bwd("gm_gelu_ln_bwd", f_gm_in, [sv["z"]], [sv["lg"], sv["lb_"]], [du, dvn], [BF16], tm=128)
            sg["gm_ln_g"][j], sg["gm_ln_b"][j] = d_lg, d_lb
            da_next = mm_bwd_x("gm_in_dx", dz, G["gm_w_in"], j, "col")
            DW["gm_w_in"] = mm_bwd_w("gm_in_dw", sv["a"], dz, DW["gm_w_in"], j, "col")
        tok = start_scatter(i, "mix") if (i, "mix") not in scattering else tok_last
        if i + 1 < depth:
            tok = tok + finish_scatter(i + 1, "rest", da_next) + finish_scatter(i + 1, "mix", da_next)
        if i == last_gm:
            sharing["gm"] = start_share("share_gm", SMALL_GM, {})
            tok = tok + sharing["gm"][-1]
    tok = tok + finish_scatter(0, "rest", tok)
    if not tail:
        share_rest(tok, da_next, dh)
    grad_x, tok_s = tail["grad_x"], tail["tok_s"]

    out_g, out_d, out_m, out_v = {}, {}, {}, {}
    late = [k for k, _ in groups[0, "mix"]]
    early = [k for k in BIG if k not in late]
    finish_joins(tok_s)
    for k in early:
        out_d[k], out_m[k], out_v[k], out_g[k] = adamw("adamw_" + k, W[k], GRAD[k], M[k], V[k], with_grad=True)
    small_red = {}
    for part, keys in (("gm", SMALL_GM), ("rest", SMALL_REST)):
        name, shapes, s_send, s_recv, slots, _ = sharing[part]
        slots = split_wait(name + "_wait", share_build, s_send, s_recv, slots, out_v[early[-1]])[0]
        small_red.update(zip(keys, _unpack(sum_devices(slots), shapes)))
    for k in ("gm_ln_g", "gm_ln_b"):
        small_red[k] = lax.dynamic_slice_in_dim(small_red[k], q_me * dq, dq, axis=1)
    pk = lambda d: _pack([d[k] for k in SMALL])
    s_delta, s_m, s_v = adamw("adamw_small", pk(W), pk(small_red), pk(M), pk(V))
    shard_shapes = [W[k].shape for k in SMALL]
    out_g.update(small_red)
    for dct, packed in ((out_d, s_delta), (out_m, s_m), (out_v, s_v)):
        dct.update(zip(SMALL, _unpack(packed, shard_shapes)))

    finish_joins(finish_scatter(0, "mix", s_v))
    for k in late:
        out_d[k], out_m[k], out_v[k], out_g[k] = adamw("adamw_" + k, W[k], GRAD[k], M[k], V[k], with_grad=True)

    outs = [loss, grad_x.reshape(x.shape)]
    for dct in (out_g, out_d, out_m, out_v):
        outs += [dct[k] for k in WEIGHTS]
    return tuple(outs)


def kernel(x, p, hg_w_in, hg_lb_logits, hg_out_norm, hg_w_out, gm_w_in, gm_ln_g, gm_ln_b, gm_w_s, gm_b_s, gm_w_out, norm_mix_pre, norm_mix_post, norm_ffn_pre, norm_ffn_post, ffn_w_gate, ffn_w_up, ffn_w_down, ple_w_proj, ple_w_gate, ple_norm, loss_target, m_hg_w_in, m_hg_lb_logits, m_hg_out_norm, m_hg_w_out, m_gm_w_in, m_gm_ln_g, m_gm_ln_b, m_gm_w_s, m_gm_b_s, m_gm_w_out, m_norm_mix_pre, m_norm_mix_post, m_norm_ffn_pre, m_norm_ffn_post, m_ffn_w_gate, m_ffn_w_up, m_ffn_w_down, m_ple_w_proj, m_ple_w_gate, m_ple_norm, v_hg_w_in, v_hg_lb_logits, v_hg_out_norm, v_hg_w_out, v_gm_w_in, v_gm_ln_g, v_gm_ln_b, v_gm_w_s, v_gm_b_s, v_gm_w_out, v_norm_mix_pre, v_norm_mix_post, v_norm_ffn_pre, v_norm_ffn_post, v_ffn_w_gate, v_ffn_w_up, v_ffn_w_down, v_ple_w_proj, v_ple_w_gate, v_ple_norm):
    W = dict(zip(WEIGHTS, (hg_w_in, hg_lb_logits, hg_out_norm, hg_w_out, gm_w_in, gm_ln_g, gm_ln_b, gm_w_s, gm_b_s, gm_w_out,
                           norm_mix_pre, norm_mix_post, norm_ffn_pre, norm_ffn_post, ffn_w_gate, ffn_w_up, ffn_w_down,
                           ple_w_proj, ple_w_gate, ple_norm)))
    M = dict(zip(WEIGHTS, (m_hg_w_in, m_hg_lb_logits, m_hg_out_norm, m_hg_w_out, m_gm_w_in, m_gm_ln_g, m_gm_ln_b, m_gm_w_s,
                           m_gm_b_s, m_gm_w_out, m_norm_mix_pre, m_norm_mix_post, m_norm_ffn_pre, m_norm_ffn_post,
                           m_ffn_w_gate, m_ffn_w_up, m_ffn_w_down, m_ple_w_proj, m_ple_w_gate, m_ple_norm)))
    V = dict(zip(WEIGHTS, (v_hg_w_in, v_hg_lb_logits, v_hg_out_norm, v_hg_w_out, v_gm_w_in, v_gm_ln_g, v_gm_ln_b, v_gm_w_s,
                           v_gm_b_s, v_gm_w_out, v_norm_mix_pre, v_norm_mix_post, v_norm_ffn_pre, v_norm_ffn_post,
                           v_ffn_w_gate, v_ffn_w_up, v_ffn_w_down, v_ple_w_proj, v_ple_w_gate, v_ple_norm)))
    return _step(x, p, W, M, V, loss_target)
```
